```python
import math
import jax, jax.numpy as jnp
from jax import lax
import numpy as np

D_MODEL = 1024
BATCH = 16
SEQ = 4096
DEPTH = 4

N_META = 16
CHUNK = 128
EPS = 1e-6
NEG = -1e30

SSD_D_INNER = D_MODEL
SSD_HEAD_DIM = 64
SSD_HEADS = SSD_D_INNER // SSD_HEAD_DIM
SSD_GROUPS = 2
SSD_STATE = 128
SSD_CONV = 4
SSD_CONV_DIM = SSD_D_INNER + 2 * SSD_GROUPS * SSD_STATE

FOX_HEADS = 8
FOX_HEAD_DIM = 128
FOX_WIDTH = FOX_HEADS * FOX_HEAD_DIM

S5_WIDTH = D_MODEL
S5_GROUP = 16
S5_GROUPS = S5_WIDTH // S5_GROUP
S5_STATE = 64

N_BRANCH = 3
D_FF = ((8 * D_MODEL + 3 * 256 - 1) // (3 * 256)) * 256

IN_SPLITS = (SSD_D_INNER, SSD_CONV_DIM, SSD_HEADS, 3 * FOX_WIDTH, FOX_HEADS, S5_WIDTH, N_BRANCH * D_MODEL)
D_IN = SSD_D_INNER + SSD_CONV_DIM + SSD_HEADS + 3 * FOX_WIDTH + FOX_HEADS + S5_WIDTH + N_BRANCH * D_MODEL

kernel_name = 'hybrid_ssd_fox_s5_gated_block'


def rmsnorm(x, w):
    xf = x.astype(jnp.float32)
    y = xf * lax.rsqrt(jnp.mean(xf * xf, axis=-1, keepdims=True) + EPS)
    return (y * w.astype(jnp.float32)).astype(x.dtype)


def causal_dwconv(x, w, b):
    c = x.shape[-1]
    y = lax.conv_general_dilated(x, w[:, None, :].astype(x.dtype), window_strides=(1,),
                                 padding=[(w.shape[0] - 1, 0)],
                                 dimension_numbers=('NWC', 'WIO', 'NWC'),
                                 feature_group_count=c)
    return y + b.astype(x.dtype)


def segsum(a):
    t = a.shape[-1]
    ar = jnp.broadcast_to(a[..., :, None], a.shape + (t,))
    cs = jnp.cumsum(jnp.where(jnp.tril(jnp.ones((t, t), bool), -1), ar, 0.0), axis=-2)
    return jnp.where(jnp.tril(jnp.ones((t, t), bool)), cs, -jnp.inf)


def pad_front(t, pad, axis=1):
    widths = [(0, 0)] * t.ndim
    widths[axis] = (pad, 0)
    return jnp.pad(t, widths)


def ssd_mixer(z, xbc, dt_raw, conv_w, conv_b, dt_bias, a_log, d_skip, norm_w):
    f32 = jnp.float32
    b, L, _ = xbc.shape
    lp = ((L + CHUNK - 1) // CHUNK) * CHUNK
    pad = lp - L
    nc = lp // CHUNK
    G, R, P, N = SSD_GROUPS, SSD_HEADS // SSD_GROUPS, SSD_HEAD_DIM, SSD_STATE
    xbc = jax.nn.silu(causal_dwconv(xbc, conv_w, conv_b))
    xs, bm, cm = jnp.split(xbc, [SSD_D_INNER, SSD_D_INNER + G * N], axis=-1)
    xs = xs.astype(f32)
    dt = jax.nn.softplus(dt_raw.astype(f32) + dt_bias.astype(f32))
    a = -jnp.exp(a_log.astype(f32))
    xdt = (xs.reshape(b, L, G, R, P) * dt.reshape(b, L, G, R)[..., None])
    adt = (dt * a).reshape(b, L, G, R)
    xc = pad_front(xdt, pad).reshape(b, nc, CHUNK, G, R, P)
    bc = pad_front(bm.astype(f32).reshape(b, L, G, N), pad).reshape(b, nc, CHUNK, G, N)
    cc = pad_front(cm.astype(f32).reshape(b, L, G, N), pad).reshape(b, nc, CHUNK, G, N)
    ac = pad_front(adt, pad).reshape(b, nc, CHUNK, G, R).transpose(0, 3, 4, 1, 2)
    a_cs = jnp.cumsum(ac, axis=-1)
    lmat = jnp.exp(segsum(ac))
    y_diag = jnp.einsum('bclgn,bcsgn,bgrcls,bcsgrp->bclgrp', cc, bc, lmat, xc)
    decay_states = jnp.exp(a_cs[..., -1:] - a_cs)
    states = jnp.einsum('bclgn,bgrcl,bclgrp->bcgrpn', bc, decay_states, xc)
    states = jnp.concatenate([jnp.zeros_like(states[:, :1]), states], axis=1)
    chunk_decay = jnp.exp(segsum(pad_front(a_cs[..., -1], 1, axis=3)))
    states = jnp.einsum('bgrzc,bcgrpn->bzgrpn', chunk_decay, states)[:, :-1]
    y_off = jnp.einsum('bclgn,bcgrpn,bgrcl->bclgrp', cc, states, jnp.exp(a_cs))
    y = (y_diag + y_off).reshape(b, lp, SSD_D_INNER)[:, pad:]
    y = y + xs * jnp.repeat(d_skip.astype(f32), SSD_HEAD_DIM)
    y = y * jax.nn.silu(z.astype(f32))
    return rmsnorm(y, norm_w).astype(z.dtype)


def fox_mixer(qkv, f_raw, b_f):
    f32 = jnp.float32
    b, L, _ = qkv.shape
    lp = ((L + CHUNK - 1) // CHUNK) * CHUNK
    pad = lp - L
    nblk = lp // CHUNK
    q, k, v = jnp.split(qkv, 3, axis=-1)
    q, k, v = [pad_front(t.reshape(b, L, FOX_HEADS, FOX_HEAD_DIM), pad) for t in (q, k, v)]
    logf = jax.nn.log_sigmoid(f_raw.astype(f32) + b_f.astype(f32))
    cum = pad_front(jnp.cumsum(logf, axis=1).transpose(0, 2, 1), pad, axis=2)
    key_pos = jnp.arange(lp)
    key_valid = key_pos >= pad
    scale = FOX_HEAD_DIM ** -0.5

    def block(i):
        start = i * CHUNK
        qb = lax.dynamic_slice_in_dim(q, start, CHUNK, axis=1)
        cq = lax.dynamic_slice_in_dim(cum, start, CHUNK, axis=2)
        s = jnp.einsum('bqhd,bkhd->bhqk', qb, k).astype(f32) * scale
        s = s + cq[..., :, None] - cum[:, :, None, :]
        qpos = start + jnp.arange(CHUNK)
        mask = (key_pos[None, :] <= qpos[:, None]) & key_valid[None, :]
        p = jax.nn.softmax(jnp.where(mask, s, NEG), axis=-1)
        return jnp.einsum('bhqk,bkhd->bqhd', p.astype(v.dtype), v)

    out = lax.map(block, jnp.arange(nblk))
    return out.transpose(1, 0, 2, 3, 4).reshape(b, lp, FOX_WIDTH)[:, pad:]


def s5_mixer(u, lam_re, lam_im, b_re, b_im, c_re, c_im, log_step, d_skip, w_glu):
    f32 = jnp.float32
    b, L, _ = u.shape
    uf = u.astype(f32)
    lam = lax.complex(lam_re.astype(f32), lam_im.astype(f32))
    step = jnp.exp(log_step.astype(f32))[:, None]
    lam_bar = jnp.exp(lam * step)
    b_bar = ((lam_bar - 1.0) / lam)[..., None] * lax.complex(b_re.astype(f32), b_im.astype(f32))
    cmat = lax.complex(c_re.astype(f32), c_im.astype(f32))
    bu = jnp.einsum('gpc,blgc->blgp', b_bar, uf.reshape(b, L, S5_GROUPS, S5_GROUP))
    a_seq = jnp.broadcast_to(lam_bar, (1, L) + lam_bar.shape)

    def combine(e1, e2):
        a1, h1 = e1
        a2, h2 = e2
        return a1 * a2, a2 * h1 + h2

    _, h = lax.associative_scan(combine, (a_seq, bu), axis=1)
    y = jnp.real(jnp.einsum('gcp,blgp->blgc', cmat, h)).reshape(b, L, S5_WIDTH)
    y = jax.nn.gelu(y + d_skip.astype(f32) * uf)
    y = y * jax.nn.sigmoid(y @ w_glu.astype(f32))
    return y.astype(u.dtype)


def _fwd_setup_inputs(seed: int = 0) -> dict:
    key = jax.random.key(seed)
    ks = jax.random.split(key, 32)
    f32 = jnp.float32

    def nrm(k, shape, scale):
        return jax.random.normal(k, shape, f32) * scale

    G, P = S5_GROUPS, S5_STATE
    dt0 = jnp.exp(jax.random.uniform(ks[5], (DEPTH, SSD_HEADS), f32, minval=math.log(1e-3), maxval=math.log(1e-1)))
    n_idx = jnp.arange(P, dtype=f32)
    return {
        'x': nrm(ks[0], (BATCH, SEQ, D_MODEL), 1.0),
        'meta': nrm(ks[1], (N_META, D_MODEL), 1.0),
        'norm1': 1.0 + nrm(ks[2], (DEPTH, D_MODEL), 0.01),
        'w_in': nrm(ks[3], (DEPTH, D_MODEL, D_IN), D_MODEL ** -0.5),
        'ssd_conv_w': nrm(ks[4], (DEPTH, SSD_CONV, SSD_CONV_DIM), SSD_CONV ** -0.5),
        'ssd_conv_b': nrm(ks[6], (DEPTH, SSD_CONV_DIM), 0.01),
        'ssd_dt_bias': dt0 + jnp.log(-jnp.expm1(-dt0)),
        'ssd_a_log': jnp.log(jax.random.uniform(ks[7], (DEPTH, SSD_HEADS), f32, minval=1.0, maxval=16.0)),
        'ssd_d': 1.0 + nrm(ks[8], (DEPTH, SSD_HEADS), 0.1),
        'ssd_norm': 1.0 + nrm(ks[9], (DEPTH, SSD_D_INNER), 0.01),
        'fox_bf': jax.random.uniform(ks[10], (DEPTH, FOX_HEADS), f32, minval=1.0, maxval=4.0),
        's5_lam_re': -0.5 + nrm(ks[11], (DEPTH, G, P), 0.01),
        's5_lam_im': math.pi * n_idx + nrm(ks[12], (DEPTH, G, P), 0.01),
        's5_b_re': nrm(ks[13], (DEPTH, G, P, S5_GROUP), (2 * S5_GROUP) ** -0.5),
        's5_b_im': nrm(ks[14], (DEPTH, G, P, S5_GROUP), (2 * S5_GROUP) ** -0.5),
        's5_c_re': nrm(ks[15], (DEPTH, G, S5_GROUP, P), P ** -0.5),
        's5_c_im': nrm(ks[16], (DEPTH, G, S5_GROUP, P), P ** -0.5),
        's5_log_step': jax.random.uniform(ks[17], (DEPTH, G), f32, minval=math.log(1e-3), maxval=math.log(1e-1)),
        's5_d': nrm(ks[18], (DEPTH, S5_WIDTH), 1.0),
        's5_w_glu': nrm(ks[19], (DEPTH, S5_WIDTH, S5_WIDTH), S5_WIDTH ** -0.5),
        'w_branch': nrm(ks[20], (DEPTH, N_BRANCH, D_MODEL, D_MODEL), D_MODEL ** -0.5),
        'w_out': nrm(ks[21], (DEPTH, D_MODEL, D_MODEL), D_MODEL ** -0.5),
        'norm2': 1.0 + nrm(ks[22], (DEPTH, D_MODEL), 0.01),
        'w_ffn_in': nrm(ks[23], (DEPTH, D_MODEL, 2 * D_FF), D_MODEL ** -0.5),
        'w_ffn_out': nrm(ks[24], (DEPTH, D_FF, D_MODEL), D_FF ** -0.5),
        'norm_f': 1.0 + nrm(ks[25], (D_MODEL,), 0.01),
    }


def _fwd_reference(x, meta, norm1, w_in, ssd_conv_w, ssd_conv_b, ssd_dt_bias, ssd_a_log, ssd_d, ssd_norm,
              fox_bf, s5_lam_re, s5_lam_im, s5_b_re, s5_b_im, s5_c_re, s5_c_im, s5_log_step, s5_d,
              s5_w_glu, w_branch, w_out, norm2, w_ffn_in, w_ffn_out, norm_f):
    b = x.shape[0]
    x = jnp.concatenate([jnp.broadcast_to(meta[None].astype(x.dtype), (b, N_META, D_MODEL)), x], axis=1)
    L = x.shape[1]
    offsets = np.cumsum(IN_SPLITS)[:-1].tolist()
    for i in range(DEPTH):
        xn = rmsnorm(x, norm1[i])
        proj = xn @ w_in[i]
        z, xbc, dt_raw, qkv, f_raw, u, gate_logits = jnp.split(proj, offsets, axis=-1)
        y_a = ssd_mixer(z, xbc, dt_raw, ssd_conv_w[i], ssd_conv_b[i], ssd_dt_bias[i], ssd_a_log[i], ssd_d[i], ssd_norm[i])
        y_b = fox_mixer(qkv, f_raw, fox_bf[i])
        y_c = s5_mixer(u, s5_lam_re[i], s5_lam_im[i], s5_b_re[i], s5_b_im[i], s5_c_re[i], s5_c_im[i],
                       s5_log_step[i], s5_d[i], s5_w_glu[i])
        ys = jnp.stack([y_a, y_b, y_c], axis=2)
        branches = jnp.einsum('blnw,nwd->blnd', ys, w_branch[i])
        gates = jax.nn.sigmoid(gate_logits.reshape(b, L, N_BRANCH, D_MODEL))
        x = x + jnp.sum(gates * branches, axis=2) @ w_out[i]
        g, up = jnp.split(rmsnorm(x, norm2[i]) @ w_ffn_in[i], 2, axis=-1)
        x = x + (jax.nn.silu(g) * up) @ w_ffn_out[i]
    return rmsnorm(x, norm_f)[:, N_META:]


import jax as _jax
import jax.numpy as _jnp

TWIN_FORMAT = 'train_step'
FWD_PARAMS = ['x', 'meta', 'norm1', 'w_in', 'ssd_conv_w', 'ssd_conv_b', 'ssd_dt_bias', 'ssd_a_log', 'ssd_d', 'ssd_norm', 'fox_bf', 's5_lam_re', 's5_lam_im', 's5_b_re', 's5_b_im', 's5_c_re', 's5_c_im', 's5_log_step', 's5_d', 's5_w_glu', 'w_branch', 'w_out', 'norm2', 'w_ffn_in', 'w_ffn_out', 'norm_f']
TWIN_WEIGHTS = ['meta', 'norm1', 'w_in', 'ssd_conv_w', 'ssd_conv_b', 'ssd_dt_bias', 'ssd_a_log', 'ssd_d', 'ssd_norm', 'fox_bf', 's5_lam_re', 's5_lam_im', 's5_b_re', 's5_b_im', 's5_c_re', 's5_c_im', 's5_log_step', 's5_d', 's5_w_glu', 'w_branch', 'w_out', 'norm2', 'w_ffn_in', 'w_ffn_out', 'norm_f']
TWIN_DIFF_INPUT = 'x'
TWIN_INPUTS = ['x', 'meta', 'norm1', 'w_in', 'ssd_conv_w', 'ssd_conv_b', 'ssd_dt_bias', 'ssd_a_log', 'ssd_d', 'ssd_norm', 'fox_bf', 's5_lam_re', 's5_lam_im', 's5_b_re', 's5_b_im', 's5_c_re', 's5_c_im', 's5_log_step', 's5_d', 's5_w_glu', 'w_branch', 'w_out', 'norm2', 'w_ffn_in', 'w_ffn_out', 'norm_f', 'loss_target', 'm_meta', 'm_norm1', 'm_w_in', 'm_ssd_conv_w', 'm_ssd_conv_b', 'm_ssd_dt_bias', 'm_ssd_a_log', 'm_ssd_d', 'm_ssd_norm', 'm_fox_bf', 'm_s5_lam_re', 'm_s5_lam_im', 'm_s5_b_re', 'm_s5_b_im', 'm_s5_c_re', 'm_s5_c_im', 'm_s5_log_step', 'm_s5_d', 'm_s5_w_glu', 'm_w_branch', 'm_w_out', 'm_norm2', 'm_w_ffn_in', 'm_w_ffn_out', 'm_norm_f', 'v_meta', 'v_norm1', 'v_w_in', 'v_ssd_conv_w', 'v_ssd_conv_b', 'v_ssd_dt_bias', 'v_ssd_a_log', 'v_ssd_d', 'v_ssd_norm', 'v_fox_bf', 'v_s5_lam_re', 'v_s5_lam_im', 'v_s5_b_re', 'v_s5_b_im', 'v_s5_c_re', 'v_s5_c_im', 'v_s5_log_step', 'v_s5_d', 'v_s5_w_glu', 'v_w_branch', 'v_w_out', 'v_norm2', 'v_w_ffn_in', 'v_w_ffn_out', 'v_norm_f']
TWIN_OUTPUTS = ['loss', 'grad_x', 'grad_meta', 'grad_norm1', 'grad_w_in', 'grad_ssd_conv_w', 'grad_ssd_conv_b', 'grad_ssd_dt_bias', 'grad_ssd_a_log', 'grad_ssd_d', 'grad_ssd_norm', 'grad_fox_bf', 'grad_s5_lam_re', 'grad_s5_lam_im', 'grad_s5_b_re', 'grad_s5_b_im', 'grad_s5_c_re', 'grad_s5_c_im', 'grad_s5_log_step', 'grad_s5_d', 'grad_s5_w_glu', 'grad_w_branch', 'grad_w_out', 'grad_norm2', 'grad_w_ffn_in', 'grad_w_ffn_out', 'grad_norm_f', 'delta_meta', 'delta_norm1', 'delta_w_in', 'delta_ssd_conv_w', 'delta_ssd_conv_b', 'delta_ssd_dt_bias', 'delta_ssd_a_log', 'delta_ssd_d', 'delta_ssd_norm', 'delta_fox_bf', 'delta_s5_lam_re', 'delta_s5_lam_im', 'delta_s5_b_re', 'delta_s5_b_im', 'delta_s5_c_re', 'delta_s5_c_im', 'delta_s5_log_step', 'delta_s5_d', 'delta_s5_w_glu', 'delta_w_branch', 'delta_w_out', 'delta_norm2', 'delta_w_ffn_in', 'delta_w_ffn_out', 'delta_norm_f', 'new_m_meta', 'new_m_norm1', 'new_m_w_in', 'new_m_ssd_conv_w', 'new_m_ssd_conv_b', 'new_m_ssd_dt_bias', 'new_m_ssd_a_log', 'new_m_ssd_d', 'new_m_ssd_norm', 'new_m_fox_bf', 'new_m_s5_lam_re', 'new_m_s5_lam_im', 'new_m_s5_b_re', 'new_m_s5_b_im', 'new_m_s5_c_re', 'new_m_s5_c_im', 'new_m_s5_log_step', 'new_m_s5_d', 'new_m_s5_w_glu', 'new_m_w_branch', 'new_m_w_out', 'new_m_norm2', 'new_m_w_ffn_in', 'new_m_w_ffn_out', 'new_m_norm_f', 'new_v_meta', 'new_v_norm1', 'new_v_w_in', 'new_v_ssd_conv_w', 'new_v_ssd_conv_b', 'new_v_ssd_dt_bias', 'new_v_ssd_a_log', 'new_v_ssd_d', 'new_v_ssd_norm', 'new_v_fox_bf', 'new_v_s5_lam_re', 'new_v_s5_lam_im', 'new_v_s5_b_re', 'new_v_s5_b_im', 'new_v_s5_c_re', 'new_v_s5_c_im', 'new_v_s5_log_step', 'new_v_s5_d', 'new_v_s5_w_glu', 'new_v_w_branch', 'new_v_w_out', 'new_v_norm2', 'new_v_w_ffn_in', 'new_v_w_ffn_out', 'new_v_norm_f']
TWIN_LEAF_KINDS = {'loss': 'loss', 'grad_x': 'grad_x', 'grad_meta': 'grad_w', 'grad_norm1': 'grad_w', 'grad_w_in': 'grad_w', 'grad_ssd_conv_w': 'grad_w', 'grad_ssd_conv_b': 'grad_w', 'grad_ssd_dt_bias': 'grad_w', 'grad_ssd_a_log': 'grad_w', 'grad_ssd_d': 'grad_w', 'grad_ssd_norm': 'grad_w', 'grad_fox_bf': 'grad_w', 'grad_s5_lam_re': 'grad_w', 'grad_s5_lam_im': 'grad_w', 'grad_s5_b_re': 'grad_w', 'grad_s5_b_im': 'grad_w', 'grad_s5_c_re': 'grad_w', 'grad_s5_c_im': 'grad_w', 'grad_s5_log_step': 'grad_w', 'grad_s5_d': 'grad_w', 'grad_s5_w_glu': 'grad_w', 'grad_w_branch': 'grad_w', 'grad_w_out': 'grad_w', 'grad_norm2': 'grad_w', 'grad_w_ffn_in': 'grad_w', 'grad_w_ffn_out': 'grad_w', 'grad_norm_f': 'grad_w', 'delta_meta': 'delta_w', 'delta_norm1': 'delta_w', 'delta_w_in': 'delta_w', 'delta_ssd_conv_w': 'delta_w', 'delta_ssd_conv_b': 'delta_w', 'delta_ssd_dt_bias': 'delta_w', 'delta_ssd_a_log': 'delta_w', 'delta_ssd_d': 'delta_w', 'delta_ssd_norm': 'delta_w', 'delta_fox_bf': 'delta_w', 'delta_s5_lam_re': 'delta_w', 'delta_s5_lam_im': 'delta_w', 'delta_s5_b_re': 'delta_w', 'delta_s5_b_im': 'delta_w', 'delta_s5_c_re': 'delta_w', 'delta_s5_c_im': 'delta_w', 'delta_s5_log_step': 'delta_w', 'delta_s5_d': 'delta_w', 'delta_s5_w_glu': 'delta_w', 'delta_w_branch': 'delta_w', 'delta_w_out': 'delta_w', 'delta_norm2': 'delta_w', 'delta_w_ffn_in': 'delta_w', 'delta_w_ffn_out': 'delta_w', 'delta_norm_f': 'delta_w', 'new_m_meta': 'new_m', 'new_m_norm1': 'new_m', 'new_m_w_in': 'new_m', 'new_m_ssd_conv_w': 'new_m', 'new_m_ssd_conv_b': 'new_m', 'new_m_ssd_dt_bias': 'new_m', 'new_m_ssd_a_log': 'new_m', 'new_m_ssd_d': 'new_m', 'new_m_ssd_norm': 'new_m', 'new_m_fox_bf': 'new_m', 'new_m_s5_lam_re': 'new_m', 'new_m_s5_lam_im': 'new_m', 'new_m_s5_b_re': 'new_m', 'new_m_s5_b_im': 'new_m', 'new_m_s5_c_re': 'new_m', 'new_m_s5_c_im': 'new_m', 'new_m_s5_log_step': 'new_m', 'new_m_s5_d': 'new_m', 'new_m_s5_w_glu': 'new_m', 'new_m_w_branch': 'new_m', 'new_m_w_out': 'new_m', 'new_m_norm2': 'new_m', 'new_m_w_ffn_in': 'new_m', 'new_m_w_ffn_out': 'new_m', 'new_m_norm_f': 'new_m', 'new_v_meta': 'new_v', 'new_v_norm1': 'new_v', 'new_v_w_in': 'new_v', 'new_v_ssd_conv_w': 'new_v', 'new_v_ssd_conv_b': 'new_v', 'new_v_ssd_dt_bias': 'new_v', 'new_v_ssd_a_log': 'new_v', 'new_v_ssd_d': 'new_v', 'new_v_ssd_norm': 'new_v', 'new_v_fox_bf': 'new_v', 'new_v_s5_lam_re': 'new_v', 'new_v_s5_lam_im': 'new_v', 'new_v_s5_b_re': 'new_v', 'new_v_s5_b_im': 'new_v', 'new_v_s5_c_re': 'new_v', 'new_v_s5_c_im': 'new_v', 'new_v_s5_log_step': 'new_v', 'new_v_s5_d': 'new_v', 'new_v_s5_w_glu': 'new_v', 'new_v_w_branch': 'new_v', 'new_v_w_out': 'new_v', 'new_v_norm2': 'new_v', 'new_v_w_ffn_in': 'new_v', 'new_v_w_ffn_out': 'new_v', 'new_v_norm_f': 'new_v'}


def _forward(args):
    return _fwd_reference(*[args[k] for k in FWD_PARAMS])


def _output_shape():
    out = _jax.eval_shape(lambda: _forward(_fwd_setup_inputs(0)))
    return out.shape, out.dtype

N_MICROBATCH = 1
ADAM_LR = 0.001
ADAM_B1 = 0.9
ADAM_B2 = 0.999
ADAM_EPS = 1e-08
ADAM_WD = 0.01
ADAM_STEP = 10
PER_EXAMPLE_BATCH_AXIS = {'x': 0, 'loss_target': 0}
SHARED_INPUTS = []
_WEIGHT_DTYPES = {'meta': _jnp.float32, 'norm1': _jnp.float32, 'w_in': _jnp.float32, 'ssd_conv_w': _jnp.float32, 'ssd_conv_b': _jnp.float32, 'ssd_dt_bias': _jnp.float32, 'ssd_a_log': _jnp.float32, 'ssd_d': _jnp.float32, 'ssd_norm': _jnp.float32, 'fox_bf': _jnp.float32, 's5_lam_re': _jnp.float32, 's5_lam_im': _jnp.float32, 's5_b_re': _jnp.float32, 's5_b_im': _jnp.float32, 's5_c_re': _jnp.float32, 's5_c_im': _jnp.float32, 's5_log_step': _jnp.float32, 's5_d': _jnp.float32, 's5_w_glu': _jnp.float32, 'w_branch': _jnp.float32, 'w_out': _jnp.float32, 'norm2': _jnp.float32, 'w_ffn_in': _jnp.float32, 'w_ffn_out': _jnp.float32, 'norm_f': _jnp.float32}
MOMENT_SCALE = {'meta': 1.115027e-02, 'norm1': 2.291729e-01, 'w_in': 7.495937e-02, 'ssd_conv_w': 1.151268e-01, 'ssd_conv_b': 1.552477e-01, 'ssd_dt_bias': 7.501824e-01, 'ssd_a_log': 4.786846e-01, 'ssd_d': 1.093556e+00, 'ssd_norm': 1.406937e-01, 'fox_bf': 2.791266e-01, 's5_lam_re': 4.228460e-03, 's5_lam_im': 3.985174e-03, 's5_b_re': 2.376185e-03, 's5_b_im': 2.440937e-03, 's5_c_re': 3.397541e-03, 's5_c_im': 3.376071e-03, 's5_log_step': 2.972594e+00, 's5_d': 5.072313e-02, 's5_w_glu': 1.394932e-02, 'w_branch': 8.674859e-02, 'w_out': 1.499579e-01, 'norm2': 1.928511e-01, 'w_ffn_in': 7.522194e-02, 'w_ffn_out': 1.226323e-01, 'norm_f': 6.398525e+01}


def _to_microbatches(a, axis):
    t = _jnp.moveaxis(a, axis, 0)
    t = t.reshape((N_MICROBATCH, t.shape[0] // N_MICROBATCH) + t.shape[1:])
    return _jnp.moveaxis(t, 1, axis + 1)


def setup_inputs(seed: int = 0) -> dict:
    inp = _fwd_setup_inputs(seed)
    key = _jax.random.fold_in(_jax.random.key(seed), 7919)
    shape, _ = _output_shape()
    out = dict(inp)
    out["loss_target"] = _jax.random.normal(_jax.random.fold_in(key, 0), shape, _jnp.float32)
    for i, name in enumerate(TWIN_WEIGHTS):
        w = inp[name].astype(_jnp.float32)
        if MOMENT_SCALE is None:
            s = _jnp.sqrt(_jnp.mean(_jnp.square(w)) + 1e-30)
        else:
            s = MOMENT_SCALE[name]
        km, kv = _jax.random.split(_jax.random.fold_in(key, i + 1))
        out[name] = w
        out["m_" + name] = s * _jax.random.normal(km, w.shape, _jnp.float32)
        out["v_" + name] = (s * s) * _jax.random.uniform(kv, w.shape, _jnp.float32, 0.5, 1.5)
    if N_MICROBATCH > 1:
        for name, axis in PER_EXAMPLE_BATCH_AXIS.items():
            out[name] = _to_microbatches(out[name], axis)
    return {'x': out['x'], 'meta': out['meta'], 'norm1': out['norm1'], 'w_in': out['w_in'], 'ssd_conv_w': out['ssd_conv_w'], 'ssd_conv_b': out['ssd_conv_b'], 'ssd_dt_bias': out['ssd_dt_bias'], 'ssd_a_log': out['ssd_a_log'], 'ssd_d': out['ssd_d'], 'ssd_norm': out['ssd_norm'], 'fox_bf': out['fox_bf'], 's5_lam_re': out['s5_lam_re'], 's5_lam_im': out['s5_lam_im'], 's5_b_re': out['s5_b_re'], 's5_b_im': out['s5_b_im'], 's5_c_re': out['s5_c_re'], 's5_c_im': out['s5_c_im'], 's5_log_step': out['s5_log_step'], 's5_d': out['s5_d'], 's5_w_glu': out['s5_w_glu'], 'w_branch': out['w_branch'], 'w_out': out['w_out'], 'norm2': out['norm2'], 'w_ffn_in': out['w_ffn_in'], 'w_ffn_out': out['w_ffn_out'], 'norm_f': out['norm_f'], 'loss_target': out['loss_target'], 'm_meta': out['m_meta'], 'm_norm1': out['m_norm1'], 'm_w_in': out['m_w_in'], 'm_ssd_conv_w': out['m_ssd_conv_w'], 'm_ssd_conv_b': out['m_ssd_conv_b'], 'm_ssd_dt_bias': out['m_ssd_dt_bias'], 'm_ssd_a_log': out['m_ssd_a_log'], 'm_ssd_d': out['m_ssd_d'], 'm_ssd_norm': out['m_ssd_norm'], 'm_fox_bf': out['m_fox_bf'], 'm_s5_lam_re': out['m_s5_lam_re'], 'm_s5_lam_im': out['m_s5_lam_im'], 'm_s5_b_re': out['m_s5_b_re'], 'm_s5_b_im': out['m_s5_b_im'], 'm_s5_c_re': out['m_s5_c_re'], 'm_s5_c_im': out['m_s5_c_im'], 'm_s5_log_step': out['m_s5_log_step'], 'm_s5_d': out['m_s5_d'], 'm_s5_w_glu': out['m_s5_w_glu'], 'm_w_branch': out['m_w_branch'], 'm_w_out': out['m_w_out'], 'm_norm2': out['m_norm2'], 'm_w_ffn_in': out['m_w_ffn_in'], 'm_w_ffn_out': out['m_w_ffn_out'], 'm_norm_f': out['m_norm_f'], 'v_meta': out['v_meta'], 'v_norm1': out['v_norm1'], 'v_w_in': out['v_w_in'], 'v_ssd_conv_w': out['v_ssd_conv_w'], 'v_ssd_conv_b': out['v_ssd_conv_b'], 'v_ssd_dt_bias': out['v_ssd_dt_bias'], 'v_ssd_a_log': out['v_ssd_a_log'], 'v_ssd_d': out['v_ssd_d'], 'v_ssd_norm': out['v_ssd_norm'], 'v_fox_bf': out['v_fox_bf'], 'v_s5_lam_re': out['v_s5_lam_re'], 'v_s5_lam_im': out['v_s5_lam_im'], 'v_s5_b_re': out['v_s5_b_re'], 'v_s5_b_im': out['v_s5_b_im'], 'v_s5_c_re': out['v_s5_c_re'], 'v_s5_c_im': out['v_s5_c_im'], 'v_s5_log_step': out['v_s5_log_step'], 'v_s5_d': out['v_s5_d'], 'v_s5_w_glu': out['v_s5_w_glu'], 'v_w_branch': out['v_w_branch'], 'v_w_out': out['v_w_out'], 'v_norm2': out['v_norm2'], 'v_w_ffn_in': out['v_w_ffn_in'], 'v_w_ffn_out': out['v_w_ffn_out'], 'v_norm_f': out['v_norm_f']}


def _loss(weights, diff, rest, loss_target):
    with _jax.named_scope("forward"):
        args = {**rest, TWIN_DIFF_INPUT: diff, **{k: w.astype(_WEIGHT_DTYPES[k]) for k, w in weights.items()}}
        y = _forward(args)
    with _jax.named_scope("loss_head"):
        err = _jnp.square(y.astype(_jnp.float32) - loss_target)
        return 0.5 * _jnp.sum(_jnp.mean(err, axis=-1)) if err.ndim else 0.5 * err


def _adamw(w, g, m, v):
    m = ADAM_B1 * m + (1.0 - ADAM_B1) * g
    v = ADAM_B2 * v + (1.0 - ADAM_B2) * _jnp.square(g)
    m_hat = m / (1.0 - ADAM_B1 ** ADAM_STEP)
    v_hat = v / (1.0 - ADAM_B2 ** ADAM_STEP)
    delta = -ADAM_LR * (m_hat / (_jnp.sqrt(v_hat) + ADAM_EPS) + ADAM_WD * w)
    return delta, m, v


def reference(x, meta, norm1, w_in, ssd_conv_w, ssd_conv_b, ssd_dt_bias, ssd_a_log, ssd_d, ssd_norm, fox_bf, s5_lam_re, s5_lam_im, s5_b_re, s5_b_im, s5_c_re, s5_c_im, s5_log_step, s5_d, s5_w_glu, w_branch, w_out, norm2, w_ffn_in, w_ffn_out, norm_f, loss_target, m_meta, m_norm1, m_w_in, m_ssd_conv_w, m_ssd_conv_b, m_ssd_dt_bias, m_ssd_a_log, m_ssd_d, m_ssd_norm, m_fox_bf, m_s5_lam_re, m_s5_lam_im, m_s5_b_re, m_s5_b_im, m_s5_c_re, m_s5_c_im, m_s5_log_step, m_s5_d, m_s5_w_glu, m_w_branch, m_w_out, m_norm2, m_w_ffn_in, m_w_ffn_out, m_norm_f, v_meta, v_norm1, v_w_in, v_ssd_conv_w, v_ssd_conv_b, v_ssd_dt_bias, v_ssd_a_log, v_ssd_d, v_ssd_norm, v_fox_bf, v_s5_lam_re, v_s5_lam_im, v_s5_b_re, v_s5_b_im, v_s5_c_re, v_s5_c_im, v_s5_log_step, v_s5_d, v_s5_w_glu, v_w_branch, v_w_out, v_norm2, v_w_ffn_in, v_w_ffn_out, v_norm_f):
    given = dict(x=x, meta=meta, norm1=norm1, w_in=w_in, ssd_conv_w=ssd_conv_w, ssd_conv_b=ssd_conv_b, ssd_dt_bias=ssd_dt_bias, ssd_a_log=ssd_a_log, ssd_d=ssd_d, ssd_norm=ssd_norm, fox_bf=fox_bf, s5_lam_re=s5_lam_re, s5_lam_im=s5_lam_im, s5_b_re=s5_b_re, s5_b_im=s5_b_im, s5_c_re=s5_c_re, s5_c_im=s5_c_im, s5_log_step=s5_log_step, s5_d=s5_d, s5_w_glu=s5_w_glu, w_branch=w_branch, w_out=w_out, norm2=norm2, w_ffn_in=w_ffn_in, w_ffn_out=w_ffn_out, norm_f=norm_f, loss_target=loss_target, m_meta=m_meta, m_norm1=m_norm1, m_w_in=m_w_in, m_ssd_conv_w=m_ssd_conv_w, m_ssd_conv_b=m_ssd_conv_b, m_ssd_dt_bias=m_ssd_dt_bias, m_ssd_a_log=m_ssd_a_log, m_ssd_d=m_ssd_d, m_ssd_norm=m_ssd_norm, m_fox_bf=m_fox_bf, m_s5_lam_re=m_s5_lam_re, m_s5_lam_im=m_s5_lam_im, m_s5_b_re=m_s5_b_re, m_s5_b_im=m_s5_b_im, m_s5_c_re=m_s5_c_re, m_s5_c_im=m_s5_c_im, m_s5_log_step=m_s5_log_step, m_s5_d=m_s5_d, m_s5_w_glu=m_s5_w_glu, m_w_branch=m_w_branch, m_w_out=m_w_out, m_norm2=m_norm2, m_w_ffn_in=m_w_ffn_in, m_w_ffn_out=m_w_ffn_out, m_norm_f=m_norm_f, v_meta=v_meta, v_norm1=v_norm1, v_w_in=v_w_in, v_ssd_conv_w=v_ssd_conv_w, v_ssd_conv_b=v_ssd_conv_b, v_ssd_dt_bias=v_ssd_dt_bias, v_ssd_a_log=v_ssd_a_log, v_ssd_d=v_ssd_d, v_ssd_norm=v_ssd_norm, v_fox_bf=v_fox_bf, v_s5_lam_re=v_s5_lam_re, v_s5_lam_im=v_s5_lam_im, v_s5_b_re=v_s5_b_re, v_s5_b_im=v_s5_b_im, v_s5_c_re=v_s5_c_re, v_s5_c_im=v_s5_c_im, v_s5_log_step=v_s5_log_step, v_s5_d=v_s5_d, v_s5_w_glu=v_s5_w_glu, v_w_branch=v_w_branch, v_w_out=v_w_out, v_norm2=v_norm2, v_w_ffn_in=v_w_ffn_in, v_w_ffn_out=v_w_ffn_out, v_norm_f=v_norm_f)
    weights = {n: given[n] for n in TWIN_WEIGHTS}
    shared = {n: given[n] for n in SHARED_INPUTS}
    per_example = {n: given[n] for n in ['x']}
    grad_fn = _jax.value_and_grad(_loss, argnums=(0, 1))

    def one_microbatch(ex, loss_target):
        ex = dict(ex)
        diff = ex.pop(TWIN_DIFF_INPUT)
        return grad_fn(weights, diff, {**shared, **ex}, loss_target)

    if N_MICROBATCH == 1:
        loss, (grad_w, grad_x) = one_microbatch(per_example, given["loss_target"])
    else:
        def body(carry, xs):
            loss_sum, grad_sum = carry
            l_k, (gw_k, gx_k) = one_microbatch(xs[0], xs[1])
            with _jax.named_scope("update"):
                return (loss_sum + l_k, _jax.tree.map(_jnp.add, grad_sum, gw_k)), gx_k

        init = (_jnp.zeros((), _jnp.float32), _jax.tree.map(_jnp.zeros_like, weights))
        (loss, grad_w), grad_x = _jax.lax.scan(body, init, (per_example, given["loss_target"]))
    with _jax.named_scope("update"):
        delta_w, new_m, new_v = {}, {}, {}
        for n in TWIN_WEIGHTS:
            delta_w[n], new_m[n], new_v[n] = _adamw(weights[n], grad_w[n], given["m_" + n], given["v_" + n])
    return (loss, grad_x, *[grad_w[n] for n in TWIN_WEIGHTS], *[delta_w[n] for n in TWIN_WEIGHTS],
            *[new_m[n] for n in TWIN_WEIGHTS], *[new_v[n] for n in TWIN_WEIGHTS])
```

```python
import functools
import math

import numpy as np
import jax
import jax.numpy as jnp
from jax import lax
from jax.experimental import pallas as pl
from jax.experimental.pallas import tpu as pltpu

F32 = jnp.float32
BF16 = jnp.bfloat16
AXES = ("x", "y", "c")
MESH = pl.DeviceIdType.MESH
N_DEV = 8

D = 1024
N_META = 16
CH = 128
EPS = 1e-6
NEG = -1e30
SSD_H, SSD_P, SSD_N, SSD_G = 16, 64, 128, 2
CONV_K, CONV_DIM = 4, 1536
FOX_H, FOX_DH = 8, 128
S5_G, S5_P, S5_C = 64, 64, 16
S5_SG = 8
S5_W = S5_SG * S5_P
DFF = 2816
D_IN = 9752
OFF_Z, OFF_XBC, OFF_QKV, OFF_U, OFF_G, OFF_SM, D_CAT = 0, 1024, 2560, 5632, 6656, 9728, 9856
O_Z, O_XBC, O_DT, O_QKV, O_F, O_U, O_G = 0, 1024, 2560, 2576, 5648, 5656, 6680

ADAM_LR, ADAM_B1, ADAM_B2, ADAM_EPS, ADAM_WD, ADAM_STEP = 0.001, 0.9, 0.999, 1e-08, 0.01, 10

VMEM_LIMIT_V7X = 52 * 1024 * 1024
HI = lax.Precision.HIGHEST


def _cp(sem=None):
    return pltpu.CompilerParams(dimension_semantics=sem, vmem_limit_bytes=VMEM_LIMIT_V7X)


def _pick(n, cands):
    for c in cands:
        if n % c == 0:
            return c
    raise ValueError(f"no tile for {n}")


_TILES = (1408, 1024, 896, 768, 512, 384, 256, 128)


def _mm(a, b, mode, out_dtype, *, name, n=None, b_off=0, res=None, tm=None, tn=None, tk=None):
    if mode == "tn":
        K, M = a.shape
    else:
        M, K = a.shape
    if mode == "nt":
        N = b.shape[0]
    else:
        N = n if n is not None else b.shape[1]
    tm = tm or _pick(M, (1024, 768, 512, 384, 256, 128, 64, 16, 8))
    tn = tn or _pick(math.gcd(N, b_off) if b_off else N, (1024, 896, 768, 512, 384, 256, 128))
    tk = tk or _pick(K, _TILES)
    nk = K // tk
    joff = b_off // tn

    def body(*refs):
        if res is None:
            a_ref, b_ref, o_ref, acc = refs
            r_ref = None
        else:
            a_ref, b_ref, r_ref, o_ref, acc = refs
        k = pl.program_id(2)
        av = a_ref[...].astype(BF16)
        bv = b_ref[...].astype(BF16)
        if mode == "nn":
            p = jnp.dot(av, bv, preferred_element_type=F32)
        elif mode == "nt":
            p = lax.dot_general(av, bv, (((1,), (1,)), ((), ())), preferred_element_type=F32)
        else:
            p = lax.dot_general(av, bv, (((0,), (0,)), ((), ())), preferred_element_type=F32)

        @pl.when(k == 0)
        def _():
            acc[...] = p

        @pl.when(k > 0)
        def _():
            acc[...] += p

        @pl.when(k == nk - 1)
        def _():
            r = acc[...]
            if r_ref is not None:
                r = r + r_ref[...]
            o_ref[...] = r.astype(o_ref.dtype)

    if mode == "tn":
        a_spec = pl.BlockSpec((tk, tm), lambda i, j, k: (k, i))
    else:
        a_spec = pl.BlockSpec((tm, tk), lambda i, j, k: (i, k))
    if mode == "nt":
        b_spec = pl.BlockSpec((tn, tk), lambda i, j, k: (j, k))
    else:
        b_spec = pl.BlockSpec((tk, tn), lambda i, j, k: (k, j + joff))
    o_spec = pl.BlockSpec((tm, tn), lambda i, j, k: (i, j))
    in_specs = [a_spec, b_spec] + ([o_spec] if res is not None else [])
    args = (a, b) + ((res,) if res is not None else ())
    return pl.pallas_call(
        body, name=name, grid=(M // tm, N // tn, nk),
        in_specs=in_specs, out_specs=o_spec,
        out_shape=jax.ShapeDtypeStruct((M, N), out_dtype),
        scratch_shapes=[pltpu.VMEM((tm, tn), F32)],
        compiler_params=_cp(("parallel", "parallel", "arbitrary")),
    )(*args)


def _rowmap(fn, row_ins, const_ins, row_outs, acc_outs, *, geom, tile, name):
    T, Lp, pad = geom
    assert Lp % tile == 0
    per_seq = Lp // tile
    specs, args = [], []
    for r in row_ins:
        arr, w, cb = r if isinstance(r, tuple) else (r, r.shape[1], 0)
        specs.append(pl.BlockSpec((tile, w), functools.partial(lambda i, cb: (i, cb), cb=cb)))
        args.append(arr)
    for c in const_ins:
        specs.append(pl.BlockSpec(c.shape, functools.partial(lambda i, nd: (0,) * nd, nd=c.ndim)))
        args.append(c)
    n_r, n_c, n_o, n_a = len(row_ins), len(const_ins), len(row_outs), len(acc_outs)
    out_specs = [pl.BlockSpec((tile, w), lambda i: (i, 0)) for w, _ in row_outs]
    out_specs += [pl.BlockSpec(s, lambda i: (0, 0)) for s in acc_outs]
    out_shape = [jax.ShapeDtypeStruct((T, w), dt) for w, dt in row_outs]
    out_shape += [jax.ShapeDtypeStruct(s, F32) for s in acc_outs]

    def body(*refs):
        i = pl.program_id(0)
        pos = (i % per_seq) * tile + lax.broadcasted_iota(jnp.int32, (tile, 1), 0)
        valid = pos >= pad
        vals = [r[...] for r in refs[:n_r + n_c]]
        outs = fn(valid, *vals)
        if not isinstance(outs, (tuple, list)):
            outs = (outs,)
        orefs = refs[n_r + n_c:]
        for r, v in zip(orefs[:n_o], outs[:n_o]):
            r[...] = v.astype(r.dtype)
        for r, v in zip(orefs[n_o:], outs[n_o:]):
            @pl.when(i == 0)
            def _(r=r, v=v):
                r[...] = v

            @pl.when(i > 0)
            def _(r=r, v=v):
                r[...] += v

    res = pl.pallas_call(
        body, name=name, grid=(T // tile,), in_specs=specs, out_specs=out_specs, out_shape=out_shape,
        compiler_params=_cp(("arbitrary",)),
    )(*args)
    return res


def _sigmoid(x):
    return 1.0 / (1.0 + jnp.exp(-x))


def _silu(x):
    return x * _sigmoid(x)


def _softplus(x):
    return jnp.maximum(x, 0.0) + jnp.log(1.0 + jnp.exp(-jnp.abs(x)))


def _gelu(x):
    return 0.5 * x * (1.0 + jnp.tanh(math.sqrt(2.0 / math.pi) * (x + 0.044715 * x * x * x)))


def _rms(x, w):
    return x * lax.rsqrt(jnp.mean(x * x, axis=-1, keepdims=True) + EPS) * w


def _colsum(v):
    return jnp.sum(v, axis=0, keepdims=True)


def _f_norm(valid, x, w):
    return _rms(x, w)


def _b_norm(valid, x, dxn, dres, w):
    _, vjp = jax.vjp(_rms, x, w)
    dx, dw = vjp(dxn)
    return jnp.where(valid, dx + dres, 0.0), dw


def _smallact(valid, raw, bias):
    lane = lax.broadcasted_iota(jnp.int32, raw.shape, 1)
    v = raw + bias
    dt = _softplus(v)
    logf = -_softplus(-v)
    out = jnp.where(lane < SSD_H, dt, jnp.where(lane < SSD_H + FOX_H, logf, 0.0))
    return jnp.where(valid, out, 0.0)


def _b_smallact(valid, raw, d1, d2, bias):
    _, vjp = jax.vjp(lambda r, b: _smallact(valid, r, b), raw, bias)
    return vjp(d1 + d2)


def _ssd_post(valid, y, xs, z, drep, nw):
    y = (y + xs * drep) * _silu(z)
    return _rms(y, nw)


def _b_ssd_post(valid, y, xs, z, dya, drep, nw):
    _, vjp = jax.vjp(lambda a, b, c, d, e: _ssd_post(valid, a, b, c, d, e), y, xs, z, drep, nw)
    dy, dxs, dz, dd, dn = vjp(dya)
    return dy, dxs, dz, dd, dn


def _s5_pre(valid, ys, u, d):
    return _gelu(ys + d * u)


def _s5_glu(valid, ys, u, t, d):
    y1 = _gelu(ys + d * u)
    return y1 * _sigmoid(t)


def _b_s5_glu(valid, ys, u, t, dyc, d):
    y1 = _gelu(ys + d * u)
    _, vjp = jax.vjp(lambda a, b: a * _sigmoid(b), y1, t)
    dy1, dt = vjp(dyc)
    return dt, dy1


def _b_s5_pre(valid, ys, u, dy1a, dy1b, d):
    _, vjp = jax.vjp(lambda a, b, c: _gelu(a + c * b), ys, u, d)
    dys, du, dd = vjp(dy1a + dy1b)
    return dys, du, dd


def _merge(valid, g0, g1, g2, b0, b1, b2):
    m = _sigmoid(g0) * b0 + _sigmoid(g1) * b1 + _sigmoid(g2) * b2
    return jnp.where(valid, m, 0.0)


def _b_merge(valid, g0, g1, g2, b0, b1, b2, dmix):
    _, vjp = jax.vjp(lambda *a: _merge(valid, *a), g0, g1, g2, b0, b1, b2)
    d = vjp(dmix)
    return jnp.concatenate(d[:3], axis=1), d[3], d[4], d[5]


def _swiglu(valid, g, up):
    return _silu(g) * up


def _b_swiglu(valid, g, up, dact):
    _, vjp = jax.vjp(lambda a, b: _silu(a) * b, g, up)
    dg, dup = vjp(dact)
    return jnp.concatenate([dg, dup], axis=1)


def _conv_taps(ext, tile):
    taps = []
    for k in range(CONV_K):
        sh = CONV_K - 1 - k
        v = ext if sh == 0 else pltpu.roll(ext, sh, 0)
        taps.append(v[8:8 + tile])
    return taps


def _conv_fwd(x, w, b, *, geom, tile):
    T, Lp, pad = geom
    per_seq = Lp // tile
    hb = tile // 8

    def body(x_ref, h_ref, w_ref, b_ref, o_ref):
        i = pl.program_id(0)
        pos = (i % per_seq) * tile + lax.broadcasted_iota(jnp.int32, (tile, 1), 0)
        ext = jnp.concatenate([h_ref[...], x_ref[...]], axis=0)
        taps = _conv_taps(ext, tile)
        acc = b_ref[...] + taps[0] * w_ref[0:1, :]
        for k in range(1, CONV_K):
            acc = acc + taps[k] * w_ref[k:k + 1, :]
        o_ref[...] = jnp.where(pos >= pad, _silu(acc), 0.0)

    return pl.pallas_call(
        body, name="conv_fwd", grid=(T // tile,),
        in_specs=[pl.BlockSpec((tile, CONV_DIM), lambda i: (i, 0)),
                  pl.BlockSpec((8, CONV_DIM), lambda i: (jnp.maximum(i * hb - 1, 0), 0)),
                  pl.BlockSpec((CONV_K, CONV_DIM), lambda i: (0, 0)),
                  pl.BlockSpec((1, CONV_DIM), lambda i: (0, 0))],
        out_specs=pl.BlockSpec((tile, CONV_DIM), lambda i: (i, 0)),
        out_shape=jax.ShapeDtypeStruct((T, CONV_DIM), F32),
        compiler_params=_cp(("arbitrary",)),
    )(x, x, w, b)


def _conv_bwd_pre(x, dact, w, b, *, geom, tile):
    T, Lp, pad = geom
    per_seq = Lp // tile
    hb = tile // 8

    def body(x_ref, h_ref, d_ref, w_ref, b_ref, dc_ref, dw_ref, db_ref):
        i = pl.program_id(0)
        pos = (i % per_seq) * tile + lax.broadcasted_iota(jnp.int32, (tile, 1), 0)
        ext = jnp.concatenate([h_ref[...], x_ref[...]], axis=0)
        taps = _conv_taps(ext, tile)
        acc = b_ref[...] + taps[0] * w_ref[0:1, :]
        for k in range(1, CONV_K):
            acc = acc + taps[k] * w_ref[k:k + 1, :]
        sg = _sigmoid(acc)
        dsilu = sg * (1.0 + acc * (1.0 - sg))
        dc = jnp.where(pos >= pad, d_ref[...] * dsilu, 0.0)
        dc_ref[...] = dc
        dw = jnp.concatenate([_colsum(dc * taps[k]) for k in range(CONV_K)], axis=0)
        db = _colsum(dc)

        @pl.when(i == 0)
        def _():
            dw_ref[...] = dw
            db_ref[...] = db

        @pl.when(i > 0)
        def _():
            dw_ref[...] += dw
            db_ref[...] += db

    return pl.pallas_call(
        body, name="conv_bwd_pre", grid=(T // tile,),
        in_specs=[pl.BlockSpec((tile, CONV_DIM), lambda i: (i, 0)),
                  pl.BlockSpec((8, CONV_DIM), lambda i: (jnp.maximum(i * hb - 1, 0), 0)),
                  pl.BlockSpec((tile, CONV_DIM), lambda i: (i, 0)),
                  pl.BlockSpec((CONV_K, CONV_DIM), lambda i: (0, 0)),
                  pl.BlockSpec((1, CONV_DIM), lambda i: (0, 0))],
        out_specs=[pl.BlockSpec((tile, CONV_DIM), lambda i: (i, 0)),
                   pl.BlockSpec((CONV_K, CONV_DIM), lambda i: (0, 0)),
                   pl.BlockSpec((1, CONV_DIM), lambda i: (0, 0))],
        out_shape=[jax.ShapeDtypeStruct((T, CONV_DIM), F32),
                   jax.ShapeDtypeStruct((CONV_K, CONV_DIM), F32),
                   jax.ShapeDtypeStruct((1, CONV_DIM), F32)],
        compiler_params=_cp(("arbitrary",)),
    )(x, x, dact, w, b)


def _conv_bwd_x(dc, w, *, geom, tile):
    T, Lp, pad = geom
    nt = T // tile
    hb = tile // 8

    def body(d_ref, h_ref, w_ref, o_ref):
        i = pl.program_id(0)
        halo = jnp.where(i < nt - 1, h_ref[...], 0.0)
        ext = jnp.concatenate([d_ref[...], halo], axis=0)
        n_ext = tile + 8
        acc = ext[0:tile] * w_ref[CONV_K - 1:CONV_K, :]
        for j in range(1, CONV_K):
            acc = acc + pltpu.roll(ext, n_ext - j, 0)[0:tile] * w_ref[CONV_K - 1 - j:CONV_K - j, :]
        o_ref[...] = acc.astype(o_ref.dtype)

    return pl.pallas_call(
        body, name="conv_bwd_x", grid=(nt,),
        in_specs=[pl.BlockSpec((tile, CONV_DIM), lambda i: (i, 0)),
                  pl.BlockSpec((8, CONV_DIM), lambda i: (jnp.minimum((i + 1) * hb, nt * hb - 1), 0)),
                  pl.BlockSpec((CONV_K, CONV_DIM), lambda i: (0, 0))],
        out_specs=pl.BlockSpec((tile, CONV_DIM), lambda i: (i, 0)),
        out_shape=jax.ShapeDtypeStruct((T, CONV_DIM), BF16),
        compiler_params=_cp(("arbitrary",)),
    )(dc, dc, w)


def _ssd_common(sm_ref, alog_ref):
    lane = lax.broadcasted_iota(jnp.int32, (1, CH), 1)
    A = jnp.where(lane < SSD_H, -jnp.exp(alog_ref[...]), 0.0)
    dt = sm_ref[...]
    adt = dt * A
    r = lax.broadcasted_iota(jnp.int32, (CH, CH), 0)
    c = lax.broadcasted_iota(jnp.int32, (CH, CH), 1)
    tril = (r >= c).astype(F32)
    cs = jnp.dot(tril, adt, precision=HI, preferred_element_type=F32)
    csT = cs.T
    cs_last = jnp.sum(jnp.where(r == CH - 1, cs, 0.0), axis=0, keepdims=True)
    return A, dt, cs, csT, cs_last, tril, r, c


def _nt(a, b):
    return lax.dot_general(a, b, (((1,), (1,)), ((), ())), preferred_element_type=F32)


def _tn(a, b):
    return lax.dot_general(a, b, (((0,), (0,)), ((), ())), preferred_element_type=F32)


def _nn(a, b):
    return jnp.dot(a, b, preferred_element_type=F32)


def _ssd_fwd(xbc, sm, alog, *, B, NC):
    T = B * NC * CH

    def body(x_ref, sm_ref, alog_ref, y_ref, st_ref, S):
        cidx = pl.program_id(1)

        @pl.when(cidx == 0)
        def _():
            S[...] = jnp.zeros_like(S)

        st_ref[0] = S[...]
        A, dt, cs, csT, cs_last, tril, _, _ = _ssd_common(sm_ref, alog_ref)
        for g in range(SSD_G):
            Bg = x_ref[:, D + g * SSD_N:D + (g + 1) * SSD_N]
            Cg = x_ref[:, D + SSD_G * SSD_N + g * SSD_N:D + SSD_G * SSD_N + (g + 1) * SSD_N]
            Cb = Cg.astype(BF16)
            G = _nt(Cb, Bg.astype(BF16))
            for rr in range(SSD_H // SSD_G):
                h = g * (SSD_H // SSD_G) + rr
                col = cs[:, h:h + 1]
                row = csT[h:h + 1, :]
                Ld = jnp.where(tril > 0, jnp.exp(jnp.minimum(col - row, 0.0)), 0.0)
                M = (G * Ld).astype(BF16)
                xdt = (x_ref[:, h * SSD_P:(h + 1) * SSD_P] * dt[:, h:h + 1]).astype(BF16)
                ST = S[h]
                y = _nn(M, xdt) + jnp.exp(col) * _nt(Cb, ST.astype(BF16))
                y_ref[:, h * SSD_P:(h + 1) * SSD_P] = y
                cl = cs_last[:, h:h + 1]
                Bd = (Bg * jnp.exp(cl - col)).astype(BF16)
                S[h] = jnp.exp(cl) * ST + _tn(xdt, Bd)

    return pl.pallas_call(
        body, name="ssd_fwd", grid=(B, NC),
        in_specs=[pl.BlockSpec((CH, CONV_DIM), lambda b, c: (b * NC + c, 0)),
                  pl.BlockSpec((CH, CH), lambda b, c: (b * NC + c, 0)),
                  pl.BlockSpec((1, CH), lambda b, c: (0, 0))],
        out_specs=[pl.BlockSpec((CH, D), lambda b, c: (b * NC + c, 0)),
                   pl.BlockSpec((1, SSD_H, SSD_P, SSD_N), lambda b, c: (b * NC + c, 0, 0, 0))],
        out_shape=[jax.ShapeDtypeStruct((T, D), F32),
                   jax.ShapeDtypeStruct((B * NC, SSD_H, SSD_P, SSD_N), F32)],
        scratch_shapes=[pltpu.VMEM((SSD_H, SSD_P, SSD_N), F32)],
        compiler_params=_cp(("arbitrary", "arbitrary")),
    )(xbc, sm, alog)


def _ssd_bwd(xbc, sm, alog, states, dy, dxs_skip, *, B, NC):
    T = B * NC * CH

    def rix(b, c):
        return b * NC + (NC - 1 - c)

    def body(x_ref, sm_ref, alog_ref, st_ref, dy_ref, sk_ref, dx_ref, ddt_ref, dal_ref, dS):
        bidx = pl.program_id(0)
        cidx = pl.program_id(1)

        @pl.when(cidx == 0)
        def _():
            dS[...] = jnp.zeros_like(dS)

        A, dt, cs, csT, cs_last, tril, r, c = _ssd_common(sm_ref, alog_ref)
        lane = lax.broadcasted_iota(jnp.int32, (1, CH), 1)
        DCcol = jnp.zeros((CH, CH), F32)
        DCrow = jnp.zeros((CH, CH), F32)
        DX = jnp.zeros((CH, CH), F32)
        dlast = jnp.zeros((1, CH), F32)
        for g in range(SSD_G):
            ob = D + g * SSD_N
            oc = D + SSD_G * SSD_N + g * SSD_N
            Bg = x_ref[:, ob:ob + SSD_N]
            Cg = x_ref[:, oc:oc + SSD_N]
            Bb = Bg.astype(BF16)
            Cb = Cg.astype(BF16)
            G = _nt(Cb, Bb)
            dG = jnp.zeros((CH, CH), F32)
            dBg = jnp.zeros((CH, SSD_N), F32)
            dCg = jnp.zeros((CH, SSD_N), F32)
            for rr in range(SSD_H // SSD_G):
                h = g * (SSD_H // SSD_G) + rr
                sl = slice(h * SSD_P, (h + 1) * SSD_P)
                col = cs[:, h:h + 1]
                row = csT[h:h + 1, :]
                Ld = jnp.where(tril > 0, jnp.exp(jnp.minimum(col - row, 0.0)), 0.0)
                Mf = G * Ld
                dth = dt[:, h:h + 1]
                xs_h = x_ref[:, sl]
                xdt = (xs_h * dth).astype(BF16)
                ST = st_ref[0, h]
                STb = ST.astype(BF16)
                dST = dS[h]
                dSTb = dST.astype(BF16)
                dyh = dy_ref[:, sl]
                dyb = dyh.astype(BF16)
                E = jnp.exp(col)
                yo = _nt(Cb, STb)
                dxdt = _tn(Mf.astype(BF16), dyb)
                dM = _nt(dyb, xdt)
                dG = dG + dM * Ld
                W = dM * Mf
                dcol = jnp.sum(W, axis=1, keepdims=True) + jnp.sum(dyh * yo, axis=1, keepdims=True) * E
                drow = -jnp.sum(W, axis=0, keepdims=True)
                dyE = (dyh * E).astype(BF16)
                dCg = dCg + _nn(dyE, STb)
                dS_in = _tn(dyE, Cb)
                cl = cs_last[:, h:h + 1]
                decay = jnp.exp(cl - col)
                Bd = (Bg * decay).astype(BF16)
                dxdt = dxdt + _nt(Bd, dSTb)
                dBd = _nn(xdt, dSTb)
                dBg = dBg + decay * dBd
                dd = jnp.sum(dBd * Bg, axis=1, keepdims=True) * decay
                dcol = dcol - dd
                el = jnp.exp(cl)
                dl = jnp.sum(dd, axis=0, keepdims=True) + el * jnp.sum(
                    jnp.sum(dST * ST, axis=1, keepdims=True), axis=0, keepdims=True)
                dS[h] = dS_in + el * dST
                dx_ref[:, sl] = dxdt * dth + sk_ref[:, sl]
                ddt_x = jnp.sum(dxdt * xs_h, axis=1, keepdims=True)
                DCcol = DCcol + jnp.where(c == h, dcol, 0.0)
                DCrow = DCrow + jnp.where(r == h, drow, 0.0)
                DX = DX + jnp.where(c == h, ddt_x, 0.0)
                dlast = dlast + jnp.where(lane == h, dl, 0.0)
            dGb = dG.astype(BF16)
            dx_ref[:, ob:ob + SSD_N] = dBg + _tn(dGb, Cb)
            dx_ref[:, oc:oc + SSD_N] = dCg + _nn(dGb, Bb)
        DC = DCcol + DCrow.T + jnp.where(r == CH - 1, dlast, 0.0)
        triu = (r <= c).astype(F32)
        dadt = jnp.dot(triu, DC, precision=HI, preferred_element_type=F32)
        ddt_ref[...] = dadt * A + DX
        dal = jnp.sum(dadt * dt, axis=0, keepdims=True) * A

        @pl.when((bidx == 0) & (cidx == 0))
        def _():
            dal_ref[...] = dal

        @pl.when((bidx > 0) | (cidx > 0))
        def _():
            dal_ref[...] += dal

    return pl.pallas_call(
        body, name="ssd_bwd", grid=(B, NC),
        in_specs=[pl.BlockSpec((CH, CONV_DIM), lambda b, c: (rix(b, c), 0)),
                  pl.BlockSpec((CH, CH), lambda b, c: (rix(b, c), 0)),
                  pl.BlockSpec((1, CH), lambda b, c: (0, 0)),
                  pl.BlockSpec((1, SSD_H, SSD_P, SSD_N), lambda b, c: (rix(b, c), 0, 0, 0)),
                  pl.BlockSpec((CH, D), lambda b, c: (rix(b, c), 0)),
                  pl.BlockSpec((CH, D), lambda b, c: (rix(b, c), 0))],
        out_specs=[pl.BlockSpec((CH, CONV_DIM), lambda b, c: (rix(b, c), 0)),
                   pl.BlockSpec((CH, CH), lambda b, c: (rix(b, c), 0)),
                   pl.BlockSpec((1, CH), lambda b, c: (0, 0))],
        out_shape=[jax.ShapeDtypeStruct((T, CONV_DIM), F32),
                   jax.ShapeDtypeStruct((T, CH), F32),
                   jax.ShapeDtypeStruct((1, CH), F32)],
        scratch_shapes=[pltpu.VMEM((SSD_H, SSD_P, SSD_N), F32)],
        compiler_params=_cp(("arbitrary", "arbitrary")),
    )(xbc, sm, alog, states, dy, dxs_skip)


def _cumsum_seq(v, *, B, NC, reverse, name):
    T = B * NC * CH

    def ix(b, c):
        return b * NC + ((NC - 1 - c) if reverse else c)

    def body(v_ref, o_ref, carry):
        cidx = pl.program_id(1)

        @pl.when(cidx == 0)
        def _():
            carry[...] = jnp.zeros_like(carry)

        r = lax.broadcasted_iota(jnp.int32, (CH, CH), 0)
        c = lax.broadcasted_iota(jnp.int32, (CH, CH), 1)
        tri = ((r <= c) if reverse else (r >= c)).astype(F32)
        cs = jnp.dot(tri, v_ref[...], precision=HI, preferred_element_type=F32) + carry[...]
        o_ref[...] = cs
        edge = 0 if reverse else CH - 1
        carry[...] = jnp.sum(jnp.where(r == edge, cs, 0.0), axis=0, keepdims=True)

    return pl.pallas_call(
        body, name=name, grid=(B, NC),
        in_specs=[pl.BlockSpec((CH, CH), lambda b, c: (ix(b, c), 0))],
        out_specs=pl.BlockSpec((CH, CH), lambda b, c: (ix(b, c), 0)),
        out_shape=jax.ShapeDtypeStruct((T, CH), F32),
        scratch_shapes=[pltpu.VMEM((1, CH), F32)],
        compiler_params=_cp(("arbitrary", "arbitrary")),
    )(v)


def _fox_tb(Lp):
    return 384 if (Lp % 384 == 0 and Lp > 384) else CH


def _fox_col(cum, h):
    lane = lax.broadcasted_iota(jnp.int32, cum.shape, 1)
    return jnp.sum(jnp.where(lane == SSD_H + h, cum, 0.0), axis=1, keepdims=True)


def _fox_scores(q, k, cq, ck, qpos, kpos, pad):
    s = _nt(q, k) * (FOX_DH ** -0.5) + (cq - ck)
    mask = (kpos <= qpos) & (kpos >= pad)
    return s, mask


def _fox_fwd(qkv, cum, cumT, *, B, Lp, pad):
    TB = _fox_tb(Lp)
    NQ = Lp // TB
    T = B * Lp

    def body(q_ref, k_ref, v_ref, cum_ref, ct_ref, o_ref, lse_ref):
        h = pl.program_id(1)
        i = pl.program_id(2)
        q = q_ref[...]
        cq = _fox_col(cum_ref[...], h)
        qpos = i * TB + lax.broadcasted_iota(jnp.int32, (TB, 1), 0)

        def step(j, carry):
            m, l, acc = carry
            off = pl.multiple_of(j * TB, TB)
            k = k_ref[pl.ds(off, TB), :]
            v = v_ref[pl.ds(off, TB), :]
            ck = ct_ref[0, :, pl.ds(off, TB)]
            kpos = j * TB + lax.broadcasted_iota(jnp.int32, (1, TB), 1)
            s, mask = _fox_scores(q, k, cq, ck, qpos, kpos, pad)
            s = jnp.where(mask, s, NEG)
            m_new = jnp.maximum(m, jnp.max(s, axis=1, keepdims=True))
            p = jnp.exp(s - m_new)
            alpha = jnp.exp(m - m_new)
            l = alpha * l + jnp.sum(p, axis=1, keepdims=True)
            acc = alpha * acc + _nn(p.astype(BF16), v)
            return m_new, l, acc

        init = (jnp.full((TB, 1), NEG, F32), jnp.zeros((TB, 1), F32), jnp.zeros((TB, FOX_DH), F32))
        m, l, acc = lax.fori_loop(0, i + 1, step, init)
        o_ref[...] = (acc / l).astype(o_ref.dtype)
        lse_ref[0, 0] = m + jnp.log(l)

    return pl.pallas_call(
        body, name="fox_fwd", grid=(B, FOX_H, NQ),
        in_specs=[pl.BlockSpec((TB, FOX_DH), lambda b, h, i: (b * NQ + i, h)),
                  pl.BlockSpec((Lp, FOX_DH), lambda b, h, i: (b, FOX_H + h)),
                  pl.BlockSpec((Lp, FOX_DH), lambda b, h, i: (b, 2 * FOX_H + h)),
                  pl.BlockSpec((TB, CH), lambda b, h, i: (b * NQ + i, 0)),
                  pl.BlockSpec((1, 1, Lp), lambda b, h, i: (b * FOX_H + h, 0, 0))],
        out_specs=[pl.BlockSpec((TB, FOX_DH), lambda b, h, i: (b * NQ + i, h)),
                   pl.BlockSpec((1, 1, TB, 1), lambda b, h, i: (b, h, i, 0))],
        out_shape=[jax.ShapeDtypeStruct((T, D), BF16),
                   jax.ShapeDtypeStruct((B, FOX_H, Lp, 1), F32)],
        compiler_params=_cp(("arbitrary", "arbitrary", "arbitrary")),
    )(qkv, qkv, qkv, cum, cumT)


def _fox_bwd_q(qkv, dy, o, lse, cum, cumT, *, B, Lp, pad):
    TB = _fox_tb(Lp)
    NQ = Lp // TB
    T = B * Lp

    def body(q_ref, k_ref, v_ref, dy_ref, o_ref, lse_ref, cum_ref, ct_ref, dq_ref, dcq_ref, dl_ref):
        h = pl.program_id(1)
        i = pl.program_id(2)
        q = q_ref[...]
        do = dy_ref[...]
        dob = do.astype(BF16)
        delta = jnp.sum(do * o_ref[...].astype(F32), axis=1, keepdims=True)
        lse = lse_ref[0, 0]
        cq = _fox_col(cum_ref[...], h)
        qpos = i * TB + lax.broadcasted_iota(jnp.int32, (TB, 1), 0)

        def step(j, carry):
            dq, dcq = carry
            off = pl.multiple_of(j * TB, TB)
            k = k_ref[pl.ds(off, TB), :]
            v = v_ref[pl.ds(off, TB), :]
            ck = ct_ref[0, :, pl.ds(off, TB)]
            kpos = j * TB + lax.broadcasted_iota(jnp.int32, (1, TB), 1)
            s, mask = _fox_scores(q, k, cq, ck, qpos, kpos, pad)
            p = jnp.where(mask, jnp.exp(jnp.minimum(s - lse, 0.0)), 0.0)
            ds = p * (_nt(dob, v) - delta)
            dq = dq + _nn(ds.astype(BF16), k)
            dcq = dcq + jnp.sum(ds, axis=1, keepdims=True)
            return dq, dcq

        dq, dcq = lax.fori_loop(0, i + 1, step, (jnp.zeros((TB, FOX_DH), F32), jnp.zeros((TB, 1), F32)))
        dq_ref[...] = (dq * (FOX_DH ** -0.5)).astype(dq_ref.dtype)
        dcq_ref[0, 0] = dcq
        dl_ref[0, 0] = delta

    col4 = pl.BlockSpec((1, 1, TB, 1), lambda b, h, i: (b, h, i, 0))
    return pl.pallas_call(
        body, name="fox_bwd_q", grid=(B, FOX_H, NQ),
        in_specs=[pl.BlockSpec((TB, FOX_DH), lambda b, h, i: (b * NQ + i, h)),
                  pl.BlockSpec((Lp, FOX_DH), lambda b, h, i: (b, FOX_H + h)),
                  pl.BlockSpec((Lp, FOX_DH), lambda b, h, i: (b, 2 * FOX_H + h)),
                  pl.BlockSpec((TB, FOX_DH), lambda b, h, i: (b * NQ + i, h)),
                  pl.BlockSpec((TB, FOX_DH), lambda b, h, i: (b * NQ + i, h)),
                  col4,
                  pl.BlockSpec((TB, CH), lambda b, h, i: (b * NQ + i, 0)),
                  pl.BlockSpec((1, 1, Lp), lambda b, h, i: (b * FOX_H + h, 0, 0))],
        out_specs=[pl.BlockSpec((TB, FOX_DH), lambda b, h, i: (b * NQ + i, h)), col4, col4],
        out_shape=[jax.ShapeDtypeStruct((T, D), BF16),
                   jax.ShapeDtypeStruct((B, FOX_H, Lp, 1), F32),
                   jax.ShapeDtypeStruct((B, FOX_H, Lp, 1), F32)],
        compiler_params=_cp(("arbitrary", "arbitrary", "arbitrary")),
    )(qkv, qkv, qkv, dy, o, lse, cum, cumT)


def _fox_bwd_kv(qkv, dy, lse, delta, cum, cumT, *, B, Lp, pad):
    TB = _fox_tb(Lp)
    NQ = Lp // TB
    T = B * Lp

    def body(q_ref, k_ref, v_ref, dy_ref, lse_ref, dl_ref, cum_ref, ct_ref, dk_ref, dv_ref, dck_ref):
        h = pl.program_id(1)
        j = pl.program_id(2)
        k = k_ref[...]
        v = v_ref[...]
        ck = ct_ref[0]
        kpos = j * TB + lax.broadcasted_iota(jnp.int32, (1, TB), 1)

        def step(i, carry):
            dk, dv, dck = carry
            off = pl.multiple_of(i * TB, TB)
            q = q_ref[pl.ds(off, TB), :]
            dob = dy_ref[pl.ds(off, TB), :].astype(BF16)
            lse = lse_ref[0, 0, pl.ds(off, TB), :]
            delta = dl_ref[0, 0, pl.ds(off, TB), :]
            cq = _fox_col(cum_ref[pl.ds(off, TB), :], h)
            qpos = i * TB + lax.broadcasted_iota(jnp.int32, (TB, 1), 0)
            s, mask = _fox_scores(q, k, cq, ck, qpos, kpos, pad)
            p = jnp.where(mask, jnp.exp(jnp.minimum(s - lse, 0.0)), 0.0)
            dv = dv + _tn(p.astype(BF16), dob)
            ds = p * (_nt(dob, v) - delta)
            dk = dk + _tn(ds.astype(BF16), q)
            dck = dck - jnp.sum(ds, axis=0, keepdims=True)
            return dk, dv, dck

        z = jnp.zeros((TB, FOX_DH), F32)
        dk, dv, dck = lax.fori_loop(j, NQ, step, (z, z, jnp.zeros((1, TB), F32)))
        dk_ref[...] = (dk * (FOX_DH ** -0.5)).astype(dk_ref.dtype)
        dv_ref[...] = dv.astype(dv_ref.dtype)
        dck_ref[0] = dck

    full4 = pl.BlockSpec((1, 1, Lp, 1), lambda b, h, j: (b, h, 0, 0))
    return pl.pallas_call(
        body, name="fox_bwd_kv", grid=(B, FOX_H, NQ),
        in_specs=[pl.BlockSpec((Lp, FOX_DH), lambda b, h, j: (b, h)),
                  pl.BlockSpec((TB, FOX_DH), lambda b, h, j: (b * NQ + j, FOX_H + h)),
                  pl.BlockSpec((TB, FOX_DH), lambda b, h, j: (b * NQ + j, 2 * FOX_H + h)),
                  pl.BlockSpec((Lp, FOX_DH), lambda b, h, j: (b, h)),
                  full4, full4,
                  pl.BlockSpec((Lp, CH), lambda b, h, j: (b, 0)),
                  pl.BlockSpec((1, 1, TB), lambda b, h, j: (b * FOX_H + h, 0, j))],
        out_specs=[pl.BlockSpec((TB, FOX_DH), lambda b, h, j: (b * NQ + j, h)),
                   pl.BlockSpec((TB, FOX_DH), lambda b, h, j: (b * NQ + j, h)),
                   pl.BlockSpec((1, 1, TB), lambda b, h, j: (b * FOX_H + h, 0, j))],
        out_shape=[jax.ShapeDtypeStruct((T, D), BF16),
                   jax.ShapeDtypeStruct((T, D), BF16),
                   jax.ShapeDtypeStruct((B * FOX_H, 1, Lp), F32)],
        compiler_params=_cp(("arbitrary", "arbitrary", "arbitrary")),
    )(qkv, qkv, qkv, dy, lse, delta, cum, cumT)


def _s5_pows(lam_ref, pw, reverse):
    ar = lam_ref[0, :, 0:S5_W]
    ai = lam_ref[0, :, S5_W:2 * S5_W]
    if reverse:
        ai = -ai
    for k in range(8):
        pw[k:k + 1, 0:S5_W] = ar
        pw[k:k + 1, S5_W:2 * S5_W] = ai
        ar, ai = ar * ar - ai * ai, 2.0 * ar * ai


def _s5_scan(vr, vi, pw, reverse):
    n = vr.shape[0]
    row = lax.broadcasted_iota(jnp.int32, (n, 1), 0)
    s, k = 1, 0
    while s < n:
        if reverse:
            keep = row < n - s
            sr = jnp.where(keep, pltpu.roll(vr, n - s, 0), 0.0)
            si = jnp.where(keep, pltpu.roll(vi, n - s, 0), 0.0)
        else:
            keep = row >= s
            sr = jnp.where(keep, pltpu.roll(vr, s, 0), 0.0)
            si = jnp.where(keep, pltpu.roll(vi, s, 0), 0.0)
        ar = pw[k:k + 1, 0:S5_W]
        ai = pw[k:k + 1, S5_W:2 * S5_W]
        vr, vi = vr + ar * sr - ai * si, vi + ar * si + ai * sr
        s *= 2
        k += 1
    return vr, vi


def _s5_table(pw, tab, reverse):
    n = tab.shape[0]
    row = lax.broadcasted_iota(jnp.int32, (n, 1), 0)
    edge = (n - 1) if reverse else 0
    vr = jnp.where(row == edge, pw[0:1, 0:S5_W], 0.0)
    vi = jnp.where(row == edge, pw[0:1, S5_W:2 * S5_W], 0.0)
    vr, vi = _s5_scan(vr, vi, pw, reverse)
    tab[:, 0:S5_W] = vr
    tab[:, S5_W:2 * S5_W] = vi


def _s5_fwd(u, Bsg, Csg, lam, *, B, NC):
    T = B * NC * CH

    def body(u_ref, b_ref, c_ref, lam_ref, y_ref, h_ref, pw, tab, hs, carry):
        cidx = pl.program_id(2)

        @pl.when(cidx == 0)
        def _():
            _s5_pows(lam_ref, pw, False)
            _s5_table(pw, tab, False)
            carry[...] = jnp.zeros_like(carry)

        bu = _nn(u_ref[...].astype(BF16), b_ref[0])
        vr, vi = _s5_scan(bu[:, 0:S5_W], bu[:, S5_W:], pw, False)
        cr = carry[:, 0:S5_W]
        ci = carry[:, S5_W:]
        tr = tab[:, 0:S5_W]
        ti = tab[:, S5_W:]
        hs[:, 0:S5_W] = vr + tr * cr - ti * ci
        hs[:, S5_W:] = vi + tr * ci + ti * cr
        carry[...] = hs[CH - 1:CH, :]
        hb = hs[...].astype(BF16)
        h_ref[...] = hb
        y_ref[...] = _nn(hb, c_ref[0])

    return pl.pallas_call(
        body, name="s5_fwd", grid=(B, S5_G // S5_SG, NC),
        in_specs=[pl.BlockSpec((CH, CH), lambda b, s, c: (b * NC + c, s)),
                  pl.BlockSpec((1, CH, 2 * S5_W), lambda b, s, c: (s, 0, 0)),
                  pl.BlockSpec((1, 2 * S5_W, CH), lambda b, s, c: (s, 0, 0)),
                  pl.BlockSpec((1, 1, 2 * S5_W), lambda b, s, c: (s, 0, 0))],
        out_specs=[pl.BlockSpec((CH, CH), lambda b, s, c: (b * NC + c, s)),
                   pl.BlockSpec((CH, 2 * S5_W), lambda b, s, c: (b * NC + c, s))],
        out_shape=[jax.ShapeDtypeStruct((T, D), F32),
                   jax.ShapeDtypeStruct((T, (S5_G // S5_SG) * 2 * S5_W), BF16)],
        scratch_shapes=[pltpu.VMEM((8, 2 * S5_W), F32), pltpu.VMEM((CH, 2 * S5_W), F32),
                        pltpu.VMEM((CH, 2 * S5_W), F32), pltpu.VMEM((1, 2 * S5_W), F32)],
        compiler_params=_cp(("arbitrary", "arbitrary", "arbitrary")),
    )(u, Bsg, Csg, lam)


def _s5_bwd(u, hst, dy, du_skip, Bsg, Csg, lam, *, B, NC):
    T = B * NC * CH
    NS = S5_G // S5_SG
    hb16 = CH // 16

    def rix(b, c):
        return b * NC + (NC - 1 - c)

    def body(u_ref, h_ref, hp_ref, dy_ref, sk_ref, b_ref, c_ref, lam_ref,
             du_ref, db_ref, dc_ref, dl_ref, pw, tab, gs, carry):
        bidx = pl.program_id(1)
        cidx = pl.program_id(2)
        first = (bidx == 0) & (cidx == 0)

        @pl.when(cidx == 0)
        def _():
            _s5_pows(lam_ref, pw, True)
            _s5_table(pw, tab, True)
            carry[...] = jnp.zeros_like(carry)

        dyb = dy_ref[...].astype(BF16)
        w = _nt(dyb, c_ref[0])
        vr, vi = _s5_scan(w[:, 0:S5_W], w[:, S5_W:], pw, True)
        cr = carry[:, 0:S5_W]
        ci = carry[:, S5_W:]
        tr = tab[:, 0:S5_W]
        ti = tab[:, S5_W:]
        gr = vr + tr * cr - ti * ci
        gi = vi + tr * ci + ti * cr
        gs[:, 0:S5_W] = gr
        gs[:, S5_W:] = gi
        carry[...] = gs[0:1, :]
        gb = gs[...].astype(BF16)
        du_ref[...] = (_nt(gb, b_ref[0]) + sk_ref[...]).astype(du_ref.dtype)
        ub = u_ref[...].astype(BF16)
        hcur = h_ref[...]
        dB = _tn(ub, gb)
        dC = _tn(hcur, dyb)
        hf = hcur.astype(F32)
        row = lax.broadcasted_iota(jnp.int32, (CH, 1), 0)
        prev_last = jnp.where(cidx < NC - 1, hp_ref[15:16, :].astype(F32), 0.0)
        hprev = jnp.where(row == 0, prev_last, pltpu.roll(hf, 1, 0))
        pr = hprev[:, 0:S5_W]
        pi = hprev[:, S5_W:]
        da = _colsum(gr * pr + gi * pi)
        dbb = _colsum(gi * pr - gr * pi)
        dl = jnp.concatenate([da, dbb], axis=1)

        @pl.when(first)
        def _():
            db_ref[0] = dB
            dc_ref[0] = dC
            dl_ref[0] = dl

        @pl.when(jnp.logical_not(first))
        def _():
            db_ref[0] += dB
            dc_ref[0] += dC
            dl_ref[0] += dl

    return pl.pallas_call(
        body, name="s5_bwd", grid=(NS, B, NC),
        in_specs=[pl.BlockSpec((CH, CH), lambda s, b, c: (rix(b, c), s)),
                  pl.BlockSpec((CH, 2 * S5_W), lambda s, b, c: (rix(b, c), s)),
                  pl.BlockSpec((16, 2 * S5_W), lambda s, b, c: (jnp.maximum(rix(b, c) * hb16 - 1, 0), s)),
                  pl.BlockSpec((CH, CH), lambda s, b, c: (rix(b, c), s)),
                  pl.BlockSpec((CH, CH), lambda s, b, c: (rix(b, c), s)),
                  pl.BlockSpec((1, CH, 2 * S5_W), lambda s, b, c: (s, 0, 0)),
                  pl.BlockSpec((1, 2 * S5_W, CH), lambda s, b, c: (s, 0, 0)),
                  pl.BlockSpec((1, 1, 2 * S5_W), lambda s, b, c: (s, 0, 0))],
        out_specs=[pl.BlockSpec((CH, CH), lambda s, b, c: (rix(b, c), s)),
                   pl.BlockSpec((1, CH, 2 * S5_W), lambda s, b, c: (s, 0, 0)),
                   pl.BlockSpec((1, 2 * S5_W, CH), lambda s, b, c: (s, 0, 0)),
                   pl.BlockSpec((1, 1, 2 * S5_W), lambda s, b, c: (s, 0, 0))],
        out_shape=[jax.ShapeDtypeStruct((T, D), BF16),
                   jax.ShapeDtypeStruct((NS, CH, 2 * S5_W), F32),
                   jax.ShapeDtypeStruct((NS, 2 * S5_W, CH), F32),
                   jax.ShapeDtypeStruct((NS, 1, 2 * S5_W), F32)],
        scratch_shapes=[pltpu.VMEM((8, 2 * S5_W), F32), pltpu.VMEM((CH, 2 * S5_W), F32),
                        pltpu.VMEM((CH, 2 * S5_W), F32), pltpu.VMEM((1, 2 * S5_W), F32)],
        compiler_params=_cp(("arbitrary", "arbitrary", "arbitrary")),
    )(u, hst, hst, dy, du_skip, Bsg, Csg, lam)


def _s5_param_fn(lre, lim, lstep, bre, bim):
    step = jnp.exp(lstep)
    zr = lre * step
    zi = lim * step
    e = jnp.exp(zr)
    a = e * jnp.cos(zi)
    b = e * jnp.sin(zi)
    den = lre * lre + lim * lim
    qr = ((a - 1.0) * lre + b * lim) / den
    qi = (b * lre - (a - 1.0) * lim) / den
    return a, b, qr[None] * bre - qi[None] * bim, qr[None] * bim + qi[None] * bre


_S5_ROWS = S5_G * S5_P // CH


def _s5_tile(v):
    return v.reshape(_S5_ROWS, CH)


def _s5_tile_b(v):
    return v.reshape(S5_G * S5_P, S5_C).T.reshape(S5_C, _S5_ROWS, CH)


def _s5_untile_b(v):
    return v.reshape(S5_C, S5_G * S5_P).T.reshape(S5_G, S5_P, S5_C)


def _s5_params(lre, lim, lstep, bre, bim):
    def body(a_ref, b_ref, c_ref, d_ref, e_ref, o1, o2, o3, o4):
        outs = _s5_param_fn(a_ref[...], b_ref[...], c_ref[...], d_ref[...], e_ref[...])
        for o, v in zip((o1, o2, o3, o4), outs):
            o[...] = v

    shp = [jax.ShapeDtypeStruct(lre.shape, F32)] * 2 + [jax.ShapeDtypeStruct(bre.shape, F32)] * 2
    return pl.pallas_call(body, name="s5_params", out_shape=shp, compiler_params=_cp())(lre, lim, lstep, bre, bim)


def _s5_params_bwd(lre, lim, lstep, bre, bim, da, db, dbr, dbi):
    def body(a_ref, b_ref, c_ref, d_ref, e_ref, g1, g2, g3, g4, o1, o2, o3, o4, o5):
        _, vjp = jax.vjp(_s5_param_fn, a_ref[...], b_ref[...], c_ref[...], d_ref[...], e_ref[...])
        outs = vjp((g1[...], g2[...], g3[...], g4[...]))
        for o, v in zip((o1, o2, o3, o4, o5), outs):
            o[...] = v

    shp = [jax.ShapeDtypeStruct(lre.shape, F32)] * 3 + [jax.ShapeDtypeStruct(bre.shape, F32)] * 2
    return pl.pallas_call(body, name="s5_params_bwd", out_shape=shp, compiler_params=_cp())(
        lre, lim, lstep, bre, bim, da, db, dbr, dbi)


def _s5_blockdiag(br, bi, cre, cim):
    NS = S5_G // S5_SG
    eye = jnp.eye(S5_SG, dtype=F32)

    def bmat(v):
        v = v.reshape(NS, S5_SG, S5_P, S5_C)
        m = jnp.einsum("sgpc,gh->sgchp", v, eye)
        return m.reshape(NS, S5_SG * S5_C, S5_SG * S5_P)

    def cmat(v):
        v = v.reshape(NS, S5_SG, S5_C, S5_P)
        m = jnp.einsum("sgcp,gh->sgphc", v, eye)
        return m.reshape(NS, S5_SG * S5_P, S5_SG * S5_C)

    Bsg = jnp.concatenate([bmat(br), bmat(bi)], axis=2).astype(BF16)
    Csg = jnp.concatenate([cmat(cre), cmat(-cim)], axis=1).astype(BF16)
    return Bsg, Csg


def _s5_unblock(dBsg, dCsg):
    NS = S5_G // S5_SG

    def ub(m):
        m = m.reshape(NS, S5_SG, S5_C, S5_SG, S5_P)
        d = jnp.diagonal(m, axis1=1, axis2=3)
        return d.transpose(0, 3, 2, 1).reshape(S5_G, S5_P, S5_C)

    def uc(m):
        m = m.reshape(NS, S5_SG, S5_P, S5_SG, S5_C)
        d = jnp.diagonal(m, axis1=1, axis2=3)
        return d.transpose(0, 3, 2, 1).reshape(S5_G, S5_C, S5_P)

    dbr = ub(dBsg[:, :, 0:S5_W])
    dbi = ub(dBsg[:, :, S5_W:])
    dcr = uc(dCsg[:, 0:S5_W, :])
    dci = -uc(dCsg[:, S5_W:, :])
    return dbr, dbi, dcr, dci


def _loss_head(x, nf, target, *, B, NC, S):
    T = B * NC * CH
    nts = S // CH

    def f(xv, w, t):
        y = _rms(xv, w)
        return 0.5 * _colsum(jnp.mean(jnp.square(y - t), axis=-1, keepdims=True))

    def body(x_ref, w_ref, t_ref, dx_ref, ls_ref, dw_ref):
        i = pl.program_id(0)
        on = (i % NC) > 0
        t = t_ref[...]
        l, vjp = jax.vjp(lambda a, b: f(a, b, t), x_ref[...], w_ref[...])
        dx, dw = vjp(jnp.ones((1, 1), F32))
        g = jnp.where(on, 1.0, 0.0)
        dx_ref[...] = dx * g
        lv = jnp.zeros((1, CH), F32) + l * g

        @pl.when(i == 0)
        def _():
            ls_ref[...] = lv
            dw_ref[...] = dw * g

        @pl.when(i > 0)
        def _():
            ls_ref[...] += lv
            dw_ref[...] += dw * g

    def tix(i):
        return ((i // NC) * nts + jnp.maximum(i % NC - 1, 0), 0)

    return pl.pallas_call(
        body, name="loss_head", grid=(B * NC,),
        in_specs=[pl.BlockSpec((CH, D), lambda i: (i, 0)),
                  pl.BlockSpec((1, D), lambda i: (0, 0)),
                  pl.BlockSpec((CH, D), tix)],
        out_specs=[pl.BlockSpec((CH, D), lambda i: (i, 0)),
                   pl.BlockSpec((1, CH), lambda i: (0, 0)),
                   pl.BlockSpec((1, D), lambda i: (0, 0))],
        out_shape=[jax.ShapeDtypeStruct((T, D), F32),
                   jax.ShapeDtypeStruct((1, CH), F32),
                   jax.ShapeDtypeStruct((1, D), F32)],
        compiler_params=_cp(("arbitrary",)),
    )(x, nf, target)


def _ew(fn, ins, n_out, out_dtypes, *, name, tile=None):
    R, C = ins[0].shape
    tile = tile or _pick(R, (512, 256, 128, 64, 32, 16, 8, 1))
    if tile % 8 != 0:
        tile = R

    def body(*refs):
        outs = fn(*[r[...] for r in refs[:len(ins)]])
        if not isinstance(outs, (tuple, list)):
            outs = (outs,)
        for r, v in zip(refs[len(ins):], outs):
            r[...] = v.astype(r.dtype)

    spec = pl.BlockSpec((tile, C), lambda i: (i, 0))
    res = pl.pallas_call(
        body, name=name, grid=(R // tile,), in_specs=[spec] * len(ins), out_specs=[spec] * n_out,
        out_shape=[jax.ShapeDtypeStruct((R, C), dt) for dt in out_dtypes],
        compiler_params=_cp(("parallel",)),
    )(*ins)
    return res


def _adam_fn(w, g, m, v):
    m = ADAM_B1 * m + (1.0 - ADAM_B1) * g
    v = ADAM_B2 * v + (1.0 - ADAM_B2) * jnp.square(g)
    m_hat = m / (1.0 - ADAM_B1 ** ADAM_STEP)
    v_hat = v / (1.0 - ADAM_B2 ** ADAM_STEP)
    delta = -ADAM_LR * (m_hat / (jnp.sqrt(v_hat) + ADAM_EPS) + ADAM_WD * w)
    return delta, m, v


def _adam(w, g, m, v, name):
    shp = w.shape
    C = shp[-1]
    f = lambda a: a.reshape(-1, C)
    d, nm, nv = _ew(_adam_fn, [f(w), f(g), f(m), f(v)], 3, [F32] * 3, name=name)
    return d.reshape(shp), nm.reshape(shp), nv.reshape(shp)


def _me():
    return lax.axis_index("x"), lax.axis_index("y"), lax.axis_index("c")


def _all_gather(v, name):
    R, C = v.shape

    def body(x_ref, out_ref, send_sems, recv_sems, local_sem):
        x, y, c = _me()
        me, sibling = (x, y, c), (x, y, 1 - c)
        chips = [(1 - x, y), (x, 1 - y), (1 - x, 1 - y)]

        def slot(px, py, pc):
            return out_ref.at[4 * px + 2 * py + pc]

        def copy(k, block, to, src=None):
            return pltpu.make_async_remote_copy(
                src_ref=slot(*block) if src is None else src, dst_ref=slot(*block),
                send_sem=send_sems.at[k], recv_sem=recv_sems.at[k], device_id=to, device_id_type=MESH)

        mine = pltpu.make_async_copy(x_ref, slot(*me), local_sem)
        mine.start()
        first = [copy(0, me, sibling, src=x_ref)]
        first += [copy(1 + j, me, (*chip, c), src=x_ref) for j, chip in enumerate(chips)]
        for cp in first:
            cp.start()
        passed = [copy(4 + j, (*chip, c), sibling) for j, chip in enumerate(chips)]
        for j, chip in enumerate(chips):
            copy(1 + j, (*chip, c), me).wait_recv()
            passed[j].start()
        copy(0, sibling, me).wait_recv()
        for j, chip in enumerate(chips):
            copy(4 + j, (*chip, 1 - c), me).wait_recv()
        for cp in first + passed:
            cp.wait_send()
        mine.wait()

    return pl.pallas_call(
        body, name=name, out_shape=jax.ShapeDtypeStruct((N_DEV, R, C), v.dtype),
        in_specs=[pl.BlockSpec(memory_space=pl.ANY)], out_specs=pl.BlockSpec(memory_space=pl.ANY),
        scratch_shapes=[pltpu.SemaphoreType.DMA((7,)), pltpu.SemaphoreType.DMA((7,)), pltpu.SemaphoreType.DMA],
    )(v)


def _swap_core(g, name):
    _, _, R, C = g.shape

    def body(g_ref, out_ref, send_sems, recv_sems):
        x, y, c = _me()
        cps = [pltpu.make_async_remote_copy(
            src_ref=g_ref.at[q, 1 - c], dst_ref=out_ref.at[q], send_sem=send_sems.at[q], recv_sem=recv_sems.at[q],
            device_id=(x, y, 1 - c), device_id_type=MESH) for q in range(4)]
        for cp in cps:
            cp.start()
        for cp in cps:
            cp.wait()

    return pl.pallas_call(
        body, name=name, out_shape=jax.ShapeDtypeStruct((4, R, C), g.dtype),
        in_specs=[pl.BlockSpec(memory_space=pl.ANY)], out_specs=pl.BlockSpec(memory_space=pl.ANY),
        scratch_shapes=[pltpu.SemaphoreType.DMA((4,)), pltpu.SemaphoreType.DMA((4,))],
    )(g)


def _swap_chips(hb, name):
    _, R, C = hb.shape
    flips = [(1, 0), (0, 1), (1, 1)]

    def body(h_ref, out_ref, send_sems, recv_sems):
        x, y, c = _me()
        cps = []
        for j, (fx, fy) in enumerate(flips):
            px = x + fx - 2 * x * fx
            py = y + fy - 2 * y * fy
            cps.append(pltpu.make_async_remote_copy(
                src_ref=h_ref.at[2 * px + py], dst_ref=out_ref.at[j], send_sem=send_sems.at[j],
                recv_sem=recv_sems.at[j], device_id=(px, py, c), device_id_type=MESH))
        for cp in cps:
            cp.start()
        for cp in cps:
            cp.wait()

    return pl.pallas_call(
        body, name=name, out_shape=jax.ShapeDtypeStruct((3, R, C), hb.dtype),
        in_specs=[pl.BlockSpec(memory_space=pl.ANY)], out_specs=pl.BlockSpec(memory_space=pl.ANY),
        scratch_shapes=[pltpu.SemaphoreType.DMA((3,)), pltpu.SemaphoreType.DMA((3,))],
    )(hb)


def _reduce_scatter(g8):
    _, R, C = g8.shape
    x, y, c = _me()
    g4 = g8.reshape(4, 2, R, C)
    got = _swap_core(g4, "rs_core")
    mine = lax.dynamic_index_in_dim(g4, c, axis=1, keepdims=False)
    h, hb = _ew(lambda a, b: (a + b, a + b), [mine.reshape(4 * R, C), got.reshape(4 * R, C)], 2, [F32, BF16],
                name="rs_pair_sum")
    own = lax.dynamic_index_in_dim(h.reshape(4, R, C), 2 * x + y, axis=0, keepdims=False)
    got3 = _swap_chips(hb.reshape(4, R, C), "rs_chips")
    out, = _ew(lambda a, b, c_, d: a + b.astype(F32) + c_.astype(F32) + d.astype(F32),
               [own, got3[0], got3[1], got3[2]], 1, [F32], name="rs_chip_sum")
    return out


def _sum8(a):
    out, = _ew(lambda *v: functools.reduce(lambda p, q: p + q, v), [a[k] for k in range(N_DEV)], 1, [F32],
               name="sum8")
    return out


def _pad_rows(flat, cols, mult):
    n = flat.shape[0]
    per = cols * mult
    tot = ((n + per - 1) // per) * per
    return jnp.pad(flat, (0, tot - n)).reshape(-1, cols)


class _Packer:
    def __init__(self, shapes, mult):
        self.shapes = shapes
        self.sizes = [int(np.prod(s)) for s in shapes]
        self.mult = mult

    def pack(self, arrs, dtype):
        flat = jnp.concatenate([a.reshape(-1).astype(dtype) for a in arrs])
        return _pad_rows(flat, CH, self.mult)

    def unpack(self, buf):
        flat = buf.reshape(-1)
        out, o = [], 0
        for s, n in zip(self.shapes, self.sizes):
            out.append(flat[o:o + n].reshape(s))
            o += n
        return out


def _w_cat(w_in_l):
    sm = jnp.concatenate([w_in_l[:, O_DT:O_DT + SSD_H], w_in_l[:, O_F:O_F + FOX_H],
                          jnp.zeros((D, CH - SSD_H - FOX_H), w_in_l.dtype)], axis=1)
    return jnp.concatenate([w_in_l[:, O_Z:O_XBC], w_in_l[:, O_XBC:O_DT], w_in_l[:, O_QKV:O_F],
                            w_in_l[:, O_U:O_G], w_in_l[:, O_G:D_IN], sm], axis=1)


def _w_uncat(g):
    return jnp.concatenate([g[:, OFF_Z:OFF_XBC], g[:, OFF_XBC:OFF_QKV], g[:, OFF_SM:OFF_SM + SSD_H],
                            g[:, OFF_QKV:OFF_U], g[:, OFF_SM + SSD_H:OFF_SM + SSD_H + FOX_H],
                            g[:, OFF_U:OFF_G], g[:, OFF_G:OFF_SM]], axis=1)


def _layer_fwd(x, p, geom, dims):
    B, NC, Lp, pad = dims
    T = geom[0]
    rm = functools.partial(_rowmap, geom=geom)
    sv = {}
    xn1, = rm(_f_norm, [x], [p["norm1"]], [(D, BF16)], [], tile=384 if Lp % 384 == 0 else CH, name="norm1")
    Wc = p["w_cat"]
    pz = _mm(xn1, Wc, "nn", F32, n=D, b_off=OFF_Z, name="in_z")
    pxbc = _mm(xn1, Wc, "nn", F32, n=CONV_DIM, b_off=OFF_XBC, name="in_xbc")
    qkv = _mm(xn1, Wc, "nn", BF16, n=3 * D, b_off=OFF_QKV, name="in_qkv")
    pu = _mm(xn1, Wc, "nn", F32, n=D, b_off=OFF_U, name="in_u")
    pg = _mm(xn1, Wc, "nn", F32, n=3 * D, b_off=OFF_G, name="in_g")
    psm = _mm(xn1, Wc, "nn", F32, n=CH, b_off=OFF_SM, name="in_sm")
    t_r = 384 if Lp % 384 == 0 else CH
    sm, = rm(_smallact, [psm], [p["smallbias"]], [(CH, F32)], [], tile=t_r, name="smallact")
    xbc = _conv_fwd(pxbc, p["conv_w"], p["conv_b"], geom=geom, tile=CH)
    y_ssd, states = _ssd_fwd(xbc, sm, p["a_log"], B=B, NC=NC)
    y_a, = rm(_ssd_post, [y_ssd, (xbc, D, 0), pz], [p["d_rep"], p["ssd_norm"]], [(D, BF16)], [], tile=t_r,
              name="ssd_post")
    cum = _cumsum_seq(sm, B=B, NC=NC, reverse=False, name="fox_cum")
    cumT = cum.reshape(B, Lp, CH)[:, :, SSD_H:SSD_H + FOX_H].transpose(0, 2, 1).reshape(B * FOX_H, 1, Lp)
    y_b, lse = _fox_fwd(qkv, cum, cumT, B=B, Lp=Lp, pad=pad)
    y_ssm, hst = _s5_fwd(pu, p["Bsg"], p["Csg"], p["lam"], B=B, NC=NC)
    y1, = rm(_s5_pre, [y_ssm, pu], [p["s5_d"]], [(D, BF16)], [], tile=t_r, name="s5_pre")
    tg = _mm(y1, p["w_glu"], "nn", F32, name="s5_glu_mm")
    y_c, = rm(_s5_glu, [y_ssm, pu, tg], [p["s5_d"]], [(D, BF16)], [], tile=t_r, name="s5_glu")
    br = [_mm(yy, p["w_branch"][n], "nn", F32, name=f"branch{n}") for n, yy in enumerate((y_a, y_b, y_c))]
    mix, = rm(_merge, [(pg, D, 0), (pg, D, 1), (pg, D, 2)] + br, [], [(D, BF16)], [], tile=CH, name="merge")
    x_mid = _mm(mix, p["w_out"], "nn", F32, res=x, name="out_proj")
    xn2, = rm(_f_norm, [x_mid], [p["norm2"]], [(D, BF16)], [], tile=t_r, name="norm2")
    hff = _mm(xn2, p["w_ffn_in"], "nn", F32, name="ffn_in")
    act, = rm(_swiglu, [(hff, DFF, 0), (hff, DFF, 1)], [], [(DFF, BF16)], [], tile=CH, name="swiglu")
    x_out = _mm(act, p["w_ffn_out"], "nn", F32, res=x_mid, name="ffn_out")
    sv.update(x=x, xn1=xn1, pz=pz, pxbc=pxbc, qkv=qkv, pu=pu, pg=pg, psm=psm, sm=sm, xbc=xbc, y_ssd=y_ssd,
              states=states, y_a=y_a, cum=cum, cumT=cumT, y_b=y_b, lse=lse, y_ssm=y_ssm, hst=hst, y1=y1, tg=tg,
              y_c=y_c, br=br, mix=mix, x_mid=x_mid, xn2=xn2, hff=hff, act=act)
    return x_out, sv


def _layer_bwd(dx_out, p, sv, geom, dims):
    B, NC, Lp, pad = dims
    T = geom[0]
    rm = functools.partial(_rowmap, geom=geom)
    t_r = 384 if Lp % 384 == 0 else CH
    g = {}
    dact = _mm(dx_out, p["w_ffn_out"], "nt", F32, name="ffn_out_dx")
    g["w_ffn_out"] = _mm(sv["act"], dx_out, "tn", F32, name="ffn_out_dw")
    dhff, = rm(_b_swiglu, [(sv["hff"], DFF, 0), (sv["hff"], DFF, 1), dact], [], [(2 * DFF, BF16)], [], tile=CH,
               name="swiglu_bwd")
    dxn2 = _mm(dhff, p["w_ffn_in"], "nt", F32, name="ffn_in_dx")
    g["w_ffn_in"] = _mm(sv["xn2"], dhff, "tn", F32, name="ffn_in_dw")
    dx_mid, g["norm2"] = rm(_b_norm, [sv["x_mid"], dxn2, dx_out], [p["norm2"]], [(D, F32)], [(1, D)], tile=t_r,
                            name="norm2_bwd")
    dmix = _mm(dx_mid, p["w_out"], "nt", F32, name="out_proj_dx")
    g["w_out"] = _mm(sv["mix"], dx_mid, "tn", F32, name="out_proj_dw")
    pg = sv["pg"]
    dpg, db0, db1, db2 = rm(_b_merge, [(pg, D, 0), (pg, D, 1), (pg, D, 2)] + sv["br"] + [dmix], [],
                            [(3 * D, BF16), (D, BF16), (D, BF16), (D, BF16)], [], tile=CH, name="merge_bwd")
    ys = (sv["y_a"], sv["y_b"], sv["y_c"])
    dbs = (db0, db1, db2)
    g["w_branch"] = [_mm(ys[n], dbs[n], "tn", F32, name=f"branch{n}_dw") for n in range(3)]
    dy = [_mm(dbs[n], p["w_branch"][n], "nt", F32, name=f"branch{n}_dx") for n in range(3)]
    dtg, dy1a = rm(_b_s5_glu, [sv["y_ssm"], sv["pu"], sv["tg"], dy[2]], [p["s5_d"]], [(D, BF16), (D, F32)], [],
                   tile=t_r, name="s5_glu_bwd")
    dy1b = _mm(dtg, p["w_glu"], "nt", F32, name="s5_glu_mm_dx")
    g["w_glu"] = _mm(sv["y1"], dtg, "tn", F32, name="s5_glu_mm_dw")
    dys, du_skip, g["s5_d"] = rm(_b_s5_pre, [sv["y_ssm"], sv["pu"], dy1a, dy1b], [p["s5_d"]],
                                 [(D, F32), (D, F32)], [(1, D)], tile=t_r, name="s5_pre_bwd")
    du, g["Bsg"], g["Csg"], g["lam"] = _s5_bwd(sv["pu"], sv["hst"], dys, du_skip, p["Bsg"], p["Csg"], p["lam"],
                                               B=B, NC=NC)
    dq, dcq, delta = _fox_bwd_q(sv["qkv"], dy[1], sv["y_b"], sv["lse"], sv["cum"], sv["cumT"], B=B, Lp=Lp, pad=pad)
    dk, dv, dckT = _fox_bwd_kv(sv["qkv"], dy[1], sv["lse"], delta, sv["cum"], sv["cumT"], B=B, Lp=Lp, pad=pad)
    dcum8 = dcq.reshape(B, FOX_H, Lp).transpose(0, 2, 1) + dckT.reshape(B, FOX_H, Lp).transpose(0, 2, 1)
    dcum = jnp.pad(dcum8.reshape(T, FOX_H), ((0, 0), (SSD_H, CH - SSD_H - FOX_H)))
    dlogf = _cumsum_seq(dcum, B=B, NC=NC, reverse=True, name="fox_cum_bwd")
    dy_ssd, dxs_skip, dz, g["d_rep"], g["ssd_norm"] = rm(
        _b_ssd_post, [sv["y_ssd"], (sv["xbc"], D, 0), sv["pz"], dy[0]], [p["d_rep"], p["ssd_norm"]],
        [(D, F32), (D, F32), (D, BF16)], [(1, D), (1, D)], tile=t_r, name="ssd_post_bwd")
    dxbc_act, ddt, g["a_log"] = _ssd_bwd(sv["xbc"], sv["sm"], p["a_log"], sv["states"], dy_ssd, dxs_skip, B=B, NC=NC)
    dpsm, g["smallbias"] = rm(_b_smallact, [sv["psm"], ddt, dlogf], [p["smallbias"]], [(CH, BF16)], [(1, CH)],
                              tile=t_r, name="smallact_bwd")
    dconv, g["conv_w"], g["conv_b"] = _conv_bwd_pre(sv["pxbc"], dxbc_act, p["conv_w"], p["conv_b"], geom=geom, tile=CH)
    dpxbc = _conv_bwd_x(dconv, p["conv_w"], geom=geom, tile=CH)
    dproj = jnp.concatenate([dz, dpxbc, dq, dk, dv, du, dpg, dpsm], axis=1)
    dxn1 = _mm(dproj, p["w_cat"], "nt", F32, name="in_dx")
    g["w_cat"] = _mm(sv["xn1"], dproj, "tn", F32, name="in_dw")
    dx_in, g["norm1"] = rm(_b_norm, [sv["x"], dxn1, dx_mid], [p["norm1"]], [(D, F32)], [(1, D)], tile=t_r,
                           name="norm1_bwd")
    return dx_in, g


_BIG = ["w_in", "s5_w_glu", "w_branch", "w_out", "w_ffn_in", "w_ffn_out"]
_NAMES = ['meta', 'norm1', 'w_in', 'ssd_conv_w', 'ssd_conv_b', 'ssd_dt_bias', 'ssd_a_log', 'ssd_d', 'ssd_norm',
          'fox_bf', 's5_lam_re', 's5_lam_im', 's5_b_re', 's5_b_im', 's5_c_re', 's5_c_im', 's5_log_step', 's5_d',
          's5_w_glu', 'w_branch', 'w_out', 'norm2', 'w_ffn_in', 'w_ffn_out', 'norm_f']
_SHARD_AXIS = {"meta": 1, "ssd_conv_w": 2}


def kernel(x, meta, norm1, w_in, ssd_conv_w, ssd_conv_b, ssd_dt_bias, ssd_a_log, ssd_d, ssd_norm, fox_bf, s5_lam_re, s5_lam_im, s5_b_re, s5_b_im, s5_c_re, s5_c_im, s5_log_step, s5_d, s5_w_glu, w_branch, w_out, norm2, w_ffn_in, w_ffn_out, norm_f, loss_target, m_meta, m_norm1, m_w_in, m_ssd_conv_w, m_ssd_conv_b, m_ssd_dt_bias, m_ssd_a_log, m_ssd_d, m_ssd_norm, m_fox_bf, m_s5_lam_re, m_s5_lam_im, m_s5_b_re, m_s5_b_im, m_s5_c_re, m_s5_c_im, m_s5_log_step, m_s5_d, m_s5_w_glu, m_w_branch, m_w_out, m_norm2, m_w_ffn_in, m_w_ffn_out, m_norm_f, v_meta, v_norm1, v_w_in, v_ssd_conv_w, v_ssd_conv_b, v_ssd_dt_bias, v_ssd_a_log, v_ssd_d, v_ssd_norm, v_fox_bf, v_s5_lam_re, v_s5_lam_im, v_s5_b_re, v_s5_b_im, v_s5_c_re, v_s5_c_im, v_s5_log_step, v_s5_d, v_s5_w_glu, v_w_branch, v_w_out, v_norm2, v_w_ffn_in, v_w_ffn_out, v_norm_f):
    args = locals()
    W = {n: args[n] for n in _NAMES}
    Mo = {n: args["m_" + n] for n in _NAMES}
    Vo = {n: args["v_" + n] for n in _NAMES}
    B, S, _ = x.shape
    depth = norm1.shape[0]
    L = S + N_META
    Lp = ((L + CH - 1) // CH) * CH
    pad = Lp - L
    assert pad + N_META == CH and S % CH == 0
    NC = Lp // CH
    T = B * Lp
    geom = (T, Lp, pad)
    dims = (B, NC, Lp, pad)
    xi, yi, ci = _me()
    dev = 4 * xi + 2 * yi + ci

    big_pack = _Packer([W[n].shape for n in _BIG], 16)
    gathered = _all_gather(big_pack.pack([W[n] for n in _BIG], BF16), "gather_weights")
    parts = [big_pack.unpack(gathered[k]) for k in range(N_DEV)]
    full = {}
    full["w_in"] = jnp.concatenate([parts[k][0] for k in range(N_DEV)], axis=2)
    full["s5_w_glu"] = jnp.concatenate([parts[k][1] for k in range(N_DEV)], axis=1)
    full["w_branch"] = jnp.concatenate([parts[k][2] for k in range(N_DEV)], axis=2)
    full["w_out"] = jnp.concatenate([parts[k][3] for k in range(N_DEV)], axis=1)
    full["w_ffn_in"] = jnp.concatenate([parts[k][4] for k in range(N_DEV)], axis=2)
    full["w_ffn_out"] = jnp.concatenate([parts[k][5] for k in range(N_DEV)], axis=1)
    sm_pack = _Packer([meta.shape, ssd_conv_w.shape], 8)
    sm_g = _all_gather(sm_pack.pack([meta, ssd_conv_w], F32), "gather_small")
    sm_parts = [sm_pack.unpack(sm_g[k]) for k in range(N_DEV)]
    meta_full = jnp.concatenate([sm_parts[k][0] for k in range(N_DEV)], axis=1)
    conv_w_full = jnp.concatenate([sm_parts[k][1] for k in range(N_DEV)], axis=2)

    layers = []
    s5_in = []
    for l in range(depth):
        lre = _s5_tile(s5_lam_re[l])
        lim = _s5_tile(s5_lam_im[l])
        lst = _s5_tile(jnp.repeat(s5_log_step[l], S5_P))
        bre = _s5_tile_b(s5_b_re[l])
        bim = _s5_tile_b(s5_b_im[l])
        s5_in.append((lre, lim, lst, bre, bim))
        a, b, br_, bi_ = _s5_params(lre, lim, lst, bre, bim)
        Bsg, Csg = _s5_blockdiag(_s5_untile_b(br_), _s5_untile_b(bi_), s5_c_re[l], s5_c_im[l])
        NS = S5_G // S5_SG
        lam = jnp.concatenate([a.reshape(NS, 1, S5_W), b.reshape(NS, 1, S5_W)], axis=2)
        zpad = jnp.zeros((CH - SSD_H - FOX_H,), F32)
        layers.append(dict(
            norm1=norm1[l][None], w_cat=_w_cat(full["w_in"][l]),
            smallbias=jnp.concatenate([ssd_dt_bias[l], fox_bf[l], zpad])[None],
            conv_w=conv_w_full[l], conv_b=ssd_conv_b[l][None],
            a_log=jnp.concatenate([ssd_a_log[l], jnp.zeros((CH - SSD_H,), F32)])[None],
            d_rep=jnp.repeat(ssd_d[l], SSD_P)[None], ssd_norm=ssd_norm[l][None],
            Bsg=Bsg, Csg=Csg, lam=lam, s5_d=s5_d[l][None], w_glu=full["s5_w_glu"][l],
            w_branch=[full["w_branch"][l, n] for n in range(3)], w_out=full["w_out"][l],
            norm2=norm2[l][None], w_ffn_in=full["w_ffn_in"][l], w_ffn_out=full["w_ffn_out"][l]))

    xs = jnp.concatenate([jnp.zeros((B, pad, D), F32), jnp.broadcast_to(meta_full[None], (B, N_META, D)), x], axis=1)
    h = xs.reshape(T, D)
    saved = []
    for l in range(depth):
        h, sv = _layer_fwd(h, layers[l], geom, dims)
        saved.append(sv)
    dh, loss_row, g_nf = _loss_head(h, norm_f[None], loss_target.reshape(B * S, D), B=B, NC=NC, S=S)
    loss = lax.psum(loss_row[0, 0], AXES)

    G = {n: [None] * depth for n in _NAMES}
    for l in reversed(range(depth)):
        dh, g = _layer_bwd(dh, layers[l], saved[l], geom, dims)
        saved[l] = None
        G["norm1"][l] = g["norm1"][0]
        G["norm2"][l] = g["norm2"][0]
        G["w_in"][l] = _w_uncat(g["w_cat"])
        G["ssd_conv_w"][l] = g["conv_w"]
        G["ssd_conv_b"][l] = g["conv_b"][0]
        G["ssd_dt_bias"][l] = g["smallbias"][0, 0:SSD_H]
        G["fox_bf"][l] = g["smallbias"][0, SSD_H:SSD_H + FOX_H]
        G["ssd_a_log"][l] = g["a_log"][0, 0:SSD_H]
        G["ssd_d"][l] = g["d_rep"].reshape(SSD_H, SSD_P).sum(axis=1)
        G["ssd_norm"][l] = g["ssd_norm"][0]
        dbr, dbi, dcr, dci = _s5_unblock(g["Bsg"], g["Csg"])
        da = _s5_tile(g["lam"][:, 0, 0:S5_W])
        db = _s5_tile(g["lam"][:, 0, S5_W:])
        dlre, dlim, dlst, dbre, dbim = _s5_params_bwd(*s5_in[l], da, db, _s5_tile_b(dbr), _s5_tile_b(dbi))
        G["s5_lam_re"][l] = dlre.reshape(S5_G, S5_P)
        G["s5_lam_im"][l] = dlim.reshape(S5_G, S5_P)
        G["s5_log_step"][l] = dlst.reshape(S5_G, S5_P).sum(axis=1)
        G["s5_b_re"][l] = _s5_untile_b(dbre)
        G["s5_b_im"][l] = _s5_untile_b(dbim)
        G["s5_c_re"][l] = dcr
        G["s5_c_im"][l] = dci
        G["s5_d"][l] = g["s5_d"][0]
        G["s5_w_glu"][l] = g["w_glu"]
        G["w_branch"][l] = jnp.stack(g["w_branch"])
        G["w_out"][l] = g["w_out"]
        G["w_ffn_in"][l] = g["w_ffn_in"]
        G["w_ffn_out"][l] = g["w_ffn_out"]
    dxs = dh.reshape(B, Lp, D)
    grad_x = dxs[:, pad + N_META:, :]
    part = {n: jnp.stack(G[n]) for n in _NAMES if n not in ("meta", "norm_f")}
    part["meta"] = dxs[:, pad:pad + N_META, :].sum(axis=0)
    part["norm_f"] = g_nf[0]

    def shard(n, k):
        a = part[n]
        if n == "w_in":
            return a.reshape(depth, D, N_DEV, -1)[:, :, k]
        if n == "w_ffn_in":
            return a.reshape(depth, D, N_DEV, -1)[:, :, k]
        if n == "w_branch":
            return a.reshape(depth, 3, N_DEV, -1, D)[:, :, k]
        return a.reshape(depth, N_DEV, -1, D)[:, k]

    g8 = jnp.stack([big_pack.pack([shard(n, k) for n in _BIG], F32) for k in range(N_DEV)])
    big_red = big_pack.unpack(_reduce_scatter(g8))
    grads = dict(zip(_BIG, big_red))

    small = [n for n in _NAMES if n not in _BIG]
    sp = _Packer([part[n].shape for n in small], 8)
    tot = sp.unpack(_sum8(_all_gather(sp.pack([part[n] for n in small], F32), "gather_small_grads")))
    for n, t in zip(small, tot):
        if n in _SHARD_AXIS:
            ax = _SHARD_AXIS[n]
            w = W[n].shape[ax]
            t = lax.dynamic_slice_in_dim(t, dev * w, w, axis=ax)
        grads[n] = t

    delta, new_m, new_v = {}, {}, {}
    for n in _BIG:
        delta[n], new_m[n], new_v[n] = _adam(W[n], grads[n], Mo[n], Vo[n], "adam_" + n)
    ap = _Packer([W[n].shape for n in small], 8)
    d_, m_, v_ = _adam(ap.pack([W[n] for n in small], F32), ap.pack([grads[n] for n in small], F32),
                       ap.pack([Mo[n] for n in small], F32), ap.pack([Vo[n] for n in small], F32), "adam_small")
    for n, a, b, c in zip(small, ap.unpack(d_), ap.unpack(m_), ap.unpack(v_)):
        delta[n], new_m[n], new_v[n] = a, b, c
    return (loss, grad_x, *[grads[n] for n in _NAMES], *[delta[n] for n in _NAMES],
            *[new_m[n] for n in _NAMES], *[new_v[n] for n in _NAMES])
```

```python
import functools
import math

import numpy as np
import jax
import jax.numpy as jnp
from jax import lax
from jax.experimental import pallas as pl
from jax.experimental.pallas import tpu as pltpu

F32 = jnp.float32
BF16 = jnp.bfloat16
AXES = ("x", "y", "c")
MESH = pl.DeviceIdType.MESH
N_DEV = 8

D = 1024
N_META = 16
CH = 128
EPS = 1e-6
NEG = -1e30
SSD_H, SSD_P, SSD_N, SSD_G = 16, 64, 128, 2
CONV_K, CONV_DIM = 4, 1536
FOX_H, FOX_DH = 8, 128
S5_G, S5_P, S5_C = 64, 64, 16
S5_SG = 8
S5_W = S5_SG * S5_P
DFF = 2816
D_IN = 9752
OFF_Z, OFF_XBC, OFF_QKV, OFF_U, OFF_G, OFF_SM, D_CAT = 0, 1024, 2560, 5632, 6656, 9728, 9856
O_Z, O_XBC, O_DT, O_QKV, O_F, O_U, O_G = 0, 1024, 2560, 2576, 5648, 5656, 6680

ADAM_LR, ADAM_B1, ADAM_B2, ADAM_EPS, ADAM_WD, ADAM_STEP = 0.001, 0.9, 0.999, 1e-08, 0.01, 10

VMEM_LIMIT_V7X = 52 * 1024 * 1024
HI = lax.Precision.HIGHEST


def _cp(sem=None):
    return pltpu.CompilerParams(dimension_semantics=sem, vmem_limit_bytes=VMEM_LIMIT_V7X)


def _pick(n, cands):
    for c in cands:
        if n % c == 0:
            return c
    raise ValueError(f"no tile for {n}")


_TILES = (1408, 1024, 896, 768, 512, 384, 256, 128)


def _mm(a, b, mode, out_dtype, *, name, n=None, b_off=0, res=None, tm=None, tn=None, tk=None):
    if mode == "tn":
        K, M = a.shape
    else:
        M, K = a.shape
    if mode == "nt":
        N = b.shape[0]
    else:
        N = n if n is not None else b.shape[1]
    tm = tm or _pick(M, (1024, 768, 512, 384, 256, 128, 64, 16, 8))
    tn = tn or _pick(math.gcd(N, b_off) if b_off else N, (1024, 896, 768, 512, 384, 256, 128))
    tk = tk or _pick(K, _TILES)
    nk = K // tk
    joff = b_off // tn

    def body(*refs):
        if res is None:
            a_ref, b_ref, o_ref, acc = refs
            r_ref = None
        else:
            a_ref, b_ref, r_ref, o_ref, acc = refs
        k = pl.program_id(2)
        av = a_ref[...].astype(BF16)
        bv = b_ref[...].astype(BF16)
        if mode == "nn":
            p = jnp.dot(av, bv, preferred_element_type=F32)
        elif mode == "nt":
            p = lax.dot_general(av, bv, (((1,), (1,)), ((), ())), preferred_element_type=F32)
        else:
            p = lax.dot_general(av, bv, (((0,), (0,)), ((), ())), preferred_element_type=F32)

        @pl.when(k == 0)
        def _():
            acc[...] = p

        @pl.when(k > 0)
        def _():
            acc[...] += p

        @pl.when(k == nk - 1)
        def _():
            r = acc[...]
            if r_ref is not None:
                r = r + r_ref[...]
            o_ref[...] = r.astype(o_ref.dtype)

    if mode == "tn":
        a_spec = pl.BlockSpec((tk, tm), lambda i, j, k: (k, i))
    else:
        a_spec = pl.BlockSpec((tm, tk), lambda i, j, k: (i, k))
    if mode == "nt":
        b_spec = pl.BlockSpec((tn, tk), lambda i, j, k: (j, k))
    else:
        b_spec = pl.BlockSpec((tk, tn), lambda i, j, k: (k, j + joff))
    o_spec = pl.BlockSpec((tm, tn), lambda i, j, k: (i, j))
    in_specs = [a_spec, b_spec] + ([o_spec] if res is not None else [])
    args = (a, b) + ((res,) if res is not None else ())
    return pl.pallas_call(
        body, name=name, grid=(M // tm, N // tn, nk),
        in_specs=in_specs, out_specs=o_spec,
        out_shape=jax.ShapeDtypeStruct((M, N), out_dtype),
        scratch_shapes=[pltpu.VMEM((tm, tn), F32)],
        compiler_params=_cp(("parallel", "parallel", "arbitrary")),
    )(*args)


def _rowmap(fn, row_ins, const_ins, row_outs, acc_outs, *, geom, tile, name):
    T, Lp, pad = geom
    assert Lp % tile == 0
    per_seq = Lp // tile
    specs, args = [], []
    for r in row_ins:
        arr, w, cb = r if isinstance(r, tuple) else (r, r.shape[1], 0)
        specs.append(pl.BlockSpec((tile, w), functools.partial(lambda i, cb: (i, cb), cb=cb)))
        args.append(arr)
    for c in const_ins:
        specs.append(pl.BlockSpec(c.shape, functools.partial(lambda i, nd: (0,) * nd, nd=c.ndim)))
        args.append(c)
    n_r, n_c, n_o, n_a = len(row_ins), len(const_ins), len(row_outs), len(acc_outs)
    out_specs = [pl.BlockSpec((tile, w), lambda i: (i, 0)) for w, _ in row_outs]
    out_specs += [pl.BlockSpec(s, lambda i: (0, 0)) for s in acc_outs]
    out_shape = [jax.ShapeDtypeStruct((T, w), dt) for w, dt in row_outs]
    out_shape += [jax.ShapeDtypeStruct(s, F32) for s in acc_outs]

    def body(*refs):
        i = pl.program_id(0)
        pos = (i % per_seq) * tile + lax.broadcasted_iota(jnp.int32, (tile, 1), 0)
        valid = pos >= pad
        vals = [r[...] for r in refs[:n_r + n_c]]
        outs = fn(valid, *vals)
        if not isinstance(outs, (tuple, list)):
            outs = (outs,)
        orefs = refs[n_r + n_c:]
        for r, v in zip(orefs[:n_o], outs[:n_o]):
            r[...] = v.astype(r.dtype)
        for r, v in zip(orefs[n_o:], outs[n_o:]):
            @pl.when(i == 0)
            def _(r=r, v=v):
                r[...] = v

            @pl.when(i > 0)
            def _(r=r, v=v):
                r[...] += v

    res = pl.pallas_call(
        body, name=name, grid=(T // tile,), in_specs=specs, out_specs=out_specs, out_shape=out_shape,
        compiler_params=_cp(("arbitrary",)),
    )(*args)
    return res


def _sigmoid(x):
    return 1.0 / (1.0 + jnp.exp(-x))


def _silu(x):
    return x * _sigmoid(x)


def _softplus(x):
    return jnp.maximum(x, 0.0) + jnp.log(1.0 + jnp.exp(-jnp.abs(x)))


def _gelu(x):
    return 0.5 * x * (1.0 + jnp.tanh(math.sqrt(2.0 / math.pi) * (x + 0.044715 * x * x * x)))


def _rms(x, w):
    return x * lax.rsqrt(jnp.mean(x * x, axis=-1, keepdims=True) + EPS) * w


def _colsum(v):
    return jnp.sum(v, axis=0, keepdims=True)


def _f_norm(valid, x, w):
    return _rms(x, w)


def _b_norm(valid, x, dxn, dres, w):
    _, vjp = jax.vjp(_rms, x, w)
    dx, dw = vjp(dxn)
    return jnp.where(valid, dx + dres, 0.0), dw


def _smallact(valid, raw, bias):
    lane = lax.broadcasted_iota(jnp.int32, raw.shape, 1)
    v = raw + bias
    dt = _softplus(v)
    logf = -_softplus(-v)
    out = jnp.where(lane < SSD_H, dt, jnp.where(lane < SSD_H + FOX_H, logf, 0.0))
    return jnp.where(valid, out, 0.0)


def _b_smallact(valid, raw, d1, d2, bias):
    _, vjp = jax.vjp(lambda r, b: _smallact(valid, r, b), raw, bias)
    return vjp(d1 + d2)


def _ssd_post(valid, y, xs, z, drep, nw):
    y = (y + xs * drep) * _silu(z)
    return _rms(y, nw)


def _b_ssd_post(valid, y, xs, z, dya, drep, nw):
    _, vjp = jax.vjp(lambda a, b, c, d, e: _ssd_post(valid, a, b, c, d, e), y, xs, z, drep, nw)
    dy, dxs, dz, dd, dn = vjp(dya)
    return dy, dxs, dz, dd, dn


def _s5_pre(valid, ys, u, d):
    return _gelu(ys + d * u)


def _s5_glu(valid, ys, u, t, d):
    y1 = _gelu(ys + d * u)
    return y1 * _sigmoid(t)


def _b_s5_glu(valid, ys, u, t, dyc, d):
    y1 = _gelu(ys + d * u)
    _, vjp = jax.vjp(lambda a, b: a * _sigmoid(b), y1, t)
    dy1, dt = vjp(dyc)
    return dt, dy1


def _b_s5_pre(valid, ys, u, dy1a, dy1b, d):
    _, vjp = jax.vjp(lambda a, b, c: _gelu(a + c * b), ys, u, d)
    dys, du, dd = vjp(dy1a + dy1b)
    return dys, du, dd


def _merge(valid, g0, g1, g2, b0, b1, b2):
    m = _sigmoid(g0) * b0 + _sigmoid(g1) * b1 + _sigmoid(g2) * b2
    return jnp.where(valid, m, 0.0)


def _b_merge(valid, g0, g1, g2, b0, b1, b2, dmix):
    _, vjp = jax.vjp(lambda *a: _merge(valid, *a), g0, g1, g2, b0, b1, b2)
    d = vjp(dmix)
    return jnp.concatenate(d[:3], axis=1), d[3], d[4], d[5]


def _swiglu(valid, g, up):
    return _silu(g) * up


def _b_swiglu(valid, g, up, dact):
    _, vjp = jax.vjp(lambda a, b: _silu(a) * b, g, up)
    dg, dup = vjp(dact)
    return jnp.concatenate([dg, dup], axis=1)


def _conv_taps(ext, tile):
    taps = []
    for k in range(CONV_K):
        sh = CONV_K - 1 - k
        v = ext if sh == 0 else pltpu.roll(ext, sh, 0)
        taps.append(v[8:8 + tile])
    return taps


def _conv_fwd(x, w, b, *, geom, tile):
    T, Lp, pad = geom
    per_seq = Lp // tile
    hb = tile // 8

    def body(x_ref, h_ref, w_ref, b_ref, o_ref):
        i = pl.program_id(0)
        pos = (i % per_seq) * tile + lax.broadcasted_iota(jnp.int32, (tile, 1), 0)
        ext = jnp.concatenate([h_ref[...], x_ref[...]], axis=0)
        taps = _conv_taps(ext, tile)
        acc = b_ref[...] + taps[0] * w_ref[0:1, :]
        for k in range(1, CONV_K):
            acc = acc + taps[k] * w_ref[k:k + 1, :]
        o_ref[...] = jnp.where(pos >= pad, _silu(acc), 0.0)

    return pl.pallas_call(
        body, name="conv_fwd", grid=(T // tile,),
        in_specs=[pl.BlockSpec((tile, CONV_DIM), lambda i: (i, 0)),
                  pl.BlockSpec((8, CONV_DIM), lambda i: (jnp.maximum(i * hb - 1, 0), 0)),
                  pl.BlockSpec((CONV_K, CONV_DIM), lambda i: (0, 0)),
                  pl.BlockSpec((1, CONV_DIM), lambda i: (0, 0))],
        out_specs=pl.BlockSpec((tile, CONV_DIM), lambda i: (i, 0)),
        out_shape=jax.ShapeDtypeStruct((T, CONV_DIM), F32),
        compiler_params=_cp(("arbitrary",)),
    )(x, x, w, b)


def _conv_bwd_pre(x, dact, w, b, *, geom, tile):
    T, Lp, pad = geom
    per_seq = Lp // tile
    hb = tile // 8

    def body(x_ref, h_ref, d_ref, w_ref, b_ref, dc_ref, dw_ref, db_ref):
        i = pl.program_id(0)
        pos = (i % per_seq) * tile + lax.broadcasted_iota(jnp.int32, (tile, 1), 0)
        ext = jnp.concatenate([h_ref[...], x_ref[...]], axis=0)
        taps = _conv_taps(ext, tile)
        acc = b_ref[...] + taps[0] * w_ref[0:1, :]
        for k in range(1, CONV_K):
            acc = acc + taps[k] * w_ref[k:k + 1, :]
        sg = _sigmoid(acc)
        dsilu = sg * (1.0 + acc * (1.0 - sg))
        dc = jnp.where(pos >= pad, d_ref[...] * dsilu, 0.0)
        dc_ref[...] = dc
        dw = jnp.concatenate([_colsum(dc * taps[k]) for k in range(CONV_K)], axis=0)
        db = _colsum(dc)

        @pl.when(i == 0)
        def _():
            dw_ref[...] = dw
            db_ref[...] = db

        @pl.when(i > 0)
        def _():
            dw_ref[...] += dw
            db_ref[...] += db

    return pl.pallas_call(
        body, name="conv_bwd_pre", grid=(T // tile,),
        in_specs=[pl.BlockSpec((tile, CONV_DIM), lambda i: (i, 0)),
                  pl.BlockSpec((8, CONV_DIM), lambda i: (jnp.maximum(i * hb - 1, 0), 0)),
                  pl.BlockSpec((tile, CONV_DIM), lambda i: (i, 0)),
                  pl.BlockSpec((CONV_K, CONV_DIM), lambda i: (0, 0)),
                  pl.BlockSpec((1, CONV_DIM), lambda i: (0, 0))],
        out_specs=[pl.BlockSpec((tile, CONV_DIM), lambda i: (i, 0)),
                   pl.BlockSpec((CONV_K, CONV_DIM), lambda i: (0, 0)),
                   pl.BlockSpec((1, CONV_DIM), lambda i: (0, 0))],
        out_shape=[jax.ShapeDtypeStruct((T, CONV_DIM), F32),
                   jax.ShapeDtypeStruct((CONV_K, CONV_DIM), F32),
                   jax.ShapeDtypeStruct((1, CONV_DIM), F32)],
        compiler_params=_cp(("arbitrary",)),
    )(x, x, dact, w, b)


def _conv_bwd_x(dc, w, *, geom, tile):
    T, Lp, pad = geom
    nt = T // tile
    hb = tile // 8

    def body(d_ref, h_ref, w_ref, o_ref):
        i = pl.program_id(0)
        halo = jnp.where(i < nt - 1, h_ref[...], 0.0)
        ext = jnp.concatenate([d_ref[...], halo], axis=0)
        n_ext = tile + 8
        acc = ext[0:tile] * w_ref[CONV_K - 1:CONV_K, :]
        for j in range(1, CONV_K):
            acc = acc + pltpu.roll(ext, n_ext - j, 0)[0:tile] * w_ref[CONV_K - 1 - j:CONV_K - j, :]
        o_ref[...] = acc.astype(o_ref.dtype)

    return pl.pallas_call(
        body, name="conv_bwd_x", grid=(nt,),
        in_specs=[pl.BlockSpec((tile, CONV_DIM), lambda i: (i, 0)),
                  pl.BlockSpec((8, CONV_DIM), lambda i: (jnp.minimum((i + 1) * hb, nt * hb - 1), 0)),
                  pl.BlockSpec((CONV_K, CONV_DIM), lambda i: (0, 0))],
        out_specs=pl.BlockSpec((tile, CONV_DIM), lambda i: (i, 0)),
        out_shape=jax.ShapeDtypeStruct((T, CONV_DIM), BF16),
        compiler_params=_cp(("arbitrary",)),
    )(dc, dc, w)


def _ssd_common(sm_ref, alog_ref):
    lane = lax.broadcasted_iota(jnp.int32, (1, CH), 1)
    A = jnp.where(lane < SSD_H, -jnp.exp(alog_ref[...]), 0.0)
    dt = sm_ref[...]
    adt = dt * A
    r = lax.broadcasted_iota(jnp.int32, (CH, CH), 0)
    c = lax.broadcasted_iota(jnp.int32, (CH, CH), 1)
    tril = (r >= c).astype(F32)
    cs = jnp.dot(tril, adt, precision=HI, preferred_element_type=F32)
    csT = cs.T
    cs_last = jnp.sum(jnp.where(r == CH - 1, cs, 0.0), axis=0, keepdims=True)
    return A, dt, cs, csT, cs_last, tril, r, c


def _nt(a, b):
    return lax.dot_general(a, b, (((1,), (1,)), ((), ())), preferred_element_type=F32)


def _tn(a, b):
    return lax.dot_general(a, b, (((0,), (0,)), ((), ())), preferred_element_type=F32)


def _nn(a, b):
    return jnp.dot(a, b, preferred_element_type=F32)


def _ssd_fwd(xbc, sm, alog, *, B, NC):
    T = B * NC * CH

    def body(x_ref, sm_ref, alog_ref, y_ref, st_ref, S):
        cidx = pl.program_id(1)

        @pl.when(cidx == 0)
        def _():
            S[...] = jnp.zeros_like(S)

        st_ref[0] = S[...]
        A, dt, cs, csT, cs_last, tril, _, _ = _ssd_common(sm_ref, alog_ref)
        for g in range(SSD_G):
            Bg = x_ref[:, D + g * SSD_N:D + (g + 1) * SSD_N]
            Cg = x_ref[:, D + SSD_G * SSD_N + g * SSD_N:D + SSD_G * SSD_N + (g + 1) * SSD_N]
            Cb = Cg.astype(BF16)
            G = _nt(Cb, Bg.astype(BF16))
            for rr in range(SSD_H // SSD_G):
                h = g * (SSD_H // SSD_G) + rr
                col = cs[:, h:h + 1]
                row = csT[h:h + 1, :]
                Ld = jnp.where(tril > 0, jnp.exp(jnp.minimum(col - row, 0.0)), 0.0)
                M = (G * Ld).astype(BF16)
                xdt = (x_ref[:, h * SSD_P:(h + 1) * SSD_P] * dt[:, h:h + 1]).astype(BF16)
                ST = S[h]
                y = _nn(M, xdt) + jnp.exp(col) * _nt(Cb, ST.astype(BF16))
                y_ref[:, h * SSD_P:(h + 1) * SSD_P] = y
                cl = cs_last[:, h:h + 1]
                Bd = (Bg * jnp.exp(cl - col)).astype(BF16)
                S[h] = jnp.exp(cl) * ST + _tn(xdt, Bd)

    return pl.pallas_call(
        body, name="ssd_fwd", grid=(B, NC),
        in_specs=[pl.BlockSpec((CH, CONV_DIM), lambda b, c: (b * NC + c, 0)),
                  pl.BlockSpec((CH, CH), lambda b, c: (b * NC + c, 0)),
                  pl.BlockSpec((1, CH), lambda b, c: (0, 0))],
        out_specs=[pl.BlockSpec((CH, D), lambda b, c: (b * NC + c, 0)),
                   pl.BlockSpec((1, SSD_H, SSD_P, SSD_N), lambda b, c: (b * NC + c, 0, 0, 0))],
        out_shape=[jax.ShapeDtypeStruct((T, D), F32),
                   jax.ShapeDtypeStruct((B * NC, SSD_H, SSD_P, SSD_N), F32)],
        scratch_shapes=[pltpu.VMEM((SSD_H, SSD_P, SSD_N), F32)],
        compiler_params=_cp(("arbitrary", "arbitrary")),
    )(xbc, sm, alog)


def _ssd_bwd(xbc, sm, alog, states, dy, dxs_skip, *, B, NC):
    T = B * NC * CH

    def rix(b, c):
        return b * NC + (NC - 1 - c)

    def body(x_ref, sm_ref, alog_ref, st_ref, dy_ref, sk_ref, dx_ref, ddt_ref, dal_ref, dS):
        bidx = pl.program_id(0)
        cidx = pl.program_id(1)

        @pl.when(cidx == 0)
        def _():
            dS[...] = jnp.zeros_like(dS)

        A, dt, cs, csT, cs_last, tril, r, c = _ssd_common(sm_ref, alog_ref)
        lane = lax.broadcasted_iota(jnp.int32, (1, CH), 1)
        DCcol = jnp.zeros((CH, CH), F32)
        DCrow = jnp.zeros((CH, CH), F32)
        DX = jnp.zeros((CH, CH), F32)
        dlast = jnp.zeros((1, CH), F32)
        for g in range(SSD_G):
            ob = D + g * SSD_N
            oc = D + SSD_G * SSD_N + g * SSD_N
            Bg = x_ref[:, ob:ob + SSD_N]
            Cg = x_ref[:, oc:oc + SSD_N]
            Bb = Bg.astype(BF16)
            Cb = Cg.astype(BF16)
            G = _nt(Cb, Bb)
            dG = jnp.zeros((CH, CH), F32)
            dBg = jnp.zeros((CH, SSD_N), F32)
            dCg = jnp.zeros((CH, SSD_N), F32)
            for rr in range(SSD_H // SSD_G):
                h = g * (SSD_H // SSD_G) + rr
                sl = slice(h * SSD_P, (h + 1) * SSD_P)
                col = cs[:, h:h + 1]
                row = csT[h:h + 1, :]
                Ld = jnp.where(tril > 0, jnp.exp(jnp.minimum(col - row, 0.0)), 0.0)
                Mf = G * Ld
                dth = dt[:, h:h + 1]
                xs_h = x_ref[:, sl]
                xdt = (xs_h * dth).astype(BF16)
                ST = st_ref[0, h]
                STb = ST.astype(BF16)
                dST = dS[h]
                dSTb = dST.astype(BF16)
                dyh = dy_ref[:, sl]
                dyb = dyh.astype(BF16)
                E = jnp.exp(col)
                yo = _nt(Cb, STb)
                dxdt = _tn(Mf.astype(BF16), dyb)
                dM = _nt(dyb, xdt)
                dG = dG + dM * Ld
                W = dM * Mf
                dcol = jnp.sum(W, axis=1, keepdims=True) + jnp.sum(dyh * yo, axis=1, keepdims=True) * E
                drow = -jnp.sum(W, axis=0, keepdims=True)
                dyE = (dyh * E).astype(BF16)
                dCg = dCg + _nn(dyE, STb)
                dS_in = _tn(dyE, Cb)
                cl = cs_last[:, h:h + 1]
                decay = jnp.exp(cl - col)
                Bd = (Bg * decay).astype(BF16)
                dxdt = dxdt + _nt(Bd, dSTb)
                dBd = _nn(xdt, dSTb)
                dBg = dBg + decay * dBd
                dd = jnp.sum(dBd * Bg, axis=1, keepdims=True) * decay
                dcol = dcol - dd
                el = jnp.exp(cl)
                dl = jnp.sum(dd, axis=0, keepdims=True) + el * jnp.sum(
                    jnp.sum(dST * ST, axis=1, keepdims=True), axis=0, keepdims=True)
                dS[h] = dS_in + el * dST
                dx_ref[:, sl] = dxdt * dth + sk_ref[:, sl]
                ddt_x = jnp.sum(dxdt * xs_h, axis=1, keepdims=True)
                DCcol = DCcol + jnp.where(c == h, dcol, 0.0)
                DCrow = DCrow + jnp.where(r == h, drow, 0.0)
                DX = DX + jnp.where(c == h, ddt_x, 0.0)
                dlast = dlast + jnp.where(lane == h, dl, 0.0)
            dGb = dG.astype(BF16)
            dx_ref[:, ob:ob + SSD_N] = dBg + _tn(dGb, Cb)
            dx_ref[:, oc:oc + SSD_N] = dCg + _nn(dGb, Bb)
        DC = DCcol + DCrow.T + jnp.where(r == CH - 1, dlast, 0.0)
        triu = (r <= c).astype(F32)
        dadt = jnp.dot(triu, DC, precision=HI, preferred_element_type=F32)
        ddt_ref[...] = dadt * A + DX
        dal = jnp.sum(dadt * dt, axis=0, keepdims=True) * A

        @pl.when((bidx == 0) & (cidx == 0))
        def _():
            dal_ref[...] = dal

        @pl.when((bidx > 0) | (cidx > 0))
        def _():
            dal_ref[...] += dal

    return pl.pallas_call(
        body, name="ssd_bwd", grid=(B, NC),
        in_specs=[pl.BlockSpec((CH, CONV_DIM), lambda b, c: (rix(b, c), 0)),
                  pl.BlockSpec((CH, CH), lambda b, c: (rix(b, c), 0)),
                  pl.BlockSpec((1, CH), lambda b, c: (0, 0)),
                  pl.BlockSpec((1, SSD_H, SSD_P, SSD_N), lambda b, c: (rix(b, c), 0, 0, 0)),
                  pl.BlockSpec((CH, D), lambda b, c: (rix(b, c), 0)),
                  pl.BlockSpec((CH, D), lambda b, c: (rix(b, c), 0))],
        out_specs=[pl.BlockSpec((CH, CONV_DIM), lambda b, c: (rix(b, c), 0)),
                   pl.BlockSpec((CH, CH), lambda b, c: (rix(b, c), 0)),
                   pl.BlockSpec((1, CH), lambda b, c: (0, 0))],
        out_shape=[jax.ShapeDtypeStruct((T, CONV_DIM), F32),
                   jax.ShapeDtypeStruct((T, CH), F32),
                   jax.ShapeDtypeStruct((1, CH), F32)],
        scratch_shapes=[pltpu.VMEM((SSD_H, SSD_P, SSD_N), F32)],
        compiler_params=_cp(("arbitrary", "arbitrary")),
    )(xbc, sm, alog, states, dy, dxs_skip)


def _cumsum_seq(v, *, B, NC, reverse, name):
    T = B * NC * CH

    def ix(b, c):
        return b * NC + ((NC - 1 - c) if reverse else c)

    def body(v_ref, o_ref, carry):
        cidx = pl.program_id(1)

        @pl.when(cidx == 0)
        def _():
            carry[...] = jnp.zeros_like(carry)

        r = lax.broadcasted_iota(jnp.int32, (CH, CH), 0)
        c = lax.broadcasted_iota(jnp.int32, (CH, CH), 1)
        tri = ((r <= c) if reverse else (r >= c)).astype(F32)
        cs = jnp.dot(tri, v_ref[...], precision=HI, preferred_element_type=F32) + carry[...]
        o_ref[...] = cs
        edge = 0 if reverse else CH - 1
        carry[...] = jnp.sum(jnp.where(r == edge, cs, 0.0), axis=0, keepdims=True)

    return pl.pallas_call(
        body, name=name, grid=(B, NC),
        in_specs=[pl.BlockSpec((CH, CH), lambda b, c: (ix(b, c), 0))],
        out_specs=pl.BlockSpec((CH, CH), lambda b, c: (ix(b, c), 0)),
        out_shape=jax.ShapeDtypeStruct((T, CH), F32),
        scratch_shapes=[pltpu.VMEM((1, CH), F32)],
        compiler_params=_cp(("arbitrary", "arbitrary")),
    )(v)


def _fox_tb(Lp):
    return 384 if (Lp % 384 == 0 and Lp > 384) else CH


def _fox_keybias(cum, *, B, Lp, pad):
    ck = cum.reshape(B, Lp, CH)[:, :, SSD_H:SSD_H + FOX_H].transpose(0, 2, 1)
    pos = lax.broadcasted_iota(jnp.int32, ck.shape, 2)
    return jnp.where(pos < pad, -NEG, ck).reshape(B * FOX_H, 1, Lp)


def _fox_tril(TB):
    r = lax.broadcasted_iota(jnp.int32, (TB, TB), 0)
    c = lax.broadcasted_iota(jnp.int32, (TB, TB), 1)
    return r >= c


def _fox_fwd(qkv, cumT, *, B, Lp):
    TB = _fox_tb(Lp)
    NQ = Lp // TB
    T = B * Lp
    scale = FOX_DH ** -0.5

    def body(q_ref, k_ref, v_ref, ct_ref, o_ref, lse_ref):
        i = pl.program_id(2)
        q = q_ref[...]

        def block(j, carry, diag):
            m, l, acc = carry
            off = pl.multiple_of(j * TB, TB)
            k = k_ref[pl.ds(off, TB), :]
            v = v_ref[pl.ds(off, TB), :]
            s = _nt(q, k) * scale - ct_ref[0, :, pl.ds(off, TB)]
            if diag:
                s = jnp.where(_fox_tril(TB), s, NEG)
            m_new = jnp.maximum(m, jnp.max(s, axis=1, keepdims=True))
            p = jnp.exp(s - m_new)
            alpha = jnp.exp(m - m_new)
            l = alpha * l + jnp.sum(p, axis=1, keepdims=True)
            acc = alpha * acc + _nn(p.astype(BF16), v)
            return m_new, l, acc

        init = (jnp.full((TB, 1), NEG, F32), jnp.zeros((TB, 1), F32), jnp.zeros((TB, FOX_DH), F32))
        carry = lax.fori_loop(0, i, lambda j, c: block(j, c, False), init)
        m, l, acc = block(i, carry, True)
        o_ref[...] = (acc / l).astype(o_ref.dtype)
        lse_ref[0, 0] = m + jnp.log(l)

    return pl.pallas_call(
        body, name="fox_fwd", grid=(B, FOX_H, NQ),
        in_specs=[pl.BlockSpec((TB, FOX_DH), lambda b, h, i: (b * NQ + i, h)),
                  pl.BlockSpec((Lp, FOX_DH), lambda b, h, i: (b, FOX_H + h)),
                  pl.BlockSpec((Lp, FOX_DH), lambda b, h, i: (b, 2 * FOX_H + h)),
                  pl.BlockSpec((1, 1, Lp), lambda b, h, i: (b * FOX_H + h, 0, 0))],
        out_specs=[pl.BlockSpec((TB, FOX_DH), lambda b, h, i: (b * NQ + i, h)),
                   pl.BlockSpec((1, 1, TB, 1), lambda b, h, i: (b, h, i, 0))],
        out_shape=[jax.ShapeDtypeStruct((T, D), BF16),
                   jax.ShapeDtypeStruct((B, FOX_H, Lp, 1), F32)],
        compiler_params=_cp(("arbitrary", "arbitrary", "arbitrary")),
    )(qkv, qkv, qkv, cumT)


def _fox_bwd(qkv, dy, o, lse, cumT, *, B, Lp):
    TB = _fox_tb(Lp)
    NQ = Lp // TB
    T = B * Lp
    scale = FOX_DH ** -0.5

    def body(q_ref, k_ref, v_ref, dy_ref, o_ref, lse_ref, ct_ref, dq_ref, dk_ref, dv_ref, dck_ref, dcq_ref, dl_s):
        j = pl.program_id(2)
        k = k_ref[...]
        v = v_ref[...]
        ck = ct_ref[0]

        @pl.when(j == 0)
        def _():
            dq_ref[...] = jnp.zeros_like(dq_ref)
            dcq_ref[...] = jnp.zeros_like(dcq_ref)
            for i in range(NQ):
                sl = slice(i * TB, (i + 1) * TB)
                dl_s[sl, :] = jnp.sum(dy_ref[sl, :] * o_ref[sl, :].astype(F32), axis=1, keepdims=True)

        def block(i, carry, diag):
            dk, dv, dck = carry
            off = pl.multiple_of(i * TB, TB)
            q = q_ref[pl.ds(off, TB), :]
            dob = dy_ref[pl.ds(off, TB), :].astype(BF16)
            e = _nt(q, k) * scale - ck - lse_ref[0, 0, pl.ds(off, TB), :]
            if diag:
                e = jnp.where(_fox_tril(TB), e, NEG)
            p = jnp.exp(e)
            dv = dv + _tn(p.astype(BF16), dob)
            ds = p * (_nt(dob, v) - dl_s[pl.ds(off, TB), :])
            dsb = ds.astype(BF16)
            dk = dk + _tn(dsb, q)
            dq_ref[pl.ds(off, TB), :] += _nn(dsb, k) * scale
            dcq_ref[0, 0, pl.ds(off, TB), :] += jnp.sum(ds, axis=1, keepdims=True)
            dck = dck - jnp.sum(ds, axis=0, keepdims=True)
            return dk, dv, dck

        z = jnp.zeros((TB, FOX_DH), F32)
        carry = block(j, (z, z, jnp.zeros((1, TB), F32)), True)
        dk, dv, dck = lax.fori_loop(j + 1, NQ, lambda i, c: block(i, c, False), carry)
        dk_ref[...] = (dk * scale).astype(dk_ref.dtype)
        dv_ref[...] = dv.astype(dv_ref.dtype)
        dck_ref[0] = dck

    head = lambda b, h, j: (b, h)
    return pl.pallas_call(
        body, name="fox_bwd", grid=(B, FOX_H, NQ),
        in_specs=[pl.BlockSpec((Lp, FOX_DH), head),
                  pl.BlockSpec((TB, FOX_DH), lambda b, h, j: (b * NQ + j, FOX_H + h)),
                  pl.BlockSpec((TB, FOX_DH), lambda b, h, j: (b * NQ + j, 2 * FOX_H + h)),
                  pl.BlockSpec((Lp, FOX_DH), head),
                  pl.BlockSpec((Lp, FOX_DH), head),
                  pl.BlockSpec((1, 1, Lp, 1), lambda b, h, j: (b, h, 0, 0)),
                  pl.BlockSpec((1, 1, TB), lambda b, h, j: (b * FOX_H + h, 0, j))],
        out_specs=[pl.BlockSpec((Lp, FOX_DH), head),
                   pl.BlockSpec((TB, FOX_DH), lambda b, h, j: (b * NQ + j, h)),
                   pl.BlockSpec((TB, FOX_DH), lambda b, h, j: (b * NQ + j, h)),
                   pl.BlockSpec((1, 1, TB), lambda b, h, j: (b * FOX_H + h, 0, j)),
                   pl.BlockSpec((1, 1, Lp, 1), lambda b, h, j: (b, h, 0, 0))],
        out_shape=[jax.ShapeDtypeStruct((T, D), F32),
                   jax.ShapeDtypeStruct((T, D), BF16),
                   jax.ShapeDtypeStruct((T, D), BF16),
                   jax.ShapeDtypeStruct((B * FOX_H, 1, Lp), F32),
                   jax.ShapeDtypeStruct((B, FOX_H, Lp, 1), F32)],
        scratch_shapes=[pltpu.VMEM((Lp, 1), F32)],
        compiler_params=_cp(("arbitrary", "arbitrary", "arbitrary")),
    )(qkv, qkv, qkv, dy, o, lse, cumT)


S5_TILE = 8


def _s5_pows(lam_ref, pw, tab, reverse):
    lr = lam_ref[0, :, 0:S5_W]
    li = lam_ref[0, :, S5_W:2 * S5_W]
    if reverse:
        li = -li
    ar, ai = lr, li
    for k in range(3):
        pw[k:k + 1, 0:S5_W] = ar
        pw[k:k + 1, S5_W:2 * S5_W] = ai
        ar, ai = ar * ar - ai * ai, 2.0 * ar * ai
    ar, ai = lr, li
    for r in range(S5_TILE):
        row = (S5_TILE - 1 - r) if reverse else r
        tab[row:row + 1, 0:S5_W] = ar
        tab[row:row + 1, S5_W:2 * S5_W] = ai
        ar, ai = ar * lr - ai * li, ar * li + ai * lr


def _s5_scan_tiles(vr, vi, pw, reverse):
    n = vr.shape[0]
    sub = lax.broadcasted_iota(jnp.int32, (n, 1), 0) & (S5_TILE - 1)
    for k, s in enumerate((1, 2, 4)):
        if reverse:
            keep = sub < S5_TILE - s
            sr = jnp.where(keep, pltpu.roll(vr, n - s, 0), 0.0)
            si = jnp.where(keep, pltpu.roll(vi, n - s, 0), 0.0)
        else:
            keep = sub >= s
            sr = jnp.where(keep, pltpu.roll(vr, s, 0), 0.0)
            si = jnp.where(keep, pltpu.roll(vi, s, 0), 0.0)
        ar = pw[k:k + 1, 0:S5_W]
        ai = pw[k:k + 1, S5_W:2 * S5_W]
        vr, vi = vr + ar * sr - ai * si, vi + ar * si + ai * sr
    return vr, vi


def _s5_carry_tiles(hs, tab, carry, reverse):
    n = hs.shape[0]
    tr = tab[:, 0:S5_W]
    ti = tab[:, S5_W:2 * S5_W]
    cr = carry[:, 0:S5_W]
    ci = carry[:, S5_W:2 * S5_W]
    order = range(n // S5_TILE)
    for k in (reversed(order) if reverse else order):
        lo = k * S5_TILE
        hs[lo:lo + S5_TILE, 0:S5_W] = hs[lo:lo + S5_TILE, 0:S5_W] + tr * cr - ti * ci
        hs[lo:lo + S5_TILE, S5_W:2 * S5_W] = hs[lo:lo + S5_TILE, S5_W:2 * S5_W] + tr * ci + ti * cr
        edge = lo if reverse else lo + S5_TILE - 1
        cr = hs[edge:edge + 1, 0:S5_W]
        ci = hs[edge:edge + 1, S5_W:2 * S5_W]
    carry[:, 0:S5_W] = cr
    carry[:, S5_W:2 * S5_W] = ci


def _s5_fwd(u, Bsg, Csg, lam, *, B, NC):
    T = B * NC * CH

    def body(u_ref, b_ref, c_ref, lam_ref, y_ref, h_ref, pw, tab, hs, carry):
        cidx = pl.program_id(2)

        @pl.when(cidx == 0)
        def _():
            _s5_pows(lam_ref, pw, tab, False)
            carry[...] = jnp.zeros_like(carry)

        bu = _nn(u_ref[...].astype(BF16), b_ref[0])
        vr, vi = _s5_scan_tiles(bu[:, 0:S5_W], bu[:, S5_W:], pw, False)
        hs[:, 0:S5_W] = vr
        hs[:, S5_W:] = vi
        _s5_carry_tiles(hs, tab, carry, False)
        hb = hs[...].astype(BF16)
        h_ref[...] = hb
        y_ref[...] = _nn(hb, c_ref[0])

    return pl.pallas_call(
        body, name="s5_fwd", grid=(B, S5_G // S5_SG, NC),
        in_specs=[pl.BlockSpec((CH, CH), lambda b, s, c: (b * NC + c, s)),
                  pl.BlockSpec((1, CH, 2 * S5_W), lambda b, s, c: (s, 0, 0)),
                  pl.BlockSpec((1, 2 * S5_W, CH), lambda b, s, c: (s, 0, 0)),
                  pl.BlockSpec((1, 1, 2 * S5_W), lambda b, s, c: (s, 0, 0))],
        out_specs=[pl.BlockSpec((CH, CH), lambda b, s, c: (b * NC + c, s)),
                   pl.BlockSpec((CH, 2 * S5_W), lambda b, s, c: (b * NC + c, s))],
        out_shape=[jax.ShapeDtypeStruct((T, D), F32),
                   jax.ShapeDtypeStruct((T, (S5_G // S5_SG) * 2 * S5_W), BF16)],
        scratch_shapes=[pltpu.VMEM((8, 2 * S5_W), F32), pltpu.VMEM((S5_TILE, 2 * S5_W), F32),
                        pltpu.VMEM((CH, 2 * S5_W), F32), pltpu.VMEM((1, 2 * S5_W), F32)],
        compiler_params=_cp(("arbitrary", "arbitrary", "arbitrary")),
    )(u, Bsg, Csg, lam)


def _s5_bwd(u, hst, dy, du_skip, Bsg, Csg, lam, *, B, NC):
    T = B * NC * CH
    NS = S5_G // S5_SG
    hb16 = CH // 16

    def rix(b, c):
        return b * NC + (NC - 1 - c)

    def body(u_ref, h_ref, hp_ref, dy_ref, sk_ref, b_ref, c_ref, lam_ref,
             du_ref, db_ref, dc_ref, dl_ref, pw, tab, gs, carry):
        bidx = pl.program_id(1)
        cidx = pl.program_id(2)
        first = (bidx == 0) & (cidx == 0)

        @pl.when(cidx == 0)
        def _():
            _s5_pows(lam_ref, pw, tab, True)
            carry[...] = jnp.zeros_like(carry)

        dyb = dy_ref[...].astype(BF16)
        w = _nt(dyb, c_ref[0])
        vr, vi = _s5_scan_tiles(w[:, 0:S5_W], w[:, S5_W:], pw, True)
        gs[:, 0:S5_W] = vr
        gs[:, S5_W:] = vi
        _s5_carry_tiles(gs, tab, carry, True)
        gr = gs[:, 0:S5_W]
        gi = gs[:, S5_W:]
        gb = gs[...].astype(BF16)
        du_ref[...] = (_nt(gb, b_ref[0]) + sk_ref[...]).astype(du_ref.dtype)
        ub = u_ref[...].astype(BF16)
        hcur = h_ref[...]
        dB = _tn(ub, gb)
        dC = _tn(hcur, dyb)
        hf = hcur.astype(F32)
        row = lax.broadcasted_iota(jnp.int32, (CH, 1), 0)
        prev_last = jnp.where(cidx < NC - 1, hp_ref[15:16, :].astype(F32), 0.0)
        hprev = jnp.where(row == 0, prev_last, pltpu.roll(hf, 1, 0))
        pr = hprev[:, 0:S5_W]
        pi = hprev[:, S5_W:]
        da = _colsum(gr * pr + gi * pi)
        dbb = _colsum(gi * pr - gr * pi)
        dl = jnp.concatenate([da, dbb], axis=1)

        @pl.when(first)
        def _():
            db_ref[0] = dB
            dc_ref[0] = dC
            dl_ref[0] = dl

        @pl.when(jnp.logical_not(first))
        def _():
            db_ref[0] += dB
            dc_ref[0] += dC
            dl_ref[0] += dl

    return pl.pallas_call(
        body, name="s5_bwd", grid=(NS, B, NC),
        in_specs=[pl.BlockSpec((CH, CH), lambda s, b, c: (rix(b, c), s)),
                  pl.BlockSpec((CH, 2 * S5_W), lambda s, b, c: (rix(b, c), s)),
                  pl.BlockSpec((16, 2 * S5_W), lambda s, b, c: (jnp.maximum(rix(b, c) * hb16 - 1, 0), s)),
                  pl.BlockSpec((CH, CH), lambda s, b, c: (rix(b, c), s)),
                  pl.BlockSpec((CH, CH), lambda s, b, c: (rix(b, c), s)),
                  pl.BlockSpec((1, CH, 2 * S5_W), lambda s, b, c: (s, 0, 0)),
                  pl.BlockSpec((1, 2 * S5_W, CH), lambda s, b, c: (s, 0, 0)),
                  pl.BlockSpec((1, 1, 2 * S5_W), lambda s, b, c: (s, 0, 0))],
        out_specs=[pl.BlockSpec((CH, CH), lambda s, b, c: (rix(b, c), s)),
                   pl.BlockSpec((1, CH, 2 * S5_W), lambda s, b, c: (s, 0, 0)),
                   pl.BlockSpec((1, 2 * S5_W, CH), lambda s, b, c: (s, 0, 0)),
                   pl.BlockSpec((1, 1, 2 * S5_W), lambda s, b, c: (s, 0, 0))],
        out_shape=[jax.ShapeDtypeStruct((T, D), BF16),
                   jax.ShapeDtypeStruct((NS, CH, 2 * S5_W), F32),
                   jax.ShapeDtypeStruct((NS, 2 * S5_W, CH), F32),
                   jax.ShapeDtypeStruct((NS, 1, 2 * S5_W), F32)],
        scratch_shapes=[pltpu.VMEM((8, 2 * S5_W), F32), pltpu.VMEM((S5_TILE, 2 * S5_W), F32),
                        pltpu.VMEM((CH, 2 * S5_W), F32), pltpu.VMEM((1, 2 * S5_W), F32)],
        compiler_params=_cp(("arbitrary", "arbitrary", "arbitrary")),
    )(u, hst, hst, dy, du_skip, Bsg, Csg, lam)


def _s5_param_fn(lre, lim, lstep, bre, bim):
    step = jnp.exp(lstep)
    zr = lre * step
    zi = lim * step
    e = jnp.exp(zr)
    a = e * jnp.cos(zi)
    b = e * jnp.sin(zi)
    den = lre * lre + lim * lim
    qr = ((a - 1.0) * lre + b * lim) / den
    qi = (b * lre - (a - 1.0) * lim) / den
    return a, b, qr[None] * bre - qi[None] * bim, qr[None] * bim + qi[None] * bre


_S5_ROWS = S5_G * S5_P // CH


def _s5_tile(v):
    return v.reshape(_S5_ROWS, CH)


def _s5_tile_b(v):
    return v.reshape(S5_G * S5_P, S5_C).T.reshape(S5_C, _S5_ROWS, CH)


def _s5_untile_b(v):
    return v.reshape(S5_C, S5_G * S5_P).T.reshape(S5_G, S5_P, S5_C)


def _s5_params(lre, lim, lstep, bre, bim):
    def body(a_ref, b_ref, c_ref, d_ref, e_ref, o1, o2, o3, o4):
        outs = _s5_param_fn(a_ref[...], b_ref[...], c_ref[...], d_ref[...], e_ref[...])
        for o, v in zip((o1, o2, o3, o4), outs):
            o[...] = v

    shp = [jax.ShapeDtypeStruct(lre.shape, F32)] * 2 + [jax.ShapeDtypeStruct(bre.shape, F32)] * 2
    return pl.pallas_call(body, name="s5_params", out_shape=shp, compiler_params=_cp())(lre, lim, lstep, bre, bim)


def _s5_params_bwd(lre, lim, lstep, bre, bim, da, db, dbr, dbi):
    def body(a_ref, b_ref, c_ref, d_ref, e_ref, g1, g2, g3, g4, o1, o2, o3, o4, o5):
        _, vjp = jax.vjp(_s5_param_fn, a_ref[...], b_ref[...], c_ref[...], d_ref[...], e_ref[...])
        outs = vjp((g1[...], g2[...], g3[...], g4[...]))
        for o, v in zip((o1, o2, o3, o4, o5), outs):
            o[...] = v

    shp = [jax.ShapeDtypeStruct(lre.shape, F32)] * 3 + [jax.ShapeDtypeStruct(bre.shape, F32)] * 2
    return pl.pallas_call(body, name="s5_params_bwd", out_shape=shp, compiler_params=_cp())(
        lre, lim, lstep, bre, bim, da, db, dbr, dbi)


def _s5_blockdiag(br, bi, cre, cim):
    NS = S5_G // S5_SG
    eye = jnp.eye(S5_SG, dtype=F32)

    def bmat(v):
        v = v.reshape(NS, S5_SG, S5_P, S5_C)
        m = jnp.einsum("sgpc,gh->sgchp", v, eye)
        return m.reshape(NS, S5_SG * S5_C, S5_SG * S5_P)

    def cmat(v):
        v = v.reshape(NS, S5_SG, S5_C, S5_P)
        m = jnp.einsum("sgcp,gh->sgphc", v, eye)
        return m.reshape(NS, S5_SG * S5_P, S5_SG * S5_C)

    Bsg = jnp.concatenate([bmat(br), bmat(bi)], axis=2).astype(BF16)
    Csg = jnp.concatenate([cmat(cre), cmat(-cim)], axis=1).astype(BF16)
    return Bsg, Csg


def _s5_unblock(dBsg, dCsg):
    NS = S5_G // S5_SG

    def ub(m):
        m = m.reshape(NS, S5_SG, S5_C, S5_SG, S5_P)
        d = jnp.stack([m[:, g, :, g, :] for g in range(S5_SG)], axis=1)
        return d.transpose(0, 1, 3, 2).reshape(S5_G, S5_P, S5_C)

    def uc(m):
        m = m.reshape(NS, S5_SG, S5_P, S5_SG, S5_C)
        d = jnp.stack([m[:, g, :, g, :] for g in range(S5_SG)], axis=1)
        return d.transpose(0, 1, 3, 2).reshape(S5_G, S5_C, S5_P)

    dbr = ub(dBsg[:, :, 0:S5_W])
    dbi = ub(dBsg[:, :, S5_W:])
    dcr = uc(dCsg[:, 0:S5_W, :])
    dci = -uc(dCsg[:, S5_W:, :])
    return dbr, dbi, dcr, dci


def _loss_head(x, nf, target, *, B, NC, S):
    T = B * NC * CH
    nts = S // CH

    def f(xv, w, t):
        y = _rms(xv, w)
        return 0.5 * _colsum(jnp.mean(jnp.square(y - t), axis=-1, keepdims=True))

    def body(x_ref, w_ref, t_ref, dx_ref, ls_ref, dw_ref):
        i = pl.program_id(0)
        on = (i % NC) > 0
        t = t_ref[...]
        l, vjp = jax.vjp(lambda a, b: f(a, b, t), x_ref[...], w_ref[...])
        dx, dw = vjp(jnp.ones((1, 1), F32))
        g = jnp.where(on, 1.0, 0.0)
        dx_ref[...] = dx * g
        lv = jnp.zeros((1, CH), F32) + l * g

        @pl.when(i == 0)
        def _():
            ls_ref[...] = lv
            dw_ref[...] = dw * g

        @pl.when(i > 0)
        def _():
            ls_ref[...] += lv
            dw_ref[...] += dw * g

    def tix(i):
        return ((i // NC) * nts + jnp.maximum(i % NC - 1, 0), 0)

    return pl.pallas_call(
        body, name="loss_head", grid=(B * NC,),
        in_specs=[pl.BlockSpec((CH, D), lambda i: (i, 0)),
                  pl.BlockSpec((1, D), lambda i: (0, 0)),
                  pl.BlockSpec((CH, D), tix)],
        out_specs=[pl.BlockSpec((CH, D), lambda i: (i, 0)),
                   pl.BlockSpec((1, CH), lambda i: (0, 0)),
                   pl.BlockSpec((1, D), lambda i: (0, 0))],
        out_shape=[jax.ShapeDtypeStruct((T, D), F32),
                   jax.ShapeDtypeStruct((1, CH), F32),
                   jax.ShapeDtypeStruct((1, D), F32)],
        compiler_params=_cp(("arbitrary",)),
    )(x, nf, target)


def _ew(fn, ins, n_out, out_dtypes, *, name, tile=None):
    R, C = ins[0].shape
    tile = tile or _pick(R, (512, 256, 128, 64, 32, 16, 8, 1))
    if tile % 8 != 0:
        tile = R

    def body(*refs):
        outs = fn(*[r[...] for r in refs[:len(ins)]])
        if not isinstance(outs, (tuple, list)):
            outs = (outs,)
        for r, v in zip(refs[len(ins):], outs):
            r[...] = v.astype(r.dtype)

    spec = pl.BlockSpec((tile, C), lambda i: (i, 0))
    res = pl.pallas_call(
        body, name=name, grid=(R // tile,), in_specs=[spec] * len(ins), out_specs=[spec] * n_out,
        out_shape=[jax.ShapeDtypeStruct((R, C), dt) for dt in out_dtypes],
        compiler_params=_cp(("parallel",)),
    )(*ins)
    return res


def _adam_fn(w, g, m, v):
    m = ADAM_B1 * m + (1.0 - ADAM_B1) * g
    v = ADAM_B2 * v + (1.0 - ADAM_B2) * jnp.square(g)
    m_hat = m / (1.0 - ADAM_B1 ** ADAM_STEP)
    v_hat = v / (1.0 - ADAM_B2 ** ADAM_STEP)
    delta = -ADAM_LR * (m_hat / (jnp.sqrt(v_hat) + ADAM_EPS) + ADAM_WD * w)
    return delta, m, v


def _adam(w, g, m, v, name):
    shp = w.shape
    C = shp[-1]
    f = lambda a: a.reshape(-1, C)
    d, nm, nv = _ew(_adam_fn, [f(w), f(g), f(m), f(v)], 3, [F32] * 3, name=name)
    return d.reshape(shp), nm.reshape(shp), nv.reshape(shp)


def _me():
    return lax.axis_index("x"), lax.axis_index("y"), lax.axis_index("c")


def _all_gather(v, name):
    R, C = v.shape

    def body(x_ref, out_ref, send_sems, recv_sems, local_sem):
        x, y, c = _me()
        me, sibling = (x, y, c), (x, y, 1 - c)
        chips = [(1 - x, y), (x, 1 - y), (1 - x, 1 - y)]

        def slot(px, py, pc):
            return out_ref.at[4 * px + 2 * py + pc]

        def copy(k, block, to, src=None):
            return pltpu.make_async_remote_copy(
                src_ref=slot(*block) if src is None else src, dst_ref=slot(*block),
                send_sem=send_sems.at[k], recv_sem=recv_sems.at[k], device_id=to, device_id_type=MESH)

        mine = pltpu.make_async_copy(x_ref, slot(*me), local_sem)
        mine.start()
        first = [copy(0, me, sibling, src=x_ref)]
        first += [copy(1 + j, me, (*chip, c), src=x_ref) for j, chip in enumerate(chips)]
        for cp in first:
            cp.start()
        passed = [copy(4 + j, (*chip, c), sibling) for j, chip in enumerate(chips)]
        for j, chip in enumerate(chips):
            copy(1 + j, (*chip, c), me).wait_recv()
            passed[j].start()
        copy(0, sibling, me).wait_recv()
        for j, chip in enumerate(chips):
            copy(4 + j, (*chip, 1 - c), me).wait_recv()
        for cp in first + passed:
            cp.wait_send()
        mine.wait()

    return pl.pallas_call(
        body, name=name, out_shape=jax.ShapeDtypeStruct((N_DEV, R, C), v.dtype),
        in_specs=[pl.BlockSpec(memory_space=pl.ANY)], out_specs=pl.BlockSpec(memory_space=pl.ANY),
        scratch_shapes=[pltpu.SemaphoreType.DMA((7,)), pltpu.SemaphoreType.DMA((7,)), pltpu.SemaphoreType.DMA],
    )(v)


def _swap_core(g, name):
    _, _, R, C = g.shape

    def body(g_ref, out_ref, send_sems, recv_sems):
        x, y, c = _me()
        cps = [pltpu.make_async_remote_copy(
            src_ref=g_ref.at[q, 1 - c], dst_ref=out_ref.at[q], send_sem=send_sems.at[q], recv_sem=recv_sems.at[q],
            device_id=(x, y, 1 - c), device_id_type=MESH) for q in range(4)]
        for cp in cps:
            cp.start()
        for cp in cps:
            cp.wait()

    return pl.pallas_call(
        body, name=name, out_shape=jax.ShapeDtypeStruct((4, R, C), g.dtype),
        in_specs=[pl.BlockSpec(memory_space=pl.ANY)], out_specs=pl.BlockSpec(memory_space=pl.ANY),
        scratch_shapes=[pltpu.SemaphoreType.DMA((4,)), pltpu.SemaphoreType.DMA((4,))],
    )(g)


def _swap_chips(hb, name):
    _, R, C = hb.shape
    flips = [(1, 0), (0, 1), (1, 1)]

    def body(h_ref, out_ref, send_sems, recv_sems):
        x, y, c = _me()
        cps = []
        for j, (fx, fy) in enumerate(flips):
            px = x + fx - 2 * x * fx
            py = y + fy - 2 * y * fy
            cps.append(pltpu.make_async_remote_copy(
                src_ref=h_ref.at[2 * px + py], dst_ref=out_ref.at[j], send_sem=send_sems.at[j],
                recv_sem=recv_sems.at[j], device_id=(px, py, c), device_id_type=MESH))
        for cp in cps:
            cp.start()
        for cp in cps:
            cp.wait()

    return pl.pallas_call(
        body, name=name, out_shape=jax.ShapeDtypeStruct((3, R, C), hb.dtype),
        in_specs=[pl.BlockSpec(memory_space=pl.ANY)], out_specs=pl.BlockSpec(memory_space=pl.ANY),
        scratch_shapes=[pltpu.SemaphoreType.DMA((3,)), pltpu.SemaphoreType.DMA((3,))],
    )(hb)


def _reduce_scatter(g8):
    _, R, C = g8.shape
    x, y, c = _me()
    g4 = g8.reshape(4, 2, R, C)
    got = _swap_core(g4, "rs_core")
    mine = lax.dynamic_index_in_dim(g4, c, axis=1, keepdims=False)
    h, hb = _ew(lambda a, b: (a + b, a + b), [mine.reshape(4 * R, C), got.reshape(4 * R, C)], 2, [F32, BF16],
                name="rs_pair_sum")
    own = lax.dynamic_index_in_dim(h.reshape(4, R, C), 2 * x + y, axis=0, keepdims=False)
    got3 = _swap_chips(hb.reshape(4, R, C), "rs_chips")
    out, = _ew(lambda a, b, c_, d: a + b.astype(F32) + c_.astype(F32) + d.astype(F32),
               [own, got3[0], got3[1], got3[2]], 1, [F32], name="rs_chip_sum")
    return out


def _sum8(a):
    out, = _ew(lambda *v: functools.reduce(lambda p, q: p + q, v), [a[k] for k in range(N_DEV)], 1, [F32],
               name="sum8")
    return out


def _pad_rows(flat, cols, mult):
    n = flat.shape[0]
    per = cols * mult
    tot = ((n + per - 1) // per) * per
    return jnp.pad(flat, (0, tot - n)).reshape(-1, cols)


class _Packer:
    def __init__(self, shapes, mult):
        self.shapes = shapes
        self.sizes = [int(np.prod(s)) for s in shapes]
        self.mult = mult

    def pack(self, arrs, dtype):
        flat = jnp.concatenate([a.reshape(-1).astype(dtype) for a in arrs])
        return _pad_rows(flat, D, self.mult)

    def unpack(self, buf):
        flat = buf.reshape(-1)
        out, o = [], 0
        for s, n in zip(self.shapes, self.sizes):
            out.append(flat[o:o + n].reshape(s))
            o += n
        return out


def _w_cat(w_in_l):
    sm = jnp.concatenate([w_in_l[:, O_DT:O_DT + SSD_H], w_in_l[:, O_F:O_F + FOX_H],
                          jnp.zeros((D, CH - SSD_H - FOX_H), w_in_l.dtype)], axis=1)
    return jnp.concatenate([w_in_l[:, O_Z:O_XBC], w_in_l[:, O_XBC:O_DT], w_in_l[:, O_QKV:O_F],
                            w_in_l[:, O_U:O_G], w_in_l[:, O_G:D_IN], sm], axis=1)


def _w_uncat(g):
    return jnp.concatenate([g[:, OFF_Z:OFF_XBC], g[:, OFF_XBC:OFF_QKV], g[:, OFF_SM:OFF_SM + SSD_H],
                            g[:, OFF_QKV:OFF_U], g[:, OFF_SM + SSD_H:OFF_SM + SSD_H + FOX_H],
                            g[:, OFF_U:OFF_G], g[:, OFF_G:OFF_SM]], axis=1)


def _layer_fwd(x, p, geom, dims):
    B, NC, Lp, pad = dims
    T = geom[0]
    rm = functools.partial(_rowmap, geom=geom)
    sv = {}
    xn1, = rm(_f_norm, [x], [p["norm1"]], [(D, BF16)], [], tile=384 if Lp % 384 == 0 else CH, name="norm1")
    Wc = p["w_cat"]
    pz = _mm(xn1, Wc, "nn", F32, n=D, b_off=OFF_Z, name="in_z")
    pxbc = _mm(xn1, Wc, "nn", F32, n=CONV_DIM, b_off=OFF_XBC, name="in_xbc")
    qkv = _mm(xn1, Wc, "nn", BF16, n=3 * D, b_off=OFF_QKV, name="in_qkv")
    pu = _mm(xn1, Wc, "nn", F32, n=D, b_off=OFF_U, name="in_u")
    pg = _mm(xn1, Wc, "nn", F32, n=3 * D, b_off=OFF_G, name="in_g")
    psm = _mm(xn1, Wc, "nn", F32, n=CH, b_off=OFF_SM, name="in_sm")
    t_r = 384 if Lp % 384 == 0 else CH
    sm, = rm(_smallact, [psm], [p["smallbias"]], [(CH, F32)], [], tile=t_r, name="smallact")
    xbc = _conv_fwd(pxbc, p["conv_w"], p["conv_b"], geom=geom, tile=CH)
    y_ssd, states = _ssd_fwd(xbc, sm, p["a_log"], B=B, NC=NC)
    y_a, = rm(_ssd_post, [y_ssd, (xbc, D, 0), pz], [p["d_rep"], p["ssd_norm"]], [(D, BF16)], [], tile=t_r,
              name="ssd_post")
    cum = _cumsum_seq(sm, B=B, NC=NC, reverse=False, name="fox_cum")
    cumT = _fox_keybias(cum, B=B, Lp=Lp, pad=pad)
    y_b, lse = _fox_fwd(qkv, cumT, B=B, Lp=Lp)
    y_ssm, hst = _s5_fwd(pu, p["Bsg"], p["Csg"], p["lam"], B=B, NC=NC)
    y1, = rm(_s5_pre, [y_ssm, pu], [p["s5_d"]], [(D, BF16)], [], tile=t_r, name="s5_pre")
    tg = _mm(y1, p["w_glu"], "nn", F32, name="s5_glu_mm")
    y_c, = rm(_s5_glu, [y_ssm, pu, tg], [p["s5_d"]], [(D, BF16)], [], tile=t_r, name="s5_glu")
    br = [_mm(yy, p["w_branch"][n], "nn", F32, name=f"branch{n}") for n, yy in enumerate((y_a, y_b, y_c))]
    mix, = rm(_merge, [(pg, D, 0), (pg, D, 1), (pg, D, 2)] + br, [], [(D, BF16)], [], tile=CH, name="merge")
    x_mid = _mm(mix, p["w_out"], "nn", F32, res=x, name="out_proj")
    xn2, = rm(_f_norm, [x_mid], [p["norm2"]], [(D, BF16)], [], tile=t_r, name="norm2")
    hff = _mm(xn2, p["w_ffn_in"], "nn", F32, name="ffn_in")
    act, = rm(_swiglu, [(hff, DFF, 0), (hff, DFF, 1)], [], [(DFF, BF16)], [], tile=CH, name="swiglu")
    x_out = _mm(act, p["w_ffn_out"], "nn", F32, res=x_mid, name="ffn_out")
    sv.update(x=x, xn1=xn1, pz=pz, pxbc=pxbc, qkv=qkv, pu=pu, pg=pg, psm=psm, sm=sm, xbc=xbc, y_ssd=y_ssd,
              states=states, y_a=y_a, cum=cum, cumT=cumT, y_b=y_b, lse=lse, y_ssm=y_ssm, hst=hst, y1=y1, tg=tg,
              y_c=y_c, br=br, mix=mix, x_mid=x_mid, xn2=xn2, hff=hff, act=act)
    return x_out, sv


def _layer_bwd(dx_out, p, sv, geom, dims):
    B, NC, Lp, pad = dims
    T = geom[0]
    rm = functools.partial(_rowmap, geom=geom)
    t_r = 384 if Lp % 384 == 0 else CH
    g = {}
    dact = _mm(dx_out, p["w_ffn_out"], "nt", F32, name="ffn_out_dx")
    g["w_ffn_out"] = _mm(sv["act"], dx_out, "tn", F32, name="ffn_out_dw")
    dhff, = rm(_b_swiglu, [(sv["hff"], DFF, 0), (sv["hff"], DFF, 1), dact], [], [(2 * DFF, BF16)], [], tile=CH,
               name="swiglu_bwd")
    dxn2 = _mm(dhff, p["w_ffn_in"], "nt", F32, name="ffn_in_dx")
    g["w_ffn_in"] = _mm(sv["xn2"], dhff, "tn", F32, name="ffn_in_dw")
    dx_mid, g["norm2"] = rm(_b_norm, [sv["x_mid"], dxn2, dx_out], [p["norm2"]], [(D, F32)], [(1, D)], tile=t_r,
                            name="norm2_bwd")
    dmix = _mm(dx_mid, p["w_out"], "nt", F32, name="out_proj_dx")
    g["w_out"] = _mm(sv["mix"], dx_mid, "tn", F32, name="out_proj_dw")
    pg = sv["pg"]
    dpg, db0, db1, db2 = rm(_b_merge, [(pg, D, 0), (pg, D, 1), (pg, D, 2)] + sv["br"] + [dmix], [],
                            [(3 * D, BF16), (D, BF16), (D, BF16), (D, BF16)], [], tile=CH, name="merge_bwd")
    ys = (sv["y_a"], sv["y_b"], sv["y_c"])
    dbs = (db0, db1, db2)
    g["w_branch"] = [_mm(ys[n], dbs[n], "tn", F32, name=f"branch{n}_dw") for n in range(3)]
    dy = [_mm(dbs[n], p["w_branch"][n], "nt", F32, name=f"branch{n}_dx") for n in range(3)]
    dtg, dy1a = rm(_b_s5_glu, [sv["y_ssm"], sv["pu"], sv["tg"], dy[2]], [p["s5_d"]], [(D, BF16), (D, F32)], [],
                   tile=t_r, name="s5_glu_bwd")
    dy1b = _mm(dtg, p["w_glu"], "nt", F32, name="s5_glu_mm_dx")
    g["w_glu"] = _mm(sv["y1"], dtg, "tn", F32, name="s5_glu_mm_dw")
    dys, du_skip, g["s5_d"] = rm(_b_s5_pre, [sv["y_ssm"], sv["pu"], dy1a, dy1b], [p["s5_d"]],
                                 [(D, F32), (D, F32)], [(1, D)], tile=t_r, name="s5_pre_bwd")
    du, g["Bsg"], g["Csg"], g["lam"] = _s5_bwd(sv["pu"], sv["hst"], dys, du_skip, p["Bsg"], p["Csg"], p["lam"],
                                               B=B, NC=NC)
    dq, dk, dv, dckT, dcq = _fox_bwd(sv["qkv"], dy[1], sv["y_b"], sv["lse"], sv["cumT"], B=B, Lp=Lp)
    dcum8 = dcq.reshape(B, FOX_H, Lp).transpose(0, 2, 1) + dckT.reshape(B, FOX_H, Lp).transpose(0, 2, 1)
    dcum = jnp.pad(dcum8.reshape(T, FOX_H), ((0, 0), (SSD_H, CH - SSD_H - FOX_H)))
    dlogf = _cumsum_seq(dcum, B=B, NC=NC, reverse=True, name="fox_cum_bwd")
    dy_ssd, dxs_skip, dz, g["d_rep"], g["ssd_norm"] = rm(
        _b_ssd_post, [sv["y_ssd"], (sv["xbc"], D, 0), sv["pz"], dy[0]], [p["d_rep"], p["ssd_norm"]],
        [(D, F32), (D, F32), (D, BF16)], [(1, D), (1, D)], tile=t_r, name="ssd_post_bwd")
    dxbc_act, ddt, g["a_log"] = _ssd_bwd(sv["xbc"], sv["sm"], p["a_log"], sv["states"], dy_ssd, dxs_skip, B=B, NC=NC)
    dpsm, g["smallbias"] = rm(_b_smallact, [sv["psm"], ddt, dlogf], [p["smallbias"]], [(CH, BF16)], [(1, CH)],
                              tile=t_r, name="smallact_bwd")
    dconv, g["conv_w"], g["conv_b"] = _conv_bwd_pre(sv["pxbc"], dxbc_act, p["conv_w"], p["conv_b"], geom=geom, tile=CH)
    dpxbc = _conv_bwd_x(dconv, p["conv_w"], geom=geom, tile=CH)
    dproj = jnp.concatenate([dz, dpxbc, dq.astype(BF16), dk, dv, du, dpg, dpsm], axis=1)
    dxn1 = _mm(dproj, p["w_cat"], "nt", F32, name="in_dx")
    g["w_cat"] = _mm(sv["xn1"], dproj, "tn", F32, name="in_dw")
    dx_in, g["norm1"] = rm(_b_norm, [sv["x"], dxn1, dx_mid], [p["norm1"]], [(D, F32)], [(1, D)], tile=t_r,
                           name="norm1_bwd")
    return dx_in, g


_BIG = ["w_in", "s5_w_glu", "w_branch", "w_out", "w_ffn_in", "w_ffn_out"]
_NAMES = ['meta', 'norm1', 'w_in', 'ssd_conv_w', 'ssd_conv_b', 'ssd_dt_bias', 'ssd_a_log', 'ssd_d', 'ssd_norm',
          'fox_bf', 's5_lam_re', 's5_lam_im', 's5_b_re', 's5_b_im', 's5_c_re', 's5_c_im', 's5_log_step', 's5_d',
          's5_w_glu', 'w_branch', 'w_out', 'norm2', 'w_ffn_in', 'w_ffn_out', 'norm_f']
_SHARD_AXIS = {"meta": 1, "ssd_conv_w": 2}


def kernel(x, meta, norm1, w_in, ssd_conv_w, ssd_conv_b, ssd_dt_bias, ssd_a_log, ssd_d, ssd_norm, fox_bf, s5_lam_re, s5_lam_im, s5_b_re, s5_b_im, s5_c_re, s5_c_im, s5_log_step, s5_d, s5_w_glu, w_branch, w_out, norm2, w_ffn_in, w_ffn_out, norm_f, loss_target, m_meta, m_norm1, m_w_in, m_ssd_conv_w, m_ssd_conv_b, m_ssd_dt_bias, m_ssd_a_log, m_ssd_d, m_ssd_norm, m_fox_bf, m_s5_lam_re, m_s5_lam_im, m_s5_b_re, m_s5_b_im, m_s5_c_re, m_s5_c_im, m_s5_log_step, m_s5_d, m_s5_w_glu, m_w_branch, m_w_out, m_norm2, m_w_ffn_in, m_w_ffn_out, m_norm_f, v_meta, v_norm1, v_w_in, v_ssd_conv_w, v_ssd_conv_b, v_ssd_dt_bias, v_ssd_a_log, v_ssd_d, v_ssd_norm, v_fox_bf, v_s5_lam_re, v_s5_lam_im, v_s5_b_re, v_s5_b_im, v_s5_c_re, v_s5_c_im, v_s5_log_step, v_s5_d, v_s5_w_glu, v_w_branch, v_w_out, v_norm2, v_w_ffn_in, v_w_ffn_out, v_norm_f):
    args = locals()
    W = {n: args[n] for n in _NAMES}
    Mo = {n: args["m_" + n] for n in _NAMES}
    Vo = {n: args["v_" + n] for n in _NAMES}
    B, S, _ = x.shape
    depth = norm1.shape[0]
    L = S + N_META
    Lp = ((L + CH - 1) // CH) * CH
    pad = Lp - L
    assert pad + N_META == CH and S % CH == 0
    NC = Lp // CH
    T = B * Lp
    geom = (T, Lp, pad)
    dims = (B, NC, Lp, pad)
    xi, yi, ci = _me()
    dev = 4 * xi + 2 * yi + ci

    big_pack = _Packer([W[n].shape for n in _BIG], 512)
    gathered = _all_gather(big_pack.pack([W[n] for n in _BIG], BF16), "gather_weights")
    parts = [big_pack.unpack(gathered[k]) for k in range(N_DEV)]
    full = {}
    full["w_in"] = jnp.concatenate([parts[k][0] for k in range(N_DEV)], axis=2)
    full["s5_w_glu"] = jnp.concatenate([parts[k][1] for k in range(N_DEV)], axis=1)
    full["w_branch"] = jnp.concatenate([parts[k][2] for k in range(N_DEV)], axis=2)
    full["w_out"] = jnp.concatenate([parts[k][3] for k in range(N_DEV)], axis=1)
    full["w_ffn_in"] = jnp.concatenate([parts[k][4] for k in range(N_DEV)], axis=2)
    full["w_ffn_out"] = jnp.concatenate([parts[k][5] for k in range(N_DEV)], axis=1)
    sm_pack = _Packer([meta.shape, ssd_conv_w.shape], 8)
    sm_g = _all_gather(sm_pack.pack([meta, ssd_conv_w], F32), "gather_small")
    sm_parts = [sm_pack.unpack(sm_g[k]) for k in range(N_DEV)]
    meta_full = jnp.concatenate([sm_parts[k][0] for k in range(N_DEV)], axis=1)
    conv_w_full = jnp.concatenate([sm_parts[k][1] for k in range(N_DEV)], axis=2)

    layers = []
    s5_in = []
    for l in range(depth):
        lre = _s5_tile(s5_lam_re[l])
        lim = _s5_tile(s5_lam_im[l])
        lst = _s5_tile(jnp.repeat(s5_log_step[l], S5_P))
        bre = _s5_tile_b(s5_b_re[l])
        bim = _s5_tile_b(s5_b_im[l])
        s5_in.append((lre, lim, lst, bre, bim))
        a, b, br_, bi_ = _s5_params(lre, lim, lst, bre, bim)
        Bsg, Csg = _s5_blockdiag(_s5_untile_b(br_), _s5_untile_b(bi_), s5_c_re[l], s5_c_im[l])
        NS = S5_G // S5_SG
        lam = jnp.concatenate([a.reshape(NS, 1, S5_W), b.reshape(NS, 1, S5_W)], axis=2)
        zpad = jnp.zeros((CH - SSD_H - FOX_H,), F32)
        layers.append(dict(
            norm1=norm1[l][None], w_cat=_w_cat(full["w_in"][l]),
            smallbias=jnp.concatenate([ssd_dt_bias[l], fox_bf[l], zpad])[None],
            conv_w=conv_w_full[l], conv_b=ssd_conv_b[l][None],
            a_log=jnp.concatenate([ssd_a_log[l], jnp.zeros((CH - SSD_H,), F32)])[None],
            d_rep=jnp.repeat(ssd_d[l], SSD_P)[None], ssd_norm=ssd_norm[l][None],
            Bsg=Bsg, Csg=Csg, lam=lam, s5_d=s5_d[l][None], w_glu=full["s5_w_glu"][l],
            w_branch=[full["w_branch"][l, n] for n in range(3)], w_out=full["w_out"][l],
            norm2=norm2[l][None], w_ffn_in=full["w_ffn_in"][l], w_ffn_out=full["w_ffn_out"][l]))

    xs = jnp.concatenate([jnp.zeros((B, pad, D), F32), jnp.broadcast_to(meta_full[None], (B, N_META, D)), x], axis=1)
    h = xs.reshape(T, D)
    saved = []
    for l in range(depth):
        h, sv = _layer_fwd(h, layers[l], geom, dims)
        saved.append(sv)
    dh, loss_row, g_nf = _loss_head(h, norm_f[None], loss_target.reshape(B * S, D), B=B, NC=NC, S=S)
    loss = lax.psum(loss_row[0, 0], AXES)

    G = {n: [None] * depth for n in _NAMES}
    for l in reversed(range(depth)):
        dh, g = _layer_bwd(dh, layers[l], saved[l], geom, dims)
        saved[l] = None
        G["norm1"][l] = g["norm1"][0]
        G["norm2"][l] = g["norm2"][0]
        G["w_in"][l] = _w_uncat(g["w_cat"])
        G["ssd_conv_w"][l] = g["conv_w"]
        G["ssd_conv_b"][l] = g["conv_b"][0]
        G["ssd_dt_bias"][l] = g["smallbias"][0, 0:SSD_H]
        G["fox_bf"][l] = g["smallbias"][0, SSD_H:SSD_H + FOX_H]
        G["ssd_a_log"][l] = g["a_log"][0, 0:SSD_H]
        G["ssd_d"][l] = g["d_rep"].reshape(SSD_H, SSD_P).sum(axis=1)
        G["ssd_norm"][l] = g["ssd_norm"][0]
        dbr, dbi, dcr, dci = _s5_unblock(g["Bsg"], g["Csg"])
        da = _s5_tile(g["lam"][:, 0, 0:S5_W])
        db = _s5_tile(g["lam"][:, 0, S5_W:])
        dlre, dlim, dlst, dbre, dbim = _s5_params_bwd(*s5_in[l], da, db, _s5_tile_b(dbr), _s5_tile_b(dbi))
        G["s5_lam_re"][l] = dlre.reshape(S5_G, S5_P)
        G["s5_lam_im"][l] = dlim.reshape(S5_G, S5_P)
        G["s5_log_step"][l] = dlst.reshape(S5_G, S5_P).sum(axis=1)
        G["s5_b_re"][l] = _s5_untile_b(dbre)
        G["s5_b_im"][l] = _s5_untile_b(dbim)
        G["s5_c_re"][l] = dcr
        G["s5_c_im"][l] = dci
        G["s5_d"][l] = g["s5_d"][0]
        G["s5_w_glu"][l] = g["w_glu"]
        G["w_branch"][l] = jnp.stack(g["w_branch"])
        G["w_out"][l] = g["w_out"]
        G["w_ffn_in"][l] = g["w_ffn_in"]
        G["w_ffn_out"][l] = g["w_ffn_out"]
    dxs = dh.reshape(B, Lp, D)
    grad_x = dxs[:, pad + N_META:, :]
    part = {n: jnp.stack(G[n]) for n in _NAMES if n not in ("meta", "norm_f")}
    part["meta"] = dxs[:, pad:pad + N_META, :].sum(axis=0)
    part["norm_f"] = g_nf[0]

    def shard(n, k):
        a = part[n]
        if n == "w_in":
            return a.reshape(depth, D, N_DEV, -1)[:, :, k]
        if n == "w_ffn_in":
            return a.reshape(depth, D, N_DEV, -1)[:, :, k]
        if n == "w_branch":
            return a.reshape(depth, 3, N_DEV, -1, D)[:, :, k]
        return a.reshape(depth, N_DEV, -1, D)[:, k]

    g8 = jnp.stack([big_pack.pack([shard(n, k) for n in _BIG], F32) for k in range(N_DEV)])
    big_red = big_pack.unpack(_reduce_scatter(g8))
    grads = dict(zip(_BIG, big_red))

    small = [n for n in _NAMES if n not in _BIG]
    sp = _Packer([part[n].shape for n in small], 128)
    tot = sp.unpack(_sum8(_all_gather(sp.pack([part[n] for n in small], F32), "gather_small_grads")))
    for n, t in zip(small, tot):
        if n in _SHARD_AXIS:
            ax = _SHARD_AXIS[n]
            w = W[n].shape[ax]
            t = lax.dynamic_slice_in_dim(t, dev * w, w, axis=ax)
        grads[n] = t

    delta, new_m, new_v = {}, {}, {}
    for n in _BIG:
        delta[n], new_m[n], new_v[n] = _adam(W[n], grads[n], Mo[n], Vo[n], "adam_" + n)
    ap = _Packer([W[n].shape for n in small], 128)
    d_, m_, v_ = _adam(ap.pack([W[n] for n in small], F32), ap.pack([grads[n] for n in small], F32),
                       ap.pack([Mo[n] for n in small], F32), ap.pack([Vo[n] for n in small], F32), "adam_small")
    for n, a, b, c in zip(small, ap.unpack(d_), ap.unpack(m_), ap.unpack(v_)):
        delta[n], new_m[n], new_v[n] = a, b, c
    return (loss, grad_x, *[grads[n] for n in _NAMES], *[delta[n] for n in _NAMES],
            *[new_m[n] for n in _NAMES], *[new_v[n] for n in _NAMES])
```

```python
import functools
import math

import numpy as np
import jax
import jax.numpy as jnp
from jax import lax
from jax.experimental import pallas as pl
from jax.experimental.pallas import tpu as pltpu

F32 = jnp.float32
BF16 = jnp.bfloat16
AXES = ("x", "y", "c")
MESH = pl.DeviceIdType.MESH
N_DEV = 8

D = 1024
N_META = 16
CH = 128
EPS = 1e-6
NEG = -1e30
SSD_H, SSD_P, SSD_N, SSD_G = 16, 64, 128, 2
CONV_K, CONV_DIM = 4, 1536
FOX_H, FOX_DH = 8, 128
S5_G, S5_P, S5_C = 64, 64, 16
S5_SG = 8
S5_W = S5_SG * S5_P
DFF = 2816
D_IN = 9752
OFF_Z, OFF_XBC, OFF_QKV, OFF_U, OFF_G, OFF_SM, D_CAT = 0, 1024, 2560, 5632, 6656, 9728, 9856
O_Z, O_XBC, O_DT, O_QKV, O_F, O_U, O_G = 0, 1024, 2560, 2576, 5648, 5656, 6680

ADAM_LR, ADAM_B1, ADAM_B2, ADAM_EPS, ADAM_WD, ADAM_STEP = 0.001, 0.9, 0.999, 1e-08, 0.01, 10

VMEM_LIMIT_V7X = 52 * 1024 * 1024
HI = lax.Precision.HIGHEST


def _cp(sem=None):
    return pltpu.CompilerParams(dimension_semantics=sem, vmem_limit_bytes=VMEM_LIMIT_V7X)


def _pick(n, cands):
    for c in cands:
        if n % c == 0:
            return c
    raise ValueError(f"no tile for {n}")


_TILES = (1408, 1024, 896, 768, 512, 384, 256, 128)


def _mm(a, b, mode, out_dtype, *, name, n=None, b_off=0, res=None, tm=None, tn=None, tk=None):
    if mode == "tn":
        K, M = a.shape
    else:
        M, K = a.shape
    if mode == "nt":
        N = b.shape[0]
    else:
        N = n if n is not None else b.shape[1]
    tm = tm or _pick(M, (1024, 768, 512, 384, 256, 128, 64, 16, 8))
    tn = tn or _pick(math.gcd(N, b_off) if b_off else N, (1024, 896, 768, 512, 384, 256, 128))
    tk = tk or _pick(K, _TILES)
    nk = K // tk
    joff = b_off // tn

    def body(*refs):
        if res is None:
            a_ref, b_ref, o_ref, acc = refs
            r_ref = None
        else:
            a_ref, b_ref, r_ref, o_ref, acc = refs
        k = pl.program_id(2)
        av = a_ref[...].astype(BF16)
        bv = b_ref[...].astype(BF16)
        if mode == "nn":
            p = jnp.dot(av, bv, preferred_element_type=F32)
        elif mode == "nt":
            p = lax.dot_general(av, bv, (((1,), (1,)), ((), ())), preferred_element_type=F32)
        else:
            p = lax.dot_general(av, bv, (((0,), (0,)), ((), ())), preferred_element_type=F32)

        @pl.when(k == 0)
        def _():
            acc[...] = p

        @pl.when(k > 0)
        def _():
            acc[...] += p

        @pl.when(k == nk - 1)
        def _():
            r = acc[...]
            if r_ref is not None:
                r = r + r_ref[...]
            o_ref[...] = r.astype(o_ref.dtype)

    if mode == "tn":
        a_spec = pl.BlockSpec((tk, tm), lambda i, j, k: (k, i))
    else:
        a_spec = pl.BlockSpec((tm, tk), lambda i, j, k: (i, k))
    if mode == "nt":
        b_spec = pl.BlockSpec((tn, tk), lambda i, j, k: (j, k))
    else:
        b_spec = pl.BlockSpec((tk, tn), lambda i, j, k: (k, j + joff))
    o_spec = pl.BlockSpec((tm, tn), lambda i, j, k: (i, j))
    in_specs = [a_spec, b_spec] + ([o_spec] if res is not None else [])
    args = (a, b) + ((res,) if res is not None else ())
    return pl.pallas_call(
        body, name=name, grid=(M // tm, N // tn, nk),
        in_specs=in_specs, out_specs=o_spec,
        out_shape=jax.ShapeDtypeStruct((M, N), out_dtype),
        scratch_shapes=[pltpu.VMEM((tm, tn), F32)],
        compiler_params=_cp(("parallel", "parallel", "arbitrary")),
    )(*args)


def _rowmap(fn, row_ins, const_ins, row_outs, acc_outs, *, geom, tile, name):
    T, Lp, pad = geom
    assert Lp % tile == 0
    per_seq = Lp // tile
    specs, args = [], []
    for r in row_ins:
        arr, w, cb = r if isinstance(r, tuple) else (r, r.shape[1], 0)
        specs.append(pl.BlockSpec((tile, w), functools.partial(lambda i, cb: (i, cb), cb=cb)))
        args.append(arr)
    for c in const_ins:
        specs.append(pl.BlockSpec(c.shape, functools.partial(lambda i, nd: (0,) * nd, nd=c.ndim)))
        args.append(c)
    n_r, n_c, n_o, n_a = len(row_ins), len(const_ins), len(row_outs), len(acc_outs)
    out_specs = [pl.BlockSpec((tile, w), lambda i: (i, 0)) for w, _ in row_outs]
    out_specs += [pl.BlockSpec(s, lambda i: (0, 0)) for s in acc_outs]
    out_shape = [jax.ShapeDtypeStruct((T, w), dt) for w, dt in row_outs]
    out_shape += [jax.ShapeDtypeStruct(s, F32) for s in acc_outs]

    def body(*refs):
        i = pl.program_id(0)
        pos = (i % per_seq) * tile + lax.broadcasted_iota(jnp.int32, (tile, 1), 0)
        valid = pos >= pad
        vals = [r[...] for r in refs[:n_r + n_c]]
        outs = fn(valid, *vals)
        if not isinstance(outs, (tuple, list)):
            outs = (outs,)
        orefs = refs[n_r + n_c:]
        for r, v in zip(orefs[:n_o], outs[:n_o]):
            r[...] = v.astype(r.dtype)
        for r, v in zip(orefs[n_o:], outs[n_o:]):
            @pl.when(i == 0)
            def _(r=r, v=v):
                r[...] = v

            @pl.when(i > 0)
            def _(r=r, v=v):
                r[...] += v

    res = pl.pallas_call(
        body, name=name, grid=(T // tile,), in_specs=specs, out_specs=out_specs, out_shape=out_shape,
        compiler_params=_cp(("arbitrary",)),
    )(*args)
    return res


def _sigmoid(x):
    return 1.0 / (1.0 + jnp.exp(-x))


def _silu(x):
    return x * _sigmoid(x)


def _softplus(x):
    return jnp.maximum(x, 0.0) + jnp.log(1.0 + jnp.exp(-jnp.abs(x)))


def _gelu(x):
    return 0.5 * x * (1.0 + jnp.tanh(math.sqrt(2.0 / math.pi) * (x + 0.044715 * x * x * x)))


def _rms(x, w):
    return x * lax.rsqrt(jnp.mean(x * x, axis=-1, keepdims=True) + EPS) * w


def _colsum(v):
    return jnp.sum(v, axis=0, keepdims=True)


def _f_norm(valid, x, w):
    return _rms(x, w)


def _b_norm(valid, x, dxn, dres, w):
    _, vjp = jax.vjp(_rms, x, w)
    dx, dw = vjp(dxn)
    return jnp.where(valid, dx + dres, 0.0), dw


def _smallact(valid, raw, bias):
    lane = lax.broadcasted_iota(jnp.int32, raw.shape, 1)
    v = raw + bias
    dt = _softplus(v)
    logf = -_softplus(-v)
    out = jnp.where(lane < SSD_H, dt, jnp.where(lane < SSD_H + FOX_H, logf, 0.0))
    return jnp.where(valid, out, 0.0)


def _b_smallact(valid, raw, d1, d2, bias):
    _, vjp = jax.vjp(lambda r, b: _smallact(valid, r, b), raw, bias)
    return vjp(d1 + d2)


def _ssd_post(valid, y, xs, z, drep, nw):
    y = (y + xs * drep) * _silu(z)
    return _rms(y, nw)


def _b_ssd_post(valid, y, xs, z, dya, drep, nw):
    _, vjp = jax.vjp(lambda a, b, c, d, e: _ssd_post(valid, a, b, c, d, e), y, xs, z, drep, nw)
    dy, dxs, dz, dd, dn = vjp(dya)
    return dy, dxs, dz, dd, dn


def _s5_pre(valid, ys, u, d):
    return _gelu(ys + d * u)


def _s5_glu(valid, ys, u, t, d):
    y1 = _gelu(ys + d * u)
    return y1 * _sigmoid(t)


def _b_s5_glu(valid, ys, u, t, dyc, d):
    y1 = _gelu(ys + d * u)
    _, vjp = jax.vjp(lambda a, b: a * _sigmoid(b), y1, t)
    dy1, dt = vjp(dyc)
    return dt, dy1


def _b_s5_pre(valid, ys, u, dy1a, dy1b, d):
    _, vjp = jax.vjp(lambda a, b, c: _gelu(a + c * b), ys, u, d)
    dys, du, dd = vjp(dy1a + dy1b)
    return dys, du, dd


def _merge(valid, g0, g1, g2, b0, b1, b2):
    m = _sigmoid(g0) * b0 + _sigmoid(g1) * b1 + _sigmoid(g2) * b2
    return jnp.where(valid, m, 0.0)


def _b_merge(valid, g0, g1, g2, b0, b1, b2, dmix):
    _, vjp = jax.vjp(lambda *a: _merge(valid, *a), g0, g1, g2, b0, b1, b2)
    d = vjp(dmix)
    return jnp.concatenate(d[:3], axis=1), d[3], d[4], d[5]


def _swiglu(valid, g, up):
    return _silu(g) * up


def _b_swiglu(valid, g, up, dact):
    _, vjp = jax.vjp(lambda a, b: _silu(a) * b, g, up)
    dg, dup = vjp(dact)
    return jnp.concatenate([dg, dup], axis=1)


def _conv_taps(ext, tile):
    taps = []
    for k in range(CONV_K):
        sh = CONV_K - 1 - k
        v = ext if sh == 0 else pltpu.roll(ext, sh, 0)
        taps.append(v[8:8 + tile])
    return taps


def _conv_fwd(x, w, b, *, geom, tile):
    T, Lp, pad = geom
    per_seq = Lp // tile
    hb = tile // 8

    def body(x_ref, h_ref, w_ref, b_ref, o_ref):
        i = pl.program_id(0)
        pos = (i % per_seq) * tile + lax.broadcasted_iota(jnp.int32, (tile, 1), 0)
        ext = jnp.concatenate([h_ref[...], x_ref[...]], axis=0)
        taps = _conv_taps(ext, tile)
        acc = b_ref[...] + taps[0] * w_ref[0:1, :]
        for k in range(1, CONV_K):
            acc = acc + taps[k] * w_ref[k:k + 1, :]
        o_ref[...] = jnp.where(pos >= pad, _silu(acc), 0.0)

    return pl.pallas_call(
        body, name="conv_fwd", grid=(T // tile,),
        in_specs=[pl.BlockSpec((tile, CONV_DIM), lambda i: (i, 0)),
                  pl.BlockSpec((8, CONV_DIM), lambda i: (jnp.maximum(i * hb - 1, 0), 0)),
                  pl.BlockSpec((CONV_K, CONV_DIM), lambda i: (0, 0)),
                  pl.BlockSpec((1, CONV_DIM), lambda i: (0, 0))],
        out_specs=pl.BlockSpec((tile, CONV_DIM), lambda i: (i, 0)),
        out_shape=jax.ShapeDtypeStruct((T, CONV_DIM), F32),
        compiler_params=_cp(("arbitrary",)),
    )(x, x, w, b)


def _conv_bwd_pre(x, dact, w, b, *, geom, tile):
    T, Lp, pad = geom
    per_seq = Lp // tile
    hb = tile // 8

    def body(x_ref, h_ref, d_ref, w_ref, b_ref, dc_ref, dw_ref, db_ref):
        i = pl.program_id(0)
        pos = (i % per_seq) * tile + lax.broadcasted_iota(jnp.int32, (tile, 1), 0)
        ext = jnp.concatenate([h_ref[...], x_ref[...]], axis=0)
        taps = _conv_taps(ext, tile)
        acc = b_ref[...] + taps[0] * w_ref[0:1, :]
        for k in range(1, CONV_K):
            acc = acc + taps[k] * w_ref[k:k + 1, :]
        sg = _sigmoid(acc)
        dsilu = sg * (1.0 + acc * (1.0 - sg))
        dc = jnp.where(pos >= pad, d_ref[...] * dsilu, 0.0)
        dc_ref[...] = dc
        dw = jnp.concatenate([_colsum(dc * taps[k]) for k in range(CONV_K)], axis=0)
        db = _colsum(dc)

        @pl.when(i == 0)
        def _():
            dw_ref[...] = dw
            db_ref[...] = db

        @pl.when(i > 0)
        def _():
            dw_ref[...] += dw
            db_ref[...] += db

    return pl.pallas_call(
        body, name="conv_bwd_pre", grid=(T // tile,),
        in_specs=[pl.BlockSpec((tile, CONV_DIM), lambda i: (i, 0)),
                  pl.BlockSpec((8, CONV_DIM), lambda i: (jnp.maximum(i * hb - 1, 0), 0)),
                  pl.BlockSpec((tile, CONV_DIM), lambda i: (i, 0)),
                  pl.BlockSpec((CONV_K, CONV_DIM), lambda i: (0, 0)),
                  pl.BlockSpec((1, CONV_DIM), lambda i: (0, 0))],
        out_specs=[pl.BlockSpec((tile, CONV_DIM), lambda i: (i, 0)),
                   pl.BlockSpec((CONV_K, CONV_DIM), lambda i: (0, 0)),
                   pl.BlockSpec((1, CONV_DIM), lambda i: (0, 0))],
        out_shape=[jax.ShapeDtypeStruct((T, CONV_DIM), F32),
                   jax.ShapeDtypeStruct((CONV_K, CONV_DIM), F32),
                   jax.ShapeDtypeStruct((1, CONV_DIM), F32)],
        compiler_params=_cp(("arbitrary",)),
    )(x, x, dact, w, b)


def _conv_bwd_x(dc, w, *, geom, tile):
    T, Lp, pad = geom
    nt = T // tile
    hb = tile // 8

    def body(d_ref, h_ref, w_ref, o_ref):
        i = pl.program_id(0)
        halo = jnp.where(i < nt - 1, h_ref[...], 0.0)
        ext = jnp.concatenate([d_ref[...], halo], axis=0)
        n_ext = tile + 8
        acc = ext[0:tile] * w_ref[CONV_K - 1:CONV_K, :]
        for j in range(1, CONV_K):
            acc = acc + pltpu.roll(ext, n_ext - j, 0)[0:tile] * w_ref[CONV_K - 1 - j:CONV_K - j, :]
        o_ref[...] = acc.astype(o_ref.dtype)

    return pl.pallas_call(
        body, name="conv_bwd_x", grid=(nt,),
        in_specs=[pl.BlockSpec((tile, CONV_DIM), lambda i: (i, 0)),
                  pl.BlockSpec((8, CONV_DIM), lambda i: (jnp.minimum((i + 1) * hb, nt * hb - 1), 0)),
                  pl.BlockSpec((CONV_K, CONV_DIM), lambda i: (0, 0))],
        out_specs=pl.BlockSpec((tile, CONV_DIM), lambda i: (i, 0)),
        out_shape=jax.ShapeDtypeStruct((T, CONV_DIM), BF16),
        compiler_params=_cp(("arbitrary",)),
    )(dc, dc, w)


def _ssd_common(sm_ref, alog_ref):
    lane = lax.broadcasted_iota(jnp.int32, (1, CH), 1)
    A = jnp.where(lane < SSD_H, -jnp.exp(alog_ref[...]), 0.0)
    dt = sm_ref[...]
    adt = dt * A
    r = lax.broadcasted_iota(jnp.int32, (CH, CH), 0)
    c = lax.broadcasted_iota(jnp.int32, (CH, CH), 1)
    tril = (r >= c).astype(F32)
    cs = jnp.dot(tril, adt, precision=HI, preferred_element_type=F32)
    csT = cs.T
    cs_last = jnp.sum(jnp.where(r == CH - 1, cs, 0.0), axis=0, keepdims=True)
    return A, dt, cs, csT, cs_last, tril, r, c


def _nt(a, b):
    return lax.dot_general(a, b, (((1,), (1,)), ((), ())), preferred_element_type=F32)


def _tn(a, b):
    return lax.dot_general(a, b, (((0,), (0,)), ((), ())), preferred_element_type=F32)


def _nn(a, b):
    return jnp.dot(a, b, preferred_element_type=F32)


def _ssd_fwd(xbc, sm, alog, *, B, NC):
    T = B * NC * CH

    def body(x_ref, sm_ref, alog_ref, y_ref, st_ref, S):
        cidx = pl.program_id(1)

        @pl.when(cidx == 0)
        def _():
            S[...] = jnp.zeros_like(S)

        st_ref[0] = S[...]
        A, dt, cs, csT, cs_last, tril, _, _ = _ssd_common(sm_ref, alog_ref)
        for g in range(SSD_G):
            Bg = x_ref[:, D + g * SSD_N:D + (g + 1) * SSD_N]
            Cg = x_ref[:, D + SSD_G * SSD_N + g * SSD_N:D + SSD_G * SSD_N + (g + 1) * SSD_N]
            Cb = Cg.astype(BF16)
            G = _nt(Cb, Bg.astype(BF16))
            for rr in range(SSD_H // SSD_G):
                h = g * (SSD_H // SSD_G) + rr
                col = cs[:, h:h + 1]
                row = csT[h:h + 1, :]
                Ld = jnp.where(tril > 0, jnp.exp(jnp.minimum(col - row, 0.0)), 0.0)
                M = (G * Ld).astype(BF16)
                xdt = (x_ref[:, h * SSD_P:(h + 1) * SSD_P] * dt[:, h:h + 1]).astype(BF16)
                ST = S[h]
                y = _nn(M, xdt) + jnp.exp(col) * _nt(Cb, ST.astype(BF16))
                y_ref[:, h * SSD_P:(h + 1) * SSD_P] = y
                cl = cs_last[:, h:h + 1]
                Bd = (Bg * jnp.exp(cl - col)).astype(BF16)
                S[h] = jnp.exp(cl) * ST + _tn(xdt, Bd)

    return pl.pallas_call(
        body, name="ssd_fwd", grid=(B, NC),
        in_specs=[pl.BlockSpec((CH, CONV_DIM), lambda b, c: (b * NC + c, 0)),
                  pl.BlockSpec((CH, CH), lambda b, c: (b * NC + c, 0)),
                  pl.BlockSpec((1, CH), lambda b, c: (0, 0))],
        out_specs=[pl.BlockSpec((CH, D), lambda b, c: (b * NC + c, 0)),
                   pl.BlockSpec((1, SSD_H, SSD_P, SSD_N), lambda b, c: (b * NC + c, 0, 0, 0))],
        out_shape=[jax.ShapeDtypeStruct((T, D), F32),
                   jax.ShapeDtypeStruct((B * NC, SSD_H, SSD_P, SSD_N), F32)],
        scratch_shapes=[pltpu.VMEM((SSD_H, SSD_P, SSD_N), F32)],
        compiler_params=_cp(("arbitrary", "arbitrary")),
    )(xbc, sm, alog)


def _ssd_bwd(xbc, sm, alog, states, dy, dxs_skip, *, B, NC):
    T = B * NC * CH

    def rix(b, c):
        return b * NC + (NC - 1 - c)

    def body(x_ref, sm_ref, alog_ref, st_ref, dy_ref, sk_ref, dx_ref, ddt_ref, dal_ref, dS):
        bidx = pl.program_id(0)
        cidx = pl.program_id(1)

        @pl.when(cidx == 0)
        def _():
            dS[...] = jnp.zeros_like(dS)

        A, dt, cs, csT, cs_last, tril, r, c = _ssd_common(sm_ref, alog_ref)
        lane = lax.broadcasted_iota(jnp.int32, (1, CH), 1)
        DCcol = jnp.zeros((CH, CH), F32)
        DCrow = jnp.zeros((CH, CH), F32)
        DX = jnp.zeros((CH, CH), F32)
        dlast = jnp.zeros((1, CH), F32)
        for g in range(SSD_G):
            ob = D + g * SSD_N
            oc = D + SSD_G * SSD_N + g * SSD_N
            Bg = x_ref[:, ob:ob + SSD_N]
            Cg = x_ref[:, oc:oc + SSD_N]
            Bb = Bg.astype(BF16)
            Cb = Cg.astype(BF16)
            G = _nt(Cb, Bb)
            dG = jnp.zeros((CH, CH), F32)
            dBg = jnp.zeros((CH, SSD_N), F32)
            dCg = jnp.zeros((CH, SSD_N), F32)
            for rr in range(SSD_H // SSD_G):
                h = g * (SSD_H // SSD_G) + rr
                sl = slice(h * SSD_P, (h + 1) * SSD_P)
                col = cs[:, h:h + 1]
                row = csT[h:h + 1, :]
                Ld = jnp.where(tril > 0, jnp.exp(jnp.minimum(col - row, 0.0)), 0.0)
                Mf = G * Ld
                dth = dt[:, h:h + 1]
                xs_h = x_ref[:, sl]
                xdt = (xs_h * dth).astype(BF16)
                ST = st_ref[0, h]
                STb = ST.astype(BF16)
                dST = dS[h]
                dSTb = dST.astype(BF16)
                dyh = dy_ref[:, sl]
                dyb = dyh.astype(BF16)
                E = jnp.exp(col)
                yo = _nt(Cb, STb)
                dxdt = _tn(Mf.astype(BF16), dyb)
                dM = _nt(dyb, xdt)
                dG = dG + dM * Ld
                W = dM * Mf
                dcol = jnp.sum(W, axis=1, keepdims=True) + jnp.sum(dyh * yo, axis=1, keepdims=True) * E
                drow = -jnp.sum(W, axis=0, keepdims=True)
                dyE = (dyh * E).astype(BF16)
                dCg = dCg + _nn(dyE, STb)
                dS_in = _tn(dyE, Cb)
                cl = cs_last[:, h:h + 1]
                decay = jnp.exp(cl - col)
                Bd = (Bg * decay).astype(BF16)
                dxdt = dxdt + _nt(Bd, dSTb)
                dBd = _nn(xdt, dSTb)
                dBg = dBg + decay * dBd
                dd = jnp.sum(dBd * Bg, axis=1, keepdims=True) * decay
                dcol = dcol - dd
                el = jnp.exp(cl)
                dl = jnp.sum(dd, axis=0, keepdims=True) + el * jnp.sum(
                    jnp.sum(dST * ST, axis=1, keepdims=True), axis=0, keepdims=True)
                dS[h] = dS_in + el * dST
                dx_ref[:, sl] = dxdt * dth + sk_ref[:, sl]
                ddt_x = jnp.sum(dxdt * xs_h, axis=1, keepdims=True)
                DCcol = DCcol + jnp.where(c == h, dcol, 0.0)
                DCrow = DCrow + jnp.where(r == h, drow, 0.0)
                DX = DX + jnp.where(c == h, ddt_x, 0.0)
                dlast = dlast + jnp.where(lane == h, dl, 0.0)
            dGb = dG.astype(BF16)
            dx_ref[:, ob:ob + SSD_N] = dBg + _tn(dGb, Cb)
            dx_ref[:, oc:oc + SSD_N] = dCg + _nn(dGb, Bb)
        DC = DCcol + DCrow.T + jnp.where(r == CH - 1, dlast, 0.0)
        triu = (r <= c).astype(F32)
        dadt = jnp.dot(triu, DC, precision=HI, preferred_element_type=F32)
        ddt_ref[...] = dadt * A + DX
        dal = jnp.sum(dadt * dt, axis=0, keepdims=True) * A

        @pl.when((bidx == 0) & (cidx == 0))
        def _():
            dal_ref[...] = dal

        @pl.when((bidx > 0) | (cidx > 0))
        def _():
            dal_ref[...] += dal

    return pl.pallas_call(
        body, name="ssd_bwd", grid=(B, NC),
        in_specs=[pl.BlockSpec((CH, CONV_DIM), lambda b, c: (rix(b, c), 0)),
                  pl.BlockSpec((CH, CH), lambda b, c: (rix(b, c), 0)),
                  pl.BlockSpec((1, CH), lambda b, c: (0, 0)),
                  pl.BlockSpec((1, SSD_H, SSD_P, SSD_N), lambda b, c: (rix(b, c), 0, 0, 0)),
                  pl.BlockSpec((CH, D), lambda b, c: (rix(b, c), 0)),
                  pl.BlockSpec((CH, D), lambda b, c: (rix(b, c), 0))],
        out_specs=[pl.BlockSpec((CH, CONV_DIM), lambda b, c: (rix(b, c), 0)),
                   pl.BlockSpec((CH, CH), lambda b, c: (rix(b, c), 0)),
                   pl.BlockSpec((1, CH), lambda b, c: (0, 0))],
        out_shape=[jax.ShapeDtypeStruct((T, CONV_DIM), F32),
                   jax.ShapeDtypeStruct((T, CH), F32),
                   jax.ShapeDtypeStruct((1, CH), F32)],
        scratch_shapes=[pltpu.VMEM((SSD_H, SSD_P, SSD_N), F32)],
        compiler_params=_cp(("arbitrary", "arbitrary")),
    )(xbc, sm, alog, states, dy, dxs_skip)


def _cumsum_seq(v, *, B, NC, reverse, name):
    T = B * NC * CH

    def ix(b, c):
        return b * NC + ((NC - 1 - c) if reverse else c)

    def body(v_ref, o_ref, carry):
        cidx = pl.program_id(1)

        @pl.when(cidx == 0)
        def _():
            carry[...] = jnp.zeros_like(carry)

        r = lax.broadcasted_iota(jnp.int32, (CH, CH), 0)
        c = lax.broadcasted_iota(jnp.int32, (CH, CH), 1)
        tri = ((r <= c) if reverse else (r >= c)).astype(F32)
        cs = jnp.dot(tri, v_ref[...], precision=HI, preferred_element_type=F32) + carry[...]
        o_ref[...] = cs
        edge = 0 if reverse else CH - 1
        carry[...] = jnp.sum(jnp.where(r == edge, cs, 0.0), axis=0, keepdims=True)

    return pl.pallas_call(
        body, name=name, grid=(B, NC),
        in_specs=[pl.BlockSpec((CH, CH), lambda b, c: (ix(b, c), 0))],
        out_specs=pl.BlockSpec((CH, CH), lambda b, c: (ix(b, c), 0)),
        out_shape=jax.ShapeDtypeStruct((T, CH), F32),
        scratch_shapes=[pltpu.VMEM((1, CH), F32)],
        compiler_params=_cp(("arbitrary", "arbitrary")),
    )(v)


def _fox_tb(Lp):
    return 384 if (Lp % 384 == 0 and Lp > 384) else CH


def _fox_keybias(cum, *, B, Lp, pad):
    ck = cum.reshape(B, Lp, CH)[:, :, SSD_H:SSD_H + FOX_H].transpose(0, 2, 1)
    pos = lax.broadcasted_iota(jnp.int32, ck.shape, 2)
    return jnp.where(pos < pad, -NEG, ck).reshape(B * FOX_H, 1, Lp)


def _fox_tril(TB):
    r = lax.broadcasted_iota(jnp.int32, (TB, TB), 0)
    c = lax.broadcasted_iota(jnp.int32, (TB, TB), 1)
    return r >= c


def _fox_fwd(qkv, cumT, *, B, Lp):
    TB = _fox_tb(Lp)
    NQ = Lp // TB
    T = B * Lp
    scale = FOX_DH ** -0.5

    def body(q_ref, k_ref, v_ref, ct_ref, o_ref, lse_ref):
        i = pl.program_id(2)
        q = q_ref[...]

        def block(j, carry, diag):
            m, l, acc = carry
            off = pl.multiple_of(j * TB, TB)
            k = k_ref[pl.ds(off, TB), :]
            v = v_ref[pl.ds(off, TB), :]
            s = _nt(q, k) * scale - ct_ref[0, :, pl.ds(off, TB)]
            if diag:
                s = jnp.where(_fox_tril(TB), s, NEG)
            m_new = jnp.maximum(m, jnp.max(s, axis=1, keepdims=True))
            p = jnp.exp(s - m_new)
            alpha = jnp.exp(m - m_new)
            l = alpha * l + jnp.sum(p, axis=1, keepdims=True)
            acc = alpha * acc + _nn(p.astype(BF16), v)
            return m_new, l, acc

        init = (jnp.full((TB, 1), NEG, F32), jnp.zeros((TB, 1), F32), jnp.zeros((TB, FOX_DH), F32))
        carry = lax.fori_loop(0, i, lambda j, c: block(j, c, False), init)
        m, l, acc = block(i, carry, True)
        o_ref[...] = (acc / l).astype(o_ref.dtype)
        lse_ref[0, 0] = m + jnp.log(l)

    return pl.pallas_call(
        body, name="fox_fwd", grid=(B, FOX_H, NQ),
        in_specs=[pl.BlockSpec((TB, FOX_DH), lambda b, h, i: (b * NQ + i, h)),
                  pl.BlockSpec((Lp, FOX_DH), lambda b, h, i: (b, FOX_H + h)),
                  pl.BlockSpec((Lp, FOX_DH), lambda b, h, i: (b, 2 * FOX_H + h)),
                  pl.BlockSpec((1, 1, Lp), lambda b, h, i: (b * FOX_H + h, 0, 0))],
        out_specs=[pl.BlockSpec((TB, FOX_DH), lambda b, h, i: (b * NQ + i, h)),
                   pl.BlockSpec((1, 1, TB, 1), lambda b, h, i: (b, h, i, 0))],
        out_shape=[jax.ShapeDtypeStruct((T, D), BF16),
                   jax.ShapeDtypeStruct((B, FOX_H, Lp, 1), F32)],
        compiler_params=_cp(("arbitrary", "arbitrary", "arbitrary")),
    )(qkv, qkv, qkv, cumT)


def _fox_bwd(qkv, dy, o, lse, cumT, *, B, Lp):
    TB = _fox_tb(Lp)
    NQ = Lp // TB
    T = B * Lp
    scale = FOX_DH ** -0.5

    def body(q_ref, k_ref, v_ref, dy_ref, o_ref, lse_ref, ct_ref, dq_ref, dk_ref, dv_ref, dck_ref, dcq_ref, dl_s):
        j = pl.program_id(2)
        k = k_ref[...]
        v = v_ref[...]
        ck = ct_ref[0]

        @pl.when(j == 0)
        def _():
            dq_ref[...] = jnp.zeros_like(dq_ref)
            dcq_ref[...] = jnp.zeros_like(dcq_ref)
            for i in range(NQ):
                sl = slice(i * TB, (i + 1) * TB)
                dl_s[sl, :] = jnp.sum(dy_ref[sl, :] * o_ref[sl, :].astype(F32), axis=1, keepdims=True)

        def block(i, carry, diag):
            dk, dv, dck = carry
            off = pl.multiple_of(i * TB, TB)
            q = q_ref[pl.ds(off, TB), :]
            dob = dy_ref[pl.ds(off, TB), :].astype(BF16)
            e = _nt(q, k) * scale - ck - lse_ref[0, 0, pl.ds(off, TB), :]
            if diag:
                e = jnp.where(_fox_tril(TB), e, NEG)
            p = jnp.exp(e)
            dv = dv + _tn(p.astype(BF16), dob)
            ds = p * (_nt(dob, v) - dl_s[pl.ds(off, TB), :])
            dsb = ds.astype(BF16)
            dk = dk + _tn(dsb, q)
            dq_ref[pl.ds(off, TB), :] += _nn(dsb, k) * scale
            dcq_ref[0, 0, pl.ds(off, TB), :] += jnp.sum(ds, axis=1, keepdims=True)
            dck = dck - jnp.sum(ds, axis=0, keepdims=True)
            return dk, dv, dck

        z = jnp.zeros((TB, FOX_DH), F32)
        carry = block(j, (z, z, jnp.zeros((1, TB), F32)), True)
        dk, dv, dck = lax.fori_loop(j + 1, NQ, lambda i, c: block(i, c, False), carry)
        dk_ref[...] = (dk * scale).astype(dk_ref.dtype)
        dv_ref[...] = dv.astype(dv_ref.dtype)
        dck_ref[0] = dck

    head = lambda b, h, j: (b, h)
    return pl.pallas_call(
        body, name="fox_bwd", grid=(B, FOX_H, NQ),
        in_specs=[pl.BlockSpec((Lp, FOX_DH), head),
                  pl.BlockSpec((TB, FOX_DH), lambda b, h, j: (b * NQ + j, FOX_H + h)),
                  pl.BlockSpec((TB, FOX_DH), lambda b, h, j: (b * NQ + j, 2 * FOX_H + h)),
                  pl.BlockSpec((Lp, FOX_DH), head),
                  pl.BlockSpec((Lp, FOX_DH), head),
                  pl.BlockSpec((1, 1, Lp, 1), lambda b, h, j: (b, h, 0, 0)),
                  pl.BlockSpec((1, 1, TB), lambda b, h, j: (b * FOX_H + h, 0, j))],
        out_specs=[pl.BlockSpec((Lp, FOX_DH), head),
                   pl.BlockSpec((TB, FOX_DH), lambda b, h, j: (b * NQ + j, h)),
                   pl.BlockSpec((TB, FOX_DH), lambda b, h, j: (b * NQ + j, h)),
                   pl.BlockSpec((1, 1, TB), lambda b, h, j: (b * FOX_H + h, 0, j)),
                   pl.BlockSpec((1, 1, Lp, 1), lambda b, h, j: (b, h, 0, 0))],
        out_shape=[jax.ShapeDtypeStruct((T, D), F32),
                   jax.ShapeDtypeStruct((T, D), BF16),
                   jax.ShapeDtypeStruct((T, D), BF16),
                   jax.ShapeDtypeStruct((B * FOX_H, 1, Lp), F32),
                   jax.ShapeDtypeStruct((B, FOX_H, Lp, 1), F32)],
        scratch_shapes=[pltpu.VMEM((Lp, 1), F32)],
        compiler_params=_cp(("arbitrary", "arbitrary", "arbitrary")),
    )(qkv, qkv, qkv, dy, o, lse, cumT)


S5_TILE = 8


def _s5_pows(lam_ref, pw, tab, reverse):
    lr = lam_ref[0, :, 0:S5_W]
    li = lam_ref[0, :, S5_W:2 * S5_W]
    if reverse:
        li = -li
    ar, ai = lr, li
    sub = lax.broadcasted_iota(jnp.int32, (S5_TILE, 1), 0)
    for k, s in enumerate((1, 2, 4)):
        keep = (sub < S5_TILE - s) if reverse else (sub >= s)
        pw[k * S5_TILE:(k + 1) * S5_TILE, 0:S5_W] = jnp.where(keep, ar, 0.0)
        pw[k * S5_TILE:(k + 1) * S5_TILE, S5_W:2 * S5_W] = jnp.where(keep, ai, 0.0)
        ar, ai = ar * ar - ai * ai, 2.0 * ar * ai
    ar, ai = lr, li
    for r in range(S5_TILE):
        row = (S5_TILE - 1 - r) if reverse else r
        tab[row:row + 1, 0:S5_W] = ar
        tab[row:row + 1, S5_W:2 * S5_W] = ai
        ar, ai = ar * lr - ai * li, ar * li + ai * lr


def _s5_scan(hs, pw, tab, carry, reverse):
    n = hs.shape[0]
    tr = tab[:, 0:S5_W]
    ti = tab[:, S5_W:2 * S5_W]
    cr = carry[:, 0:S5_W]
    ci = carry[:, S5_W:2 * S5_W]
    order = range(n // S5_TILE)
    for t in (reversed(order) if reverse else order):
        lo = t * S5_TILE
        vr = hs[lo:lo + S5_TILE, 0:S5_W]
        vi = hs[lo:lo + S5_TILE, S5_W:2 * S5_W]
        for k, s in enumerate((1, 2, 4)):
            sh = (S5_TILE - s) if reverse else s
            sr = pltpu.roll(vr, sh, 0)
            si = pltpu.roll(vi, sh, 0)
            ar = pw[k * S5_TILE:(k + 1) * S5_TILE, 0:S5_W]
            ai = pw[k * S5_TILE:(k + 1) * S5_TILE, S5_W:2 * S5_W]
            vr, vi = vr + ar * sr - ai * si, vi + ar * si + ai * sr
        hs[lo:lo + S5_TILE, 0:S5_W] = vr + tr * cr - ti * ci
        hs[lo:lo + S5_TILE, S5_W:2 * S5_W] = vi + tr * ci + ti * cr
        edge = lo if reverse else lo + S5_TILE - 1
        cr = hs[edge:edge + 1, 0:S5_W]
        ci = hs[edge:edge + 1, S5_W:2 * S5_W]
    carry[:, 0:S5_W] = cr
    carry[:, S5_W:2 * S5_W] = ci


def _s5_fwd(u, Bsg, Csg, lam, *, B, NC):
    T = B * NC * CH

    def body(u_ref, b_ref, c_ref, lam_ref, y_ref, h_ref, pw, tab, hs, carry):
        cidx = pl.program_id(2)

        @pl.when(cidx == 0)
        def _():
            _s5_pows(lam_ref, pw, tab, False)
            carry[...] = jnp.zeros_like(carry)

        hs[...] = _nn(u_ref[...].astype(BF16), b_ref[0])
        _s5_scan(hs, pw, tab, carry, False)
        hb = hs[...].astype(BF16)
        h_ref[...] = hb
        y_ref[...] = _nn(hb, c_ref[0])

    return pl.pallas_call(
        body, name="s5_fwd", grid=(B, S5_G // S5_SG, NC),
        in_specs=[pl.BlockSpec((CH, CH), lambda b, s, c: (b * NC + c, s)),
                  pl.BlockSpec((1, CH, 2 * S5_W), lambda b, s, c: (s, 0, 0)),
                  pl.BlockSpec((1, 2 * S5_W, CH), lambda b, s, c: (s, 0, 0)),
                  pl.BlockSpec((1, 1, 2 * S5_W), lambda b, s, c: (s, 0, 0))],
        out_specs=[pl.BlockSpec((CH, CH), lambda b, s, c: (b * NC + c, s)),
                   pl.BlockSpec((CH, 2 * S5_W), lambda b, s, c: (b * NC + c, s))],
        out_shape=[jax.ShapeDtypeStruct((T, D), F32),
                   jax.ShapeDtypeStruct((T, (S5_G // S5_SG) * 2 * S5_W), BF16)],
        scratch_shapes=[pltpu.VMEM((3 * S5_TILE, 2 * S5_W), F32), pltpu.VMEM((S5_TILE, 2 * S5_W), F32),
                        pltpu.VMEM((CH, 2 * S5_W), F32), pltpu.VMEM((1, 2 * S5_W), F32)],
        compiler_params=_cp(("arbitrary", "arbitrary", "arbitrary")),
    )(u, Bsg, Csg, lam)


def _s5_bwd(u, hst, dy, du_skip, Bsg, Csg, lam, *, B, NC):
    T = B * NC * CH
    NS = S5_G // S5_SG
    hb16 = CH // 16

    def rix(b, c):
        return b * NC + (NC - 1 - c)

    def body(u_ref, h_ref, hp_ref, dy_ref, sk_ref, b_ref, c_ref, lam_ref,
             du_ref, db_ref, dc_ref, dl_ref, pw, tab, gs, carry):
        bidx = pl.program_id(1)
        cidx = pl.program_id(2)
        first = (bidx == 0) & (cidx == 0)

        @pl.when(cidx == 0)
        def _():
            _s5_pows(lam_ref, pw, tab, True)
            carry[...] = jnp.zeros_like(carry)

        dyb = dy_ref[...].astype(BF16)
        gs[...] = _nt(dyb, c_ref[0])
        _s5_scan(gs, pw, tab, carry, True)
        gr = gs[:, 0:S5_W]
        gi = gs[:, S5_W:]
        gb = gs[...].astype(BF16)
        du_ref[...] = (_nt(gb, b_ref[0]) + sk_ref[...]).astype(du_ref.dtype)
        ub = u_ref[...].astype(BF16)
        hcur = h_ref[...]
        dB = _tn(ub, gb)
        dC = _tn(hcur, dyb)
        hf = hcur.astype(F32)
        row = lax.broadcasted_iota(jnp.int32, (CH, 1), 0)
        prev_last = jnp.where(cidx < NC - 1, hp_ref[15:16, :].astype(F32), 0.0)
        hprev = jnp.where(row == 0, prev_last, pltpu.roll(hf, 1, 0))
        pr = hprev[:, 0:S5_W]
        pi = hprev[:, S5_W:]
        da = _colsum(gr * pr + gi * pi)
        dbb = _colsum(gi * pr - gr * pi)
        dl = jnp.concatenate([da, dbb], axis=1)

        @pl.when(first)
        def _():
            db_ref[0] = dB
            dc_ref[0] = dC
            dl_ref[0] = dl

        @pl.when(jnp.logical_not(first))
        def _():
            db_ref[0] += dB
            dc_ref[0] += dC
            dl_ref[0] += dl

    return pl.pallas_call(
        body, name="s5_bwd", grid=(NS, B, NC),
        in_specs=[pl.BlockSpec((CH, CH), lambda s, b, c: (rix(b, c), s)),
                  pl.BlockSpec((CH, 2 * S5_W), lambda s, b, c: (rix(b, c), s)),
                  pl.BlockSpec((16, 2 * S5_W), lambda s, b, c: (jnp.maximum(rix(b, c) * hb16 - 1, 0), s)),
                  pl.BlockSpec((CH, CH), lambda s, b, c: (rix(b, c), s)),
                  pl.BlockSpec((CH, CH), lambda s, b, c: (rix(b, c), s)),
                  pl.BlockSpec((1, CH, 2 * S5_W), lambda s, b, c: (s, 0, 0)),
                  pl.BlockSpec((1, 2 * S5_W, CH), lambda s, b, c: (s, 0, 0)),
                  pl.BlockSpec((1, 1, 2 * S5_W), lambda s, b, c: (s, 0, 0))],
        out_specs=[pl.BlockSpec((CH, CH), lambda s, b, c: (rix(b, c), s)),
                   pl.BlockSpec((1, CH, 2 * S5_W), lambda s, b, c: (s, 0, 0)),
                   pl.BlockSpec((1, 2 * S5_W, CH), lambda s, b, c: (s, 0, 0)),
                   pl.BlockSpec((1, 1, 2 * S5_W), lambda s, b, c: (s, 0, 0))],
        out_shape=[jax.ShapeDtypeStruct((T, D), BF16),
                   jax.ShapeDtypeStruct((NS, CH, 2 * S5_W), F32),
                   jax.ShapeDtypeStruct((NS, 2 * S5_W, CH), F32),
                   jax.ShapeDtypeStruct((NS, 1, 2 * S5_W), F32)],
        scratch_shapes=[pltpu.VMEM((3 * S5_TILE, 2 * S5_W), F32), pltpu.VMEM((S5_TILE, 2 * S5_W), F32),
                        pltpu.VMEM((CH, 2 * S5_W), F32), pltpu.VMEM((1, 2 * S5_W), F32)],
        compiler_params=_cp(("arbitrary", "arbitrary", "arbitrary")),
    )(u, hst, hst, dy, du_skip, Bsg, Csg, lam)


def _s5_param_fn(lre, lim, lstep, bre, bim):
    step = jnp.exp(lstep)
    zr = lre * step
    zi = lim * step
    e = jnp.exp(zr)
    a = e * jnp.cos(zi)
    b = e * jnp.sin(zi)
    den = lre * lre + lim * lim
    qr = ((a - 1.0) * lre + b * lim) / den
    qi = (b * lre - (a - 1.0) * lim) / den
    return a, b, qr[None] * bre - qi[None] * bim, qr[None] * bim + qi[None] * bre


_S5_ROWS = S5_G * S5_P // CH


def _s5_tile(v):
    return v.reshape(_S5_ROWS, CH)


def _s5_tile_b(v):
    return v.reshape(S5_G * S5_P, S5_C).T.reshape(S5_C, _S5_ROWS, CH)


def _s5_untile_b(v):
    return v.reshape(S5_C, S5_G * S5_P).T.reshape(S5_G, S5_P, S5_C)


def _s5_params(lre, lim, lstep, bre, bim):
    def body(a_ref, b_ref, c_ref, d_ref, e_ref, o1, o2, o3, o4):
        outs = _s5_param_fn(a_ref[...], b_ref[...], c_ref[...], d_ref[...], e_ref[...])
        for o, v in zip((o1, o2, o3, o4), outs):
            o[...] = v

    shp = [jax.ShapeDtypeStruct(lre.shape, F32)] * 2 + [jax.ShapeDtypeStruct(bre.shape, F32)] * 2
    return pl.pallas_call(body, name="s5_params", out_shape=shp, compiler_params=_cp())(lre, lim, lstep, bre, bim)


def _s5_params_bwd(lre, lim, lstep, bre, bim, da, db, dbr, dbi):
    def body(a_ref, b_ref, c_ref, d_ref, e_ref, g1, g2, g3, g4, o1, o2, o3, o4, o5):
        _, vjp = jax.vjp(_s5_param_fn, a_ref[...], b_ref[...], c_ref[...], d_ref[...], e_ref[...])
        outs = vjp((g1[...], g2[...], g3[...], g4[...]))
        for o, v in zip((o1, o2, o3, o4, o5), outs):
            o[...] = v

    shp = [jax.ShapeDtypeStruct(lre.shape, F32)] * 3 + [jax.ShapeDtypeStruct(bre.shape, F32)] * 2
    return pl.pallas_call(body, name="s5_params_bwd", out_shape=shp, compiler_params=_cp())(
        lre, lim, lstep, bre, bim, da, db, dbr, dbi)


def _s5_blockdiag(br, bi, cre, cim):
    NS = S5_G // S5_SG
    eye = jnp.eye(S5_SG, dtype=F32)

    def bmat(v):
        v = v.reshape(NS, S5_SG, S5_P, S5_C)
        m = jnp.einsum("sgpc,gh->sgchp", v, eye)
        return m.reshape(NS, S5_SG * S5_C, S5_SG * S5_P)

    def cmat(v):
        v = v.reshape(NS, S5_SG, S5_C, S5_P)
        m = jnp.einsum("sgcp,gh->sgphc", v, eye)
        return m.reshape(NS, S5_SG * S5_P, S5_SG * S5_C)

    Bsg = jnp.concatenate([bmat(br), bmat(bi)], axis=2).astype(BF16)
    Csg = jnp.concatenate([cmat(cre), cmat(-cim)], axis=1).astype(BF16)
    return Bsg, Csg


def _s5_unblock(dBsg, dCsg):
    NS = S5_G // S5_SG

    def ub(m):
        m = m.reshape(NS, S5_SG, S5_C, S5_SG, S5_P)
        d = jnp.stack([m[:, g, :, g, :] for g in range(S5_SG)], axis=1)
        return d.transpose(0, 1, 3, 2).reshape(S5_G, S5_P, S5_C)

    def uc(m):
        m = m.reshape(NS, S5_SG, S5_P, S5_SG, S5_C)
        d = jnp.stack([m[:, g, :, g, :] for g in range(S5_SG)], axis=1)
        return d.transpose(0, 1, 3, 2).reshape(S5_G, S5_C, S5_P)

    dbr = ub(dBsg[:, :, 0:S5_W])
    dbi = ub(dBsg[:, :, S5_W:])
    dcr = uc(dCsg[:, 0:S5_W, :])
    dci = -uc(dCsg[:, S5_W:, :])
    return dbr, dbi, dcr, dci


def _loss_head(x, nf, target, *, B, NC, S):
    T = B * NC * CH
    nts = S // CH

    def f(xv, w, t):
        y = _rms(xv, w)
        return 0.5 * _colsum(jnp.mean(jnp.square(y - t), axis=-1, keepdims=True))

    def body(x_ref, w_ref, t_ref, dx_ref, ls_ref, dw_ref):
        i = pl.program_id(0)
        on = (i % NC) > 0
        t = t_ref[...]
        l, vjp = jax.vjp(lambda a, b: f(a, b, t), x_ref[...], w_ref[...])
        dx, dw = vjp(jnp.ones((1, 1), F32))
        g = jnp.where(on, 1.0, 0.0)
        dx_ref[...] = dx * g
        lv = jnp.zeros((1, CH), F32) + l * g

        @pl.when(i == 0)
        def _():
            ls_ref[...] = lv
            dw_ref[...] = dw * g

        @pl.when(i > 0)
        def _():
            ls_ref[...] += lv
            dw_ref[...] += dw * g

    def tix(i):
        return ((i // NC) * nts + jnp.maximum(i % NC - 1, 0), 0)

    return pl.pallas_call(
        body, name="loss_head", grid=(B * NC,),
        in_specs=[pl.BlockSpec((CH, D), lambda i: (i, 0)),
                  pl.BlockSpec((1, D), lambda i: (0, 0)),
                  pl.BlockSpec((CH, D), tix)],
        out_specs=[pl.BlockSpec((CH, D), lambda i: (i, 0)),
                   pl.BlockSpec((1, CH), lambda i: (0, 0)),
                   pl.BlockSpec((1, D), lambda i: (0, 0))],
        out_shape=[jax.ShapeDtypeStruct((T, D), F32),
                   jax.ShapeDtypeStruct((1, CH), F32),
                   jax.ShapeDtypeStruct((1, D), F32)],
        compiler_params=_cp(("arbitrary",)),
    )(x, nf, target)


def _ew(fn, ins, n_out, out_dtypes, *, name, tile=None):
    R, C = ins[0].shape
    tile = tile or _pick(R, (512, 256, 128, 64, 32, 16, 8, 1))
    if tile % 8 != 0:
        tile = R

    def body(*refs):
        outs = fn(*[r[...] for r in refs[:len(ins)]])
        if not isinstance(outs, (tuple, list)):
            outs = (outs,)
        for r, v in zip(refs[len(ins):], outs):
            r[...] = v.astype(r.dtype)

    spec = pl.BlockSpec((tile, C), lambda i: (i, 0))
    res = pl.pallas_call(
        body, name=name, grid=(R // tile,), in_specs=[spec] * len(ins), out_specs=[spec] * n_out,
        out_shape=[jax.ShapeDtypeStruct((R, C), dt) for dt in out_dtypes],
        compiler_params=_cp(("parallel",)),
    )(*ins)
    return res


def _adam_fn(w, g, m, v):
    m = ADAM_B1 * m + (1.0 - ADAM_B1) * g
    v = ADAM_B2 * v + (1.0 - ADAM_B2) * jnp.square(g)
    m_hat = m / (1.0 - ADAM_B1 ** ADAM_STEP)
    v_hat = v / (1.0 - ADAM_B2 ** ADAM_STEP)
    delta = -ADAM_LR * (m_hat / (jnp.sqrt(v_hat) + ADAM_EPS) + ADAM_WD * w)
    return delta, m, v


def _adam(w, g, m, v, name):
    shp = w.shape
    C = shp[-1]
    f = lambda a: a.reshape(-1, C)
    d, nm, nv = _ew(_adam_fn, [f(w), f(g), f(m), f(v)], 3, [F32] * 3, name=name)
    return d.reshape(shp), nm.reshape(shp), nv.reshape(shp)


def _me():
    return lax.axis_index("x"), lax.axis_index("y"), lax.axis_index("c")


def _all_gather(v, name):
    def body(x_ref, out_ref, send_sems, recv_sems, local_sem):
        x, y, c = _me()
        me, sibling = (x, y, c), (x, y, 1 - c)
        chips = [(1 - x, y), (x, 1 - y), (1 - x, 1 - y)]

        def slot(px, py, pc):
            return out_ref.at[4 * px + 2 * py + pc]

        def copy(k, block, to, src=None):
            return pltpu.make_async_remote_copy(
                src_ref=slot(*block) if src is None else src, dst_ref=slot(*block),
                send_sem=send_sems.at[k], recv_sem=recv_sems.at[k], device_id=to, device_id_type=MESH)

        mine = pltpu.make_async_copy(x_ref, slot(*me), local_sem)
        mine.start()
        first = [copy(0, me, sibling, src=x_ref)]
        first += [copy(1 + j, me, (*chip, c), src=x_ref) for j, chip in enumerate(chips)]
        for cp in first:
            cp.start()
        passed = [copy(4 + j, (*chip, c), sibling) for j, chip in enumerate(chips)]
        for j, chip in enumerate(chips):
            copy(1 + j, (*chip, c), me).wait_recv()
            passed[j].start()
        copy(0, sibling, me).wait_recv()
        for j, chip in enumerate(chips):
            copy(4 + j, (*chip, 1 - c), me).wait_recv()
        for cp in first + passed:
            cp.wait_send()
        mine.wait()

    return pl.pallas_call(
        body, name=name, out_shape=jax.ShapeDtypeStruct((N_DEV,) + v.shape, v.dtype),
        in_specs=[pl.BlockSpec(memory_space=pl.ANY)], out_specs=pl.BlockSpec(memory_space=pl.ANY),
        scratch_shapes=[pltpu.SemaphoreType.DMA((7,)), pltpu.SemaphoreType.DMA((7,)), pltpu.SemaphoreType.DMA],
    )(v)


def _swap_core(g, name):
    def body(g_ref, out_ref, send_sems, recv_sems):
        x, y, c = _me()
        cps = [pltpu.make_async_remote_copy(
            src_ref=g_ref.at[q, 1 - c], dst_ref=out_ref.at[q], send_sem=send_sems.at[q], recv_sem=recv_sems.at[q],
            device_id=(x, y, 1 - c), device_id_type=MESH) for q in range(4)]
        for cp in cps:
            cp.start()
        for cp in cps:
            cp.wait()

    return pl.pallas_call(
        body, name=name, out_shape=jax.ShapeDtypeStruct((4,) + g.shape[2:], g.dtype),
        in_specs=[pl.BlockSpec(memory_space=pl.ANY)], out_specs=pl.BlockSpec(memory_space=pl.ANY),
        scratch_shapes=[pltpu.SemaphoreType.DMA((4,)), pltpu.SemaphoreType.DMA((4,))],
    )(g)


def _swap_chips(hb, name):
    flips = [(1, 0), (0, 1), (1, 1)]

    def body(h_ref, out_ref, send_sems, recv_sems):
        x, y, c = _me()
        cps = []
        for j, (fx, fy) in enumerate(flips):
            px = x + fx - 2 * x * fx
            py = y + fy - 2 * y * fy
            cps.append(pltpu.make_async_remote_copy(
                src_ref=h_ref.at[2 * px + py], dst_ref=out_ref.at[j], send_sem=send_sems.at[j],
                recv_sem=recv_sems.at[j], device_id=(px, py, c), device_id_type=MESH))
        for cp in cps:
            cp.start()
        for cp in cps:
            cp.wait()

    return pl.pallas_call(
        body, name=name, out_shape=jax.ShapeDtypeStruct((3,) + hb.shape[1:], hb.dtype),
        in_specs=[pl.BlockSpec(memory_space=pl.ANY)], out_specs=pl.BlockSpec(memory_space=pl.ANY),
        scratch_shapes=[pltpu.SemaphoreType.DMA((3,)), pltpu.SemaphoreType.DMA((3,))],
    )(hb)


def _reduce_scatter(g8, tag):
    shard = g8.shape[1:]
    C = shard[-1]
    x, y, c = _me()
    g4 = g8.reshape((4, 2) + shard)
    got = _swap_core(g4, "rs_core_" + tag)
    mine = lax.dynamic_index_in_dim(g4, c, axis=1, keepdims=False)
    h, hb = _ew(lambda a, b: (a + b, a + b), [mine.reshape(-1, C), got.reshape(-1, C)], 2, [F32, BF16],
                name="rs_pair_sum_" + tag)
    own = lax.dynamic_index_in_dim(h.reshape((4,) + shard), 2 * x + y, axis=0, keepdims=False)
    got3 = _swap_chips(hb.reshape((4,) + shard), "rs_chips_" + tag)
    out, = _ew(lambda a, b, c_, d: a + b.astype(F32) + c_.astype(F32) + d.astype(F32),
               [own.reshape(-1, C)] + [got3[j].reshape(-1, C) for j in range(3)], 1, [F32],
               name="rs_chip_sum_" + tag)
    return out.reshape(shard)


def _sum8(a):
    out, = _ew(lambda *v: functools.reduce(lambda p, q: p + q, v), [a[k] for k in range(N_DEV)], 1, [F32],
               name="sum8")
    return out


def _pad_rows(flat, cols, mult):
    n = flat.shape[0]
    per = cols * mult
    tot = ((n + per - 1) // per) * per
    return jnp.pad(flat, (0, tot - n)).reshape(-1, cols)


class _Packer:
    def __init__(self, shapes, mult):
        self.shapes = shapes
        self.sizes = [int(np.prod(s)) for s in shapes]
        self.mult = mult

    def pack(self, arrs, dtype):
        flat = jnp.concatenate([a.reshape(-1).astype(dtype) for a in arrs])
        return _pad_rows(flat, D, self.mult)

    def unpack(self, buf):
        flat = buf.reshape(-1)
        out, o = [], 0
        for s, n in zip(self.shapes, self.sizes):
            out.append(flat[o:o + n].reshape(s))
            o += n
        return out


def _w_cat(w_in_l):
    sm = jnp.concatenate([w_in_l[:, O_DT:O_DT + SSD_H], w_in_l[:, O_F:O_F + FOX_H],
                          jnp.zeros((D, CH - SSD_H - FOX_H), w_in_l.dtype)], axis=1)
    return jnp.concatenate([w_in_l[:, O_Z:O_XBC], w_in_l[:, O_XBC:O_DT], w_in_l[:, O_QKV:O_F],
                            w_in_l[:, O_U:O_G], w_in_l[:, O_G:D_IN], sm], axis=1)


def _w_uncat(g):
    return jnp.concatenate([g[:, OFF_Z:OFF_XBC], g[:, OFF_XBC:OFF_QKV], g[:, OFF_SM:OFF_SM + SSD_H],
                            g[:, OFF_QKV:OFF_U], g[:, OFF_SM + SSD_H:OFF_SM + SSD_H + FOX_H],
                            g[:, OFF_U:OFF_G], g[:, OFF_G:OFF_SM]], axis=1)


def _layer_fwd(x, p, geom, dims):
    B, NC, Lp, pad = dims
    T = geom[0]
    rm = functools.partial(_rowmap, geom=geom)
    sv = {}
    xn1, = rm(_f_norm, [x], [p["norm1"]], [(D, BF16)], [], tile=384 if Lp % 384 == 0 else CH, name="norm1")
    Wc = p["w_cat"]
    pz = _mm(xn1, Wc, "nn", F32, n=D, b_off=OFF_Z, name="in_z")
    pxbc = _mm(xn1, Wc, "nn", F32, n=CONV_DIM, b_off=OFF_XBC, name="in_xbc")
    qkv = _mm(xn1, Wc, "nn", BF16, n=3 * D, b_off=OFF_QKV, name="in_qkv")
    pu = _mm(xn1, Wc, "nn", F32, n=D, b_off=OFF_U, name="in_u")
    pg = _mm(xn1, Wc, "nn", F32, n=3 * D, b_off=OFF_G, name="in_g")
    psm = _mm(xn1, Wc, "nn", F32, n=CH, b_off=OFF_SM, name="in_sm")
    t_r = 384 if Lp % 384 == 0 else CH
    sm, = rm(_smallact, [psm], [p["smallbias"]], [(CH, F32)], [], tile=t_r, name="smallact")
    xbc = _conv_fwd(pxbc, p["conv_w"], p["conv_b"], geom=geom, tile=CH)
    y_ssd, states = _ssd_fwd(xbc, sm, p["a_log"], B=B, NC=NC)
    y_a, = rm(_ssd_post, [y_ssd, (xbc, D, 0), pz], [p["d_rep"], p["ssd_norm"]], [(D, BF16)], [], tile=t_r,
              name="ssd_post")
    cum = _cumsum_seq(sm, B=B, NC=NC, reverse=False, name="fox_cum")
    cumT = _fox_keybias(cum, B=B, Lp=Lp, pad=pad)
    y_b, lse = _fox_fwd(qkv, cumT, B=B, Lp=Lp)
    y_ssm, hst = _s5_fwd(pu, p["Bsg"], p["Csg"], p["lam"], B=B, NC=NC)
    y1, = rm(_s5_pre, [y_ssm, pu], [p["s5_d"]], [(D, BF16)], [], tile=t_r, name="s5_pre")
    tg = _mm(y1, p["w_glu"], "nn", F32, name="s5_glu_mm")
    y_c, = rm(_s5_glu, [y_ssm, pu, tg], [p["s5_d"]], [(D, BF16)], [], tile=t_r, name="s5_glu")
    br = [_mm(yy, p["w_branch"][n], "nn", F32, name=f"branch{n}") for n, yy in enumerate((y_a, y_b, y_c))]
    mix, = rm(_merge, [(pg, D, 0), (pg, D, 1), (pg, D, 2)] + br, [], [(D, BF16)], [], tile=CH, name="merge")
    x_mid = _mm(mix, p["w_out"], "nn", F32, res=x, name="out_proj")
    xn2, = rm(_f_norm, [x_mid], [p["norm2"]], [(D, BF16)], [], tile=t_r, name="norm2")
    hff = _mm(xn2, p["w_ffn_in"], "nn", F32, name="ffn_in")
    act, = rm(_swiglu, [(hff, DFF, 0), (hff, DFF, 1)], [], [(DFF, BF16)], [], tile=CH, name="swiglu")
    x_out = _mm(act, p["w_ffn_out"], "nn", F32, res=x_mid, name="ffn_out")
    sv.update(x=x, xn1=xn1, pz=pz, pxbc=pxbc, qkv=qkv, pu=pu, pg=pg, psm=psm, sm=sm, xbc=xbc, y_ssd=y_ssd,
              states=states, y_a=y_a, cum=cum, cumT=cumT, y_b=y_b, lse=lse, y_ssm=y_ssm, hst=hst, y1=y1, tg=tg,
              y_c=y_c, br=br, mix=mix, x_mid=x_mid, xn2=xn2, hff=hff, act=act)
    return x_out, sv


def _layer_bwd(dx_out, p, sv, geom, dims):
    B, NC, Lp, pad = dims
    T = geom[0]
    rm = functools.partial(_rowmap, geom=geom)
    t_r = 384 if Lp % 384 == 0 else CH
    g = {}
    dact = _mm(dx_out, p["w_ffn_out"], "nt", F32, name="ffn_out_dx")
    g["w_ffn_out"] = _mm(sv["act"], dx_out, "tn", F32, name="ffn_out_dw")
    dhff, = rm(_b_swiglu, [(sv["hff"], DFF, 0), (sv["hff"], DFF, 1), dact], [], [(2 * DFF, BF16)], [], tile=CH,
               name="swiglu_bwd")
    dxn2 = _mm(dhff, p["w_ffn_in"], "nt", F32, name="ffn_in_dx")
    g["w_ffn_in"] = _mm(sv["xn2"], dhff, "tn", F32, name="ffn_in_dw")
    dx_mid, g["norm2"] = rm(_b_norm, [sv["x_mid"], dxn2, dx_out], [p["norm2"]], [(D, F32)], [(1, D)], tile=t_r,
                            name="norm2_bwd")
    dmix = _mm(dx_mid, p["w_out"], "nt", F32, name="out_proj_dx")
    g["w_out"] = _mm(sv["mix"], dx_mid, "tn", F32, name="out_proj_dw")
    pg = sv["pg"]
    dpg, db0, db1, db2 = rm(_b_merge, [(pg, D, 0), (pg, D, 1), (pg, D, 2)] + sv["br"] + [dmix], [],
                            [(3 * D, BF16), (D, BF16), (D, BF16), (D, BF16)], [], tile=CH, name="merge_bwd")
    ys = (sv["y_a"], sv["y_b"], sv["y_c"])
    dbs = (db0, db1, db2)
    g["w_branch"] = [_mm(ys[n], dbs[n], "tn", F32, name=f"branch{n}_dw") for n in range(3)]
    dy = [_mm(dbs[n], p["w_branch"][n], "nt", F32, name=f"branch{n}_dx") for n in range(3)]
    dtg, dy1a = rm(_b_s5_glu, [sv["y_ssm"], sv["pu"], sv["tg"], dy[2]], [p["s5_d"]], [(D, BF16), (D, F32)], [],
                   tile=t_r, name="s5_glu_bwd")
    dy1b = _mm(dtg, p["w_glu"], "nt", F32, name="s5_glu_mm_dx")
    g["w_glu"] = _mm(sv["y1"], dtg, "tn", F32, name="s5_glu_mm_dw")
    dys, du_skip, g["s5_d"] = rm(_b_s5_pre, [sv["y_ssm"], sv["pu"], dy1a, dy1b], [p["s5_d"]],
                                 [(D, F32), (D, F32)], [(1, D)], tile=t_r, name="s5_pre_bwd")
    du, g["Bsg"], g["Csg"], g["lam"] = _s5_bwd(sv["pu"], sv["hst"], dys, du_skip, p["Bsg"], p["Csg"], p["lam"],
                                               B=B, NC=NC)
    dq, dk, dv, dckT, dcq = _fox_bwd(sv["qkv"], dy[1], sv["y_b"], sv["lse"], sv["cumT"], B=B, Lp=Lp)
    dcum8 = dcq.reshape(B, FOX_H, Lp).transpose(0, 2, 1) + dckT.reshape(B, FOX_H, Lp).transpose(0, 2, 1)
    dcum = jnp.pad(dcum8.reshape(T, FOX_H), ((0, 0), (SSD_H, CH - SSD_H - FOX_H)))
    dlogf = _cumsum_seq(dcum, B=B, NC=NC, reverse=True, name="fox_cum_bwd")
    dy_ssd, dxs_skip, dz, g["d_rep"], g["ssd_norm"] = rm(
        _b_ssd_post, [sv["y_ssd"], (sv["xbc"], D, 0), sv["pz"], dy[0]], [p["d_rep"], p["ssd_norm"]],
        [(D, F32), (D, F32), (D, BF16)], [(1, D), (1, D)], tile=t_r, name="ssd_post_bwd")
    dxbc_act, ddt, g["a_log"] = _ssd_bwd(sv["xbc"], sv["sm"], p["a_log"], sv["states"], dy_ssd, dxs_skip, B=B, NC=NC)
    dpsm, g["smallbias"] = rm(_b_smallact, [sv["psm"], ddt, dlogf], [p["smallbias"]], [(CH, BF16)], [(1, CH)],
                              tile=t_r, name="smallact_bwd")
    dconv, g["conv_w"], g["conv_b"] = _conv_bwd_pre(sv["pxbc"], dxbc_act, p["conv_w"], p["conv_b"], geom=geom, tile=CH)
    dpxbc = _conv_bwd_x(dconv, p["conv_w"], geom=geom, tile=CH)
    dproj = jnp.concatenate([dz, dpxbc, dq.astype(BF16), dk, dv, du, dpg, dpsm], axis=1)
    dxn1 = _mm(dproj, p["w_cat"], "nt", F32, name="in_dx")
    g["w_cat"] = _mm(sv["xn1"], dproj, "tn", F32, name="in_dw")
    dx_in, g["norm1"] = rm(_b_norm, [sv["x"], dxn1, dx_mid], [p["norm1"]], [(D, F32)], [(1, D)], tile=t_r,
                           name="norm1_bwd")
    return dx_in, g


_BIG = ["w_in", "s5_w_glu", "w_branch", "w_out", "w_ffn_in", "w_ffn_out"]
_NAMES = ['meta', 'norm1', 'w_in', 'ssd_conv_w', 'ssd_conv_b', 'ssd_dt_bias', 'ssd_a_log', 'ssd_d', 'ssd_norm',
          'fox_bf', 's5_lam_re', 's5_lam_im', 's5_b_re', 's5_b_im', 's5_c_re', 's5_c_im', 's5_log_step', 's5_d',
          's5_w_glu', 'w_branch', 'w_out', 'norm2', 'w_ffn_in', 'w_ffn_out', 'norm_f']
_SHARD_AXIS = {"meta": 1, "ssd_conv_w": 2}
_BIG_AXIS = {"w_in": 2, "s5_w_glu": 1, "w_branch": 2, "w_out": 1, "w_ffn_in": 2, "w_ffn_out": 1}


def kernel(x, meta, norm1, w_in, ssd_conv_w, ssd_conv_b, ssd_dt_bias, ssd_a_log, ssd_d, ssd_norm, fox_bf, s5_lam_re, s5_lam_im, s5_b_re, s5_b_im, s5_c_re, s5_c_im, s5_log_step, s5_d, s5_w_glu, w_branch, w_out, norm2, w_ffn_in, w_ffn_out, norm_f, loss_target, m_meta, m_norm1, m_w_in, m_ssd_conv_w, m_ssd_conv_b, m_ssd_dt_bias, m_ssd_a_log, m_ssd_d, m_ssd_norm, m_fox_bf, m_s5_lam_re, m_s5_lam_im, m_s5_b_re, m_s5_b_im, m_s5_c_re, m_s5_c_im, m_s5_log_step, m_s5_d, m_s5_w_glu, m_w_branch, m_w_out, m_norm2, m_w_ffn_in, m_w_ffn_out, m_norm_f, v_meta, v_norm1, v_w_in, v_ssd_conv_w, v_ssd_conv_b, v_ssd_dt_bias, v_ssd_a_log, v_ssd_d, v_ssd_norm, v_fox_bf, v_s5_lam_re, v_s5_lam_im, v_s5_b_re, v_s5_b_im, v_s5_c_re, v_s5_c_im, v_s5_log_step, v_s5_d, v_s5_w_glu, v_w_branch, v_w_out, v_norm2, v_w_ffn_in, v_w_ffn_out, v_norm_f):
    args = locals()
    W = {n: args[n] for n in _NAMES}
    Mo = {n: args["m_" + n] for n in _NAMES}
    Vo = {n: args["v_" + n] for n in _NAMES}
    B, S, _ = x.shape
    depth = norm1.shape[0]
    L = S + N_META
    Lp = ((L + CH - 1) // CH) * CH
    pad = Lp - L
    assert pad + N_META == CH and S % CH == 0
    NC = Lp // CH
    T = B * Lp
    geom = (T, Lp, pad)
    dims = (B, NC, Lp, pad)
    xi, yi, ci = _me()
    dev = 4 * xi + 2 * yi + ci

    gath = {n: _all_gather(W[n].astype(BF16), "gather_" + n) for n in _BIG}
    full = {n: jnp.concatenate([gath[n][k] for k in range(N_DEV)], axis=_BIG_AXIS[n]) for n in _BIG}
    sm_pack = _Packer([meta.shape, ssd_conv_w.shape], 8)
    sm_g = _all_gather(sm_pack.pack([meta, ssd_conv_w], F32), "gather_small")
    sm_parts = [sm_pack.unpack(sm_g[k]) for k in range(N_DEV)]
    meta_full = jnp.concatenate([sm_parts[k][0] for k in range(N_DEV)], axis=1)
    conv_w_full = jnp.concatenate([sm_parts[k][1] for k in range(N_DEV)], axis=2)

    layers = []
    s5_in = []
    for l in range(depth):
        lre = _s5_tile(s5_lam_re[l])
        lim = _s5_tile(s5_lam_im[l])
        lst = _s5_tile(jnp.repeat(s5_log_step[l], S5_P))
        bre = _s5_tile_b(s5_b_re[l])
        bim = _s5_tile_b(s5_b_im[l])
        s5_in.append((lre, lim, lst, bre, bim))
        a, b, br_, bi_ = _s5_params(lre, lim, lst, bre, bim)
        Bsg, Csg = _s5_blockdiag(_s5_untile_b(br_), _s5_untile_b(bi_), s5_c_re[l], s5_c_im[l])
        NS = S5_G // S5_SG
        lam = jnp.concatenate([a.reshape(NS, 1, S5_W), b.reshape(NS, 1, S5_W)], axis=2)
        zpad = jnp.zeros((CH - SSD_H - FOX_H,), F32)
        layers.append(dict(
            norm1=norm1[l][None], w_cat=_w_cat(full["w_in"][l]),
            smallbias=jnp.concatenate([ssd_dt_bias[l], fox_bf[l], zpad])[None],
            conv_w=conv_w_full[l], conv_b=ssd_conv_b[l][None],
            a_log=jnp.concatenate([ssd_a_log[l], jnp.zeros((CH - SSD_H,), F32)])[None],
            d_rep=jnp.repeat(ssd_d[l], SSD_P)[None], ssd_norm=ssd_norm[l][None],
            Bsg=Bsg, Csg=Csg, lam=lam, s5_d=s5_d[l][None], w_glu=full["s5_w_glu"][l],
            w_branch=[full["w_branch"][l, n] for n in range(3)], w_out=full["w_out"][l],
            norm2=norm2[l][None], w_ffn_in=full["w_ffn_in"][l], w_ffn_out=full["w_ffn_out"][l]))

    xs = jnp.concatenate([jnp.zeros((B, pad, D), F32), jnp.broadcast_to(meta_full[None], (B, N_META, D)), x], axis=1)
    h = xs.reshape(T, D)
    saved = []
    for l in range(depth):
        h, sv = _layer_fwd(h, layers[l], geom, dims)
        saved.append(sv)
    dh, loss_row, g_nf = _loss_head(h, norm_f[None], loss_target.reshape(B * S, D), B=B, NC=NC, S=S)
    loss = lax.psum(loss_row[0, 0], AXES)

    G = {n: [None] * depth for n in _NAMES}
    for l in reversed(range(depth)):
        dh, g = _layer_bwd(dh, layers[l], saved[l], geom, dims)
        saved[l] = None
        G["norm1"][l] = g["norm1"][0]
        G["norm2"][l] = g["norm2"][0]
        G["w_in"][l] = _w_uncat(g["w_cat"])
        G["ssd_conv_w"][l] = g["conv_w"]
        G["ssd_conv_b"][l] = g["conv_b"][0]
        G["ssd_dt_bias"][l] = g["smallbias"][0, 0:SSD_H]
        G["fox_bf"][l] = g["smallbias"][0, SSD_H:SSD_H + FOX_H]
        G["ssd_a_log"][l] = g["a_log"][0, 0:SSD_H]
        G["ssd_d"][l] = g["d_rep"].reshape(SSD_H, SSD_P).sum(axis=1)
        G["ssd_norm"][l] = g["ssd_norm"][0]
        dbr, dbi, dcr, dci = _s5_unblock(g["Bsg"], g["Csg"])
        da = _s5_tile(g["lam"][:, 0, 0:S5_W])
        db = _s5_tile(g["lam"][:, 0, S5_W:])
        dlre, dlim, dlst, dbre, dbim = _s5_params_bwd(*s5_in[l], da, db, _s5_tile_b(dbr), _s5_tile_b(dbi))
        G["s5_lam_re"][l] = dlre.reshape(S5_G, S5_P)
        G["s5_lam_im"][l] = dlim.reshape(S5_G, S5_P)
        G["s5_log_step"][l] = dlst.reshape(S5_G, S5_P).sum(axis=1)
        G["s5_b_re"][l] = _s5_untile_b(dbre)
        G["s5_b_im"][l] = _s5_untile_b(dbim)
        G["s5_c_re"][l] = dcr
        G["s5_c_im"][l] = dci
        G["s5_d"][l] = g["s5_d"][0]
        G["s5_w_glu"][l] = g["w_glu"]
        G["w_branch"][l] = jnp.stack(g["w_branch"])
        G["w_out"][l] = g["w_out"]
        G["w_ffn_in"][l] = g["w_ffn_in"]
        G["w_ffn_out"][l] = g["w_ffn_out"]
    dxs = dh.reshape(B, Lp, D)
    grad_x = dxs[:, pad + N_META:, :]
    part = {n: jnp.stack(G[n]) for n in _NAMES if n not in ("meta", "norm_f")}
    part["meta"] = dxs[:, pad:pad + N_META, :].sum(axis=0)
    part["norm_f"] = g_nf[0]

    grads = {}
    for n in _BIG:
        ax = _BIG_AXIS[n]
        a = part[n]
        a = a.reshape(a.shape[:ax] + (N_DEV, a.shape[ax] // N_DEV) + a.shape[ax + 1:])
        grads[n] = _reduce_scatter(jnp.moveaxis(a, ax, 0), n)

    small = [n for n in _NAMES if n not in _BIG]
    sp = _Packer([part[n].shape for n in small], 128)
    tot = sp.unpack(_sum8(_all_gather(sp.pack([part[n] for n in small], F32), "gather_small_grads")))
    for n, t in zip(small, tot):
        if n in _SHARD_AXIS:
            ax = _SHARD_AXIS[n]
            w = W[n].shape[ax]
            t = lax.dynamic_slice_in_dim(t, dev * w, w, axis=ax)
        grads[n] = t

    delta, new_m, new_v = {}, {}, {}
    for n in _BIG:
        delta[n], new_m[n], new_v[n] = _adam(W[n], grads[n], Mo[n], Vo[n], "adam_" + n)
    ap = _Packer([W[n].shape for n in small], 128)
    d_, m_, v_ = _adam(ap.pack([W[n] for n in small], F32), ap.pack([grads[n] for n in small], F32),
                       ap.pack([Mo[n] for n in small], F32), ap.pack([Vo[n] for n in small], F32), "adam_small")
    for n, a, b, c in zip(small, ap.unpack(d_), ap.unpack(m_), ap.unpack(v_)):
        delta[n], new_m[n], new_v[n] = a, b, c
    return (loss, grad_x, *[grads[n] for n in _NAMES], *[delta[n] for n in _NAMES],
            *[new_m[n] for n in _NAMES], *[new_v[n] for n in _NAMES])
```

```python
import functools
import math

import numpy as np
import jax
import jax.numpy as jnp
from jax import lax
from jax.experimental import pallas as pl
from jax.experimental.pallas import tpu as pltpu

F32 = jnp.float32
BF16 = jnp.bfloat16
AXES = ("x", "y", "c")
MESH = pl.DeviceIdType.MESH
N_DEV = 8

D = 1024
N_META = 16
CH = 128
EPS = 1e-6
NEG = -1e30
SSD_H, SSD_P, SSD_N, SSD_G = 16, 64, 128, 2
CONV_K, CONV_DIM = 4, 1536
FOX_H, FOX_DH = 8, 128
S5_G, S5_P, S5_C = 64, 64, 16
S5_SG = 8
S5_W = S5_SG * S5_P
DFF = 2816
D_IN = 9752
OFF_Z, OFF_XBC, OFF_QKV, OFF_U, OFF_G, OFF_SM, D_CAT = 0, 1024, 2560, 5632, 6656, 9728, 9856
O_Z, O_XBC, O_DT, O_QKV, O_F, O_U, O_G = 0, 1024, 2560, 2576, 5648, 5656, 6680

ADAM_LR, ADAM_B1, ADAM_B2, ADAM_EPS, ADAM_WD, ADAM_STEP = 0.001, 0.9, 0.999, 1e-08, 0.01, 10

VMEM_LIMIT_V7X = 52 * 1024 * 1024
HI = lax.Precision.HIGHEST


def _cp(sem=None):
    return pltpu.CompilerParams(dimension_semantics=sem, vmem_limit_bytes=VMEM_LIMIT_V7X)


def _pick(n, cands):
    for c in cands:
        if n % c == 0:
            return c
    raise ValueError(f"no tile for {n}")


_TILES = (1408, 1024, 896, 768, 512, 384, 256, 128)


def _mm(a, b, mode, out_dtype, *, name, n=None, b_off=0, res=None, tm=None, tn=None, tk=None):
    if mode == "tn":
        K, M = a.shape
    else:
        M, K = a.shape
    if mode == "nt":
        N = b.shape[0]
    else:
        N = n if n is not None else b.shape[1]
    tm = tm or _pick(M, (1024, 768, 512, 384, 256, 128, 64, 16, 8))
    tn = tn or _pick(math.gcd(N, b_off) if b_off else N, (1024, 896, 768, 512, 384, 256, 128))
    tk = tk or _pick(K, _TILES)
    nk = K // tk
    joff = b_off // tn

    def body(*refs):
        if res is None:
            a_ref, b_ref, o_ref, acc = refs
            r_ref = None
        else:
            a_ref, b_ref, r_ref, o_ref, acc = refs
        k = pl.program_id(2)
        av = a_ref[...].astype(BF16)
        bv = b_ref[...].astype(BF16)
        if mode == "nn":
            p = jnp.dot(av, bv, preferred_element_type=F32)
        elif mode == "nt":
            p = lax.dot_general(av, bv, (((1,), (1,)), ((), ())), preferred_element_type=F32)
        else:
            p = lax.dot_general(av, bv, (((0,), (0,)), ((), ())), preferred_element_type=F32)

        @pl.when(k == 0)
        def _():
            acc[...] = p

        @pl.when(k > 0)
        def _():
            acc[...] += p

        @pl.when(k == nk - 1)
        def _():
            r = acc[...]
            if r_ref is not None:
                r = r + r_ref[...]
            o_ref[...] = r.astype(o_ref.dtype)

    if mode == "tn":
        a_spec = pl.BlockSpec((tk, tm), lambda i, j, k: (k, i))
    else:
        a_spec = pl.BlockSpec((tm, tk), lambda i, j, k: (i, k))
    if mode == "nt":
        b_spec = pl.BlockSpec((tn, tk), lambda i, j, k: (j, k))
    else:
        b_spec = pl.BlockSpec((tk, tn), lambda i, j, k: (k, j + joff))
    o_spec = pl.BlockSpec((tm, tn), lambda i, j, k: (i, j))
    in_specs = [a_spec, b_spec] + ([o_spec] if res is not None else [])
    args = (a, b) + ((res,) if res is not None else ())
    return pl.pallas_call(
        body, name=name, grid=(M // tm, N // tn, nk),
        in_specs=in_specs, out_specs=o_spec,
        out_shape=jax.ShapeDtypeStruct((M, N), out_dtype),
        scratch_shapes=[pltpu.VMEM((tm, tn), F32)],
        compiler_params=_cp(("parallel", "parallel", "arbitrary")),
    )(*args)


def _rowmap(fn, row_ins, const_ins, row_outs, acc_outs, *, geom, tile, name):
    T, Lp, pad = geom
    assert Lp % tile == 0
    per_seq = Lp // tile
    specs, args = [], []
    for r in row_ins:
        arr, w, cb = r if isinstance(r, tuple) else (r, r.shape[1], 0)
        specs.append(pl.BlockSpec((tile, w), functools.partial(lambda i, cb: (i, cb), cb=cb)))
        args.append(arr)
    for c in const_ins:
        specs.append(pl.BlockSpec(c.shape, functools.partial(lambda i, nd: (0,) * nd, nd=c.ndim)))
        args.append(c)
    n_r, n_c, n_o, n_a = len(row_ins), len(const_ins), len(row_outs), len(acc_outs)
    out_specs = [pl.BlockSpec((tile, w), lambda i: (i, 0)) for w, _ in row_outs]
    out_specs += [pl.BlockSpec(s, lambda i: (0, 0)) for s in acc_outs]
    out_shape = [jax.ShapeDtypeStruct((T, w), dt) for w, dt in row_outs]
    out_shape += [jax.ShapeDtypeStruct(s, F32) for s in acc_outs]

    def body(*refs):
        i = pl.program_id(0)
        pos = (i % per_seq) * tile + lax.broadcasted_iota(jnp.int32, (tile, 1), 0)
        valid = pos >= pad
        vals = [r[...] for r in refs[:n_r + n_c]]
        outs = fn(valid, *vals)
        if not isinstance(outs, (tuple, list)):
            outs = (outs,)
        orefs = refs[n_r + n_c:]
        for r, v in zip(orefs[:n_o], outs[:n_o]):
            r[...] = v.astype(r.dtype)
        for r, v in zip(orefs[n_o:], outs[n_o:]):
            @pl.when(i == 0)
            def _(r=r, v=v):
                r[...] = v

            @pl.when(i > 0)
            def _(r=r, v=v):
                r[...] += v

    res = pl.pallas_call(
        body, name=name, grid=(T // tile,), in_specs=specs, out_specs=out_specs, out_shape=out_shape,
        compiler_params=_cp(("arbitrary",)),
    )(*args)
    return res


def _sigmoid(x):
    return 1.0 / (1.0 + jnp.exp(-x))


def _silu(x):
    return x * _sigmoid(x)


def _softplus(x):
    return jnp.maximum(x, 0.0) + jnp.log(1.0 + jnp.exp(-jnp.abs(x)))


def _gelu(x):
    return 0.5 * x * (1.0 + jnp.tanh(math.sqrt(2.0 / math.pi) * (x + 0.044715 * x * x * x)))


def _rms(x, w):
    return x * lax.rsqrt(jnp.mean(x * x, axis=-1, keepdims=True) + EPS) * w


def _colsum(v):
    return jnp.sum(v, axis=0, keepdims=True)


def _f_norm(valid, x, w):
    return _rms(x, w)


def _b_norm(valid, x, dxn, dres, w):
    _, vjp = jax.vjp(_rms, x, w)
    dx, dw = vjp(dxn)
    return jnp.where(valid, dx + dres, 0.0), dw


def _smallact(valid, raw, bias):
    lane = lax.broadcasted_iota(jnp.int32, raw.shape, 1)
    v = raw + bias
    dt = _softplus(v)
    logf = -_softplus(-v)
    out = jnp.where(lane < SSD_H, dt, jnp.where(lane < SSD_H + FOX_H, logf, 0.0))
    return jnp.where(valid, out, 0.0)


def _b_smallact(valid, raw, d1, d2, bias):
    _, vjp = jax.vjp(lambda r, b: _smallact(valid, r, b), raw, bias)
    return vjp(d1 + d2)


def _ssd_post(valid, y, xs, z, drep, nw):
    y = (y + xs * drep) * _silu(z)
    return _rms(y, nw)


def _b_ssd_post(valid, y, xs, z, dya, drep, nw):
    _, vjp = jax.vjp(lambda a, b, c, d, e: _ssd_post(valid, a, b, c, d, e), y, xs, z, drep, nw)
    dy, dxs, dz, dd, dn = vjp(dya)
    return dy, dxs, dz, dd, dn


def _s5_pre(valid, ys, u, d):
    return _gelu(ys + d * u)


def _s5_glu(valid, ys, u, t, d):
    y1 = _gelu(ys + d * u)
    return y1 * _sigmoid(t)


def _b_s5_glu(valid, ys, u, t, dyc, d):
    y1 = _gelu(ys + d * u)
    _, vjp = jax.vjp(lambda a, b: a * _sigmoid(b), y1, t)
    dy1, dt = vjp(dyc)
    return dt, dy1


def _b_s5_pre(valid, ys, u, dy1a, dy1b, d):
    _, vjp = jax.vjp(lambda a, b, c: _gelu(a + c * b), ys, u, d)
    dys, du, dd = vjp(dy1a + dy1b)
    return dys, du, dd


def _merge(valid, g0, g1, g2, b0, b1, b2):
    m = _sigmoid(g0) * b0 + _sigmoid(g1) * b1 + _sigmoid(g2) * b2
    return jnp.where(valid, m, 0.0)


def _b_merge(valid, g0, g1, g2, b0, b1, b2, dmix):
    _, vjp = jax.vjp(lambda *a: _merge(valid, *a), g0, g1, g2, b0, b1, b2)
    d = vjp(dmix)
    return jnp.concatenate(d[:3], axis=1), d[3], d[4], d[5]


def _swiglu(valid, g, up):
    return _silu(g) * up


def _b_swiglu(valid, g, up, dact):
    _, vjp = jax.vjp(lambda a, b: _silu(a) * b, g, up)
    dg, dup = vjp(dact)
    return jnp.concatenate([dg, dup], axis=1)


def _conv_taps(ext, tile):
    taps = []
    for k in range(CONV_K):
        sh = CONV_K - 1 - k
        v = ext if sh == 0 else pltpu.roll(ext, sh, 0)
        taps.append(v[8:8 + tile])
    return taps


def _conv_fwd(x, w, b, *, geom, tile):
    T, Lp, pad = geom
    per_seq = Lp // tile
    hb = tile // 8

    def body(x_ref, h_ref, w_ref, b_ref, o_ref):
        i = pl.program_id(0)
        pos = (i % per_seq) * tile + lax.broadcasted_iota(jnp.int32, (tile, 1), 0)
        ext = jnp.concatenate([h_ref[...], x_ref[...]], axis=0)
        taps = _conv_taps(ext, tile)
        acc = b_ref[...] + taps[0] * w_ref[0:1, :]
        for k in range(1, CONV_K):
            acc = acc + taps[k] * w_ref[k:k + 1, :]
        o_ref[...] = jnp.where(pos >= pad, _silu(acc), 0.0)

    return pl.pallas_call(
        body, name="conv_fwd", grid=(T // tile,),
        in_specs=[pl.BlockSpec((tile, CONV_DIM), lambda i: (i, 0)),
                  pl.BlockSpec((8, CONV_DIM), lambda i: (jnp.maximum(i * hb - 1, 0), 0)),
                  pl.BlockSpec((CONV_K, CONV_DIM), lambda i: (0, 0)),
                  pl.BlockSpec((1, CONV_DIM), lambda i: (0, 0))],
        out_specs=pl.BlockSpec((tile, CONV_DIM), lambda i: (i, 0)),
        out_shape=jax.ShapeDtypeStruct((T, CONV_DIM), F32),
        compiler_params=_cp(("arbitrary",)),
    )(x, x, w, b)


def _conv_bwd_pre(x, dact, w, b, *, geom, tile):
    T, Lp, pad = geom
    per_seq = Lp // tile
    hb = tile // 8

    def body(x_ref, h_ref, d_ref, w_ref, b_ref, dc_ref, dw_ref, db_ref):
        i = pl.program_id(0)
        pos = (i % per_seq) * tile + lax.broadcasted_iota(jnp.int32, (tile, 1), 0)
        ext = jnp.concatenate([h_ref[...], x_ref[...]], axis=0)
        taps = _conv_taps(ext, tile)
        acc = b_ref[...] + taps[0] * w_ref[0:1, :]
        for k in range(1, CONV_K):
            acc = acc + taps[k] * w_ref[k:k + 1, :]
        sg = _sigmoid(acc)
        dsilu = sg * (1.0 + acc * (1.0 - sg))
        dc = jnp.where(pos >= pad, d_ref[...] * dsilu, 0.0)
        dc_ref[...] = dc
        dw = jnp.concatenate([_colsum(dc * taps[k]) for k in range(CONV_K)], axis=0)
        db = _colsum(dc)

        @pl.when(i == 0)
        def _():
            dw_ref[...] = dw
            db_ref[...] = db

        @pl.when(i > 0)
        def _():
            dw_ref[...] += dw
            db_ref[...] += db

    return pl.pallas_call(
        body, name="conv_bwd_pre", grid=(T // tile,),
        in_specs=[pl.BlockSpec((tile, CONV_DIM), lambda i: (i, 0)),
                  pl.BlockSpec((8, CONV_DIM), lambda i: (jnp.maximum(i * hb - 1, 0), 0)),
                  pl.BlockSpec((tile, CONV_DIM), lambda i: (i, 0)),
                  pl.BlockSpec((CONV_K, CONV_DIM), lambda i: (0, 0)),
                  pl.BlockSpec((1, CONV_DIM), lambda i: (0, 0))],
        out_specs=[pl.BlockSpec((tile, CONV_DIM), lambda i: (i, 0)),
                   pl.BlockSpec((CONV_K, CONV_DIM), lambda i: (0, 0)),
                   pl.BlockSpec((1, CONV_DIM), lambda i: (0, 0))],
        out_shape=[jax.ShapeDtypeStruct((T, CONV_DIM), F32),
                   jax.ShapeDtypeStruct((CONV_K, CONV_DIM), F32),
                   jax.ShapeDtypeStruct((1, CONV_DIM), F32)],
        compiler_params=_cp(("arbitrary",)),
    )(x, x, dact, w, b)


def _conv_bwd_x(dc, w, *, geom, tile):
    T, Lp, pad = geom
    nt = T // tile
    hb = tile // 8

    def body(d_ref, h_ref, w_ref, o_ref):
        i = pl.program_id(0)
        halo = jnp.where(i < nt - 1, h_ref[...], 0.0)
        ext = jnp.concatenate([d_ref[...], halo], axis=0)
        n_ext = tile + 8
        acc = ext[0:tile] * w_ref[CONV_K - 1:CONV_K, :]
        for j in range(1, CONV_K):
            acc = acc + pltpu.roll(ext, n_ext - j, 0)[0:tile] * w_ref[CONV_K - 1 - j:CONV_K - j, :]
        o_ref[...] = acc.astype(o_ref.dtype)

    return pl.pallas_call(
        body, name="conv_bwd_x", grid=(nt,),
        in_specs=[pl.BlockSpec((tile, CONV_DIM), lambda i: (i, 0)),
                  pl.BlockSpec((8, CONV_DIM), lambda i: (jnp.minimum((i + 1) * hb, nt * hb - 1), 0)),
                  pl.BlockSpec((CONV_K, CONV_DIM), lambda i: (0, 0))],
        out_specs=pl.BlockSpec((tile, CONV_DIM), lambda i: (i, 0)),
        out_shape=jax.ShapeDtypeStruct((T, CONV_DIM), BF16),
        compiler_params=_cp(("arbitrary",)),
    )(dc, dc, w)


def _ssd_common(sm_ref, alog_ref):
    lane = lax.broadcasted_iota(jnp.int32, (1, CH), 1)
    A = jnp.where(lane < SSD_H, -jnp.exp(alog_ref[...]), 0.0)
    dt = sm_ref[...]
    adt = dt * A
    r = lax.broadcasted_iota(jnp.int32, (CH, CH), 0)
    c = lax.broadcasted_iota(jnp.int32, (CH, CH), 1)
    tril = (r >= c).astype(F32)
    cs = jnp.dot(tril, adt, precision=HI, preferred_element_type=F32)
    csT = cs.T
    cs_last = jnp.sum(jnp.where(r == CH - 1, cs, 0.0), axis=0, keepdims=True)
    return A, dt, cs, csT, cs_last, tril, r, c


def _nt(a, b):
    return lax.dot_general(a, b, (((1,), (1,)), ((), ())), preferred_element_type=F32)


def _tn(a, b):
    return lax.dot_general(a, b, (((0,), (0,)), ((), ())), preferred_element_type=F32)


def _nn(a, b):
    return jnp.dot(a, b, preferred_element_type=F32)


def _ssd_fwd(xbc, sm, alog, *, B, NC):
    T = B * NC * CH

    def body(x_ref, sm_ref, alog_ref, y_ref, st_ref, S):
        cidx = pl.program_id(1)

        @pl.when(cidx == 0)
        def _():
            S[...] = jnp.zeros_like(S)

        st_ref[0] = S[...]
        A, dt, cs, csT, cs_last, tril, _, _ = _ssd_common(sm_ref, alog_ref)
        for g in range(SSD_G):
            Bg = x_ref[:, D + g * SSD_N:D + (g + 1) * SSD_N]
            Cg = x_ref[:, D + SSD_G * SSD_N + g * SSD_N:D + SSD_G * SSD_N + (g + 1) * SSD_N]
            Cb = Cg.astype(BF16)
            G = _nt(Cb, Bg.astype(BF16))
            for rr in range(SSD_H // SSD_G):
                h = g * (SSD_H // SSD_G) + rr
                col = cs[:, h:h + 1]
                row = csT[h:h + 1, :]
                Ld = jnp.where(tril > 0, jnp.exp(jnp.minimum(col - row, 0.0)), 0.0)
                M = (G * Ld).astype(BF16)
                xdt = (x_ref[:, h * SSD_P:(h + 1) * SSD_P] * dt[:, h:h + 1]).astype(BF16)
                ST = S[h]
                y = _nn(M, xdt) + jnp.exp(col) * _nt(Cb, ST.astype(BF16))
                y_ref[:, h * SSD_P:(h + 1) * SSD_P] = y
                cl = cs_last[:, h:h + 1]
                Bd = (Bg * jnp.exp(cl - col)).astype(BF16)
                S[h] = jnp.exp(cl) * ST + _tn(xdt, Bd)

    return pl.pallas_call(
        body, name="ssd_fwd", grid=(B, NC),
        in_specs=[pl.BlockSpec((CH, CONV_DIM), lambda b, c: (b * NC + c, 0)),
                  pl.BlockSpec((CH, CH), lambda b, c: (b * NC + c, 0)),
                  pl.BlockSpec((1, CH), lambda b, c: (0, 0))],
        out_specs=[pl.BlockSpec((CH, D), lambda b, c: (b * NC + c, 0)),
                   pl.BlockSpec((1, SSD_H, SSD_P, SSD_N), lambda b, c: (b * NC + c, 0, 0, 0))],
        out_shape=[jax.ShapeDtypeStruct((T, D), F32),
                   jax.ShapeDtypeStruct((B * NC, SSD_H, SSD_P, SSD_N), F32)],
        scratch_shapes=[pltpu.VMEM((SSD_H, SSD_P, SSD_N), F32)],
        compiler_params=_cp(("arbitrary", "arbitrary")),
    )(xbc, sm, alog)


def _ssd_bwd(xbc, sm, alog, states, dy, dxs_skip, *, B, NC):
    T = B * NC * CH

    def rix(b, c):
        return b * NC + (NC - 1 - c)

    def body(x_ref, sm_ref, alog_ref, st_ref, dy_ref, sk_ref, dx_ref, ddt_ref, dal_ref, dS):
        bidx = pl.program_id(0)
        cidx = pl.program_id(1)

        @pl.when(cidx == 0)
        def _():
            dS[...] = jnp.zeros_like(dS)

        A, dt, cs, csT, cs_last, tril, r, c = _ssd_common(sm_ref, alog_ref)
        lane = lax.broadcasted_iota(jnp.int32, (1, CH), 1)
        DCcol = jnp.zeros((CH, CH), F32)
        DCrow = jnp.zeros((CH, CH), F32)
        DX = jnp.zeros((CH, CH), F32)
        dlast = jnp.zeros((1, CH), F32)
        for g in range(SSD_G):
            ob = D + g * SSD_N
            oc = D + SSD_G * SSD_N + g * SSD_N
            Bg = x_ref[:, ob:ob + SSD_N]
            Cg = x_ref[:, oc:oc + SSD_N]
            Bb = Bg.astype(BF16)
            Cb = Cg.astype(BF16)
            G = _nt(Cb, Bb)
            dG = jnp.zeros((CH, CH), F32)
            dBg = jnp.zeros((CH, SSD_N), F32)
            dCg = jnp.zeros((CH, SSD_N), F32)
            for rr in range(SSD_H // SSD_G):
                h = g * (SSD_H // SSD_G) + rr
                sl = slice(h * SSD_P, (h + 1) * SSD_P)
                col = cs[:, h:h + 1]
                row = csT[h:h + 1, :]
                Ld = jnp.where(tril > 0, jnp.exp(jnp.minimum(col - row, 0.0)), 0.0)
                Mf = G * Ld
                dth = dt[:, h:h + 1]
                xs_h = x_ref[:, sl]
                xdt = (xs_h * dth).astype(BF16)
                ST = st_ref[0, h]
                STb = ST.astype(BF16)
                dST = dS[h]
                dSTb = dST.astype(BF16)
                dyh = dy_ref[:, sl]
                dyb = dyh.astype(BF16)
                E = jnp.exp(col)
                yo = _nt(Cb, STb)
                dxdt = _tn(Mf.astype(BF16), dyb)
                dM = _nt(dyb, xdt)
                dG = dG + dM * Ld
                W = dM * Mf
                dcol = jnp.sum(W, axis=1, keepdims=True) + jnp.sum(dyh * yo, axis=1, keepdims=True) * E
                drow = -jnp.sum(W, axis=0, keepdims=True)
                dyE = (dyh * E).astype(BF16)
                dCg = dCg + _nn(dyE, STb)
                dS_in = _tn(dyE, Cb)
                cl = cs_last[:, h:h + 1]
                decay = jnp.exp(cl - col)
                Bd = (Bg * decay).astype(BF16)
                dxdt = dxdt + _nt(Bd, dSTb)
                dBd = _nn(xdt, dSTb)
                dBg = dBg + decay * dBd
                dd = jnp.sum(dBd * Bg, axis=1, keepdims=True) * decay
                dcol = dcol - dd
                el = jnp.exp(cl)
                dl = jnp.sum(dd, axis=0, keepdims=True) + el * jnp.sum(
                    jnp.sum(dST * ST, axis=1, keepdims=True), axis=0, keepdims=True)
                dS[h] = dS_in + el * dST
                dx_ref[:, sl] = dxdt * dth + sk_ref[:, sl]
                ddt_x = jnp.sum(dxdt * xs_h, axis=1, keepdims=True)
                DCcol = DCcol + jnp.where(c == h, dcol, 0.0)
                DCrow = DCrow + jnp.where(r == h, drow, 0.0)
                DX = DX + jnp.where(c == h, ddt_x, 0.0)
                dlast = dlast + jnp.where(lane == h, dl, 0.0)
            dGb = dG.astype(BF16)
            dx_ref[:, ob:ob + SSD_N] = dBg + _tn(dGb, Cb)
            dx_ref[:, oc:oc + SSD_N] = dCg + _nn(dGb, Bb)
        DC = DCcol + DCrow.T + jnp.where(r == CH - 1, dlast, 0.0)
        triu = (r <= c).astype(F32)
        dadt = jnp.dot(triu, DC, precision=HI, preferred_element_type=F32)
        ddt_ref[...] = dadt * A + DX
        dal = jnp.sum(dadt * dt, axis=0, keepdims=True) * A

        @pl.when((bidx == 0) & (cidx == 0))
        def _():
            dal_ref[...] = dal

        @pl.when((bidx > 0) | (cidx > 0))
        def _():
            dal_ref[...] += dal

    return pl.pallas_call(
        body, name="ssd_bwd", grid=(B, NC),
        in_specs=[pl.BlockSpec((CH, CONV_DIM), lambda b, c: (rix(b, c), 0)),
                  pl.BlockSpec((CH, CH), lambda b, c: (rix(b, c), 0)),
                  pl.BlockSpec((1, CH), lambda b, c: (0, 0)),
                  pl.BlockSpec((1, SSD_H, SSD_P, SSD_N), lambda b, c: (rix(b, c), 0, 0, 0)),
                  pl.BlockSpec((CH, D), lambda b, c: (rix(b, c), 0)),
                  pl.BlockSpec((CH, D), lambda b, c: (rix(b, c), 0))],
        out_specs=[pl.BlockSpec((CH, CONV_DIM), lambda b, c: (rix(b, c), 0)),
                   pl.BlockSpec((CH, CH), lambda b, c: (rix(b, c), 0)),
                   pl.BlockSpec((1, CH), lambda b, c: (0, 0))],
        out_shape=[jax.ShapeDtypeStruct((T, CONV_DIM), F32),
                   jax.ShapeDtypeStruct((T, CH), F32),
                   jax.ShapeDtypeStruct((1, CH), F32)],
        scratch_shapes=[pltpu.VMEM((SSD_H, SSD_P, SSD_N), F32)],
        compiler_params=_cp(("arbitrary", "arbitrary")),
    )(xbc, sm, alog, states, dy, dxs_skip)


def _cumsum_seq(v, *, B, NC, reverse, name):
    T = B * NC * CH

    def ix(b, c):
        return b * NC + ((NC - 1 - c) if reverse else c)

    def body(v_ref, o_ref, carry):
        cidx = pl.program_id(1)

        @pl.when(cidx == 0)
        def _():
            carry[...] = jnp.zeros_like(carry)

        r = lax.broadcasted_iota(jnp.int32, (CH, CH), 0)
        c = lax.broadcasted_iota(jnp.int32, (CH, CH), 1)
        tri = ((r <= c) if reverse else (r >= c)).astype(F32)
        cs = jnp.dot(tri, v_ref[...], precision=HI, preferred_element_type=F32) + carry[...]
        o_ref[...] = cs
        edge = 0 if reverse else CH - 1
        carry[...] = jnp.sum(jnp.where(r == edge, cs, 0.0), axis=0, keepdims=True)

    return pl.pallas_call(
        body, name=name, grid=(B, NC),
        in_specs=[pl.BlockSpec((CH, CH), lambda b, c: (ix(b, c), 0))],
        out_specs=pl.BlockSpec((CH, CH), lambda b, c: (ix(b, c), 0)),
        out_shape=jax.ShapeDtypeStruct((T, CH), F32),
        scratch_shapes=[pltpu.VMEM((1, CH), F32)],
        compiler_params=_cp(("arbitrary", "arbitrary")),
    )(v)


FOX_BLK = 256


def _fox_keybias(cum, *, B, Lp, pad):
    ck = cum.reshape(B, Lp, CH)[:, :, SSD_H:SSD_H + FOX_H].transpose(0, 2, 1)
    pos = lax.broadcasted_iota(jnp.int32, ck.shape, 2)
    return jnp.where(pos < pad, -NEG, ck).reshape(B * FOX_H, 1, Lp)


def _fox_tril(TB):
    r = lax.broadcasted_iota(jnp.int32, (TB, TB), 0)
    c = lax.broadcasted_iota(jnp.int32, (TB, TB), 1)
    return r >= c


def _fox_fwd(qkv, cumT, *, B, Lp):
    NB, TL = Lp // FOX_BLK, Lp % FOX_BLK
    T = B * Lp
    scale = FOX_DH ** -0.5

    def body(q_ref, k_ref, v_ref, ct_ref, o_ref, lse_ref):
        def query_block(qoff, qn, nfull):
            q = q_ref[pl.ds(qoff, qn), :]

            def block(koff, kn, carry, diag):
                m, l, acc = carry
                k = k_ref[pl.ds(koff, kn), :]
                v = v_ref[pl.ds(koff, kn), :]
                s = _nt(q, k) * scale - ct_ref[0, :, pl.ds(koff, kn)]
                if diag:
                    s = jnp.where(_fox_tril(qn), s, NEG)
                m_new = jnp.maximum(m, jnp.max(s, axis=1, keepdims=True))
                p = jnp.exp(s - m_new)
                alpha = jnp.exp(m - m_new)
                l = alpha * l + jnp.sum(p, axis=1, keepdims=True)
                acc = alpha * acc + _nn(p.astype(BF16), v)
                return m_new, l, acc

            init = (jnp.full((qn, 1), NEG, F32), jnp.zeros((qn, 1), F32), jnp.zeros((qn, FOX_DH), F32))
            carry = lax.fori_loop(0, nfull, lambda j, c: block(pl.multiple_of(j * FOX_BLK, FOX_BLK), FOX_BLK, c,
                                                               False), init)
            m, l, acc = block(qoff, qn, carry, True)
            o_ref[pl.ds(qoff, qn), :] = (acc / l).astype(o_ref.dtype)
            lse_ref[0, 0, pl.ds(qoff, qn), :] = m + jnp.log(l)

        def full(i, _):
            query_block(pl.multiple_of(i * FOX_BLK, FOX_BLK), FOX_BLK, i)
            return 0

        lax.fori_loop(0, NB, full, 0)
        if TL:
            query_block(NB * FOX_BLK, TL, NB)

    return pl.pallas_call(
        body, name="fox_fwd", grid=(B, FOX_H),
        in_specs=[pl.BlockSpec((Lp, FOX_DH), lambda b, h: (b, h)),
                  pl.BlockSpec((Lp, FOX_DH), lambda b, h: (b, FOX_H + h)),
                  pl.BlockSpec((Lp, FOX_DH), lambda b, h: (b, 2 * FOX_H + h)),
                  pl.BlockSpec((1, 1, Lp), lambda b, h: (b * FOX_H + h, 0, 0))],
        out_specs=[pl.BlockSpec((Lp, FOX_DH), lambda b, h: (b, h)),
                   pl.BlockSpec((1, 1, Lp, 1), lambda b, h: (b, h, 0, 0))],
        out_shape=[jax.ShapeDtypeStruct((T, D), BF16),
                   jax.ShapeDtypeStruct((B, FOX_H, Lp, 1), F32)],
        compiler_params=_cp(("arbitrary", "arbitrary")),
    )(qkv, qkv, qkv, cumT)


def _fox_bwd(qkv, dy, o, lse, cumT, *, B, Lp):
    NB, TL = Lp // FOX_BLK, Lp % FOX_BLK
    T = B * Lp
    scale = FOX_DH ** -0.5

    def body(q_ref, k_ref, v_ref, dy_ref, o_ref, lse_ref, ct_ref, dq_ref, dk_ref, dv_ref, dck_ref, dcq_ref, dl_s):
        dq_ref[...] = jnp.zeros_like(dq_ref)
        dcq_ref[...] = jnp.zeros_like(dcq_ref)
        for lo in range(0, Lp, CH):
            sl = slice(lo, lo + CH)
            dl_s[sl, :] = jnp.sum(dy_ref[sl, :] * o_ref[sl, :].astype(F32), axis=1, keepdims=True)

        def key_block(koff, kn, nxt):
            k = k_ref[pl.ds(koff, kn), :]
            v = v_ref[pl.ds(koff, kn), :]
            ck = ct_ref[0, :, pl.ds(koff, kn)]

            def pair(qoff, qn, carry, diag):
                dk, dv, dck = carry
                q = q_ref[pl.ds(qoff, qn), :]
                dob = dy_ref[pl.ds(qoff, qn), :].astype(BF16)
                e = _nt(q, k) * scale - ck - lse_ref[0, 0, pl.ds(qoff, qn), :]
                if diag:
                    e = jnp.where(_fox_tril(qn), e, NEG)
                p = jnp.exp(e)
                dv = dv + _tn(p.astype(BF16), dob)
                ds = p * (_nt(dob, v) - dl_s[pl.ds(qoff, qn), :])
                dsb = ds.astype(BF16)
                dk = dk + _tn(dsb, q)
                dq_ref[pl.ds(qoff, qn), :] += _nn(dsb, k) * scale
                dcq_ref[0, 0, pl.ds(qoff, qn), :] += jnp.sum(ds, axis=1, keepdims=True)
                dck = dck - jnp.sum(ds, axis=0, keepdims=True)
                return dk, dv, dck

            z = jnp.zeros((kn, FOX_DH), F32)
            carry = pair(koff, kn, (z, z, jnp.zeros((1, kn), F32)), True)
            if nxt is not None:
                carry = lax.fori_loop(nxt, NB, lambda i, c: pair(pl.multiple_of(i * FOX_BLK, FOX_BLK), FOX_BLK, c,
                                                                 False), carry)
                if TL:
                    carry = pair(NB * FOX_BLK, TL, carry, False)
            dk, dv, dck = carry
            dk_ref[pl.ds(koff, kn), :] = (dk * scale).astype(dk_ref.dtype)
            dv_ref[pl.ds(koff, kn), :] = dv.astype(dv_ref.dtype)
            dck_ref[0, :, pl.ds(koff, kn)] = dck

        def full(j, _):
            key_block(pl.multiple_of(j * FOX_BLK, FOX_BLK), FOX_BLK, j + 1)
            return 0

        lax.fori_loop(0, NB, full, 0)
        if TL:
            key_block(NB * FOX_BLK, TL, None)

    head = lambda b, h: (b, h)
    row = lambda b, h: (b * FOX_H + h, 0, 0)
    col = lambda b, h: (b, h, 0, 0)
    return pl.pallas_call(
        body, name="fox_bwd", grid=(B, FOX_H),
        in_specs=[pl.BlockSpec((Lp, FOX_DH), head),
                  pl.BlockSpec((Lp, FOX_DH), lambda b, h: (b, FOX_H + h)),
                  pl.BlockSpec((Lp, FOX_DH), lambda b, h: (b, 2 * FOX_H + h)),
                  pl.BlockSpec((Lp, FOX_DH), head),
                  pl.BlockSpec((Lp, FOX_DH), head),
                  pl.BlockSpec((1, 1, Lp, 1), col),
                  pl.BlockSpec((1, 1, Lp), row)],
        out_specs=[pl.BlockSpec((Lp, FOX_DH), head),
                   pl.BlockSpec((Lp, FOX_DH), head),
                   pl.BlockSpec((Lp, FOX_DH), head),
                   pl.BlockSpec((1, 1, Lp), row),
                   pl.BlockSpec((1, 1, Lp, 1), col)],
        out_shape=[jax.ShapeDtypeStruct((T, D), F32),
                   jax.ShapeDtypeStruct((T, D), BF16),
                   jax.ShapeDtypeStruct((T, D), BF16),
                   jax.ShapeDtypeStruct((B * FOX_H, 1, Lp), F32),
                   jax.ShapeDtypeStruct((B, FOX_H, Lp, 1), F32)],
        scratch_shapes=[pltpu.VMEM((Lp, 1), F32)],
        compiler_params=_cp(("arbitrary", "arbitrary")),
    )(qkv, qkv, qkv, dy, o, lse, cumT)


S5_TILE = 8


def _s5_pows(lam_ref, pw, tab, reverse):
    lr = lam_ref[0, :, 0:S5_W]
    li = lam_ref[0, :, S5_W:2 * S5_W]
    if reverse:
        li = -li
    ar, ai = lr, li
    sub = lax.broadcasted_iota(jnp.int32, (S5_TILE, 1), 0)
    for k, s in enumerate((1, 2, 4)):
        keep = (sub < S5_TILE - s) if reverse else (sub >= s)
        pw[k * S5_TILE:(k + 1) * S5_TILE, 0:S5_W] = jnp.where(keep, ar, 0.0)
        pw[k * S5_TILE:(k + 1) * S5_TILE, S5_W:2 * S5_W] = jnp.where(keep, ai, 0.0)
        ar, ai = ar * ar - ai * ai, 2.0 * ar * ai
    ar, ai = lr, li
    for r in range(S5_TILE):
        row = (S5_TILE - 1 - r) if reverse else r
        tab[row:row + 1, 0:S5_W] = ar
        tab[row:row + 1, S5_W:2 * S5_W] = ai
        ar, ai = ar * lr - ai * li, ar * li + ai * lr


def _s5_scan(hs, pw, tab, carry, reverse):
    n = hs.shape[0]
    tr = tab[:, 0:S5_W]
    ti = tab[:, S5_W:2 * S5_W]
    cr = carry[:, 0:S5_W]
    ci = carry[:, S5_W:2 * S5_W]
    order = range(n // S5_TILE)
    for t in (reversed(order) if reverse else order):
        lo = t * S5_TILE
        vr = hs[lo:lo + S5_TILE, 0:S5_W]
        vi = hs[lo:lo + S5_TILE, S5_W:2 * S5_W]
        for k, s in enumerate((1, 2, 4)):
            sh = (S5_TILE - s) if reverse else s
            sr = pltpu.roll(vr, sh, 0)
            si = pltpu.roll(vi, sh, 0)
            ar = pw[k * S5_TILE:(k + 1) * S5_TILE, 0:S5_W]
            ai = pw[k * S5_TILE:(k + 1) * S5_TILE, S5_W:2 * S5_W]
            vr, vi = vr + ar * sr - ai * si, vi + ar * si + ai * sr
        hs[lo:lo + S5_TILE, 0:S5_W] = vr + tr * cr - ti * ci
        hs[lo:lo + S5_TILE, S5_W:2 * S5_W] = vi + tr * ci + ti * cr
        edge = lo if reverse else lo + S5_TILE - 1
        cr = hs[edge:edge + 1, 0:S5_W]
        ci = hs[edge:edge + 1, S5_W:2 * S5_W]
    carry[:, 0:S5_W] = cr
    carry[:, S5_W:2 * S5_W] = ci


def _s5_fwd(u, Bsg, Csg, lam, *, B, NC):
    T = B * NC * CH

    def body(u_ref, b_ref, c_ref, lam_ref, y_ref, h_ref, pw, tab, hs, carry):
        cidx = pl.program_id(2)

        @pl.when(cidx == 0)
        def _():
            _s5_pows(lam_ref, pw, tab, False)
            carry[...] = jnp.zeros_like(carry)

        hs[...] = _nn(u_ref[...].astype(BF16), b_ref[0])
        _s5_scan(hs, pw, tab, carry, False)
        hb = hs[...].astype(BF16)
        h_ref[...] = hb
        y_ref[...] = _nn(hb, c_ref[0])

    return pl.pallas_call(
        body, name="s5_fwd", grid=(B, S5_G // S5_SG, NC),
        in_specs=[pl.BlockSpec((CH, CH), lambda b, s, c: (b * NC + c, s)),
                  pl.BlockSpec((1, CH, 2 * S5_W), lambda b, s, c: (s, 0, 0)),
                  pl.BlockSpec((1, 2 * S5_W, CH), lambda b, s, c: (s, 0, 0)),
                  pl.BlockSpec((1, 1, 2 * S5_W), lambda b, s, c: (s, 0, 0))],
        out_specs=[pl.BlockSpec((CH, CH), lambda b, s, c: (b * NC + c, s)),
                   pl.BlockSpec((CH, 2 * S5_W), lambda b, s, c: (b * NC + c, s))],
        out_shape=[jax.ShapeDtypeStruct((T, D), F32),
                   jax.ShapeDtypeStruct((T, (S5_G // S5_SG) * 2 * S5_W), BF16)],
        scratch_shapes=[pltpu.VMEM((3 * S5_TILE, 2 * S5_W), F32), pltpu.VMEM((S5_TILE, 2 * S5_W), F32),
                        pltpu.VMEM((CH, 2 * S5_W), F32), pltpu.VMEM((1, 2 * S5_W), F32)],
        compiler_params=_cp(("arbitrary", "arbitrary", "arbitrary")),
    )(u, Bsg, Csg, lam)


def _s5_bwd(u, hst, dy, du_skip, Bsg, Csg, lam, *, B, NC):
    T = B * NC * CH
    NS = S5_G // S5_SG
    hb16 = CH // 16

    def rix(b, c):
        return b * NC + (NC - 1 - c)

    def body(u_ref, h_ref, hp_ref, dy_ref, sk_ref, b_ref, c_ref, lam_ref,
             du_ref, db_ref, dc_ref, dl_ref, pw, tab, gs, carry):
        bidx = pl.program_id(1)
        cidx = pl.program_id(2)
        first = (bidx == 0) & (cidx == 0)

        @pl.when(cidx == 0)
        def _():
            _s5_pows(lam_ref, pw, tab, True)
            carry[...] = jnp.zeros_like(carry)

        dyb = dy_ref[...].astype(BF16)
        gs[...] = _nt(dyb, c_ref[0])
        _s5_scan(gs, pw, tab, carry, True)
        gr = gs[:, 0:S5_W]
        gi = gs[:, S5_W:]
        gb = gs[...].astype(BF16)
        du_ref[...] = (_nt(gb, b_ref[0]) + sk_ref[...]).astype(du_ref.dtype)
        ub = u_ref[...].astype(BF16)
        hcur = h_ref[...]
        dB = _tn(ub, gb)
        dC = _tn(hcur, dyb)
        hf = hcur.astype(F32)
        row = lax.broadcasted_iota(jnp.int32, (CH, 1), 0)
        prev_last = jnp.where(cidx < NC - 1, hp_ref[15:16, :].astype(F32), 0.0)
        hprev = jnp.where(row == 0, prev_last, pltpu.roll(hf, 1, 0))
        pr = hprev[:, 0:S5_W]
        pi = hprev[:, S5_W:]
        da = _colsum(gr * pr + gi * pi)
        dbb = _colsum(gi * pr - gr * pi)
        dl = jnp.concatenate([da, dbb], axis=1)

        @pl.when(first)
        def _():
            db_ref[0] = dB
            dc_ref[0] = dC
            dl_ref[0] = dl

        @pl.when(jnp.logical_not(first))
        def _():
            db_ref[0] += dB
            dc_ref[0] += dC
            dl_ref[0] += dl

    return pl.pallas_call(
        body, name="s5_bwd", grid=(NS, B, NC),
        in_specs=[pl.BlockSpec((CH, CH), lambda s, b, c: (rix(b, c), s)),
                  pl.BlockSpec((CH, 2 * S5_W), lambda s, b, c: (rix(b, c), s)),
                  pl.BlockSpec((16, 2 * S5_W), lambda s, b, c: (jnp.maximum(rix(b, c) * hb16 - 1, 0), s)),
                  pl.BlockSpec((CH, CH), lambda s, b, c: (rix(b, c), s)),
                  pl.BlockSpec((CH, CH), lambda s, b, c: (rix(b, c), s)),
                  pl.BlockSpec((1, CH, 2 * S5_W), lambda s, b, c: (s, 0, 0)),
                  pl.BlockSpec((1, 2 * S5_W, CH), lambda s, b, c: (s, 0, 0)),
                  pl.BlockSpec((1, 1, 2 * S5_W), lambda s, b, c: (s, 0, 0))],
        out_specs=[pl.BlockSpec((CH, CH), lambda s, b, c: (rix(b, c), s)),
                   pl.BlockSpec((1, CH, 2 * S5_W), lambda s, b, c: (s, 0, 0)),
                   pl.BlockSpec((1, 2 * S5_W, CH), lambda s, b, c: (s, 0, 0)),
                   pl.BlockSpec((1, 1, 2 * S5_W), lambda s, b, c: (s, 0, 0))],
        out_shape=[jax.ShapeDtypeStruct((T, D), BF16),
                   jax.ShapeDtypeStruct((NS, CH, 2 * S5_W), F32),
                   jax.ShapeDtypeStruct((NS, 2 * S5_W, CH), F32),
                   jax.ShapeDtypeStruct((NS, 1, 2 * S5_W), F32)],
        scratch_shapes=[pltpu.VMEM((3 * S5_TILE, 2 * S5_W), F32), pltpu.VMEM((S5_TILE, 2 * S5_W), F32),
                        pltpu.VMEM((CH, 2 * S5_W), F32), pltpu.VMEM((1, 2 * S5_W), F32)],
        compiler_params=_cp(("arbitrary", "arbitrary", "arbitrary")),
    )(u, hst, hst, dy, du_skip, Bsg, Csg, lam)


def _s5_param_fn(lre, lim, lstep, bre, bim):
    step = jnp.exp(lstep)
    zr = lre * step
    zi = lim * step
    e = jnp.exp(zr)
    a = e * jnp.cos(zi)
    b = e * jnp.sin(zi)
    den = lre * lre + lim * lim
    qr = ((a - 1.0) * lre + b * lim) / den
    qi = (b * lre - (a - 1.0) * lim) / den
    return a, b, qr[None] * bre - qi[None] * bim, qr[None] * bim + qi[None] * bre


_S5_ROWS = S5_G * S5_P // CH


def _s5_tile(v):
    return v.reshape(_S5_ROWS, CH)


def _s5_tile_b(v):
    return v.reshape(S5_G * S5_P, S5_C).T.reshape(S5_C, _S5_ROWS, CH)


def _s5_untile_b(v):
    return v.reshape(S5_C, S5_G * S5_P).T.reshape(S5_G, S5_P, S5_C)


def _s5_params(lre, lim, lstep, bre, bim):
    def body(a_ref, b_ref, c_ref, d_ref, e_ref, o1, o2, o3, o4):
        outs = _s5_param_fn(a_ref[...], b_ref[...], c_ref[...], d_ref[...], e_ref[...])
        for o, v in zip((o1, o2, o3, o4), outs):
            o[...] = v

    shp = [jax.ShapeDtypeStruct(lre.shape, F32)] * 2 + [jax.ShapeDtypeStruct(bre.shape, F32)] * 2
    return pl.pallas_call(body, name="s5_params", out_shape=shp, compiler_params=_cp())(lre, lim, lstep, bre, bim)


def _s5_params_bwd(lre, lim, lstep, bre, bim, da, db, dbr, dbi):
    def body(a_ref, b_ref, c_ref, d_ref, e_ref, g1, g2, g3, g4, o1, o2, o3, o4, o5):
        _, vjp = jax.vjp(_s5_param_fn, a_ref[...], b_ref[...], c_ref[...], d_ref[...], e_ref[...])
        outs = vjp((g1[...], g2[...], g3[...], g4[...]))
        for o, v in zip((o1, o2, o3, o4, o5), outs):
            o[...] = v

    shp = [jax.ShapeDtypeStruct(lre.shape, F32)] * 3 + [jax.ShapeDtypeStruct(bre.shape, F32)] * 2
    return pl.pallas_call(body, name="s5_params_bwd", out_shape=shp, compiler_params=_cp())(
        lre, lim, lstep, bre, bim, da, db, dbr, dbi)


def _s5_blockdiag(br, bi, cre, cim):
    NS = S5_G // S5_SG
    eye = jnp.eye(S5_SG, dtype=F32)

    def bmat(v):
        v = v.reshape(NS, S5_SG, S5_P, S5_C)
        m = jnp.einsum("sgpc,gh->sgchp", v, eye)
        return m.reshape(NS, S5_SG * S5_C, S5_SG * S5_P)

    def cmat(v):
        v = v.reshape(NS, S5_SG, S5_C, S5_P)
        m = jnp.einsum("sgcp,gh->sgphc", v, eye)
        return m.reshape(NS, S5_SG * S5_P, S5_SG * S5_C)

    Bsg = jnp.concatenate([bmat(br), bmat(bi)], axis=2).astype(BF16)
    Csg = jnp.concatenate([cmat(cre), cmat(-cim)], axis=1).astype(BF16)
    return Bsg, Csg


def _s5_unblock(dBsg, dCsg):
    NS = S5_G // S5_SG

    def ub(m):
        m = m.reshape(NS, S5_SG, S5_C, S5_SG, S5_P)
        d = jnp.stack([m[:, g, :, g, :] for g in range(S5_SG)], axis=1)
        return d.transpose(0, 1, 3, 2).reshape(S5_G, S5_P, S5_C)

    def uc(m):
        m = m.reshape(NS, S5_SG, S5_P, S5_SG, S5_C)
        d = jnp.stack([m[:, g, :, g, :] for g in range(S5_SG)], axis=1)
        return d.transpose(0, 1, 3, 2).reshape(S5_G, S5_C, S5_P)

    dbr = ub(dBsg[:, :, 0:S5_W])
    dbi = ub(dBsg[:, :, S5_W:])
    dcr = uc(dCsg[:, 0:S5_W, :])
    dci = -uc(dCsg[:, S5_W:, :])
    return dbr, dbi, dcr, dci


def _loss_head(x, nf, target, *, B, NC, S):
    T = B * NC * CH
    nts = S // CH

    def f(xv, w, t):
        y = _rms(xv, w)
        return 0.5 * _colsum(jnp.mean(jnp.square(y - t), axis=-1, keepdims=True))

    def body(x_ref, w_ref, t_ref, dx_ref, ls_ref, dw_ref):
        i = pl.program_id(0)
        on = (i % NC) > 0
        t = t_ref[...]
        l, vjp = jax.vjp(lambda a, b: f(a, b, t), x_ref[...], w_ref[...])
        dx, dw = vjp(jnp.ones((1, 1), F32))
        g = jnp.where(on, 1.0, 0.0)
        dx_ref[...] = dx * g
        lv = jnp.zeros((1, CH), F32) + l * g

        @pl.when(i == 0)
        def _():
            ls_ref[...] = lv
            dw_ref[...] = dw * g

        @pl.when(i > 0)
        def _():
            ls_ref[...] += lv
            dw_ref[...] += dw * g

    def tix(i):
        return ((i // NC) * nts + jnp.maximum(i % NC - 1, 0), 0)

    return pl.pallas_call(
        body, name="loss_head", grid=(B * NC,),
        in_specs=[pl.BlockSpec((CH, D), lambda i: (i, 0)),
                  pl.BlockSpec((1, D), lambda i: (0, 0)),
                  pl.BlockSpec((CH, D), tix)],
        out_specs=[pl.BlockSpec((CH, D), lambda i: (i, 0)),
                   pl.BlockSpec((1, CH), lambda i: (0, 0)),
                   pl.BlockSpec((1, D), lambda i: (0, 0))],
        out_shape=[jax.ShapeDtypeStruct((T, D), F32),
                   jax.ShapeDtypeStruct((1, CH), F32),
                   jax.ShapeDtypeStruct((1, D), F32)],
        compiler_params=_cp(("arbitrary",)),
    )(x, nf, target)


def _ew(fn, ins, n_out, out_dtypes, *, name, tile=None):
    R, C = ins[0].shape
    tile = tile or _pick(R, (512, 256, 128, 64, 32, 16, 8, 1))
    if tile % 8 != 0:
        tile = R

    def body(*refs):
        outs = fn(*[r[...] for r in refs[:len(ins)]])
        if not isinstance(outs, (tuple, list)):
            outs = (outs,)
        for r, v in zip(refs[len(ins):], outs):
            r[...] = v.astype(r.dtype)

    spec = pl.BlockSpec((tile, C), lambda i: (i, 0))
    res = pl.pallas_call(
        body, name=name, grid=(R // tile,), in_specs=[spec] * len(ins), out_specs=[spec] * n_out,
        out_shape=[jax.ShapeDtypeStruct((R, C), dt) for dt in out_dtypes],
        compiler_params=_cp(("parallel",)),
    )(*ins)
    return res


def _adam_fn(w, g, m, v):
    m = ADAM_B1 * m + (1.0 - ADAM_B1) * g
    v = ADAM_B2 * v + (1.0 - ADAM_B2) * jnp.square(g)
    m_hat = m / (1.0 - ADAM_B1 ** ADAM_STEP)
    v_hat = v / (1.0 - ADAM_B2 ** ADAM_STEP)
    delta = -ADAM_LR * (m_hat / (jnp.sqrt(v_hat) + ADAM_EPS) + ADAM_WD * w)
    return delta, m, v


def _adam(w, g, m, v, name):
    shp = w.shape
    C = shp[-1]
    f = lambda a: a.reshape(-1, C)
    d, nm, nv = _ew(_adam_fn, [f(w), f(g), f(m), f(v)], 3, [F32] * 3, name=name)
    return d.reshape(shp), nm.reshape(shp), nv.reshape(shp)


def _me():
    return lax.axis_index("x"), lax.axis_index("y"), lax.axis_index("c")


def _all_gather(v, name):
    def body(x_ref, out_ref, send_sems, recv_sems, local_sem):
        x, y, c = _me()
        me, sibling = (x, y, c), (x, y, 1 - c)
        chips = [(1 - x, y), (x, 1 - y), (1 - x, 1 - y)]

        def slot(px, py, pc):
            return out_ref.at[4 * px + 2 * py + pc]

        def copy(k, block, to, src=None):
            return pltpu.make_async_remote_copy(
                src_ref=slot(*block) if src is None else src, dst_ref=slot(*block),
                send_sem=send_sems.at[k], recv_sem=recv_sems.at[k], device_id=to, device_id_type=MESH)

        mine = pltpu.make_async_copy(x_ref, slot(*me), local_sem)
        mine.start()
        first = [copy(0, me, sibling, src=x_ref)]
        first += [copy(1 + j, me, (*chip, c), src=x_ref) for j, chip in enumerate(chips)]
        for cp in first:
            cp.start()
        passed = [copy(4 + j, (*chip, c), sibling) for j, chip in enumerate(chips)]
        for j, chip in enumerate(chips):
            copy(1 + j, (*chip, c), me).wait_recv()
            passed[j].start()
        copy(0, sibling, me).wait_recv()
        for j, chip in enumerate(chips):
            copy(4 + j, (*chip, 1 - c), me).wait_recv()
        for cp in first + passed:
            cp.wait_send()
        mine.wait()

    return pl.pallas_call(
        body, name=name, out_shape=jax.ShapeDtypeStruct((N_DEV,) + v.shape, v.dtype),
        in_specs=[pl.BlockSpec(memory_space=pl.ANY)], out_specs=pl.BlockSpec(memory_space=pl.ANY),
        scratch_shapes=[pltpu.SemaphoreType.DMA((7,)), pltpu.SemaphoreType.DMA((7,)), pltpu.SemaphoreType.DMA],
    )(v)


def _swap_core(g, name):
    def body(g_ref, out_ref, send_sems, recv_sems):
        x, y, c = _me()
        cps = [pltpu.make_async_remote_copy(
            src_ref=g_ref.at[q, 1 - c], dst_ref=out_ref.at[q], send_sem=send_sems.at[q], recv_sem=recv_sems.at[q],
            device_id=(x, y, 1 - c), device_id_type=MESH) for q in range(4)]
        for cp in cps:
            cp.start()
        for cp in cps:
            cp.wait()

    return pl.pallas_call(
        body, name=name, out_shape=jax.ShapeDtypeStruct((4,) + g.shape[2:], g.dtype),
        in_specs=[pl.BlockSpec(memory_space=pl.ANY)], out_specs=pl.BlockSpec(memory_space=pl.ANY),
        scratch_shapes=[pltpu.SemaphoreType.DMA((4,)), pltpu.SemaphoreType.DMA((4,))],
    )(g)


def _swap_chips(hb, name):
    flips = [(1, 0), (0, 1), (1, 1)]

    def body(h_ref, out_ref, send_sems, recv_sems):
        x, y, c = _me()
        cps = []
        for j, (fx, fy) in enumerate(flips):
            px = x + fx - 2 * x * fx
            py = y + fy - 2 * y * fy
            cps.append(pltpu.make_async_remote_copy(
                src_ref=h_ref.at[2 * px + py], dst_ref=out_ref.at[j], send_sem=send_sems.at[j],
                recv_sem=recv_sems.at[j], device_id=(px, py, c), device_id_type=MESH))
        for cp in cps:
            cp.start()
        for cp in cps:
            cp.wait()

    return pl.pallas_call(
        body, name=name, out_shape=jax.ShapeDtypeStruct((3,) + hb.shape[1:], hb.dtype),
        in_specs=[pl.BlockSpec(memory_space=pl.ANY)], out_specs=pl.BlockSpec(memory_space=pl.ANY),
        scratch_shapes=[pltpu.SemaphoreType.DMA((3,)), pltpu.SemaphoreType.DMA((3,))],
    )(hb)


def _reduce_scatter(g8, tag):
    shard = g8.shape[1:]
    C = shard[-1]
    x, y, c = _me()
    g4 = g8.reshape((4, 2) + shard)
    got = _swap_core(g4, "rs_core_" + tag)
    mine = lax.dynamic_index_in_dim(g4, c, axis=1, keepdims=False)
    h, hb = _ew(lambda a, b: (a + b, a + b), [mine.reshape(-1, C), got.reshape(-1, C)], 2, [F32, BF16],
                name="rs_pair_sum_" + tag)
    own = lax.dynamic_index_in_dim(h.reshape((4,) + shard), 2 * x + y, axis=0, keepdims=False)
    got3 = _swap_chips(hb.reshape((4,) + shard), "rs_chips_" + tag)
    out, = _ew(lambda a, b, c_, d: a + b.astype(F32) + c_.astype(F32) + d.astype(F32),
               [own.reshape(-1, C)] + [got3[j].reshape(-1, C) for j in range(3)], 1, [F32],
               name="rs_chip_sum_" + tag)
    return out.reshape(shard)


def _sum8(a):
    out, = _ew(lambda *v: functools.reduce(lambda p, q: p + q, v), [a[k] for k in range(N_DEV)], 1, [F32],
               name="sum8")
    return out


def _pad_rows(flat, cols, mult):
    n = flat.shape[0]
    per = cols * mult
    tot = ((n + per - 1) // per) * per
    return jnp.pad(flat, (0, tot - n)).reshape(-1, cols)


class _Packer:
    def __init__(self, shapes, mult):
        self.shapes = shapes
        self.sizes = [int(np.prod(s)) for s in shapes]
        self.mult = mult

    def pack(self, arrs, dtype):
        flat = jnp.concatenate([a.reshape(-1).astype(dtype) for a in arrs])
        return _pad_rows(flat, D, self.mult)

    def unpack(self, buf):
        flat = buf.reshape(-1)
        out, o = [], 0
        for s, n in zip(self.shapes, self.sizes):
            out.append(flat[o:o + n].reshape(s))
            o += n
        return out


def _w_cat(w_in_l):
    sm = jnp.concatenate([w_in_l[:, O_DT:O_DT + SSD_H], w_in_l[:, O_F:O_F + FOX_H],
                          jnp.zeros((D, CH - SSD_H - FOX_H), w_in_l.dtype)], axis=1)
    return jnp.concatenate([w_in_l[:, O_Z:O_XBC], w_in_l[:, O_XBC:O_DT], w_in_l[:, O_QKV:O_F],
                            w_in_l[:, O_U:O_G], w_in_l[:, O_G:D_IN], sm], axis=1)


def _w_uncat(g):
    return jnp.concatenate([g[:, OFF_Z:OFF_XBC], g[:, OFF_XBC:OFF_QKV], g[:, OFF_SM:OFF_SM + SSD_H],
                            g[:, OFF_QKV:OFF_U], g[:, OFF_SM + SSD_H:OFF_SM + SSD_H + FOX_H],
                            g[:, OFF_U:OFF_G], g[:, OFF_G:OFF_SM]], axis=1)


def _layer_fwd(x, p, geom, dims):
    B, NC, Lp, pad = dims
    T = geom[0]
    rm = functools.partial(_rowmap, geom=geom)
    sv = {}
    xn1, = rm(_f_norm, [x], [p["norm1"]], [(D, BF16)], [], tile=384 if Lp % 384 == 0 else CH, name="norm1")
    Wc = p["w_cat"]
    pz = _mm(xn1, Wc, "nn", F32, n=D, b_off=OFF_Z, name="in_z")
    pxbc = _mm(xn1, Wc, "nn", F32, n=CONV_DIM, b_off=OFF_XBC, name="in_xbc")
    qkv = _mm(xn1, Wc, "nn", BF16, n=3 * D, b_off=OFF_QKV, name="in_qkv")
    pu = _mm(xn1, Wc, "nn", F32, n=D, b_off=OFF_U, name="in_u")
    pg = _mm(xn1, Wc, "nn", F32, n=3 * D, b_off=OFF_G, name="in_g")
    psm = _mm(xn1, Wc, "nn", F32, n=CH, b_off=OFF_SM, name="in_sm")
    t_r = 384 if Lp % 384 == 0 else CH
    sm, = rm(_smallact, [psm], [p["smallbias"]], [(CH, F32)], [], tile=t_r, name="smallact")
    xbc = _conv_fwd(pxbc, p["conv_w"], p["conv_b"], geom=geom, tile=CH)
    y_ssd, states = _ssd_fwd(xbc, sm, p["a_log"], B=B, NC=NC)
    y_a, = rm(_ssd_post, [y_ssd, (xbc, D, 0), pz], [p["d_rep"], p["ssd_norm"]], [(D, BF16)], [], tile=t_r,
              name="ssd_post")
    cum = _cumsum_seq(sm, B=B, NC=NC, reverse=False, name="fox_cum")
    cumT = _fox_keybias(cum, B=B, Lp=Lp, pad=pad)
    y_b, lse = _fox_fwd(qkv, cumT, B=B, Lp=Lp)
    y_ssm, hst = _s5_fwd(pu, p["Bsg"], p["Csg"], p["lam"], B=B, NC=NC)
    y1, = rm(_s5_pre, [y_ssm, pu], [p["s5_d"]], [(D, BF16)], [], tile=t_r, name="s5_pre")
    tg = _mm(y1, p["w_glu"], "nn", F32, name="s5_glu_mm")
    y_c, = rm(_s5_glu, [y_ssm, pu, tg], [p["s5_d"]], [(D, BF16)], [], tile=t_r, name="s5_glu")
    br = [_mm(yy, p["w_branch"][n], "nn", F32, name=f"branch{n}") for n, yy in enumerate((y_a, y_b, y_c))]
    mix, = rm(_merge, [(pg, D, 0), (pg, D, 1), (pg, D, 2)] + br, [], [(D, BF16)], [], tile=CH, name="merge")
    x_mid = _mm(mix, p["w_out"], "nn", F32, res=x, name="out_proj")
    xn2, = rm(_f_norm, [x_mid], [p["norm2"]], [(D, BF16)], [], tile=t_r, name="norm2")
    hff = _mm(xn2, p["w_ffn_in"], "nn", F32, name="ffn_in")
    act, = rm(_swiglu, [(hff, DFF, 0), (hff, DFF, 1)], [], [(DFF, BF16)], [], tile=CH, name="swiglu")
    x_out = _mm(act, p["w_ffn_out"], "nn", F32, res=x_mid, name="ffn_out")
    sv.update(x=x, xn1=xn1, pz=pz, pxbc=pxbc, qkv=qkv, pu=pu, pg=pg, psm=psm, sm=sm, xbc=xbc, y_ssd=y_ssd,
              states=states, y_a=y_a, cum=cum, cumT=cumT, y_b=y_b, lse=lse, y_ssm=y_ssm, hst=hst, y1=y1, tg=tg,
              y_c=y_c, br=br, mix=mix, x_mid=x_mid, xn2=xn2, hff=hff, act=act)
    return x_out, sv


def _layer_bwd(dx_out, p, sv, geom, dims):
    B, NC, Lp, pad = dims
    T = geom[0]
    rm = functools.partial(_rowmap, geom=geom)
    t_r = 384 if Lp % 384 == 0 else CH
    g = {}
    dact = _mm(dx_out, p["w_ffn_out"], "nt", F32, name="ffn_out_dx")
    g["w_ffn_out"] = _mm(sv["act"], dx_out, "tn", F32, name="ffn_out_dw")
    dhff, = rm(_b_swiglu, [(sv["hff"], DFF, 0), (sv["hff"], DFF, 1), dact], [], [(2 * DFF, BF16)], [], tile=CH,
               name="swiglu_bwd")
    dxn2 = _mm(dhff, p["w_ffn_in"], "nt", F32, name="ffn_in_dx")
    g["w_ffn_in"] = _mm(sv["xn2"], dhff, "tn", F32, name="ffn_in_dw")
    dx_mid, g["norm2"] = rm(_b_norm, [sv["x_mid"], dxn2, dx_out], [p["norm2"]], [(D, F32)], [(1, D)], tile=t_r,
                            name="norm2_bwd")
    dmix = _mm(dx_mid, p["w_out"], "nt", F32, name="out_proj_dx")
    g["w_out"] = _mm(sv["mix"], dx_mid, "tn", F32, name="out_proj_dw")
    pg = sv["pg"]
    dpg, db0, db1, db2 = rm(_b_merge, [(pg, D, 0), (pg, D, 1), (pg, D, 2)] + sv["br"] + [dmix], [],
                            [(3 * D, BF16), (D, BF16), (D, BF16), (D, BF16)], [], tile=CH, name="merge_bwd")
    ys = (sv["y_a"], sv["y_b"], sv["y_c"])
    dbs = (db0, db1, db2)
    g["w_branch"] = [_mm(ys[n], dbs[n], "tn", F32, name=f"branch{n}_dw") for n in range(3)]
    dy = [_mm(dbs[n], p["w_branch"][n], "nt", F32, name=f"branch{n}_dx") for n in range(3)]
    dtg, dy1a = rm(_b_s5_glu, [sv["y_ssm"], sv["pu"], sv["tg"], dy[2]], [p["s5_d"]], [(D, BF16), (D, F32)], [],
                   tile=t_r, name="s5_glu_bwd")
    dy1b = _mm(dtg, p["w_glu"], "nt", F32, name="s5_glu_mm_dx")
    g["w_glu"] = _mm(sv["y1"], dtg, "tn", F32, name="s5_glu_mm_dw")
    dys, du_skip, g["s5_d"] = rm(_b_s5_pre, [sv["y_ssm"], sv["pu"], dy1a, dy1b], [p["s5_d"]],
                                 [(D, F32), (D, F32)], [(1, D)], tile=t_r, name="s5_pre_bwd")
    du, g["Bsg"], g["Csg"], g["lam"] = _s5_bwd(sv["pu"], sv["hst"], dys, du_skip, p["Bsg"], p["Csg"], p["lam"],
                                               B=B, NC=NC)
    dq, dk, dv, dckT, dcq = _fox_bwd(sv["qkv"], dy[1], sv["y_b"], sv["lse"], sv["cumT"], B=B, Lp=Lp)
    dcum8 = dcq.reshape(B, FOX_H, Lp).transpose(0, 2, 1) + dckT.reshape(B, FOX_H, Lp).transpose(0, 2, 1)
    dcum = jnp.pad(dcum8.reshape(T, FOX_H), ((0, 0), (SSD_H, CH - SSD_H - FOX_H)))
    dlogf = _cumsum_seq(dcum, B=B, NC=NC, reverse=True, name="fox_cum_bwd")
    dy_ssd, dxs_skip, dz, g["d_rep"], g["ssd_norm"] = rm(
        _b_ssd_post, [sv["y_ssd"], (sv["xbc"], D, 0), sv["pz"], dy[0]], [p["d_rep"], p["ssd_norm"]],
        [(D, F32), (D, F32), (D, BF16)], [(1, D), (1, D)], tile=t_r, name="ssd_post_bwd")
    dxbc_act, ddt, g["a_log"] = _ssd_bwd(sv["xbc"], sv["sm"], p["a_log"], sv["states"], dy_ssd, dxs_skip, B=B, NC=NC)
    dpsm, g["smallbias"] = rm(_b_smallact, [sv["psm"], ddt, dlogf], [p["smallbias"]], [(CH, BF16)], [(1, CH)],
                              tile=t_r, name="smallact_bwd")
    dconv, g["conv_w"], g["conv_b"] = _conv_bwd_pre(sv["pxbc"], dxbc_act, p["conv_w"], p["conv_b"], geom=geom, tile=CH)
    dpxbc = _conv_bwd_x(dconv, p["conv_w"], geom=geom, tile=CH)
    dproj = jnp.concatenate([dz, dpxbc, dq.astype(BF16), dk, dv, du, dpg, dpsm], axis=1)
    dxn1 = _mm(dproj, p["w_cat"], "nt", F32, name="in_dx")
    g["w_cat"] = _mm(sv["xn1"], dproj, "tn", F32, name="in_dw")
    dx_in, g["norm1"] = rm(_b_norm, [sv["x"], dxn1, dx_mid], [p["norm1"]], [(D, F32)], [(1, D)], tile=t_r,
                           name="norm1_bwd")
    return dx_in, g


_BIG = ["w_in", "s5_w_glu", "w_branch", "w_out", "w_ffn_in", "w_ffn_out"]
_NAMES = ['meta', 'norm1', 'w_in', 'ssd_conv_w', 'ssd_conv_b', 'ssd_dt_bias', 'ssd_a_log', 'ssd_d', 'ssd_norm',
          'fox_bf', 's5_lam_re', 's5_lam_im', 's5_b_re', 's5_b_im', 's5_c_re', 's5_c_im', 's5_log_step', 's5_d',
          's5_w_glu', 'w_branch', 'w_out', 'norm2', 'w_ffn_in', 'w_ffn_out', 'norm_f']
_SHARD_AXIS = {"meta": 1, "ssd_conv_w": 2}
_BIG_AXIS = {"w_in": 2, "s5_w_glu": 1, "w_branch": 2, "w_out": 1, "w_ffn_in": 2, "w_ffn_out": 1}


def kernel(x, meta, norm1, w_in, ssd_conv_w, ssd_conv_b, ssd_dt_bias, ssd_a_log, ssd_d, ssd_norm, fox_bf, s5_lam_re, s5_lam_im, s5_b_re, s5_b_im, s5_c_re, s5_c_im, s5_log_step, s5_d, s5_w_glu, w_branch, w_out, norm2, w_ffn_in, w_ffn_out, norm_f, loss_target, m_meta, m_norm1, m_w_in, m_ssd_conv_w, m_ssd_conv_b, m_ssd_dt_bias, m_ssd_a_log, m_ssd_d, m_ssd_norm, m_fox_bf, m_s5_lam_re, m_s5_lam_im, m_s5_b_re, m_s5_b_im, m_s5_c_re, m_s5_c_im, m_s5_log_step, m_s5_d, m_s5_w_glu, m_w_branch, m_w_out, m_norm2, m_w_ffn_in, m_w_ffn_out, m_norm_f, v_meta, v_norm1, v_w_in, v_ssd_conv_w, v_ssd_conv_b, v_ssd_dt_bias, v_ssd_a_log, v_ssd_d, v_ssd_norm, v_fox_bf, v_s5_lam_re, v_s5_lam_im, v_s5_b_re, v_s5_b_im, v_s5_c_re, v_s5_c_im, v_s5_log_step, v_s5_d, v_s5_w_glu, v_w_branch, v_w_out, v_norm2, v_w_ffn_in, v_w_ffn_out, v_norm_f):
    args = locals()
    W = {n: args[n] for n in _NAMES}
    Mo = {n: args["m_" + n] for n in _NAMES}
    Vo = {n: args["v_" + n] for n in _NAMES}
    B, S, _ = x.shape
    depth = norm1.shape[0]
    L = S + N_META
    Lp = ((L + CH - 1) // CH) * CH
    pad = Lp - L
    assert pad + N_META == CH and S % CH == 0
    NC = Lp // CH
    T = B * Lp
    geom = (T, Lp, pad)
    dims = (B, NC, Lp, pad)
    xi, yi, ci = _me()
    dev = 4 * xi + 2 * yi + ci

    gath = {n: _all_gather(W[n].astype(BF16), "gather_" + n) for n in _BIG}
    full = {n: jnp.concatenate([gath[n][k] for k in range(N_DEV)], axis=_BIG_AXIS[n]) for n in _BIG}
    sm_pack = _Packer([meta.shape, ssd_conv_w.shape], 8)
    sm_g = _all_gather(sm_pack.pack([meta, ssd_conv_w], F32), "gather_small")
    sm_parts = [sm_pack.unpack(sm_g[k]) for k in range(N_DEV)]
    meta_full = jnp.concatenate([sm_parts[k][0] for k in range(N_DEV)], axis=1)
    conv_w_full = jnp.concatenate([sm_parts[k][1] for k in range(N_DEV)], axis=2)

    layers = []
    s5_in = []
    for l in range(depth):
        lre = _s5_tile(s5_lam_re[l])
        lim = _s5_tile(s5_lam_im[l])
        lst = _s5_tile(jnp.repeat(s5_log_step[l], S5_P))
        bre = _s5_tile_b(s5_b_re[l])
        bim = _s5_tile_b(s5_b_im[l])
        s5_in.append((lre, lim, lst, bre, bim))
        a, b, br_, bi_ = _s5_params(lre, lim, lst, bre, bim)
        Bsg, Csg = _s5_blockdiag(_s5_untile_b(br_), _s5_untile_b(bi_), s5_c_re[l], s5_c_im[l])
        NS = S5_G // S5_SG
        lam = jnp.concatenate([a.reshape(NS, 1, S5_W), b.reshape(NS, 1, S5_W)], axis=2)
        zpad = jnp.zeros((CH - SSD_H - FOX_H,), F32)
        layers.append(dict(
            norm1=norm1[l][None], w_cat=_w_cat(full["w_in"][l]),
            smallbias=jnp.concatenate([ssd_dt_bias[l], fox_bf[l], zpad])[None],
            conv_w=conv_w_full[l], conv_b=ssd_conv_b[l][None],
            a_log=jnp.concatenate([ssd_a_log[l], jnp.zeros((CH - SSD_H,), F32)])[None],
            d_rep=jnp.repeat(ssd_d[l], SSD_P)[None], ssd_norm=ssd_norm[l][None],
            Bsg=Bsg, Csg=Csg, lam=lam, s5_d=s5_d[l][None], w_glu=full["s5_w_glu"][l],
            w_branch=[full["w_branch"][l, n] for n in range(3)], w_out=full["w_out"][l],
            norm2=norm2[l][None], w_ffn_in=full["w_ffn_in"][l], w_ffn_out=full["w_ffn_out"][l]))

    xs = jnp.concatenate([jnp.zeros((B, pad, D), F32), jnp.broadcast_to(meta_full[None], (B, N_META, D)), x], axis=1)
    h = xs.reshape(T, D)
    saved = []
    for l in range(depth):
        h, sv = _layer_fwd(h, layers[l], geom, dims)
        saved.append(sv)
    dh, loss_row, g_nf = _loss_head(h, norm_f[None], loss_target.reshape(B * S, D), B=B, NC=NC, S=S)
    loss = lax.psum(loss_row[0, 0], AXES)

    G = {n: [None] * depth for n in _NAMES}
    for l in reversed(range(depth)):
        dh, g = _layer_bwd(dh, layers[l], saved[l], geom, dims)
        saved[l] = None
        G["norm1"][l] = g["norm1"][0]
        G["norm2"][l] = g["norm2"][0]
        G["w_in"][l] = _w_uncat(g["w_cat"])
        G["ssd_conv_w"][l] = g["conv_w"]
        G["ssd_conv_b"][l] = g["conv_b"][0]
        G["ssd_dt_bias"][l] = g["smallbias"][0, 0:SSD_H]
        G["fox_bf"][l] = g["smallbias"][0, SSD_H:SSD_H + FOX_H]
        G["ssd_a_log"][l] = g["a_log"][0, 0:SSD_H]
        G["ssd_d"][l] = g["d_rep"].reshape(SSD_H, SSD_P).sum(axis=1)
        G["ssd_norm"][l] = g["ssd_norm"][0]
        dbr, dbi, dcr, dci = _s5_unblock(g["Bsg"], g["Csg"])
        da = _s5_tile(g["lam"][:, 0, 0:S5_W])
        db = _s5_tile(g["lam"][:, 0, S5_W:])
        dlre, dlim, dlst, dbre, dbim = _s5_params_bwd(*s5_in[l], da, db, _s5_tile_b(dbr), _s5_tile_b(dbi))
        G["s5_lam_re"][l] = dlre.reshape(S5_G, S5_P)
        G["s5_lam_im"][l] = dlim.reshape(S5_G, S5_P)
        G["s5_log_step"][l] = dlst.reshape(S5_G, S5_P).sum(axis=1)
        G["s5_b_re"][l] = _s5_untile_b(dbre)
        G["s5_b_im"][l] = _s5_untile_b(dbim)
        G["s5_c_re"][l] = dcr
        G["s5_c_im"][l] = dci
        G["s5_d"][l] = g["s5_d"][0]
        G["s5_w_glu"][l] = g["w_glu"]
        G["w_branch"][l] = jnp.stack(g["w_branch"])
        G["w_out"][l] = g["w_out"]
        G["w_ffn_in"][l] = g["w_ffn_in"]
        G["w_ffn_out"][l] = g["w_ffn_out"]
    dxs = dh.reshape(B, Lp, D)
    grad_x = dxs[:, pad + N_META:, :]
    part = {n: jnp.stack(G[n]) for n in _NAMES if n not in ("meta", "norm_f")}
    part["meta"] = dxs[:, pad:pad + N_META, :].sum(axis=0)
    part["norm_f"] = g_nf[0]

    grads = {}
    for n in _BIG:
        ax = _BIG_AXIS[n]
        a = part[n]
        a = a.reshape(a.shape[:ax] + (N_DEV, a.shape[ax] // N_DEV) + a.shape[ax + 1:])
        grads[n] = _reduce_scatter(jnp.moveaxis(a, ax, 0), n)

    small = [n for n in _NAMES if n not in _BIG]
    sp = _Packer([part[n].shape for n in small], 128)
    tot = sp.unpack(_sum8(_all_gather(sp.pack([part[n] for n in small], F32), "gather_small_grads")))
    for n, t in zip(small, tot):
        if n in _SHARD_AXIS:
            ax = _SHARD_AXIS[n]
            w = W[n].shape[ax]
            t = lax.dynamic_slice_in_dim(t, dev * w, w, axis=ax)
        grads[n] = t

    delta, new_m, new_v = {}, {}, {}
    for n in _BIG:
        delta[n], new_m[n], new_v[n] = _adam(W[n], grads[n], Mo[n], Vo[n], "adam_" + n)
    ap = _Packer([W[n].shape for n in small], 128)
    d_, m_, v_ = _adam(ap.pack([W[n] for n in small], F32), ap.pack([grads[n] for n in small], F32),
                       ap.pack([Mo[n] for n in small], F32), ap.pack([Vo[n] for n in small], F32), "adam_small")
    for n, a, b, c in zip(small, ap.unpack(d_), ap.unpack(m_), ap.unpack(v_)):
        delta[n], new_m[n], new_v[n] = a, b, c
    return (loss, grad_x, *[grads[n] for n in _NAMES], *[delta[n] for n in _NAMES],
            *[new_m[n] for n in _NAMES], *[new_v[n] for n in _NAMES])
```

```python
import functools
import math

import numpy as np
import jax
import jax.numpy as jnp
from jax import lax
from jax.experimental import pallas as pl
from jax.experimental.pallas import tpu as pltpu

F32 = jnp.float32
BF16 = jnp.bfloat16
AXES = ("x", "y", "c")
MESH = pl.DeviceIdType.MESH
N_DEV = 8

D = 1024
N_META = 16
CH = 128
EPS = 1e-6
NEG = -1e30
SSD_H, SSD_P, SSD_N, SSD_G = 16, 64, 128, 2
CONV_K, CONV_DIM = 4, 1536
FOX_H, FOX_DH = 8, 128
S5_G, S5_P, S5_C = 64, 64, 16
S5_SG = 8
S5_W = S5_SG * S5_P
DFF = 2816
D_IN = 9752
OFF_Z, OFF_XBC, OFF_QKV, OFF_U, OFF_G, OFF_SM, D_CAT = 0, 1024, 2560, 5632, 6656, 9728, 9856
O_Z, O_XBC, O_DT, O_QKV, O_F, O_U, O_G = 0, 1024, 2560, 2576, 5648, 5656, 6680

ADAM_LR, ADAM_B1, ADAM_B2, ADAM_EPS, ADAM_WD, ADAM_STEP = 0.001, 0.9, 0.999, 1e-08, 0.01, 10

VMEM_LIMIT_V7X = 52 * 1024 * 1024
HI = lax.Precision.HIGHEST


def _cp(sem=None):
    return pltpu.CompilerParams(dimension_semantics=sem, vmem_limit_bytes=VMEM_LIMIT_V7X)


def _pick(n, cands):
    for c in cands:
        if n % c == 0:
            return c
    raise ValueError(f"no tile for {n}")


_TILES = (1408, 1024, 896, 768, 512, 384, 256, 128)


def _mm(a, b, mode, out_dtype, *, name, n=None, b_off=0, res=None, tm=None, tn=None, tk=None):
    if mode == "tn":
        K, M = a.shape
    else:
        M, K = a.shape
    if mode == "nt":
        N = b.shape[0]
    else:
        N = n if n is not None else b.shape[1]
    tm = tm or _pick(M, (1024, 768, 512, 384, 256, 128, 64, 16, 8))
    tn = tn or _pick(math.gcd(N, b_off) if b_off else N, (1024, 896, 768, 512, 384, 256, 128))
    tk = tk or _pick(K, _TILES)
    nk = K // tk
    joff = b_off // tn

    def body(*refs):
        if res is None:
            a_ref, b_ref, o_ref, acc = refs
            r_ref = None
        else:
            a_ref, b_ref, r_ref, o_ref, acc = refs
        k = pl.program_id(2)
        av = a_ref[...].astype(BF16)
        bv = b_ref[...].astype(BF16)
        if mode == "nn":
            p = jnp.dot(av, bv, preferred_element_type=F32)
        elif mode == "nt":
            p = lax.dot_general(av, bv, (((1,), (1,)), ((), ())), preferred_element_type=F32)
        else:
            p = lax.dot_general(av, bv, (((0,), (0,)), ((), ())), preferred_element_type=F32)

        @pl.when(k == 0)
        def _():
            acc[...] = p

        @pl.when(k > 0)
        def _():
            acc[...] += p

        @pl.when(k == nk - 1)
        def _():
            r = acc[...]
            if r_ref is not None:
                r = r + r_ref[...]
            o_ref[...] = r.astype(o_ref.dtype)

    if mode == "tn":
        a_spec = pl.BlockSpec((tk, tm), lambda i, j, k: (k, i))
    else:
        a_spec = pl.BlockSpec((tm, tk), lambda i, j, k: (i, k))
    if mode == "nt":
        b_spec = pl.BlockSpec((tn, tk), lambda i, j, k: (j, k))
    else:
        b_spec = pl.BlockSpec((tk, tn), lambda i, j, k: (k, j + joff))
    o_spec = pl.BlockSpec((tm, tn), lambda i, j, k: (i, j))
    in_specs = [a_spec, b_spec] + ([o_spec] if res is not None else [])
    args = (a, b) + ((res,) if res is not None else ())
    return pl.pallas_call(
        body, name=name, grid=(M // tm, N // tn, nk),
        in_specs=in_specs, out_specs=o_spec,
        out_shape=jax.ShapeDtypeStruct((M, N), out_dtype),
        scratch_shapes=[pltpu.VMEM((tm, tn), F32)],
        compiler_params=_cp(("parallel", "parallel", "arbitrary")),
    )(*args)


def _rowmap(fn, row_ins, const_ins, row_outs, acc_outs, *, geom, tile, name):
    T, Lp, pad = geom
    assert Lp % tile == 0
    per_seq = Lp // tile
    specs, args = [], []
    for r in row_ins:
        arr, w, cb = r if isinstance(r, tuple) else (r, r.shape[1], 0)
        specs.append(pl.BlockSpec((tile, w), functools.partial(lambda i, cb: (i, cb), cb=cb)))
        args.append(arr)
    for c in const_ins:
        specs.append(pl.BlockSpec(c.shape, functools.partial(lambda i, nd: (0,) * nd, nd=c.ndim)))
        args.append(c)
    n_r, n_c, n_o, n_a = len(row_ins), len(const_ins), len(row_outs), len(acc_outs)
    out_specs = [pl.BlockSpec((tile, w), lambda i: (i, 0)) for w, _ in row_outs]
    out_specs += [pl.BlockSpec(s, lambda i: (0, 0)) for s in acc_outs]
    out_shape = [jax.ShapeDtypeStruct((T, w), dt) for w, dt in row_outs]
    out_shape += [jax.ShapeDtypeStruct(s, F32) for s in acc_outs]

    def body(*refs):
        i = pl.program_id(0)
        pos = (i % per_seq) * tile + lax.broadcasted_iota(jnp.int32, (tile, 1), 0)
        valid = pos >= pad
        vals = [r[...].astype(F32) for r in refs[:n_r]] + [r[...] for r in refs[n_r:n_r + n_c]]
        outs = fn(valid, *vals)
        if not isinstance(outs, (tuple, list)):
            outs = (outs,)
        orefs = refs[n_r + n_c:]
        for r, v in zip(orefs[:n_o], outs[:n_o]):
            r[...] = v.astype(r.dtype)
        for r, v in zip(orefs[n_o:], outs[n_o:]):
            @pl.when(i == 0)
            def _(r=r, v=v):
                r[...] = v

            @pl.when(i > 0)
            def _(r=r, v=v):
                r[...] += v

    res = pl.pallas_call(
        body, name=name, grid=(T // tile,), in_specs=specs, out_specs=out_specs, out_shape=out_shape,
        compiler_params=_cp(("arbitrary",)),
    )(*args)
    return res


def _sigmoid(x):
    return 1.0 / (1.0 + jnp.exp(-x))


def _silu(x):
    return x * _sigmoid(x)


def _softplus(x):
    return jnp.maximum(x, 0.0) + jnp.log(1.0 + jnp.exp(-jnp.abs(x)))


def _gelu(x):
    return 0.5 * x * (1.0 + jnp.tanh(math.sqrt(2.0 / math.pi) * (x + 0.044715 * x * x * x)))


def _rms(x, w):
    return x * lax.rsqrt(jnp.mean(x * x, axis=-1, keepdims=True) + EPS) * w


def _colsum(v):
    return jnp.sum(v, axis=0, keepdims=True)


def _f_norm(valid, x, w):
    return _rms(x, w)


def _b_norm(valid, x, dxn, dres, w):
    _, vjp = jax.vjp(_rms, x, w)
    dx, dw = vjp(dxn)
    return jnp.where(valid, dx + dres, 0.0), dw


def _smallact(valid, raw, bias):
    lane = lax.broadcasted_iota(jnp.int32, raw.shape, 1)
    v = raw + bias
    dt = _softplus(v)
    logf = -_softplus(-v)
    out = jnp.where(lane < SSD_H, dt, jnp.where(lane < SSD_H + FOX_H, logf, 0.0))
    return jnp.where(valid, out, 0.0)


def _b_smallact(valid, raw, d1, d2, bias):
    _, vjp = jax.vjp(lambda r, b: _smallact(valid, r, b), raw, bias)
    return vjp(d1 + d2)


def _ssd_post(valid, y, xs, z, drep, nw):
    y = (y + xs * drep) * _silu(z)
    return _rms(y, nw)


def _b_ssd_post(valid, y, xs, z, dya, drep, nw):
    _, vjp = jax.vjp(lambda a, b, c, d, e: _ssd_post(valid, a, b, c, d, e), y, xs, z, drep, nw)
    dy, dxs, dz, dd, dn = vjp(dya)
    return dy, dxs, dz, dd, dn


def _s5_pre(valid, ys, u, d):
    return _gelu(ys + d * u)


def _s5_glu(valid, ys, u, t, d):
    y1 = _gelu(ys + d * u)
    return y1 * _sigmoid(t)


def _b_s5_glu(valid, ys, u, t, dyc, d):
    y1 = _gelu(ys + d * u)
    _, vjp = jax.vjp(lambda a, b: a * _sigmoid(b), y1, t)
    dy1, dt = vjp(dyc)
    return dt, dy1


def _b_s5_pre(valid, ys, u, dy1a, dy1b, d):
    _, vjp = jax.vjp(lambda a, b, c: _gelu(a + c * b), ys, u, d)
    dys, du, dd = vjp(dy1a + dy1b)
    return dys, du, dd


def _merge(valid, g0, g1, g2, b0, b1, b2):
    m = _sigmoid(g0) * b0 + _sigmoid(g1) * b1 + _sigmoid(g2) * b2
    return jnp.where(valid, m, 0.0)


def _b_merge(valid, g0, g1, g2, b0, b1, b2, dmix):
    _, vjp = jax.vjp(lambda *a: _merge(valid, *a), g0, g1, g2, b0, b1, b2)
    d = vjp(dmix)
    return jnp.concatenate(d[:3], axis=1), d[3], d[4], d[5]


def _swiglu(valid, g, up):
    return _silu(g) * up


def _b_swiglu(valid, g, up, dact):
    _, vjp = jax.vjp(lambda a, b: _silu(a) * b, g, up)
    dg, dup = vjp(dact)
    return jnp.concatenate([dg, dup], axis=1)


def _conv_taps(ext, tile):
    taps = []
    for k in range(CONV_K):
        sh = CONV_K - 1 - k
        v = ext if sh == 0 else pltpu.roll(ext, sh, 0)
        taps.append(v[8:8 + tile])
    return taps


def _conv_fwd(x, w, b, *, geom, tile):
    T, Lp, pad = geom
    per_seq = Lp // tile
    hb = tile // 8

    def body(x_ref, h_ref, w_ref, b_ref, o_ref):
        i = pl.program_id(0)
        pos = (i % per_seq) * tile + lax.broadcasted_iota(jnp.int32, (tile, 1), 0)
        ext = jnp.concatenate([h_ref[...], x_ref[...]], axis=0)
        taps = _conv_taps(ext, tile)
        acc = b_ref[...] + taps[0] * w_ref[0:1, :]
        for k in range(1, CONV_K):
            acc = acc + taps[k] * w_ref[k:k + 1, :]
        o_ref[...] = jnp.where(pos >= pad, _silu(acc), 0.0)

    return pl.pallas_call(
        body, name="conv_fwd", grid=(T // tile,),
        in_specs=[pl.BlockSpec((tile, CONV_DIM), lambda i: (i, 0)),
                  pl.BlockSpec((8, CONV_DIM), lambda i: (jnp.maximum(i * hb - 1, 0), 0)),
                  pl.BlockSpec((CONV_K, CONV_DIM), lambda i: (0, 0)),
                  pl.BlockSpec((1, CONV_DIM), lambda i: (0, 0))],
        out_specs=pl.BlockSpec((tile, CONV_DIM), lambda i: (i, 0)),
        out_shape=jax.ShapeDtypeStruct((T, CONV_DIM), F32),
        compiler_params=_cp(("arbitrary",)),
    )(x, x, w, b)


def _conv_bwd_pre(x, dact, w, b, *, geom, tile):
    T, Lp, pad = geom
    per_seq = Lp // tile
    hb = tile // 8

    def body(x_ref, h_ref, d_ref, w_ref, b_ref, dc_ref, dw_ref, db_ref):
        i = pl.program_id(0)
        pos = (i % per_seq) * tile + lax.broadcasted_iota(jnp.int32, (tile, 1), 0)
        ext = jnp.concatenate([h_ref[...], x_ref[...]], axis=0)
        taps = _conv_taps(ext, tile)
        acc = b_ref[...] + taps[0] * w_ref[0:1, :]
        for k in range(1, CONV_K):
            acc = acc + taps[k] * w_ref[k:k + 1, :]
        sg = _sigmoid(acc)
        dsilu = sg * (1.0 + acc * (1.0 - sg))
        dc = jnp.where(pos >= pad, d_ref[...] * dsilu, 0.0)
        dc_ref[...] = dc
        dw = jnp.concatenate([_colsum(dc * taps[k]) for k in range(CONV_K)], axis=0)
        db = _colsum(dc)

        @pl.when(i == 0)
        def _():
            dw_ref[...] = dw
            db_ref[...] = db

        @pl.when(i > 0)
        def _():
            dw_ref[...] += dw
            db_ref[...] += db

    return pl.pallas_call(
        body, name="conv_bwd_pre", grid=(T // tile,),
        in_specs=[pl.BlockSpec((tile, CONV_DIM), lambda i: (i, 0)),
                  pl.BlockSpec((8, CONV_DIM), lambda i: (jnp.maximum(i * hb - 1, 0), 0)),
                  pl.BlockSpec((tile, CONV_DIM), lambda i: (i, 0)),
                  pl.BlockSpec((CONV_K, CONV_DIM), lambda i: (0, 0)),
                  pl.BlockSpec((1, CONV_DIM), lambda i: (0, 0))],
        out_specs=[pl.BlockSpec((tile, CONV_DIM), lambda i: (i, 0)),
                   pl.BlockSpec((CONV_K, CONV_DIM), lambda i: (0, 0)),
                   pl.BlockSpec((1, CONV_DIM), lambda i: (0, 0))],
        out_shape=[jax.ShapeDtypeStruct((T, CONV_DIM), F32),
                   jax.ShapeDtypeStruct((CONV_K, CONV_DIM), F32),
                   jax.ShapeDtypeStruct((1, CONV_DIM), F32)],
        compiler_params=_cp(("arbitrary",)),
    )(x, x, dact, w, b)


def _conv_bwd_x(dc, w, *, geom, tile):
    T, Lp, pad = geom
    nt = T // tile
    hb = tile // 8

    def body(d_ref, h_ref, w_ref, o_ref):
        i = pl.program_id(0)
        halo = jnp.where(i < nt - 1, h_ref[...], 0.0)
        ext = jnp.concatenate([d_ref[...], halo], axis=0)
        n_ext = tile + 8
        acc = ext[0:tile] * w_ref[CONV_K - 1:CONV_K, :]
        for j in range(1, CONV_K):
            acc = acc + pltpu.roll(ext, n_ext - j, 0)[0:tile] * w_ref[CONV_K - 1 - j:CONV_K - j, :]
        o_ref[...] = acc.astype(o_ref.dtype)

    return pl.pallas_call(
        body, name="conv_bwd_x", grid=(nt,),
        in_specs=[pl.BlockSpec((tile, CONV_DIM), lambda i: (i, 0)),
                  pl.BlockSpec((8, CONV_DIM), lambda i: (jnp.minimum((i + 1) * hb, nt * hb - 1), 0)),
                  pl.BlockSpec((CONV_K, CONV_DIM), lambda i: (0, 0))],
        out_specs=pl.BlockSpec((tile, CONV_DIM), lambda i: (i, 0)),
        out_shape=jax.ShapeDtypeStruct((T, CONV_DIM), BF16),
        compiler_params=_cp(("arbitrary",)),
    )(dc, dc, w)


def _ssd_common(sm_ref, alog_ref):
    lane = lax.broadcasted_iota(jnp.int32, (1, CH), 1)
    A = jnp.where(lane < SSD_H, -jnp.exp(alog_ref[...]), 0.0)
    dt = sm_ref[...]
    adt = dt * A
    r = lax.broadcasted_iota(jnp.int32, (CH, CH), 0)
    c = lax.broadcasted_iota(jnp.int32, (CH, CH), 1)
    tril = (r >= c).astype(F32)
    cs = jnp.dot(tril, adt, precision=HI, preferred_element_type=F32)
    csT = cs.T
    cs_last = jnp.sum(jnp.where(r == CH - 1, cs, 0.0), axis=0, keepdims=True)
    return A, dt, cs, csT, cs_last, tril, r, c


def _nt(a, b):
    return lax.dot_general(a, b, (((1,), (1,)), ((), ())), preferred_element_type=F32)


def _tn(a, b):
    return lax.dot_general(a, b, (((0,), (0,)), ((), ())), preferred_element_type=F32)


def _nn(a, b):
    return jnp.dot(a, b, preferred_element_type=F32)


def _ssd_fwd(xbc, sm, alog, *, B, NC):
    T = B * NC * CH

    def body(x_ref, sm_ref, alog_ref, y_ref, st_ref, S):
        cidx = pl.program_id(1)

        @pl.when(cidx == 0)
        def _():
            S[...] = jnp.zeros_like(S)

        st_ref[0] = S[...]
        A, dt, cs, csT, cs_last, tril, _, _ = _ssd_common(sm_ref, alog_ref)
        for g in range(SSD_G):
            Bg = x_ref[:, D + g * SSD_N:D + (g + 1) * SSD_N]
            Cg = x_ref[:, D + SSD_G * SSD_N + g * SSD_N:D + SSD_G * SSD_N + (g + 1) * SSD_N]
            Cb = Cg.astype(BF16)
            G = _nt(Cb, Bg.astype(BF16))
            for rr in range(SSD_H // SSD_G):
                h = g * (SSD_H // SSD_G) + rr
                col = cs[:, h:h + 1]
                row = csT[h:h + 1, :]
                Ld = jnp.where(tril > 0, jnp.exp(jnp.minimum(col - row, 0.0)), 0.0)
                M = (G * Ld).astype(BF16)
                xdt = (x_ref[:, h * SSD_P:(h + 1) * SSD_P] * dt[:, h:h + 1]).astype(BF16)
                ST = S[h]
                y = _nn(M, xdt) + jnp.exp(col) * _nt(Cb, ST.astype(BF16))
                y_ref[:, h * SSD_P:(h + 1) * SSD_P] = y
                cl = cs_last[:, h:h + 1]
                Bd = (Bg * jnp.exp(cl - col)).astype(BF16)
                S[h] = jnp.exp(cl) * ST + _tn(xdt, Bd)

    return pl.pallas_call(
        body, name="ssd_fwd", grid=(B, NC),
        in_specs=[pl.BlockSpec((CH, CONV_DIM), lambda b, c: (b * NC + c, 0)),
                  pl.BlockSpec((CH, CH), lambda b, c: (b * NC + c, 0)),
                  pl.BlockSpec((1, CH), lambda b, c: (0, 0))],
        out_specs=[pl.BlockSpec((CH, D), lambda b, c: (b * NC + c, 0)),
                   pl.BlockSpec((1, SSD_H, SSD_P, SSD_N), lambda b, c: (b * NC + c, 0, 0, 0))],
        out_shape=[jax.ShapeDtypeStruct((T, D), F32),
                   jax.ShapeDtypeStruct((B * NC, SSD_H, SSD_P, SSD_N), F32)],
        scratch_shapes=[pltpu.VMEM((SSD_H, SSD_P, SSD_N), F32)],
        compiler_params=_cp(("arbitrary", "arbitrary")),
    )(xbc, sm, alog)


def _ssd_bwd(xbc, sm, alog, states, dy, dxs_skip, *, B, NC):
    T = B * NC * CH

    def rix(b, c):
        return b * NC + (NC - 1 - c)

    def body(x_ref, sm_ref, alog_ref, st_ref, dy_ref, sk_ref, dx_ref, ddt_ref, dal_ref, dS):
        bidx = pl.program_id(0)
        cidx = pl.program_id(1)

        @pl.when(cidx == 0)
        def _():
            dS[...] = jnp.zeros_like(dS)

        A, dt, cs, csT, cs_last, tril, r, c = _ssd_common(sm_ref, alog_ref)
        lane = lax.broadcasted_iota(jnp.int32, (1, CH), 1)
        DCcol = jnp.zeros((CH, CH), F32)
        DCrow = jnp.zeros((CH, CH), F32)
        DX = jnp.zeros((CH, CH), F32)
        dlast = jnp.zeros((1, CH), F32)
        for g in range(SSD_G):
            ob = D + g * SSD_N
            oc = D + SSD_G * SSD_N + g * SSD_N
            Bg = x_ref[:, ob:ob + SSD_N]
            Cg = x_ref[:, oc:oc + SSD_N]
            Bb = Bg.astype(BF16)
            Cb = Cg.astype(BF16)
            G = _nt(Cb, Bb)
            dG = jnp.zeros((CH, CH), F32)
            dBg = jnp.zeros((CH, SSD_N), F32)
            dCg = jnp.zeros((CH, SSD_N), F32)
            for rr in range(SSD_H // SSD_G):
                h = g * (SSD_H // SSD_G) + rr
                sl = slice(h * SSD_P, (h + 1) * SSD_P)
                col = cs[:, h:h + 1]
                row = csT[h:h + 1, :]
                Ld = jnp.where(tril > 0, jnp.exp(jnp.minimum(col - row, 0.0)), 0.0)
                Mf = G * Ld
                dth = dt[:, h:h + 1]
                xs_h = x_ref[:, sl]
                xdt = (xs_h * dth).astype(BF16)
                ST = st_ref[0, h]
                STb = ST.astype(BF16)
                dST = dS[h]
                dSTb = dST.astype(BF16)
                dyh = dy_ref[:, sl]
                dyb = dyh.astype(BF16)
                E = jnp.exp(col)
                yo = _nt(Cb, STb)
                dxdt = _tn(Mf.astype(BF16), dyb)
                dM = _nt(dyb, xdt)
                dG = dG + dM * Ld
                W = dM * Mf
                dcol = jnp.sum(W, axis=1, keepdims=True) + jnp.sum(dyh * yo, axis=1, keepdims=True) * E
                drow = -jnp.sum(W, axis=0, keepdims=True)
                dyE = (dyh * E).astype(BF16)
                dCg = dCg + _nn(dyE, STb)
                dS_in = _tn(dyE, Cb)
                cl = cs_last[:, h:h + 1]
                decay = jnp.exp(cl - col)
                Bd = (Bg * decay).astype(BF16)
                dxdt = dxdt + _nt(Bd, dSTb)
                dBd = _nn(xdt, dSTb)
                dBg = dBg + decay * dBd
                dd = jnp.sum(dBd * Bg, axis=1, keepdims=True) * decay
                dcol = dcol - dd
                el = jnp.exp(cl)
                dl = jnp.sum(dd, axis=0, keepdims=True) + el * jnp.sum(
                    jnp.sum(dST * ST, axis=1, keepdims=True), axis=0, keepdims=True)
                dS[h] = dS_in + el * dST
                dx_ref[:, sl] = dxdt * dth + sk_ref[:, sl]
                ddt_x = jnp.sum(dxdt * xs_h, axis=1, keepdims=True)
                DCcol = DCcol + jnp.where(c == h, dcol, 0.0)
                DCrow = DCrow + jnp.where(r == h, drow, 0.0)
                DX = DX + jnp.where(c == h, ddt_x, 0.0)
                dlast = dlast + jnp.where(lane == h, dl, 0.0)
            dGb = dG.astype(BF16)
            dx_ref[:, ob:ob + SSD_N] = dBg + _tn(dGb, Cb)
            dx_ref[:, oc:oc + SSD_N] = dCg + _nn(dGb, Bb)
        DC = DCcol + DCrow.T + jnp.where(r == CH - 1, dlast, 0.0)
        triu = (r <= c).astype(F32)
        dadt = jnp.dot(triu, DC, precision=HI, preferred_element_type=F32)
        ddt_ref[...] = dadt * A + DX
        dal = jnp.sum(dadt * dt, axis=0, keepdims=True) * A

        @pl.when((bidx == 0) & (cidx == 0))
        def _():
            dal_ref[...] = dal

        @pl.when((bidx > 0) | (cidx > 0))
        def _():
            dal_ref[...] += dal

    return pl.pallas_call(
        body, name="ssd_bwd", grid=(B, NC),
        in_specs=[pl.BlockSpec((CH, CONV_DIM), lambda b, c: (rix(b, c), 0)),
                  pl.BlockSpec((CH, CH), lambda b, c: (rix(b, c), 0)),
                  pl.BlockSpec((1, CH), lambda b, c: (0, 0)),
                  pl.BlockSpec((1, SSD_H, SSD_P, SSD_N), lambda b, c: (rix(b, c), 0, 0, 0)),
                  pl.BlockSpec((CH, D), lambda b, c: (rix(b, c), 0)),
                  pl.BlockSpec((CH, D), lambda b, c: (rix(b, c), 0))],
        out_specs=[pl.BlockSpec((CH, CONV_DIM), lambda b, c: (rix(b, c), 0)),
                   pl.BlockSpec((CH, CH), lambda b, c: (rix(b, c), 0)),
                   pl.BlockSpec((1, CH), lambda b, c: (0, 0))],
        out_shape=[jax.ShapeDtypeStruct((T, CONV_DIM), F32),
                   jax.ShapeDtypeStruct((T, CH), F32),
                   jax.ShapeDtypeStruct((1, CH), F32)],
        scratch_shapes=[pltpu.VMEM((SSD_H, SSD_P, SSD_N), F32)],
        compiler_params=_cp(("arbitrary", "arbitrary")),
    )(xbc, sm, alog, states, dy, dxs_skip)


def _cumsum_seq(v, *, B, NC, reverse, name):
    T = B * NC * CH

    def ix(b, c):
        return b * NC + ((NC - 1 - c) if reverse else c)

    def body(v_ref, o_ref, carry):
        cidx = pl.program_id(1)

        @pl.when(cidx == 0)
        def _():
            carry[...] = jnp.zeros_like(carry)

        r = lax.broadcasted_iota(jnp.int32, (CH, CH), 0)
        c = lax.broadcasted_iota(jnp.int32, (CH, CH), 1)
        tri = ((r <= c) if reverse else (r >= c)).astype(F32)
        cs = jnp.dot(tri, v_ref[...], precision=HI, preferred_element_type=F32) + carry[...]
        o_ref[...] = cs
        edge = 0 if reverse else CH - 1
        carry[...] = jnp.sum(jnp.where(r == edge, cs, 0.0), axis=0, keepdims=True)

    return pl.pallas_call(
        body, name=name, grid=(B, NC),
        in_specs=[pl.BlockSpec((CH, CH), lambda b, c: (ix(b, c), 0))],
        out_specs=pl.BlockSpec((CH, CH), lambda b, c: (ix(b, c), 0)),
        out_shape=jax.ShapeDtypeStruct((T, CH), F32),
        scratch_shapes=[pltpu.VMEM((1, CH), F32)],
        compiler_params=_cp(("arbitrary", "arbitrary")),
    )(v)


def _fox_tb(Lp):
    return 384 if (Lp % 384 == 0 and Lp > 384) else CH


def _fox_keybias(cum, *, B, Lp, pad):
    ck = cum.reshape(B, Lp, CH)[:, :, SSD_H:SSD_H + FOX_H].transpose(0, 2, 1)
    pos = lax.broadcasted_iota(jnp.int32, ck.shape, 2)
    return jnp.where(pos < pad, -NEG, ck).reshape(B * FOX_H, 1, Lp)


def _fox_tril(TB):
    r = lax.broadcasted_iota(jnp.int32, (TB, TB), 0)
    c = lax.broadcasted_iota(jnp.int32, (TB, TB), 1)
    return r >= c


def _fox_fwd(qkv, cumT, *, B, Lp):
    TB = _fox_tb(Lp)
    NQ = Lp // TB
    T = B * Lp
    scale = FOX_DH ** -0.5

    def body(q_ref, k_ref, v_ref, ct_ref, o_ref, lse_ref):
        i = pl.program_id(2)
        q = q_ref[...]

        def block(j, nb, carry, diag):
            m, l, acc = carry
            off = pl.multiple_of(j * TB, TB)
            k = k_ref[pl.ds(off, nb * TB), :]
            v = v_ref[pl.ds(off, nb * TB), :]
            s = _nt(q, k) * scale - ct_ref[0, :, pl.ds(off, nb * TB)]
            if diag:
                s = jnp.where(_fox_tril(TB), s, NEG)
            m_new = jnp.maximum(m, jnp.max(s, axis=1, keepdims=True))
            p = jnp.exp(s - m_new)
            alpha = jnp.exp(m - m_new)
            l = alpha * l + jnp.sum(p, axis=1, keepdims=True)
            acc = alpha * acc + _nn(p.astype(BF16), v)
            return m_new, l, acc

        init = (jnp.full((TB, 1), NEG, F32), jnp.zeros((TB, 1), F32), jnp.zeros((TB, FOX_DH), F32))
        carry = lax.fori_loop(0, i // 2, lambda t, c: block(2 * t, 2, c, False), init)
        carry = lax.fori_loop(2 * (i // 2), i, lambda j, c: block(j, 1, c, False), carry)
        m, l, acc = block(i, 1, carry, True)
        o_ref[...] = (acc / l).astype(o_ref.dtype)
        lse_ref[0, 0] = m + jnp.log(l)

    return pl.pallas_call(
        body, name="fox_fwd", grid=(B, FOX_H, NQ),
        in_specs=[pl.BlockSpec((TB, FOX_DH), lambda b, h, i: (b * NQ + i, h)),
                  pl.BlockSpec((Lp, FOX_DH), lambda b, h, i: (b, FOX_H + h)),
                  pl.BlockSpec((Lp, FOX_DH), lambda b, h, i: (b, 2 * FOX_H + h)),
                  pl.BlockSpec((1, 1, Lp), lambda b, h, i: (b * FOX_H + h, 0, 0))],
        out_specs=[pl.BlockSpec((TB, FOX_DH), lambda b, h, i: (b * NQ + i, h)),
                   pl.BlockSpec((1, 1, TB, 1), lambda b, h, i: (b, h, i, 0))],
        out_shape=[jax.ShapeDtypeStruct((T, D), BF16),
                   jax.ShapeDtypeStruct((B, FOX_H, Lp, 1), F32)],
        compiler_params=_cp(("arbitrary", "arbitrary", "arbitrary")),
    )(qkv, qkv, qkv, cumT)


def _fox_bwd(qkv, dy, o, lse, cumT, *, B, Lp):
    TB = _fox_tb(Lp)
    NQ = Lp // TB
    T = B * Lp
    scale = FOX_DH ** -0.5

    def body(q_ref, k_ref, v_ref, dy_ref, o_ref, lse_ref, ct_ref, dq_ref, dk_ref, dv_ref, dck_ref, dcq_ref, dl_s):
        j = pl.program_id(2)
        k = k_ref[...]
        v = v_ref[...]
        ck = ct_ref[0]

        @pl.when(j == 0)
        def _():
            dq_ref[...] = jnp.zeros_like(dq_ref)
            dcq_ref[...] = jnp.zeros_like(dcq_ref)
            for i in range(NQ):
                sl = slice(i * TB, (i + 1) * TB)
                dl_s[sl, :] = jnp.sum(dy_ref[sl, :].astype(F32) * o_ref[sl, :].astype(F32), axis=1, keepdims=True)

        def block(i, nb, carry, diag):
            dk, dv, dck = carry
            off = pl.multiple_of(i * TB, TB)
            rows = pl.ds(off, nb * TB)
            q = q_ref[rows, :]
            dob = dy_ref[rows, :].astype(BF16)
            e = _nt(q, k) * scale - ck - lse_ref[0, 0, rows, :]
            if diag:
                e = jnp.where(_fox_tril(TB), e, NEG)
            p = jnp.exp(e)
            dv = dv + _tn(p.astype(BF16), dob)
            ds = p * (_nt(dob, v) - dl_s[rows, :])
            dsb = ds.astype(BF16)
            dk = dk + _tn(dsb, q)
            dq_ref[rows, :] += _nn(dsb, k) * scale
            dcq_ref[0, 0, rows, :] += jnp.sum(ds, axis=1, keepdims=True)
            dck = dck - jnp.sum(ds, axis=0, keepdims=True)
            return dk, dv, dck

        z = jnp.zeros((TB, FOX_DH), F32)
        carry = block(j, 1, (z, z, jnp.zeros((1, TB), F32)), True)
        npair = (NQ - 1 - j) // 2
        carry = lax.fori_loop(0, npair, lambda t, c: block(j + 1 + 2 * t, 2, c, False), carry)
        dk, dv, dck = lax.fori_loop(j + 1 + 2 * npair, NQ, lambda i, c: block(i, 1, c, False), carry)
        dk_ref[...] = (dk * scale).astype(dk_ref.dtype)
        dv_ref[...] = dv.astype(dv_ref.dtype)
        dck_ref[0] = dck

    head = lambda b, h, j: (b, h)
    return pl.pallas_call(
        body, name="fox_bwd", grid=(B, FOX_H, NQ),
        in_specs=[pl.BlockSpec((Lp, FOX_DH), head),
                  pl.BlockSpec((TB, FOX_DH), lambda b, h, j: (b * NQ + j, FOX_H + h)),
                  pl.BlockSpec((TB, FOX_DH), lambda b, h, j: (b * NQ + j, 2 * FOX_H + h)),
                  pl.BlockSpec((Lp, FOX_DH), head),
                  pl.BlockSpec((Lp, FOX_DH), head),
                  pl.BlockSpec((1, 1, Lp, 1), lambda b, h, j: (b, h, 0, 0)),
                  pl.BlockSpec((1, 1, TB), lambda b, h, j: (b * FOX_H + h, 0, j))],
        out_specs=[pl.BlockSpec((Lp, FOX_DH), head),
                   pl.BlockSpec((TB, FOX_DH), lambda b, h, j: (b * NQ + j, h)),
                   pl.BlockSpec((TB, FOX_DH), lambda b, h, j: (b * NQ + j, h)),
                   pl.BlockSpec((1, 1, TB), lambda b, h, j: (b * FOX_H + h, 0, j)),
                   pl.BlockSpec((1, 1, Lp, 1), lambda b, h, j: (b, h, 0, 0))],
        out_shape=[jax.ShapeDtypeStruct((T, D), F32),
                   jax.ShapeDtypeStruct((T, D), BF16),
                   jax.ShapeDtypeStruct((T, D), BF16),
                   jax.ShapeDtypeStruct((B * FOX_H, 1, Lp), F32),
                   jax.ShapeDtypeStruct((B, FOX_H, Lp, 1), F32)],
        scratch_shapes=[pltpu.VMEM((Lp, 1), F32)],
        compiler_params=_cp(("arbitrary", "arbitrary", "arbitrary")),
    )(qkv, qkv, qkv, dy, o, lse, cumT)


S5_TILE = 8


def _s5_pows(lam_ref, pw, tab, reverse):
    lr = lam_ref[0, :, 0:S5_W]
    li = lam_ref[0, :, S5_W:2 * S5_W]
    if reverse:
        li = -li
    ar, ai = lr, li
    sub = lax.broadcasted_iota(jnp.int32, (S5_TILE, 1), 0)
    for k, s in enumerate((1, 2, 4)):
        keep = (sub < S5_TILE - s) if reverse else (sub >= s)
        pw[k * S5_TILE:(k + 1) * S5_TILE, 0:S5_W] = jnp.where(keep, ar, 0.0)
        pw[k * S5_TILE:(k + 1) * S5_TILE, S5_W:2 * S5_W] = jnp.where(keep, ai, 0.0)
        ar, ai = ar * ar - ai * ai, 2.0 * ar * ai
    ar, ai = lr, li
    for r in range(S5_TILE):
        row = (S5_TILE - 1 - r) if reverse else r
        tab[row:row + 1, 0:S5_W] = ar
        tab[row:row + 1, S5_W:2 * S5_W] = ai
        ar, ai = ar * lr - ai * li, ar * li + ai * lr


def _s5_scan(hs, pw, tab, carry, reverse):
    n = hs.shape[0]
    tr = tab[:, 0:S5_W]
    ti = tab[:, S5_W:2 * S5_W]
    cr = carry[:, 0:S5_W]
    ci = carry[:, S5_W:2 * S5_W]
    order = range(n // S5_TILE)
    for t in (reversed(order) if reverse else order):
        lo = t * S5_TILE
        vr = hs[lo:lo + S5_TILE, 0:S5_W]
        vi = hs[lo:lo + S5_TILE, S5_W:2 * S5_W]
        for k, s in enumerate((1, 2, 4)):
            sh = (S5_TILE - s) if reverse else s
            sr = pltpu.roll(vr, sh, 0)
            si = pltpu.roll(vi, sh, 0)
            ar = pw[k * S5_TILE:(k + 1) * S5_TILE, 0:S5_W]
            ai = pw[k * S5_TILE:(k + 1) * S5_TILE, S5_W:2 * S5_W]
            vr, vi = vr + ar * sr - ai * si, vi + ar * si + ai * sr
        hs[lo:lo + S5_TILE, 0:S5_W] = vr + tr * cr - ti * ci
        hs[lo:lo + S5_TILE, S5_W:2 * S5_W] = vi + tr * ci + ti * cr
        edge = lo if reverse else lo + S5_TILE - 1
        cr = hs[edge:edge + 1, 0:S5_W]
        ci = hs[edge:edge + 1, S5_W:2 * S5_W]
    carry[:, 0:S5_W] = cr
    carry[:, S5_W:2 * S5_W] = ci


def _s5_fwd(u, Bsg, Csg, lam, *, B, NC):
    T = B * NC * CH

    def body(u_ref, b_ref, c_ref, lam_ref, y_ref, h_ref, pw, tab, hs, carry):
        cidx = pl.program_id(2)

        @pl.when(cidx == 0)
        def _():
            _s5_pows(lam_ref, pw, tab, False)
            carry[...] = jnp.zeros_like(carry)

        hs[...] = _nn(u_ref[...].astype(BF16), b_ref[0])
        _s5_scan(hs, pw, tab, carry, False)
        hb = hs[...].astype(BF16)
        h_ref[...] = hb
        y_ref[...] = _nn(hb, c_ref[0])

    return pl.pallas_call(
        body, name="s5_fwd", grid=(B, S5_G // S5_SG, NC),
        in_specs=[pl.BlockSpec((CH, CH), lambda b, s, c: (b * NC + c, s)),
                  pl.BlockSpec((1, CH, 2 * S5_W), lambda b, s, c: (s, 0, 0)),
                  pl.BlockSpec((1, 2 * S5_W, CH), lambda b, s, c: (s, 0, 0)),
                  pl.BlockSpec((1, 1, 2 * S5_W), lambda b, s, c: (s, 0, 0))],
        out_specs=[pl.BlockSpec((CH, CH), lambda b, s, c: (b * NC + c, s)),
                   pl.BlockSpec((CH, 2 * S5_W), lambda b, s, c: (b * NC + c, s))],
        out_shape=[jax.ShapeDtypeStruct((T, D), F32),
                   jax.ShapeDtypeStruct((T, (S5_G // S5_SG) * 2 * S5_W), BF16)],
        scratch_shapes=[pltpu.VMEM((3 * S5_TILE, 2 * S5_W), F32), pltpu.VMEM((S5_TILE, 2 * S5_W), F32),
                        pltpu.VMEM((CH, 2 * S5_W), F32), pltpu.VMEM((1, 2 * S5_W), F32)],
        compiler_params=_cp(("arbitrary", "arbitrary", "arbitrary")),
    )(u, Bsg, Csg, lam)


def _s5_bwd(u, hst, dy, du_skip, Bsg, Csg, lam, *, B, NC):
    T = B * NC * CH
    NS = S5_G // S5_SG
    hb16 = CH // 16

    def rix(b, c):
        return b * NC + (NC - 1 - c)

    def body(u_ref, h_ref, hp_ref, dy_ref, sk_ref, b_ref, c_ref, lam_ref,
             du_ref, db_ref, dc_ref, dl_ref, pw, tab, gs, carry):
        bidx = pl.program_id(1)
        cidx = pl.program_id(2)
        first = (bidx == 0) & (cidx == 0)

        @pl.when(cidx == 0)
        def _():
            _s5_pows(lam_ref, pw, tab, True)
            carry[...] = jnp.zeros_like(carry)

        dyb = dy_ref[...].astype(BF16)
        gs[...] = _nt(dyb, c_ref[0])
        _s5_scan(gs, pw, tab, carry, True)
        gr = gs[:, 0:S5_W]
        gi = gs[:, S5_W:]
        gb = gs[...].astype(BF16)
        du_ref[...] = (_nt(gb, b_ref[0]) + sk_ref[...]).astype(du_ref.dtype)
        ub = u_ref[...].astype(BF16)
        hcur = h_ref[...]
        dB = _tn(ub, gb)
        dC = _tn(hcur, dyb)
        hf = hcur.astype(F32)
        row = lax.broadcasted_iota(jnp.int32, (CH, 1), 0)
        prev_last = jnp.where(cidx < NC - 1, hp_ref[15:16, :].astype(F32), 0.0)
        hprev = jnp.where(row == 0, prev_last, pltpu.roll(hf, 1, 0))
        pr = hprev[:, 0:S5_W]
        pi = hprev[:, S5_W:]
        da = _colsum(gr * pr + gi * pi)
        dbb = _colsum(gi * pr - gr * pi)
        dl = jnp.concatenate([da, dbb], axis=1)

        @pl.when(first)
        def _():
            db_ref[0] = dB
            dc_ref[0] = dC
            dl_ref[0] = dl

        @pl.when(jnp.logical_not(first))
        def _():
            db_ref[0] += dB
            dc_ref[0] += dC
            dl_ref[0] += dl

    return pl.pallas_call(
        body, name="s5_bwd", grid=(NS, B, NC),
        in_specs=[pl.BlockSpec((CH, CH), lambda s, b, c: (rix(b, c), s)),
                  pl.BlockSpec((CH, 2 * S5_W), lambda s, b, c: (rix(b, c), s)),
                  pl.BlockSpec((16, 2 * S5_W), lambda s, b, c: (jnp.maximum(rix(b, c) * hb16 - 1, 0), s)),
                  pl.BlockSpec((CH, CH), lambda s, b, c: (rix(b, c), s)),
                  pl.BlockSpec((CH, CH), lambda s, b, c: (rix(b, c), s)),
                  pl.BlockSpec((1, CH, 2 * S5_W), lambda s, b, c: (s, 0, 0)),
                  pl.BlockSpec((1, 2 * S5_W, CH), lambda s, b, c: (s, 0, 0)),
                  pl.BlockSpec((1, 1, 2 * S5_W), lambda s, b, c: (s, 0, 0))],
        out_specs=[pl.BlockSpec((CH, CH), lambda s, b, c: (rix(b, c), s)),
                   pl.BlockSpec((1, CH, 2 * S5_W), lambda s, b, c: (s, 0, 0)),
                   pl.BlockSpec((1, 2 * S5_W, CH), lambda s, b, c: (s, 0, 0)),
                   pl.BlockSpec((1, 1, 2 * S5_W), lambda s, b, c: (s, 0, 0))],
        out_shape=[jax.ShapeDtypeStruct((T, D), BF16),
                   jax.ShapeDtypeStruct((NS, CH, 2 * S5_W), F32),
                   jax.ShapeDtypeStruct((NS, 2 * S5_W, CH), F32),
                   jax.ShapeDtypeStruct((NS, 1, 2 * S5_W), F32)],
        scratch_shapes=[pltpu.VMEM((3 * S5_TILE, 2 * S5_W), F32), pltpu.VMEM((S5_TILE, 2 * S5_W), F32),
                        pltpu.VMEM((CH, 2 * S5_W), F32), pltpu.VMEM((1, 2 * S5_W), F32)],
        compiler_params=_cp(("arbitrary", "arbitrary", "arbitrary")),
    )(u, hst, hst, dy, du_skip, Bsg, Csg, lam)


def _s5_param_fn(lre, lim, lstep, bre, bim):
    step = jnp.exp(lstep)
    zr = lre * step
    zi = lim * step
    e = jnp.exp(zr)
    a = e * jnp.cos(zi)
    b = e * jnp.sin(zi)
    den = lre * lre + lim * lim
    qr = ((a - 1.0) * lre + b * lim) / den
    qi = (b * lre - (a - 1.0) * lim) / den
    return a, b, qr[None] * bre - qi[None] * bim, qr[None] * bim + qi[None] * bre


_S5_ROWS = S5_G * S5_P // CH


def _s5_tile(v):
    return v.reshape(_S5_ROWS, CH)


def _s5_tile_b(v):
    return v.reshape(S5_G * S5_P, S5_C).T.reshape(S5_C, _S5_ROWS, CH)


def _s5_untile_b(v):
    return v.reshape(S5_C, S5_G * S5_P).T.reshape(S5_G, S5_P, S5_C)


def _s5_params(lre, lim, lstep, bre, bim):
    def body(a_ref, b_ref, c_ref, d_ref, e_ref, o1, o2, o3, o4):
        outs = _s5_param_fn(a_ref[...], b_ref[...], c_ref[...], d_ref[...], e_ref[...])
        for o, v in zip((o1, o2, o3, o4), outs):
            o[...] = v

    shp = [jax.ShapeDtypeStruct(lre.shape, F32)] * 2 + [jax.ShapeDtypeStruct(bre.shape, F32)] * 2
    return pl.pallas_call(body, name="s5_params", out_shape=shp, compiler_params=_cp())(lre, lim, lstep, bre, bim)


def _s5_params_bwd(lre, lim, lstep, bre, bim, da, db, dbr, dbi):
    def body(a_ref, b_ref, c_ref, d_ref, e_ref, g1, g2, g3, g4, o1, o2, o3, o4, o5):
        _, vjp = jax.vjp(_s5_param_fn, a_ref[...], b_ref[...], c_ref[...], d_ref[...], e_ref[...])
        outs = vjp((g1[...], g2[...], g3[...], g4[...]))
        for o, v in zip((o1, o2, o3, o4, o5), outs):
            o[...] = v

    shp = [jax.ShapeDtypeStruct(lre.shape, F32)] * 3 + [jax.ShapeDtypeStruct(bre.shape, F32)] * 2
    return pl.pallas_call(body, name="s5_params_bwd", out_shape=shp, compiler_params=_cp())(
        lre, lim, lstep, bre, bim, da, db, dbr, dbi)


def _s5_blockdiag(br, bi, cre, cim):
    NS = S5_G // S5_SG
    eye = jnp.eye(S5_SG, dtype=F32)

    def bmat(v):
        v = v.reshape(NS, S5_SG, S5_P, S5_C)
        m = jnp.einsum("sgpc,gh->sgchp", v, eye)
        return m.reshape(NS, S5_SG * S5_C, S5_SG * S5_P)

    def cmat(v):
        v = v.reshape(NS, S5_SG, S5_C, S5_P)
        m = jnp.einsum("sgcp,gh->sgphc", v, eye)
        return m.reshape(NS, S5_SG * S5_P, S5_SG * S5_C)

    Bsg = jnp.concatenate([bmat(br), bmat(bi)], axis=2).astype(BF16)
    Csg = jnp.concatenate([cmat(cre), cmat(-cim)], axis=1).astype(BF16)
    return Bsg, Csg


def _s5_unblock(dBsg, dCsg):
    NS = S5_G // S5_SG

    def ub(m):
        m = m.reshape(NS, S5_SG, S5_C, S5_SG, S5_P)
        d = jnp.stack([m[:, g, :, g, :] for g in range(S5_SG)], axis=1)
        return d.transpose(0, 1, 3, 2).reshape(S5_G, S5_P, S5_C)

    def uc(m):
        m = m.reshape(NS, S5_SG, S5_P, S5_SG, S5_C)
        d = jnp.stack([m[:, g, :, g, :] for g in range(S5_SG)], axis=1)
        return d.transpose(0, 1, 3, 2).reshape(S5_G, S5_C, S5_P)

    dbr = ub(dBsg[:, :, 0:S5_W])
    dbi = ub(dBsg[:, :, S5_W:])
    dcr = uc(dCsg[:, 0:S5_W, :])
    dci = -uc(dCsg[:, S5_W:, :])
    return dbr, dbi, dcr, dci


def _loss_head(x, nf, target, *, B, NC, S):
    T = B * NC * CH
    nts = S // CH

    def f(xv, w, t):
        y = _rms(xv, w)
        return 0.5 * _colsum(jnp.mean(jnp.square(y - t), axis=-1, keepdims=True))

    def body(x_ref, w_ref, t_ref, dx_ref, ls_ref, dw_ref):
        i = pl.program_id(0)
        on = (i % NC) > 0
        t = t_ref[...]
        l, vjp = jax.vjp(lambda a, b: f(a, b, t), x_ref[...], w_ref[...])
        dx, dw = vjp(jnp.ones((1, 1), F32))
        g = jnp.where(on, 1.0, 0.0)
        dx_ref[...] = dx * g
        lv = jnp.zeros((1, CH), F32) + l * g

        @pl.when(i == 0)
        def _():
            ls_ref[...] = lv
            dw_ref[...] = dw * g

        @pl.when(i > 0)
        def _():
            ls_ref[...] += lv
            dw_ref[...] += dw * g

    def tix(i):
        return ((i // NC) * nts + jnp.maximum(i % NC - 1, 0), 0)

    return pl.pallas_call(
        body, name="loss_head", grid=(B * NC,),
        in_specs=[pl.BlockSpec((CH, D), lambda i: (i, 0)),
                  pl.BlockSpec((1, D), lambda i: (0, 0)),
                  pl.BlockSpec((CH, D), tix)],
        out_specs=[pl.BlockSpec((CH, D), lambda i: (i, 0)),
                   pl.BlockSpec((1, CH), lambda i: (0, 0)),
                   pl.BlockSpec((1, D), lambda i: (0, 0))],
        out_shape=[jax.ShapeDtypeStruct((T, D), F32),
                   jax.ShapeDtypeStruct((1, CH), F32),
                   jax.ShapeDtypeStruct((1, D), F32)],
        compiler_params=_cp(("arbitrary",)),
    )(x, nf, target)


def _ew(fn, ins, n_out, out_dtypes, *, name, tile=None):
    R, C = ins[0].shape
    tile = tile or _pick(R, (512, 256, 128, 64, 32, 16, 8, 1))
    if tile % 8 != 0:
        tile = R

    def body(*refs):
        outs = fn(*[r[...] for r in refs[:len(ins)]])
        if not isinstance(outs, (tuple, list)):
            outs = (outs,)
        for r, v in zip(refs[len(ins):], outs):
            r[...] = v.astype(r.dtype)

    spec = pl.BlockSpec((tile, C), lambda i: (i, 0))
    res = pl.pallas_call(
        body, name=name, grid=(R // tile,), in_specs=[spec] * len(ins), out_specs=[spec] * n_out,
        out_shape=[jax.ShapeDtypeStruct((R, C), dt) for dt in out_dtypes],
        compiler_params=_cp(("parallel",)),
    )(*ins)
    return res


def _adam_fn(w, g, m, v):
    m = ADAM_B1 * m + (1.0 - ADAM_B1) * g
    v = ADAM_B2 * v + (1.0 - ADAM_B2) * jnp.square(g)
    m_hat = m / (1.0 - ADAM_B1 ** ADAM_STEP)
    v_hat = v / (1.0 - ADAM_B2 ** ADAM_STEP)
    delta = -ADAM_LR * (m_hat / (jnp.sqrt(v_hat) + ADAM_EPS) + ADAM_WD * w)
    return delta, m, v


def _adam(w, g, m, v, name):
    shp = w.shape
    C = shp[-1]
    f = lambda a: a.reshape(-1, C)
    d, nm, nv = _ew(_adam_fn, [f(w), f(g), f(m), f(v)], 3, [F32] * 3, name=name)
    return d.reshape(shp), nm.reshape(shp), nv.reshape(shp)


def _me():
    return lax.axis_index("x"), lax.axis_index("y"), lax.axis_index("c")


def _all_gather(v, name):
    def body(x_ref, out_ref, send_sems, recv_sems, local_sem):
        x, y, c = _me()
        me, sibling = (x, y, c), (x, y, 1 - c)
        chips = [(1 - x, y), (x, 1 - y), (1 - x, 1 - y)]

        def slot(px, py, pc):
            return out_ref.at[4 * px + 2 * py + pc]

        def copy(k, block, to, src=None):
            return pltpu.make_async_remote_copy(
                src_ref=slot(*block) if src is None else src, dst_ref=slot(*block),
                send_sem=send_sems.at[k], recv_sem=recv_sems.at[k], device_id=to, device_id_type=MESH)

        mine = pltpu.make_async_copy(x_ref, slot(*me), local_sem)
        mine.start()
        first = [copy(0, me, sibling, src=x_ref)]
        first += [copy(1 + j, me, (*chip, c), src=x_ref) for j, chip in enumerate(chips)]
        for cp in first:
            cp.start()
        passed = [copy(4 + j, (*chip, c), sibling) for j, chip in enumerate(chips)]
        for j, chip in enumerate(chips):
            copy(1 + j, (*chip, c), me).wait_recv()
            passed[j].start()
        copy(0, sibling, me).wait_recv()
        for j, chip in enumerate(chips):
            copy(4 + j, (*chip, 1 - c), me).wait_recv()
        for cp in first + passed:
            cp.wait_send()
        mine.wait()

    return pl.pallas_call(
        body, name=name, out_shape=jax.ShapeDtypeStruct((N_DEV,) + v.shape, v.dtype),
        in_specs=[pl.BlockSpec(memory_space=pl.ANY)], out_specs=pl.BlockSpec(memory_space=pl.ANY),
        scratch_shapes=[pltpu.SemaphoreType.DMA((7,)), pltpu.SemaphoreType.DMA((7,)), pltpu.SemaphoreType.DMA],
    )(v)


def _swap_core(g, name):
    def body(g_ref, out_ref, send_sems, recv_sems):
        x, y, c = _me()
        cps = [pltpu.make_async_remote_copy(
            src_ref=g_ref.at[q, 1 - c], dst_ref=out_ref.at[q], send_sem=send_sems.at[q], recv_sem=recv_sems.at[q],
            device_id=(x, y, 1 - c), device_id_type=MESH) for q in range(4)]
        for cp in cps:
            cp.start()
        for cp in cps:
            cp.wait()

    return pl.pallas_call(
        body, name=name, out_shape=jax.ShapeDtypeStruct((4,) + g.shape[2:], g.dtype),
        in_specs=[pl.BlockSpec(memory_space=pl.ANY)], out_specs=pl.BlockSpec(memory_space=pl.ANY),
        scratch_shapes=[pltpu.SemaphoreType.DMA((4,)), pltpu.SemaphoreType.DMA((4,))],
    )(g)


def _swap_chips(hb, name):
    flips = [(1, 0), (0, 1), (1, 1)]

    def body(h_ref, out_ref, send_sems, recv_sems):
        x, y, c = _me()
        cps = []
        for j, (fx, fy) in enumerate(flips):
            px = x + fx - 2 * x * fx
            py = y + fy - 2 * y * fy
            cps.append(pltpu.make_async_remote_copy(
                src_ref=h_ref.at[2 * px + py], dst_ref=out_ref.at[j], send_sem=send_sems.at[j],
                recv_sem=recv_sems.at[j], device_id=(px, py, c), device_id_type=MESH))
        for cp in cps:
            cp.start()
        for cp in cps:
            cp.wait()

    return pl.pallas_call(
        body, name=name, out_shape=jax.ShapeDtypeStruct((3,) + hb.shape[1:], hb.dtype),
        in_specs=[pl.BlockSpec(memory_space=pl.ANY)], out_specs=pl.BlockSpec(memory_space=pl.ANY),
        scratch_shapes=[pltpu.SemaphoreType.DMA((3,)), pltpu.SemaphoreType.DMA((3,))],
    )(hb)


def _reduce_scatter(g8, tag):
    shard = g8.shape[1:]
    C = shard[-1]
    x, y, c = _me()
    g4 = g8.reshape((4, 2) + shard)
    got = _swap_core(g4, "rs_core_" + tag)
    mine = lax.dynamic_index_in_dim(g4, c, axis=1, keepdims=False)
    h, hb = _ew(lambda a, b: (a + b, a + b), [mine.reshape(-1, C), got.reshape(-1, C)], 2, [F32, BF16],
                name="rs_pair_sum_" + tag)
    own = lax.dynamic_index_in_dim(h.reshape((4,) + shard), 2 * x + y, axis=0, keepdims=False)
    got3 = _swap_chips(hb.reshape((4,) + shard), "rs_chips_" + tag)
    out, = _ew(lambda a, b, c_, d: a + b.astype(F32) + c_.astype(F32) + d.astype(F32),
               [own.reshape(-1, C)] + [got3[j].reshape(-1, C) for j in range(3)], 1, [F32],
               name="rs_chip_sum_" + tag)
    return out.reshape(shard)


def _sum8(a):
    out, = _ew(lambda *v: functools.reduce(lambda p, q: p + q, v), [a[k] for k in range(N_DEV)], 1, [F32],
               name="sum8")
    return out


def _pad_rows(flat, cols, mult):
    n = flat.shape[0]
    per = cols * mult
    tot = ((n + per - 1) // per) * per
    return jnp.pad(flat, (0, tot - n)).reshape(-1, cols)


class _Packer:
    def __init__(self, shapes, mult):
        self.shapes = shapes
        self.sizes = [int(np.prod(s)) for s in shapes]
        self.mult = mult

    def pack(self, arrs, dtype):
        flat = jnp.concatenate([a.reshape(-1).astype(dtype) for a in arrs])
        return _pad_rows(flat, D, self.mult)

    def unpack(self, buf):
        flat = buf.reshape(-1)
        out, o = [], 0
        for s, n in zip(self.shapes, self.sizes):
            out.append(flat[o:o + n].reshape(s))
            o += n
        return out


def _w_cat(w_in_l):
    sm = jnp.concatenate([w_in_l[:, O_DT:O_DT + SSD_H], w_in_l[:, O_F:O_F + FOX_H],
                          jnp.zeros((D, CH - SSD_H - FOX_H), w_in_l.dtype)], axis=1)
    return jnp.concatenate([w_in_l[:, O_Z:O_XBC], w_in_l[:, O_XBC:O_DT], w_in_l[:, O_QKV:O_F],
                            w_in_l[:, O_U:O_G], w_in_l[:, O_G:D_IN], sm], axis=1)


def _w_uncat(g):
    return jnp.concatenate([g[:, OFF_Z:OFF_XBC], g[:, OFF_XBC:OFF_QKV], g[:, OFF_SM:OFF_SM + SSD_H],
                            g[:, OFF_QKV:OFF_U], g[:, OFF_SM + SSD_H:OFF_SM + SSD_H + FOX_H],
                            g[:, OFF_U:OFF_G], g[:, OFF_G:OFF_SM]], axis=1)


def _layer_fwd(x, p, geom, dims):
    B, NC, Lp, pad = dims
    T = geom[0]
    rm = functools.partial(_rowmap, geom=geom)
    sv = {}
    xn1, = rm(_f_norm, [x], [p["norm1"]], [(D, BF16)], [], tile=384 if Lp % 384 == 0 else CH, name="norm1")
    Wc = p["w_cat"]
    pz = _mm(xn1, Wc, "nn", BF16, n=D, b_off=OFF_Z, name="in_z")
    pxbc = _mm(xn1, Wc, "nn", F32, n=CONV_DIM, b_off=OFF_XBC, name="in_xbc")
    qkv = _mm(xn1, Wc, "nn", BF16, n=3 * D, b_off=OFF_QKV, name="in_qkv")
    pu = _mm(xn1, Wc, "nn", BF16, n=D, b_off=OFF_U, name="in_u")
    pg = _mm(xn1, Wc, "nn", BF16, n=3 * D, b_off=OFF_G, name="in_g")
    psm = _mm(xn1, Wc, "nn", F32, n=CH, b_off=OFF_SM, name="in_sm")
    t_r = 384 if Lp % 384 == 0 else CH
    sm, = rm(_smallact, [psm], [p["smallbias"]], [(CH, F32)], [], tile=t_r, name="smallact")
    xbc = _conv_fwd(pxbc, p["conv_w"], p["conv_b"], geom=geom, tile=CH)
    y_ssd, states = _ssd_fwd(xbc, sm, p["a_log"], B=B, NC=NC)
    y_a, = rm(_ssd_post, [y_ssd, (xbc, D, 0), pz], [p["d_rep"], p["ssd_norm"]], [(D, BF16)], [], tile=t_r,
              name="ssd_post")
    cum = _cumsum_seq(sm, B=B, NC=NC, reverse=False, name="fox_cum")
    cumT = _fox_keybias(cum, B=B, Lp=Lp, pad=pad)
    y_b, lse = _fox_fwd(qkv, cumT, B=B, Lp=Lp)
    y_ssm, hst = _s5_fwd(pu, p["Bsg"], p["Csg"], p["lam"], B=B, NC=NC)
    y1, = rm(_s5_pre, [y_ssm, pu], [p["s5_d"]], [(D, BF16)], [], tile=t_r, name="s5_pre")
    tg = _mm(y1, p["w_glu"], "nn", BF16, name="s5_glu_mm")
    y_c, = rm(_s5_glu, [y_ssm, pu, tg], [p["s5_d"]], [(D, BF16)], [], tile=t_r, name="s5_glu")
    br = [_mm(yy, p["w_branch"][n], "nn", BF16, name=f"branch{n}") for n, yy in enumerate((y_a, y_b, y_c))]
    mix, = rm(_merge, [(pg, D, 0), (pg, D, 1), (pg, D, 2)] + br, [], [(D, BF16)], [], tile=CH, name="merge")
    x_mid = _mm(mix, p["w_out"], "nn", F32, res=x, name="out_proj")
    xn2, = rm(_f_norm, [x_mid], [p["norm2"]], [(D, BF16)], [], tile=t_r, name="norm2")
    hff = _mm(xn2, p["w_ffn_in"], "nn", BF16, name="ffn_in")
    act, = rm(_swiglu, [(hff, DFF, 0), (hff, DFF, 1)], [], [(DFF, BF16)], [], tile=CH, name="swiglu")
    x_out = _mm(act, p["w_ffn_out"], "nn", F32, res=x_mid, name="ffn_out")
    sv.update(x=x, xn1=xn1, pz=pz, pxbc=pxbc, qkv=qkv, pu=pu, pg=pg, psm=psm, sm=sm, xbc=xbc, y_ssd=y_ssd,
              states=states, y_a=y_a, cum=cum, cumT=cumT, y_b=y_b, lse=lse, y_ssm=y_ssm, hst=hst, y1=y1, tg=tg,
              y_c=y_c, br=br, mix=mix, x_mid=x_mid, xn2=xn2, hff=hff, act=act)
    return x_out, sv


def _layer_bwd(dx_out, p, sv, geom, dims):
    B, NC, Lp, pad = dims
    T = geom[0]
    rm = functools.partial(_rowmap, geom=geom)
    t_r = 384 if Lp % 384 == 0 else CH
    g = {}
    dact = _mm(dx_out, p["w_ffn_out"], "nt", BF16, name="ffn_out_dx")
    g["w_ffn_out"] = _mm(sv["act"], dx_out, "tn", F32, name="ffn_out_dw")
    dhff, = rm(_b_swiglu, [(sv["hff"], DFF, 0), (sv["hff"], DFF, 1), dact], [], [(2 * DFF, BF16)], [], tile=CH,
               name="swiglu_bwd")
    dxn2 = _mm(dhff, p["w_ffn_in"], "nt", F32, name="ffn_in_dx")
    g["w_ffn_in"] = _mm(sv["xn2"], dhff, "tn", F32, name="ffn_in_dw")
    dx_mid, g["norm2"] = rm(_b_norm, [sv["x_mid"], dxn2, dx_out], [p["norm2"]], [(D, F32)], [(1, D)], tile=t_r,
                            name="norm2_bwd")
    dmix = _mm(dx_mid, p["w_out"], "nt", BF16, name="out_proj_dx")
    g["w_out"] = _mm(sv["mix"], dx_mid, "tn", F32, name="out_proj_dw")
    pg = sv["pg"]
    dpg, db0, db1, db2 = rm(_b_merge, [(pg, D, 0), (pg, D, 1), (pg, D, 2)] + sv["br"] + [dmix], [],
                            [(3 * D, BF16), (D, BF16), (D, BF16), (D, BF16)], [], tile=CH, name="merge_bwd")
    ys = (sv["y_a"], sv["y_b"], sv["y_c"])
    dbs = (db0, db1, db2)
    g["w_branch"] = [_mm(ys[n], dbs[n], "tn", F32, name=f"branch{n}_dw") for n in range(3)]
    dy = [_mm(dbs[n], p["w_branch"][n], "nt", BF16, name=f"branch{n}_dx") for n in range(3)]
    dtg, dy1a = rm(_b_s5_glu, [sv["y_ssm"], sv["pu"], sv["tg"], dy[2]], [p["s5_d"]], [(D, BF16), (D, F32)], [],
                   tile=t_r, name="s5_glu_bwd")
    dy1b = _mm(dtg, p["w_glu"], "nt", BF16, name="s5_glu_mm_dx")
    g["w_glu"] = _mm(sv["y1"], dtg, "tn", F32, name="s5_glu_mm_dw")
    dys, du_skip, g["s5_d"] = rm(_b_s5_pre, [sv["y_ssm"], sv["pu"], dy1a, dy1b], [p["s5_d"]],
                                 [(D, F32), (D, F32)], [(1, D)], tile=t_r, name="s5_pre_bwd")
    du, g["Bsg"], g["Csg"], g["lam"] = _s5_bwd(sv["pu"], sv["hst"], dys, du_skip, p["Bsg"], p["Csg"], p["lam"],
                                               B=B, NC=NC)
    dq, dk, dv, dckT, dcq = _fox_bwd(sv["qkv"], dy[1], sv["y_b"], sv["lse"], sv["cumT"], B=B, Lp=Lp)
    dcum8 = dcq.reshape(B, FOX_H, Lp).transpose(0, 2, 1) + dckT.reshape(B, FOX_H, Lp).transpose(0, 2, 1)
    dcum = jnp.pad(dcum8.reshape(T, FOX_H), ((0, 0), (SSD_H, CH - SSD_H - FOX_H)))
    dlogf = _cumsum_seq(dcum, B=B, NC=NC, reverse=True, name="fox_cum_bwd")
    dy_ssd, dxs_skip, dz, g["d_rep"], g["ssd_norm"] = rm(
        _b_ssd_post, [sv["y_ssd"], (sv["xbc"], D, 0), sv["pz"], dy[0]], [p["d_rep"], p["ssd_norm"]],
        [(D, F32), (D, F32), (D, BF16)], [(1, D), (1, D)], tile=t_r, name="ssd_post_bwd")
    dxbc_act, ddt, g["a_log"] = _ssd_bwd(sv["xbc"], sv["sm"], p["a_log"], sv["states"], dy_ssd, dxs_skip, B=B, NC=NC)
    dpsm, g["smallbias"] = rm(_b_smallact, [sv["psm"], ddt, dlogf], [p["smallbias"]], [(CH, BF16)], [(1, CH)],
                              tile=t_r, name="smallact_bwd")
    dconv, g["conv_w"], g["conv_b"] = _conv_bwd_pre(sv["pxbc"], dxbc_act, p["conv_w"], p["conv_b"], geom=geom, tile=CH)
    dpxbc = _conv_bwd_x(dconv, p["conv_w"], geom=geom, tile=CH)
    dproj = jnp.concatenate([dz, dpxbc, dq.astype(BF16), dk, dv, du, dpg, dpsm], axis=1)
    dxn1 = _mm(dproj, p["w_cat"], "nt", F32, name="in_dx")
    g["w_cat"] = _mm(sv["xn1"], dproj, "tn", F32, name="in_dw")
    dx_in, g["norm1"] = rm(_b_norm, [sv["x"], dxn1, dx_mid], [p["norm1"]], [(D, F32)], [(1, D)], tile=t_r,
                           name="norm1_bwd")
    return dx_in, g


_BIG = ["w_in", "s5_w_glu", "w_branch", "w_out", "w_ffn_in", "w_ffn_out"]
_NAMES = ['meta', 'norm1', 'w_in', 'ssd_conv_w', 'ssd_conv_b', 'ssd_dt_bias', 'ssd_a_log', 'ssd_d', 'ssd_norm',
          'fox_bf', 's5_lam_re', 's5_lam_im', 's5_b_re', 's5_b_im', 's5_c_re', 's5_c_im', 's5_log_step', 's5_d',
          's5_w_glu', 'w_branch', 'w_out', 'norm2', 'w_ffn_in', 'w_ffn_out', 'norm_f']
_SHARD_AXIS = {"meta": 1, "ssd_conv_w": 2}
_BIG_AXIS = {"w_in": 2, "s5_w_glu": 1, "w_branch": 2, "w_out": 1, "w_ffn_in": 2, "w_ffn_out": 1}


def kernel(x, meta, norm1, w_in, ssd_conv_w, ssd_conv_b, ssd_dt_bias, ssd_a_log, ssd_d, ssd_norm, fox_bf, s5_lam_re, s5_lam_im, s5_b_re, s5_b_im, s5_c_re, s5_c_im, s5_log_step, s5_d, s5_w_glu, w_branch, w_out, norm2, w_ffn_in, w_ffn_out, norm_f, loss_target, m_meta, m_norm1, m_w_in, m_ssd_conv_w, m_ssd_conv_b, m_ssd_dt_bias, m_ssd_a_log, m_ssd_d, m_ssd_norm, m_fox_bf, m_s5_lam_re, m_s5_lam_im, m_s5_b_re, m_s5_b_im, m_s5_c_re, m_s5_c_im, m_s5_log_step, m_s5_d, m_s5_w_glu, m_w_branch, m_w_out, m_norm2, m_w_ffn_in, m_w_ffn_out, m_norm_f, v_meta, v_norm1, v_w_in, v_ssd_conv_w, v_ssd_conv_b, v_ssd_dt_bias, v_ssd_a_log, v_ssd_d, v_ssd_norm, v_fox_bf, v_s5_lam_re, v_s5_lam_im, v_s5_b_re, v_s5_b_im, v_s5_c_re, v_s5_c_im, v_s5_log_step, v_s5_d, v_s5_w_glu, v_w_branch, v_w_out, v_norm2, v_w_ffn_in, v_w_ffn_out, v_norm_f):
    args = locals()
    W = {n: args[n] for n in _NAMES}
    Mo = {n: args["m_" + n] for n in _NAMES}
    Vo = {n: args["v_" + n] for n in _NAMES}
    B, S, _ = x.shape
    depth = norm1.shape[0]
    L = S + N_META
    Lp = ((L + CH - 1) // CH) * CH
    pad = Lp - L
    assert pad + N_META == CH and S % CH == 0
    NC = Lp // CH
    T = B * Lp
    geom = (T, Lp, pad)
    dims = (B, NC, Lp, pad)
    xi, yi, ci = _me()
    dev = 4 * xi + 2 * yi + ci

    gath = {n: _all_gather(W[n].astype(BF16), "gather_" + n) for n in _BIG}
    full = {n: jnp.concatenate([gath[n][k] for k in range(N_DEV)], axis=_BIG_AXIS[n]) for n in _BIG}
    sm_pack = _Packer([meta.shape, ssd_conv_w.shape], 8)
    sm_g = _all_gather(sm_pack.pack([meta, ssd_conv_w], F32), "gather_small")
    sm_parts = [sm_pack.unpack(sm_g[k]) for k in range(N_DEV)]
    meta_full = jnp.concatenate([sm_parts[k][0] for k in range(N_DEV)], axis=1)
    conv_w_full = jnp.concatenate([sm_parts[k][1] for k in range(N_DEV)], axis=2)

    layers = []
    s5_in = []
    for l in range(depth):
        lre = _s5_tile(s5_lam_re[l])
        lim = _s5_tile(s5_lam_im[l])
        lst = _s5_tile(jnp.repeat(s5_log_step[l], S5_P))
        bre = _s5_tile_b(s5_b_re[l])
        bim = _s5_tile_b(s5_b_im[l])
        s5_in.append((lre, lim, lst, bre, bim))
        a, b, br_, bi_ = _s5_params(lre, lim, lst, bre, bim)
        Bsg, Csg = _s5_blockdiag(_s5_untile_b(br_), _s5_untile_b(bi_), s5_c_re[l], s5_c_im[l])
        NS = S5_G // S5_SG
        lam = jnp.concatenate([a.reshape(NS, 1, S5_W), b.reshape(NS, 1, S5_W)], axis=2)
        zpad = jnp.zeros((CH - SSD_H - FOX_H,), F32)
        layers.append(dict(
            norm1=norm1[l][None], w_cat=_w_cat(full["w_in"][l]),
            smallbias=jnp.concatenate([ssd_dt_bias[l], fox_bf[l], zpad])[None],
            conv_w=conv_w_full[l], conv_b=ssd_conv_b[l][None],
            a_log=jnp.concatenate([ssd_a_log[l], jnp.zeros((CH - SSD_H,), F32)])[None],
            d_rep=jnp.repeat(ssd_d[l], SSD_P)[None], ssd_norm=ssd_norm[l][None],
            Bsg=Bsg, Csg=Csg, lam=lam, s5_d=s5_d[l][None], w_glu=full["s5_w_glu"][l],
            w_branch=[full["w_branch"][l, n] for n in range(3)], w_out=full["w_out"][l],
            norm2=norm2[l][None], w_ffn_in=full["w_ffn_in"][l], w_ffn_out=full["w_ffn_out"][l]))

    xs = jnp.concatenate([jnp.zeros((B, pad, D), F32), jnp.broadcast_to(meta_full[None], (B, N_META, D)), x], axis=1)
    h = xs.reshape(T, D)
    saved = []
    for l in range(depth):
        h, sv = _layer_fwd(h, layers[l], geom, dims)
        saved.append(sv)
    dh, loss_row, g_nf = _loss_head(h, norm_f[None], loss_target.reshape(B * S, D), B=B, NC=NC, S=S)
    loss = lax.psum(loss_row[0, 0], AXES)

    G = {n: [None] * depth for n in _NAMES}
    for l in reversed(range(depth)):
        dh, g = _layer_bwd(dh, layers[l], saved[l], geom, dims)
        saved[l] = None
        G["norm1"][l] = g["norm1"][0]
        G["norm2"][l] = g["norm2"][0]
        G["w_in"][l] = _w_uncat(g["w_cat"])
        G["ssd_conv_w"][l] = g["conv_w"]
        G["ssd_conv_b"][l] = g["conv_b"][0]
        G["ssd_dt_bias"][l] = g["smallbias"][0, 0:SSD_H]
        G["fox_bf"][l] = g["smallbias"][0, SSD_H:SSD_H + FOX_H]
        G["ssd_a_log"][l] = g["a_log"][0, 0:SSD_H]
        G["ssd_d"][l] = g["d_rep"].reshape(SSD_H, SSD_P).sum(axis=1)
        G["ssd_norm"][l] = g["ssd_norm"][0]
        dbr, dbi, dcr, dci = _s5_unblock(g["Bsg"], g["Csg"])
        da = _s5_tile(g["lam"][:, 0, 0:S5_W])
        db = _s5_tile(g["lam"][:, 0, S5_W:])
        dlre, dlim, dlst, dbre, dbim = _s5_params_bwd(*s5_in[l], da, db, _s5_tile_b(dbr), _s5_tile_b(dbi))
        G["s5_lam_re"][l] = dlre.reshape(S5_G, S5_P)
        G["s5_lam_im"][l] = dlim.reshape(S5_G, S5_P)
        G["s5_log_step"][l] = dlst.reshape(S5_G, S5_P).sum(axis=1)
        G["s5_b_re"][l] = _s5_untile_b(dbre)
        G["s5_b_im"][l] = _s5_untile_b(dbim)
        G["s5_c_re"][l] = dcr
        G["s5_c_im"][l] = dci
        G["s5_d"][l] = g["s5_d"][0]
        G["s5_w_glu"][l] = g["w_glu"]
        G["w_branch"][l] = jnp.stack(g["w_branch"])
        G["w_out"][l] = g["w_out"]
        G["w_ffn_in"][l] = g["w_ffn_in"]
        G["w_ffn_out"][l] = g["w_ffn_out"]
    dxs = dh.reshape(B, Lp, D)
    grad_x = dxs[:, pad + N_META:, :]
    part = {n: jnp.stack(G[n]) for n in _NAMES if n not in ("meta", "norm_f")}
    part["meta"] = dxs[:, pad:pad + N_META, :].sum(axis=0)
    part["norm_f"] = g_nf[0]

    grads = {}
    for n in _BIG:
        ax = _BIG_AXIS[n]
        a = part[n]
        a = a.reshape(a.shape[:ax] + (N_DEV, a.shape[ax] // N_DEV) + a.shape[ax + 1:])
        grads[n] = _reduce_scatter(jnp.moveaxis(a, ax, 0), n)

    small = [n for n in _NAMES if n not in _BIG]
    sp = _Packer([part[n].shape for n in small], 128)
    tot = sp.unpack(_sum8(_all_gather(sp.pack([part[n] for n in small], F32), "gather_small_grads")))
    for n, t in zip(small, tot):
        if n in _SHARD_AXIS:
            ax = _SHARD_AXIS[n]
            w = W[n].shape[ax]
            t = lax.dynamic_slice_in_dim(t, dev * w, w, axis=ax)
        grads[n] = t

    delta, new_m, new_v = {}, {}, {}
    for n in _BIG:
        delta[n], new_m[n], new_v[n] = _adam(W[n], grads[n], Mo[n], Vo[n], "adam_" + n)
    ap = _Packer([W[n].shape for n in small], 128)
    d_, m_, v_ = _adam(ap.pack([W[n] for n in small], F32), ap.pack([grads[n] for n in small], F32),
                       ap.pack([Mo[n] for n in small], F32), ap.pack([Vo[n] for n in small], F32), "adam_small")
    for n, a, b, c in zip(small, ap.unpack(d_), ap.unpack(m_), ap.unpack(v_)):
        delta[n], new_m[n], new_v[n] = a, b, c
    return (loss, grad_x, *[grads[n] for n in _NAMES], *[delta[n] for n in _NAMES],
            *[new_m[n] for n in _NAMES], *[new_v[n] for n in _NAMES])
```

```python
import functools
import math

import numpy as np
import jax
import jax.numpy as jnp
from jax import lax
from jax.experimental import pallas as pl
from jax.experimental.pallas import tpu as pltpu

F32 = jnp.float32
BF16 = jnp.bfloat16
AXES = ("x", "y", "c")
MESH = pl.DeviceIdType.MESH
N_DEV = 8

D = 1024
N_META = 16
CH = 128
EPS = 1e-6
NEG = -1e30
SSD_H, SSD_P, SSD_N, SSD_G = 16, 64, 128, 2
CONV_K, CONV_DIM = 4, 1536
FOX_H, FOX_DH = 8, 128
S5_G, S5_P, S5_C = 64, 64, 16
S5_SG = 8
S5_W = S5_SG * S5_P
DFF = 2816
D_IN = 9752
OFF_Z, OFF_XBC, OFF_QKV, OFF_U, OFF_G, OFF_SM, D_CAT = 0, 1024, 2560, 5632, 6656, 9728, 9856
O_Z, O_XBC, O_DT, O_QKV, O_F, O_U, O_G = 0, 1024, 2560, 2576, 5648, 5656, 6680

ADAM_LR, ADAM_B1, ADAM_B2, ADAM_EPS, ADAM_WD, ADAM_STEP = 0.001, 0.9, 0.999, 1e-08, 0.01, 10

VMEM_LIMIT_V7X = 52 * 1024 * 1024
HI = lax.Precision.HIGHEST


def _cp(sem=None):
    return pltpu.CompilerParams(dimension_semantics=sem, vmem_limit_bytes=VMEM_LIMIT_V7X)


def _pick(n, cands):
    for c in cands:
        if n % c == 0:
            return c
    raise ValueError(f"no tile for {n}")


_TILES = (1408, 1024, 896, 768, 512, 384, 256, 128)


def _mm(a, b, mode, out_dtype, *, name, n=None, b_off=0, res=None, tm=None, tn=None, tk=None):
    if mode == "tn":
        K, M = a.shape
    else:
        M, K = a.shape
    if mode == "nt":
        N = b.shape[0]
    else:
        N = n if n is not None else b.shape[1]
    tm = tm or _pick(M, (1024, 768, 512, 384, 256, 128, 64, 16, 8))
    tn = tn or _pick(math.gcd(N, b_off) if b_off else N, (1024, 896, 768, 512, 384, 256, 128))
    tk = tk or _pick(K, _TILES)
    nk = K // tk
    joff = b_off // tn

    def body(*refs):
        if res is None:
            a_ref, b_ref, o_ref, acc = refs
            r_ref = None
        else:
            a_ref, b_ref, r_ref, o_ref, acc = refs
        k = pl.program_id(2)
        av = a_ref[...].astype(BF16)
        bv = b_ref[...].astype(BF16)
        if mode == "nn":
            p = jnp.dot(av, bv, preferred_element_type=F32)
        elif mode == "nt":
            p = lax.dot_general(av, bv, (((1,), (1,)), ((), ())), preferred_element_type=F32)
        else:
            p = lax.dot_general(av, bv, (((0,), (0,)), ((), ())), preferred_element_type=F32)

        @pl.when(k == 0)
        def _():
            acc[...] = p

        @pl.when(k > 0)
        def _():
            acc[...] += p

        @pl.when(k == nk - 1)
        def _():
            r = acc[...]
            if r_ref is not None:
                r = r + r_ref[...]
            o_ref[...] = r.astype(o_ref.dtype)

    if mode == "tn":
        a_spec = pl.BlockSpec((tk, tm), lambda i, j, k: (k, i))
    else:
        a_spec = pl.BlockSpec((tm, tk), lambda i, j, k: (i, k))
    if mode == "nt":
        b_spec = pl.BlockSpec((tn, tk), lambda i, j, k: (j, k))
    else:
        b_spec = pl.BlockSpec((tk, tn), lambda i, j, k: (k, j + joff))
    o_spec = pl.BlockSpec((tm, tn), lambda i, j, k: (i, j))
    in_specs = [a_spec, b_spec] + ([o_spec] if res is not None else [])
    args = (a, b) + ((res,) if res is not None else ())
    return pl.pallas_call(
        body, name=name, grid=(M // tm, N // tn, nk),
        in_specs=in_specs, out_specs=o_spec,
        out_shape=jax.ShapeDtypeStruct((M, N), out_dtype),
        scratch_shapes=[pltpu.VMEM((tm, tn), F32)],
        compiler_params=_cp(("parallel", "parallel", "arbitrary")),
    )(*args)


def _rowmap(fn, row_ins, const_ins, row_outs, acc_outs, *, geom, tile, name):
    T, Lp, pad = geom
    assert Lp % tile == 0
    per_seq = Lp // tile
    specs, args = [], []
    for r in row_ins:
        arr, w, cb = r if isinstance(r, tuple) else (r, r.shape[1], 0)
        specs.append(pl.BlockSpec((tile, w), functools.partial(lambda i, cb: (i, cb), cb=cb)))
        args.append(arr)
    for c in const_ins:
        specs.append(pl.BlockSpec(c.shape, functools.partial(lambda i, nd: (0,) * nd, nd=c.ndim)))
        args.append(c)
    n_r, n_c, n_o, n_a = len(row_ins), len(const_ins), len(row_outs), len(acc_outs)
    out_specs = [pl.BlockSpec((tile, w), lambda i: (i, 0)) for w, _ in row_outs]
    out_specs += [pl.BlockSpec(s, lambda i: (0, 0)) for s in acc_outs]
    out_shape = [jax.ShapeDtypeStruct((T, w), dt) for w, dt in row_outs]
    out_shape += [jax.ShapeDtypeStruct(s, F32) for s in acc_outs]

    def body(*refs):
        i = pl.program_id(0)
        pos = (i % per_seq) * tile + lax.broadcasted_iota(jnp.int32, (tile, 1), 0)
        valid = pos >= pad
        vals = [r[...].astype(F32) for r in refs[:n_r]] + [r[...] for r in refs[n_r:n_r + n_c]]
        outs = fn(valid, *vals)
        if not isinstance(outs, (tuple, list)):
            outs = (outs,)
        orefs = refs[n_r + n_c:]
        for r, v in zip(orefs[:n_o], outs[:n_o]):
            r[...] = v.astype(r.dtype)
        for r, v in zip(orefs[n_o:], outs[n_o:]):
            @pl.when(i == 0)
            def _(r=r, v=v):
                r[...] = v

            @pl.when(i > 0)
            def _(r=r, v=v):
                r[...] += v

    res = pl.pallas_call(
        body, name=name, grid=(T // tile,), in_specs=specs, out_specs=out_specs, out_shape=out_shape,
        compiler_params=_cp(("arbitrary",)),
    )(*args)
    return res


def _sigmoid(x):
    return 1.0 / (1.0 + jnp.exp(-x))


def _silu(x):
    return x * _sigmoid(x)


def _softplus(x):
    return jnp.maximum(x, 0.0) + jnp.log(1.0 + jnp.exp(-jnp.abs(x)))


def _gelu(x):
    return 0.5 * x * (1.0 + jnp.tanh(math.sqrt(2.0 / math.pi) * (x + 0.044715 * x * x * x)))


def _rms(x, w):
    return x * lax.rsqrt(jnp.mean(x * x, axis=-1, keepdims=True) + EPS) * w


def _colsum(v):
    return jnp.sum(v, axis=0, keepdims=True)


def _f_norm(valid, x, w):
    return _rms(x, w)


def _b_norm(valid, x, dxn, dres, w):
    _, vjp = jax.vjp(_rms, x, w)
    dx, dw = vjp(dxn)
    return jnp.where(valid, dx + dres, 0.0), dw


def _smallact(valid, raw, bias):
    lane = lax.broadcasted_iota(jnp.int32, raw.shape, 1)
    v = raw + bias
    dt = _softplus(v)
    logf = -_softplus(-v)
    out = jnp.where(lane < SSD_H, dt, jnp.where(lane < SSD_H + FOX_H, logf, 0.0))
    return jnp.where(valid, out, 0.0)


def _b_smallact(valid, raw, d1, d2, bias):
    _, vjp = jax.vjp(lambda r, b: _smallact(valid, r, b), raw, bias)
    return vjp(d1 + d2)


def _ssd_post(valid, y, xs, z, drep, nw):
    y = (y + xs * drep) * _silu(z)
    return _rms(y, nw)


def _b_ssd_post(valid, y, xs, z, dya, drep, nw):
    _, vjp = jax.vjp(lambda a, b, c, d, e: _ssd_post(valid, a, b, c, d, e), y, xs, z, drep, nw)
    dy, dxs, dz, dd, dn = vjp(dya)
    return dy, dxs, dz, dd, dn


def _s5_pre(valid, ys, u, d):
    return _gelu(ys + d * u)


def _s5_glu(valid, ys, u, t, d):
    y1 = _gelu(ys + d * u)
    return y1 * _sigmoid(t)


def _b_s5_glu(valid, ys, u, t, dyc, d):
    y1 = _gelu(ys + d * u)
    _, vjp = jax.vjp(lambda a, b: a * _sigmoid(b), y1, t)
    dy1, dt = vjp(dyc)
    return dt, dy1


def _b_s5_pre(valid, ys, u, dy1a, dy1b, d):
    _, vjp = jax.vjp(lambda a, b, c: _gelu(a + c * b), ys, u, d)
    dys, du, dd = vjp(dy1a + dy1b)
    return dys, du, dd


def _merge(valid, g0, g1, g2, b0, b1, b2):
    m = _sigmoid(g0) * b0 + _sigmoid(g1) * b1 + _sigmoid(g2) * b2
    return jnp.where(valid, m, 0.0)


def _b_merge(valid, g0, g1, g2, b0, b1, b2, dmix):
    _, vjp = jax.vjp(lambda *a: _merge(valid, *a), g0, g1, g2, b0, b1, b2)
    d = vjp(dmix)
    return jnp.concatenate(d[:3], axis=1), d[3], d[4], d[5]


def _swiglu(valid, g, up):
    return _silu(g) * up


def _b_swiglu(valid, g, up, dact):
    _, vjp = jax.vjp(lambda a, b: _silu(a) * b, g, up)
    dg, dup = vjp(dact)
    return jnp.concatenate([dg, dup], axis=1)


def _conv_taps(ext, tile):
    taps = []
    for k in range(CONV_K):
        sh = CONV_K - 1 - k
        v = ext if sh == 0 else pltpu.roll(ext, sh, 0)
        taps.append(v[8:8 + tile])
    return taps


def _conv_fwd(x, w, b, *, geom, tile):
    T, Lp, pad = geom
    per_seq = Lp // tile
    hb = tile // 8

    def body(x_ref, h_ref, w_ref, b_ref, o_ref):
        i = pl.program_id(0)
        pos = (i % per_seq) * tile + lax.broadcasted_iota(jnp.int32, (tile, 1), 0)
        ext = jnp.concatenate([h_ref[...], x_ref[...]], axis=0)
        taps = _conv_taps(ext, tile)
        acc = b_ref[...] + taps[0] * w_ref[0:1, :]
        for k in range(1, CONV_K):
            acc = acc + taps[k] * w_ref[k:k + 1, :]
        o_ref[...] = jnp.where(pos >= pad, _silu(acc), 0.0)

    return pl.pallas_call(
        body, name="conv_fwd", grid=(T // tile,),
        in_specs=[pl.BlockSpec((tile, CONV_DIM), lambda i: (i, 0)),
                  pl.BlockSpec((8, CONV_DIM), lambda i: (jnp.maximum(i * hb - 1, 0), 0)),
                  pl.BlockSpec((CONV_K, CONV_DIM), lambda i: (0, 0)),
                  pl.BlockSpec((1, CONV_DIM), lambda i: (0, 0))],
        out_specs=pl.BlockSpec((tile, CONV_DIM), lambda i: (i, 0)),
        out_shape=jax.ShapeDtypeStruct((T, CONV_DIM), F32),
        compiler_params=_cp(("arbitrary",)),
    )(x, x, w, b)


def _conv_bwd_pre(x, dact, w, b, *, geom, tile):
    T, Lp, pad = geom
    per_seq = Lp // tile
    hb = tile // 8

    def body(x_ref, h_ref, d_ref, w_ref, b_ref, dc_ref, dw_ref, db_ref):
        i = pl.program_id(0)
        pos = (i % per_seq) * tile + lax.broadcasted_iota(jnp.int32, (tile, 1), 0)
        ext = jnp.concatenate([h_ref[...], x_ref[...]], axis=0)
        taps = _conv_taps(ext, tile)
        acc = b_ref[...] + taps[0] * w_ref[0:1, :]
        for k in range(1, CONV_K):
            acc = acc + taps[k] * w_ref[k:k + 1, :]
        sg = _sigmoid(acc)
        dsilu = sg * (1.0 + acc * (1.0 - sg))
        dc = jnp.where(pos >= pad, d_ref[...] * dsilu, 0.0)
        dc_ref[...] = dc
        dw = jnp.concatenate([_colsum(dc * taps[k]) for k in range(CONV_K)], axis=0)
        db = _colsum(dc)

        @pl.when(i == 0)
        def _():
            dw_ref[...] = dw
            db_ref[...] = db

        @pl.when(i > 0)
        def _():
            dw_ref[...] += dw
            db_ref[...] += db

    return pl.pallas_call(
        body, name="conv_bwd_pre", grid=(T // tile,),
        in_specs=[pl.BlockSpec((tile, CONV_DIM), lambda i: (i, 0)),
                  pl.BlockSpec((8, CONV_DIM), lambda i: (jnp.maximum(i * hb - 1, 0), 0)),
                  pl.BlockSpec((tile, CONV_DIM), lambda i: (i, 0)),
                  pl.BlockSpec((CONV_K, CONV_DIM), lambda i: (0, 0)),
                  pl.BlockSpec((1, CONV_DIM), lambda i: (0, 0))],
        out_specs=[pl.BlockSpec((tile, CONV_DIM), lambda i: (i, 0)),
                   pl.BlockSpec((CONV_K, CONV_DIM), lambda i: (0, 0)),
                   pl.BlockSpec((1, CONV_DIM), lambda i: (0, 0))],
        out_shape=[jax.ShapeDtypeStruct((T, CONV_DIM), F32),
                   jax.ShapeDtypeStruct((CONV_K, CONV_DIM), F32),
                   jax.ShapeDtypeStruct((1, CONV_DIM), F32)],
        compiler_params=_cp(("arbitrary",)),
    )(x, x, dact, w, b)


def _conv_bwd_x(dc, w, *, geom, tile):
    T, Lp, pad = geom
    nt = T // tile
    hb = tile // 8

    def body(d_ref, h_ref, w_ref, o_ref):
        i = pl.program_id(0)
        halo = jnp.where(i < nt - 1, h_ref[...], 0.0)
        ext = jnp.concatenate([d_ref[...], halo], axis=0)
        n_ext = tile + 8
        acc = ext[0:tile] * w_ref[CONV_K - 1:CONV_K, :]
        for j in range(1, CONV_K):
            acc = acc + pltpu.roll(ext, n_ext - j, 0)[0:tile] * w_ref[CONV_K - 1 - j:CONV_K - j, :]
        o_ref[...] = acc.astype(o_ref.dtype)

    return pl.pallas_call(
        body, name="conv_bwd_x", grid=(nt,),
        in_specs=[pl.BlockSpec((tile, CONV_DIM), lambda i: (i, 0)),
                  pl.BlockSpec((8, CONV_DIM), lambda i: (jnp.minimum((i + 1) * hb, nt * hb - 1), 0)),
                  pl.BlockSpec((CONV_K, CONV_DIM), lambda i: (0, 0))],
        out_specs=pl.BlockSpec((tile, CONV_DIM), lambda i: (i, 0)),
        out_shape=jax.ShapeDtypeStruct((T, CONV_DIM), BF16),
        compiler_params=_cp(("arbitrary",)),
    )(dc, dc, w)


def _ssd_common(sm_ref, alog_ref):
    lane = lax.broadcasted_iota(jnp.int32, (1, CH), 1)
    A = jnp.where(lane < SSD_H, -jnp.exp(alog_ref[...]), 0.0)
    dt = sm_ref[...]
    adt = dt * A
    r = lax.broadcasted_iota(jnp.int32, (CH, CH), 0)
    c = lax.broadcasted_iota(jnp.int32, (CH, CH), 1)
    tril = (r >= c).astype(F32)
    cs = jnp.dot(tril, adt, precision=HI, preferred_element_type=F32)
    csT = cs.T
    cs_last = jnp.sum(jnp.where(r == CH - 1, cs, 0.0), axis=0, keepdims=True)
    return A, dt, cs, csT, cs_last, tril, r, c


def _ssd_lanes():
    r = lax.broadcasted_iota(jnp.int32, (CH, D), 0)
    c = lax.broadcasted_iota(jnp.int32, (CH, D), 1)
    return (c // SSD_P == r).astype(F32)


def _ssd_per_lane(dt, cs):
    ex = _ssd_lanes()
    dt_rep = jnp.dot(dt, ex, precision=HI, preferred_element_type=F32)
    cs_rep = jnp.dot(cs, ex, precision=HI, preferred_element_type=F32)
    r = lax.broadcasted_iota(jnp.int32, (CH, D), 0)
    last_rep = jnp.sum(jnp.where(r == CH - 1, cs_rep, 0.0), axis=0, keepdims=True)
    return dt_rep, cs_rep, last_rep


def _ssd_per_head(*per_lane):
    ex = _ssd_lanes()
    return [lax.dot_general(v, ex, (((1,), (1,)), ((), ())), precision=HI, preferred_element_type=F32)
            for v in per_lane]


def _nt(a, b):
    return lax.dot_general(a, b, (((1,), (1,)), ((), ())), preferred_element_type=F32)


def _tn(a, b):
    return lax.dot_general(a, b, (((0,), (0,)), ((), ())), preferred_element_type=F32)


def _nn(a, b):
    return jnp.dot(a, b, preferred_element_type=F32)


def _ssd_fwd(xbc, sm, alog, *, B, NC):
    T = B * NC * CH

    def body(x_ref, sm_ref, alog_ref, y_ref, st_ref, S):
        cidx = pl.program_id(1)

        @pl.when(cidx == 0)
        def _():
            S[...] = jnp.zeros_like(S)

        st_ref[0] = S[...]
        A, dt, cs, csT, cs_last, tril, _, _ = _ssd_common(sm_ref, alog_ref)
        dt_rep, cs_rep, last_rep = _ssd_per_lane(dt, cs)
        xdt = x_ref[:, 0:D] * dt_rep
        xdec = (xdt * jnp.exp(last_rep - cs_rep)).astype(BF16)
        e_rep = jnp.exp(cs_rep)
        HG = SSD_H // SSD_G
        for g in range(SSD_G):
            gl = slice(g * HG * SSD_P, (g + 1) * HG * SSD_P)
            Bb = x_ref[:, D + g * SSD_N:D + (g + 1) * SSD_N].astype(BF16)
            Cb = x_ref[:, D + SSD_G * SSD_N + g * SSD_N:D + SSD_G * SSD_N + (g + 1) * SSD_N].astype(BF16)
            G = _nt(Cb, Bb)
            STg = S[g * HG:(g + 1) * HG].reshape(HG * SSD_P, SSD_N)
            y_off = e_rep[:, gl] * _nt(Cb, STg.astype(BF16))
            upd = _tn(xdec[:, gl], Bb)
            for rr in range(HG):
                h = g * HG + rr
                hl = slice(h * SSD_P, (h + 1) * SSD_P)
                col = cs[:, h:h + 1]
                row = csT[h:h + 1, :]
                Ld = jnp.where(tril > 0, jnp.exp(jnp.minimum(col - row, 0.0)), 0.0)
                M = (G * Ld).astype(BF16)
                y_ref[:, hl] = _nn(M, xdt[:, hl].astype(BF16)) + y_off[:, rr * SSD_P:(rr + 1) * SSD_P]
                rows = slice(rr * SSD_P, (rr + 1) * SSD_P)
                S[h] = jnp.exp(cs_last[:, h:h + 1]) * STg[rows] + upd[rows]

    return pl.pallas_call(
        body, name="ssd_fwd", grid=(B, NC),
        in_specs=[pl.BlockSpec((CH, CONV_DIM), lambda b, c: (b * NC + c, 0)),
                  pl.BlockSpec((CH, CH), lambda b, c: (b * NC + c, 0)),
                  pl.BlockSpec((1, CH), lambda b, c: (0, 0))],
        out_specs=[pl.BlockSpec((CH, D), lambda b, c: (b * NC + c, 0)),
                   pl.BlockSpec((1, SSD_H, SSD_P, SSD_N), lambda b, c: (b * NC + c, 0, 0, 0))],
        out_shape=[jax.ShapeDtypeStruct((T, D), F32),
                   jax.ShapeDtypeStruct((B * NC, SSD_H, SSD_P, SSD_N), F32)],
        scratch_shapes=[pltpu.VMEM((SSD_H, SSD_P, SSD_N), F32)],
        compiler_params=_cp(("arbitrary", "arbitrary")),
    )(xbc, sm, alog)


def _ssd_bwd(xbc, sm, alog, states, dy, dxs_skip, *, B, NC):
    T = B * NC * CH

    def rix(b, c):
        return b * NC + (NC - 1 - c)

    def body(x_ref, sm_ref, alog_ref, st_ref, dy_ref, sk_ref, dx_ref, ddt_ref, dal_ref, dS):
        bidx = pl.program_id(0)
        cidx = pl.program_id(1)

        @pl.when(cidx == 0)
        def _():
            dS[...] = jnp.zeros_like(dS)

        A, dt, cs, csT, cs_last, tril, r, c = _ssd_common(sm_ref, alog_ref)
        lane = lax.broadcasted_iota(jnp.int32, (1, CH), 1)
        dt_rep, cs_rep, last_rep = _ssd_per_lane(dt, cs)
        xs = x_ref[:, 0:D]
        xdt = xs * dt_rep
        e_rep = jnp.exp(cs_rep)
        dec_rep = jnp.exp(last_rep - cs_rep)
        dye = dy_ref[...] * e_rep
        xdec = xdt * dec_rep
        HG = SSD_H // SSD_G
        DCcol = jnp.zeros((CH, CH), F32)
        DCrow = jnp.zeros((CH, CH), F32)
        dlast = jnp.zeros((1, CH), F32)
        t_off, t_dec, dx_state = [], [], []
        for g in range(SSD_G):
            ob = D + g * SSD_N
            oc = D + SSD_G * SSD_N + g * SSD_N
            gl = slice(g * HG * SSD_P, (g + 1) * HG * SSD_P)
            Bb = x_ref[:, ob:ob + SSD_N].astype(BF16)
            Cb = x_ref[:, oc:oc + SSD_N].astype(BF16)
            G = _nt(Cb, Bb)
            STg = st_ref[0, g * HG:(g + 1) * HG].reshape(HG * SSD_P, SSD_N)
            dSTg = dS[g * HG:(g + 1) * HG].reshape(HG * SSD_P, SSD_N)
            STb = STg.astype(BF16)
            dSTb = dSTg.astype(BF16)
            dyeb = dye[:, gl].astype(BF16)
            t_off.append(dye[:, gl] * _nt(Cb, STb))
            dCg = _nn(dyeb, STb)
            dS_in = _tn(dyeb, Cb)
            Z = _nt(Bb, dSTb)
            dx_state.append(dec_rep[:, gl] * Z)
            t_dec.append(xdec[:, gl] * Z)
            dBg = _nn(xdec[:, gl].astype(BF16), dSTb)
            dG = jnp.zeros((CH, CH), F32)
            for rr in range(HG):
                h = g * HG + rr
                hl = slice(h * SSD_P, (h + 1) * SSD_P)
                rows = slice(rr * SSD_P, (rr + 1) * SSD_P)
                col = cs[:, h:h + 1]
                row = csT[h:h + 1, :]
                Ld = jnp.where(tril > 0, jnp.exp(jnp.minimum(col - row, 0.0)), 0.0)
                Mf = G * Ld
                dyb = dy_ref[:, hl].astype(BF16)
                dx_ref[:, hl] = _tn(Mf.astype(BF16), dyb)
                dM = _nt(dyb, xdt[:, hl].astype(BF16))
                dG = dG + dM * Ld
                W = dM * Mf
                DCcol = DCcol + jnp.where(c == h, jnp.sum(W, axis=1, keepdims=True), 0.0)
                DCrow = DCrow - jnp.where(r == h, jnp.sum(W, axis=0, keepdims=True), 0.0)
                el = jnp.exp(cs_last[:, h:h + 1])
                dl = el * jnp.sum(jnp.sum(dSTg[rows] * STg[rows], axis=1, keepdims=True), axis=0, keepdims=True)
                dlast = dlast + jnp.where(lane == h, dl, 0.0)
                dS[h] = dS_in[rows] + el * dSTg[rows]
            dGb = dG.astype(BF16)
            dx_ref[:, ob:ob + SSD_N] = dBg + _tn(dGb, Cb)
            dx_ref[:, oc:oc + SSD_N] = dCg + _nn(dGb, Bb)
        dxdt = dx_ref[:, 0:D] + jnp.concatenate(dx_state, axis=1)
        dx_ref[:, 0:D] = dxdt * dt_rep + sk_ref[...]
        s_off, s_dec, DX = _ssd_per_head(jnp.concatenate(t_off, axis=1), jnp.concatenate(t_dec, axis=1), dxdt * xs)
        dlast = dlast + jnp.sum(s_dec, axis=0, keepdims=True)
        DC = DCcol + s_off - s_dec + DCrow.T + jnp.where(r == CH - 1, dlast, 0.0)
        triu = (r <= c).astype(F32)
        dadt = jnp.dot(triu, DC, precision=HI, preferred_element_type=F32)
        ddt_ref[...] = dadt * A + DX
        dal = jnp.sum(dadt * dt, axis=0, keepdims=True) * A

        @pl.when((bidx == 0) & (cidx == 0))
        def _():
            dal_ref[...] = dal

        @pl.when((bidx > 0) | (cidx > 0))
        def _():
            dal_ref[...] += dal

    return pl.pallas_call(
        body, name="ssd_bwd", grid=(B, NC),
        in_specs=[pl.BlockSpec((CH, CONV_DIM), lambda b, c: (rix(b, c), 0)),
                  pl.BlockSpec((CH, CH), lambda b, c: (rix(b, c), 0)),
                  pl.BlockSpec((1, CH), lambda b, c: (0, 0)),
                  pl.BlockSpec((1, SSD_H, SSD_P, SSD_N), lambda b, c: (rix(b, c), 0, 0, 0)),
                  pl.BlockSpec((CH, D), lambda b, c: (rix(b, c), 0)),
                  pl.BlockSpec((CH, D), lambda b, c: (rix(b, c), 0))],
        out_specs=[pl.BlockSpec((CH, CONV_DIM), lambda b, c: (rix(b, c), 0)),
                   pl.BlockSpec((CH, CH), lambda b, c: (rix(b, c), 0)),
                   pl.BlockSpec((1, CH), lambda b, c: (0, 0))],
        out_shape=[jax.ShapeDtypeStruct((T, CONV_DIM), F32),
                   jax.ShapeDtypeStruct((T, CH), F32),
                   jax.ShapeDtypeStruct((1, CH), F32)],
        scratch_shapes=[pltpu.VMEM((SSD_H, SSD_P, SSD_N), F32)],
        compiler_params=_cp(("arbitrary", "arbitrary")),
    )(xbc, sm, alog, states, dy, dxs_skip)


def _cumsum_seq(v, *, B, NC, reverse, name):
    T = B * NC * CH

    def ix(b, c):
        return b * NC + ((NC - 1 - c) if reverse else c)

    def body(v_ref, o_ref, carry):
        cidx = pl.program_id(1)

        @pl.when(cidx == 0)
        def _():
            carry[...] = jnp.zeros_like(carry)

        r = lax.broadcasted_iota(jnp.int32, (CH, CH), 0)
        c = lax.broadcasted_iota(jnp.int32, (CH, CH), 1)
        tri = ((r <= c) if reverse else (r >= c)).astype(F32)
        cs = jnp.dot(tri, v_ref[...], precision=HI, preferred_element_type=F32) + carry[...]
        o_ref[...] = cs
        edge = 0 if reverse else CH - 1
        carry[...] = jnp.sum(jnp.where(r == edge, cs, 0.0), axis=0, keepdims=True)

    return pl.pallas_call(
        body, name=name, grid=(B, NC),
        in_specs=[pl.BlockSpec((CH, CH), lambda b, c: (ix(b, c), 0))],
        out_specs=pl.BlockSpec((CH, CH), lambda b, c: (ix(b, c), 0)),
        out_shape=jax.ShapeDtypeStruct((T, CH), F32),
        scratch_shapes=[pltpu.VMEM((1, CH), F32)],
        compiler_params=_cp(("arbitrary", "arbitrary")),
    )(v)


def _fox_tb(Lp):
    return 384 if (Lp % 384 == 0 and Lp > 384) else CH


def _fox_keybias(cum, *, B, Lp, pad):
    ck = cum.reshape(B, Lp, CH)[:, :, SSD_H:SSD_H + FOX_H].transpose(0, 2, 1)
    pos = lax.broadcasted_iota(jnp.int32, ck.shape, 2)
    return jnp.where(pos < pad, -NEG, ck).reshape(B * FOX_H, 1, Lp)


def _fox_tril(TB):
    r = lax.broadcasted_iota(jnp.int32, (TB, TB), 0)
    c = lax.broadcasted_iota(jnp.int32, (TB, TB), 1)
    return r >= c


def _fox_fwd(qkv, cumT, *, B, Lp):
    TB = _fox_tb(Lp)
    NQ = Lp // TB
    T = B * Lp
    scale = FOX_DH ** -0.5

    def body(q_ref, k_ref, v_ref, ct_ref, o_ref, lse_ref):
        i = pl.program_id(2)
        q = q_ref[...]

        def block(j, nb, carry, diag):
            m, l, acc = carry
            off = pl.multiple_of(j * TB, TB)
            k = k_ref[pl.ds(off, nb * TB), :]
            v = v_ref[pl.ds(off, nb * TB), :]
            s = _nt(q, k) * scale - ct_ref[0, :, pl.ds(off, nb * TB)]
            if diag:
                s = jnp.where(_fox_tril(TB), s, NEG)
            m_new = jnp.maximum(m, jnp.max(s, axis=1, keepdims=True))
            p = jnp.exp(s - m_new)
            alpha = jnp.exp(m - m_new)
            l = alpha * l + jnp.sum(p, axis=1, keepdims=True)
            acc = alpha * acc + _nn(p.astype(BF16), v)
            return m_new, l, acc

        init = (jnp.full((TB, 1), NEG, F32), jnp.zeros((TB, 1), F32), jnp.zeros((TB, FOX_DH), F32))
        carry = lax.fori_loop(0, i // 2, lambda t, c: block(2 * t, 2, c, False), init)
        carry = lax.fori_loop(2 * (i // 2), i, lambda j, c: block(j, 1, c, False), carry)
        m, l, acc = block(i, 1, carry, True)
        o_ref[...] = (acc / l).astype(o_ref.dtype)
        lse_ref[0, 0] = m + jnp.log(l)

    return pl.pallas_call(
        body, name="fox_fwd", grid=(B, FOX_H, NQ),
        in_specs=[pl.BlockSpec((TB, FOX_DH), lambda b, h, i: (b * NQ + i, h)),
                  pl.BlockSpec((Lp, FOX_DH), lambda b, h, i: (b, FOX_H + h)),
                  pl.BlockSpec((Lp, FOX_DH), lambda b, h, i: (b, 2 * FOX_H + h)),
                  pl.BlockSpec((1, 1, Lp), lambda b, h, i: (b * FOX_H + h, 0, 0))],
        out_specs=[pl.BlockSpec((TB, FOX_DH), lambda b, h, i: (b * NQ + i, h)),
                   pl.BlockSpec((1, 1, TB, 1), lambda b, h, i: (b, h, i, 0))],
        out_shape=[jax.ShapeDtypeStruct((T, D), BF16),
                   jax.ShapeDtypeStruct((B, FOX_H, Lp, 1), F32)],
        compiler_params=_cp(("arbitrary", "arbitrary", "arbitrary")),
    )(qkv, qkv, qkv, cumT)


def _fox_bwd(qkv, dy, o, lse, cumT, *, B, Lp):
    TB = _fox_tb(Lp)
    NQ = Lp // TB
    T = B * Lp
    scale = FOX_DH ** -0.5

    def body(q_ref, k_ref, v_ref, dy_ref, o_ref, lse_ref, ct_ref, dq_ref, dk_ref, dv_ref, dck_ref, dcq_ref, dl_s):
        j = pl.program_id(2)
        k = k_ref[...]
        v = v_ref[...]
        ck = ct_ref[0]

        @pl.when(j == 0)
        def _():
            dq_ref[...] = jnp.zeros_like(dq_ref)
            dcq_ref[...] = jnp.zeros_like(dcq_ref)
            for i in range(NQ):
                sl = slice(i * TB, (i + 1) * TB)
                dl_s[sl, :] = jnp.sum(dy_ref[sl, :].astype(F32) * o_ref[sl, :].astype(F32), axis=1, keepdims=True)

        def block(i, nb, carry, diag):
            dk, dv, dck = carry
            off = pl.multiple_of(i * TB, TB)
            rows = pl.ds(off, nb * TB)
            q = q_ref[rows, :]
            dob = dy_ref[rows, :].astype(BF16)
            e = _nt(q, k) * scale - ck - lse_ref[0, 0, rows, :]
            if diag:
                e = jnp.where(_fox_tril(TB), e, NEG)
            p = jnp.exp(e)
            dv = dv + _tn(p.astype(BF16), dob)
            ds = p * (_nt(dob, v) - dl_s[rows, :])
            dsb = ds.astype(BF16)
            dk = dk + _tn(dsb, q)
            dq_ref[rows, :] += _nn(dsb, k) * scale
            dcq_ref[0, 0, rows, :] += jnp.sum(ds, axis=1, keepdims=True)
            dck = dck - jnp.sum(ds, axis=0, keepdims=True)
            return dk, dv, dck

        z = jnp.zeros((TB, FOX_DH), F32)
        carry = block(j, 1, (z, z, jnp.zeros((1, TB), F32)), True)
        npair = (NQ - 1 - j) // 2
        carry = lax.fori_loop(0, npair, lambda t, c: block(j + 1 + 2 * t, 2, c, False), carry)
        dk, dv, dck = lax.fori_loop(j + 1 + 2 * npair, NQ, lambda i, c: block(i, 1, c, False), carry)
        dk_ref[...] = (dk * scale).astype(dk_ref.dtype)
        dv_ref[...] = dv.astype(dv_ref.dtype)
        dck_ref[0] = dck

    head = lambda b, h, j: (b, h)
    return pl.pallas_call(
        body, name="fox_bwd", grid=(B, FOX_H, NQ),
        in_specs=[pl.BlockSpec((Lp, FOX_DH), head),
                  pl.BlockSpec((TB, FOX_DH), lambda b, h, j: (b * NQ + j, FOX_H + h)),
                  pl.BlockSpec((TB, FOX_DH), lambda b, h, j: (b * NQ + j, 2 * FOX_H + h)),
                  pl.BlockSpec((Lp, FOX_DH), head),
                  pl.BlockSpec((Lp, FOX_DH), head),
                  pl.BlockSpec((1, 1, Lp, 1), lambda b, h, j: (b, h, 0, 0)),
                  pl.BlockSpec((1, 1, TB), lambda b, h, j: (b * FOX_H + h, 0, j))],
        out_specs=[pl.BlockSpec((Lp, FOX_DH), head),
                   pl.BlockSpec((TB, FOX_DH), lambda b, h, j: (b * NQ + j, h)),
                   pl.BlockSpec((TB, FOX_DH), lambda b, h, j: (b * NQ + j, h)),
                   pl.BlockSpec((1, 1, TB), lambda b, h, j: (b * FOX_H + h, 0, j)),
                   pl.BlockSpec((1, 1, Lp, 1), lambda b, h, j: (b, h, 0, 0))],
        out_shape=[jax.ShapeDtypeStruct((T, D), F32),
                   jax.ShapeDtypeStruct((T, D), BF16),
                   jax.ShapeDtypeStruct((T, D), BF16),
                   jax.ShapeDtypeStruct((B * FOX_H, 1, Lp), F32),
                   jax.ShapeDtypeStruct((B, FOX_H, Lp, 1), F32)],
        scratch_shapes=[pltpu.VMEM((Lp, 1), F32)],
        compiler_params=_cp(("arbitrary", "arbitrary", "arbitrary")),
    )(qkv, qkv, qkv, dy, o, lse, cumT)


S5_TILE = 8


def _s5_pows(lam_ref, pw, tab, reverse):
    lr = lam_ref[0, :, 0:S5_W]
    li = lam_ref[0, :, S5_W:2 * S5_W]
    if reverse:
        li = -li
    ar, ai = lr, li
    sub = lax.broadcasted_iota(jnp.int32, (S5_TILE, 1), 0)
    for k, s in enumerate((1, 2, 4)):
        keep = (sub < S5_TILE - s) if reverse else (sub >= s)
        pw[k * S5_TILE:(k + 1) * S5_TILE, 0:S5_W] = jnp.where(keep, ar, 0.0)
        pw[k * S5_TILE:(k + 1) * S5_TILE, S5_W:2 * S5_W] = jnp.where(keep, ai, 0.0)
        ar, ai = ar * ar - ai * ai, 2.0 * ar * ai
    ar, ai = lr, li
    for r in range(S5_TILE):
        row = (S5_TILE - 1 - r) if reverse else r
        tab[row:row + 1, 0:S5_W] = ar
        tab[row:row + 1, S5_W:2 * S5_W] = ai
        ar, ai = ar * lr - ai * li, ar * li + ai * lr


def _s5_scan(hs, pw, tab, carry, reverse):
    n = hs.shape[0]
    tr = tab[:, 0:S5_W]
    ti = tab[:, S5_W:2 * S5_W]
    cr = carry[:, 0:S5_W]
    ci = carry[:, S5_W:2 * S5_W]
    order = range(n // S5_TILE)
    for t in (reversed(order) if reverse else order):
        lo = t * S5_TILE
        vr = hs[lo:lo + S5_TILE, 0:S5_W]
        vi = hs[lo:lo + S5_TILE, S5_W:2 * S5_W]
        for k, s in enumerate((1, 2, 4)):
            sh = (S5_TILE - s) if reverse else s
            sr = pltpu.roll(vr, sh, 0)
            si = pltpu.roll(vi, sh, 0)
            ar = pw[k * S5_TILE:(k + 1) * S5_TILE, 0:S5_W]
            ai = pw[k * S5_TILE:(k + 1) * S5_TILE, S5_W:2 * S5_W]
            vr, vi = vr + ar * sr - ai * si, vi + ar * si + ai * sr
        hs[lo:lo + S5_TILE, 0:S5_W] = vr + tr * cr - ti * ci
        hs[lo:lo + S5_TILE, S5_W:2 * S5_W] = vi + tr * ci + ti * cr
        edge = lo if reverse else lo + S5_TILE - 1
        cr = hs[edge:edge + 1, 0:S5_W]
        ci = hs[edge:edge + 1, S5_W:2 * S5_W]
    carry[:, 0:S5_W] = cr
    carry[:, S5_W:2 * S5_W] = ci


def _s5_fwd(u, Bsg, Csg, lam, *, B, NC):
    T = B * NC * CH

    def body(u_ref, b_ref, c_ref, lam_ref, y_ref, h_ref, pw, tab, hs, carry):
        cidx = pl.program_id(2)

        @pl.when(cidx == 0)
        def _():
            _s5_pows(lam_ref, pw, tab, False)
            carry[...] = jnp.zeros_like(carry)

        hs[...] = _nn(u_ref[...].astype(BF16), b_ref[0])
        _s5_scan(hs, pw, tab, carry, False)
        hb = hs[...].astype(BF16)
        h_ref[...] = hb
        y_ref[...] = _nn(hb, c_ref[0])

    return pl.pallas_call(
        body, name="s5_fwd", grid=(B, S5_G // S5_SG, NC),
        in_specs=[pl.BlockSpec((CH, CH), lambda b, s, c: (b * NC + c, s)),
                  pl.BlockSpec((1, CH, 2 * S5_W), lambda b, s, c: (s, 0, 0)),
                  pl.BlockSpec((1, 2 * S5_W, CH), lambda b, s, c: (s, 0, 0)),
                  pl.BlockSpec((1, 1, 2 * S5_W), lambda b, s, c: (s, 0, 0))],
        out_specs=[pl.BlockSpec((CH, CH), lambda b, s, c: (b * NC + c, s)),
                   pl.BlockSpec((CH, 2 * S5_W), lambda b, s, c: (b * NC + c, s))],
        out_shape=[jax.ShapeDtypeStruct((T, D), F32),
                   jax.ShapeDtypeStruct((T, (S5_G // S5_SG) * 2 * S5_W), BF16)],
        scratch_shapes=[pltpu.VMEM((3 * S5_TILE, 2 * S5_W), F32), pltpu.VMEM((S5_TILE, 2 * S5_W), F32),
                        pltpu.VMEM((CH, 2 * S5_W), F32), pltpu.VMEM((1, 2 * S5_W), F32)],
        compiler_params=_cp(("arbitrary", "arbitrary", "arbitrary")),
    )(u, Bsg, Csg, lam)


def _s5_bwd(u, hst, dy, du_skip, Bsg, Csg, lam, *, B, NC):
    T = B * NC * CH
    NS = S5_G // S5_SG
    hb16 = CH // 16

    def rix(b, c):
        return b * NC + (NC - 1 - c)

    def body(u_ref, h_ref, hp_ref, dy_ref, sk_ref, b_ref, c_ref, lam_ref,
             du_ref, db_ref, dc_ref, dl_ref, pw, tab, gs, carry):
        bidx = pl.program_id(1)
        cidx = pl.program_id(2)
        first = (bidx == 0) & (cidx == 0)

        @pl.when(cidx == 0)
        def _():
            _s5_pows(lam_ref, pw, tab, True)
            carry[...] = jnp.zeros_like(carry)

        dyb = dy_ref[...].astype(BF16)
        gs[...] = _nt(dyb, c_ref[0])
        _s5_scan(gs, pw, tab, carry, True)
        gr = gs[:, 0:S5_W]
        gi = gs[:, S5_W:]
        gb = gs[...].astype(BF16)
        du_ref[...] = (_nt(gb, b_ref[0]) + sk_ref[...]).astype(du_ref.dtype)
        ub = u_ref[...].astype(BF16)
        hcur = h_ref[...]
        dB = _tn(ub, gb)
        dC = _tn(hcur, dyb)
        hf = hcur.astype(F32)
        row = lax.broadcasted_iota(jnp.int32, (CH, 1), 0)
        prev_last = jnp.where(cidx < NC - 1, hp_ref[15:16, :].astype(F32), 0.0)
        hprev = jnp.where(row == 0, prev_last, pltpu.roll(hf, 1, 0))
        pr = hprev[:, 0:S5_W]
        pi = hprev[:, S5_W:]
        da = _colsum(gr * pr + gi * pi)
        dbb = _colsum(gi * pr - gr * pi)
        dl = jnp.concatenate([da, dbb], axis=1)

        @pl.when(first)
        def _():
            db_ref[0] = dB
            dc_ref[0] = dC
            dl_ref[0] = dl

        @pl.when(jnp.logical_not(first))
        def _():
            db_ref[0] += dB
            dc_ref[0] += dC
            dl_ref[0] += dl

    return pl.pallas_call(
        body, name="s5_bwd", grid=(NS, B, NC),
        in_specs=[pl.BlockSpec((CH, CH), lambda s, b, c: (rix(b, c), s)),
                  pl.BlockSpec((CH, 2 * S5_W), lambda s, b, c: (rix(b, c), s)),
                  pl.BlockSpec((16, 2 * S5_W), lambda s, b, c: (jnp.maximum(rix(b, c) * hb16 - 1, 0), s)),
                  pl.BlockSpec((CH, CH), lambda s, b, c: (rix(b, c), s)),
                  pl.BlockSpec((CH, CH), lambda s, b, c: (rix(b, c), s)),
                  pl.BlockSpec((1, CH, 2 * S5_W), lambda s, b, c: (s, 0, 0)),
                  pl.BlockSpec((1, 2 * S5_W, CH), lambda s, b, c: (s, 0, 0)),
                  pl.BlockSpec((1, 1, 2 * S5_W), lambda s, b, c: (s, 0, 0))],
        out_specs=[pl.BlockSpec((CH, CH), lambda s, b, c: (rix(b, c), s)),
                   pl.BlockSpec((1, CH, 2 * S5_W), lambda s, b, c: (s, 0, 0)),
                   pl.BlockSpec((1, 2 * S5_W, CH), lambda s, b, c: (s, 0, 0)),
                   pl.BlockSpec((1, 1, 2 * S5_W), lambda s, b, c: (s, 0, 0))],
        out_shape=[jax.ShapeDtypeStruct((T, D), BF16),
                   jax.ShapeDtypeStruct((NS, CH, 2 * S5_W), F32),
                   jax.ShapeDtypeStruct((NS, 2 * S5_W, CH), F32),
                   jax.ShapeDtypeStruct((NS, 1, 2 * S5_W), F32)],
        scratch_shapes=[pltpu.VMEM((3 * S5_TILE, 2 * S5_W), F32), pltpu.VMEM((S5_TILE, 2 * S5_W), F32),
                        pltpu.VMEM((CH, 2 * S5_W), F32), pltpu.VMEM((1, 2 * S5_W), F32)],
        compiler_params=_cp(("arbitrary", "arbitrary", "arbitrary")),
    )(u, hst, hst, dy, du_skip, Bsg, Csg, lam)


def _s5_param_fn(lre, lim, lstep, bre, bim):
    step = jnp.exp(lstep)
    zr = lre * step
    zi = lim * step
    e = jnp.exp(zr)
    a = e * jnp.cos(zi)
    b = e * jnp.sin(zi)
    den = lre * lre + lim * lim
    qr = ((a - 1.0) * lre + b * lim) / den
    qi = (b * lre - (a - 1.0) * lim) / den
    return a, b, qr[None] * bre - qi[None] * bim, qr[None] * bim + qi[None] * bre


_S5_ROWS = S5_G * S5_P // CH


def _s5_tile(v):
    return v.reshape(_S5_ROWS, CH)


def _s5_tile_b(v):
    return v.reshape(S5_G * S5_P, S5_C).T.reshape(S5_C, _S5_ROWS, CH)


def _s5_untile_b(v):
    return v.reshape(S5_C, S5_G * S5_P).T.reshape(S5_G, S5_P, S5_C)


def _s5_params(lre, lim, lstep, bre, bim):
    def body(a_ref, b_ref, c_ref, d_ref, e_ref, o1, o2, o3, o4):
        outs = _s5_param_fn(a_ref[...], b_ref[...], c_ref[...], d_ref[...], e_ref[...])
        for o, v in zip((o1, o2, o3, o4), outs):
            o[...] = v

    shp = [jax.ShapeDtypeStruct(lre.shape, F32)] * 2 + [jax.ShapeDtypeStruct(bre.shape, F32)] * 2
    return pl.pallas_call(body, name="s5_params", out_shape=shp, compiler_params=_cp())(lre, lim, lstep, bre, bim)


def _s5_params_bwd(lre, lim, lstep, bre, bim, da, db, dbr, dbi):
    def body(a_ref, b_ref, c_ref, d_ref, e_ref, g1, g2, g3, g4, o1, o2, o3, o4, o5):
        _, vjp = jax.vjp(_s5_param_fn, a_ref[...], b_ref[...], c_ref[...], d_ref[...], e_ref[...])
        outs = vjp((g1[...], g2[...], g3[...], g4[...]))
        for o, v in zip((o1, o2, o3, o4, o5), outs):
            o[...] = v

    shp = [jax.ShapeDtypeStruct(lre.shape, F32)] * 3 + [jax.ShapeDtypeStruct(bre.shape, F32)] * 2
    return pl.pallas_call(body, name="s5_params_bwd", out_shape=shp, compiler_params=_cp())(
        lre, lim, lstep, bre, bim, da, db, dbr, dbi)


def _s5_blockdiag(br, bi, cre, cim):
    NS = S5_G // S5_SG
    eye = jnp.eye(S5_SG, dtype=F32)

    def bmat(v):
        v = v.reshape(NS, S5_SG, S5_P, S5_C)
        m = jnp.einsum("sgpc,gh->sgchp", v, eye)
        return m.reshape(NS, S5_SG * S5_C, S5_SG * S5_P)

    def cmat(v):
        v = v.reshape(NS, S5_SG, S5_C, S5_P)
        m = jnp.einsum("sgcp,gh->sgphc", v, eye)
        return m.reshape(NS, S5_SG * S5_P, S5_SG * S5_C)

    Bsg = jnp.concatenate([bmat(br), bmat(bi)], axis=2).astype(BF16)
    Csg = jnp.concatenate([cmat(cre), cmat(-cim)], axis=1).astype(BF16)
    return Bsg, Csg


def _s5_unblock(dBsg, dCsg):
    NS = S5_G // S5_SG

    def ub(m):
        m = m.reshape(NS, S5_SG, S5_C, S5_SG, S5_P)
        d = jnp.stack([m[:, g, :, g, :] for g in range(S5_SG)], axis=1)
        return d.transpose(0, 1, 3, 2).reshape(S5_G, S5_P, S5_C)

    def uc(m):
        m = m.reshape(NS, S5_SG, S5_P, S5_SG, S5_C)
        d = jnp.stack([m[:, g, :, g, :] for g in range(S5_SG)], axis=1)
        return d.transpose(0, 1, 3, 2).reshape(S5_G, S5_C, S5_P)

    dbr = ub(dBsg[:, :, 0:S5_W])
    dbi = ub(dBsg[:, :, S5_W:])
    dcr = uc(dCsg[:, 0:S5_W, :])
    dci = -uc(dCsg[:, S5_W:, :])
    return dbr, dbi, dcr, dci


def _loss_head(x, nf, target, *, B, NC, S):
    T = B * NC * CH
    nts = S // CH

    def f(xv, w, t):
        y = _rms(xv, w)
        return 0.5 * _colsum(jnp.mean(jnp.square(y - t), axis=-1, keepdims=True))

    def body(x_ref, w_ref, t_ref, dx_ref, ls_ref, dw_ref):
        i = pl.program_id(0)
        on = (i % NC) > 0
        t = t_ref[...]
        l, vjp = jax.vjp(lambda a, b: f(a, b, t), x_ref[...], w_ref[...])
        dx, dw = vjp(jnp.ones((1, 1), F32))
        g = jnp.where(on, 1.0, 0.0)
        dx_ref[...] = dx * g
        lv = jnp.zeros((1, CH), F32) + l * g

        @pl.when(i == 0)
        def _():
            ls_ref[...] = lv
            dw_ref[...] = dw * g

        @pl.when(i > 0)
        def _():
            ls_ref[...] += lv
            dw_ref[...] += dw * g

    def tix(i):
        return ((i // NC) * nts + jnp.maximum(i % NC - 1, 0), 0)

    return pl.pallas_call(
        body, name="loss_head", grid=(B * NC,),
        in_specs=[pl.BlockSpec((CH, D), lambda i: (i, 0)),
                  pl.BlockSpec((1, D), lambda i: (0, 0)),
                  pl.BlockSpec((CH, D), tix)],
        out_specs=[pl.BlockSpec((CH, D), lambda i: (i, 0)),
                   pl.BlockSpec((1, CH), lambda i: (0, 0)),
                   pl.BlockSpec((1, D), lambda i: (0, 0))],
        out_shape=[jax.ShapeDtypeStruct((T, D), F32),
                   jax.ShapeDtypeStruct((1, CH), F32),
                   jax.ShapeDtypeStruct((1, D), F32)],
        compiler_params=_cp(("arbitrary",)),
    )(x, nf, target)


def _ew(fn, ins, n_out, out_dtypes, *, name, tile=None):
    R, C = ins[0].shape
    tile = tile or _pick(R, (512, 256, 128, 64, 32, 16, 8, 1))
    if tile % 8 != 0:
        tile = R

    def body(*refs):
        outs = fn(*[r[...] for r in refs[:len(ins)]])
        if not isinstance(outs, (tuple, list)):
            outs = (outs,)
        for r, v in zip(refs[len(ins):], outs):
            r[...] = v.astype(r.dtype)

    spec = pl.BlockSpec((tile, C), lambda i: (i, 0))
    res = pl.pallas_call(
        body, name=name, grid=(R // tile,), in_specs=[spec] * len(ins), out_specs=[spec] * n_out,
        out_shape=[jax.ShapeDtypeStruct((R, C), dt) for dt in out_dtypes],
        compiler_params=_cp(("parallel",)),
    )(*ins)
    return res


def _adam_fn(w, g, m, v):
    m = ADAM_B1 * m + (1.0 - ADAM_B1) * g
    v = ADAM_B2 * v + (1.0 - ADAM_B2) * jnp.square(g)
    m_hat = m / (1.0 - ADAM_B1 ** ADAM_STEP)
    v_hat = v / (1.0 - ADAM_B2 ** ADAM_STEP)
    delta = -ADAM_LR * (m_hat / (jnp.sqrt(v_hat) + ADAM_EPS) + ADAM_WD * w)
    return delta, m, v


def _adam(w, g, m, v, name):
    shp = w.shape
    C = shp[-1]
    f = lambda a: a.reshape(-1, C)
    d, nm, nv = _ew(_adam_fn, [f(w), f(g), f(m), f(v)], 3, [F32] * 3, name=name)
    return d.reshape(shp), nm.reshape(shp), nv.reshape(shp)


def _me():
    return lax.axis_index("x"), lax.axis_index("y"), lax.axis_index("c")


def _all_gather(v, name):
    def body(x_ref, out_ref, send_sems, recv_sems, local_sem):
        x, y, c = _me()
        me, sibling = (x, y, c), (x, y, 1 - c)
        chips = [(1 - x, y), (x, 1 - y), (1 - x, 1 - y)]

        def slot(px, py, pc):
            return out_ref.at[4 * px + 2 * py + pc]

        def copy(k, block, to, src=None):
            return pltpu.make_async_remote_copy(
                src_ref=slot(*block) if src is None else src, dst_ref=slot(*block),
                send_sem=send_sems.at[k], recv_sem=recv_sems.at[k], device_id=to, device_id_type=MESH)

        mine = pltpu.make_async_copy(x_ref, slot(*me), local_sem)
        mine.start()
        first = [copy(0, me, sibling, src=x_ref)]
        first += [copy(1 + j, me, (*chip, c), src=x_ref) for j, chip in enumerate(chips)]
        for cp in first:
            cp.start()
        passed = [copy(4 + j, (*chip, c), sibling) for j, chip in enumerate(chips)]
        for j, chip in enumerate(chips):
            copy(1 + j, (*chip, c), me).wait_recv()
            passed[j].start()
        copy(0, sibling, me).wait_recv()
        for j, chip in enumerate(chips):
            copy(4 + j, (*chip, 1 - c), me).wait_recv()
        for cp in first + passed:
            cp.wait_send()
        mine.wait()

    return pl.pallas_call(
        body, name=name, out_shape=jax.ShapeDtypeStruct((N_DEV,) + v.shape, v.dtype),
        in_specs=[pl.BlockSpec(memory_space=pl.ANY)], out_specs=pl.BlockSpec(memory_space=pl.ANY),
        scratch_shapes=[pltpu.SemaphoreType.DMA((7,)), pltpu.SemaphoreType.DMA((7,)), pltpu.SemaphoreType.DMA],
    )(v)


def _swap_core(g, name):
    def body(g_ref, out_ref, send_sems, recv_sems):
        x, y, c = _me()
        cps = [pltpu.make_async_remote_copy(
            src_ref=g_ref.at[q, 1 - c], dst_ref=out_ref.at[q], send_sem=send_sems.at[q], recv_sem=recv_sems.at[q],
            device_id=(x, y, 1 - c), device_id_type=MESH) for q in range(4)]
        for cp in cps:
            cp.start()
        for cp in cps:
            cp.wait()

    return pl.pallas_call(
        body, name=name, out_shape=jax.ShapeDtypeStruct((4,) + g.shape[2:], g.dtype),
        in_specs=[pl.BlockSpec(memory_space=pl.ANY)], out_specs=pl.BlockSpec(memory_space=pl.ANY),
        scratch_shapes=[pltpu.SemaphoreType.DMA((4,)), pltpu.SemaphoreType.DMA((4,))],
    )(g)


def _swap_chips(hb, name):
    flips = [(1, 0), (0, 1), (1, 1)]

    def body(h_ref, out_ref, send_sems, recv_sems):
        x, y, c = _me()
        cps = []
        for j, (fx, fy) in enumerate(flips):
            px = x + fx - 2 * x * fx
            py = y + fy - 2 * y * fy
            cps.append(pltpu.make_async_remote_copy(
                src_ref=h_ref.at[2 * px + py], dst_ref=out_ref.at[j], send_sem=send_sems.at[j],
                recv_sem=recv_sems.at[j], device_id=(px, py, c), device_id_type=MESH))
        for cp in cps:
            cp.start()
        for cp in cps:
            cp.wait()

    return pl.pallas_call(
        body, name=name, out_shape=jax.ShapeDtypeStruct((3,) + hb.shape[1:], hb.dtype),
        in_specs=[pl.BlockSpec(memory_space=pl.ANY)], out_specs=pl.BlockSpec(memory_space=pl.ANY),
        scratch_shapes=[pltpu.SemaphoreType.DMA((3,)), pltpu.SemaphoreType.DMA((3,))],
    )(hb)


def _reduce_scatter(g8, tag):
    shard = g8.shape[1:]
    C = shard[-1]
    x, y, c = _me()
    g4 = g8.reshape((4, 2) + shard)
    got = _swap_core(g4, "rs_core_" + tag)
    mine = lax.dynamic_index_in_dim(g4, c, axis=1, keepdims=False)
    h, hb = _ew(lambda a, b: (a + b, a + b), [mine.reshape(-1, C), got.reshape(-1, C)], 2, [F32, BF16],
                name="rs_pair_sum_" + tag)
    own = lax.dynamic_index_in_dim(h.reshape((4,) + shard), 2 * x + y, axis=0, keepdims=False)
    got3 = _swap_chips(hb.reshape((4,) + shard), "rs_chips_" + tag)
    out, = _ew(lambda a, b, c_, d: a + b.astype(F32) + c_.astype(F32) + d.astype(F32),
               [own.reshape(-1, C)] + [got3[j].reshape(-1, C) for j in range(3)], 1, [F32],
               name="rs_chip_sum_" + tag)
    return out.reshape(shard)


def _sum8(a):
    out, = _ew(lambda *v: functools.reduce(lambda p, q: p + q, v), [a[k] for k in range(N_DEV)], 1, [F32],
               name="sum8")
    return out


def _pad_rows(flat, cols, mult):
    n = flat.shape[0]
    per = cols * mult
    tot = ((n + per - 1) // per) * per
    return jnp.pad(flat, (0, tot - n)).reshape(-1, cols)


class _Packer:
    def __init__(self, shapes, mult):
        self.shapes = shapes
        self.sizes = [int(np.prod(s)) for s in shapes]
        self.mult = mult

    def pack(self, arrs, dtype):
        flat = jnp.concatenate([a.reshape(-1).astype(dtype) for a in arrs])
        return _pad_rows(flat, D, self.mult)

    def unpack(self, buf):
        flat = buf.reshape(-1)
        out, o = [], 0
        for s, n in zip(self.shapes, self.sizes):
            out.append(flat[o:o + n].reshape(s))
            o += n
        return out


def _w_cat(w_in_l):
    sm = jnp.concatenate([w_in_l[:, O_DT:O_DT + SSD_H], w_in_l[:, O_F:O_F + FOX_H],
                          jnp.zeros((D, CH - SSD_H - FOX_H), w_in_l.dtype)], axis=1)
    return jnp.concatenate([w_in_l[:, O_Z:O_XBC], w_in_l[:, O_XBC:O_DT], w_in_l[:, O_QKV:O_F],
                            w_in_l[:, O_U:O_G], w_in_l[:, O_G:D_IN], sm], axis=1)


def _w_uncat(g):
    return jnp.concatenate([g[:, OFF_Z:OFF_XBC], g[:, OFF_XBC:OFF_QKV], g[:, OFF_SM:OFF_SM + SSD_H],
                            g[:, OFF_QKV:OFF_U], g[:, OFF_SM + SSD_H:OFF_SM + SSD_H + FOX_H],
                            g[:, OFF_U:OFF_G], g[:, OFF_G:OFF_SM]], axis=1)


def _layer_fwd(x, p, geom, dims):
    B, NC, Lp, pad = dims
    T = geom[0]
    rm = functools.partial(_rowmap, geom=geom)
    sv = {}
    xn1, = rm(_f_norm, [x], [p["norm1"]], [(D, BF16)], [], tile=384 if Lp % 384 == 0 else CH, name="norm1")
    Wc = p["w_cat"]
    pz = _mm(xn1, Wc, "nn", BF16, n=D, b_off=OFF_Z, name="in_z")
    pxbc = _mm(xn1, Wc, "nn", F32, n=CONV_DIM, b_off=OFF_XBC, name="in_xbc")
    qkv = _mm(xn1, Wc, "nn", BF16, n=3 * D, b_off=OFF_QKV, name="in_qkv")
    pu = _mm(xn1, Wc, "nn", BF16, n=D, b_off=OFF_U, name="in_u")
    pg = _mm(xn1, Wc, "nn", BF16, n=3 * D, b_off=OFF_G, name="in_g")
    psm = _mm(xn1, Wc, "nn", F32, n=CH, b_off=OFF_SM, name="in_sm")
    t_r = 384 if Lp % 384 == 0 else CH
    sm, = rm(_smallact, [psm], [p["smallbias"]], [(CH, F32)], [], tile=t_r, name="smallact")
    xbc = _conv_fwd(pxbc, p["conv_w"], p["conv_b"], geom=geom, tile=CH)
    y_ssd, states = _ssd_fwd(xbc, sm, p["a_log"], B=B, NC=NC)
    y_a, = rm(_ssd_post, [y_ssd, (xbc, D, 0), pz], [p["d_rep"], p["ssd_norm"]], [(D, BF16)], [], tile=t_r,
              name="ssd_post")
    cum = _cumsum_seq(sm, B=B, NC=NC, reverse=False, name="fox_cum")
    cumT = _fox_keybias(cum, B=B, Lp=Lp, pad=pad)
    y_b, lse = _fox_fwd(qkv, cumT, B=B, Lp=Lp)
    y_ssm, hst = _s5_fwd(pu, p["Bsg"], p["Csg"], p["lam"], B=B, NC=NC)
    y1, = rm(_s5_pre, [y_ssm, pu], [p["s5_d"]], [(D, BF16)], [], tile=t_r, name="s5_pre")
    tg = _mm(y1, p["w_glu"], "nn", BF16, name="s5_glu_mm")
    y_c, = rm(_s5_glu, [y_ssm, pu, tg], [p["s5_d"]], [(D, BF16)], [], tile=t_r, name="s5_glu")
    br = [_mm(yy, p["w_branch"][n], "nn", BF16, name=f"branch{n}") for n, yy in enumerate((y_a, y_b, y_c))]
    mix, = rm(_merge, [(pg, D, 0), (pg, D, 1), (pg, D, 2)] + br, [], [(D, BF16)], [], tile=CH, name="merge")
    x_mid = _mm(mix, p["w_out"], "nn", F32, res=x, name="out_proj")
    xn2, = rm(_f_norm, [x_mid], [p["norm2"]], [(D, BF16)], [], tile=t_r, name="norm2")
    hff = _mm(xn2, p["w_ffn_in"], "nn", BF16, name="ffn_in")
    act, = rm(_swiglu, [(hff, DFF, 0), (hff, DFF, 1)], [], [(DFF, BF16)], [], tile=CH, name="swiglu")
    x_out = _mm(act, p["w_ffn_out"], "nn", F32, res=x_mid, name="ffn_out")
    sv.update(x=x, xn1=xn1, pz=pz, pxbc=pxbc, qkv=qkv, pu=pu, pg=pg, psm=psm, sm=sm, xbc=xbc, y_ssd=y_ssd,
              states=states, y_a=y_a, cum=cum, cumT=cumT, y_b=y_b, lse=lse, y_ssm=y_ssm, hst=hst, y1=y1, tg=tg,
              y_c=y_c, br=br, mix=mix, x_mid=x_mid, xn2=xn2, hff=hff, act=act)
    return x_out, sv


def _layer_bwd(dx_out, p, sv, geom, dims):
    B, NC, Lp, pad = dims
    T = geom[0]
    rm = functools.partial(_rowmap, geom=geom)
    t_r = 384 if Lp % 384 == 0 else CH
    g = {}
    dact = _mm(dx_out, p["w_ffn_out"], "nt", BF16, name="ffn_out_dx")
    g["w_ffn_out"] = _mm(sv["act"], dx_out, "tn", F32, name="ffn_out_dw")
    dhff, = rm(_b_swiglu, [(sv["hff"], DFF, 0), (sv["hff"], DFF, 1), dact], [], [(2 * DFF, BF16)], [], tile=CH,
               name="swiglu_bwd")
    dxn2 = _mm(dhff, p["w_ffn_in"], "nt", F32, name="ffn_in_dx")
    g["w_ffn_in"] = _mm(sv["xn2"], dhff, "tn", F32, name="ffn_in_dw")
    dx_mid, g["norm2"] = rm(_b_norm, [sv["x_mid"], dxn2, dx_out], [p["norm2"]], [(D, F32)], [(1, D)], tile=t_r,
                            name="norm2_bwd")
    dmix = _mm(dx_mid, p["w_out"], "nt", BF16, name="out_proj_dx")
    g["w_out"] = _mm(sv["mix"], dx_mid, "tn", F32, name="out_proj_dw")
    pg = sv["pg"]
    dpg, db0, db1, db2 = rm(_b_merge, [(pg, D, 0), (pg, D, 1), (pg, D, 2)] + sv["br"] + [dmix], [],
                            [(3 * D, BF16), (D, BF16), (D, BF16), (D, BF16)], [], tile=CH, name="merge_bwd")
    ys = (sv["y_a"], sv["y_b"], sv["y_c"])
    dbs = (db0, db1, db2)
    g["w_branch"] = [_mm(ys[n], dbs[n], "tn", F32, name=f"branch{n}_dw") for n in range(3)]
    dy = [_mm(dbs[n], p["w_branch"][n], "nt", BF16, name=f"branch{n}_dx") for n in range(3)]
    dtg, dy1a = rm(_b_s5_glu, [sv["y_ssm"], sv["pu"], sv["tg"], dy[2]], [p["s5_d"]], [(D, BF16), (D, F32)], [],
                   tile=t_r, name="s5_glu_bwd")
    dy1b = _mm(dtg, p["w_glu"], "nt", BF16, name="s5_glu_mm_dx")
    g["w_glu"] = _mm(sv["y1"], dtg, "tn", F32, name="s5_glu_mm_dw")
    dys, du_skip, g["s5_d"] = rm(_b_s5_pre, [sv["y_ssm"], sv["pu"], dy1a, dy1b], [p["s5_d"]],
                                 [(D, F32), (D, F32)], [(1, D)], tile=t_r, name="s5_pre_bwd")
    du, g["Bsg"], g["Csg"], g["lam"] = _s5_bwd(sv["pu"], sv["hst"], dys, du_skip, p["Bsg"], p["Csg"], p["lam"],
                                               B=B, NC=NC)
    dq, dk, dv, dckT, dcq = _fox_bwd(sv["qkv"], dy[1], sv["y_b"], sv["lse"], sv["cumT"], B=B, Lp=Lp)
    dcum8 = dcq.reshape(B, FOX_H, Lp).transpose(0, 2, 1) + dckT.reshape(B, FOX_H, Lp).transpose(0, 2, 1)
    dcum = jnp.pad(dcum8.reshape(T, FOX_H), ((0, 0), (SSD_H, CH - SSD_H - FOX_H)))
    dlogf = _cumsum_seq(dcum, B=B, NC=NC, reverse=True, name="fox_cum_bwd")
    dy_ssd, dxs_skip, dz, g["d_rep"], g["ssd_norm"] = rm(
        _b_ssd_post, [sv["y_ssd"], (sv["xbc"], D, 0), sv["pz"], dy[0]], [p["d_rep"], p["ssd_norm"]],
        [(D, F32), (D, F32), (D, BF16)], [(1, D), (1, D)], tile=t_r, name="ssd_post_bwd")
    dxbc_act, ddt, g["a_log"] = _ssd_bwd(sv["xbc"], sv["sm"], p["a_log"], sv["states"], dy_ssd, dxs_skip, B=B, NC=NC)
    dpsm, g["smallbias"] = rm(_b_smallact, [sv["psm"], ddt, dlogf], [p["smallbias"]], [(CH, BF16)], [(1, CH)],
                              tile=t_r, name="smallact_bwd")
    dconv, g["conv_w"], g["conv_b"] = _conv_bwd_pre(sv["pxbc"], dxbc_act, p["conv_w"], p["conv_b"], geom=geom, tile=CH)
    dpxbc = _conv_bwd_x(dconv, p["conv_w"], geom=geom, tile=CH)
    dproj = jnp.concatenate([dz, dpxbc, dq.astype(BF16), dk, dv, du, dpg, dpsm], axis=1)
    dxn1 = _mm(dproj, p["w_cat"], "nt", F32, name="in_dx")
    g["w_cat"] = _mm(sv["xn1"], dproj, "tn", F32, name="in_dw")
    dx_in, g["norm1"] = rm(_b_norm, [sv["x"], dxn1, dx_mid], [p["norm1"]], [(D, F32)], [(1, D)], tile=t_r,
                           name="norm1_bwd")
    return dx_in, g


_BIG = ["w_in", "s5_w_glu", "w_branch", "w_out", "w_ffn_in", "w_ffn_out"]
_NAMES = ['meta', 'norm1', 'w_in', 'ssd_conv_w', 'ssd_conv_b', 'ssd_dt_bias', 'ssd_a_log', 'ssd_d', 'ssd_norm',
          'fox_bf', 's5_lam_re', 's5_lam_im', 's5_b_re', 's5_b_im', 's5_c_re', 's5_c_im', 's5_log_step', 's5_d',
          's5_w_glu', 'w_branch', 'w_out', 'norm2', 'w_ffn_in', 'w_ffn_out', 'norm_f']
_SHARD_AXIS = {"meta": 1, "ssd_conv_w": 2}
_BIG_AXIS = {"w_in": 2, "s5_w_glu": 1, "w_branch": 2, "w_out": 1, "w_ffn_in": 2, "w_ffn_out": 1}


def kernel(x, meta, norm1, w_in, ssd_conv_w, ssd_conv_b, ssd_dt_bias, ssd_a_log, ssd_d, ssd_norm, fox_bf, s5_lam_re, s5_lam_im, s5_b_re, s5_b_im, s5_c_re, s5_c_im, s5_log_step, s5_d, s5_w_glu, w_branch, w_out, norm2, w_ffn_in, w_ffn_out, norm_f, loss_target, m_meta, m_norm1, m_w_in, m_ssd_conv_w, m_ssd_conv_b, m_ssd_dt_bias, m_ssd_a_log, m_ssd_d, m_ssd_norm, m_fox_bf, m_s5_lam_re, m_s5_lam_im, m_s5_b_re, m_s5_b_im, m_s5_c_re, m_s5_c_im, m_s5_log_step, m_s5_d, m_s5_w_glu, m_w_branch, m_w_out, m_norm2, m_w_ffn_in, m_w_ffn_out, m_norm_f, v_meta, v_norm1, v_w_in, v_ssd_conv_w, v_ssd_conv_b, v_ssd_dt_bias, v_ssd_a_log, v_ssd_d, v_ssd_norm, v_fox_bf, v_s5_lam_re, v_s5_lam_im, v_s5_b_re, v_s5_b_im, v_s5_c_re, v_s5_c_im, v_s5_log_step, v_s5_d, v_s5_w_glu, v_w_branch, v_w_out, v_norm2, v_w_ffn_in, v_w_ffn_out, v_norm_f):
    args = locals()
    W = {n: args[n] for n in _NAMES}
    Mo = {n: args["m_" + n] for n in _NAMES}
    Vo = {n: args["v_" + n] for n in _NAMES}
    B, S, _ = x.shape
    depth = norm1.shape[0]
    L = S + N_META
    Lp = ((L + CH - 1) // CH) * CH
    pad = Lp - L
    assert pad + N_META == CH and S % CH == 0
    NC = Lp // CH
    T = B * Lp
    geom = (T, Lp, pad)
    dims = (B, NC, Lp, pad)
    xi, yi, ci = _me()
    dev = 4 * xi + 2 * yi + ci

    gath = {n: _all_gather(W[n].astype(BF16), "gather_" + n) for n in _BIG}
    full = {n: jnp.concatenate([gath[n][k] for k in range(N_DEV)], axis=_BIG_AXIS[n]) for n in _BIG}
    sm_pack = _Packer([meta.shape, ssd_conv_w.shape], 8)
    sm_g = _all_gather(sm_pack.pack([meta, ssd_conv_w], F32), "gather_small")
    sm_parts = [sm_pack.unpack(sm_g[k]) for k in range(N_DEV)]
    meta_full = jnp.concatenate([sm_parts[k][0] for k in range(N_DEV)], axis=1)
    conv_w_full = jnp.concatenate([sm_parts[k][1] for k in range(N_DEV)], axis=2)

    layers = []
    s5_in = []
    for l in range(depth):
        lre = _s5_tile(s5_lam_re[l])
        lim = _s5_tile(s5_lam_im[l])
        lst = _s5_tile(jnp.repeat(s5_log_step[l], S5_P))
        bre = _s5_tile_b(s5_b_re[l])
        bim = _s5_tile_b(s5_b_im[l])
        s5_in.append((lre, lim, lst, bre, bim))
        a, b, br_, bi_ = _s5_params(lre, lim, lst, bre, bim)
        Bsg, Csg = _s5_blockdiag(_s5_untile_b(br_), _s5_untile_b(bi_), s5_c_re[l], s5_c_im[l])
        NS = S5_G // S5_SG
        lam = jnp.concatenate([a.reshape(NS, 1, S5_W), b.reshape(NS, 1, S5_W)], axis=2)
        zpad = jnp.zeros((CH - SSD_H - FOX_H,), F32)
        layers.append(dict(
            norm1=norm1[l][None], w_cat=_w_cat(full["w_in"][l]),
            smallbias=jnp.concatenate([ssd_dt_bias[l], fox_bf[l], zpad])[None],
            conv_w=conv_w_full[l], conv_b=ssd_conv_b[l][None],
            a_log=jnp.concatenate([ssd_a_log[l], jnp.zeros((CH - SSD_H,), F32)])[None],
            d_rep=jnp.repeat(ssd_d[l], SSD_P)[None], ssd_norm=ssd_norm[l][None],
            Bsg=Bsg, Csg=Csg, lam=lam, s5_d=s5_d[l][None], w_glu=full["s5_w_glu"][l],
            w_branch=[full["w_branch"][l, n] for n in range(3)], w_out=full["w_out"][l],
            norm2=norm2[l][None], w_ffn_in=full["w_ffn_in"][l], w_ffn_out=full["w_ffn_out"][l]))

    xs = jnp.concatenate([jnp.zeros((B, pad, D), F32), jnp.broadcast_to(meta_full[None], (B, N_META, D)), x], axis=1)
    h = xs.reshape(T, D)
    saved = []
    for l in range(depth):
        h, sv = _layer_fwd(h, layers[l], geom, dims)
        saved.append(sv)
    dh, loss_row, g_nf = _loss_head(h, norm_f[None], loss_target.reshape(B * S, D), B=B, NC=NC, S=S)
    loss = lax.psum(loss_row[0, 0], AXES)

    G = {n: [None] * depth for n in _NAMES}
    for l in reversed(range(depth)):
        dh, g = _layer_bwd(dh, layers[l], saved[l], geom, dims)
        saved[l] = None
        G["norm1"][l] = g["norm1"][0]
        G["norm2"][l] = g["norm2"][0]
        G["w_in"][l] = _w_uncat(g["w_cat"])
        G["ssd_conv_w"][l] = g["conv_w"]
        G["ssd_conv_b"][l] = g["conv_b"][0]
        G["ssd_dt_bias"][l] = g["smallbias"][0, 0:SSD_H]
        G["fox_bf"][l] = g["smallbias"][0, SSD_H:SSD_H + FOX_H]
        G["ssd_a_log"][l] = g["a_log"][0, 0:SSD_H]
        G["ssd_d"][l] = g["d_rep"].reshape(SSD_H, SSD_P).sum(axis=1)
        G["ssd_norm"][l] = g["ssd_norm"][0]
        dbr, dbi, dcr, dci = _s5_unblock(g["Bsg"], g["Csg"])
        da = _s5_tile(g["lam"][:, 0, 0:S5_W])
        db = _s5_tile(g["lam"][:, 0, S5_W:])
        dlre, dlim, dlst, dbre, dbim = _s5_params_bwd(*s5_in[l], da, db, _s5_tile_b(dbr), _s5_tile_b(dbi))
        G["s5_lam_re"][l] = dlre.reshape(S5_G, S5_P)
        G["s5_lam_im"][l] = dlim.reshape(S5_G, S5_P)
        G["s5_log_step"][l] = dlst.reshape(S5_G, S5_P).sum(axis=1)
        G["s5_b_re"][l] = _s5_untile_b(dbre)
        G["s5_b_im"][l] = _s5_untile_b(dbim)
        G["s5_c_re"][l] = dcr
        G["s5_c_im"][l] = dci
        G["s5_d"][l] = g["s5_d"][0]
        G["s5_w_glu"][l] = g["w_glu"]
        G["w_branch"][l] = jnp.stack(g["w_branch"])
        G["w_out"][l] = g["w_out"]
        G["w_ffn_in"][l] = g["w_ffn_in"]
        G["w_ffn_out"][l] = g["w_ffn_out"]
    dxs = dh.reshape(B, Lp, D)
    grad_x = dxs[:, pad + N_META:, :]
    part = {n: jnp.stack(G[n]) for n in _NAMES if n not in ("meta", "norm_f")}
    part["meta"] = dxs[:, pad:pad + N_META, :].sum(axis=0)
    part["norm_f"] = g_nf[0]

    grads = {}
    for n in _BIG:
        ax = _BIG_AXIS[n]
        a = part[n]
        a = a.reshape(a.shape[:ax] + (N_DEV, a.shape[ax] // N_DEV) + a.shape[ax + 1:])
        grads[n] = _reduce_scatter(jnp.moveaxis(a, ax, 0), n)

    small = [n for n in _NAMES if n not in _BIG]
    sp = _Packer([part[n].shape for n in small], 128)
    tot = sp.unpack(_sum8(_all_gather(sp.pack([part[n] for n in small], F32), "gather_small_grads")))
    for n, t in zip(small, tot):
        if n in _SHARD_AXIS:
            ax = _SHARD_AXIS[n]
            w = W[n].shape[ax]
            t = lax.dynamic_slice_in_dim(t, dev * w, w, axis=ax)
        grads[n] = t

    delta, new_m, new_v = {}, {}, {}
    for n in _BIG:
        delta[n], new_m[n], new_v[n] = _adam(W[n], grads[n], Mo[n], Vo[n], "adam_" + n)
    ap = _Packer([W[n].shape for n in small], 128)
    d_, m_, v_ = _adam(ap.pack([W[n] for n in small], F32), ap.pack([grads[n] for n in small], F32),
                       ap.pack([Mo[n] for n in small], F32), ap.pack([Vo[n] for n in small], F32), "adam_small")
    for n, a, b, c in zip(small, ap.unpack(d_), ap.unpack(m_), ap.unpack(v_)):
        delta[n], new_m[n], new_v[n] = a, b, c
    return (loss, grad_x, *[grads[n] for n in _NAMES], *[delta[n] for n in _NAMES],
            *[new_m[n] for n in _NAMES], *[new_v[n] for n in _NAMES])
```

```python
import functools
import math

import numpy as np
import jax
import jax.numpy as jnp
from jax import lax
from jax.experimental import pallas as pl
from jax.experimental.pallas import tpu as pltpu

F32 = jnp.float32
BF16 = jnp.bfloat16
AXES = ("x", "y", "c")
MESH = pl.DeviceIdType.MESH
N_DEV = 8

D = 1024
N_META = 16
CH = 128
EPS = 1e-6
NEG = -1e30
SSD_H, SSD_P, SSD_N, SSD_G = 16, 64, 128, 2
CONV_K, CONV_DIM = 4, 1536
FOX_H, FOX_DH = 8, 128
S5_G, S5_P, S5_C = 64, 64, 16
S5_SG = 8
S5_W = S5_SG * S5_P
DFF = 2816
D_IN = 9752
OFF_Z, OFF_XBC, OFF_QKV, OFF_U, OFF_G, OFF_SM, D_CAT = 0, 1024, 2560, 5632, 6656, 9728, 9856
O_Z, O_XBC, O_DT, O_QKV, O_F, O_U, O_G = 0, 1024, 2560, 2576, 5648, 5656, 6680

ADAM_LR, ADAM_B1, ADAM_B2, ADAM_EPS, ADAM_WD, ADAM_STEP = 0.001, 0.9, 0.999, 1e-08, 0.01, 10

VMEM_LIMIT_V7X = 52 * 1024 * 1024
HI = lax.Precision.HIGHEST


def _cp(sem=None):
    return pltpu.CompilerParams(dimension_semantics=sem, vmem_limit_bytes=VMEM_LIMIT_V7X)


def _pick(n, cands):
    for c in cands:
        if n % c == 0:
            return c
    raise ValueError(f"no tile for {n}")


_TILES = (1408, 1024, 896, 768, 512, 384, 256, 128)


def _mm(a, b, mode, out_dtype, *, name, n=None, b_off=0, res=None, tm=None, tn=None, tk=None):
    if mode == "tn":
        K, M = a.shape
    else:
        M, K = a.shape
    if mode == "nt":
        N = b.shape[0]
    else:
        N = n if n is not None else b.shape[1]
    tm = tm or _pick(M, (1024, 768, 512, 384, 256, 128, 64, 16, 8))
    wide = (1408,) if mode == "tn" else ()
    tn = tn or _pick(math.gcd(N, b_off) if b_off else N, wide + (1024, 896, 768, 512, 384, 256, 128))
    tk = tk or _pick(K, _TILES)
    nk = K // tk
    joff = b_off // tn

    def body(*refs):
        if res is None:
            a_ref, b_ref, o_ref, acc = refs
            r_ref = None
        else:
            a_ref, b_ref, r_ref, o_ref, acc = refs
        k = pl.program_id(2)
        av = a_ref[...].astype(BF16)
        bv = b_ref[...].astype(BF16)
        if mode == "nn":
            p = jnp.dot(av, bv, preferred_element_type=F32)
        elif mode == "nt":
            p = lax.dot_general(av, bv, (((1,), (1,)), ((), ())), preferred_element_type=F32)
        else:
            p = lax.dot_general(av, bv, (((0,), (0,)), ((), ())), preferred_element_type=F32)

        @pl.when(k == 0)
        def _():
            acc[...] = p

        @pl.when(k > 0)
        def _():
            acc[...] += p

        @pl.when(k == nk - 1)
        def _():
            r = acc[...]
            if r_ref is not None:
                r = r + r_ref[...]
            o_ref[...] = r.astype(o_ref.dtype)

    if mode == "tn":
        a_spec = pl.BlockSpec((tk, tm), lambda i, j, k: (k, i))
    else:
        a_spec = pl.BlockSpec((tm, tk), lambda i, j, k: (i, k))
    if mode == "nt":
        b_spec = pl.BlockSpec((tn, tk), lambda i, j, k: (j, k))
    else:
        b_spec = pl.BlockSpec((tk, tn), lambda i, j, k: (k, j + joff))
    o_spec = pl.BlockSpec((tm, tn), lambda i, j, k: (i, j))
    in_specs = [a_spec, b_spec] + ([o_spec] if res is not None else [])
    args = (a, b) + ((res,) if res is not None else ())
    return pl.pallas_call(
        body, name=name, grid=(M // tm, N // tn, nk),
        in_specs=in_specs, out_specs=o_spec,
        out_shape=jax.ShapeDtypeStruct((M, N), out_dtype),
        scratch_shapes=[pltpu.VMEM((tm, tn), F32)],
        compiler_params=_cp(("parallel", "parallel", "arbitrary")),
    )(*args)


def _rowmap(fn, row_ins, const_ins, row_outs, acc_outs, *, geom, tile, name):
    T, Lp, pad = geom
    assert Lp % tile == 0
    per_seq = Lp // tile
    specs, args = [], []
    for r in row_ins:
        arr, w, cb = r if isinstance(r, tuple) else (r, r.shape[1], 0)
        specs.append(pl.BlockSpec((tile, w), functools.partial(lambda i, cb: (i, cb), cb=cb)))
        args.append(arr)
    for c in const_ins:
        specs.append(pl.BlockSpec(c.shape, functools.partial(lambda i, nd: (0,) * nd, nd=c.ndim)))
        args.append(c)
    n_r, n_c, n_o, n_a = len(row_ins), len(const_ins), len(row_outs), len(acc_outs)
    out_specs = [pl.BlockSpec((tile, w), lambda i: (i, 0)) for w, _ in row_outs]
    out_specs += [pl.BlockSpec(s, lambda i: (0, 0)) for s in acc_outs]
    out_shape = [jax.ShapeDtypeStruct((T, w), dt) for w, dt in row_outs]
    out_shape += [jax.ShapeDtypeStruct(s, F32) for s in acc_outs]

    def body(*refs):
        i = pl.program_id(0)
        pos = (i % per_seq) * tile + lax.broadcasted_iota(jnp.int32, (tile, 1), 0)
        valid = pos >= pad
        vals = [r[...].astype(F32) for r in refs[:n_r]] + [r[...] for r in refs[n_r:n_r + n_c]]
        outs = fn(valid, *vals)
        if not isinstance(outs, (tuple, list)):
            outs = (outs,)
        orefs = refs[n_r + n_c:]
        for r, v in zip(orefs[:n_o], outs[:n_o]):
            r[...] = v.astype(r.dtype)
        for r, v in zip(orefs[n_o:], outs[n_o:]):
            @pl.when(i == 0)
            def _(r=r, v=v):
                r[...] = v

            @pl.when(i > 0)
            def _(r=r, v=v):
                r[...] += v

    res = pl.pallas_call(
        body, name=name, grid=(T // tile,), in_specs=specs, out_specs=out_specs, out_shape=out_shape,
        compiler_params=_cp(("arbitrary",)),
    )(*args)
    return res


def _sigmoid(x):
    return 1.0 / (1.0 + jnp.exp(-x))


def _silu(x):
    return x * _sigmoid(x)


def _softplus(x):
    return jnp.maximum(x, 0.0) + jnp.log(1.0 + jnp.exp(-jnp.abs(x)))


def _gelu(x):
    return 0.5 * x * (1.0 + jnp.tanh(math.sqrt(2.0 / math.pi) * (x + 0.044715 * x * x * x)))


def _rms(x, w):
    return x * lax.rsqrt(jnp.mean(x * x, axis=-1, keepdims=True) + EPS) * w


def _colsum(v):
    return jnp.sum(v, axis=0, keepdims=True)


def _f_norm(valid, x, w):
    return _rms(x, w)


def _b_norm(valid, x, dxn, dres, w):
    _, vjp = jax.vjp(_rms, x, w)
    dx, dw = vjp(dxn)
    return jnp.where(valid, dx + dres, 0.0), dw


def _smallact(valid, raw, bias):
    lane = lax.broadcasted_iota(jnp.int32, raw.shape, 1)
    v = raw + bias
    dt = _softplus(v)
    logf = -_softplus(-v)
    out = jnp.where(lane < SSD_H, dt, jnp.where(lane < SSD_H + FOX_H, logf, 0.0))
    return jnp.where(valid, out, 0.0)


def _b_smallact(valid, raw, d1, d2, bias):
    _, vjp = jax.vjp(lambda r, b: _smallact(valid, r, b), raw, bias)
    return vjp(d1 + d2)


def _ssd_post(valid, y, xs, z, drep, nw):
    y = (y + xs * drep) * _silu(z)
    return _rms(y, nw)


def _b_ssd_post(valid, y, xs, z, dya, drep, nw):
    _, vjp = jax.vjp(lambda a, b, c, d, e: _ssd_post(valid, a, b, c, d, e), y, xs, z, drep, nw)
    dy, dxs, dz, dd, dn = vjp(dya)
    return dy, dxs, dz, dd, dn


def _s5_pre(valid, ys, u, d):
    return _gelu(ys + d * u)


def _s5_glu(valid, ys, u, t, d):
    y1 = _gelu(ys + d * u)
    return y1 * _sigmoid(t)


def _b_s5_glu(valid, ys, u, t, dyc, d):
    y1 = _gelu(ys + d * u)
    _, vjp = jax.vjp(lambda a, b: a * _sigmoid(b), y1, t)
    dy1, dt = vjp(dyc)
    return dt, dy1


def _b_s5_pre(valid, ys, u, dy1a, dy1b, d):
    _, vjp = jax.vjp(lambda a, b, c: _gelu(a + c * b), ys, u, d)
    dys, du, dd = vjp(dy1a + dy1b)
    return dys, du, dd


def _merge(valid, g0, g1, g2, b0, b1, b2):
    m = _sigmoid(g0) * b0 + _sigmoid(g1) * b1 + _sigmoid(g2) * b2
    return jnp.where(valid, m, 0.0)


def _b_merge(valid, g0, g1, g2, b0, b1, b2, dmix):
    _, vjp = jax.vjp(lambda *a: _merge(valid, *a), g0, g1, g2, b0, b1, b2)
    d = vjp(dmix)
    return jnp.concatenate(d[:3], axis=1), d[3], d[4], d[5]


def _swiglu(valid, g, up):
    return _silu(g) * up


def _b_swiglu(valid, g, up, dact):
    _, vjp = jax.vjp(lambda a, b: _silu(a) * b, g, up)
    dg, dup = vjp(dact)
    return jnp.concatenate([dg, dup], axis=1)


def _conv_taps(ext, tile):
    taps = []
    for k in range(CONV_K):
        sh = CONV_K - 1 - k
        v = ext if sh == 0 else pltpu.roll(ext, sh, 0)
        taps.append(v[8:8 + tile])
    return taps


def _conv_fwd(x, w, b, *, geom, tile):
    T, Lp, pad = geom
    per_seq = Lp // tile
    hb = tile // 8

    def body(x_ref, h_ref, w_ref, b_ref, o_ref):
        i = pl.program_id(0)
        pos = (i % per_seq) * tile + lax.broadcasted_iota(jnp.int32, (tile, 1), 0)
        ext = jnp.concatenate([h_ref[...], x_ref[...]], axis=0)
        taps = _conv_taps(ext, tile)
        acc = b_ref[...] + taps[0] * w_ref[0:1, :]
        for k in range(1, CONV_K):
            acc = acc + taps[k] * w_ref[k:k + 1, :]
        o_ref[...] = jnp.where(pos >= pad, _silu(acc), 0.0)

    return pl.pallas_call(
        body, name="conv_fwd", grid=(T // tile,),
        in_specs=[pl.BlockSpec((tile, CONV_DIM), lambda i: (i, 0)),
                  pl.BlockSpec((8, CONV_DIM), lambda i: (jnp.maximum(i * hb - 1, 0), 0)),
                  pl.BlockSpec((CONV_K, CONV_DIM), lambda i: (0, 0)),
                  pl.BlockSpec((1, CONV_DIM), lambda i: (0, 0))],
        out_specs=pl.BlockSpec((tile, CONV_DIM), lambda i: (i, 0)),
        out_shape=jax.ShapeDtypeStruct((T, CONV_DIM), F32),
        compiler_params=_cp(("arbitrary",)),
    )(x, x, w, b)


def _conv_bwd_pre(x, dact, w, b, *, geom, tile):
    T, Lp, pad = geom
    per_seq = Lp // tile
    hb = tile // 8

    def body(x_ref, h_ref, d_ref, w_ref, b_ref, dc_ref, dw_ref, db_ref):
        i = pl.program_id(0)
        pos = (i % per_seq) * tile + lax.broadcasted_iota(jnp.int32, (tile, 1), 0)
        ext = jnp.concatenate([h_ref[...], x_ref[...]], axis=0)
        taps = _conv_taps(ext, tile)
        acc = b_ref[...] + taps[0] * w_ref[0:1, :]
        for k in range(1, CONV_K):
            acc = acc + taps[k] * w_ref[k:k + 1, :]
        sg = _sigmoid(acc)
        dsilu = sg * (1.0 + acc * (1.0 - sg))
        dc = jnp.where(pos >= pad, d_ref[...] * dsilu, 0.0)
        dc_ref[...] = dc
        dw = jnp.concatenate([_colsum(dc * taps[k]) for k in range(CONV_K)], axis=0)
        db = _colsum(dc)

        @pl.when(i == 0)
        def _():
            dw_ref[...] = dw
            db_ref[...] = db

        @pl.when(i > 0)
        def _():
            dw_ref[...] += dw
            db_ref[...] += db

    return pl.pallas_call(
        body, name="conv_bwd_pre", grid=(T // tile,),
        in_specs=[pl.BlockSpec((tile, CONV_DIM), lambda i: (i, 0)),
                  pl.BlockSpec((8, CONV_DIM), lambda i: (jnp.maximum(i * hb - 1, 0), 0)),
                  pl.BlockSpec((tile, CONV_DIM), lambda i: (i, 0)),
                  pl.BlockSpec((CONV_K, CONV_DIM), lambda i: (0, 0)),
                  pl.BlockSpec((1, CONV_DIM), lambda i: (0, 0))],
        out_specs=[pl.BlockSpec((tile, CONV_DIM), lambda i: (i, 0)),
                   pl.BlockSpec((CONV_K, CONV_DIM), lambda i: (0, 0)),
                   pl.BlockSpec((1, CONV_DIM), lambda i: (0, 0))],
        out_shape=[jax.ShapeDtypeStruct((T, CONV_DIM), F32),
                   jax.ShapeDtypeStruct((CONV_K, CONV_DIM), F32),
                   jax.ShapeDtypeStruct((1, CONV_DIM), F32)],
        compiler_params=_cp(("arbitrary",)),
    )(x, x, dact, w, b)


def _conv_bwd_x(dc, w, *, geom, tile):
    T, Lp, pad = geom
    nt = T // tile
    hb = tile // 8

    def body(d_ref, h_ref, w_ref, o_ref):
        i = pl.program_id(0)
        halo = jnp.where(i < nt - 1, h_ref[...], 0.0)
        ext = jnp.concatenate([d_ref[...], halo], axis=0)
        n_ext = tile + 8
        acc = ext[0:tile] * w_ref[CONV_K - 1:CONV_K, :]
        for j in range(1, CONV_K):
            acc = acc + pltpu.roll(ext, n_ext - j, 0)[0:tile] * w_ref[CONV_K - 1 - j:CONV_K - j, :]
        o_ref[...] = acc.astype(o_ref.dtype)

    return pl.pallas_call(
        body, name="conv_bwd_x", grid=(nt,),
        in_specs=[pl.BlockSpec((tile, CONV_DIM), lambda i: (i, 0)),
                  pl.BlockSpec((8, CONV_DIM), lambda i: (jnp.minimum((i + 1) * hb, nt * hb - 1), 0)),
                  pl.BlockSpec((CONV_K, CONV_DIM), lambda i: (0, 0))],
        out_specs=pl.BlockSpec((tile, CONV_DIM), lambda i: (i, 0)),
        out_shape=jax.ShapeDtypeStruct((T, CONV_DIM), BF16),
        compiler_params=_cp(("arbitrary",)),
    )(dc, dc, w)


def _ssd_common(sm_ref, alog_ref):
    lane = lax.broadcasted_iota(jnp.int32, (1, CH), 1)
    A = jnp.where(lane < SSD_H, -jnp.exp(alog_ref[...]), 0.0)
    dt = sm_ref[...]
    adt = dt * A
    r = lax.broadcasted_iota(jnp.int32, (CH, CH), 0)
    c = lax.broadcasted_iota(jnp.int32, (CH, CH), 1)
    tril = (r >= c).astype(F32)
    cs = jnp.dot(tril, adt, precision=HI, preferred_element_type=F32)
    csT = cs.T
    cs_last = jnp.sum(jnp.where(r == CH - 1, cs, 0.0), axis=0, keepdims=True)
    return A, dt, cs, csT, cs_last, tril, r, c


def _ssd_lanes():
    r = lax.broadcasted_iota(jnp.int32, (CH, D), 0)
    c = lax.broadcasted_iota(jnp.int32, (CH, D), 1)
    return (c // SSD_P == r).astype(F32)


def _ssd_per_lane(dt, cs):
    ex = _ssd_lanes()
    dt_rep = jnp.dot(dt, ex, precision=HI, preferred_element_type=F32)
    cs_rep = jnp.dot(cs, ex, precision=HI, preferred_element_type=F32)
    r = lax.broadcasted_iota(jnp.int32, (CH, D), 0)
    last_rep = jnp.sum(jnp.where(r == CH - 1, cs_rep, 0.0), axis=0, keepdims=True)
    return dt_rep, cs_rep, last_rep


def _ssd_per_head(*per_lane):
    ex = _ssd_lanes()
    return [lax.dot_general(v, ex, (((1,), (1,)), ((), ())), precision=HI, preferred_element_type=F32)
            for v in per_lane]


def _nt(a, b):
    return lax.dot_general(a, b, (((1,), (1,)), ((), ())), preferred_element_type=F32)


def _tn(a, b):
    return lax.dot_general(a, b, (((0,), (0,)), ((), ())), preferred_element_type=F32)


def _nn(a, b):
    return jnp.dot(a, b, preferred_element_type=F32)


def _ssd_fwd(xbc, sm, alog, *, B, NC):
    T = B * NC * CH

    def body(x_ref, sm_ref, alog_ref, y_ref, st_ref, S):
        cidx = pl.program_id(1)

        @pl.when(cidx == 0)
        def _():
            S[...] = jnp.zeros_like(S)

        st_ref[0] = S[...]
        A, dt, cs, csT, cs_last, tril, _, _ = _ssd_common(sm_ref, alog_ref)
        dt_rep, cs_rep, last_rep = _ssd_per_lane(dt, cs)
        xdt = x_ref[:, 0:D] * dt_rep
        xdec = (xdt * jnp.exp(last_rep - cs_rep)).astype(BF16)
        e_rep = jnp.exp(cs_rep)
        HG = SSD_H // SSD_G
        for g in range(SSD_G):
            gl = slice(g * HG * SSD_P, (g + 1) * HG * SSD_P)
            Bb = x_ref[:, D + g * SSD_N:D + (g + 1) * SSD_N].astype(BF16)
            Cb = x_ref[:, D + SSD_G * SSD_N + g * SSD_N:D + SSD_G * SSD_N + (g + 1) * SSD_N].astype(BF16)
            G = _nt(Cb, Bb)
            STg = S[g * HG:(g + 1) * HG].reshape(HG * SSD_P, SSD_N)
            y_off = e_rep[:, gl] * _nt(Cb, STg.astype(BF16))
            upd = _tn(xdec[:, gl], Bb)
            for rr in range(HG):
                h = g * HG + rr
                hl = slice(h * SSD_P, (h + 1) * SSD_P)
                col = cs[:, h:h + 1]
                row = csT[h:h + 1, :]
                Ld = jnp.where(tril > 0, jnp.exp(jnp.minimum(col - row, 0.0)), 0.0)
                M = (G * Ld).astype(BF16)
                y_ref[:, hl] = _nn(M, xdt[:, hl].astype(BF16)) + y_off[:, rr * SSD_P:(rr + 1) * SSD_P]
                rows = slice(rr * SSD_P, (rr + 1) * SSD_P)
                S[h] = jnp.exp(cs_last[:, h:h + 1]) * STg[rows] + upd[rows]

    return pl.pallas_call(
        body, name="ssd_fwd", grid=(B, NC),
        in_specs=[pl.BlockSpec((CH, CONV_DIM), lambda b, c: (b * NC + c, 0)),
                  pl.BlockSpec((CH, CH), lambda b, c: (b * NC + c, 0)),
                  pl.BlockSpec((1, CH), lambda b, c: (0, 0))],
        out_specs=[pl.BlockSpec((CH, D), lambda b, c: (b * NC + c, 0)),
                   pl.BlockSpec((1, SSD_H, SSD_P, SSD_N), lambda b, c: (b * NC + c, 0, 0, 0))],
        out_shape=[jax.ShapeDtypeStruct((T, D), F32),
                   jax.ShapeDtypeStruct((B * NC, SSD_H, SSD_P, SSD_N), F32)],
        scratch_shapes=[pltpu.VMEM((SSD_H, SSD_P, SSD_N), F32)],
        compiler_params=_cp(("arbitrary", "arbitrary")),
    )(xbc, sm, alog)


def _ssd_bwd(xbc, sm, alog, states, dy, dxs_skip, *, B, NC):
    T = B * NC * CH

    def rix(b, c):
        return b * NC + (NC - 1 - c)

    def body(x_ref, sm_ref, alog_ref, st_ref, dy_ref, sk_ref, dx_ref, ddt_ref, dal_ref, dS):
        bidx = pl.program_id(0)
        cidx = pl.program_id(1)

        @pl.when(cidx == 0)
        def _():
            dS[...] = jnp.zeros_like(dS)

        A, dt, cs, csT, cs_last, tril, r, c = _ssd_common(sm_ref, alog_ref)
        lane = lax.broadcasted_iota(jnp.int32, (1, CH), 1)
        dt_rep, cs_rep, last_rep = _ssd_per_lane(dt, cs)
        xs = x_ref[:, 0:D]
        xdt = xs * dt_rep
        e_rep = jnp.exp(cs_rep)
        dec_rep = jnp.exp(last_rep - cs_rep)
        dye = dy_ref[...] * e_rep
        xdec = xdt * dec_rep
        HG = SSD_H // SSD_G
        DCcol = jnp.zeros((CH, CH), F32)
        DCrow = jnp.zeros((CH, CH), F32)
        dlast = jnp.zeros((1, CH), F32)
        t_off, t_dec, dx_state = [], [], []
        for g in range(SSD_G):
            ob = D + g * SSD_N
            oc = D + SSD_G * SSD_N + g * SSD_N
            gl = slice(g * HG * SSD_P, (g + 1) * HG * SSD_P)
            Bb = x_ref[:, ob:ob + SSD_N].astype(BF16)
            Cb = x_ref[:, oc:oc + SSD_N].astype(BF16)
            G = _nt(Cb, Bb)
            STg = st_ref[0, g * HG:(g + 1) * HG].reshape(HG * SSD_P, SSD_N)
            dSTg = dS[g * HG:(g + 1) * HG].reshape(HG * SSD_P, SSD_N)
            STb = STg.astype(BF16)
            dSTb = dSTg.astype(BF16)
            dyeb = dye[:, gl].astype(BF16)
            t_off.append(dye[:, gl] * _nt(Cb, STb))
            dCg = _nn(dyeb, STb)
            dS_in = _tn(dyeb, Cb)
            Z = _nt(Bb, dSTb)
            dx_state.append(dec_rep[:, gl] * Z)
            t_dec.append(xdec[:, gl] * Z)
            dBg = _nn(xdec[:, gl].astype(BF16), dSTb)
            dG = jnp.zeros((CH, CH), F32)
            for rr in range(HG):
                h = g * HG + rr
                hl = slice(h * SSD_P, (h + 1) * SSD_P)
                rows = slice(rr * SSD_P, (rr + 1) * SSD_P)
                col = cs[:, h:h + 1]
                row = csT[h:h + 1, :]
                Ld = jnp.where(tril > 0, jnp.exp(jnp.minimum(col - row, 0.0)), 0.0)
                Mf = G * Ld
                dyb = dy_ref[:, hl].astype(BF16)
                dx_ref[:, hl] = _tn(Mf.astype(BF16), dyb)
                dM = _nt(dyb, xdt[:, hl].astype(BF16))
                dG = dG + dM * Ld
                W = dM * Mf
                DCcol = DCcol + jnp.where(c == h, jnp.sum(W, axis=1, keepdims=True), 0.0)
                DCrow = DCrow - jnp.where(r == h, jnp.sum(W, axis=0, keepdims=True), 0.0)
                el = jnp.exp(cs_last[:, h:h + 1])
                dl = el * jnp.sum(jnp.sum(dSTg[rows] * STg[rows], axis=1, keepdims=True), axis=0, keepdims=True)
                dlast = dlast + jnp.where(lane == h, dl, 0.0)
                dS[h] = dS_in[rows] + el * dSTg[rows]
            dGb = dG.astype(BF16)
            dx_ref[:, ob:ob + SSD_N] = dBg + _tn(dGb, Cb)
            dx_ref[:, oc:oc + SSD_N] = dCg + _nn(dGb, Bb)
        dxdt = dx_ref[:, 0:D] + jnp.concatenate(dx_state, axis=1)
        dx_ref[:, 0:D] = dxdt * dt_rep + sk_ref[...]
        s_off, s_dec, DX = _ssd_per_head(jnp.concatenate(t_off, axis=1), jnp.concatenate(t_dec, axis=1), dxdt * xs)
        dlast = dlast + jnp.sum(s_dec, axis=0, keepdims=True)
        DC = DCcol + s_off - s_dec + DCrow.T + jnp.where(r == CH - 1, dlast, 0.0)
        triu = (r <= c).astype(F32)
        dadt = jnp.dot(triu, DC, precision=HI, preferred_element_type=F32)
        ddt_ref[...] = dadt * A + DX
        dal = jnp.sum(dadt * dt, axis=0, keepdims=True) * A

        @pl.when((bidx == 0) & (cidx == 0))
        def _():
            dal_ref[...] = dal

        @pl.when((bidx > 0) | (cidx > 0))
        def _():
            dal_ref[...] += dal

    return pl.pallas_call(
        body, name="ssd_bwd", grid=(B, NC),
        in_specs=[pl.BlockSpec((CH, CONV_DIM), lambda b, c: (rix(b, c), 0)),
                  pl.BlockSpec((CH, CH), lambda b, c: (rix(b, c), 0)),
                  pl.BlockSpec((1, CH), lambda b, c: (0, 0)),
                  pl.BlockSpec((1, SSD_H, SSD_P, SSD_N), lambda b, c: (rix(b, c), 0, 0, 0)),
                  pl.BlockSpec((CH, D), lambda b, c: (rix(b, c), 0)),
                  pl.BlockSpec((CH, D), lambda b, c: (rix(b, c), 0))],
        out_specs=[pl.BlockSpec((CH, CONV_DIM), lambda b, c: (rix(b, c), 0)),
                   pl.BlockSpec((CH, CH), lambda b, c: (rix(b, c), 0)),
                   pl.BlockSpec((1, CH), lambda b, c: (0, 0))],
        out_shape=[jax.ShapeDtypeStruct((T, CONV_DIM), F32),
                   jax.ShapeDtypeStruct((T, CH), F32),
                   jax.ShapeDtypeStruct((1, CH), F32)],
        scratch_shapes=[pltpu.VMEM((SSD_H, SSD_P, SSD_N), F32)],
        compiler_params=_cp(("arbitrary", "arbitrary")),
    )(xbc, sm, alog, states, dy, dxs_skip)


def _cumsum_seq(v, *, B, NC, reverse, name):
    T = B * NC * CH

    def ix(b, c):
        return b * NC + ((NC - 1 - c) if reverse else c)

    def body(v_ref, o_ref, carry):
        cidx = pl.program_id(1)

        @pl.when(cidx == 0)
        def _():
            carry[...] = jnp.zeros_like(carry)

        r = lax.broadcasted_iota(jnp.int32, (CH, CH), 0)
        c = lax.broadcasted_iota(jnp.int32, (CH, CH), 1)
        tri = ((r <= c) if reverse else (r >= c)).astype(F32)
        cs = jnp.dot(tri, v_ref[...], precision=HI, preferred_element_type=F32) + carry[...]
        o_ref[...] = cs
        edge = 0 if reverse else CH - 1
        carry[...] = jnp.sum(jnp.where(r == edge, cs, 0.0), axis=0, keepdims=True)

    return pl.pallas_call(
        body, name=name, grid=(B, NC),
        in_specs=[pl.BlockSpec((CH, CH), lambda b, c: (ix(b, c), 0))],
        out_specs=pl.BlockSpec((CH, CH), lambda b, c: (ix(b, c), 0)),
        out_shape=jax.ShapeDtypeStruct((T, CH), F32),
        scratch_shapes=[pltpu.VMEM((1, CH), F32)],
        compiler_params=_cp(("arbitrary", "arbitrary")),
    )(v)


def _fox_tb(Lp):
    return 384 if (Lp % 384 == 0 and Lp > 384) else CH


def _fox_keybias(cum, *, B, Lp, pad):
    ck = cum.reshape(B, Lp, CH)[:, :, SSD_H:SSD_H + FOX_H].transpose(0, 2, 1)
    pos = lax.broadcasted_iota(jnp.int32, ck.shape, 2)
    return jnp.where(pos < pad, -NEG, ck).reshape(B * FOX_H, 1, Lp)


def _fox_tril(TB):
    r = lax.broadcasted_iota(jnp.int32, (TB, TB), 0)
    c = lax.broadcasted_iota(jnp.int32, (TB, TB), 1)
    return r >= c


def _fox_fwd(qkv, cumT, *, B, Lp):
    TB = _fox_tb(Lp)
    NQ = Lp // TB
    T = B * Lp
    scale = FOX_DH ** -0.5

    def body(q_ref, k_ref, v_ref, ct_ref, o_ref, lse_ref):
        i = pl.program_id(2)
        q = q_ref[...]

        def block(j, nb, carry, diag):
            m, l, acc = carry
            off = pl.multiple_of(j * TB, TB)
            k = k_ref[pl.ds(off, nb * TB), :]
            v = v_ref[pl.ds(off, nb * TB), :]
            s = _nt(q, k) * scale - ct_ref[0, :, pl.ds(off, nb * TB)]
            if diag:
                s = jnp.where(_fox_tril(TB), s, NEG)
            m_new = jnp.maximum(m, jnp.max(s, axis=1, keepdims=True))
            p = jnp.exp(s - m_new)
            alpha = jnp.exp(m - m_new)
            l = alpha * l + jnp.sum(p, axis=1, keepdims=True)
            acc = alpha * acc + _nn(p.astype(BF16), v)
            return m_new, l, acc

        init = (jnp.full((TB, 1), NEG, F32), jnp.zeros((TB, 1), F32), jnp.zeros((TB, FOX_DH), F32))
        carry = lax.fori_loop(0, i // 2, lambda t, c: block(2 * t, 2, c, False), init)
        carry = lax.fori_loop(2 * (i // 2), i, lambda j, c: block(j, 1, c, False), carry)
        m, l, acc = block(i, 1, carry, True)
        o_ref[...] = (acc / l).astype(o_ref.dtype)
        lse_ref[0, 0] = m + jnp.log(l)

    return pl.pallas_call(
        body, name="fox_fwd", grid=(B, FOX_H, NQ),
        in_specs=[pl.BlockSpec((TB, FOX_DH), lambda b, h, i: (b * NQ + i, h)),
                  pl.BlockSpec((Lp, FOX_DH), lambda b, h, i: (b, FOX_H + h)),
                  pl.BlockSpec((Lp, FOX_DH), lambda b, h, i: (b, 2 * FOX_H + h)),
                  pl.BlockSpec((1, 1, Lp), lambda b, h, i: (b * FOX_H + h, 0, 0))],
        out_specs=[pl.BlockSpec((TB, FOX_DH), lambda b, h, i: (b * NQ + i, h)),
                   pl.BlockSpec((1, 1, TB, 1), lambda b, h, i: (b, h, i, 0))],
        out_shape=[jax.ShapeDtypeStruct((T, D), BF16),
                   jax.ShapeDtypeStruct((B, FOX_H, Lp, 1), F32)],
        compiler_params=_cp(("arbitrary", "arbitrary", "arbitrary")),
    )(qkv, qkv, qkv, cumT)


def _fox_bwd(qkv, dy, o, lse, cumT, *, B, Lp):
    TB = _fox_tb(Lp)
    NQ = Lp // TB
    T = B * Lp
    scale = FOX_DH ** -0.5

    def body(q_ref, k_ref, v_ref, dy_ref, o_ref, lse_ref, ct_ref, dq_ref, dk_ref, dv_ref, dck_ref, dcq_ref, dl_s):
        j = pl.program_id(2)
        k = k_ref[...]
        v = v_ref[...]
        ck = ct_ref[0]

        @pl.when(j == 0)
        def _():
            dq_ref[...] = jnp.zeros_like(dq_ref)
            dcq_ref[...] = jnp.zeros_like(dcq_ref)
            for i in range(NQ):
                sl = slice(i * TB, (i + 1) * TB)
                dl_s[sl, :] = jnp.sum(dy_ref[sl, :].astype(F32) * o_ref[sl, :].astype(F32), axis=1, keepdims=True)

        def block(i, nb, carry, diag):
            dk, dv, dck = carry
            off = pl.multiple_of(i * TB, TB)
            rows = pl.ds(off, nb * TB)
            q = q_ref[rows, :]
            dob = dy_ref[rows, :].astype(BF16)
            e = _nt(q, k) * scale - ck - lse_ref[0, 0, rows, :]
            if diag:
                e = jnp.where(_fox_tril(TB), e, NEG)
            p = jnp.exp(e)
            dv = dv + _tn(p.astype(BF16), dob)
            ds = p * (_nt(dob, v) - dl_s[rows, :])
            dsb = ds.astype(BF16)
            dk = dk + _tn(dsb, q)
            dq_ref[rows, :] += _nn(dsb, k) * scale
            dcq_ref[0, 0, rows, :] += jnp.sum(ds, axis=1, keepdims=True)
            dck = dck - jnp.sum(ds, axis=0, keepdims=True)
            return dk, dv, dck

        z = jnp.zeros((TB, FOX_DH), F32)
        carry = block(j, 1, (z, z, jnp.zeros((1, TB), F32)), True)
        npair = (NQ - 1 - j) // 2
        carry = lax.fori_loop(0, npair, lambda t, c: block(j + 1 + 2 * t, 2, c, False), carry)
        dk, dv, dck = lax.fori_loop(j + 1 + 2 * npair, NQ, lambda i, c: block(i, 1, c, False), carry)
        dk_ref[...] = (dk * scale).astype(dk_ref.dtype)
        dv_ref[...] = dv.astype(dv_ref.dtype)
        dck_ref[0] = dck

    head = lambda b, h, j: (b, h)
    return pl.pallas_call(
        body, name="fox_bwd", grid=(B, FOX_H, NQ),
        in_specs=[pl.BlockSpec((Lp, FOX_DH), head),
                  pl.BlockSpec((TB, FOX_DH), lambda b, h, j: (b * NQ + j, FOX_H + h)),
                  pl.BlockSpec((TB, FOX_DH), lambda b, h, j: (b * NQ + j, 2 * FOX_H + h)),
                  pl.BlockSpec((Lp, FOX_DH), head),
                  pl.BlockSpec((Lp, FOX_DH), head),
                  pl.BlockSpec((1, 1, Lp, 1), lambda b, h, j: (b, h, 0, 0)),
                  pl.BlockSpec((1, 1, TB), lambda b, h, j: (b * FOX_H + h, 0, j))],
        out_specs=[pl.BlockSpec((Lp, FOX_DH), head),
                   pl.BlockSpec((TB, FOX_DH), lambda b, h, j: (b * NQ + j, h)),
                   pl.BlockSpec((TB, FOX_DH), lambda b, h, j: (b * NQ + j, h)),
                   pl.BlockSpec((1, 1, TB), lambda b, h, j: (b * FOX_H + h, 0, j)),
                   pl.BlockSpec((1, 1, Lp, 1), lambda b, h, j: (b, h, 0, 0))],
        out_shape=[jax.ShapeDtypeStruct((T, D), F32),
                   jax.ShapeDtypeStruct((T, D), BF16),
                   jax.ShapeDtypeStruct((T, D), BF16),
                   jax.ShapeDtypeStruct((B * FOX_H, 1, Lp), F32),
                   jax.ShapeDtypeStruct((B, FOX_H, Lp, 1), F32)],
        scratch_shapes=[pltpu.VMEM((Lp, 1), F32)],
        compiler_params=_cp(("arbitrary", "arbitrary", "arbitrary")),
    )(qkv, qkv, qkv, dy, o, lse, cumT)


S5_TILE = 8


def _s5_pows(lam_ref, pw, tab, reverse):
    lr = lam_ref[0, :, 0:S5_W]
    li = lam_ref[0, :, S5_W:2 * S5_W]
    if reverse:
        li = -li
    ar, ai = lr, li
    sub = lax.broadcasted_iota(jnp.int32, (S5_TILE, 1), 0)
    for k, s in enumerate((1, 2, 4)):
        keep = (sub < S5_TILE - s) if reverse else (sub >= s)
        pw[k * S5_TILE:(k + 1) * S5_TILE, 0:S5_W] = jnp.where(keep, ar, 0.0)
        pw[k * S5_TILE:(k + 1) * S5_TILE, S5_W:2 * S5_W] = jnp.where(keep, ai, 0.0)
        ar, ai = ar * ar - ai * ai, 2.0 * ar * ai
    ar, ai = lr, li
    for r in range(S5_TILE):
        row = (S5_TILE - 1 - r) if reverse else r
        tab[row:row + 1, 0:S5_W] = ar
        tab[row:row + 1, S5_W:2 * S5_W] = ai
        ar, ai = ar * lr - ai * li, ar * li + ai * lr


def _s5_scan(hs, pw, tab, carry, reverse):
    n = hs.shape[0]
    tiles = list(range(n // S5_TILE))
    if reverse:
        tiles.reverse()
    for t in tiles:
        lo = t * S5_TILE
        vr = hs[lo:lo + S5_TILE, 0:S5_W]
        vi = hs[lo:lo + S5_TILE, S5_W:2 * S5_W]
        for k, s in enumerate((1, 2, 4)):
            sh = (S5_TILE - s) if reverse else s
            sr = pltpu.roll(vr, sh, 0)
            si = pltpu.roll(vi, sh, 0)
            ar = pw[k * S5_TILE:(k + 1) * S5_TILE, 0:S5_W]
            ai = pw[k * S5_TILE:(k + 1) * S5_TILE, S5_W:2 * S5_W]
            vr, vi = vr + ar * sr - ai * si, vi + ar * si + ai * sr
        hs[lo:lo + S5_TILE, 0:S5_W] = vr
        hs[lo:lo + S5_TILE, S5_W:2 * S5_W] = vi
    e8 = 0 if reverse else S5_TILE - 1
    l8r = tab[e8:e8 + 1, 0:S5_W]
    l8i = tab[e8:e8 + 1, S5_W:2 * S5_W]
    cr = carry[:, 0:S5_W]
    ci = carry[:, S5_W:2 * S5_W]
    states = []
    for t in tiles:
        states.append((cr, ci))
        edge = t * S5_TILE + e8
        er = hs[edge:edge + 1, 0:S5_W]
        ei = hs[edge:edge + 1, S5_W:2 * S5_W]
        cr, ci = er + l8r * cr - l8i * ci, ei + l8r * ci + l8i * cr
    carry[:, 0:S5_W] = cr
    carry[:, S5_W:2 * S5_W] = ci
    tr = tab[:, 0:S5_W]
    ti = tab[:, S5_W:2 * S5_W]
    for t, (cr, ci) in zip(tiles, states):
        lo = t * S5_TILE
        hs[lo:lo + S5_TILE, 0:S5_W] += tr * cr - ti * ci
        hs[lo:lo + S5_TILE, S5_W:2 * S5_W] += tr * ci + ti * cr


def _s5_fwd(u, Bsg, Csg, lam, *, B, NC):
    T = B * NC * CH

    def body(u_ref, b_ref, c_ref, lam_ref, y_ref, h_ref, pw, tab, hs, carry):
        cidx = pl.program_id(2)

        @pl.when(cidx == 0)
        def _():
            _s5_pows(lam_ref, pw, tab, False)
            carry[...] = jnp.zeros_like(carry)

        hs[...] = _nn(u_ref[...].astype(BF16), b_ref[0])
        _s5_scan(hs, pw, tab, carry, False)
        hb = hs[...].astype(BF16)
        h_ref[...] = hb
        y_ref[...] = _nn(hb, c_ref[0])

    return pl.pallas_call(
        body, name="s5_fwd", grid=(B, S5_G // S5_SG, NC),
        in_specs=[pl.BlockSpec((CH, CH), lambda b, s, c: (b * NC + c, s)),
                  pl.BlockSpec((1, CH, 2 * S5_W), lambda b, s, c: (s, 0, 0)),
                  pl.BlockSpec((1, 2 * S5_W, CH), lambda b, s, c: (s, 0, 0)),
                  pl.BlockSpec((1, 1, 2 * S5_W), lambda b, s, c: (s, 0, 0))],
        out_specs=[pl.BlockSpec((CH, CH), lambda b, s, c: (b * NC + c, s)),
                   pl.BlockSpec((CH, 2 * S5_W), lambda b, s, c: (b * NC + c, s))],
        out_shape=[jax.ShapeDtypeStruct((T, D), F32),
                   jax.ShapeDtypeStruct((T, (S5_G // S5_SG) * 2 * S5_W), BF16)],
        scratch_shapes=[pltpu.VMEM((3 * S5_TILE, 2 * S5_W), F32), pltpu.VMEM((S5_TILE, 2 * S5_W), F32),
                        pltpu.VMEM((CH, 2 * S5_W), F32), pltpu.VMEM((1, 2 * S5_W), F32)],
        compiler_params=_cp(("arbitrary", "arbitrary", "arbitrary")),
    )(u, Bsg, Csg, lam)


def _s5_bwd(u, hst, dy, du_skip, Bsg, Csg, lam, *, B, NC):
    T = B * NC * CH
    NS = S5_G // S5_SG
    hb16 = CH // 16

    def rix(b, c):
        return b * NC + (NC - 1 - c)

    def body(u_ref, h_ref, hp_ref, dy_ref, sk_ref, b_ref, c_ref, lam_ref,
             du_ref, db_ref, dc_ref, dl_ref, pw, tab, gs, carry):
        bidx = pl.program_id(1)
        cidx = pl.program_id(2)
        first = (bidx == 0) & (cidx == 0)

        @pl.when(cidx == 0)
        def _():
            _s5_pows(lam_ref, pw, tab, True)
            carry[...] = jnp.zeros_like(carry)

        dyb = dy_ref[...].astype(BF16)
        gs[...] = _nt(dyb, c_ref[0])
        _s5_scan(gs, pw, tab, carry, True)
        gr = gs[:, 0:S5_W]
        gi = gs[:, S5_W:]
        gb = gs[...].astype(BF16)
        du_ref[...] = (_nt(gb, b_ref[0]) + sk_ref[...]).astype(du_ref.dtype)
        ub = u_ref[...].astype(BF16)
        hcur = h_ref[...]
        dB = _tn(ub, gb)
        dC = _tn(dyb, hcur)
        hf = hcur.astype(F32)
        row = lax.broadcasted_iota(jnp.int32, (CH, 1), 0)
        prev_last = jnp.where(cidx < NC - 1, hp_ref[15:16, :].astype(F32), 0.0)
        hprev = jnp.where(row == 0, prev_last, pltpu.roll(hf, 1, 0))
        pr = hprev[:, 0:S5_W]
        pi = hprev[:, S5_W:]
        da = _colsum(gr * pr + gi * pi)
        dbb = _colsum(gi * pr - gr * pi)
        dl = jnp.concatenate([da, dbb], axis=1)

        @pl.when(first)
        def _():
            db_ref[0] = dB
            dc_ref[0] = dC
            dl_ref[0] = dl

        @pl.when(jnp.logical_not(first))
        def _():
            db_ref[0] += dB
            dc_ref[0] += dC
            dl_ref[0] += dl

    return pl.pallas_call(
        body, name="s5_bwd", grid=(NS, B, NC),
        in_specs=[pl.BlockSpec((CH, CH), lambda s, b, c: (rix(b, c), s)),
                  pl.BlockSpec((CH, 2 * S5_W), lambda s, b, c: (rix(b, c), s)),
                  pl.BlockSpec((16, 2 * S5_W), lambda s, b, c: (jnp.maximum(rix(b, c) * hb16 - 1, 0), s)),
                  pl.BlockSpec((CH, CH), lambda s, b, c: (rix(b, c), s)),
                  pl.BlockSpec((CH, CH), lambda s, b, c: (rix(b, c), s)),
                  pl.BlockSpec((1, CH, 2 * S5_W), lambda s, b, c: (s, 0, 0)),
                  pl.BlockSpec((1, 2 * S5_W, CH), lambda s, b, c: (s, 0, 0)),
                  pl.BlockSpec((1, 1, 2 * S5_W), lambda s, b, c: (s, 0, 0))],
        out_specs=[pl.BlockSpec((CH, CH), lambda s, b, c: (rix(b, c), s)),
                   pl.BlockSpec((1, CH, 2 * S5_W), lambda s, b, c: (s, 0, 0)),
                   pl.BlockSpec((1, CH, 2 * S5_W), lambda s, b, c: (s, 0, 0)),
                   pl.BlockSpec((1, 1, 2 * S5_W), lambda s, b, c: (s, 0, 0))],
        out_shape=[jax.ShapeDtypeStruct((T, D), BF16),
                   jax.ShapeDtypeStruct((NS, CH, 2 * S5_W), F32),
                   jax.ShapeDtypeStruct((NS, CH, 2 * S5_W), F32),
                   jax.ShapeDtypeStruct((NS, 1, 2 * S5_W), F32)],
        scratch_shapes=[pltpu.VMEM((3 * S5_TILE, 2 * S5_W), F32), pltpu.VMEM((S5_TILE, 2 * S5_W), F32),
                        pltpu.VMEM((CH, 2 * S5_W), F32), pltpu.VMEM((1, 2 * S5_W), F32)],
        compiler_params=_cp(("arbitrary", "arbitrary", "arbitrary")),
    )(u, hst, hst, dy, du_skip, Bsg, Csg, lam)


def _s5_param_fn(lre, lim, lstep, bre, bim):
    step = jnp.exp(lstep)
    zr = lre * step
    zi = lim * step
    e = jnp.exp(zr)
    a = e * jnp.cos(zi)
    b = e * jnp.sin(zi)
    den = lre * lre + lim * lim
    qr = ((a - 1.0) * lre + b * lim) / den
    qi = (b * lre - (a - 1.0) * lim) / den
    return a, b, qr[None] * bre - qi[None] * bim, qr[None] * bim + qi[None] * bre


_S5_ROWS = S5_G * S5_P // CH


def _s5_tile(v):
    return v.reshape(_S5_ROWS, CH)


def _s5_tile_b(v):
    return v.reshape(S5_G * S5_P, S5_C).T.reshape(S5_C, _S5_ROWS, CH)


def _s5_untile_b(v):
    return v.reshape(S5_C, S5_G * S5_P).T.reshape(S5_G, S5_P, S5_C)


def _s5_params(lre, lim, lstep, bre, bim):
    def body(a_ref, b_ref, c_ref, d_ref, e_ref, o1, o2, o3, o4):
        outs = _s5_param_fn(a_ref[...], b_ref[...], c_ref[...], d_ref[...], e_ref[...])
        for o, v in zip((o1, o2, o3, o4), outs):
            o[...] = v

    shp = [jax.ShapeDtypeStruct(lre.shape, F32)] * 2 + [jax.ShapeDtypeStruct(bre.shape, F32)] * 2
    return pl.pallas_call(body, name="s5_params", out_shape=shp, compiler_params=_cp())(lre, lim, lstep, bre, bim)


def _s5_params_bwd(lre, lim, lstep, bre, bim, da, db, dbr, dbi):
    def body(a_ref, b_ref, c_ref, d_ref, e_ref, g1, g2, g3, g4, o1, o2, o3, o4, o5):
        _, vjp = jax.vjp(_s5_param_fn, a_ref[...], b_ref[...], c_ref[...], d_ref[...], e_ref[...])
        outs = vjp((g1[...], g2[...], g3[...], g4[...]))
        for o, v in zip((o1, o2, o3, o4, o5), outs):
            o[...] = v

    shp = [jax.ShapeDtypeStruct(lre.shape, F32)] * 3 + [jax.ShapeDtypeStruct(bre.shape, F32)] * 2
    return pl.pallas_call(body, name="s5_params_bwd", out_shape=shp, compiler_params=_cp())(
        lre, lim, lstep, bre, bim, da, db, dbr, dbi)


def _s5_blockdiag(br, bi, cre, cim):
    NS = S5_G // S5_SG
    eye = jnp.eye(S5_SG, dtype=F32)

    def bmat(v):
        v = v.reshape(NS, S5_SG, S5_P, S5_C)
        m = jnp.einsum("sgpc,gh->sgchp", v, eye)
        return m.reshape(NS, S5_SG * S5_C, S5_SG * S5_P)

    def cmat(v):
        v = v.reshape(NS, S5_SG, S5_C, S5_P)
        m = jnp.einsum("sgcp,gh->sgphc", v, eye)
        return m.reshape(NS, S5_SG * S5_P, S5_SG * S5_C)

    Bsg = jnp.concatenate([bmat(br), bmat(bi)], axis=2).astype(BF16)
    Csg = jnp.concatenate([cmat(cre), cmat(-cim)], axis=1).astype(BF16)
    return Bsg, Csg


def _s5_unblock(dBsg, dCsgT):
    NS = S5_G // S5_SG

    def diag(m):
        m = m.reshape(NS, S5_SG, S5_C, S5_SG, S5_P)
        return jnp.stack([m[:, g, :, g, :] for g in range(S5_SG)], axis=1)

    def ub(m):
        return diag(m).transpose(0, 1, 3, 2).reshape(S5_G, S5_P, S5_C)

    def uc(m):
        return diag(m).reshape(S5_G, S5_C, S5_P)

    dbr = ub(dBsg[:, :, 0:S5_W])
    dbi = ub(dBsg[:, :, S5_W:])
    dcr = uc(dCsgT[:, :, 0:S5_W])
    dci = -uc(dCsgT[:, :, S5_W:])
    return dbr, dbi, dcr, dci


def _loss_head(x, nf, target, *, B, NC, S):
    T = B * NC * CH
    nts = S // CH

    def f(xv, w, t):
        y = _rms(xv, w)
        return 0.5 * _colsum(jnp.mean(jnp.square(y - t), axis=-1, keepdims=True))

    def body(x_ref, w_ref, t_ref, dx_ref, ls_ref, dw_ref):
        i = pl.program_id(0)
        on = (i % NC) > 0
        t = t_ref[...]
        l, vjp = jax.vjp(lambda a, b: f(a, b, t), x_ref[...], w_ref[...])
        dx, dw = vjp(jnp.ones((1, 1), F32))
        g = jnp.where(on, 1.0, 0.0)
        dx_ref[...] = dx * g
        lv = jnp.zeros((1, CH), F32) + l * g

        @pl.when(i == 0)
        def _():
            ls_ref[...] = lv
            dw_ref[...] = dw * g

        @pl.when(i > 0)
        def _():
            ls_ref[...] += lv
            dw_ref[...] += dw * g

    def tix(i):
        return ((i // NC) * nts + jnp.maximum(i % NC - 1, 0), 0)

    return pl.pallas_call(
        body, name="loss_head", grid=(B * NC,),
        in_specs=[pl.BlockSpec((CH, D), lambda i: (i, 0)),
                  pl.BlockSpec((1, D), lambda i: (0, 0)),
                  pl.BlockSpec((CH, D), tix)],
        out_specs=[pl.BlockSpec((CH, D), lambda i: (i, 0)),
                   pl.BlockSpec((1, CH), lambda i: (0, 0)),
                   pl.BlockSpec((1, D), lambda i: (0, 0))],
        out_shape=[jax.ShapeDtypeStruct((T, D), F32),
                   jax.ShapeDtypeStruct((1, CH), F32),
                   jax.ShapeDtypeStruct((1, D), F32)],
        compiler_params=_cp(("arbitrary",)),
    )(x, nf, target)


def _ew(fn, ins, n_out, out_dtypes, *, name, tile=None):
    R, C = ins[0].shape
    tile = tile or _pick(R, (512, 256, 128, 64, 32, 16, 8, 1))
    if tile % 8 != 0:
        tile = R

    def body(*refs):
        outs = fn(*[r[...] for r in refs[:len(ins)]])
        if not isinstance(outs, (tuple, list)):
            outs = (outs,)
        for r, v in zip(refs[len(ins):], outs):
            r[...] = v.astype(r.dtype)

    spec = pl.BlockSpec((tile, C), lambda i: (i, 0))
    res = pl.pallas_call(
        body, name=name, grid=(R // tile,), in_specs=[spec] * len(ins), out_specs=[spec] * n_out,
        out_shape=[jax.ShapeDtypeStruct((R, C), dt) for dt in out_dtypes],
        compiler_params=_cp(("parallel",)),
    )(*ins)
    return res


def _adam_fn(w, g, m, v):
    m = ADAM_B1 * m + (1.0 - ADAM_B1) * g
    v = ADAM_B2 * v + (1.0 - ADAM_B2) * jnp.square(g)
    m_hat = m / (1.0 - ADAM_B1 ** ADAM_STEP)
    v_hat = v / (1.0 - ADAM_B2 ** ADAM_STEP)
    delta = -ADAM_LR * (m_hat / (jnp.sqrt(v_hat) + ADAM_EPS) + ADAM_WD * w)
    return delta, m, v


def _adam(w, g, m, v, name):
    shp = w.shape
    C = shp[-1]
    f = lambda a: a.reshape(-1, C)
    d, nm, nv = _ew(_adam_fn, [f(w), f(g), f(m), f(v)], 3, [F32] * 3, name=name)
    return d.reshape(shp), nm.reshape(shp), nv.reshape(shp)


def _me():
    return lax.axis_index("x"), lax.axis_index("y"), lax.axis_index("c")


def _all_gather(v, name):
    def body(x_ref, out_ref, send_sems, recv_sems, local_sem):
        x, y, c = _me()
        me, sibling = (x, y, c), (x, y, 1 - c)
        chips = [(1 - x, y), (x, 1 - y), (1 - x, 1 - y)]

        def slot(px, py, pc):
            return out_ref.at[4 * px + 2 * py + pc]

        def copy(k, block, to, src=None):
            return pltpu.make_async_remote_copy(
                src_ref=slot(*block) if src is None else src, dst_ref=slot(*block),
                send_sem=send_sems.at[k], recv_sem=recv_sems.at[k], device_id=to, device_id_type=MESH)

        mine = pltpu.make_async_copy(x_ref, slot(*me), local_sem)
        mine.start()
        first = [copy(0, me, sibling, src=x_ref)]
        first += [copy(1 + j, me, (*chip, c), src=x_ref) for j, chip in enumerate(chips)]
        for cp in first:
            cp.start()
        passed = [copy(4 + j, (*chip, c), sibling) for j, chip in enumerate(chips)]
        for j, chip in enumerate(chips):
            copy(1 + j, (*chip, c), me).wait_recv()
            passed[j].start()
        copy(0, sibling, me).wait_recv()
        for j, chip in enumerate(chips):
            copy(4 + j, (*chip, 1 - c), me).wait_recv()
        for cp in first + passed:
            cp.wait_send()
        mine.wait()

    return pl.pallas_call(
        body, name=name, out_shape=jax.ShapeDtypeStruct((N_DEV,) + v.shape, v.dtype),
        in_specs=[pl.BlockSpec(memory_space=pl.ANY)], out_specs=pl.BlockSpec(memory_space=pl.ANY),
        scratch_shapes=[pltpu.SemaphoreType.DMA((7,)), pltpu.SemaphoreType.DMA((7,)), pltpu.SemaphoreType.DMA],
    )(v)


def _swap_core(g, name):
    def body(g_ref, out_ref, send_sems, recv_sems):
        x, y, c = _me()
        cps = [pltpu.make_async_remote_copy(
            src_ref=g_ref.at[q, 1 - c], dst_ref=out_ref.at[q], send_sem=send_sems.at[q], recv_sem=recv_sems.at[q],
            device_id=(x, y, 1 - c), device_id_type=MESH) for q in range(4)]
        for cp in cps:
            cp.start()
        for cp in cps:
            cp.wait()

    return pl.pallas_call(
        body, name=name, out_shape=jax.ShapeDtypeStruct((4,) + g.shape[2:], g.dtype),
        in_specs=[pl.BlockSpec(memory_space=pl.ANY)], out_specs=pl.BlockSpec(memory_space=pl.ANY),
        scratch_shapes=[pltpu.SemaphoreType.DMA((4,)), pltpu.SemaphoreType.DMA((4,))],
    )(g)


def _swap_chips(hb, name):
    flips = [(1, 0), (0, 1), (1, 1)]

    def body(h_ref, out_ref, send_sems, recv_sems):
        x, y, c = _me()
        cps = []
        for j, (fx, fy) in enumerate(flips):
            px = x + fx - 2 * x * fx
            py = y + fy - 2 * y * fy
            cps.append(pltpu.make_async_remote_copy(
                src_ref=h_ref.at[2 * px + py], dst_ref=out_ref.at[j], send_sem=send_sems.at[j],
                recv_sem=recv_sems.at[j], device_id=(px, py, c), device_id_type=MESH))
        for cp in cps:
            cp.start()
        for cp in cps:
            cp.wait()

    return pl.pallas_call(
        body, name=name, out_shape=jax.ShapeDtypeStruct((3,) + hb.shape[1:], hb.dtype),
        in_specs=[pl.BlockSpec(memory_space=pl.ANY)], out_specs=pl.BlockSpec(memory_space=pl.ANY),
        scratch_shapes=[pltpu.SemaphoreType.DMA((3,)), pltpu.SemaphoreType.DMA((3,))],
    )(hb)


def _reduce_scatter(g8, tag):
    shard = g8.shape[1:]
    C = shard[-1]
    x, y, c = _me()
    g4 = g8.reshape((4, 2) + shard)
    got = _swap_core(g4, "rs_core_" + tag)
    mine = lax.dynamic_index_in_dim(g4, c, axis=1, keepdims=False)
    h, hb = _ew(lambda a, b: (a + b, a + b), [mine.reshape(-1, C), got.reshape(-1, C)], 2, [F32, BF16],
                name="rs_pair_sum_" + tag)
    own = lax.dynamic_index_in_dim(h.reshape((4,) + shard), 2 * x + y, axis=0, keepdims=False)
    got3 = _swap_chips(hb.reshape((4,) + shard), "rs_chips_" + tag)
    out, = _ew(lambda a, b, c_, d: a + b.astype(F32) + c_.astype(F32) + d.astype(F32),
               [own.reshape(-1, C)] + [got3[j].reshape(-1, C) for j in range(3)], 1, [F32],
               name="rs_chip_sum_" + tag)
    return out.reshape(shard)


def _sum8(a):
    out, = _ew(lambda *v: functools.reduce(lambda p, q: p + q, v), [a[k] for k in range(N_DEV)], 1, [F32],
               name="sum8")
    return out


def _pad_rows(flat, cols, mult):
    n = flat.shape[0]
    per = cols * mult
    tot = ((n + per - 1) // per) * per
    return jnp.pad(flat, (0, tot - n)).reshape(-1, cols)


class _Packer:
    def __init__(self, shapes, mult):
        self.shapes = shapes
        self.sizes = [int(np.prod(s)) for s in shapes]
        self.rows = [8 * ((n + 8 * D - 1) // (8 * D)) for n in self.sizes]
        tot = sum(self.rows)
        self.tail = (-tot) % mult

    def pack(self, arrs, dtype):
        parts = [_pad_rows(a.reshape(-1).astype(dtype), D, 8) for a in arrs]
        if self.tail:
            parts.append(jnp.zeros((self.tail, D), dtype))
        return jnp.concatenate(parts, axis=0)

    def unpack(self, buf):
        out, o = [], 0
        for s, n, r in zip(self.shapes, self.sizes, self.rows):
            out.append(buf[o:o + r].reshape(-1)[:n].reshape(s))
            o += r
        return out


def _w_cat(w_in_l):
    sm = jnp.concatenate([w_in_l[:, O_DT:O_DT + SSD_H], w_in_l[:, O_F:O_F + FOX_H],
                          jnp.zeros((D, CH - SSD_H - FOX_H), w_in_l.dtype)], axis=1)
    return jnp.concatenate([w_in_l[:, O_Z:O_XBC], w_in_l[:, O_XBC:O_DT], w_in_l[:, O_QKV:O_F],
                            w_in_l[:, O_U:O_G], w_in_l[:, O_G:D_IN], sm], axis=1)


def _w_uncat(g):
    return jnp.concatenate([g[:, OFF_Z:OFF_XBC], g[:, OFF_XBC:OFF_QKV], g[:, OFF_SM:OFF_SM + SSD_H],
                            g[:, OFF_QKV:OFF_U], g[:, OFF_SM + SSD_H:OFF_SM + SSD_H + FOX_H],
                            g[:, OFF_U:OFF_G], g[:, OFF_G:OFF_SM]], axis=1)


def _layer_fwd(x, p, geom, dims):
    B, NC, Lp, pad = dims
    T = geom[0]
    rm = functools.partial(_rowmap, geom=geom)
    sv = {}
    xn1, = rm(_f_norm, [x], [p["norm1"]], [(D, BF16)], [], tile=384 if Lp % 384 == 0 else CH, name="norm1")
    Wc = p["w_cat"]
    pz = _mm(xn1, Wc, "nn", BF16, n=D, b_off=OFF_Z, name="in_z")
    pxbc = _mm(xn1, Wc, "nn", F32, n=CONV_DIM, b_off=OFF_XBC, name="in_xbc")
    qkv = _mm(xn1, Wc, "nn", BF16, n=3 * D, b_off=OFF_QKV, name="in_qkv")
    pu = _mm(xn1, Wc, "nn", BF16, n=D, b_off=OFF_U, name="in_u")
    pg = _mm(xn1, Wc, "nn", BF16, n=3 * D, b_off=OFF_G, name="in_g")
    psm = _mm(xn1, Wc, "nn", F32, n=CH, b_off=OFF_SM, name="in_sm")
    t_r = 384 if Lp % 384 == 0 else CH
    sm, = rm(_smallact, [psm], [p["smallbias"]], [(CH, F32)], [], tile=t_r, name="smallact")
    xbc = _conv_fwd(pxbc, p["conv_w"], p["conv_b"], geom=geom, tile=CH)
    y_ssd, states = _ssd_fwd(xbc, sm, p["a_log"], B=B, NC=NC)
    y_a, = rm(_ssd_post, [y_ssd, (xbc, D, 0), pz], [p["d_rep"], p["ssd_norm"]], [(D, BF16)], [], tile=t_r,
              name="ssd_post")
    cum = _cumsum_seq(sm, B=B, NC=NC, reverse=False, name="fox_cum")
    cumT = _fox_keybias(cum, B=B, Lp=Lp, pad=pad)
    y_b, lse = _fox_fwd(qkv, cumT, B=B, Lp=Lp)
    y_ssm, hst = _s5_fwd(pu, p["Bsg"], p["Csg"], p["lam"], B=B, NC=NC)
    y1, = rm(_s5_pre, [y_ssm, pu], [p["s5_d"]], [(D, BF16)], [], tile=t_r, name="s5_pre")
    tg = _mm(y1, p["w_glu"], "nn", BF16, name="s5_glu_mm")
    y_c, = rm(_s5_glu, [y_ssm, pu, tg], [p["s5_d"]], [(D, BF16)], [], tile=t_r, name="s5_glu")
    br = [_mm(yy, p["w_branch"][n], "nn", BF16, name=f"branch{n}") for n, yy in enumerate((y_a, y_b, y_c))]
    mix, = rm(_merge, [(pg, D, 0), (pg, D, 1), (pg, D, 2)] + br, [], [(D, BF16)], [], tile=CH, name="merge")
    x_mid = _mm(mix, p["w_out"], "nn", F32, res=x, name="out_proj")
    xn2, = rm(_f_norm, [x_mid], [p["norm2"]], [(D, BF16)], [], tile=t_r, name="norm2")
    hff = _mm(xn2, p["w_ffn_in"], "nn", BF16, name="ffn_in")
    act, = rm(_swiglu, [(hff, DFF, 0), (hff, DFF, 1)], [], [(DFF, BF16)], [], tile=CH, name="swiglu")
    x_out = _mm(act, p["w_ffn_out"], "nn", F32, res=x_mid, name="ffn_out")
    sv.update(x=x, xn1=xn1, pz=pz, pxbc=pxbc, qkv=qkv, pu=pu, pg=pg, psm=psm, sm=sm, xbc=xbc, y_ssd=y_ssd,
              states=states, y_a=y_a, cum=cum, cumT=cumT, y_b=y_b, lse=lse, y_ssm=y_ssm, hst=hst, y1=y1, tg=tg,
              y_c=y_c, br=br, mix=mix, x_mid=x_mid, xn2=xn2, hff=hff, act=act)
    return x_out, sv


def _layer_bwd(dx_out, p, sv, geom, dims):
    B, NC, Lp, pad = dims
    T = geom[0]
    rm = functools.partial(_rowmap, geom=geom)
    t_r = 384 if Lp % 384 == 0 else CH
    g = {}
    dact = _mm(dx_out, p["w_ffn_out"], "nt", BF16, name="ffn_out_dx")
    g["w_ffn_out"] = _mm(sv["act"], dx_out, "tn", F32, name="ffn_out_dw")
    dhff, = rm(_b_swiglu, [(sv["hff"], DFF, 0), (sv["hff"], DFF, 1), dact], [], [(2 * DFF, BF16)], [], tile=CH,
               name="swiglu_bwd")
    dxn2 = _mm(dhff, p["w_ffn_in"], "nt", F32, name="ffn_in_dx")
    g["w_ffn_in"] = _mm(sv["xn2"], dhff, "tn", F32, name="ffn_in_dw")
    dx_mid, g["norm2"] = rm(_b_norm, [sv["x_mid"], dxn2, dx_out], [p["norm2"]], [(D, F32)], [(1, D)], tile=t_r,
                            name="norm2_bwd")
    dmix = _mm(dx_mid, p["w_out"], "nt", BF16, name="out_proj_dx")
    g["w_out"] = _mm(sv["mix"], dx_mid, "tn", F32, name="out_proj_dw")
    pg = sv["pg"]
    dpg, db0, db1, db2 = rm(_b_merge, [(pg, D, 0), (pg, D, 1), (pg, D, 2)] + sv["br"] + [dmix], [],
                            [(3 * D, BF16), (D, BF16), (D, BF16), (D, BF16)], [], tile=CH, name="merge_bwd")
    ys = (sv["y_a"], sv["y_b"], sv["y_c"])
    dbs = (db0, db1, db2)
    g["w_branch"] = [_mm(ys[n], dbs[n], "tn", F32, name=f"branch{n}_dw") for n in range(3)]
    dy = [_mm(dbs[n], p["w_branch"][n], "nt", BF16, name=f"branch{n}_dx") for n in range(3)]
    dtg, dy1a = rm(_b_s5_glu, [sv["y_ssm"], sv["pu"], sv["tg"], dy[2]], [p["s5_d"]], [(D, BF16), (D, F32)], [],
                   tile=t_r, name="s5_glu_bwd")
    dy1b = _mm(dtg, p["w_glu"], "nt", BF16, name="s5_glu_mm_dx")
    g["w_glu"] = _mm(sv["y1"], dtg, "tn", F32, name="s5_glu_mm_dw")
    dys, du_skip, g["s5_d"] = rm(_b_s5_pre, [sv["y_ssm"], sv["pu"], dy1a, dy1b], [p["s5_d"]],
                                 [(D, F32), (D, F32)], [(1, D)], tile=t_r, name="s5_pre_bwd")
    du, g["Bsg"], g["Csg"], g["lam"] = _s5_bwd(sv["pu"], sv["hst"], dys, du_skip, p["Bsg"], p["Csg"], p["lam"],
                                               B=B, NC=NC)
    dq, dk, dv, dckT, dcq = _fox_bwd(sv["qkv"], dy[1], sv["y_b"], sv["lse"], sv["cumT"], B=B, Lp=Lp)
    dcum8 = dcq.reshape(B, FOX_H, Lp).transpose(0, 2, 1) + dckT.reshape(B, FOX_H, Lp).transpose(0, 2, 1)
    dcum = jnp.pad(dcum8.reshape(T, FOX_H), ((0, 0), (SSD_H, CH - SSD_H - FOX_H)))
    dlogf = _cumsum_seq(dcum, B=B, NC=NC, reverse=True, name="fox_cum_bwd")
    dy_ssd, dxs_skip, dz, g["d_rep"], g["ssd_norm"] = rm(
        _b_ssd_post, [sv["y_ssd"], (sv["xbc"], D, 0), sv["pz"], dy[0]], [p["d_rep"], p["ssd_norm"]],
        [(D, F32), (D, F32), (D, BF16)], [(1, D), (1, D)], tile=t_r, name="ssd_post_bwd")
    dxbc_act, ddt, g["a_log"] = _ssd_bwd(sv["xbc"], sv["sm"], p["a_log"], sv["states"], dy_ssd, dxs_skip, B=B, NC=NC)
    dpsm, g["smallbias"] = rm(_b_smallact, [sv["psm"], ddt, dlogf], [p["smallbias"]], [(CH, BF16)], [(1, CH)],
                              tile=t_r, name="smallact_bwd")
    dconv, g["conv_w"], g["conv_b"] = _conv_bwd_pre(sv["pxbc"], dxbc_act, p["conv_w"], p["conv_b"], geom=geom, tile=CH)
    dpxbc = _conv_bwd_x(dconv, p["conv_w"], geom=geom, tile=CH)
    dproj = jnp.concatenate([dz, dpxbc, dq.astype(BF16), dk, dv, du, dpg, dpsm], axis=1)
    dxn1 = _mm(dproj, p["w_cat"], "nt", F32, name="in_dx")
    g["w_cat"] = _mm(sv["xn1"], dproj, "tn", F32, name="in_dw")
    dx_in, g["norm1"] = rm(_b_norm, [sv["x"], dxn1, dx_mid], [p["norm1"]], [(D, F32)], [(1, D)], tile=t_r,
                           name="norm1_bwd")
    return dx_in, g


_BIG = ["w_in", "s5_w_glu", "w_branch", "w_out", "w_ffn_in", "w_ffn_out"]
_NAMES = ['meta', 'norm1', 'w_in', 'ssd_conv_w', 'ssd_conv_b', 'ssd_dt_bias', 'ssd_a_log', 'ssd_d', 'ssd_norm',
          'fox_bf', 's5_lam_re', 's5_lam_im', 's5_b_re', 's5_b_im', 's5_c_re', 's5_c_im', 's5_log_step', 's5_d',
          's5_w_glu', 'w_branch', 'w_out', 'norm2', 'w_ffn_in', 'w_ffn_out', 'norm_f']
_SHARD_AXIS = {"meta": 1, "ssd_conv_w": 2}
_BIG_AXIS = {"w_in": 2, "s5_w_glu": 1, "w_branch": 2, "w_out": 1, "w_ffn_in": 2, "w_ffn_out": 1}


def kernel(x, meta, norm1, w_in, ssd_conv_w, ssd_conv_b, ssd_dt_bias, ssd_a_log, ssd_d, ssd_norm, fox_bf, s5_lam_re, s5_lam_im, s5_b_re, s5_b_im, s5_c_re, s5_c_im, s5_log_step, s5_d, s5_w_glu, w_branch, w_out, norm2, w_ffn_in, w_ffn_out, norm_f, loss_target, m_meta, m_norm1, m_w_in, m_ssd_conv_w, m_ssd_conv_b, m_ssd_dt_bias, m_ssd_a_log, m_ssd_d, m_ssd_norm, m_fox_bf, m_s5_lam_re, m_s5_lam_im, m_s5_b_re, m_s5_b_im, m_s5_c_re, m_s5_c_im, m_s5_log_step, m_s5_d, m_s5_w_glu, m_w_branch, m_w_out, m_norm2, m_w_ffn_in, m_w_ffn_out, m_norm_f, v_meta, v_norm1, v_w_in, v_ssd_conv_w, v_ssd_conv_b, v_ssd_dt_bias, v_ssd_a_log, v_ssd_d, v_ssd_norm, v_fox_bf, v_s5_lam_re, v_s5_lam_im, v_s5_b_re, v_s5_b_im, v_s5_c_re, v_s5_c_im, v_s5_log_step, v_s5_d, v_s5_w_glu, v_w_branch, v_w_out, v_norm2, v_w_ffn_in, v_w_ffn_out, v_norm_f):
    args = locals()
    W = {n: args[n] for n in _NAMES}
    Mo = {n: args["m_" + n] for n in _NAMES}
    Vo = {n: args["v_" + n] for n in _NAMES}
    B, S, _ = x.shape
    depth = norm1.shape[0]
    L = S + N_META
    Lp = ((L + CH - 1) // CH) * CH
    pad = Lp - L
    assert pad + N_META == CH and S % CH == 0
    NC = Lp // CH
    T = B * Lp
    geom = (T, Lp, pad)
    dims = (B, NC, Lp, pad)
    xi, yi, ci = _me()
    dev = 4 * xi + 2 * yi + ci

    gath = {n: _all_gather(W[n].astype(BF16), "gather_" + n) for n in _BIG}
    full = {n: jnp.concatenate([gath[n][k] for k in range(N_DEV)], axis=_BIG_AXIS[n]) for n in _BIG}
    sm_pack = _Packer([meta.shape, ssd_conv_w.shape], 8)
    sm_g = _all_gather(sm_pack.pack([meta, ssd_conv_w], F32), "gather_small")
    sm_parts = [sm_pack.unpack(sm_g[k]) for k in range(N_DEV)]
    meta_full = jnp.concatenate([sm_parts[k][0] for k in range(N_DEV)], axis=1)
    conv_w_full = jnp.concatenate([sm_parts[k][1] for k in range(N_DEV)], axis=2)

    layers = []
    s5_in = []
    for l in range(depth):
        lre = _s5_tile(s5_lam_re[l])
        lim = _s5_tile(s5_lam_im[l])
        lst = _s5_tile(jnp.repeat(s5_log_step[l], S5_P))
        bre = _s5_tile_b(s5_b_re[l])
        bim = _s5_tile_b(s5_b_im[l])
        s5_in.append((lre, lim, lst, bre, bim))
        a, b, br_, bi_ = _s5_params(lre, lim, lst, bre, bim)
        Bsg, Csg = _s5_blockdiag(_s5_untile_b(br_), _s5_untile_b(bi_), s5_c_re[l], s5_c_im[l])
        NS = S5_G // S5_SG
        lam = jnp.concatenate([a.reshape(NS, 1, S5_W), b.reshape(NS, 1, S5_W)], axis=2)
        zpad = jnp.zeros((CH - SSD_H - FOX_H,), F32)
        layers.append(dict(
            norm1=norm1[l][None], w_cat=_w_cat(full["w_in"][l]),
            smallbias=jnp.concatenate([ssd_dt_bias[l], fox_bf[l], zpad])[None],
            conv_w=conv_w_full[l], conv_b=ssd_conv_b[l][None],
            a_log=jnp.concatenate([ssd_a_log[l], jnp.zeros((CH - SSD_H,), F32)])[None],
            d_rep=jnp.repeat(ssd_d[l], SSD_P)[None], ssd_norm=ssd_norm[l][None],
            Bsg=Bsg, Csg=Csg, lam=lam, s5_d=s5_d[l][None], w_glu=full["s5_w_glu"][l],
            w_branch=[full["w_branch"][l, n] for n in range(3)], w_out=full["w_out"][l],
            norm2=norm2[l][None], w_ffn_in=full["w_ffn_in"][l], w_ffn_out=full["w_ffn_out"][l]))

    xs = jnp.concatenate([jnp.zeros((B, pad, D), F32), jnp.broadcast_to(meta_full[None], (B, N_META, D)), x], axis=1)
    h = xs.reshape(T, D)
    saved = []
    for l in range(depth):
        h, sv = _layer_fwd(h, layers[l], geom, dims)
        saved.append(sv)
    dh, loss_row, g_nf = _loss_head(h, norm_f[None], loss_target.reshape(B * S, D), B=B, NC=NC, S=S)
    loss = lax.psum(loss_row[0, 0], AXES)

    G = {n: [None] * depth for n in _NAMES}
    for l in reversed(range(depth)):
        dh, g = _layer_bwd(dh, layers[l], saved[l], geom, dims)
        saved[l] = None
        G["norm1"][l] = g["norm1"][0]
        G["norm2"][l] = g["norm2"][0]
        G["w_in"][l] = _w_uncat(g["w_cat"])
        G["ssd_conv_w"][l] = g["conv_w"]
        G["ssd_conv_b"][l] = g["conv_b"][0]
        G["ssd_dt_bias"][l] = g["smallbias"][0, 0:SSD_H]
        G["fox_bf"][l] = g["smallbias"][0, SSD_H:SSD_H + FOX_H]
        G["ssd_a_log"][l] = g["a_log"][0, 0:SSD_H]
        G["ssd_d"][l] = g["d_rep"].reshape(SSD_H, SSD_P).sum(axis=1)
        G["ssd_norm"][l] = g["ssd_norm"][0]
        dbr, dbi, dcr, dci = _s5_unblock(g["Bsg"], g["Csg"])
        da = _s5_tile(g["lam"][:, 0, 0:S5_W])
        db = _s5_tile(g["lam"][:, 0, S5_W:])
        dlre, dlim, dlst, dbre, dbim = _s5_params_bwd(*s5_in[l], da, db, _s5_tile_b(dbr), _s5_tile_b(dbi))
        G["s5_lam_re"][l] = dlre.reshape(S5_G, S5_P)
        G["s5_lam_im"][l] = dlim.reshape(S5_G, S5_P)
        G["s5_log_step"][l] = dlst.reshape(S5_G, S5_P).sum(axis=1)
        G["s5_b_re"][l] = _s5_untile_b(dbre)
        G["s5_b_im"][l] = _s5_untile_b(dbim)
        G["s5_c_re"][l] = dcr
        G["s5_c_im"][l] = dci
        G["s5_d"][l] = g["s5_d"][0]
        G["s5_w_glu"][l] = g["w_glu"]
        G["w_branch"][l] = jnp.stack(g["w_branch"])
        G["w_out"][l] = g["w_out"]
        G["w_ffn_in"][l] = g["w_ffn_in"]
        G["w_ffn_out"][l] = g["w_ffn_out"]
    dxs = dh.reshape(B, Lp, D)
    grad_x = dxs[:, pad + N_META:, :]
    part = {n: jnp.stack(G[n]) for n in _NAMES if n not in ("meta", "norm_f")}
    part["meta"] = dxs[:, pad:pad + N_META, :].sum(axis=0)
    part["norm_f"] = g_nf[0]

    grads = {}
    for n in _BIG:
        ax = _BIG_AXIS[n]
        a = part[n]
        a = a.reshape(a.shape[:ax] + (N_DEV, a.shape[ax] // N_DEV) + a.shape[ax + 1:])
        grads[n] = _reduce_scatter(jnp.moveaxis(a, ax, 0), n)

    small = [n for n in _NAMES if n not in _BIG]
    sp = _Packer([part[n].shape for n in small], 128)
    tot = sp.unpack(_sum8(_all_gather(sp.pack([part[n] for n in small], F32), "gather_small_grads")))
    for n, t in zip(small, tot):
        if n in _SHARD_AXIS:
            ax = _SHARD_AXIS[n]
            w = W[n].shape[ax]
            t = lax.dynamic_slice_in_dim(t, dev * w, w, axis=ax)
        grads[n] = t

    delta, new_m, new_v = {}, {}, {}
    for n in _BIG:
        delta[n], new_m[n], new_v[n] = _adam(W[n], grads[n], Mo[n], Vo[n], "adam_" + n)
    ap = _Packer([W[n].shape for n in small], 128)
    d_, m_, v_ = _adam(ap.pack([W[n] for n in small], F32), ap.pack([grads[n] for n in small], F32),
                       ap.pack([Mo[n] for n in small], F32), ap.pack([Vo[n] for n in small], F32), "adam_small")
    for n, a, b, c in zip(small, ap.unpack(d_), ap.unpack(m_), ap.unpack(v_)):
        delta[n], new_m[n], new_v[n] = a, b, c
    return (loss, grad_x, *[grads[n] for n in _NAMES], *[delta[n] for n in _NAMES],
            *[new_m[n] for n in _NAMES], *[new_v[n] for n in _NAMES])
```

```python
import functools
import math

import numpy as np
import jax
import jax.numpy as jnp
from jax import lax
from jax.experimental import pallas as pl
from jax.experimental.pallas import tpu as pltpu

F32 = jnp.float32
BF16 = jnp.bfloat16
AXES = ("x", "y", "c")
MESH = pl.DeviceIdType.MESH
N_DEV = 8

D = 1024
N_META = 16
CH = 128
EPS = 1e-6
NEG = -1e30
SSD_H, SSD_P, SSD_N, SSD_G = 16, 64, 128, 2
CONV_K, CONV_DIM = 4, 1536
FOX_H, FOX_DH = 8, 128
S5_G, S5_P, S5_C = 64, 64, 16
S5_SG = 8
S5_W = S5_SG * S5_P
DFF = 2816
D_IN = 9752
OFF_Z, OFF_XBC, OFF_QKV, OFF_U, OFF_G, OFF_SM, D_CAT = 0, 1024, 2560, 5632, 6656, 9728, 9856
O_Z, O_XBC, O_DT, O_QKV, O_F, O_U, O_G = 0, 1024, 2560, 2576, 5648, 5656, 6680

ADAM_LR, ADAM_B1, ADAM_B2, ADAM_EPS, ADAM_WD, ADAM_STEP = 0.001, 0.9, 0.999, 1e-08, 0.01, 10

VMEM_LIMIT_V7X = 52 * 1024 * 1024
HI = lax.Precision.HIGHEST


def _cp(sem=None):
    return pltpu.CompilerParams(dimension_semantics=sem, vmem_limit_bytes=VMEM_LIMIT_V7X)


def _pick(n, cands):
    for c in cands:
        if n % c == 0:
            return c
    raise ValueError(f"no tile for {n}")


_TILES = (1408, 1024, 896, 768, 512, 384, 256, 128)


def _mm(a, b, mode, out_dtype, *, name, n=None, b_off=0, res=None, tm=None, tn=None, tk=None):
    if mode == "tn":
        K, M = a.shape
    else:
        M, K = a.shape
    if mode == "nt":
        N = b.shape[0]
    else:
        N = n if n is not None else b.shape[1]
    tm = tm or _pick(M, (1024, 768, 512, 384, 256, 128, 64, 16, 8))
    wide = (1408,) if mode == "tn" else ()
    tn = tn or _pick(math.gcd(N, b_off) if b_off else N, wide + (1024, 896, 768, 512, 384, 256, 128))
    tk = tk or _pick(K, _TILES)
    nk = K // tk
    joff = b_off // tn

    def body(*refs):
        if res is None:
            a_ref, b_ref, o_ref, acc = refs
            r_ref = None
        else:
            a_ref, b_ref, r_ref, o_ref, acc = refs
        k = pl.program_id(2)
        av = a_ref[...].astype(BF16)
        bv = b_ref[...].astype(BF16)
        if mode == "nn":
            p = jnp.dot(av, bv, preferred_element_type=F32)
        elif mode == "nt":
            p = lax.dot_general(av, bv, (((1,), (1,)), ((), ())), preferred_element_type=F32)
        else:
            p = lax.dot_general(av, bv, (((0,), (0,)), ((), ())), preferred_element_type=F32)

        @pl.when(k == 0)
        def _():
            acc[...] = p

        @pl.when(k > 0)
        def _():
            acc[...] += p

        @pl.when(k == nk - 1)
        def _():
            r = acc[...]
            if r_ref is not None:
                r = r + r_ref[...]
            o_ref[...] = r.astype(o_ref.dtype)

    if mode == "tn":
        a_spec = pl.BlockSpec((tk, tm), lambda i, j, k: (k, i))
    else:
        a_spec = pl.BlockSpec((tm, tk), lambda i, j, k: (i, k))
    if mode == "nt":
        b_spec = pl.BlockSpec((tn, tk), lambda i, j, k: (j, k))
    else:
        b_spec = pl.BlockSpec((tk, tn), lambda i, j, k: (k, j + joff))
    o_spec = pl.BlockSpec((tm, tn), lambda i, j, k: (i, j))
    in_specs = [a_spec, b_spec] + ([o_spec] if res is not None else [])
    args = (a, b) + ((res,) if res is not None else ())
    return pl.pallas_call(
        body, name=name, grid=(M // tm, N // tn, nk),
        in_specs=in_specs, out_specs=o_spec,
        out_shape=jax.ShapeDtypeStruct((M, N), out_dtype),
        scratch_shapes=[pltpu.VMEM((tm, tn), F32)],
        compiler_params=_cp(("parallel", "parallel", "arbitrary")),
    )(*args)


def _rowmap(fn, row_ins, const_ins, row_outs, acc_outs, *, geom, tile, name):
    T, Lp, pad = geom
    assert Lp % tile == 0
    per_seq = Lp // tile
    specs, args = [], []
    for r in row_ins:
        arr, w, cb = r if isinstance(r, tuple) else (r, r.shape[1], 0)
        specs.append(pl.BlockSpec((tile, w), functools.partial(lambda i, cb: (i, cb), cb=cb)))
        args.append(arr)
    for c in const_ins:
        specs.append(pl.BlockSpec(c.shape, functools.partial(lambda i, nd: (0,) * nd, nd=c.ndim)))
        args.append(c)
    n_r, n_c, n_o, n_a = len(row_ins), len(const_ins), len(row_outs), len(acc_outs)
    out_specs = [pl.BlockSpec((tile, w), lambda i: (i, 0)) for w, _ in row_outs]
    out_specs += [pl.BlockSpec(s, lambda i: (0, 0)) for s in acc_outs]
    out_shape = [jax.ShapeDtypeStruct((T, w), dt) for w, dt in row_outs]
    out_shape += [jax.ShapeDtypeStruct(s, F32) for s in acc_outs]

    def body(*refs):
        i = pl.program_id(0)
        pos = (i % per_seq) * tile + lax.broadcasted_iota(jnp.int32, (tile, 1), 0)
        valid = pos >= pad
        vals = [r[...].astype(F32) for r in refs[:n_r]] + [r[...] for r in refs[n_r:n_r + n_c]]
        outs = fn(valid, *vals)
        if not isinstance(outs, (tuple, list)):
            outs = (outs,)
        orefs = refs[n_r + n_c:]
        for r, v in zip(orefs[:n_o], outs[:n_o]):
            r[...] = v.astype(r.dtype)
        for r, v in zip(orefs[n_o:], outs[n_o:]):
            @pl.when(i == 0)
            def _(r=r, v=v):
                r[...] = v

            @pl.when(i > 0)
            def _(r=r, v=v):
                r[...] += v

    res = pl.pallas_call(
        body, name=name, grid=(T // tile,), in_specs=specs, out_specs=out_specs, out_shape=out_shape,
        compiler_params=_cp(("arbitrary",)),
    )(*args)
    return res


def _sigmoid(x):
    return 1.0 / (1.0 + jnp.exp(-x))


def _silu(x):
    return x * _sigmoid(x)


def _softplus(x):
    return jnp.maximum(x, 0.0) + jnp.log(1.0 + jnp.exp(-jnp.abs(x)))


def _gelu(x):
    return 0.5 * x * (1.0 + jnp.tanh(math.sqrt(2.0 / math.pi) * (x + 0.044715 * x * x * x)))


def _rms(x, w):
    return x * lax.rsqrt(jnp.mean(x * x, axis=-1, keepdims=True) + EPS) * w


def _colsum(v):
    return jnp.sum(v, axis=0, keepdims=True)


def _f_norm(valid, x, w):
    return _rms(x, w)


def _b_norm(valid, x, dxn, dres, w):
    _, vjp = jax.vjp(_rms, x, w)
    dx, dw = vjp(dxn)
    return jnp.where(valid, dx + dres, 0.0), dw


def _smallact(valid, raw, bias):
    lane = lax.broadcasted_iota(jnp.int32, raw.shape, 1)
    v = raw + bias
    dt = _softplus(v)
    logf = -_softplus(-v)
    out = jnp.where(lane < SSD_H, dt, jnp.where(lane < SSD_H + FOX_H, logf, 0.0))
    return jnp.where(valid, out, 0.0)


def _b_smallact(valid, raw, d1, d2, bias):
    _, vjp = jax.vjp(lambda r, b: _smallact(valid, r, b), raw, bias)
    return vjp(d1 + d2)


def _ssd_post(valid, y, xs, z, drep, nw):
    y = (y + xs * drep) * _silu(z)
    return _rms(y, nw)


def _b_ssd_post(valid, y, xs, z, dya, drep, nw):
    _, vjp = jax.vjp(lambda a, b, c, d, e: _ssd_post(valid, a, b, c, d, e), y, xs, z, drep, nw)
    dy, dxs, dz, dd, dn = vjp(dya)
    return dy, dxs, dz, dd, dn


def _s5_pre(valid, ys, u, d):
    return _gelu(ys + d * u)


def _s5_glu(valid, ys, u, t, d):
    y1 = _gelu(ys + d * u)
    return y1 * _sigmoid(t)


def _b_s5_glu(valid, ys, u, t, dyc, d):
    y1 = _gelu(ys + d * u)
    _, vjp = jax.vjp(lambda a, b: a * _sigmoid(b), y1, t)
    dy1, dt = vjp(dyc)
    return dt, dy1


def _b_s5_pre(valid, ys, u, dy1a, dy1b, d):
    _, vjp = jax.vjp(lambda a, b, c: _gelu(a + c * b), ys, u, d)
    dys, du, dd = vjp(dy1a + dy1b)
    return dys, du, dd


def _merge(valid, g0, g1, g2, b0, b1, b2):
    m = _sigmoid(g0) * b0 + _sigmoid(g1) * b1 + _sigmoid(g2) * b2
    return jnp.where(valid, m, 0.0)


def _b_merge(valid, g0, g1, g2, b0, b1, b2, dmix):
    _, vjp = jax.vjp(lambda *a: _merge(valid, *a), g0, g1, g2, b0, b1, b2)
    d = vjp(dmix)
    return jnp.concatenate(d[:3], axis=1), d[3], d[4], d[5]


def _swiglu(valid, g, up):
    return _silu(g) * up


def _b_swiglu(valid, g, up, dact):
    _, vjp = jax.vjp(lambda a, b: _silu(a) * b, g, up)
    dg, dup = vjp(dact)
    return jnp.concatenate([dg, dup], axis=1)


def _conv_taps(ext, tile):
    taps = []
    for k in range(CONV_K):
        sh = CONV_K - 1 - k
        v = ext if sh == 0 else pltpu.roll(ext, sh, 0)
        taps.append(v[8:8 + tile])
    return taps


def _conv_fwd(x, w, b, *, geom, tile):
    T, Lp, pad = geom
    per_seq = Lp // tile
    hb = tile // 8

    def body(x_ref, h_ref, w_ref, b_ref, o_ref):
        i = pl.program_id(0)
        pos = (i % per_seq) * tile + lax.broadcasted_iota(jnp.int32, (tile, 1), 0)
        ext = jnp.concatenate([h_ref[...], x_ref[...]], axis=0)
        taps = _conv_taps(ext, tile)
        acc = b_ref[...] + taps[0] * w_ref[0:1, :]
        for k in range(1, CONV_K):
            acc = acc + taps[k] * w_ref[k:k + 1, :]
        o_ref[...] = jnp.where(pos >= pad, _silu(acc), 0.0)

    return pl.pallas_call(
        body, name="conv_fwd", grid=(T // tile,),
        in_specs=[pl.BlockSpec((tile, CONV_DIM), lambda i: (i, 0)),
                  pl.BlockSpec((8, CONV_DIM), lambda i: (jnp.maximum(i * hb - 1, 0), 0)),
                  pl.BlockSpec((CONV_K, CONV_DIM), lambda i: (0, 0)),
                  pl.BlockSpec((1, CONV_DIM), lambda i: (0, 0))],
        out_specs=pl.BlockSpec((tile, CONV_DIM), lambda i: (i, 0)),
        out_shape=jax.ShapeDtypeStruct((T, CONV_DIM), F32),
        compiler_params=_cp(("arbitrary",)),
    )(x, x, w, b)


def _conv_bwd_pre(x, dact, w, b, *, geom, tile):
    T, Lp, pad = geom
    per_seq = Lp // tile
    hb = tile // 8

    def body(x_ref, h_ref, d_ref, w_ref, b_ref, dc_ref, dw_ref, db_ref):
        i = pl.program_id(0)
        pos = (i % per_seq) * tile + lax.broadcasted_iota(jnp.int32, (tile, 1), 0)
        ext = jnp.concatenate([h_ref[...], x_ref[...]], axis=0)
        taps = _conv_taps(ext, tile)
        acc = b_ref[...] + taps[0] * w_ref[0:1, :]
        for k in range(1, CONV_K):
            acc = acc + taps[k] * w_ref[k:k + 1, :]
        sg = _sigmoid(acc)
        dsilu = sg * (1.0 + acc * (1.0 - sg))
        dc = jnp.where(pos >= pad, d_ref[...] * dsilu, 0.0)
        dc_ref[...] = dc
        dw = jnp.concatenate([_colsum(dc * taps[k]) for k in range(CONV_K)], axis=0)
        db = _colsum(dc)

        @pl.when(i == 0)
        def _():
            dw_ref[...] = dw
            db_ref[...] = db

        @pl.when(i > 0)
        def _():
            dw_ref[...] += dw
            db_ref[...] += db

    return pl.pallas_call(
        body, name="conv_bwd_pre", grid=(T // tile,),
        in_specs=[pl.BlockSpec((tile, CONV_DIM), lambda i: (i, 0)),
                  pl.BlockSpec((8, CONV_DIM), lambda i: (jnp.maximum(i * hb - 1, 0), 0)),
                  pl.BlockSpec((tile, CONV_DIM), lambda i: (i, 0)),
                  pl.BlockSpec((CONV_K, CONV_DIM), lambda i: (0, 0)),
                  pl.BlockSpec((1, CONV_DIM), lambda i: (0, 0))],
        out_specs=[pl.BlockSpec((tile, CONV_DIM), lambda i: (i, 0)),
                   pl.BlockSpec((CONV_K, CONV_DIM), lambda i: (0, 0)),
                   pl.BlockSpec((1, CONV_DIM), lambda i: (0, 0))],
        out_shape=[jax.ShapeDtypeStruct((T, CONV_DIM), F32),
                   jax.ShapeDtypeStruct((CONV_K, CONV_DIM), F32),
                   jax.ShapeDtypeStruct((1, CONV_DIM), F32)],
        compiler_params=_cp(("arbitrary",)),
    )(x, x, dact, w, b)


def _conv_bwd_x(dc, w, *, geom, tile):
    T, Lp, pad = geom
    nt = T // tile
    hb = tile // 8

    def body(d_ref, h_ref, w_ref, o_ref):
        i = pl.program_id(0)
        halo = jnp.where(i < nt - 1, h_ref[...], 0.0)
        ext = jnp.concatenate([d_ref[...], halo], axis=0)
        n_ext = tile + 8
        acc = ext[0:tile] * w_ref[CONV_K - 1:CONV_K, :]
        for j in range(1, CONV_K):
            acc = acc + pltpu.roll(ext, n_ext - j, 0)[0:tile] * w_ref[CONV_K - 1 - j:CONV_K - j, :]
        o_ref[...] = acc.astype(o_ref.dtype)

    return pl.pallas_call(
        body, name="conv_bwd_x", grid=(nt,),
        in_specs=[pl.BlockSpec((tile, CONV_DIM), lambda i: (i, 0)),
                  pl.BlockSpec((8, CONV_DIM), lambda i: (jnp.minimum((i + 1) * hb, nt * hb - 1), 0)),
                  pl.BlockSpec((CONV_K, CONV_DIM), lambda i: (0, 0))],
        out_specs=pl.BlockSpec((tile, CONV_DIM), lambda i: (i, 0)),
        out_shape=jax.ShapeDtypeStruct((T, CONV_DIM), BF16),
        compiler_params=_cp(("arbitrary",)),
    )(dc, dc, w)


def _ssd_common(sm_ref, alog_ref):
    lane = lax.broadcasted_iota(jnp.int32, (1, CH), 1)
    A = jnp.where(lane < SSD_H, -jnp.exp(alog_ref[...]), 0.0)
    dt = sm_ref[...]
    adt = dt * A
    r = lax.broadcasted_iota(jnp.int32, (CH, CH), 0)
    c = lax.broadcasted_iota(jnp.int32, (CH, CH), 1)
    tril = (r >= c).astype(F32)
    cs = jnp.dot(tril, adt, precision=HI, preferred_element_type=F32)
    csT = cs.T
    cs_last = jnp.sum(jnp.where(r == CH - 1, cs, 0.0), axis=0, keepdims=True)
    return A, dt, cs, csT, cs_last, tril, r, c


def _ssd_lanes():
    r = lax.broadcasted_iota(jnp.int32, (CH, D), 0)
    c = lax.broadcasted_iota(jnp.int32, (CH, D), 1)
    return (c // SSD_P == r).astype(F32)


def _ssd_per_lane(dt, cs):
    ex = _ssd_lanes()
    dt_rep = jnp.dot(dt, ex, precision=HI, preferred_element_type=F32)
    cs_rep = jnp.dot(cs, ex, precision=HI, preferred_element_type=F32)
    r = lax.broadcasted_iota(jnp.int32, (CH, D), 0)
    last_rep = jnp.sum(jnp.where(r == CH - 1, cs_rep, 0.0), axis=0, keepdims=True)
    return dt_rep, cs_rep, last_rep


def _ssd_per_head(*per_lane):
    ex = _ssd_lanes()
    return [lax.dot_general(v, ex, (((1,), (1,)), ((), ())), precision=HI, preferred_element_type=F32)
            for v in per_lane]


def _nt(a, b):
    return lax.dot_general(a, b, (((1,), (1,)), ((), ())), preferred_element_type=F32)


def _tn(a, b):
    return lax.dot_general(a, b, (((0,), (0,)), ((), ())), preferred_element_type=F32)


def _nn(a, b):
    return jnp.dot(a, b, preferred_element_type=F32)


def _ssd_fwd(xbc, sm, alog, *, B, NC):
    T = B * NC * CH

    def body(x_ref, sm_ref, alog_ref, y_ref, st_ref, S):
        cidx = pl.program_id(1)

        @pl.when(cidx == 0)
        def _():
            S[...] = jnp.zeros_like(S)

        st_ref[0] = S[...]
        A, dt, cs, csT, cs_last, tril, _, _ = _ssd_common(sm_ref, alog_ref)
        dt_rep, cs_rep, last_rep = _ssd_per_lane(dt, cs)
        xdt = x_ref[:, 0:D] * dt_rep
        xdec = (xdt * jnp.exp(last_rep - cs_rep)).astype(BF16)
        e_rep = jnp.exp(cs_rep)
        HG = SSD_H // SSD_G
        for g in range(SSD_G):
            gl = slice(g * HG * SSD_P, (g + 1) * HG * SSD_P)
            Bb = x_ref[:, D + g * SSD_N:D + (g + 1) * SSD_N].astype(BF16)
            Cb = x_ref[:, D + SSD_G * SSD_N + g * SSD_N:D + SSD_G * SSD_N + (g + 1) * SSD_N].astype(BF16)
            G = _nt(Cb, Bb)
            STg = S[g * HG:(g + 1) * HG].reshape(HG * SSD_P, SSD_N)
            y_off = e_rep[:, gl] * _nt(Cb, STg.astype(BF16))
            upd = _tn(xdec[:, gl], Bb)
            for rr in range(HG):
                h = g * HG + rr
                hl = slice(h * SSD_P, (h + 1) * SSD_P)
                col = cs[:, h:h + 1]
                row = csT[h:h + 1, :]
                Ld = jnp.where(tril > 0, jnp.exp(jnp.minimum(col - row, 0.0)), 0.0)
                M = (G * Ld).astype(BF16)
                y_ref[:, hl] = _nn(M, xdt[:, hl].astype(BF16)) + y_off[:, rr * SSD_P:(rr + 1) * SSD_P]
                rows = slice(rr * SSD_P, (rr + 1) * SSD_P)
                S[h] = jnp.exp(cs_last[:, h:h + 1]) * STg[rows] + upd[rows]

    return pl.pallas_call(
        body, name="ssd_fwd", grid=(B, NC),
        in_specs=[pl.BlockSpec((CH, CONV_DIM), lambda b, c: (b * NC + c, 0)),
                  pl.BlockSpec((CH, CH), lambda b, c: (b * NC + c, 0)),
                  pl.BlockSpec((1, CH), lambda b, c: (0, 0))],
        out_specs=[pl.BlockSpec((CH, D), lambda b, c: (b * NC + c, 0)),
                   pl.BlockSpec((1, SSD_H, SSD_P, SSD_N), lambda b, c: (b * NC + c, 0, 0, 0))],
        out_shape=[jax.ShapeDtypeStruct((T, D), F32),
                   jax.ShapeDtypeStruct((B * NC, SSD_H, SSD_P, SSD_N), F32)],
        scratch_shapes=[pltpu.VMEM((SSD_H, SSD_P, SSD_N), F32)],
        compiler_params=_cp(("arbitrary", "arbitrary")),
    )(xbc, sm, alog)


def _ssd_bwd(xbc, sm, alog, states, dy, dxs_skip, *, B, NC):
    T = B * NC * CH

    def rix(b, c):
        return b * NC + (NC - 1 - c)

    def body(x_ref, sm_ref, alog_ref, st_ref, dy_ref, sk_ref, dx_ref, ddt_ref, dal_ref, dS):
        bidx = pl.program_id(0)
        cidx = pl.program_id(1)

        @pl.when(cidx == 0)
        def _():
            dS[...] = jnp.zeros_like(dS)

        A, dt, cs, csT, cs_last, tril, r, c = _ssd_common(sm_ref, alog_ref)
        lane = lax.broadcasted_iota(jnp.int32, (1, CH), 1)
        dt_rep, cs_rep, last_rep = _ssd_per_lane(dt, cs)
        xs = x_ref[:, 0:D]
        xdt = xs * dt_rep
        e_rep = jnp.exp(cs_rep)
        dec_rep = jnp.exp(last_rep - cs_rep)
        dye = dy_ref[...] * e_rep
        xdec = xdt * dec_rep
        HG = SSD_H // SSD_G
        DCcol = jnp.zeros((CH, CH), F32)
        DCrow = jnp.zeros((CH, CH), F32)
        dlast = jnp.zeros((1, CH), F32)
        t_off, t_dec, dx_state = [], [], []
        for g in range(SSD_G):
            ob = D + g * SSD_N
            oc = D + SSD_G * SSD_N + g * SSD_N
            gl = slice(g * HG * SSD_P, (g + 1) * HG * SSD_P)
            Bb = x_ref[:, ob:ob + SSD_N].astype(BF16)
            Cb = x_ref[:, oc:oc + SSD_N].astype(BF16)
            G = _nt(Cb, Bb)
            STg = st_ref[0, g * HG:(g + 1) * HG].reshape(HG * SSD_P, SSD_N)
            dSTg = dS[g * HG:(g + 1) * HG].reshape(HG * SSD_P, SSD_N)
            STb = STg.astype(BF16)
            dSTb = dSTg.astype(BF16)
            dyeb = dye[:, gl].astype(BF16)
            t_off.append(dye[:, gl] * _nt(Cb, STb))
            dCg = _nn(dyeb, STb)
            dS_in = _tn(dyeb, Cb)
            Z = _nt(Bb, dSTb)
            dx_state.append(dec_rep[:, gl] * Z)
            t_dec.append(xdec[:, gl] * Z)
            dBg = _nn(xdec[:, gl].astype(BF16), dSTb)
            dG = jnp.zeros((CH, CH), F32)
            for rr in range(HG):
                h = g * HG + rr
                hl = slice(h * SSD_P, (h + 1) * SSD_P)
                rows = slice(rr * SSD_P, (rr + 1) * SSD_P)
                col = cs[:, h:h + 1]
                row = csT[h:h + 1, :]
                Ld = jnp.where(tril > 0, jnp.exp(jnp.minimum(col - row, 0.0)), 0.0)
                Mf = G * Ld
                dyb = dy_ref[:, hl].astype(BF16)
                dx_ref[:, hl] = _tn(Mf.astype(BF16), dyb)
                dM = _nt(dyb, xdt[:, hl].astype(BF16))
                dG = dG + dM * Ld
                W = dM * Mf
                DCcol = DCcol + jnp.where(c == h, jnp.sum(W, axis=1, keepdims=True), 0.0)
                DCrow = DCrow - jnp.where(r == h, jnp.sum(W, axis=0, keepdims=True), 0.0)
                el = jnp.exp(cs_last[:, h:h + 1])
                dl = el * jnp.sum(jnp.sum(dSTg[rows] * STg[rows], axis=1, keepdims=True), axis=0, keepdims=True)
                dlast = dlast + jnp.where(lane == h, dl, 0.0)
                dS[h] = dS_in[rows] + el * dSTg[rows]
            dGb = dG.astype(BF16)
            dx_ref[:, ob:ob + SSD_N] = dBg + _tn(dGb, Cb)
            dx_ref[:, oc:oc + SSD_N] = dCg + _nn(dGb, Bb)
        dxdt = dx_ref[:, 0:D] + jnp.concatenate(dx_state, axis=1)
        dx_ref[:, 0:D] = dxdt * dt_rep + sk_ref[...]
        s_off, s_dec, DX = _ssd_per_head(jnp.concatenate(t_off, axis=1), jnp.concatenate(t_dec, axis=1), dxdt * xs)
        dlast = dlast + jnp.sum(s_dec, axis=0, keepdims=True)
        DC = DCcol + s_off - s_dec + DCrow.T + jnp.where(r == CH - 1, dlast, 0.0)
        triu = (r <= c).astype(F32)
        dadt = jnp.dot(triu, DC, precision=HI, preferred_element_type=F32)
        ddt_ref[...] = dadt * A + DX
        dal = jnp.sum(dadt * dt, axis=0, keepdims=True) * A

        @pl.when((bidx == 0) & (cidx == 0))
        def _():
            dal_ref[...] = dal

        @pl.when((bidx > 0) | (cidx > 0))
        def _():
            dal_ref[...] += dal

    return pl.pallas_call(
        body, name="ssd_bwd", grid=(B, NC),
        in_specs=[pl.BlockSpec((CH, CONV_DIM), lambda b, c: (rix(b, c), 0)),
                  pl.BlockSpec((CH, CH), lambda b, c: (rix(b, c), 0)),
                  pl.BlockSpec((1, CH), lambda b, c: (0, 0)),
                  pl.BlockSpec((1, SSD_H, SSD_P, SSD_N), lambda b, c: (rix(b, c), 0, 0, 0)),
                  pl.BlockSpec((CH, D), lambda b, c: (rix(b, c), 0)),
                  pl.BlockSpec((CH, D), lambda b, c: (rix(b, c), 0))],
        out_specs=[pl.BlockSpec((CH, CONV_DIM), lambda b, c: (rix(b, c), 0)),
                   pl.BlockSpec((CH, CH), lambda b, c: (rix(b, c), 0)),
                   pl.BlockSpec((1, CH), lambda b, c: (0, 0))],
        out_shape=[jax.ShapeDtypeStruct((T, CONV_DIM), F32),
                   jax.ShapeDtypeStruct((T, CH), F32),
                   jax.ShapeDtypeStruct((1, CH), F32)],
        scratch_shapes=[pltpu.VMEM((SSD_H, SSD_P, SSD_N), F32)],
        compiler_params=_cp(("arbitrary", "arbitrary")),
    )(xbc, sm, alog, states, dy, dxs_skip)


def _cumsum_seq(v, *, B, NC, reverse, name):
    T = B * NC * CH

    def ix(b, c):
        return b * NC + ((NC - 1 - c) if reverse else c)

    def body(v_ref, o_ref, carry):
        cidx = pl.program_id(1)

        @pl.when(cidx == 0)
        def _():
            carry[...] = jnp.zeros_like(carry)

        r = lax.broadcasted_iota(jnp.int32, (CH, CH), 0)
        c = lax.broadcasted_iota(jnp.int32, (CH, CH), 1)
        tri = ((r <= c) if reverse else (r >= c)).astype(F32)
        cs = jnp.dot(tri, v_ref[...], precision=HI, preferred_element_type=F32) + carry[...]
        o_ref[...] = cs
        edge = 0 if reverse else CH - 1
        carry[...] = jnp.sum(jnp.where(r == edge, cs, 0.0), axis=0, keepdims=True)

    return pl.pallas_call(
        body, name=name, grid=(B, NC),
        in_specs=[pl.BlockSpec((CH, CH), lambda b, c: (ix(b, c), 0))],
        out_specs=pl.BlockSpec((CH, CH), lambda b, c: (ix(b, c), 0)),
        out_shape=jax.ShapeDtypeStruct((T, CH), F32),
        scratch_shapes=[pltpu.VMEM((1, CH), F32)],
        compiler_params=_cp(("arbitrary", "arbitrary")),
    )(v)


def _fox_tb(Lp):
    return 384 if (Lp % 384 == 0 and Lp > 384) else CH


def _fox_keybias(cum, *, B, Lp, pad):
    ck = cum.reshape(B, Lp, CH)[:, :, SSD_H:SSD_H + FOX_H].transpose(0, 2, 1)
    pos = lax.broadcasted_iota(jnp.int32, ck.shape, 2)
    return jnp.where(pos < pad, -NEG, ck).reshape(B * FOX_H, 1, Lp)


def _fox_tril(TB):
    r = lax.broadcasted_iota(jnp.int32, (TB, TB), 0)
    c = lax.broadcasted_iota(jnp.int32, (TB, TB), 1)
    return r >= c


def _fox_fwd(qkv, cumT, *, B, Lp):
    TB = _fox_tb(Lp)
    NQ = Lp // TB
    T = B * Lp
    scale = FOX_DH ** -0.5

    def body(q_ref, k_ref, v_ref, ct_ref, o_ref, lse_ref):
        i = pl.program_id(2)
        q = q_ref[...]

        def block(j, nb, carry, diag):
            m, l, acc = carry
            off = pl.multiple_of(j * TB, TB)
            k = k_ref[pl.ds(off, nb * TB), :]
            v = v_ref[pl.ds(off, nb * TB), :]
            s = _nt(q, k) * scale - ct_ref[0, :, pl.ds(off, nb * TB)]
            if diag:
                s = jnp.where(_fox_tril(TB), s, NEG)
            m_new = jnp.maximum(m, jnp.max(s, axis=1, keepdims=True))
            p = jnp.exp(s - m_new)
            alpha = jnp.exp(m - m_new)
            l = alpha * l + jnp.sum(p, axis=1, keepdims=True)
            acc = alpha * acc + _nn(p.astype(BF16), v)
            return m_new, l, acc

        init = (jnp.full((TB, 1), NEG, F32), jnp.zeros((TB, 1), F32), jnp.zeros((TB, FOX_DH), F32))
        carry = lax.fori_loop(0, i // 2, lambda t, c: block(2 * t, 2, c, False), init)
        carry = lax.fori_loop(2 * (i // 2), i, lambda j, c: block(j, 1, c, False), carry)
        m, l, acc = block(i, 1, carry, True)
        o_ref[...] = (acc / l).astype(o_ref.dtype)
        lse_ref[0, 0] = m + jnp.log(l)

    return pl.pallas_call(
        body, name="fox_fwd", grid=(B, FOX_H, NQ),
        in_specs=[pl.BlockSpec((TB, FOX_DH), lambda b, h, i: (b * NQ + i, h)),
                  pl.BlockSpec((Lp, FOX_DH), lambda b, h, i: (b, FOX_H + h)),
                  pl.BlockSpec((Lp, FOX_DH), lambda b, h, i: (b, 2 * FOX_H + h)),
                  pl.BlockSpec((1, 1, Lp), lambda b, h, i: (b * FOX_H + h, 0, 0))],
        out_specs=[pl.BlockSpec((TB, FOX_DH), lambda b, h, i: (b * NQ + i, h)),
                   pl.BlockSpec((1, 1, TB, 1), lambda b, h, i: (b, h, i, 0))],
        out_shape=[jax.ShapeDtypeStruct((T, D), BF16),
                   jax.ShapeDtypeStruct((B, FOX_H, Lp, 1), F32)],
        compiler_params=_cp(("arbitrary", "arbitrary", "arbitrary")),
    )(qkv, qkv, qkv, cumT)


def _fox_bwd(qkv, dy, o, lse, cumT, *, B, Lp):
    TB = _fox_tb(Lp)
    NQ = Lp // TB
    T = B * Lp
    scale = FOX_DH ** -0.5

    def body(q_ref, k_ref, v_ref, dy_ref, o_ref, lse_ref, ct_ref, dq_ref, dk_ref, dv_ref, dck_ref, dcq_ref, dl_s):
        j = pl.program_id(2)
        k = k_ref[...]
        v = v_ref[...]
        ck = ct_ref[0]

        @pl.when(j == 0)
        def _():
            dq_ref[...] = jnp.zeros_like(dq_ref)
            dcq_ref[...] = jnp.zeros_like(dcq_ref)
            for i in range(NQ):
                sl = slice(i * TB, (i + 1) * TB)
                dl_s[sl, :] = jnp.sum(dy_ref[sl, :].astype(F32) * o_ref[sl, :].astype(F32), axis=1, keepdims=True)

        def block(i, nb, carry, diag):
            dk, dv, dck = carry
            off = pl.multiple_of(i * TB, TB)
            rows = pl.ds(off, nb * TB)
            q = q_ref[rows, :]
            dob = dy_ref[rows, :].astype(BF16)
            e = _nt(q, k) * scale - ck - lse_ref[0, 0, rows, :]
            if diag:
                e = jnp.where(_fox_tril(TB), e, NEG)
            p = jnp.exp(e)
            dv = dv + _tn(p.astype(BF16), dob)
            ds = p * (_nt(dob, v) - dl_s[rows, :])
            dsb = ds.astype(BF16)
            dk = dk + _tn(dsb, q)
            dq_ref[rows, :] += _nn(dsb, k) * scale
            dcq_ref[0, 0, rows, :] += jnp.sum(ds, axis=1, keepdims=True)
            dck = dck - jnp.sum(ds, axis=0, keepdims=True)
            return dk, dv, dck

        z = jnp.zeros((TB, FOX_DH), F32)
        carry = block(j, 1, (z, z, jnp.zeros((1, TB), F32)), True)
        npair = (NQ - 1 - j) // 2
        carry = lax.fori_loop(0, npair, lambda t, c: block(j + 1 + 2 * t, 2, c, False), carry)
        dk, dv, dck = lax.fori_loop(j + 1 + 2 * npair, NQ, lambda i, c: block(i, 1, c, False), carry)
        dk_ref[...] = (dk * scale).astype(dk_ref.dtype)
        dv_ref[...] = dv.astype(dv_ref.dtype)
        dck_ref[0] = dck

    head = lambda b, h, j: (b, h)
    return pl.pallas_call(
        body, name="fox_bwd", grid=(B, FOX_H, NQ),
        in_specs=[pl.BlockSpec((Lp, FOX_DH), head),
                  pl.BlockSpec((TB, FOX_DH), lambda b, h, j: (b * NQ + j, FOX_H + h)),
                  pl.BlockSpec((TB, FOX_DH), lambda b, h, j: (b * NQ + j, 2 * FOX_H + h)),
                  pl.BlockSpec((Lp, FOX_DH), head),
                  pl.BlockSpec((Lp, FOX_DH), head),
                  pl.BlockSpec((1, 1, Lp, 1), lambda b, h, j: (b, h, 0, 0)),
                  pl.BlockSpec((1, 1, TB), lambda b, h, j: (b * FOX_H + h, 0, j))],
        out_specs=[pl.BlockSpec((Lp, FOX_DH), head),
                   pl.BlockSpec((TB, FOX_DH), lambda b, h, j: (b * NQ + j, h)),
                   pl.BlockSpec((TB, FOX_DH), lambda b, h, j: (b * NQ + j, h)),
                   pl.BlockSpec((1, 1, TB), lambda b, h, j: (b * FOX_H + h, 0, j)),
                   pl.BlockSpec((1, 1, Lp, 1), lambda b, h, j: (b, h, 0, 0))],
        out_shape=[jax.ShapeDtypeStruct((T, D), F32),
                   jax.ShapeDtypeStruct((T, D), BF16),
                   jax.ShapeDtypeStruct((T, D), BF16),
                   jax.ShapeDtypeStruct((B * FOX_H, 1, Lp), F32),
                   jax.ShapeDtypeStruct((B, FOX_H, Lp, 1), F32)],
        scratch_shapes=[pltpu.VMEM((Lp, 1), F32)],
        compiler_params=_cp(("arbitrary", "arbitrary", "arbitrary")),
    )(qkv, qkv, qkv, dy, o, lse, cumT)


S5_TILE = 8


def _s5_pows(lam_ref, pw, tab, reverse):
    lr = lam_ref[0, :, 0:S5_W]
    li = lam_ref[0, :, S5_W:2 * S5_W]
    if reverse:
        li = -li
    ar, ai = lr, li
    sub = lax.broadcasted_iota(jnp.int32, (S5_TILE, 1), 0)
    for k, s in enumerate((1, 2, 4)):
        keep = (sub < S5_TILE - s) if reverse else (sub >= s)
        pw[k * S5_TILE:(k + 1) * S5_TILE, 0:S5_W] = jnp.where(keep, ar, 0.0)
        pw[k * S5_TILE:(k + 1) * S5_TILE, S5_W:2 * S5_W] = jnp.where(keep, ai, 0.0)
        ar, ai = ar * ar - ai * ai, 2.0 * ar * ai
    ar, ai = lr, li
    for r in range(S5_TILE):
        row = (S5_TILE - 1 - r) if reverse else r
        tab[row:row + 1, 0:S5_W] = ar
        tab[row:row + 1, S5_W:2 * S5_W] = ai
        ar, ai = ar * lr - ai * li, ar * li + ai * lr


def _s5_scan(hs, pw, tab, carry, reverse):
    n = hs.shape[0]
    tiles = list(range(n // S5_TILE))
    if reverse:
        tiles.reverse()
    for t in tiles:
        lo = t * S5_TILE
        vr = hs[lo:lo + S5_TILE, 0:S5_W]
        vi = hs[lo:lo + S5_TILE, S5_W:2 * S5_W]
        for k, s in enumerate((1, 2, 4)):
            sh = (S5_TILE - s) if reverse else s
            sr = pltpu.roll(vr, sh, 0)
            si = pltpu.roll(vi, sh, 0)
            ar = pw[k * S5_TILE:(k + 1) * S5_TILE, 0:S5_W]
            ai = pw[k * S5_TILE:(k + 1) * S5_TILE, S5_W:2 * S5_W]
            vr, vi = vr + ar * sr - ai * si, vi + ar * si + ai * sr
        hs[lo:lo + S5_TILE, 0:S5_W] = vr
        hs[lo:lo + S5_TILE, S5_W:2 * S5_W] = vi
    e8 = 0 if reverse else S5_TILE - 1
    l8r = tab[e8:e8 + 1, 0:S5_W]
    l8i = tab[e8:e8 + 1, S5_W:2 * S5_W]
    cr = carry[:, 0:S5_W]
    ci = carry[:, S5_W:2 * S5_W]
    states = []
    for t in tiles:
        states.append((cr, ci))
        edge = t * S5_TILE + e8
        er = hs[edge:edge + 1, 0:S5_W]
        ei = hs[edge:edge + 1, S5_W:2 * S5_W]
        cr, ci = er + l8r * cr - l8i * ci, ei + l8r * ci + l8i * cr
    carry[:, 0:S5_W] = cr
    carry[:, S5_W:2 * S5_W] = ci
    tr = tab[:, 0:S5_W]
    ti = tab[:, S5_W:2 * S5_W]
    for t, (cr, ci) in zip(tiles, states):
        lo = t * S5_TILE
        hs[lo:lo + S5_TILE, 0:S5_W] += tr * cr - ti * ci
        hs[lo:lo + S5_TILE, S5_W:2 * S5_W] += tr * ci + ti * cr


def _s5_rows(NC):
    return 3 * CH if (NC % 3 == 0 and NC > 3) else CH


def _s5_fwd(u, Bsg, Csg, lam, *, B, NC):
    T = B * NC * CH
    R = _s5_rows(NC)
    NR = NC * CH // R

    def body(u_ref, b_ref, c_ref, lam_ref, y_ref, h_ref, pw, tab, hs, carry):
        cidx = pl.program_id(2)

        @pl.when(cidx == 0)
        def _():
            _s5_pows(lam_ref, pw, tab, False)
            carry[...] = jnp.zeros_like(carry)

        hs[...] = _nn(u_ref[...].astype(BF16), b_ref[0])
        _s5_scan(hs, pw, tab, carry, False)
        hb = hs[...].astype(BF16)
        h_ref[...] = hb
        y_ref[...] = _nn(hb, c_ref[0])

    return pl.pallas_call(
        body, name="s5_fwd", grid=(B, S5_G // S5_SG, NR),
        in_specs=[pl.BlockSpec((R, CH), lambda b, s, c: (b * NR + c, s)),
                  pl.BlockSpec((1, CH, 2 * S5_W), lambda b, s, c: (s, 0, 0)),
                  pl.BlockSpec((1, 2 * S5_W, CH), lambda b, s, c: (s, 0, 0)),
                  pl.BlockSpec((1, 1, 2 * S5_W), lambda b, s, c: (s, 0, 0))],
        out_specs=[pl.BlockSpec((R, CH), lambda b, s, c: (b * NR + c, s)),
                   pl.BlockSpec((R, 2 * S5_W), lambda b, s, c: (b * NR + c, s))],
        out_shape=[jax.ShapeDtypeStruct((T, D), F32),
                   jax.ShapeDtypeStruct((T, (S5_G // S5_SG) * 2 * S5_W), BF16)],
        scratch_shapes=[pltpu.VMEM((3 * S5_TILE, 2 * S5_W), F32), pltpu.VMEM((S5_TILE, 2 * S5_W), F32),
                        pltpu.VMEM((R, 2 * S5_W), F32), pltpu.VMEM((1, 2 * S5_W), F32)],
        compiler_params=_cp(("arbitrary", "arbitrary", "arbitrary")),
    )(u, Bsg, Csg, lam)


def _s5_bwd(u, hst, dy, du_skip, Bsg, Csg, lam, *, B, NC):
    T = B * NC * CH
    NS = S5_G // S5_SG
    R = _s5_rows(NC)
    NR = NC * CH // R
    hb16 = R // 16

    def rix(b, c):
        return b * NR + (NR - 1 - c)

    def body(u_ref, h_ref, hp_ref, dy_ref, sk_ref, b_ref, c_ref, lam_ref,
             du_ref, db_ref, dc_ref, dl_ref, pw, tab, gs, carry):
        bidx = pl.program_id(1)
        cidx = pl.program_id(2)
        first = (bidx == 0) & (cidx == 0)

        @pl.when(cidx == 0)
        def _():
            _s5_pows(lam_ref, pw, tab, True)
            carry[...] = jnp.zeros_like(carry)

        dyb = dy_ref[...].astype(BF16)
        gs[...] = _nt(dyb, c_ref[0])
        _s5_scan(gs, pw, tab, carry, True)
        gr = gs[:, 0:S5_W]
        gi = gs[:, S5_W:]
        gb = gs[...].astype(BF16)
        du_ref[...] = (_nt(gb, b_ref[0]) + sk_ref[...]).astype(du_ref.dtype)
        ub = u_ref[...].astype(BF16)
        hcur = h_ref[...]
        dB = _tn(ub, gb)
        dC = _tn(dyb, hcur)
        hf = hcur.astype(F32)
        row = lax.broadcasted_iota(jnp.int32, (R, 1), 0)
        prev_last = jnp.where(cidx < NR - 1, hp_ref[15:16, :].astype(F32), 0.0)
        hprev = jnp.where(row == 0, prev_last, pltpu.roll(hf, 1, 0))
        pr = hprev[:, 0:S5_W]
        pi = hprev[:, S5_W:]
        da = _colsum(gr * pr + gi * pi)
        dbb = _colsum(gi * pr - gr * pi)
        dl = jnp.concatenate([da, dbb], axis=1)

        @pl.when(first)
        def _():
            db_ref[0] = dB
            dc_ref[0] = dC
            dl_ref[0] = dl

        @pl.when(jnp.logical_not(first))
        def _():
            db_ref[0] += dB
            dc_ref[0] += dC
            dl_ref[0] += dl

    return pl.pallas_call(
        body, name="s5_bwd", grid=(NS, B, NR),
        in_specs=[pl.BlockSpec((R, CH), lambda s, b, c: (rix(b, c), s)),
                  pl.BlockSpec((R, 2 * S5_W), lambda s, b, c: (rix(b, c), s)),
                  pl.BlockSpec((16, 2 * S5_W), lambda s, b, c: (jnp.maximum(rix(b, c) * hb16 - 1, 0), s)),
                  pl.BlockSpec((R, CH), lambda s, b, c: (rix(b, c), s)),
                  pl.BlockSpec((R, CH), lambda s, b, c: (rix(b, c), s)),
                  pl.BlockSpec((1, CH, 2 * S5_W), lambda s, b, c: (s, 0, 0)),
                  pl.BlockSpec((1, 2 * S5_W, CH), lambda s, b, c: (s, 0, 0)),
                  pl.BlockSpec((1, 1, 2 * S5_W), lambda s, b, c: (s, 0, 0))],
        out_specs=[pl.BlockSpec((R, CH), lambda s, b, c: (rix(b, c), s)),
                   pl.BlockSpec((1, CH, 2 * S5_W), lambda s, b, c: (s, 0, 0)),
                   pl.BlockSpec((1, CH, 2 * S5_W), lambda s, b, c: (s, 0, 0)),
                   pl.BlockSpec((1, 1, 2 * S5_W), lambda s, b, c: (s, 0, 0))],
        out_shape=[jax.ShapeDtypeStruct((T, D), BF16),
                   jax.ShapeDtypeStruct((NS, CH, 2 * S5_W), F32),
                   jax.ShapeDtypeStruct((NS, CH, 2 * S5_W), F32),
                   jax.ShapeDtypeStruct((NS, 1, 2 * S5_W), F32)],
        scratch_shapes=[pltpu.VMEM((3 * S5_TILE, 2 * S5_W), F32), pltpu.VMEM((S5_TILE, 2 * S5_W), F32),
                        pltpu.VMEM((R, 2 * S5_W), F32), pltpu.VMEM((1, 2 * S5_W), F32)],
        compiler_params=_cp(("arbitrary", "arbitrary", "arbitrary")),
    )(u, hst, hst, dy, du_skip, Bsg, Csg, lam)


def _s5_param_fn(lre, lim, lstep, bre, bim):
    step = jnp.exp(lstep)
    zr = lre * step
    zi = lim * step
    e = jnp.exp(zr)
    a = e * jnp.cos(zi)
    b = e * jnp.sin(zi)
    den = lre * lre + lim * lim
    qr = ((a - 1.0) * lre + b * lim) / den
    qi = (b * lre - (a - 1.0) * lim) / den
    return a, b, qr[None] * bre - qi[None] * bim, qr[None] * bim + qi[None] * bre


_S5_ROWS = S5_G * S5_P // CH


def _s5_tile(v):
    return v.reshape(_S5_ROWS, CH)


def _s5_tile_b(v):
    return v.reshape(S5_G * S5_P, S5_C).T.reshape(S5_C, _S5_ROWS, CH)


def _s5_untile_b(v):
    return v.reshape(S5_C, S5_G * S5_P).T.reshape(S5_G, S5_P, S5_C)


def _s5_params(lre, lim, lstep, bre, bim):
    def body(a_ref, b_ref, c_ref, d_ref, e_ref, o1, o2, o3, o4):
        outs = _s5_param_fn(a_ref[...], b_ref[...], c_ref[...], d_ref[...], e_ref[...])
        for o, v in zip((o1, o2, o3, o4), outs):
            o[...] = v

    shp = [jax.ShapeDtypeStruct(lre.shape, F32)] * 2 + [jax.ShapeDtypeStruct(bre.shape, F32)] * 2
    return pl.pallas_call(body, name="s5_params", out_shape=shp, compiler_params=_cp())(lre, lim, lstep, bre, bim)


def _s5_params_bwd(lre, lim, lstep, bre, bim, da, db, dbr, dbi):
    def body(a_ref, b_ref, c_ref, d_ref, e_ref, g1, g2, g3, g4, o1, o2, o3, o4, o5):
        _, vjp = jax.vjp(_s5_param_fn, a_ref[...], b_ref[...], c_ref[...], d_ref[...], e_ref[...])
        outs = vjp((g1[...], g2[...], g3[...], g4[...]))
        for o, v in zip((o1, o2, o3, o4, o5), outs):
            o[...] = v

    shp = [jax.ShapeDtypeStruct(lre.shape, F32)] * 3 + [jax.ShapeDtypeStruct(bre.shape, F32)] * 2
    return pl.pallas_call(body, name="s5_params_bwd", out_shape=shp, compiler_params=_cp())(
        lre, lim, lstep, bre, bim, da, db, dbr, dbi)


def _s5_blockdiag(br, bi, cre, cim):
    NS = S5_G // S5_SG
    eye = jnp.eye(S5_SG, dtype=F32)

    def bmat(v):
        v = v.reshape(NS, S5_SG, S5_P, S5_C)
        m = jnp.einsum("sgpc,gh->sgchp", v, eye)
        return m.reshape(NS, S5_SG * S5_C, S5_SG * S5_P)

    def cmat(v):
        v = v.reshape(NS, S5_SG, S5_C, S5_P)
        m = jnp.einsum("sgcp,gh->sgphc", v, eye)
        return m.reshape(NS, S5_SG * S5_P, S5_SG * S5_C)

    Bsg = jnp.concatenate([bmat(br), bmat(bi)], axis=2).astype(BF16)
    Csg = jnp.concatenate([cmat(cre), cmat(-cim)], axis=1).astype(BF16)
    return Bsg, Csg


def _s5_unblock(dBsg, dCsgT):
    NS = S5_G // S5_SG

    def diag(m):
        m = m.reshape(NS, S5_SG, S5_C, S5_SG, S5_P)
        return jnp.stack([m[:, g, :, g, :] for g in range(S5_SG)], axis=1)

    def ub(m):
        return diag(m).transpose(0, 1, 3, 2).reshape(S5_G, S5_P, S5_C)

    def uc(m):
        return diag(m).reshape(S5_G, S5_C, S5_P)

    dbr = ub(dBsg[:, :, 0:S5_W])
    dbi = ub(dBsg[:, :, S5_W:])
    dcr = uc(dCsgT[:, :, 0:S5_W])
    dci = -uc(dCsgT[:, :, S5_W:])
    return dbr, dbi, dcr, dci


def _loss_head(x, nf, target, *, B, NC, S):
    T = B * NC * CH
    nts = S // CH

    def f(xv, w, t):
        y = _rms(xv, w)
        return 0.5 * _colsum(jnp.mean(jnp.square(y - t), axis=-1, keepdims=True))

    def body(x_ref, w_ref, t_ref, dx_ref, ls_ref, dw_ref):
        i = pl.program_id(0)
        on = (i % NC) > 0
        t = t_ref[...]
        l, vjp = jax.vjp(lambda a, b: f(a, b, t), x_ref[...], w_ref[...])
        dx, dw = vjp(jnp.ones((1, 1), F32))
        g = jnp.where(on, 1.0, 0.0)
        dx_ref[...] = dx * g
        lv = jnp.zeros((1, CH), F32) + l * g

        @pl.when(i == 0)
        def _():
            ls_ref[...] = lv
            dw_ref[...] = dw * g

        @pl.when(i > 0)
        def _():
            ls_ref[...] += lv
            dw_ref[...] += dw * g

    def tix(i):
        return ((i // NC) * nts + jnp.maximum(i % NC - 1, 0), 0)

    return pl.pallas_call(
        body, name="loss_head", grid=(B * NC,),
        in_specs=[pl.BlockSpec((CH, D), lambda i: (i, 0)),
                  pl.BlockSpec((1, D), lambda i: (0, 0)),
                  pl.BlockSpec((CH, D), tix)],
        out_specs=[pl.BlockSpec((CH, D), lambda i: (i, 0)),
                   pl.BlockSpec((1, CH), lambda i: (0, 0)),
                   pl.BlockSpec((1, D), lambda i: (0, 0))],
        out_shape=[jax.ShapeDtypeStruct((T, D), F32),
                   jax.ShapeDtypeStruct((1, CH), F32),
                   jax.ShapeDtypeStruct((1, D), F32)],
        compiler_params=_cp(("arbitrary",)),
    )(x, nf, target)


def _ew(fn, ins, n_out, out_dtypes, *, name, tile=None):
    R, C = ins[0].shape
    tile = tile or _pick(R, (512, 256, 128, 64, 32, 16, 8, 1))
    if tile % 8 != 0:
        tile = R

    def body(*refs):
        outs = fn(*[r[...] for r in refs[:len(ins)]])
        if not isinstance(outs, (tuple, list)):
            outs = (outs,)
        for r, v in zip(refs[len(ins):], outs):
            r[...] = v.astype(r.dtype)

    spec = pl.BlockSpec((tile, C), lambda i: (i, 0))
    res = pl.pallas_call(
        body, name=name, grid=(R // tile,), in_specs=[spec] * len(ins), out_specs=[spec] * n_out,
        out_shape=[jax.ShapeDtypeStruct((R, C), dt) for dt in out_dtypes],
        compiler_params=_cp(("parallel",)),
    )(*ins)
    return res


def _adam_fn(w, g, m, v):
    m = ADAM_B1 * m + (1.0 - ADAM_B1) * g
    v = ADAM_B2 * v + (1.0 - ADAM_B2) * jnp.square(g)
    m_hat = m / (1.0 - ADAM_B1 ** ADAM_STEP)
    v_hat = v / (1.0 - ADAM_B2 ** ADAM_STEP)
    delta = -ADAM_LR * (m_hat / (jnp.sqrt(v_hat) + ADAM_EPS) + ADAM_WD * w)
    return delta, m, v


def _adam(w, g, m, v, name):
    shp = w.shape
    C = shp[-1]
    f = lambda a: a.reshape(-1, C)
    d, nm, nv = _ew(_adam_fn, [f(w), f(g), f(m), f(v)], 3, [F32] * 3, name=name)
    return d.reshape(shp), nm.reshape(shp), nv.reshape(shp)


def _me():
    return lax.axis_index("x"), lax.axis_index("y"), lax.axis_index("c")


def _all_gather(v, name):
    def body(x_ref, out_ref, send_sems, recv_sems, local_sem):
        x, y, c = _me()
        me, sibling = (x, y, c), (x, y, 1 - c)
        chips = [(1 - x, y), (x, 1 - y), (1 - x, 1 - y)]

        def slot(px, py, pc):
            return out_ref.at[4 * px + 2 * py + pc]

        def copy(k, block, to, src=None):
            return pltpu.make_async_remote_copy(
                src_ref=slot(*block) if src is None else src, dst_ref=slot(*block),
                send_sem=send_sems.at[k], recv_sem=recv_sems.at[k], device_id=to, device_id_type=MESH)

        mine = pltpu.make_async_copy(x_ref, slot(*me), local_sem)
        mine.start()
        first = [copy(0, me, sibling, src=x_ref)]
        first += [copy(1 + j, me, (*chip, c), src=x_ref) for j, chip in enumerate(chips)]
        for cp in first:
            cp.start()
        passed = [copy(4 + j, (*chip, c), sibling) for j, chip in enumerate(chips)]
        for j, chip in enumerate(chips):
            copy(1 + j, (*chip, c), me).wait_recv()
            passed[j].start()
        copy(0, sibling, me).wait_recv()
        for j, chip in enumerate(chips):
            copy(4 + j, (*chip, 1 - c), me).wait_recv()
        for cp in first + passed:
            cp.wait_send()
        mine.wait()

    return pl.pallas_call(
        body, name=name, out_shape=jax.ShapeDtypeStruct((N_DEV,) + v.shape, v.dtype),
        in_specs=[pl.BlockSpec(memory_space=pl.ANY)], out_specs=pl.BlockSpec(memory_space=pl.ANY),
        scratch_shapes=[pltpu.SemaphoreType.DMA((7,)), pltpu.SemaphoreType.DMA((7,)), pltpu.SemaphoreType.DMA],
    )(v)


def _swap_core(g, name):
    def body(g_ref, out_ref, send_sems, recv_sems):
        x, y, c = _me()
        cps = [pltpu.make_async_remote_copy(
            src_ref=g_ref.at[q, 1 - c], dst_ref=out_ref.at[q], send_sem=send_sems.at[q], recv_sem=recv_sems.at[q],
            device_id=(x, y, 1 - c), device_id_type=MESH) for q in range(4)]
        for cp in cps:
            cp.start()
        for cp in cps:
            cp.wait()

    return pl.pallas_call(
        body, name=name, out_shape=jax.ShapeDtypeStruct((4,) + g.shape[2:], g.dtype),
        in_specs=[pl.BlockSpec(memory_space=pl.ANY)], out_specs=pl.BlockSpec(memory_space=pl.ANY),
        scratch_shapes=[pltpu.SemaphoreType.DMA((4,)), pltpu.SemaphoreType.DMA((4,))],
    )(g)


def _swap_chips(hb, name):
    flips = [(1, 0), (0, 1), (1, 1)]

    def body(h_ref, out_ref, send_sems, recv_sems):
        x, y, c = _me()
        cps = []
        for j, (fx, fy) in enumerate(flips):
            px = x + fx - 2 * x * fx
            py = y + fy - 2 * y * fy
            cps.append(pltpu.make_async_remote_copy(
                src_ref=h_ref.at[2 * px + py], dst_ref=out_ref.at[j], send_sem=send_sems.at[j],
                recv_sem=recv_sems.at[j], device_id=(px, py, c), device_id_type=MESH))
        for cp in cps:
            cp.start()
        for cp in cps:
            cp.wait()

    return pl.pallas_call(
        body, name=name, out_shape=jax.ShapeDtypeStruct((3,) + hb.shape[1:], hb.dtype),
        in_specs=[pl.BlockSpec(memory_space=pl.ANY)], out_specs=pl.BlockSpec(memory_space=pl.ANY),
        scratch_shapes=[pltpu.SemaphoreType.DMA((3,)), pltpu.SemaphoreType.DMA((3,))],
    )(hb)


def _reduce_scatter(g8, tag):
    shard = g8.shape[1:]
    C = shard[-1]
    x, y, c = _me()
    g4 = g8.reshape((4, 2) + shard)
    got = _swap_core(g4, "rs_core_" + tag)
    mine = lax.dynamic_index_in_dim(g4, c, axis=1, keepdims=False)
    h, hb = _ew(lambda a, b: (a + b, a + b), [mine.reshape(-1, C), got.reshape(-1, C)], 2, [F32, BF16],
                name="rs_pair_sum_" + tag)
    own = lax.dynamic_index_in_dim(h.reshape((4,) + shard), 2 * x + y, axis=0, keepdims=False)
    got3 = _swap_chips(hb.reshape((4,) + shard), "rs_chips_" + tag)
    out, = _ew(lambda a, b, c_, d: a + b.astype(F32) + c_.astype(F32) + d.astype(F32),
               [own.reshape(-1, C)] + [got3[j].reshape(-1, C) for j in range(3)], 1, [F32],
               name="rs_chip_sum_" + tag)
    return out.reshape(shard)


def _sum8(a):
    out, = _ew(lambda *v: functools.reduce(lambda p, q: p + q, v), [a[k] for k in range(N_DEV)], 1, [F32],
               name="sum8")
    return out


def _pad_rows(flat, cols, mult):
    n = flat.shape[0]
    per = cols * mult
    tot = ((n + per - 1) // per) * per
    return jnp.pad(flat, (0, tot - n)).reshape(-1, cols)


class _Packer:
    def __init__(self, shapes, mult):
        self.shapes = shapes
        self.sizes = [int(np.prod(s)) for s in shapes]
        self.rows = [8 * ((n + 8 * D - 1) // (8 * D)) for n in self.sizes]
        tot = sum(self.rows)
        self.tail = (-tot) % mult

    def pack(self, arrs, dtype):
        parts = [_pad_rows(a.reshape(-1).astype(dtype), D, 8) for a in arrs]
        if self.tail:
            parts.append(jnp.zeros((self.tail, D), dtype))
        return jnp.concatenate(parts, axis=0)

    def unpack(self, buf):
        out, o = [], 0
        for s, n, r in zip(self.shapes, self.sizes, self.rows):
            out.append(buf[o:o + r].reshape(-1)[:n].reshape(s))
            o += r
        return out


def _w_cat(w_in_l):
    sm = jnp.concatenate([w_in_l[:, O_DT:O_DT + SSD_H], w_in_l[:, O_F:O_F + FOX_H],
                          jnp.zeros((D, CH - SSD_H - FOX_H), w_in_l.dtype)], axis=1)
    return jnp.concatenate([w_in_l[:, O_Z:O_XBC], w_in_l[:, O_XBC:O_DT], w_in_l[:, O_QKV:O_F],
                            w_in_l[:, O_U:O_G], w_in_l[:, O_G:D_IN], sm], axis=1)


def _w_uncat(g):
    return jnp.concatenate([g[:, OFF_Z:OFF_XBC], g[:, OFF_XBC:OFF_QKV], g[:, OFF_SM:OFF_SM + SSD_H],
                            g[:, OFF_QKV:OFF_U], g[:, OFF_SM + SSD_H:OFF_SM + SSD_H + FOX_H],
                            g[:, OFF_U:OFF_G], g[:, OFF_G:OFF_SM]], axis=1)


def _layer_fwd(x, p, geom, dims):
    B, NC, Lp, pad = dims
    T = geom[0]
    rm = functools.partial(_rowmap, geom=geom)
    sv = {}
    xn1, = rm(_f_norm, [x], [p["norm1"]], [(D, BF16)], [], tile=384 if Lp % 384 == 0 else CH, name="norm1")
    Wc = p["w_cat"]
    pz = _mm(xn1, Wc, "nn", BF16, n=D, b_off=OFF_Z, name="in_z")
    pxbc = _mm(xn1, Wc, "nn", F32, n=CONV_DIM, b_off=OFF_XBC, name="in_xbc")
    qkv = _mm(xn1, Wc, "nn", BF16, n=3 * D, b_off=OFF_QKV, name="in_qkv")
    pu = _mm(xn1, Wc, "nn", BF16, n=D, b_off=OFF_U, name="in_u")
    pg = _mm(xn1, Wc, "nn", BF16, n=3 * D, b_off=OFF_G, name="in_g")
    psm = _mm(xn1, Wc, "nn", F32, n=CH, b_off=OFF_SM, name="in_sm")
    t_r = 384 if Lp % 384 == 0 else CH
    sm, = rm(_smallact, [psm], [p["smallbias"]], [(CH, F32)], [], tile=t_r, name="smallact")
    xbc = _conv_fwd(pxbc, p["conv_w"], p["conv_b"], geom=geom, tile=CH)
    y_ssd, states = _ssd_fwd(xbc, sm, p["a_log"], B=B, NC=NC)
    y_a, = rm(_ssd_post, [y_ssd, (xbc, D, 0), pz], [p["d_rep"], p["ssd_norm"]], [(D, BF16)], [], tile=t_r,
              name="ssd_post")
    cum = _cumsum_seq(sm, B=B, NC=NC, reverse=False, name="fox_cum")
    cumT = _fox_keybias(cum, B=B, Lp=Lp, pad=pad)
    y_b, lse = _fox_fwd(qkv, cumT, B=B, Lp=Lp)
    y_ssm, hst = _s5_fwd(pu, p["Bsg"], p["Csg"], p["lam"], B=B, NC=NC)
    y1, = rm(_s5_pre, [y_ssm, pu], [p["s5_d"]], [(D, BF16)], [], tile=t_r, name="s5_pre")
    tg = _mm(y1, p["w_glu"], "nn", BF16, name="s5_glu_mm")
    y_c, = rm(_s5_glu, [y_ssm, pu, tg], [p["s5_d"]], [(D, BF16)], [], tile=t_r, name="s5_glu")
    br = [_mm(yy, p["w_branch"][n], "nn", BF16, name=f"branch{n}") for n, yy in enumerate((y_a, y_b, y_c))]
    mix, = rm(_merge, [(pg, D, 0), (pg, D, 1), (pg, D, 2)] + br, [], [(D, BF16)], [], tile=CH, name="merge")
    x_mid = _mm(mix, p["w_out"], "nn", F32, res=x, name="out_proj")
    xn2, = rm(_f_norm, [x_mid], [p["norm2"]], [(D, BF16)], [], tile=t_r, name="norm2")
    hff = _mm(xn2, p["w_ffn_in"], "nn", BF16, name="ffn_in")
    act, = rm(_swiglu, [(hff, DFF, 0), (hff, DFF, 1)], [], [(DFF, BF16)], [], tile=CH, name="swiglu")
    x_out = _mm(act, p["w_ffn_out"], "nn", F32, res=x_mid, name="ffn_out")
    sv.update(x=x, xn1=xn1, pz=pz, pxbc=pxbc, qkv=qkv, pu=pu, pg=pg, psm=psm, sm=sm, xbc=xbc, y_ssd=y_ssd,
              states=states, y_a=y_a, cum=cum, cumT=cumT, y_b=y_b, lse=lse, y_ssm=y_ssm, hst=hst, y1=y1, tg=tg,
              y_c=y_c, br=br, mix=mix, x_mid=x_mid, xn2=xn2, hff=hff, act=act)
    return x_out, sv


def _layer_bwd(dx_out, p, sv, geom, dims):
    B, NC, Lp, pad = dims
    T = geom[0]
    rm = functools.partial(_rowmap, geom=geom)
    t_r = 384 if Lp % 384 == 0 else CH
    g = {}
    dact = _mm(dx_out, p["w_ffn_out"], "nt", BF16, name="ffn_out_dx")
    g["w_ffn_out"] = _mm(sv["act"], dx_out, "tn", F32, name="ffn_out_dw")
    dhff, = rm(_b_swiglu, [(sv["hff"], DFF, 0), (sv["hff"], DFF, 1), dact], [], [(2 * DFF, BF16)], [], tile=CH,
               name="swiglu_bwd")
    dxn2 = _mm(dhff, p["w_ffn_in"], "nt", F32, name="ffn_in_dx")
    g["w_ffn_in"] = _mm(sv["xn2"], dhff, "tn", F32, name="ffn_in_dw")
    dx_mid, g["norm2"] = rm(_b_norm, [sv["x_mid"], dxn2, dx_out], [p["norm2"]], [(D, F32)], [(1, D)], tile=t_r,
                            name="norm2_bwd")
    dmix = _mm(dx_mid, p["w_out"], "nt", BF16, name="out_proj_dx")
    g["w_out"] = _mm(sv["mix"], dx_mid, "tn", F32, name="out_proj_dw")
    pg = sv["pg"]
    dpg, db0, db1, db2 = rm(_b_merge, [(pg, D, 0), (pg, D, 1), (pg, D, 2)] + sv["br"] + [dmix], [],
                            [(3 * D, BF16), (D, BF16), (D, BF16), (D, BF16)], [], tile=CH, name="merge_bwd")
    ys = (sv["y_a"], sv["y_b"], sv["y_c"])
    dbs = (db0, db1, db2)
    g["w_branch"] = [_mm(ys[n], dbs[n], "tn", F32, name=f"branch{n}_dw") for n in range(3)]
    dy = [_mm(dbs[n], p["w_branch"][n], "nt", BF16, name=f"branch{n}_dx") for n in range(3)]
    dtg, dy1a = rm(_b_s5_glu, [sv["y_ssm"], sv["pu"], sv["tg"], dy[2]], [p["s5_d"]], [(D, BF16), (D, F32)], [],
                   tile=t_r, name="s5_glu_bwd")
    dy1b = _mm(dtg, p["w_glu"], "nt", BF16, name="s5_glu_mm_dx")
    g["w_glu"] = _mm(sv["y1"], dtg, "tn", F32, name="s5_glu_mm_dw")
    dys, du_skip, g["s5_d"] = rm(_b_s5_pre, [sv["y_ssm"], sv["pu"], dy1a, dy1b], [p["s5_d"]],
                                 [(D, F32), (D, F32)], [(1, D)], tile=t_r, name="s5_pre_bwd")
    du, g["Bsg"], g["Csg"], g["lam"] = _s5_bwd(sv["pu"], sv["hst"], dys, du_skip, p["Bsg"], p["Csg"], p["lam"],
                                               B=B, NC=NC)
    dq, dk, dv, dckT, dcq = _fox_bwd(sv["qkv"], dy[1], sv["y_b"], sv["lse"], sv["cumT"], B=B, Lp=Lp)
    dcum8 = dcq.reshape(B, FOX_H, Lp).transpose(0, 2, 1) + dckT.reshape(B, FOX_H, Lp).transpose(0, 2, 1)
    dcum = jnp.pad(dcum8.reshape(T, FOX_H), ((0, 0), (SSD_H, CH - SSD_H - FOX_H)))
    dlogf = _cumsum_seq(dcum, B=B, NC=NC, reverse=True, name="fox_cum_bwd")
    dy_ssd, dxs_skip, dz, g["d_rep"], g["ssd_norm"] = rm(
        _b_ssd_post, [sv["y_ssd"], (sv["xbc"], D, 0), sv["pz"], dy[0]], [p["d_rep"], p["ssd_norm"]],
        [(D, F32), (D, F32), (D, BF16)], [(1, D), (1, D)], tile=t_r, name="ssd_post_bwd")
    dxbc_act, ddt, g["a_log"] = _ssd_bwd(sv["xbc"], sv["sm"], p["a_log"], sv["states"], dy_ssd, dxs_skip, B=B, NC=NC)
    dpsm, g["smallbias"] = rm(_b_smallact, [sv["psm"], ddt, dlogf], [p["smallbias"]], [(CH, BF16)], [(1, CH)],
                              tile=t_r, name="smallact_bwd")
    dconv, g["conv_w"], g["conv_b"] = _conv_bwd_pre(sv["pxbc"], dxbc_act, p["conv_w"], p["conv_b"], geom=geom, tile=CH)
    dpxbc = _conv_bwd_x(dconv, p["conv_w"], geom=geom, tile=CH)
    dproj = jnp.concatenate([dz, dpxbc, dq.astype(BF16), dk, dv, du, dpg, dpsm], axis=1)
    dxn1 = _mm(dproj, p["w_cat"], "nt", F32, name="in_dx")
    g["w_cat"] = _mm(sv["xn1"], dproj, "tn", F32, name="in_dw")
    dx_in, g["norm1"] = rm(_b_norm, [sv["x"], dxn1, dx_mid], [p["norm1"]], [(D, F32)], [(1, D)], tile=t_r,
                           name="norm1_bwd")
    return dx_in, g


_BIG = ["w_in", "s5_w_glu", "w_branch", "w_out", "w_ffn_in", "w_ffn_out"]
_NAMES = ['meta', 'norm1', 'w_in', 'ssd_conv_w', 'ssd_conv_b', 'ssd_dt_bias', 'ssd_a_log', 'ssd_d', 'ssd_norm',
          'fox_bf', 's5_lam_re', 's5_lam_im', 's5_b_re', 's5_b_im', 's5_c_re', 's5_c_im', 's5_log_step', 's5_d',
          's5_w_glu', 'w_branch', 'w_out', 'norm2', 'w_ffn_in', 'w_ffn_out', 'norm_f']
_SHARD_AXIS = {"meta": 1, "ssd_conv_w": 2}
_BIG_AXIS = {"w_in": 2, "s5_w_glu": 1, "w_branch": 2, "w_out": 1, "w_ffn_in": 2, "w_ffn_out": 1}


def kernel(x, meta, norm1, w_in, ssd_conv_w, ssd_conv_b, ssd_dt_bias, ssd_a_log, ssd_d, ssd_norm, fox_bf, s5_lam_re, s5_lam_im, s5_b_re, s5_b_im, s5_c_re, s5_c_im, s5_log_step, s5_d, s5_w_glu, w_branch, w_out, norm2, w_ffn_in, w_ffn_out, norm_f, loss_target, m_meta, m_norm1, m_w_in, m_ssd_conv_w, m_ssd_conv_b, m_ssd_dt_bias, m_ssd_a_log, m_ssd_d, m_ssd_norm, m_fox_bf, m_s5_lam_re, m_s5_lam_im, m_s5_b_re, m_s5_b_im, m_s5_c_re, m_s5_c_im, m_s5_log_step, m_s5_d, m_s5_w_glu, m_w_branch, m_w_out, m_norm2, m_w_ffn_in, m_w_ffn_out, m_norm_f, v_meta, v_norm1, v_w_in, v_ssd_conv_w, v_ssd_conv_b, v_ssd_dt_bias, v_ssd_a_log, v_ssd_d, v_ssd_norm, v_fox_bf, v_s5_lam_re, v_s5_lam_im, v_s5_b_re, v_s5_b_im, v_s5_c_re, v_s5_c_im, v_s5_log_step, v_s5_d, v_s5_w_glu, v_w_branch, v_w_out, v_norm2, v_w_ffn_in, v_w_ffn_out, v_norm_f):
    args = locals()
    W = {n: args[n] for n in _NAMES}
    Mo = {n: args["m_" + n] for n in _NAMES}
    Vo = {n: args["v_" + n] for n in _NAMES}
    B, S, _ = x.shape
    depth = norm1.shape[0]
    L = S + N_META
    Lp = ((L + CH - 1) // CH) * CH
    pad = Lp - L
    assert pad + N_META == CH and S % CH == 0
    NC = Lp // CH
    T = B * Lp
    geom = (T, Lp, pad)
    dims = (B, NC, Lp, pad)
    xi, yi, ci = _me()
    dev = 4 * xi + 2 * yi + ci

    gath = {n: _all_gather(W[n].astype(BF16), "gather_" + n) for n in _BIG}
    full = {n: jnp.concatenate([gath[n][k] for k in range(N_DEV)], axis=_BIG_AXIS[n]) for n in _BIG}
    sm_pack = _Packer([meta.shape, ssd_conv_w.shape], 8)
    sm_g = _all_gather(sm_pack.pack([meta, ssd_conv_w], F32), "gather_small")
    sm_parts = [sm_pack.unpack(sm_g[k]) for k in range(N_DEV)]
    meta_full = jnp.concatenate([sm_parts[k][0] for k in range(N_DEV)], axis=1)
    conv_w_full = jnp.concatenate([sm_parts[k][1] for k in range(N_DEV)], axis=2)

    layers = []
    s5_in = []
    for l in range(depth):
        lre = _s5_tile(s5_lam_re[l])
        lim = _s5_tile(s5_lam_im[l])
        lst = _s5_tile(jnp.repeat(s5_log_step[l], S5_P))
        bre = _s5_tile_b(s5_b_re[l])
        bim = _s5_tile_b(s5_b_im[l])
        s5_in.append((lre, lim, lst, bre, bim))
        a, b, br_, bi_ = _s5_params(lre, lim, lst, bre, bim)
        Bsg, Csg = _s5_blockdiag(_s5_untile_b(br_), _s5_untile_b(bi_), s5_c_re[l], s5_c_im[l])
        NS = S5_G // S5_SG
        lam = jnp.concatenate([a.reshape(NS, 1, S5_W), b.reshape(NS, 1, S5_W)], axis=2)
        zpad = jnp.zeros((CH - SSD_H - FOX_H,), F32)
        layers.append(dict(
            norm1=norm1[l][None], w_cat=_w_cat(full["w_in"][l]),
            smallbias=jnp.concatenate([ssd_dt_bias[l], fox_bf[l], zpad])[None],
            conv_w=conv_w_full[l], conv_b=ssd_conv_b[l][None],
            a_log=jnp.concatenate([ssd_a_log[l], jnp.zeros((CH - SSD_H,), F32)])[None],
            d_rep=jnp.repeat(ssd_d[l], SSD_P)[None], ssd_norm=ssd_norm[l][None],
            Bsg=Bsg, Csg=Csg, lam=lam, s5_d=s5_d[l][None], w_glu=full["s5_w_glu"][l],
            w_branch=[full["w_branch"][l, n] for n in range(3)], w_out=full["w_out"][l],
            norm2=norm2[l][None], w_ffn_in=full["w_ffn_in"][l], w_ffn_out=full["w_ffn_out"][l]))

    xs = jnp.concatenate([jnp.zeros((B, pad, D), F32), jnp.broadcast_to(meta_full[None], (B, N_META, D)), x], axis=1)
    h = xs.reshape(T, D)
    saved = []
    for l in range(depth):
        h, sv = _layer_fwd(h, layers[l], geom, dims)
        saved.append(sv)
    dh, loss_row, g_nf = _loss_head(h, norm_f[None], loss_target.reshape(B * S, D), B=B, NC=NC, S=S)
    loss = lax.psum(loss_row[0, 0], AXES)

    G = {n: [None] * depth for n in _NAMES}
    for l in reversed(range(depth)):
        dh, g = _layer_bwd(dh, layers[l], saved[l], geom, dims)
        saved[l] = None
        G["norm1"][l] = g["norm1"][0]
        G["norm2"][l] = g["norm2"][0]
        G["w_in"][l] = _w_uncat(g["w_cat"])
        G["ssd_conv_w"][l] = g["conv_w"]
        G["ssd_conv_b"][l] = g["conv_b"][0]
        G["ssd_dt_bias"][l] = g["smallbias"][0, 0:SSD_H]
        G["fox_bf"][l] = g["smallbias"][0, SSD_H:SSD_H + FOX_H]
        G["ssd_a_log"][l] = g["a_log"][0, 0:SSD_H]
        G["ssd_d"][l] = g["d_rep"].reshape(SSD_H, SSD_P).sum(axis=1)
        G["ssd_norm"][l] = g["ssd_norm"][0]
        dbr, dbi, dcr, dci = _s5_unblock(g["Bsg"], g["Csg"])
        da = _s5_tile(g["lam"][:, 0, 0:S5_W])
        db = _s5_tile(g["lam"][:, 0, S5_W:])
        dlre, dlim, dlst, dbre, dbim = _s5_params_bwd(*s5_in[l], da, db, _s5_tile_b(dbr), _s5_tile_b(dbi))
        G["s5_lam_re"][l] = dlre.reshape(S5_G, S5_P)
        G["s5_lam_im"][l] = dlim.reshape(S5_G, S5_P)
        G["s5_log_step"][l] = dlst.reshape(S5_G, S5_P).sum(axis=1)
        G["s5_b_re"][l] = _s5_untile_b(dbre)
        G["s5_b_im"][l] = _s5_untile_b(dbim)
        G["s5_c_re"][l] = dcr
        G["s5_c_im"][l] = dci
        G["s5_d"][l] = g["s5_d"][0]
        G["s5_w_glu"][l] = g["w_glu"]
        G["w_branch"][l] = jnp.stack(g["w_branch"])
        G["w_out"][l] = g["w_out"]
        G["w_ffn_in"][l] = g["w_ffn_in"]
        G["w_ffn_out"][l] = g["w_ffn_out"]
    dxs = dh.reshape(B, Lp, D)
    grad_x = dxs[:, pad + N_META:, :]
    part = {n: jnp.stack(G[n]) for n in _NAMES if n not in ("meta", "norm_f")}
    part["meta"] = dxs[:, pad:pad + N_META, :].sum(axis=0)
    part["norm_f"] = g_nf[0]

    grads = {}
    for n in _BIG:
        ax = _BIG_AXIS[n]
        a = part[n]
        a = a.reshape(a.shape[:ax] + (N_DEV, a.shape[ax] // N_DEV) + a.shape[ax + 1:])
        grads[n] = _reduce_scatter(jnp.moveaxis(a, ax, 0), n)

    small = [n for n in _NAMES if n not in _BIG]
    sp = _Packer([part[n].shape for n in small], 128)
    tot = sp.unpack(_sum8(_all_gather(sp.pack([part[n] for n in small], F32), "gather_small_grads")))
    for n, t in zip(small, tot):
        if n in _SHARD_AXIS:
            ax = _SHARD_AXIS[n]
            w = W[n].shape[ax]
            t = lax.dynamic_slice_in_dim(t, dev * w, w, axis=ax)
        grads[n] = t

    delta, new_m, new_v = {}, {}, {}
    for n in _BIG:
        delta[n], new_m[n], new_v[n] = _adam(W[n], grads[n], Mo[n], Vo[n], "adam_" + n)
    ap = _Packer([W[n].shape for n in small], 128)
    d_, m_, v_ = _adam(ap.pack([W[n] for n in small], F32), ap.pack([grads[n] for n in small], F32),
                       ap.pack([Mo[n] for n in small], F32), ap.pack([Vo[n] for n in small], F32), "adam_small")
    for n, a, b, c in zip(small, ap.unpack(d_), ap.unpack(m_), ap.unpack(v_)):
        delta[n], new_m[n], new_v[n] = a, b, c
    return (loss, grad_x, *[grads[n] for n in _NAMES], *[delta[n] for n in _NAMES],
            *[new_m[n] for n in _NAMES], *[new_v[n] for n in _NAMES])
```

```python
import functools
import math

import numpy as np
import jax
import jax.numpy as jnp
from jax import lax
from jax.experimental import pallas as pl
from jax.experimental.pallas import tpu as pltpu

F32 = jnp.float32
BF16 = jnp.bfloat16
AXES = ("x", "y", "c")
MESH = pl.DeviceIdType.MESH
N_DEV = 8

D = 1024
N_META = 16
CH = 128
EPS = 1e-6
NEG = -1e30
SSD_H, SSD_P, SSD_N, SSD_G = 16, 64, 128, 2
CONV_K, CONV_DIM = 4, 1536
FOX_H, FOX_DH = 8, 128
S5_G, S5_P, S5_C = 64, 64, 16
S5_SG = 8
S5_W = S5_SG * S5_P
DFF = 2816
D_IN = 9752
OFF_Z, OFF_XBC, OFF_QKV, OFF_U, OFF_G, OFF_SM, D_CAT = 0, 1024, 2560, 5632, 6656, 9728, 9856
O_Z, O_XBC, O_DT, O_QKV, O_F, O_U, O_G = 0, 1024, 2560, 2576, 5648, 5656, 6680

ADAM_LR, ADAM_B1, ADAM_B2, ADAM_EPS, ADAM_WD, ADAM_STEP = 0.001, 0.9, 0.999, 1e-08, 0.01, 10

VMEM_LIMIT_V7X = 52 * 1024 * 1024
HI = lax.Precision.HIGHEST


def _cp(sem=None):
    return pltpu.CompilerParams(dimension_semantics=sem, vmem_limit_bytes=VMEM_LIMIT_V7X)


def _pick(n, cands):
    for c in cands:
        if n % c == 0:
            return c
    raise ValueError(f"no tile for {n}")


_TILES = (1408, 1024, 896, 768, 512, 384, 256, 128)


def _mm(a, b, mode, out_dtype, *, name, n=None, b_off=0, res=None, tm=None, tn=None, tk=None):
    if mode == "tn":
        K, M = a.shape
    else:
        M, K = a.shape
    if mode == "nt":
        N = b.shape[0]
    else:
        N = n if n is not None else b.shape[1]
    wide = (1408,) if mode == "tn" else ()
    tm = tm or _pick(M, wide + (1024, 768, 512, 384, 256, 128, 64, 16, 8))
    tn = tn or _pick(math.gcd(N, b_off) if b_off else N, wide + (1024, 896, 768, 512, 384, 256, 128))
    tk = tk or _pick(K, _TILES)
    nk = K // tk
    joff = b_off // tn

    def body(*refs):
        if res is None:
            a_ref, b_ref, o_ref, acc = refs
            r_ref = None
        else:
            a_ref, b_ref, r_ref, o_ref, acc = refs
        k = pl.program_id(2)
        av = a_ref[...].astype(BF16)
        bv = b_ref[...].astype(BF16)
        if mode == "nn":
            p = jnp.dot(av, bv, preferred_element_type=F32)
        elif mode == "nt":
            p = lax.dot_general(av, bv, (((1,), (1,)), ((), ())), preferred_element_type=F32)
        else:
            p = lax.dot_general(av, bv, (((0,), (0,)), ((), ())), preferred_element_type=F32)

        @pl.when(k == 0)
        def _():
            acc[...] = p

        @pl.when(k > 0)
        def _():
            acc[...] += p

        @pl.when(k == nk - 1)
        def _():
            r = acc[...]
            if r_ref is not None:
                r = r + r_ref[...]
            o_ref[...] = r.astype(o_ref.dtype)

    if mode == "tn":
        a_spec = pl.BlockSpec((tk, tm), lambda i, j, k: (k, i))
    else:
        a_spec = pl.BlockSpec((tm, tk), lambda i, j, k: (i, k))
    if mode == "nt":
        b_spec = pl.BlockSpec((tn, tk), lambda i, j, k: (j, k))
    else:
        b_spec = pl.BlockSpec((tk, tn), lambda i, j, k: (k, j + joff))
    o_spec = pl.BlockSpec((tm, tn), lambda i, j, k: (i, j))
    in_specs = [a_spec, b_spec] + ([o_spec] if res is not None else [])
    args = (a, b) + ((res,) if res is not None else ())
    return pl.pallas_call(
        body, name=name, grid=(M // tm, N // tn, nk),
        in_specs=in_specs, out_specs=o_spec,
        out_shape=jax.ShapeDtypeStruct((M, N), out_dtype),
        scratch_shapes=[pltpu.VMEM((tm, tn), F32)],
        compiler_params=_cp(("parallel", "parallel", "arbitrary")),
    )(*args)


def _rowmap(fn, row_ins, const_ins, row_outs, acc_outs, *, geom, tile, name):
    T, Lp, pad = geom
    assert Lp % tile == 0
    per_seq = Lp // tile
    specs, args = [], []
    for r in row_ins:
        arr, w, cb = r if isinstance(r, tuple) else (r, r.shape[1], 0)
        specs.append(pl.BlockSpec((tile, w), functools.partial(lambda i, cb: (i, cb), cb=cb)))
        args.append(arr)
    for c in const_ins:
        specs.append(pl.BlockSpec(c.shape, functools.partial(lambda i, nd: (0,) * nd, nd=c.ndim)))
        args.append(c)
    n_r, n_c, n_o, n_a = len(row_ins), len(const_ins), len(row_outs), len(acc_outs)
    out_specs = [pl.BlockSpec((tile, w), lambda i: (i, 0)) for w, _ in row_outs]
    out_specs += [pl.BlockSpec(s, lambda i: (0, 0)) for s in acc_outs]
    out_shape = [jax.ShapeDtypeStruct((T, w), dt) for w, dt in row_outs]
    out_shape += [jax.ShapeDtypeStruct(s, F32) for s in acc_outs]

    def body(*refs):
        i = pl.program_id(0)
        pos = (i % per_seq) * tile + lax.broadcasted_iota(jnp.int32, (tile, 1), 0)
        valid = pos >= pad
        vals = [r[...].astype(F32) for r in refs[:n_r]] + [r[...] for r in refs[n_r:n_r + n_c]]
        outs = fn(valid, *vals)
        if not isinstance(outs, (tuple, list)):
            outs = (outs,)
        orefs = refs[n_r + n_c:]
        for r, v in zip(orefs[:n_o], outs[:n_o]):
            r[...] = v.astype(r.dtype)
        for r, v in zip(orefs[n_o:], outs[n_o:]):
            @pl.when(i == 0)
            def _(r=r, v=v):
                r[...] = v

            @pl.when(i > 0)
            def _(r=r, v=v):
                r[...] += v

    res = pl.pallas_call(
        body, name=name, grid=(T // tile,), in_specs=specs, out_specs=out_specs, out_shape=out_shape,
        compiler_params=_cp(("arbitrary",)),
    )(*args)
    return res


def _sigmoid(x):
    return 1.0 / (1.0 + jnp.exp(-x))


def _silu(x):
    return x * _sigmoid(x)


def _softplus(x):
    return jnp.maximum(x, 0.0) + jnp.log(1.0 + jnp.exp(-jnp.abs(x)))


def _gelu(x):
    return 0.5 * x * (1.0 + jnp.tanh(math.sqrt(2.0 / math.pi) * (x + 0.044715 * x * x * x)))


def _rms(x, w):
    return x * lax.rsqrt(jnp.mean(x * x, axis=-1, keepdims=True) + EPS) * w


def _colsum(v):
    return jnp.sum(v, axis=0, keepdims=True)


def _f_norm(valid, x, w):
    return _rms(x, w)


def _b_norm(valid, x, dxn, dres, w):
    _, vjp = jax.vjp(_rms, x, w)
    dx, dw = vjp(dxn)
    return jnp.where(valid, dx + dres, 0.0), dw


def _smallact(valid, raw, bias):
    lane = lax.broadcasted_iota(jnp.int32, raw.shape, 1)
    v = raw + bias
    dt = _softplus(v)
    logf = -_softplus(-v)
    out = jnp.where(lane < SSD_H, dt, jnp.where(lane < SSD_H + FOX_H, logf, 0.0))
    return jnp.where(valid, out, 0.0)


def _b_smallact(valid, raw, d1, d2, bias):
    _, vjp = jax.vjp(lambda r, b: _smallact(valid, r, b), raw, bias)
    return vjp(d1 + d2)


def _ssd_post(valid, y, xs, z, drep, nw):
    y = (y + xs * drep) * _silu(z)
    return _rms(y, nw)


def _b_ssd_post(valid, y, xs, z, dya, drep, nw):
    _, vjp = jax.vjp(lambda a, b, c, d, e: _ssd_post(valid, a, b, c, d, e), y, xs, z, drep, nw)
    dy, dxs, dz, dd, dn = vjp(dya)
    return dy, dxs, dz, dd, dn


def _s5_pre(valid, ys, u, d):
    return _gelu(ys + d * u)


def _s5_glu(valid, ys, u, t, d):
    y1 = _gelu(ys + d * u)
    return y1 * _sigmoid(t)


def _b_s5_glu(valid, ys, u, t, dyc, d):
    y1 = _gelu(ys + d * u)
    _, vjp = jax.vjp(lambda a, b: a * _sigmoid(b), y1, t)
    dy1, dt = vjp(dyc)
    return dt, dy1


def _b_s5_pre(valid, ys, u, dy1a, dy1b, d):
    _, vjp = jax.vjp(lambda a, b, c: _gelu(a + c * b), ys, u, d)
    dys, du, dd = vjp(dy1a + dy1b)
    return dys, du, dd


def _merge(valid, g0, g1, g2, b0, b1, b2):
    m = _sigmoid(g0) * b0 + _sigmoid(g1) * b1 + _sigmoid(g2) * b2
    return jnp.where(valid, m, 0.0)


def _b_merge(valid, g0, g1, g2, b0, b1, b2, dmix):
    _, vjp = jax.vjp(lambda *a: _merge(valid, *a), g0, g1, g2, b0, b1, b2)
    d = vjp(dmix)
    return jnp.concatenate(d[:3], axis=1), d[3], d[4], d[5]


def _swiglu(valid, g, up):
    return _silu(g) * up


def _b_swiglu(valid, g, up, dact):
    _, vjp = jax.vjp(lambda a, b: _silu(a) * b, g, up)
    dg, dup = vjp(dact)
    return jnp.concatenate([dg, dup], axis=1)


def _conv_taps(ext, tile):
    taps = []
    for k in range(CONV_K):
        sh = CONV_K - 1 - k
        v = ext if sh == 0 else pltpu.roll(ext, sh, 0)
        taps.append(v[8:8 + tile])
    return taps


def _conv_fwd(x, w, b, *, geom, tile):
    T, Lp, pad = geom
    per_seq = Lp // tile
    hb = tile // 8

    def body(x_ref, h_ref, w_ref, b_ref, o_ref):
        i = pl.program_id(0)
        pos = (i % per_seq) * tile + lax.broadcasted_iota(jnp.int32, (tile, 1), 0)
        ext = jnp.concatenate([h_ref[...], x_ref[...]], axis=0)
        taps = _conv_taps(ext, tile)
        acc = b_ref[...] + taps[0] * w_ref[0:1, :]
        for k in range(1, CONV_K):
            acc = acc + taps[k] * w_ref[k:k + 1, :]
        o_ref[...] = jnp.where(pos >= pad, _silu(acc), 0.0)

    return pl.pallas_call(
        body, name="conv_fwd", grid=(T // tile,),
        in_specs=[pl.BlockSpec((tile, CONV_DIM), lambda i: (i, 0)),
                  pl.BlockSpec((8, CONV_DIM), lambda i: (jnp.maximum(i * hb - 1, 0), 0)),
                  pl.BlockSpec((CONV_K, CONV_DIM), lambda i: (0, 0)),
                  pl.BlockSpec((1, CONV_DIM), lambda i: (0, 0))],
        out_specs=pl.BlockSpec((tile, CONV_DIM), lambda i: (i, 0)),
        out_shape=jax.ShapeDtypeStruct((T, CONV_DIM), F32),
        compiler_params=_cp(("arbitrary",)),
    )(x, x, w, b)


def _conv_bwd_pre(x, dact, w, b, *, geom, tile):
    T, Lp, pad = geom
    per_seq = Lp // tile
    hb = tile // 8

    def body(x_ref, h_ref, d_ref, w_ref, b_ref, dc_ref, dw_ref, db_ref):
        i = pl.program_id(0)
        pos = (i % per_seq) * tile + lax.broadcasted_iota(jnp.int32, (tile, 1), 0)
        ext = jnp.concatenate([h_ref[...], x_ref[...]], axis=0)
        taps = _conv_taps(ext, tile)
        acc = b_ref[...] + taps[0] * w_ref[0:1, :]
        for k in range(1, CONV_K):
            acc = acc + taps[k] * w_ref[k:k + 1, :]
        sg = _sigmoid(acc)
        dsilu = sg * (1.0 + acc * (1.0 - sg))
        dc = jnp.where(pos >= pad, d_ref[...] * dsilu, 0.0)
        dc_ref[...] = dc
        dw = jnp.concatenate([_colsum(dc * taps[k]) for k in range(CONV_K)], axis=0)
        db = _colsum(dc)

        @pl.when(i == 0)
        def _():
            dw_ref[...] = dw
            db_ref[...] = db

        @pl.when(i > 0)
        def _():
            dw_ref[...] += dw
            db_ref[...] += db

    return pl.pallas_call(
        body, name="conv_bwd_pre", grid=(T // tile,),
        in_specs=[pl.BlockSpec((tile, CONV_DIM), lambda i: (i, 0)),
                  pl.BlockSpec((8, CONV_DIM), lambda i: (jnp.maximum(i * hb - 1, 0), 0)),
                  pl.BlockSpec((tile, CONV_DIM), lambda i: (i, 0)),
                  pl.BlockSpec((CONV_K, CONV_DIM), lambda i: (0, 0)),
                  pl.BlockSpec((1, CONV_DIM), lambda i: (0, 0))],
        out_specs=[pl.BlockSpec((tile, CONV_DIM), lambda i: (i, 0)),
                   pl.BlockSpec((CONV_K, CONV_DIM), lambda i: (0, 0)),
                   pl.BlockSpec((1, CONV_DIM), lambda i: (0, 0))],
        out_shape=[jax.ShapeDtypeStruct((T, CONV_DIM), F32),
                   jax.ShapeDtypeStruct((CONV_K, CONV_DIM), F32),
                   jax.ShapeDtypeStruct((1, CONV_DIM), F32)],
        compiler_params=_cp(("arbitrary",)),
    )(x, x, dact, w, b)


def _conv_bwd_x(dc, w, *, geom, tile):
    T, Lp, pad = geom
    nt = T // tile
    hb = tile // 8

    def body(d_ref, h_ref, w_ref, o_ref):
        i = pl.program_id(0)
        halo = jnp.where(i < nt - 1, h_ref[...], 0.0)
        ext = jnp.concatenate([d_ref[...], halo], axis=0)
        n_ext = tile + 8
        acc = ext[0:tile] * w_ref[CONV_K - 1:CONV_K, :]
        for j in range(1, CONV_K):
            acc = acc + pltpu.roll(ext, n_ext - j, 0)[0:tile] * w_ref[CONV_K - 1 - j:CONV_K - j, :]
        o_ref[...] = acc.astype(o_ref.dtype)

    return pl.pallas_call(
        body, name="conv_bwd_x", grid=(nt,),
        in_specs=[pl.BlockSpec((tile, CONV_DIM), lambda i: (i, 0)),
                  pl.BlockSpec((8, CONV_DIM), lambda i: (jnp.minimum((i + 1) * hb, nt * hb - 1), 0)),
                  pl.BlockSpec((CONV_K, CONV_DIM), lambda i: (0, 0))],
        out_specs=pl.BlockSpec((tile, CONV_DIM), lambda i: (i, 0)),
        out_shape=jax.ShapeDtypeStruct((T, CONV_DIM), BF16),
        compiler_params=_cp(("arbitrary",)),
    )(dc, dc, w)


def _ssd_common(sm_ref, alog_ref):
    lane = lax.broadcasted_iota(jnp.int32, (1, CH), 1)
    A = jnp.where(lane < SSD_H, -jnp.exp(alog_ref[...]), 0.0)
    dt = sm_ref[...]
    adt = dt * A
    r = lax.broadcasted_iota(jnp.int32, (CH, CH), 0)
    c = lax.broadcasted_iota(jnp.int32, (CH, CH), 1)
    tril = (r >= c).astype(F32)
    cs = jnp.dot(tril, adt, precision=HI, preferred_element_type=F32)
    csT = cs.T
    cs_last = jnp.sum(jnp.where(r == CH - 1, cs, 0.0), axis=0, keepdims=True)
    return A, dt, cs, csT, cs_last, tril, r, c


def _ssd_lanes():
    r = lax.broadcasted_iota(jnp.int32, (CH, D), 0)
    c = lax.broadcasted_iota(jnp.int32, (CH, D), 1)
    return (c // SSD_P == r).astype(F32)


def _ssd_per_lane(dt, cs):
    ex = _ssd_lanes()
    dt_rep = jnp.dot(dt, ex, precision=HI, preferred_element_type=F32)
    cs_rep = jnp.dot(cs, ex, precision=HI, preferred_element_type=F32)
    r = lax.broadcasted_iota(jnp.int32, (CH, D), 0)
    last_rep = jnp.sum(jnp.where(r == CH - 1, cs_rep, 0.0), axis=0, keepdims=True)
    return dt_rep, cs_rep, last_rep


def _ssd_per_head(*per_lane):
    ex = _ssd_lanes()
    return [lax.dot_general(v, ex, (((1,), (1,)), ((), ())), precision=HI, preferred_element_type=F32)
            for v in per_lane]


def _nt(a, b):
    return lax.dot_general(a, b, (((1,), (1,)), ((), ())), preferred_element_type=F32)


def _tn(a, b):
    return lax.dot_general(a, b, (((0,), (0,)), ((), ())), preferred_element_type=F32)


def _nn(a, b):
    return jnp.dot(a, b, preferred_element_type=F32)


def _ssd_fwd(xbc, sm, alog, *, B, NC):
    T = B * NC * CH

    def body(x_ref, sm_ref, alog_ref, y_ref, st_ref, S):
        cidx = pl.program_id(1)

        @pl.when(cidx == 0)
        def _():
            S[...] = jnp.zeros_like(S)

        st_ref[0] = S[...]
        A, dt, cs, csT, cs_last, tril, _, _ = _ssd_common(sm_ref, alog_ref)
        dt_rep, cs_rep, last_rep = _ssd_per_lane(dt, cs)
        xdt = x_ref[:, 0:D] * dt_rep
        xdec = (xdt * jnp.exp(last_rep - cs_rep)).astype(BF16)
        e_rep = jnp.exp(cs_rep)
        HG = SSD_H // SSD_G
        for g in range(SSD_G):
            gl = slice(g * HG * SSD_P, (g + 1) * HG * SSD_P)
            Bb = x_ref[:, D + g * SSD_N:D + (g + 1) * SSD_N].astype(BF16)
            Cb = x_ref[:, D + SSD_G * SSD_N + g * SSD_N:D + SSD_G * SSD_N + (g + 1) * SSD_N].astype(BF16)
            G = _nt(Cb, Bb)
            STg = S[g * HG:(g + 1) * HG].reshape(HG * SSD_P, SSD_N)
            y_off = e_rep[:, gl] * _nt(Cb, STg.astype(BF16))
            upd = _tn(xdec[:, gl], Bb)
            for rr in range(HG):
                h = g * HG + rr
                hl = slice(h * SSD_P, (h + 1) * SSD_P)
                col = cs[:, h:h + 1]
                row = csT[h:h + 1, :]
                Ld = jnp.where(tril > 0, jnp.exp(jnp.minimum(col - row, 0.0)), 0.0)
                M = (G * Ld).astype(BF16)
                y_ref[:, hl] = _nn(M, xdt[:, hl].astype(BF16)) + y_off[:, rr * SSD_P:(rr + 1) * SSD_P]
                rows = slice(rr * SSD_P, (rr + 1) * SSD_P)
                S[h] = jnp.exp(cs_last[:, h:h + 1]) * STg[rows] + upd[rows]

    return pl.pallas_call(
        body, name="ssd_fwd", grid=(B, NC),
        in_specs=[pl.BlockSpec((CH, CONV_DIM), lambda b, c: (b * NC + c, 0)),
                  pl.BlockSpec((CH, CH), lambda b, c: (b * NC + c, 0)),
                  pl.BlockSpec((1, CH), lambda b, c: (0, 0))],
        out_specs=[pl.BlockSpec((CH, D), lambda b, c: (b * NC + c, 0)),
                   pl.BlockSpec((1, SSD_H, SSD_P, SSD_N), lambda b, c: (b * NC + c, 0, 0, 0))],
        out_shape=[jax.ShapeDtypeStruct((T, D), F32),
                   jax.ShapeDtypeStruct((B * NC, SSD_H, SSD_P, SSD_N), F32)],
        scratch_shapes=[pltpu.VMEM((SSD_H, SSD_P, SSD_N), F32)],
        compiler_params=_cp(("arbitrary", "arbitrary")),
    )(xbc, sm, alog)


def _ssd_bwd(xbc, sm, alog, states, dy, dxs_skip, *, B, NC):
    T = B * NC * CH

    def rix(b, c):
        return b * NC + (NC - 1 - c)

    def body(x_ref, sm_ref, alog_ref, st_ref, dy_ref, sk_ref, dx_ref, ddt_ref, dal_ref, dS):
        bidx = pl.program_id(0)
        cidx = pl.program_id(1)

        @pl.when(cidx == 0)
        def _():
            dS[...] = jnp.zeros_like(dS)

        A, dt, cs, csT, cs_last, tril, r, c = _ssd_common(sm_ref, alog_ref)
        lane = lax.broadcasted_iota(jnp.int32, (1, CH), 1)
        dt_rep, cs_rep, last_rep = _ssd_per_lane(dt, cs)
        xs = x_ref[:, 0:D]
        xdt = xs * dt_rep
        e_rep = jnp.exp(cs_rep)
        dec_rep = jnp.exp(last_rep - cs_rep)
        dye = dy_ref[...] * e_rep
        xdec = xdt * dec_rep
        HG = SSD_H // SSD_G
        DCcol = jnp.zeros((CH, CH), F32)
        DCrow = jnp.zeros((CH, CH), F32)
        dlast = jnp.zeros((1, CH), F32)
        t_off, t_dec, dx_state = [], [], []
        for g in range(SSD_G):
            ob = D + g * SSD_N
            oc = D + SSD_G * SSD_N + g * SSD_N
            gl = slice(g * HG * SSD_P, (g + 1) * HG * SSD_P)
            Bb = x_ref[:, ob:ob + SSD_N].astype(BF16)
            Cb = x_ref[:, oc:oc + SSD_N].astype(BF16)
            G = _nt(Cb, Bb)
            STg = st_ref[0, g * HG:(g + 1) * HG].reshape(HG * SSD_P, SSD_N)
            dSTg = dS[g * HG:(g + 1) * HG].reshape(HG * SSD_P, SSD_N)
            STb = STg.astype(BF16)
            dSTb = dSTg.astype(BF16)
            dyeb = dye[:, gl].astype(BF16)
            t_off.append(dye[:, gl] * _nt(Cb, STb))
            dCg = _nn(dyeb, STb)
            dS_in = _tn(dyeb, Cb)
            Z = _nt(Bb, dSTb)
            dx_state.append(dec_rep[:, gl] * Z)
            t_dec.append(xdec[:, gl] * Z)
            dBg = _nn(xdec[:, gl].astype(BF16), dSTb)
            dG = jnp.zeros((CH, CH), F32)
            for rr in range(HG):
                h = g * HG + rr
                hl = slice(h * SSD_P, (h + 1) * SSD_P)
                rows = slice(rr * SSD_P, (rr + 1) * SSD_P)
                col = cs[:, h:h + 1]
                row = csT[h:h + 1, :]
                Ld = jnp.where(tril > 0, jnp.exp(jnp.minimum(col - row, 0.0)), 0.0)
                Mf = G * Ld
                dyb = dy_ref[:, hl].astype(BF16)
                dx_ref[:, hl] = _tn(Mf.astype(BF16), dyb)
                dM = _nt(dyb, xdt[:, hl].astype(BF16))
                dG = dG + dM * Ld
                W = dM * Mf
                DCcol = DCcol + jnp.where(c == h, jnp.sum(W, axis=1, keepdims=True), 0.0)
                DCrow = DCrow - jnp.where(r == h, jnp.sum(W, axis=0, keepdims=True), 0.0)
                el = jnp.exp(cs_last[:, h:h + 1])
                dl = el * jnp.sum(jnp.sum(dSTg[rows] * STg[rows], axis=1, keepdims=True), axis=0, keepdims=True)
                dlast = dlast + jnp.where(lane == h, dl, 0.0)
                dS[h] = dS_in[rows] + el * dSTg[rows]
            dGb = dG.astype(BF16)
            dx_ref[:, ob:ob + SSD_N] = dBg + _tn(dGb, Cb)
            dx_ref[:, oc:oc + SSD_N] = dCg + _nn(dGb, Bb)
        dxdt = dx_ref[:, 0:D] + jnp.concatenate(dx_state, axis=1)
        dx_ref[:, 0:D] = dxdt * dt_rep + sk_ref[...]
        s_off, s_dec, DX = _ssd_per_head(jnp.concatenate(t_off, axis=1), jnp.concatenate(t_dec, axis=1), dxdt * xs)
        dlast = dlast + jnp.sum(s_dec, axis=0, keepdims=True)
        DC = DCcol + s_off - s_dec + DCrow.T + jnp.where(r == CH - 1, dlast, 0.0)
        triu = (r <= c).astype(F32)
        dadt = jnp.dot(triu, DC, precision=HI, preferred_element_type=F32)
        ddt_ref[...] = dadt * A + DX
        dal = jnp.sum(dadt * dt, axis=0, keepdims=True) * A

        @pl.when((bidx == 0) & (cidx == 0))
        def _():
            dal_ref[...] = dal

        @pl.when((bidx > 0) | (cidx > 0))
        def _():
            dal_ref[...] += dal

    return pl.pallas_call(
        body, name="ssd_bwd", grid=(B, NC),
        in_specs=[pl.BlockSpec((CH, CONV_DIM), lambda b, c: (rix(b, c), 0)),
                  pl.BlockSpec((CH, CH), lambda b, c: (rix(b, c), 0)),
                  pl.BlockSpec((1, CH), lambda b, c: (0, 0)),
                  pl.BlockSpec((1, SSD_H, SSD_P, SSD_N), lambda b, c: (rix(b, c), 0, 0, 0)),
                  pl.BlockSpec((CH, D), lambda b, c: (rix(b, c), 0)),
                  pl.BlockSpec((CH, D), lambda b, c: (rix(b, c), 0))],
        out_specs=[pl.BlockSpec((CH, CONV_DIM), lambda b, c: (rix(b, c), 0)),
                   pl.BlockSpec((CH, CH), lambda b, c: (rix(b, c), 0)),
                   pl.BlockSpec((1, CH), lambda b, c: (0, 0))],
        out_shape=[jax.ShapeDtypeStruct((T, CONV_DIM), F32),
                   jax.ShapeDtypeStruct((T, CH), F32),
                   jax.ShapeDtypeStruct((1, CH), F32)],
        scratch_shapes=[pltpu.VMEM((SSD_H, SSD_P, SSD_N), F32)],
        compiler_params=_cp(("arbitrary", "arbitrary")),
    )(xbc, sm, alog, states, dy, dxs_skip)


def _cumsum_seq(v, *, B, NC, reverse, name):
    T = B * NC * CH
    R = 3 * CH if (NC % 3 == 0 and NC > 3) else CH
    NR = NC * CH // R

    def ix(b, c):
        return b * NR + ((NR - 1 - c) if reverse else c)

    def body(v_ref, o_ref, carry):
        cidx = pl.program_id(1)

        @pl.when(cidx == 0)
        def _():
            carry[...] = jnp.zeros_like(carry)

        r = lax.broadcasted_iota(jnp.int32, (R, R), 0)
        c = lax.broadcasted_iota(jnp.int32, (R, R), 1)
        tri = ((r <= c) if reverse else (r >= c)).astype(F32)
        cs = jnp.dot(tri, v_ref[...], precision=HI, preferred_element_type=F32) + carry[...]
        o_ref[...] = cs
        edge = 0 if reverse else R - 1
        rows = lax.broadcasted_iota(jnp.int32, (R, CH), 0)
        carry[...] = jnp.sum(jnp.where(rows == edge, cs, 0.0), axis=0, keepdims=True)

    return pl.pallas_call(
        body, name=name, grid=(B, NR),
        in_specs=[pl.BlockSpec((R, CH), lambda b, c: (ix(b, c), 0))],
        out_specs=pl.BlockSpec((R, CH), lambda b, c: (ix(b, c), 0)),
        out_shape=jax.ShapeDtypeStruct((T, CH), F32),
        scratch_shapes=[pltpu.VMEM((1, CH), F32)],
        compiler_params=_cp(("arbitrary", "arbitrary")),
    )(v)


def _fox_tb(Lp):
    return 384 if (Lp % 384 == 0 and Lp > 384) else CH


def _fox_keybias(cum, *, B, Lp, pad):
    ck = cum.reshape(B, Lp, CH)[:, :, SSD_H:SSD_H + FOX_H].transpose(0, 2, 1)
    pos = lax.broadcasted_iota(jnp.int32, ck.shape, 2)
    return jnp.where(pos < pad, -NEG, ck).reshape(B * FOX_H, 1, Lp)


def _fox_tril(TB):
    r = lax.broadcasted_iota(jnp.int32, (TB, TB), 0)
    c = lax.broadcasted_iota(jnp.int32, (TB, TB), 1)
    return r >= c


def _fox_fwd(qkv, cumT, *, B, Lp):
    TB = _fox_tb(Lp)
    NQ = Lp // TB
    T = B * Lp
    scale = FOX_DH ** -0.5

    def body(q_ref, k_ref, v_ref, ct_ref, o_ref, lse_ref):
        i = pl.program_id(2)
        q = q_ref[...]

        def block(j, nb, carry, diag):
            m, l, acc = carry
            off = pl.multiple_of(j * TB, TB)
            k = k_ref[pl.ds(off, nb * TB), :]
            v = v_ref[pl.ds(off, nb * TB), :]
            s = _nt(q, k) * scale - ct_ref[0, :, pl.ds(off, nb * TB)]
            if diag:
                s = jnp.where(_fox_tril(TB), s, NEG)
            m_new = jnp.maximum(m, jnp.max(s, axis=1, keepdims=True))
            p = jnp.exp(s - m_new)
            alpha = jnp.exp(m - m_new)
            l = alpha * l + jnp.sum(p, axis=1, keepdims=True)
            acc = alpha * acc + _nn(p.astype(BF16), v)
            return m_new, l, acc

        init = (jnp.full((TB, 1), NEG, F32), jnp.zeros((TB, 1), F32), jnp.zeros((TB, FOX_DH), F32))
        carry = lax.fori_loop(0, i // 2, lambda t, c: block(2 * t, 2, c, False), init)
        carry = lax.fori_loop(2 * (i // 2), i, lambda j, c: block(j, 1, c, False), carry)
        m, l, acc = block(i, 1, carry, True)
        o_ref[...] = (acc / l).astype(o_ref.dtype)
        lse_ref[0, 0] = m + jnp.log(l)

    return pl.pallas_call(
        body, name="fox_fwd", grid=(B, FOX_H, NQ),
        in_specs=[pl.BlockSpec((TB, FOX_DH), lambda b, h, i: (b * NQ + i, h)),
                  pl.BlockSpec((Lp, FOX_DH), lambda b, h, i: (b, FOX_H + h)),
                  pl.BlockSpec((Lp, FOX_DH), lambda b, h, i: (b, 2 * FOX_H + h)),
                  pl.BlockSpec((1, 1, Lp), lambda b, h, i: (b * FOX_H + h, 0, 0))],
        out_specs=[pl.BlockSpec((TB, FOX_DH), lambda b, h, i: (b * NQ + i, h)),
                   pl.BlockSpec((1, 1, TB, 1), lambda b, h, i: (b, h, i, 0))],
        out_shape=[jax.ShapeDtypeStruct((T, D), BF16),
                   jax.ShapeDtypeStruct((B, FOX_H, Lp, 1), F32)],
        compiler_params=_cp(("arbitrary", "arbitrary", "arbitrary")),
    )(qkv, qkv, qkv, cumT)


def _fox_bwd(qkv, dy, o, lse, cumT, *, B, Lp):
    TB = _fox_tb(Lp)
    NQ = Lp // TB
    T = B * Lp
    scale = FOX_DH ** -0.5

    def body(q_ref, k_ref, v_ref, dy_ref, o_ref, lse_ref, ct_ref, dq_ref, dk_ref, dv_ref, dck_ref, dcq_ref, dl_s):
        j = pl.program_id(2)
        k = k_ref[...]
        v = v_ref[...]
        ck = ct_ref[0]

        @pl.when(j == 0)
        def _():
            dq_ref[...] = jnp.zeros_like(dq_ref)
            dcq_ref[...] = jnp.zeros_like(dcq_ref)
            for i in range(NQ):
                sl = slice(i * TB, (i + 1) * TB)
                dl_s[sl, :] = jnp.sum(dy_ref[sl, :].astype(F32) * o_ref[sl, :].astype(F32), axis=1, keepdims=True)

        def block(i, nb, carry, diag):
            dk, dv, dck = carry
            off = pl.multiple_of(i * TB, TB)
            rows = pl.ds(off, nb * TB)
            q = q_ref[rows, :]
            dob = dy_ref[rows, :].astype(BF16)
            e = _nt(q, k) * scale - ck - lse_ref[0, 0, rows, :]
            if diag:
                e = jnp.where(_fox_tril(TB), e, NEG)
            p = jnp.exp(e)
            dv = dv + _tn(p.astype(BF16), dob)
            ds = p * (_nt(dob, v) - dl_s[rows, :])
            dsb = ds.astype(BF16)
            dk = dk + _tn(dsb, q)
            dq_ref[rows, :] += _nn(dsb, k) * scale
            dcq_ref[0, 0, rows, :] += jnp.sum(ds, axis=1, keepdims=True)
            dck = dck - jnp.sum(ds, axis=0, keepdims=True)
            return dk, dv, dck

        z = jnp.zeros((TB, FOX_DH), F32)
        carry = block(j, 1, (z, z, jnp.zeros((1, TB), F32)), True)
        npair = (NQ - 1 - j) // 2
        carry = lax.fori_loop(0, npair, lambda t, c: block(j + 1 + 2 * t, 2, c, False), carry)
        dk, dv, dck = lax.fori_loop(j + 1 + 2 * npair, NQ, lambda i, c: block(i, 1, c, False), carry)
        dk_ref[...] = (dk * scale).astype(dk_ref.dtype)
        dv_ref[...] = dv.astype(dv_ref.dtype)
        dck_ref[0] = dck

    head = lambda b, h, j: (b, h)
    return pl.pallas_call(
        body, name="fox_bwd", grid=(B, FOX_H, NQ),
        in_specs=[pl.BlockSpec((Lp, FOX_DH), head),
                  pl.BlockSpec((TB, FOX_DH), lambda b, h, j: (b * NQ + j, FOX_H + h)),
                  pl.BlockSpec((TB, FOX_DH), lambda b, h, j: (b * NQ + j, 2 * FOX_H + h)),
                  pl.BlockSpec((Lp, FOX_DH), head),
                  pl.BlockSpec((Lp, FOX_DH), head),
                  pl.BlockSpec((1, 1, Lp, 1), lambda b, h, j: (b, h, 0, 0)),
                  pl.BlockSpec((1, 1, TB), lambda b, h, j: (b * FOX_H + h, 0, j))],
        out_specs=[pl.BlockSpec((Lp, FOX_DH), head),
                   pl.BlockSpec((TB, FOX_DH), lambda b, h, j: (b * NQ + j, h)),
                   pl.BlockSpec((TB, FOX_DH), lambda b, h, j: (b * NQ + j, h)),
                   pl.BlockSpec((1, 1, TB), lambda b, h, j: (b * FOX_H + h, 0, j)),
                   pl.BlockSpec((1, 1, Lp, 1), lambda b, h, j: (b, h, 0, 0))],
        out_shape=[jax.ShapeDtypeStruct((T, D), F32),
                   jax.ShapeDtypeStruct((T, D), BF16),
                   jax.ShapeDtypeStruct((T, D), BF16),
                   jax.ShapeDtypeStruct((B * FOX_H, 1, Lp), F32),
                   jax.ShapeDtypeStruct((B, FOX_H, Lp, 1), F32)],
        scratch_shapes=[pltpu.VMEM((Lp, 1), F32)],
        compiler_params=_cp(("arbitrary", "arbitrary", "arbitrary")),
    )(qkv, qkv, qkv, dy, o, lse, cumT)


S5_TILE = 8


def _s5_pows(lam_ref, pw, tab, reverse):
    lr = lam_ref[0, :, 0:S5_W]
    li = lam_ref[0, :, S5_W:2 * S5_W]
    if reverse:
        li = -li
    ar, ai = lr, li
    sub = lax.broadcasted_iota(jnp.int32, (S5_TILE, 1), 0)
    for k, s in enumerate((1, 2, 4)):
        keep = (sub < S5_TILE - s) if reverse else (sub >= s)
        pw[k * S5_TILE:(k + 1) * S5_TILE, 0:S5_W] = jnp.where(keep, ar, 0.0)
        pw[k * S5_TILE:(k + 1) * S5_TILE, S5_W:2 * S5_W] = jnp.where(keep, ai, 0.0)
        ar, ai = ar * ar - ai * ai, 2.0 * ar * ai
    ar, ai = lr, li
    for r in range(S5_TILE):
        row = (S5_TILE - 1 - r) if reverse else r
        tab[row:row + 1, 0:S5_W] = ar
        tab[row:row + 1, S5_W:2 * S5_W] = ai
        ar, ai = ar * lr - ai * li, ar * li + ai * lr


def _s5_scan(hs, pw, tab, carry, reverse):
    n = hs.shape[0]
    tiles = list(range(n // S5_TILE))
    if reverse:
        tiles.reverse()
    for t in tiles:
        lo = t * S5_TILE
        vr = hs[lo:lo + S5_TILE, 0:S5_W]
        vi = hs[lo:lo + S5_TILE, S5_W:2 * S5_W]
        for k, s in enumerate((1, 2, 4)):
            sh = (S5_TILE - s) if reverse else s
            sr = pltpu.roll(vr, sh, 0)
            si = pltpu.roll(vi, sh, 0)
            ar = pw[k * S5_TILE:(k + 1) * S5_TILE, 0:S5_W]
            ai = pw[k * S5_TILE:(k + 1) * S5_TILE, S5_W:2 * S5_W]
            vr, vi = vr + ar * sr - ai * si, vi + ar * si + ai * sr
        hs[lo:lo + S5_TILE, 0:S5_W] = vr
        hs[lo:lo + S5_TILE, S5_W:2 * S5_W] = vi
    e8 = 0 if reverse else S5_TILE - 1
    l8r = tab[e8:e8 + 1, 0:S5_W]
    l8i = tab[e8:e8 + 1, S5_W:2 * S5_W]
    cr = carry[:, 0:S5_W]
    ci = carry[:, S5_W:2 * S5_W]
    states = []
    for t in tiles:
        states.append((cr, ci))
        edge = t * S5_TILE + e8
        er = hs[edge:edge + 1, 0:S5_W]
        ei = hs[edge:edge + 1, S5_W:2 * S5_W]
        cr, ci = er + l8r * cr - l8i * ci, ei + l8r * ci + l8i * cr
    carry[:, 0:S5_W] = cr
    carry[:, S5_W:2 * S5_W] = ci
    tr = tab[:, 0:S5_W]
    ti = tab[:, S5_W:2 * S5_W]
    for t, (cr, ci) in zip(tiles, states):
        lo = t * S5_TILE
        hs[lo:lo + S5_TILE, 0:S5_W] += tr * cr - ti * ci
        hs[lo:lo + S5_TILE, S5_W:2 * S5_W] += tr * ci + ti * cr


def _s5_rows(NC):
    return 3 * CH if (NC % 3 == 0 and NC > 3) else CH


def _s5_fwd(u, Bsg, Csg, lam, *, B, NC):
    T = B * NC * CH
    R = _s5_rows(NC)
    NR = NC * CH // R

    def body(u_ref, b_ref, c_ref, lam_ref, y_ref, h_ref, pw, tab, hs, carry):
        cidx = pl.program_id(2)

        @pl.when(cidx == 0)
        def _():
            _s5_pows(lam_ref, pw, tab, False)
            carry[...] = jnp.zeros_like(carry)

        hs[...] = _nn(u_ref[...].astype(BF16), b_ref[0])
        _s5_scan(hs, pw, tab, carry, False)
        hb = hs[...].astype(BF16)
        h_ref[...] = hb
        y_ref[...] = _nn(hb, c_ref[0])

    return pl.pallas_call(
        body, name="s5_fwd", grid=(B, S5_G // S5_SG, NR),
        in_specs=[pl.BlockSpec((R, CH), lambda b, s, c: (b * NR + c, s)),
                  pl.BlockSpec((1, CH, 2 * S5_W), lambda b, s, c: (s, 0, 0)),
                  pl.BlockSpec((1, 2 * S5_W, CH), lambda b, s, c: (s, 0, 0)),
                  pl.BlockSpec((1, 1, 2 * S5_W), lambda b, s, c: (s, 0, 0))],
        out_specs=[pl.BlockSpec((R, CH), lambda b, s, c: (b * NR + c, s)),
                   pl.BlockSpec((R, 2 * S5_W), lambda b, s, c: (b * NR + c, s))],
        out_shape=[jax.ShapeDtypeStruct((T, D), F32),
                   jax.ShapeDtypeStruct((T, (S5_G // S5_SG) * 2 * S5_W), BF16)],
        scratch_shapes=[pltpu.VMEM((3 * S5_TILE, 2 * S5_W), F32), pltpu.VMEM((S5_TILE, 2 * S5_W), F32),
                        pltpu.VMEM((R, 2 * S5_W), F32), pltpu.VMEM((1, 2 * S5_W), F32)],
        compiler_params=_cp(("arbitrary", "arbitrary", "arbitrary")),
    )(u, Bsg, Csg, lam)


def _s5_bwd(u, hst, dy, du_skip, Bsg, Csg, lam, *, B, NC):
    T = B * NC * CH
    NS = S5_G // S5_SG
    R = _s5_rows(NC)
    NR = NC * CH // R
    hb16 = R // 16

    def rix(b, c):
        return b * NR + (NR - 1 - c)

    def body(u_ref, h_ref, hp_ref, dy_ref, sk_ref, b_ref, c_ref, lam_ref,
             du_ref, db_ref, dc_ref, dl_ref, pw, tab, gs, carry):
        bidx = pl.program_id(1)
        cidx = pl.program_id(2)
        first = (bidx == 0) & (cidx == 0)

        @pl.when(cidx == 0)
        def _():
            _s5_pows(lam_ref, pw, tab, True)
            carry[...] = jnp.zeros_like(carry)

        dyb = dy_ref[...].astype(BF16)
        gs[...] = _nt(dyb, c_ref[0])
        _s5_scan(gs, pw, tab, carry, True)
        gr = gs[:, 0:S5_W]
        gi = gs[:, S5_W:]
        gb = gs[...].astype(BF16)
        du_ref[...] = (_nt(gb, b_ref[0]) + sk_ref[...]).astype(du_ref.dtype)
        ub = u_ref[...].astype(BF16)
        hcur = h_ref[...]
        dB = _tn(ub, gb)
        dC = _tn(dyb, hcur)
        hf = hcur.astype(F32)
        row = lax.broadcasted_iota(jnp.int32, (R, 1), 0)
        prev_last = jnp.where(cidx < NR - 1, hp_ref[15:16, :].astype(F32), 0.0)
        hprev = jnp.where(row == 0, prev_last, pltpu.roll(hf, 1, 0))
        pr = hprev[:, 0:S5_W]
        pi = hprev[:, S5_W:]
        da = _colsum(gr * pr + gi * pi)
        dbb = _colsum(gi * pr - gr * pi)
        dl = jnp.concatenate([da, dbb], axis=1)

        @pl.when(first)
        def _():
            db_ref[0] = dB
            dc_ref[0] = dC
            dl_ref[0] = dl

        @pl.when(jnp.logical_not(first))
        def _():
            db_ref[0] += dB
            dc_ref[0] += dC
            dl_ref[0] += dl

    return pl.pallas_call(
        body, name="s5_bwd", grid=(NS, B, NR),
        in_specs=[pl.BlockSpec((R, CH), lambda s, b, c: (rix(b, c), s)),
                  pl.BlockSpec((R, 2 * S5_W), lambda s, b, c: (rix(b, c), s)),
                  pl.BlockSpec((16, 2 * S5_W), lambda s, b, c: (jnp.maximum(rix(b, c) * hb16 - 1, 0), s)),
                  pl.BlockSpec((R, CH), lambda s, b, c: (rix(b, c), s)),
                  pl.BlockSpec((R, CH), lambda s, b, c: (rix(b, c), s)),
                  pl.BlockSpec((1, CH, 2 * S5_W), lambda s, b, c: (s, 0, 0)),
                  pl.BlockSpec((1, 2 * S5_W, CH), lambda s, b, c: (s, 0, 0)),
                  pl.BlockSpec((1, 1, 2 * S5_W), lambda s, b, c: (s, 0, 0))],
        out_specs=[pl.BlockSpec((R, CH), lambda s, b, c: (rix(b, c), s)),
                   pl.BlockSpec((1, CH, 2 * S5_W), lambda s, b, c: (s, 0, 0)),
                   pl.BlockSpec((1, CH, 2 * S5_W), lambda s, b, c: (s, 0, 0)),
                   pl.BlockSpec((1, 1, 2 * S5_W), lambda s, b, c: (s, 0, 0))],
        out_shape=[jax.ShapeDtypeStruct((T, D), BF16),
                   jax.ShapeDtypeStruct((NS, CH, 2 * S5_W), F32),
                   jax.ShapeDtypeStruct((NS, CH, 2 * S5_W), F32),
                   jax.ShapeDtypeStruct((NS, 1, 2 * S5_W), F32)],
        scratch_shapes=[pltpu.VMEM((3 * S5_TILE, 2 * S5_W), F32), pltpu.VMEM((S5_TILE, 2 * S5_W), F32),
                        pltpu.VMEM((R, 2 * S5_W), F32), pltpu.VMEM((1, 2 * S5_W), F32)],
        compiler_params=_cp(("arbitrary", "arbitrary", "arbitrary")),
    )(u, hst, hst, dy, du_skip, Bsg, Csg, lam)


def _s5_param_fn(lre, lim, lstep, bre, bim):
    step = jnp.exp(lstep)
    zr = lre * step
    zi = lim * step
    e = jnp.exp(zr)
    a = e * jnp.cos(zi)
    b = e * jnp.sin(zi)
    den = lre * lre + lim * lim
    qr = ((a - 1.0) * lre + b * lim) / den
    qi = (b * lre - (a - 1.0) * lim) / den
    return a, b, qr[None] * bre - qi[None] * bim, qr[None] * bim + qi[None] * bre


_S5_ROWS = S5_G * S5_P // CH


def _s5_tile(v):
    return v.reshape(_S5_ROWS, CH)


def _s5_tile_b(v):
    return v.reshape(S5_G * S5_P, S5_C).T.reshape(S5_C, _S5_ROWS, CH)


def _s5_untile_b(v):
    return v.reshape(S5_C, S5_G * S5_P).T.reshape(S5_G, S5_P, S5_C)


def _s5_params(lre, lim, lstep, bre, bim):
    def body(a_ref, b_ref, c_ref, d_ref, e_ref, o1, o2, o3, o4):
        outs = _s5_param_fn(a_ref[...], b_ref[...], c_ref[...], d_ref[...], e_ref[...])
        for o, v in zip((o1, o2, o3, o4), outs):
            o[...] = v

    shp = [jax.ShapeDtypeStruct(lre.shape, F32)] * 2 + [jax.ShapeDtypeStruct(bre.shape, F32)] * 2
    return pl.pallas_call(body, name="s5_params", out_shape=shp, compiler_params=_cp())(lre, lim, lstep, bre, bim)


def _s5_params_bwd(lre, lim, lstep, bre, bim, da, db, dbr, dbi):
    def body(a_ref, b_ref, c_ref, d_ref, e_ref, g1, g2, g3, g4, o1, o2, o3, o4, o5):
        _, vjp = jax.vjp(_s5_param_fn, a_ref[...], b_ref[...], c_ref[...], d_ref[...], e_ref[...])
        outs = vjp((g1[...], g2[...], g3[...], g4[...]))
        for o, v in zip((o1, o2, o3, o4, o5), outs):
            o[...] = v

    shp = [jax.ShapeDtypeStruct(lre.shape, F32)] * 3 + [jax.ShapeDtypeStruct(bre.shape, F32)] * 2
    return pl.pallas_call(body, name="s5_params_bwd", out_shape=shp, compiler_params=_cp())(
        lre, lim, lstep, bre, bim, da, db, dbr, dbi)


def _s5_blockdiag(br, bi, cre, cim):
    NS = S5_G // S5_SG
    eye = jnp.eye(S5_SG, dtype=F32)

    def bmat(v):
        v = v.reshape(NS, S5_SG, S5_P, S5_C)
        m = jnp.einsum("sgpc,gh->sgchp", v, eye)
        return m.reshape(NS, S5_SG * S5_C, S5_SG * S5_P)

    def cmat(v):
        v = v.reshape(NS, S5_SG, S5_C, S5_P)
        m = jnp.einsum("sgcp,gh->sgphc", v, eye)
        return m.reshape(NS, S5_SG * S5_P, S5_SG * S5_C)

    Bsg = jnp.concatenate([bmat(br), bmat(bi)], axis=2).astype(BF16)
    Csg = jnp.concatenate([cmat(cre), cmat(-cim)], axis=1).astype(BF16)
    return Bsg, Csg


def _s5_unblock(dBsg, dCsgT):
    NS = S5_G // S5_SG

    def diag(m):
        m = m.reshape(NS, S5_SG, S5_C, S5_SG, S5_P)
        return jnp.stack([m[:, g, :, g, :] for g in range(S5_SG)], axis=1)

    def ub(m):
        return diag(m).transpose(0, 1, 3, 2).reshape(S5_G, S5_P, S5_C)

    def uc(m):
        return diag(m).reshape(S5_G, S5_C, S5_P)

    dbr = ub(dBsg[:, :, 0:S5_W])
    dbi = ub(dBsg[:, :, S5_W:])
    dcr = uc(dCsgT[:, :, 0:S5_W])
    dci = -uc(dCsgT[:, :, S5_W:])
    return dbr, dbi, dcr, dci


def _loss_head(x, nf, target, *, B, NC, S):
    T = B * NC * CH
    nts = S // CH

    def f(xv, w, t):
        y = _rms(xv, w)
        return 0.5 * _colsum(jnp.mean(jnp.square(y - t), axis=-1, keepdims=True))

    def body(x_ref, w_ref, t_ref, dx_ref, ls_ref, dw_ref):
        i = pl.program_id(0)
        on = (i % NC) > 0
        t = t_ref[...]
        l, vjp = jax.vjp(lambda a, b: f(a, b, t), x_ref[...], w_ref[...])
        dx, dw = vjp(jnp.ones((1, 1), F32))
        g = jnp.where(on, 1.0, 0.0)
        dx_ref[...] = dx * g
        lv = jnp.zeros((1, CH), F32) + l * g

        @pl.when(i == 0)
        def _():
            ls_ref[...] = lv
            dw_ref[...] = dw * g

        @pl.when(i > 0)
        def _():
            ls_ref[...] += lv
            dw_ref[...] += dw * g

    def tix(i):
        return ((i // NC) * nts + jnp.maximum(i % NC - 1, 0), 0)

    return pl.pallas_call(
        body, name="loss_head", grid=(B * NC,),
        in_specs=[pl.BlockSpec((CH, D), lambda i: (i, 0)),
                  pl.BlockSpec((1, D), lambda i: (0, 0)),
                  pl.BlockSpec((CH, D), tix)],
        out_specs=[pl.BlockSpec((CH, D), lambda i: (i, 0)),
                   pl.BlockSpec((1, CH), lambda i: (0, 0)),
                   pl.BlockSpec((1, D), lambda i: (0, 0))],
        out_shape=[jax.ShapeDtypeStruct((T, D), F32),
                   jax.ShapeDtypeStruct((1, CH), F32),
                   jax.ShapeDtypeStruct((1, D), F32)],
        compiler_params=_cp(("arbitrary",)),
    )(x, nf, target)


def _ew(fn, ins, n_out, out_dtypes, *, name, tile=None):
    R, C = ins[0].shape
    tile = tile or _pick(R, (512, 256, 128, 64, 32, 16, 8, 1))
    if tile % 8 != 0:
        tile = R

    def body(*refs):
        outs = fn(*[r[...] for r in refs[:len(ins)]])
        if not isinstance(outs, (tuple, list)):
            outs = (outs,)
        for r, v in zip(refs[len(ins):], outs):
            r[...] = v.astype(r.dtype)

    spec = pl.BlockSpec((tile, C), lambda i: (i, 0))
    res = pl.pallas_call(
        body, name=name, grid=(R // tile,), in_specs=[spec] * len(ins), out_specs=[spec] * n_out,
        out_shape=[jax.ShapeDtypeStruct((R, C), dt) for dt in out_dtypes],
        compiler_params=_cp(("parallel",)),
    )(*ins)
    return res


def _adam_fn(w, g, m, v):
    m = ADAM_B1 * m + (1.0 - ADAM_B1) * g
    v = ADAM_B2 * v + (1.0 - ADAM_B2) * jnp.square(g)
    m_hat = m / (1.0 - ADAM_B1 ** ADAM_STEP)
    v_hat = v / (1.0 - ADAM_B2 ** ADAM_STEP)
    delta = -ADAM_LR * (m_hat / (jnp.sqrt(v_hat) + ADAM_EPS) + ADAM_WD * w)
    return delta, m, v


def _adam(w, g, m, v, name):
    shp = w.shape
    C = shp[-1]
    f = lambda a: a.reshape(-1, C)
    d, nm, nv = _ew(_adam_fn, [f(w), f(g), f(m), f(v)], 3, [F32] * 3, name=name)
    return d.reshape(shp), nm.reshape(shp), nv.reshape(shp)


def _me():
    return lax.axis_index("x"), lax.axis_index("y"), lax.axis_index("c")


def _all_gather(v, name):
    def body(x_ref, out_ref, send_sems, recv_sems, local_sem):
        x, y, c = _me()
        me, sibling = (x, y, c), (x, y, 1 - c)
        chips = [(1 - x, y), (x, 1 - y), (1 - x, 1 - y)]

        def slot(px, py, pc):
            return out_ref.at[4 * px + 2 * py + pc]

        def copy(k, block, to, src=None):
            return pltpu.make_async_remote_copy(
                src_ref=slot(*block) if src is None else src, dst_ref=slot(*block),
                send_sem=send_sems.at[k], recv_sem=recv_sems.at[k], device_id=to, device_id_type=MESH)

        mine = pltpu.make_async_copy(x_ref, slot(*me), local_sem)
        mine.start()
        first = [copy(0, me, sibling, src=x_ref)]
        first += [copy(1 + j, me, (*chip, c), src=x_ref) for j, chip in enumerate(chips)]
        for cp in first:
            cp.start()
        passed = [copy(4 + j, (*chip, c), sibling) for j, chip in enumerate(chips)]
        for j, chip in enumerate(chips):
            copy(1 + j, (*chip, c), me).wait_recv()
            passed[j].start()
        copy(0, sibling, me).wait_recv()
        for j, chip in enumerate(chips):
            copy(4 + j, (*chip, 1 - c), me).wait_recv()
        for cp in first + passed:
            cp.wait_send()
        mine.wait()

    return pl.pallas_call(
        body, name=name, out_shape=jax.ShapeDtypeStruct((N_DEV,) + v.shape, v.dtype),
        in_specs=[pl.BlockSpec(memory_space=pl.ANY)], out_specs=pl.BlockSpec(memory_space=pl.ANY),
        scratch_shapes=[pltpu.SemaphoreType.DMA((7,)), pltpu.SemaphoreType.DMA((7,)), pltpu.SemaphoreType.DMA],
    )(v)


def _swap_core(g, name):
    def body(g_ref, out_ref, send_sems, recv_sems):
        x, y, c = _me()
        cps = [pltpu.make_async_remote_copy(
            src_ref=g_ref.at[q, 1 - c], dst_ref=out_ref.at[q], send_sem=send_sems.at[q], recv_sem=recv_sems.at[q],
            device_id=(x, y, 1 - c), device_id_type=MESH) for q in range(4)]
        for cp in cps:
            cp.start()
        for cp in cps:
            cp.wait()

    return pl.pallas_call(
        body, name=name, out_shape=jax.ShapeDtypeStruct((4,) + g.shape[2:], g.dtype),
        in_specs=[pl.BlockSpec(memory_space=pl.ANY)], out_specs=pl.BlockSpec(memory_space=pl.ANY),
        scratch_shapes=[pltpu.SemaphoreType.DMA((4,)), pltpu.SemaphoreType.DMA((4,))],
    )(g)


def _swap_chips(hb, name):
    flips = [(1, 0), (0, 1), (1, 1)]

    def body(h_ref, out_ref, send_sems, recv_sems):
        x, y, c = _me()
        cps = []
        for j, (fx, fy) in enumerate(flips):
            px = x + fx - 2 * x * fx
            py = y + fy - 2 * y * fy
            cps.append(pltpu.make_async_remote_copy(
                src_ref=h_ref.at[2 * px + py], dst_ref=out_ref.at[j], send_sem=send_sems.at[j],
                recv_sem=recv_sems.at[j], device_id=(px, py, c), device_id_type=MESH))
        for cp in cps:
            cp.start()
        for cp in cps:
            cp.wait()

    return pl.pallas_call(
        body, name=name, out_shape=jax.ShapeDtypeStruct((3,) + hb.shape[1:], hb.dtype),
        in_specs=[pl.BlockSpec(memory_space=pl.ANY)], out_specs=pl.BlockSpec(memory_space=pl.ANY),
        scratch_shapes=[pltpu.SemaphoreType.DMA((3,)), pltpu.SemaphoreType.DMA((3,))],
    )(hb)


def _reduce_scatter(g8, tag):
    shard = g8.shape[1:]
    C = shard[-1]
    x, y, c = _me()
    g4 = g8.reshape((4, 2) + shard)
    got = _swap_core(g4, "rs_core_" + tag)
    mine = lax.dynamic_index_in_dim(g4, c, axis=1, keepdims=False)
    h, hb = _ew(lambda a, b: (a + b, a + b), [mine.reshape(-1, C), got.reshape(-1, C)], 2, [F32, BF16],
                name="rs_pair_sum_" + tag)
    own = lax.dynamic_index_in_dim(h.reshape((4,) + shard), 2 * x + y, axis=0, keepdims=False)
    got3 = _swap_chips(hb.reshape((4,) + shard), "rs_chips_" + tag)
    out, = _ew(lambda a, b, c_, d: a + b.astype(F32) + c_.astype(F32) + d.astype(F32),
               [own.reshape(-1, C)] + [got3[j].reshape(-1, C) for j in range(3)], 1, [F32],
               name="rs_chip_sum_" + tag)
    return out.reshape(shard)


def _sum8(a):
    out, = _ew(lambda *v: functools.reduce(lambda p, q: p + q, v), [a[k] for k in range(N_DEV)], 1, [F32],
               name="sum8")
    return out


def _pad_rows(flat, cols, mult):
    n = flat.shape[0]
    per = cols * mult
    tot = ((n + per - 1) // per) * per
    return jnp.pad(flat, (0, tot - n)).reshape(-1, cols)


class _Packer:
    def __init__(self, shapes, mult):
        self.shapes = shapes
        self.sizes = [int(np.prod(s)) for s in shapes]
        self.rows = [8 * ((n + 8 * D - 1) // (8 * D)) for n in self.sizes]
        tot = sum(self.rows)
        self.tail = (-tot) % mult

    def pack(self, arrs, dtype):
        parts = [_pad_rows(a.reshape(-1).astype(dtype), D, 8) for a in arrs]
        if self.tail:
            parts.append(jnp.zeros((self.tail, D), dtype))
        return jnp.concatenate(parts, axis=0)

    def unpack(self, buf):
        out, o = [], 0
        for s, n, r in zip(self.shapes, self.sizes, self.rows):
            out.append(buf[o:o + r].reshape(-1)[:n].reshape(s))
            o += r
        return out


def _w_cat(w_in_l):
    sm = jnp.concatenate([w_in_l[:, O_DT:O_DT + SSD_H], w_in_l[:, O_F:O_F + FOX_H],
                          jnp.zeros((D, CH - SSD_H - FOX_H), w_in_l.dtype)], axis=1)
    return jnp.concatenate([w_in_l[:, O_Z:O_XBC], w_in_l[:, O_XBC:O_DT], w_in_l[:, O_QKV:O_F],
                            w_in_l[:, O_U:O_G], w_in_l[:, O_G:D_IN], sm], axis=1)


def _w_uncat(g):
    return jnp.concatenate([g[:, OFF_Z:OFF_XBC], g[:, OFF_XBC:OFF_QKV], g[:, OFF_SM:OFF_SM + SSD_H],
                            g[:, OFF_QKV:OFF_U], g[:, OFF_SM + SSD_H:OFF_SM + SSD_H + FOX_H],
                            g[:, OFF_U:OFF_G], g[:, OFF_G:OFF_SM]], axis=1)


def _layer_fwd(x, p, geom, dims):
    B, NC, Lp, pad = dims
    T = geom[0]
    rm = functools.partial(_rowmap, geom=geom)
    sv = {}
    xn1, = rm(_f_norm, [x], [p["norm1"]], [(D, BF16)], [], tile=384 if Lp % 384 == 0 else CH, name="norm1")
    Wc = p["w_cat"]
    pz = _mm(xn1, Wc, "nn", BF16, n=D, b_off=OFF_Z, name="in_z")
    pxbc = _mm(xn1, Wc, "nn", F32, n=CONV_DIM, b_off=OFF_XBC, name="in_xbc")
    qkv = _mm(xn1, Wc, "nn", BF16, n=3 * D, b_off=OFF_QKV, name="in_qkv")
    pu = _mm(xn1, Wc, "nn", BF16, n=D, b_off=OFF_U, name="in_u")
    pg = _mm(xn1, Wc, "nn", BF16, n=3 * D, b_off=OFF_G, name="in_g")
    psm = _mm(xn1, Wc, "nn", F32, n=CH, b_off=OFF_SM, name="in_sm")
    t_r = 384 if Lp % 384 == 0 else CH
    sm, = rm(_smallact, [psm], [p["smallbias"]], [(CH, F32)], [], tile=t_r, name="smallact")
    xbc = _conv_fwd(pxbc, p["conv_w"], p["conv_b"], geom=geom, tile=t_r)
    y_ssd, states = _ssd_fwd(xbc, sm, p["a_log"], B=B, NC=NC)
    y_a, = rm(_ssd_post, [y_ssd, (xbc, D, 0), pz], [p["d_rep"], p["ssd_norm"]], [(D, BF16)], [], tile=t_r,
              name="ssd_post")
    cum = _cumsum_seq(sm, B=B, NC=NC, reverse=False, name="fox_cum")
    cumT = _fox_keybias(cum, B=B, Lp=Lp, pad=pad)
    y_b, lse = _fox_fwd(qkv, cumT, B=B, Lp=Lp)
    y_ssm, hst = _s5_fwd(pu, p["Bsg"], p["Csg"], p["lam"], B=B, NC=NC)
    y1, = rm(_s5_pre, [y_ssm, pu], [p["s5_d"]], [(D, BF16)], [], tile=t_r, name="s5_pre")
    tg = _mm(y1, p["w_glu"], "nn", BF16, name="s5_glu_mm")
    y_c, = rm(_s5_glu, [y_ssm, pu, tg], [p["s5_d"]], [(D, BF16)], [], tile=t_r, name="s5_glu")
    br = [_mm(yy, p["w_branch"][n], "nn", BF16, name=f"branch{n}") for n, yy in enumerate((y_a, y_b, y_c))]
    mix, = rm(_merge, [(pg, D, 0), (pg, D, 1), (pg, D, 2)] + br, [], [(D, BF16)], [], tile=t_r, name="merge")
    x_mid = _mm(mix, p["w_out"], "nn", F32, res=x, name="out_proj")
    xn2, = rm(_f_norm, [x_mid], [p["norm2"]], [(D, BF16)], [], tile=t_r, name="norm2")
    hff = _mm(xn2, p["w_ffn_in"], "nn", BF16, name="ffn_in")
    act, = rm(_swiglu, [(hff, DFF, 0), (hff, DFF, 1)], [], [(DFF, BF16)], [], tile=CH, name="swiglu")
    x_out = _mm(act, p["w_ffn_out"], "nn", F32, res=x_mid, name="ffn_out")
    sv.update(x=x, xn1=xn1, pz=pz, pxbc=pxbc, qkv=qkv, pu=pu, pg=pg, psm=psm, sm=sm, xbc=xbc, y_ssd=y_ssd,
              states=states, y_a=y_a, cum=cum, cumT=cumT, y_b=y_b, lse=lse, y_ssm=y_ssm, hst=hst, y1=y1, tg=tg,
              y_c=y_c, br=br, mix=mix, x_mid=x_mid, xn2=xn2, hff=hff, act=act)
    return x_out, sv


def _layer_bwd(dx_out, p, sv, geom, dims):
    B, NC, Lp, pad = dims
    T = geom[0]
    rm = functools.partial(_rowmap, geom=geom)
    t_r = 384 if Lp % 384 == 0 else CH
    g = {}
    dact = _mm(dx_out, p["w_ffn_out"], "nt", BF16, name="ffn_out_dx")
    g["w_ffn_out"] = _mm(sv["act"], dx_out, "tn", F32, name="ffn_out_dw")
    dhff, = rm(_b_swiglu, [(sv["hff"], DFF, 0), (sv["hff"], DFF, 1), dact], [], [(2 * DFF, BF16)], [], tile=CH,
               name="swiglu_bwd")
    dxn2 = _mm(dhff, p["w_ffn_in"], "nt", F32, name="ffn_in_dx")
    g["w_ffn_in"] = _mm(sv["xn2"], dhff, "tn", F32, name="ffn_in_dw")
    dx_mid, g["norm2"] = rm(_b_norm, [sv["x_mid"], dxn2, dx_out], [p["norm2"]], [(D, F32)], [(1, D)], tile=t_r,
                            name="norm2_bwd")
    dmix = _mm(dx_mid, p["w_out"], "nt", BF16, name="out_proj_dx")
    g["w_out"] = _mm(sv["mix"], dx_mid, "tn", F32, name="out_proj_dw")
    pg = sv["pg"]
    dpg, db0, db1, db2 = rm(_b_merge, [(pg, D, 0), (pg, D, 1), (pg, D, 2)] + sv["br"] + [dmix], [],
                            [(3 * D, BF16), (D, BF16), (D, BF16), (D, BF16)], [], tile=t_r, name="merge_bwd")
    ys = (sv["y_a"], sv["y_b"], sv["y_c"])
    dbs = (db0, db1, db2)
    g["w_branch"] = [_mm(ys[n], dbs[n], "tn", F32, name=f"branch{n}_dw") for n in range(3)]
    dy = [_mm(dbs[n], p["w_branch"][n], "nt", BF16, name=f"branch{n}_dx") for n in range(3)]
    dtg, dy1a = rm(_b_s5_glu, [sv["y_ssm"], sv["pu"], sv["tg"], dy[2]], [p["s5_d"]], [(D, BF16), (D, F32)], [],
                   tile=t_r, name="s5_glu_bwd")
    dy1b = _mm(dtg, p["w_glu"], "nt", BF16, name="s5_glu_mm_dx")
    g["w_glu"] = _mm(sv["y1"], dtg, "tn", F32, name="s5_glu_mm_dw")
    dys, du_skip, g["s5_d"] = rm(_b_s5_pre, [sv["y_ssm"], sv["pu"], dy1a, dy1b], [p["s5_d"]],
                                 [(D, F32), (D, F32)], [(1, D)], tile=t_r, name="s5_pre_bwd")
    du, g["Bsg"], g["Csg"], g["lam"] = _s5_bwd(sv["pu"], sv["hst"], dys, du_skip, p["Bsg"], p["Csg"], p["lam"],
                                               B=B, NC=NC)
    dq, dk, dv, dckT, dcq = _fox_bwd(sv["qkv"], dy[1], sv["y_b"], sv["lse"], sv["cumT"], B=B, Lp=Lp)
    dcum8 = dcq.reshape(B, FOX_H, Lp).transpose(0, 2, 1) + dckT.reshape(B, FOX_H, Lp).transpose(0, 2, 1)
    dcum = jnp.pad(dcum8.reshape(T, FOX_H), ((0, 0), (SSD_H, CH - SSD_H - FOX_H)))
    dlogf = _cumsum_seq(dcum, B=B, NC=NC, reverse=True, name="fox_cum_bwd")
    dy_ssd, dxs_skip, dz, g["d_rep"], g["ssd_norm"] = rm(
        _b_ssd_post, [sv["y_ssd"], (sv["xbc"], D, 0), sv["pz"], dy[0]], [p["d_rep"], p["ssd_norm"]],
        [(D, F32), (D, F32), (D, BF16)], [(1, D), (1, D)], tile=t_r, name="ssd_post_bwd")
    dxbc_act, ddt, g["a_log"] = _ssd_bwd(sv["xbc"], sv["sm"], p["a_log"], sv["states"], dy_ssd, dxs_skip, B=B, NC=NC)
    dpsm, g["smallbias"] = rm(_b_smallact, [sv["psm"], ddt, dlogf], [p["smallbias"]], [(CH, BF16)], [(1, CH)],
                              tile=t_r, name="smallact_bwd")
    dconv, g["conv_w"], g["conv_b"] = _conv_bwd_pre(sv["pxbc"], dxbc_act, p["conv_w"], p["conv_b"], geom=geom, tile=t_r)
    dpxbc = _conv_bwd_x(dconv, p["conv_w"], geom=geom, tile=t_r)
    dproj = jnp.concatenate([dz, dpxbc, dq.astype(BF16), dk, dv, du, dpg, dpsm], axis=1)
    dxn1 = _mm(dproj, p["w_cat"], "nt", F32, name="in_dx")
    g["w_cat"] = _mm(sv["xn1"], dproj, "tn", F32, name="in_dw")
    dx_in, g["norm1"] = rm(_b_norm, [sv["x"], dxn1, dx_mid], [p["norm1"]], [(D, F32)], [(1, D)], tile=t_r,
                           name="norm1_bwd")
    return dx_in, g


_BIG = ["w_in", "s5_w_glu", "w_branch", "w_out", "w_ffn_in", "w_ffn_out"]
_NAMES = ['meta', 'norm1', 'w_in', 'ssd_conv_w', 'ssd_conv_b', 'ssd_dt_bias', 'ssd_a_log', 'ssd_d', 'ssd_norm',
          'fox_bf', 's5_lam_re', 's5_lam_im', 's5_b_re', 's5_b_im', 's5_c_re', 's5_c_im', 's5_log_step', 's5_d',
          's5_w_glu', 'w_branch', 'w_out', 'norm2', 'w_ffn_in', 'w_ffn_out', 'norm_f']
_SHARD_AXIS = {"meta": 1, "ssd_conv_w": 2}
_BIG_AXIS = {"w_in": 2, "s5_w_glu": 1, "w_branch": 2, "w_out": 1, "w_ffn_in": 2, "w_ffn_out": 1}


def kernel(x, meta, norm1, w_in, ssd_conv_w, ssd_conv_b, ssd_dt_bias, ssd_a_log, ssd_d, ssd_norm, fox_bf, s5_lam_re, s5_lam_im, s5_b_re, s5_b_im, s5_c_re, s5_c_im, s5_log_step, s5_d, s5_w_glu, w_branch, w_out, norm2, w_ffn_in, w_ffn_out, norm_f, loss_target, m_meta, m_norm1, m_w_in, m_ssd_conv_w, m_ssd_conv_b, m_ssd_dt_bias, m_ssd_a_log, m_ssd_d, m_ssd_norm, m_fox_bf, m_s5_lam_re, m_s5_lam_im, m_s5_b_re, m_s5_b_im, m_s5_c_re, m_s5_c_im, m_s5_log_step, m_s5_d, m_s5_w_glu, m_w_branch, m_w_out, m_norm2, m_w_ffn_in, m_w_ffn_out, m_norm_f, v_meta, v_norm1, v_w_in, v_ssd_conv_w, v_ssd_conv_b, v_ssd_dt_bias, v_ssd_a_log, v_ssd_d, v_ssd_norm, v_fox_bf, v_s5_lam_re, v_s5_lam_im, v_s5_b_re, v_s5_b_im, v_s5_c_re, v_s5_c_im, v_s5_log_step, v_s5_d, v_s5_w_glu, v_w_branch, v_w_out, v_norm2, v_w_ffn_in, v_w_ffn_out, v_norm_f):
    args = locals()
    W = {n: args[n] for n in _NAMES}
    Mo = {n: args["m_" + n] for n in _NAMES}
    Vo = {n: args["v_" + n] for n in _NAMES}
    B, S, _ = x.shape
    depth = norm1.shape[0]
    L = S + N_META
    Lp = ((L + CH - 1) // CH) * CH
    pad = Lp - L
    assert pad + N_META == CH and S % CH == 0
    NC = Lp // CH
    T = B * Lp
    geom = (T, Lp, pad)
    dims = (B, NC, Lp, pad)
    xi, yi, ci = _me()
    dev = 4 * xi + 2 * yi + ci

    gath = {n: _all_gather(W[n].astype(BF16), "gather_" + n) for n in _BIG}
    full = {n: jnp.concatenate([gath[n][k] for k in range(N_DEV)], axis=_BIG_AXIS[n]) for n in _BIG}
    sm_pack = _Packer([meta.shape, ssd_conv_w.shape], 8)
    sm_g = _all_gather(sm_pack.pack([meta, ssd_conv_w], F32), "gather_small")
    sm_parts = [sm_pack.unpack(sm_g[k]) for k in range(N_DEV)]
    meta_full = jnp.concatenate([sm_parts[k][0] for k in range(N_DEV)], axis=1)
    conv_w_full = jnp.concatenate([sm_parts[k][1] for k in range(N_DEV)], axis=2)

    layers = []
    s5_in = []
    for l in range(depth):
        lre = _s5_tile(s5_lam_re[l])
        lim = _s5_tile(s5_lam_im[l])
        lst = _s5_tile(jnp.repeat(s5_log_step[l], S5_P))
        bre = _s5_tile_b(s5_b_re[l])
        bim = _s5_tile_b(s5_b_im[l])
        s5_in.append((lre, lim, lst, bre, bim))
        a, b, br_, bi_ = _s5_params(lre, lim, lst, bre, bim)
        Bsg, Csg = _s5_blockdiag(_s5_untile_b(br_), _s5_untile_b(bi_), s5_c_re[l], s5_c_im[l])
        NS = S5_G // S5_SG
        lam = jnp.concatenate([a.reshape(NS, 1, S5_W), b.reshape(NS, 1, S5_W)], axis=2)
        zpad = jnp.zeros((CH - SSD_H - FOX_H,), F32)
        layers.append(dict(
            norm1=norm1[l][None], w_cat=_w_cat(full["w_in"][l]),
            smallbias=jnp.concatenate([ssd_dt_bias[l], fox_bf[l], zpad])[None],
            conv_w=conv_w_full[l], conv_b=ssd_conv_b[l][None],
            a_log=jnp.concatenate([ssd_a_log[l], jnp.zeros((CH - SSD_H,), F32)])[None],
            d_rep=jnp.repeat(ssd_d[l], SSD_P)[None], ssd_norm=ssd_norm[l][None],
            Bsg=Bsg, Csg=Csg, lam=lam, s5_d=s5_d[l][None], w_glu=full["s5_w_glu"][l],
            w_branch=[full["w_branch"][l, n] for n in range(3)], w_out=full["w_out"][l],
            norm2=norm2[l][None], w_ffn_in=full["w_ffn_in"][l], w_ffn_out=full["w_ffn_out"][l]))

    xs = jnp.concatenate([jnp.zeros((B, pad, D), F32), jnp.broadcast_to(meta_full[None], (B, N_META, D)), x], axis=1)
    h = xs.reshape(T, D)
    saved = []
    for l in range(depth):
        h, sv = _layer_fwd(h, layers[l], geom, dims)
        saved.append(sv)
    dh, loss_row, g_nf = _loss_head(h, norm_f[None], loss_target.reshape(B * S, D), B=B, NC=NC, S=S)
    loss = lax.psum(loss_row[0, 0], AXES)

    G = {n: [None] * depth for n in _NAMES}
    for l in reversed(range(depth)):
        dh, g = _layer_bwd(dh, layers[l], saved[l], geom, dims)
        saved[l] = None
        G["norm1"][l] = g["norm1"][0]
        G["norm2"][l] = g["norm2"][0]
        G["w_in"][l] = _w_uncat(g["w_cat"])
        G["ssd_conv_w"][l] = g["conv_w"]
        G["ssd_conv_b"][l] = g["conv_b"][0]
        G["ssd_dt_bias"][l] = g["smallbias"][0, 0:SSD_H]
        G["fox_bf"][l] = g["smallbias"][0, SSD_H:SSD_H + FOX_H]
        G["ssd_a_log"][l] = g["a_log"][0, 0:SSD_H]
        G["ssd_d"][l] = g["d_rep"].reshape(SSD_H, SSD_P).sum(axis=1)
        G["ssd_norm"][l] = g["ssd_norm"][0]
        dbr, dbi, dcr, dci = _s5_unblock(g["Bsg"], g["Csg"])
        da = _s5_tile(g["lam"][:, 0, 0:S5_W])
        db = _s5_tile(g["lam"][:, 0, S5_W:])
        dlre, dlim, dlst, dbre, dbim = _s5_params_bwd(*s5_in[l], da, db, _s5_tile_b(dbr), _s5_tile_b(dbi))
        G["s5_lam_re"][l] = dlre.reshape(S5_G, S5_P)
        G["s5_lam_im"][l] = dlim.reshape(S5_G, S5_P)
        G["s5_log_step"][l] = dlst.reshape(S5_G, S5_P).sum(axis=1)
        G["s5_b_re"][l] = _s5_untile_b(dbre)
        G["s5_b_im"][l] = _s5_untile_b(dbim)
        G["s5_c_re"][l] = dcr
        G["s5_c_im"][l] = dci
        G["s5_d"][l] = g["s5_d"][0]
        G["s5_w_glu"][l] = g["w_glu"]
        G["w_branch"][l] = jnp.stack(g["w_branch"])
        G["w_out"][l] = g["w_out"]
        G["w_ffn_in"][l] = g["w_ffn_in"]
        G["w_ffn_out"][l] = g["w_ffn_out"]
    dxs = dh.reshape(B, Lp, D)
    grad_x = dxs[:, pad + N_META:, :]
    part = {n: jnp.stack(G[n]) for n in _NAMES if n not in ("meta", "norm_f")}
    part["meta"] = dxs[:, pad:pad + N_META, :].sum(axis=0)
    part["norm_f"] = g_nf[0]

    grads = {}
    for n in _BIG:
        ax = _BIG_AXIS[n]
        a = part[n]
        a = a.reshape(a.shape[:ax] + (N_DEV, a.shape[ax] // N_DEV) + a.shape[ax + 1:])
        grads[n] = _reduce_scatter(jnp.moveaxis(a, ax, 0), n)

    small = [n for n in _NAMES if n not in _BIG]
    sp = _Packer([part[n].shape for n in small], 128)
    tot = sp.unpack(_sum8(_all_gather(sp.pack([part[n] for n in small], F32), "gather_small_grads")))
    for n, t in zip(small, tot):
        if n in _SHARD_AXIS:
            ax = _SHARD_AXIS[n]
            w = W[n].shape[ax]
            t = lax.dynamic_slice_in_dim(t, dev * w, w, axis=ax)
        grads[n] = t

    delta, new_m, new_v = {}, {}, {}
    for n in _BIG:
        delta[n], new_m[n], new_v[n] = _adam(W[n], grads[n], Mo[n], Vo[n], "adam_" + n)
    ap = _Packer([W[n].shape for n in small], 128)
    d_, m_, v_ = _adam(ap.pack([W[n] for n in small], F32), ap.pack([grads[n] for n in small], F32),
                       ap.pack([Mo[n] for n in small], F32), ap.pack([Vo[n] for n in small], F32), "adam_small")
    for n, a, b, c in zip(small, ap.unpack(d_), ap.unpack(m_), ap.unpack(v_)):
        delta[n], new_m[n], new_v[n] = a, b, c
    return (loss, grad_x, *[grads[n] for n in _NAMES], *[delta[n] for n in _NAMES],
            *[new_m[n] for n in _NAMES], *[new_v[n] for n in _NAMES])
```

```python
import functools
import math

import numpy as np
import jax
import jax.numpy as jnp
from jax import lax
from jax.experimental import pallas as pl
from jax.experimental.pallas import tpu as pltpu

F32 = jnp.float32
BF16 = jnp.bfloat16
AXES = ("x", "y", "c")
MESH = pl.DeviceIdType.MESH
N_DEV = 8

D = 1024
N_META = 16
CH = 128
EPS = 1e-6
NEG = -1e30
SSD_H, SSD_P, SSD_N, SSD_G = 16, 64, 128, 2
CONV_K, CONV_DIM = 4, 1536
FOX_H, FOX_DH = 8, 128
S5_G, S5_P, S5_C = 64, 64, 16
S5_SG = 8
S5_W = S5_SG * S5_P
DFF = 2816
D_IN = 9752
OFF_Z, OFF_XBC, OFF_QKV, OFF_U, OFF_G, OFF_SM, D_CAT = 0, 1024, 2560, 5632, 6656, 9728, 9856
O_Z, O_XBC, O_DT, O_QKV, O_F, O_U, O_G = 0, 1024, 2560, 2576, 5648, 5656, 6680

ADAM_LR, ADAM_B1, ADAM_B2, ADAM_EPS, ADAM_WD, ADAM_STEP = 0.001, 0.9, 0.999, 1e-08, 0.01, 10

VMEM_LIMIT_V7X = 52 * 1024 * 1024
HI = lax.Precision.HIGHEST


def _cp(sem=None):
    return pltpu.CompilerParams(dimension_semantics=sem, vmem_limit_bytes=VMEM_LIMIT_V7X)


def _pick(n, cands):
    for c in cands:
        if n % c == 0:
            return c
    raise ValueError(f"no tile for {n}")


_TILES = (1408, 1024, 896, 768, 512, 384, 256, 128)


def _mm(a, b, mode, out_dtype, *, name, n=None, b_off=0, res=None, tm=None, tn=None, tk=None):
    if mode == "tn":
        K, M = a.shape
    else:
        M, K = a.shape
    if mode == "nt":
        N = b.shape[0]
    else:
        N = n if n is not None else b.shape[1]
    wide = (1408,) if mode == "tn" else ()
    tm = tm or _pick(M, (1408, 1024, 768, 512, 384, 256, 128, 64, 16, 8))
    tn = tn or _pick(math.gcd(N, b_off) if b_off else N, wide + (1024, 896, 768, 512, 384, 256, 128))
    tk = tk or _pick(K, _TILES)
    nk = K // tk
    joff = b_off // tn

    def body(*refs):
        if res is None:
            a_ref, b_ref, o_ref, acc = refs
            r_ref = None
        else:
            a_ref, b_ref, r_ref, o_ref, acc = refs
        k = pl.program_id(2)
        av = a_ref[...].astype(BF16)
        bv = b_ref[...].astype(BF16)
        if mode == "nn":
            p = jnp.dot(av, bv, preferred_element_type=F32)
        elif mode == "nt":
            p = lax.dot_general(av, bv, (((1,), (1,)), ((), ())), preferred_element_type=F32)
        else:
            p = lax.dot_general(av, bv, (((0,), (0,)), ((), ())), preferred_element_type=F32)

        @pl.when(k == 0)
        def _():
            acc[...] = p

        @pl.when(k > 0)
        def _():
            acc[...] += p

        @pl.when(k == nk - 1)
        def _():
            r = acc[...]
            if r_ref is not None:
                r = r + r_ref[...]
            o_ref[...] = r.astype(o_ref.dtype)

    if mode == "tn":
        a_spec = pl.BlockSpec((tk, tm), lambda i, j, k: (k, i))
    else:
        a_spec = pl.BlockSpec((tm, tk), lambda i, j, k: (i, k))
    if mode == "nt":
        b_spec = pl.BlockSpec((tn, tk), lambda i, j, k: (j, k))
    else:
        b_spec = pl.BlockSpec((tk, tn), lambda i, j, k: (k, j + joff))
    o_spec = pl.BlockSpec((tm, tn), lambda i, j, k: (i, j))
    in_specs = [a_spec, b_spec] + ([o_spec] if res is not None else [])
    args = (a, b) + ((res,) if res is not None else ())
    return pl.pallas_call(
        body, name=name, grid=(M // tm, N // tn, nk),
        in_specs=in_specs, out_specs=o_spec,
        out_shape=jax.ShapeDtypeStruct((M, N), out_dtype),
        scratch_shapes=[pltpu.VMEM((tm, tn), F32)],
        compiler_params=_cp(("parallel", "parallel", "arbitrary")),
    )(*args)


def _rowmap(fn, row_ins, const_ins, row_outs, acc_outs, *, geom, tile, name):
    T, Lp, pad = geom
    assert Lp % tile == 0
    per_seq = Lp // tile
    specs, args = [], []
    for r in row_ins:
        arr, w, cb = r if isinstance(r, tuple) else (r, r.shape[1], 0)
        specs.append(pl.BlockSpec((tile, w), functools.partial(lambda i, cb: (i, cb), cb=cb)))
        args.append(arr)
    for c in const_ins:
        specs.append(pl.BlockSpec(c.shape, functools.partial(lambda i, nd: (0,) * nd, nd=c.ndim)))
        args.append(c)
    n_r, n_c, n_o, n_a = len(row_ins), len(const_ins), len(row_outs), len(acc_outs)
    out_specs = [pl.BlockSpec((tile, w), lambda i: (i, 0)) for w, _ in row_outs]
    out_specs += [pl.BlockSpec(s, lambda i: (0, 0)) for s in acc_outs]
    out_shape = [jax.ShapeDtypeStruct((T, w), dt) for w, dt in row_outs]
    out_shape += [jax.ShapeDtypeStruct(s, F32) for s in acc_outs]

    def body(*refs):
        i = pl.program_id(0)
        pos = (i % per_seq) * tile + lax.broadcasted_iota(jnp.int32, (tile, 1), 0)
        valid = pos >= pad
        vals = [r[...].astype(F32) for r in refs[:n_r]] + [r[...] for r in refs[n_r:n_r + n_c]]
        outs = fn(valid, *vals)
        if not isinstance(outs, (tuple, list)):
            outs = (outs,)
        orefs = refs[n_r + n_c:]
        for r, v in zip(orefs[:n_o], outs[:n_o]):
            r[...] = v.astype(r.dtype)
        for r, v in zip(orefs[n_o:], outs[n_o:]):
            @pl.when(i == 0)
            def _(r=r, v=v):
                r[...] = v

            @pl.when(i > 0)
            def _(r=r, v=v):
                r[...] += v

    res = pl.pallas_call(
        body, name=name, grid=(T // tile,), in_specs=specs, out_specs=out_specs, out_shape=out_shape,
        compiler_params=_cp(("arbitrary",)),
    )(*args)
    return res


def _sigmoid(x):
    return 1.0 / (1.0 + jnp.exp(-x))


def _silu(x):
    return x * _sigmoid(x)


def _softplus(x):
    return jnp.maximum(x, 0.0) + jnp.log(1.0 + jnp.exp(-jnp.abs(x)))


def _gelu(x):
    return 0.5 * x * (1.0 + jnp.tanh(math.sqrt(2.0 / math.pi) * (x + 0.044715 * x * x * x)))


def _rms(x, w):
    return x * lax.rsqrt(jnp.mean(x * x, axis=-1, keepdims=True) + EPS) * w


def _colsum(v):
    return jnp.sum(v, axis=0, keepdims=True)


def _f_norm(valid, x, w):
    return _rms(x, w)


def _b_norm(valid, x, dxn, dres, w):
    _, vjp = jax.vjp(_rms, x, w)
    dx, dw = vjp(dxn)
    return jnp.where(valid, dx + dres, 0.0), dw


def _smallact(valid, raw, bias):
    lane = lax.broadcasted_iota(jnp.int32, raw.shape, 1)
    v = raw + bias
    dt = _softplus(v)
    logf = -_softplus(-v)
    out = jnp.where(lane < SSD_H, dt, jnp.where(lane < SSD_H + FOX_H, logf, 0.0))
    return jnp.where(valid, out, 0.0)


def _b_smallact(valid, raw, d1, d2, bias):
    _, vjp = jax.vjp(lambda r, b: _smallact(valid, r, b), raw, bias)
    return vjp(d1 + d2)


def _ssd_post(valid, y, xs, z, drep, nw):
    y = (y + xs * drep) * _silu(z)
    return _rms(y, nw)


def _b_ssd_post(valid, y, xs, z, dya, drep, nw):
    _, vjp = jax.vjp(lambda a, b, c, d, e: _ssd_post(valid, a, b, c, d, e), y, xs, z, drep, nw)
    dy, dxs, dz, dd, dn = vjp(dya)
    return dy, dxs, dz, dd, dn


def _s5_pre(valid, ys, u, d):
    return _gelu(ys + d * u)


def _s5_glu(valid, ys, u, t, d):
    y1 = _gelu(ys + d * u)
    return y1 * _sigmoid(t)


def _b_s5_glu(valid, ys, u, t, dyc, d):
    y1 = _gelu(ys + d * u)
    _, vjp = jax.vjp(lambda a, b: a * _sigmoid(b), y1, t)
    dy1, dt = vjp(dyc)
    return dt, dy1


def _b_s5_pre(valid, ys, u, dy1a, dy1b, d):
    _, vjp = jax.vjp(lambda a, b, c: _gelu(a + c * b), ys, u, d)
    dys, du, dd = vjp(dy1a + dy1b)
    return dys, du, dd


def _merge(valid, g0, g1, g2, b0, b1, b2):
    m = _sigmoid(g0) * b0 + _sigmoid(g1) * b1 + _sigmoid(g2) * b2
    return jnp.where(valid, m, 0.0)


def _b_merge(valid, g0, g1, g2, b0, b1, b2, dmix):
    _, vjp = jax.vjp(lambda *a: _merge(valid, *a), g0, g1, g2, b0, b1, b2)
    d = vjp(dmix)
    return jnp.concatenate(d[:3], axis=1), d[3], d[4], d[5]


def _swiglu(valid, g, up):
    return _silu(g) * up


def _b_swiglu(valid, g, up, dact):
    _, vjp = jax.vjp(lambda a, b: _silu(a) * b, g, up)
    dg, dup = vjp(dact)
    return jnp.concatenate([dg, dup], axis=1)


def _conv_taps(ext, tile):
    taps = []
    for k in range(CONV_K):
        sh = CONV_K - 1 - k
        v = ext if sh == 0 else pltpu.roll(ext, sh, 0)
        taps.append(v[8:8 + tile])
    return taps


def _conv_fwd(x, w, b, *, geom, tile):
    T, Lp, pad = geom
    per_seq = Lp // tile
    hb = tile // 8

    def body(x_ref, h_ref, w_ref, b_ref, o_ref):
        i = pl.program_id(0)
        pos = (i % per_seq) * tile + lax.broadcasted_iota(jnp.int32, (tile, 1), 0)
        ext = jnp.concatenate([h_ref[...], x_ref[...]], axis=0)
        taps = _conv_taps(ext, tile)
        acc = b_ref[...] + taps[0] * w_ref[0:1, :]
        for k in range(1, CONV_K):
            acc = acc + taps[k] * w_ref[k:k + 1, :]
        o_ref[...] = jnp.where(pos >= pad, _silu(acc), 0.0)

    return pl.pallas_call(
        body, name="conv_fwd", grid=(T // tile,),
        in_specs=[pl.BlockSpec((tile, CONV_DIM), lambda i: (i, 0)),
                  pl.BlockSpec((8, CONV_DIM), lambda i: (jnp.maximum(i * hb - 1, 0), 0)),
                  pl.BlockSpec((CONV_K, CONV_DIM), lambda i: (0, 0)),
                  pl.BlockSpec((1, CONV_DIM), lambda i: (0, 0))],
        out_specs=pl.BlockSpec((tile, CONV_DIM), lambda i: (i, 0)),
        out_shape=jax.ShapeDtypeStruct((T, CONV_DIM), F32),
        compiler_params=_cp(("arbitrary",)),
    )(x, x, w, b)


def _conv_bwd_pre(x, dact, w, b, *, geom, tile):
    T, Lp, pad = geom
    per_seq = Lp // tile
    hb = tile // 8

    def body(x_ref, h_ref, d_ref, w_ref, b_ref, dc_ref, dw_ref, db_ref):
        i = pl.program_id(0)
        pos = (i % per_seq) * tile + lax.broadcasted_iota(jnp.int32, (tile, 1), 0)
        ext = jnp.concatenate([h_ref[...], x_ref[...]], axis=0)
        taps = _conv_taps(ext, tile)
        acc = b_ref[...] + taps[0] * w_ref[0:1, :]
        for k in range(1, CONV_K):
            acc = acc + taps[k] * w_ref[k:k + 1, :]
        sg = _sigmoid(acc)
        dsilu = sg * (1.0 + acc * (1.0 - sg))
        dc = jnp.where(pos >= pad, d_ref[...] * dsilu, 0.0)
        dc_ref[...] = dc
        dw = jnp.concatenate([_colsum(dc * taps[k]) for k in range(CONV_K)], axis=0)
        db = _colsum(dc)

        @pl.when(i == 0)
        def _():
            dw_ref[...] = dw
            db_ref[...] = db

        @pl.when(i > 0)
        def _():
            dw_ref[...] += dw
            db_ref[...] += db

    return pl.pallas_call(
        body, name="conv_bwd_pre", grid=(T // tile,),
        in_specs=[pl.BlockSpec((tile, CONV_DIM), lambda i: (i, 0)),
                  pl.BlockSpec((8, CONV_DIM), lambda i: (jnp.maximum(i * hb - 1, 0), 0)),
                  pl.BlockSpec((tile, CONV_DIM), lambda i: (i, 0)),
                  pl.BlockSpec((CONV_K, CONV_DIM), lambda i: (0, 0)),
                  pl.BlockSpec((1, CONV_DIM), lambda i: (0, 0))],
        out_specs=[pl.BlockSpec((tile, CONV_DIM), lambda i: (i, 0)),
                   pl.BlockSpec((CONV_K, CONV_DIM), lambda i: (0, 0)),
                   pl.BlockSpec((1, CONV_DIM), lambda i: (0, 0))],
        out_shape=[jax.ShapeDtypeStruct((T, CONV_DIM), F32),
                   jax.ShapeDtypeStruct((CONV_K, CONV_DIM), F32),
                   jax.ShapeDtypeStruct((1, CONV_DIM), F32)],
        compiler_params=_cp(("arbitrary",)),
    )(x, x, dact, w, b)


def _conv_bwd_x(dc, w, *, geom, tile):
    T, Lp, pad = geom
    nt = T // tile
    hb = tile // 8

    def body(d_ref, h_ref, w_ref, o_ref):
        i = pl.program_id(0)
        halo = jnp.where(i < nt - 1, h_ref[...], 0.0)
        ext = jnp.concatenate([d_ref[...], halo], axis=0)
        n_ext = tile + 8
        acc = ext[0:tile] * w_ref[CONV_K - 1:CONV_K, :]
        for j in range(1, CONV_K):
            acc = acc + pltpu.roll(ext, n_ext - j, 0)[0:tile] * w_ref[CONV_K - 1 - j:CONV_K - j, :]
        o_ref[...] = acc.astype(o_ref.dtype)

    return pl.pallas_call(
        body, name="conv_bwd_x", grid=(nt,),
        in_specs=[pl.BlockSpec((tile, CONV_DIM), lambda i: (i, 0)),
                  pl.BlockSpec((8, CONV_DIM), lambda i: (jnp.minimum((i + 1) * hb, nt * hb - 1), 0)),
                  pl.BlockSpec((CONV_K, CONV_DIM), lambda i: (0, 0))],
        out_specs=pl.BlockSpec((tile, CONV_DIM), lambda i: (i, 0)),
        out_shape=jax.ShapeDtypeStruct((T, CONV_DIM), BF16),
        compiler_params=_cp(("arbitrary",)),
    )(dc, dc, w)


def _ssd_common(sm_ref, alog_ref):
    lane = lax.broadcasted_iota(jnp.int32, (1, CH), 1)
    A = jnp.where(lane < SSD_H, -jnp.exp(alog_ref[...]), 0.0)
    dt = sm_ref[...]
    adt = dt * A
    r = lax.broadcasted_iota(jnp.int32, (CH, CH), 0)
    c = lax.broadcasted_iota(jnp.int32, (CH, CH), 1)
    tril = (r >= c).astype(F32)
    cs = jnp.dot(tril, adt, precision=HI, preferred_element_type=F32)
    csT = cs.T
    cs_last = jnp.sum(jnp.where(r == CH - 1, cs, 0.0), axis=0, keepdims=True)
    return A, dt, cs, csT, cs_last, tril, r, c


def _ssd_lanes():
    r = lax.broadcasted_iota(jnp.int32, (CH, D), 0)
    c = lax.broadcasted_iota(jnp.int32, (CH, D), 1)
    return (c // SSD_P == r).astype(F32)


def _ssd_per_lane(dt, cs):
    ex = _ssd_lanes()
    dt_rep = jnp.dot(dt, ex, precision=HI, preferred_element_type=F32)
    cs_rep = jnp.dot(cs, ex, precision=HI, preferred_element_type=F32)
    r = lax.broadcasted_iota(jnp.int32, (CH, D), 0)
    last_rep = jnp.sum(jnp.where(r == CH - 1, cs_rep, 0.0), axis=0, keepdims=True)
    return dt_rep, cs_rep, last_rep


def _ssd_per_head(*per_lane):
    ex = _ssd_lanes()
    return [lax.dot_general(v, ex, (((1,), (1,)), ((), ())), precision=HI, preferred_element_type=F32)
            for v in per_lane]


def _nt(a, b):
    return lax.dot_general(a, b, (((1,), (1,)), ((), ())), preferred_element_type=F32)


def _tn(a, b):
    return lax.dot_general(a, b, (((0,), (0,)), ((), ())), preferred_element_type=F32)


def _nn(a, b):
    return jnp.dot(a, b, preferred_element_type=F32)


def _ssd_fwd(xbc, sm, alog, *, B, NC):
    T = B * NC * CH

    def body(x_ref, sm_ref, alog_ref, y_ref, st_ref, S):
        cidx = pl.program_id(1)

        @pl.when(cidx == 0)
        def _():
            S[...] = jnp.zeros_like(S)

        st_ref[0] = S[...]
        A, dt, cs, csT, cs_last, tril, _, _ = _ssd_common(sm_ref, alog_ref)
        dt_rep, cs_rep, last_rep = _ssd_per_lane(dt, cs)
        xdt = x_ref[:, 0:D] * dt_rep
        xdec = (xdt * jnp.exp(last_rep - cs_rep)).astype(BF16)
        e_rep = jnp.exp(cs_rep)
        HG = SSD_H // SSD_G
        for g in range(SSD_G):
            gl = slice(g * HG * SSD_P, (g + 1) * HG * SSD_P)
            Bb = x_ref[:, D + g * SSD_N:D + (g + 1) * SSD_N].astype(BF16)
            Cb = x_ref[:, D + SSD_G * SSD_N + g * SSD_N:D + SSD_G * SSD_N + (g + 1) * SSD_N].astype(BF16)
            G = _nt(Cb, Bb)
            STg = S[g * HG:(g + 1) * HG].reshape(HG * SSD_P, SSD_N)
            y_off = e_rep[:, gl] * _nt(Cb, STg.astype(BF16))
            upd = _tn(xdec[:, gl], Bb)
            for rr in range(HG):
                h = g * HG + rr
                hl = slice(h * SSD_P, (h + 1) * SSD_P)
                col = cs[:, h:h + 1]
                row = csT[h:h + 1, :]
                Ld = jnp.where(tril > 0, jnp.exp(jnp.minimum(col - row, 0.0)), 0.0)
                M = (G * Ld).astype(BF16)
                y_ref[:, hl] = _nn(M, xdt[:, hl].astype(BF16)) + y_off[:, rr * SSD_P:(rr + 1) * SSD_P]
                rows = slice(rr * SSD_P, (rr + 1) * SSD_P)
                S[h] = jnp.exp(cs_last[:, h:h + 1]) * STg[rows] + upd[rows]

    return pl.pallas_call(
        body, name="ssd_fwd", grid=(B, NC),
        in_specs=[pl.BlockSpec((CH, CONV_DIM), lambda b, c: (b * NC + c, 0)),
                  pl.BlockSpec((CH, CH), lambda b, c: (b * NC + c, 0)),
                  pl.BlockSpec((1, CH), lambda b, c: (0, 0))],
        out_specs=[pl.BlockSpec((CH, D), lambda b, c: (b * NC + c, 0)),
                   pl.BlockSpec((1, SSD_H, SSD_P, SSD_N), lambda b, c: (b * NC + c, 0, 0, 0))],
        out_shape=[jax.ShapeDtypeStruct((T, D), F32),
                   jax.ShapeDtypeStruct((B * NC, SSD_H, SSD_P, SSD_N), F32)],
        scratch_shapes=[pltpu.VMEM((SSD_H, SSD_P, SSD_N), F32)],
        compiler_params=_cp(("arbitrary", "arbitrary")),
    )(xbc, sm, alog)


def _ssd_bwd(xbc, sm, alog, states, dy, dxs_skip, *, B, NC):
    T = B * NC * CH

    def rix(b, c):
        return b * NC + (NC - 1 - c)

    def body(x_ref, sm_ref, alog_ref, st_ref, dy_ref, sk_ref, dx_ref, ddt_ref, dal_ref, dS):
        bidx = pl.program_id(0)
        cidx = pl.program_id(1)

        @pl.when(cidx == 0)
        def _():
            dS[...] = jnp.zeros_like(dS)

        A, dt, cs, csT, cs_last, tril, r, c = _ssd_common(sm_ref, alog_ref)
        lane = lax.broadcasted_iota(jnp.int32, (1, CH), 1)
        dt_rep, cs_rep, last_rep = _ssd_per_lane(dt, cs)
        xs = x_ref[:, 0:D]
        xdt = xs * dt_rep
        e_rep = jnp.exp(cs_rep)
        dec_rep = jnp.exp(last_rep - cs_rep)
        dye = dy_ref[...] * e_rep
        xdec = xdt * dec_rep
        HG = SSD_H // SSD_G
        DCcol = jnp.zeros((CH, CH), F32)
        DCrow = jnp.zeros((CH, CH), F32)
        dlast = jnp.zeros((1, CH), F32)
        t_off, t_dec, dx_state = [], [], []
        for g in range(SSD_G):
            ob = D + g * SSD_N
            oc = D + SSD_G * SSD_N + g * SSD_N
            gl = slice(g * HG * SSD_P, (g + 1) * HG * SSD_P)
            Bb = x_ref[:, ob:ob + SSD_N].astype(BF16)
            Cb = x_ref[:, oc:oc + SSD_N].astype(BF16)
            G = _nt(Cb, Bb)
            STg = st_ref[0, g * HG:(g + 1) * HG].reshape(HG * SSD_P, SSD_N)
            dSTg = dS[g * HG:(g + 1) * HG].reshape(HG * SSD_P, SSD_N)
            STb = STg.astype(BF16)
            dSTb = dSTg.astype(BF16)
            dyeb = dye[:, gl].astype(BF16)
            t_off.append(dye[:, gl] * _nt(Cb, STb))
            dCg = _nn(dyeb, STb)
            dS_in = _tn(dyeb, Cb)
            Z = _nt(Bb, dSTb)
            dx_state.append(dec_rep[:, gl] * Z)
            t_dec.append(xdec[:, gl] * Z)
            dBg = _nn(xdec[:, gl].astype(BF16), dSTb)
            dG = jnp.zeros((CH, CH), F32)
            for rr in range(HG):
                h = g * HG + rr
                hl = slice(h * SSD_P, (h + 1) * SSD_P)
                rows = slice(rr * SSD_P, (rr + 1) * SSD_P)
                col = cs[:, h:h + 1]
                row = csT[h:h + 1, :]
                Ld = jnp.where(tril > 0, jnp.exp(jnp.minimum(col - row, 0.0)), 0.0)
                Mf = G * Ld
                dyb = dy_ref[:, hl].astype(BF16)
                dx_ref[:, hl] = _tn(Mf.astype(BF16), dyb)
                dM = _nt(dyb, xdt[:, hl].astype(BF16))
                dG = dG + dM * Ld
                W = dM * Mf
                DCcol = DCcol + jnp.where(c == h, jnp.sum(W, axis=1, keepdims=True), 0.0)
                DCrow = DCrow - jnp.where(r == h, jnp.sum(W, axis=0, keepdims=True), 0.0)
                el = jnp.exp(cs_last[:, h:h + 1])
                dl = el * jnp.sum(jnp.sum(dSTg[rows] * STg[rows], axis=1, keepdims=True), axis=0, keepdims=True)
                dlast = dlast + jnp.where(lane == h, dl, 0.0)
                dS[h] = dS_in[rows] + el * dSTg[rows]
            dGb = dG.astype(BF16)
            dx_ref[:, ob:ob + SSD_N] = dBg + _tn(dGb, Cb)
            dx_ref[:, oc:oc + SSD_N] = dCg + _nn(dGb, Bb)
        dxdt = dx_ref[:, 0:D] + jnp.concatenate(dx_state, axis=1)
        dx_ref[:, 0:D] = dxdt * dt_rep + sk_ref[...]
        s_off, s_dec, DX = _ssd_per_head(jnp.concatenate(t_off, axis=1), jnp.concatenate(t_dec, axis=1), dxdt * xs)
        dlast = dlast + jnp.sum(s_dec, axis=0, keepdims=True)
        DC = DCcol + s_off - s_dec + DCrow.T + jnp.where(r == CH - 1, dlast, 0.0)
        triu = (r <= c).astype(F32)
        dadt = jnp.dot(triu, DC, precision=HI, preferred_element_type=F32)
        ddt_ref[...] = dadt * A + DX
        dal = jnp.sum(dadt * dt, axis=0, keepdims=True) * A

        @pl.when((bidx == 0) & (cidx == 0))
        def _():
            dal_ref[...] = dal

        @pl.when((bidx > 0) | (cidx > 0))
        def _():
            dal_ref[...] += dal

    return pl.pallas_call(
        body, name="ssd_bwd", grid=(B, NC),
        in_specs=[pl.BlockSpec((CH, CONV_DIM), lambda b, c: (rix(b, c), 0)),
                  pl.BlockSpec((CH, CH), lambda b, c: (rix(b, c), 0)),
                  pl.BlockSpec((1, CH), lambda b, c: (0, 0)),
                  pl.BlockSpec((1, SSD_H, SSD_P, SSD_N), lambda b, c: (rix(b, c), 0, 0, 0)),
                  pl.BlockSpec((CH, D), lambda b, c: (rix(b, c), 0)),
                  pl.BlockSpec((CH, D), lambda b, c: (rix(b, c), 0))],
        out_specs=[pl.BlockSpec((CH, CONV_DIM), lambda b, c: (rix(b, c), 0)),
                   pl.BlockSpec((CH, CH), lambda b, c: (rix(b, c), 0)),
                   pl.BlockSpec((1, CH), lambda b, c: (0, 0))],
        out_shape=[jax.ShapeDtypeStruct((T, CONV_DIM), F32),
                   jax.ShapeDtypeStruct((T, CH), F32),
                   jax.ShapeDtypeStruct((1, CH), F32)],
        scratch_shapes=[pltpu.VMEM((SSD_H, SSD_P, SSD_N), F32)],
        compiler_params=_cp(("arbitrary", "arbitrary")),
    )(xbc, sm, alog, states, dy, dxs_skip)


def _cumsum_seq(v, *, B, NC, reverse, name):
    T = B * NC * CH
    R = 3 * CH if (NC % 3 == 0 and NC > 3) else CH
    NR = NC * CH // R

    def ix(b, c):
        return b * NR + ((NR - 1 - c) if reverse else c)

    def body(v_ref, o_ref, carry):
        cidx = pl.program_id(1)

        @pl.when(cidx == 0)
        def _():
            carry[...] = jnp.zeros_like(carry)

        r = lax.broadcasted_iota(jnp.int32, (R, R), 0)
        c = lax.broadcasted_iota(jnp.int32, (R, R), 1)
        tri = ((r <= c) if reverse else (r >= c)).astype(F32)
        cs = jnp.dot(tri, v_ref[...], precision=HI, preferred_element_type=F32) + carry[...]
        o_ref[...] = cs
        edge = 0 if reverse else R - 1
        rows = lax.broadcasted_iota(jnp.int32, (R, CH), 0)
        carry[...] = jnp.sum(jnp.where(rows == edge, cs, 0.0), axis=0, keepdims=True)

    return pl.pallas_call(
        body, name=name, grid=(B, NR),
        in_specs=[pl.BlockSpec((R, CH), lambda b, c: (ix(b, c), 0))],
        out_specs=pl.BlockSpec((R, CH), lambda b, c: (ix(b, c), 0)),
        out_shape=jax.ShapeDtypeStruct((T, CH), F32),
        scratch_shapes=[pltpu.VMEM((1, CH), F32)],
        compiler_params=_cp(("arbitrary", "arbitrary")),
    )(v)


def _fox_tb(Lp):
    return 384 if (Lp % 384 == 0 and Lp > 384) else CH


def _fox_keybias(cum, *, B, Lp, pad):
    ck = cum.reshape(B, Lp, CH)[:, :, SSD_H:SSD_H + FOX_H].transpose(0, 2, 1)
    pos = lax.broadcasted_iota(jnp.int32, ck.shape, 2)
    return jnp.where(pos < pad, -NEG, ck).reshape(B * FOX_H, 1, Lp)


def _fox_tril(TB):
    r = lax.broadcasted_iota(jnp.int32, (TB, TB), 0)
    c = lax.broadcasted_iota(jnp.int32, (TB, TB), 1)
    return r >= c


def _fox_fwd(qkv, cumT, *, B, Lp):
    TB = _fox_tb(Lp)
    NQ = Lp // TB
    T = B * Lp
    scale = FOX_DH ** -0.5

    def body(q_ref, k_ref, v_ref, ct_ref, o_ref, lse_ref):
        i = pl.program_id(2)
        q = q_ref[...]

        def block(j, nb, carry, diag):
            m, l, acc = carry
            off = pl.multiple_of(j * TB, TB)
            k = k_ref[pl.ds(off, nb * TB), :]
            v = v_ref[pl.ds(off, nb * TB), :]
            s = _nt(q, k) * scale - ct_ref[0, :, pl.ds(off, nb * TB)]
            if diag:
                s = jnp.where(_fox_tril(TB), s, NEG)
            m_new = jnp.maximum(m, jnp.max(s, axis=1, keepdims=True))
            p = jnp.exp(s - m_new)
            alpha = jnp.exp(m - m_new)
            l = alpha * l + jnp.sum(p, axis=1, keepdims=True)
            acc = alpha * acc + _nn(p.astype(BF16), v)
            return m_new, l, acc

        init = (jnp.full((TB, 1), NEG, F32), jnp.zeros((TB, 1), F32), jnp.zeros((TB, FOX_DH), F32))
        carry = lax.fori_loop(0, i // 2, lambda t, c: block(2 * t, 2, c, False), init)
        carry = lax.fori_loop(2 * (i // 2), i, lambda j, c: block(j, 1, c, False), carry)
        m, l, acc = block(i, 1, carry, True)
        o_ref[...] = (acc / l).astype(o_ref.dtype)
        lse_ref[0, 0] = m + jnp.log(l)

    return pl.pallas_call(
        body, name="fox_fwd", grid=(B, FOX_H, NQ),
        in_specs=[pl.BlockSpec((TB, FOX_DH), lambda b, h, i: (b * NQ + i, h)),
                  pl.BlockSpec((Lp, FOX_DH), lambda b, h, i: (b, FOX_H + h)),
                  pl.BlockSpec((Lp, FOX_DH), lambda b, h, i: (b, 2 * FOX_H + h)),
                  pl.BlockSpec((1, 1, Lp), lambda b, h, i: (b * FOX_H + h, 0, 0))],
        out_specs=[pl.BlockSpec((TB, FOX_DH), lambda b, h, i: (b * NQ + i, h)),
                   pl.BlockSpec((1, 1, TB, 1), lambda b, h, i: (b, h, i, 0))],
        out_shape=[jax.ShapeDtypeStruct((T, D), BF16),
                   jax.ShapeDtypeStruct((B, FOX_H, Lp, 1), F32)],
        compiler_params=_cp(("arbitrary", "arbitrary", "arbitrary")),
    )(qkv, qkv, qkv, cumT)


def _fox_bwd(qkv, dy, o, lse, cumT, *, B, Lp):
    TB = _fox_tb(Lp)
    NQ = Lp // TB
    T = B * Lp
    scale = FOX_DH ** -0.5

    def body(q_ref, k_ref, v_ref, dy_ref, o_ref, lse_ref, ct_ref, dq_ref, dk_ref, dv_ref, dck_ref, dcq_ref, dl_s):
        j = pl.program_id(2)
        k = k_ref[...]
        v = v_ref[...]
        ck = ct_ref[0]

        @pl.when(j == 0)
        def _():
            dq_ref[...] = jnp.zeros_like(dq_ref)
            dcq_ref[...] = jnp.zeros_like(dcq_ref)
            for i in range(NQ):
                sl = slice(i * TB, (i + 1) * TB)
                dl_s[sl, :] = jnp.sum(dy_ref[sl, :].astype(F32) * o_ref[sl, :].astype(F32), axis=1, keepdims=True)

        def block(i, nb, carry, diag):
            dk, dv, dck = carry
            off = pl.multiple_of(i * TB, TB)
            rows = pl.ds(off, nb * TB)
            q = q_ref[rows, :]
            dob = dy_ref[rows, :].astype(BF16)
            e = _nt(q, k) * scale - ck - lse_ref[0, 0, rows, :]
            if diag:
                e = jnp.where(_fox_tril(TB), e, NEG)
            p = jnp.exp(e)
            dv = dv + _tn(p.astype(BF16), dob)
            ds = p * (_nt(dob, v) - dl_s[rows, :])
            dsb = ds.astype(BF16)
            dk = dk + _tn(dsb, q)
            dq_ref[rows, :] += _nn(dsb, k) * scale
            dcq_ref[0, 0, rows, :] += jnp.sum(ds, axis=1, keepdims=True)
            dck = dck - jnp.sum(ds, axis=0, keepdims=True)
            return dk, dv, dck

        z = jnp.zeros((TB, FOX_DH), F32)
        carry = block(j, 1, (z, z, jnp.zeros((1, TB), F32)), True)
        npair = (NQ - 1 - j) // 2
        carry = lax.fori_loop(0, npair, lambda t, c: block(j + 1 + 2 * t, 2, c, False), carry)
        dk, dv, dck = lax.fori_loop(j + 1 + 2 * npair, NQ, lambda i, c: block(i, 1, c, False), carry)
        dk_ref[...] = (dk * scale).astype(dk_ref.dtype)
        dv_ref[...] = dv.astype(dv_ref.dtype)
        dck_ref[0] = dck

    head = lambda b, h, j: (b, h)
    return pl.pallas_call(
        body, name="fox_bwd", grid=(B, FOX_H, NQ),
        in_specs=[pl.BlockSpec((Lp, FOX_DH), head),
                  pl.BlockSpec((TB, FOX_DH), lambda b, h, j: (b * NQ + j, FOX_H + h)),
                  pl.BlockSpec((TB, FOX_DH), lambda b, h, j: (b * NQ + j, 2 * FOX_H + h)),
                  pl.BlockSpec((Lp, FOX_DH), head),
                  pl.BlockSpec((Lp, FOX_DH), head),
                  pl.BlockSpec((1, 1, Lp, 1), lambda b, h, j: (b, h, 0, 0)),
                  pl.BlockSpec((1, 1, TB), lambda b, h, j: (b * FOX_H + h, 0, j))],
        out_specs=[pl.BlockSpec((Lp, FOX_DH), head),
                   pl.BlockSpec((TB, FOX_DH), lambda b, h, j: (b * NQ + j, h)),
                   pl.BlockSpec((TB, FOX_DH), lambda b, h, j: (b * NQ + j, h)),
                   pl.BlockSpec((1, 1, TB), lambda b, h, j: (b * FOX_H + h, 0, j)),
                   pl.BlockSpec((1, 1, Lp, 1), lambda b, h, j: (b, h, 0, 0))],
        out_shape=[jax.ShapeDtypeStruct((T, D), F32),
                   jax.ShapeDtypeStruct((T, D), BF16),
                   jax.ShapeDtypeStruct((T, D), BF16),
                   jax.ShapeDtypeStruct((B * FOX_H, 1, Lp), F32),
                   jax.ShapeDtypeStruct((B, FOX_H, Lp, 1), F32)],
        scratch_shapes=[pltpu.VMEM((Lp, 1), F32)],
        compiler_params=_cp(("arbitrary", "arbitrary", "arbitrary")),
    )(qkv, qkv, qkv, dy, o, lse, cumT)


S5_TILE = 8


def _s5_pows(lam_ref, pw, tab, reverse):
    lr = lam_ref[0, :, 0:S5_W]
    li = lam_ref[0, :, S5_W:2 * S5_W]
    if reverse:
        li = -li
    ar, ai = lr, li
    sub = lax.broadcasted_iota(jnp.int32, (S5_TILE, 1), 0)
    for k, s in enumerate((1, 2, 4)):
        keep = (sub < S5_TILE - s) if reverse else (sub >= s)
        pw[k * S5_TILE:(k + 1) * S5_TILE, 0:S5_W] = jnp.where(keep, ar, 0.0)
        pw[k * S5_TILE:(k + 1) * S5_TILE, S5_W:2 * S5_W] = jnp.where(keep, ai, 0.0)
        ar, ai = ar * ar - ai * ai, 2.0 * ar * ai
    ar, ai = lr, li
    for r in range(S5_TILE):
        row = (S5_TILE - 1 - r) if reverse else r
        tab[row:row + 1, 0:S5_W] = ar
        tab[row:row + 1, S5_W:2 * S5_W] = ai
        ar, ai = ar * lr - ai * li, ar * li + ai * lr


def _s5_scan(hs, pw, tab, carry, reverse):
    n = hs.shape[0]
    tiles = list(range(n // S5_TILE))
    if reverse:
        tiles.reverse()
    for t in tiles:
        lo = t * S5_TILE
        vr = hs[lo:lo + S5_TILE, 0:S5_W]
        vi = hs[lo:lo + S5_TILE, S5_W:2 * S5_W]
        for k, s in enumerate((1, 2, 4)):
            sh = (S5_TILE - s) if reverse else s
            sr = pltpu.roll(vr, sh, 0)
            si = pltpu.roll(vi, sh, 0)
            ar = pw[k * S5_TILE:(k + 1) * S5_TILE, 0:S5_W]
            ai = pw[k * S5_TILE:(k + 1) * S5_TILE, S5_W:2 * S5_W]
            vr, vi = vr + ar * sr - ai * si, vi + ar * si + ai * sr
        hs[lo:lo + S5_TILE, 0:S5_W] = vr
        hs[lo:lo + S5_TILE, S5_W:2 * S5_W] = vi
    e8 = 0 if reverse else S5_TILE - 1
    l8r = tab[e8:e8 + 1, 0:S5_W]
    l8i = tab[e8:e8 + 1, S5_W:2 * S5_W]
    cr = carry[:, 0:S5_W]
    ci = carry[:, S5_W:2 * S5_W]
    states = []
    for t in tiles:
        states.append((cr, ci))
        edge = t * S5_TILE + e8
        er = hs[edge:edge + 1, 0:S5_W]
        ei = hs[edge:edge + 1, S5_W:2 * S5_W]
        cr, ci = er + l8r * cr - l8i * ci, ei + l8r * ci + l8i * cr
    carry[:, 0:S5_W] = cr
    carry[:, S5_W:2 * S5_W] = ci
    tr = tab[:, 0:S5_W]
    ti = tab[:, S5_W:2 * S5_W]
    for t, (cr, ci) in zip(tiles, states):
        lo = t * S5_TILE
        hs[lo:lo + S5_TILE, 0:S5_W] += tr * cr - ti * ci
        hs[lo:lo + S5_TILE, S5_W:2 * S5_W] += tr * ci + ti * cr


def _s5_rows(NC):
    return 3 * CH if (NC % 3 == 0 and NC > 3) else CH


def _s5_fwd(u, Bsg, Csg, lam, *, B, NC):
    T = B * NC * CH
    R = _s5_rows(NC)
    NR = NC * CH // R

    def body(u_ref, b_ref, c_ref, lam_ref, y_ref, h_ref, pw, tab, hs, carry):
        cidx = pl.program_id(2)

        @pl.when(cidx == 0)
        def _():
            _s5_pows(lam_ref, pw, tab, False)
            carry[...] = jnp.zeros_like(carry)

        hs[...] = _nn(u_ref[...].astype(BF16), b_ref[0])
        _s5_scan(hs, pw, tab, carry, False)
        hb = hs[...].astype(BF16)
        h_ref[...] = hb
        y_ref[...] = _nn(hb, c_ref[0])

    return pl.pallas_call(
        body, name="s5_fwd", grid=(B, S5_G // S5_SG, NR),
        in_specs=[pl.BlockSpec((R, CH), lambda b, s, c: (b * NR + c, s)),
                  pl.BlockSpec((1, CH, 2 * S5_W), lambda b, s, c: (s, 0, 0)),
                  pl.BlockSpec((1, 2 * S5_W, CH), lambda b, s, c: (s, 0, 0)),
                  pl.BlockSpec((1, 1, 2 * S5_W), lambda b, s, c: (s, 0, 0))],
        out_specs=[pl.BlockSpec((R, CH), lambda b, s, c: (b * NR + c, s)),
                   pl.BlockSpec((R, 2 * S5_W), lambda b, s, c: (b * NR + c, s))],
        out_shape=[jax.ShapeDtypeStruct((T, D), F32),
                   jax.ShapeDtypeStruct((T, (S5_G // S5_SG) * 2 * S5_W), BF16)],
        scratch_shapes=[pltpu.VMEM((3 * S5_TILE, 2 * S5_W), F32), pltpu.VMEM((S5_TILE, 2 * S5_W), F32),
                        pltpu.VMEM((R, 2 * S5_W), F32), pltpu.VMEM((1, 2 * S5_W), F32)],
        compiler_params=_cp(("arbitrary", "arbitrary", "arbitrary")),
    )(u, Bsg, Csg, lam)


def _s5_bwd(u, hst, dy, du_skip, Bsg, Csg, lam, *, B, NC):
    T = B * NC * CH
    NS = S5_G // S5_SG
    R = _s5_rows(NC)
    NR = NC * CH // R
    hb16 = R // 16

    def rix(b, c):
        return b * NR + (NR - 1 - c)

    def body(u_ref, h_ref, hp_ref, dy_ref, sk_ref, b_ref, c_ref, lam_ref,
             du_ref, db_ref, dc_ref, dl_ref, pw, tab, gs, carry):
        bidx = pl.program_id(1)
        cidx = pl.program_id(2)
        first = (bidx == 0) & (cidx == 0)

        @pl.when(cidx == 0)
        def _():
            _s5_pows(lam_ref, pw, tab, True)
            carry[...] = jnp.zeros_like(carry)

        dyb = dy_ref[...].astype(BF16)
        gs[...] = _nt(dyb, c_ref[0])
        _s5_scan(gs, pw, tab, carry, True)
        gr = gs[:, 0:S5_W]
        gi = gs[:, S5_W:]
        gb = gs[...].astype(BF16)
        du_ref[...] = (_nt(gb, b_ref[0]) + sk_ref[...]).astype(du_ref.dtype)
        ub = u_ref[...].astype(BF16)
        hcur = h_ref[...]
        dB = _tn(ub, gb)
        dC = _tn(dyb, hcur)
        hf = hcur.astype(F32)
        row = lax.broadcasted_iota(jnp.int32, (R, 1), 0)
        prev_last = jnp.where(cidx < NR - 1, hp_ref[15:16, :].astype(F32), 0.0)
        hprev = jnp.where(row == 0, prev_last, pltpu.roll(hf, 1, 0))
        pr = hprev[:, 0:S5_W]
        pi = hprev[:, S5_W:]
        da = _colsum(gr * pr + gi * pi)
        dbb = _colsum(gi * pr - gr * pi)
        dl = jnp.concatenate([da, dbb], axis=1)

        @pl.when(first)
        def _():
            db_ref[0] = dB
            dc_ref[0] = dC
            dl_ref[0] = dl

        @pl.when(jnp.logical_not(first))
        def _():
            db_ref[0] += dB
            dc_ref[0] += dC
            dl_ref[0] += dl

    return pl.pallas_call(
        body, name="s5_bwd", grid=(NS, B, NR),
        in_specs=[pl.BlockSpec((R, CH), lambda s, b, c: (rix(b, c), s)),
                  pl.BlockSpec((R, 2 * S5_W), lambda s, b, c: (rix(b, c), s)),
                  pl.BlockSpec((16, 2 * S5_W), lambda s, b, c: (jnp.maximum(rix(b, c) * hb16 - 1, 0), s)),
                  pl.BlockSpec((R, CH), lambda s, b, c: (rix(b, c), s)),
                  pl.BlockSpec((R, CH), lambda s, b, c: (rix(b, c), s)),
                  pl.BlockSpec((1, CH, 2 * S5_W), lambda s, b, c: (s, 0, 0)),
                  pl.BlockSpec((1, 2 * S5_W, CH), lambda s, b, c: (s, 0, 0)),
                  pl.BlockSpec((1, 1, 2 * S5_W), lambda s, b, c: (s, 0, 0))],
        out_specs=[pl.BlockSpec((R, CH), lambda s, b, c: (rix(b, c), s)),
                   pl.BlockSpec((1, CH, 2 * S5_W), lambda s, b, c: (s, 0, 0)),
                   pl.BlockSpec((1, CH, 2 * S5_W), lambda s, b, c: (s, 0, 0)),
                   pl.BlockSpec((1, 1, 2 * S5_W), lambda s, b, c: (s, 0, 0))],
        out_shape=[jax.ShapeDtypeStruct((T, D), BF16),
                   jax.ShapeDtypeStruct((NS, CH, 2 * S5_W), F32),
                   jax.ShapeDtypeStruct((NS, CH, 2 * S5_W), F32),
                   jax.ShapeDtypeStruct((NS, 1, 2 * S5_W), F32)],
        scratch_shapes=[pltpu.VMEM((3 * S5_TILE, 2 * S5_W), F32), pltpu.VMEM((S5_TILE, 2 * S5_W), F32),
                        pltpu.VMEM((R, 2 * S5_W), F32), pltpu.VMEM((1, 2 * S5_W), F32)],
        compiler_params=_cp(("arbitrary", "arbitrary", "arbitrary")),
    )(u, hst, hst, dy, du_skip, Bsg, Csg, lam)


def _s5_param_fn(lre, lim, lstep, bre, bim):
    step = jnp.exp(lstep)
    zr = lre * step
    zi = lim * step
    e = jnp.exp(zr)
    a = e * jnp.cos(zi)
    b = e * jnp.sin(zi)
    den = lre * lre + lim * lim
    qr = ((a - 1.0) * lre + b * lim) / den
    qi = (b * lre - (a - 1.0) * lim) / den
    return a, b, qr[None] * bre - qi[None] * bim, qr[None] * bim + qi[None] * bre


_S5_ROWS = S5_G * S5_P // CH


def _s5_tile(v):
    return v.reshape(_S5_ROWS, CH)


def _s5_tile_b(v):
    return v.reshape(S5_G * S5_P, S5_C).T.reshape(S5_C, _S5_ROWS, CH)


def _s5_untile_b(v):
    return v.reshape(S5_C, S5_G * S5_P).T.reshape(S5_G, S5_P, S5_C)


def _s5_params(lre, lim, lstep, bre, bim):
    def body(a_ref, b_ref, c_ref, d_ref, e_ref, o1, o2, o3, o4):
        outs = _s5_param_fn(a_ref[...], b_ref[...], c_ref[...], d_ref[...], e_ref[...])
        for o, v in zip((o1, o2, o3, o4), outs):
            o[...] = v

    shp = [jax.ShapeDtypeStruct(lre.shape, F32)] * 2 + [jax.ShapeDtypeStruct(bre.shape, F32)] * 2
    return pl.pallas_call(body, name="s5_params", out_shape=shp, compiler_params=_cp())(lre, lim, lstep, bre, bim)


def _s5_params_bwd(lre, lim, lstep, bre, bim, da, db, dbr, dbi):
    def body(a_ref, b_ref, c_ref, d_ref, e_ref, g1, g2, g3, g4, o1, o2, o3, o4, o5):
        _, vjp = jax.vjp(_s5_param_fn, a_ref[...], b_ref[...], c_ref[...], d_ref[...], e_ref[...])
        outs = vjp((g1[...], g2[...], g3[...], g4[...]))
        for o, v in zip((o1, o2, o3, o4, o5), outs):
            o[...] = v

    shp = [jax.ShapeDtypeStruct(lre.shape, F32)] * 3 + [jax.ShapeDtypeStruct(bre.shape, F32)] * 2
    return pl.pallas_call(body, name="s5_params_bwd", out_shape=shp, compiler_params=_cp())(
        lre, lim, lstep, bre, bim, da, db, dbr, dbi)


def _s5_blockdiag(br, bi, cre, cim):
    NS = S5_G // S5_SG
    eye = jnp.eye(S5_SG, dtype=F32)

    def bmat(v):
        v = v.reshape(NS, S5_SG, S5_P, S5_C)
        m = jnp.einsum("sgpc,gh->sgchp", v, eye)
        return m.reshape(NS, S5_SG * S5_C, S5_SG * S5_P)

    def cmat(v):
        v = v.reshape(NS, S5_SG, S5_C, S5_P)
        m = jnp.einsum("sgcp,gh->sgphc", v, eye)
        return m.reshape(NS, S5_SG * S5_P, S5_SG * S5_C)

    Bsg = jnp.concatenate([bmat(br), bmat(bi)], axis=2).astype(BF16)
    Csg = jnp.concatenate([cmat(cre), cmat(-cim)], axis=1).astype(BF16)
    return Bsg, Csg


def _s5_unblock(dBsg, dCsgT):
    NS = S5_G // S5_SG

    def diag(m):
        m = m.reshape(NS, S5_SG, S5_C, S5_SG, S5_P)
        return jnp.stack([m[:, g, :, g, :] for g in range(S5_SG)], axis=1)

    def ub(m):
        return diag(m).transpose(0, 1, 3, 2).reshape(S5_G, S5_P, S5_C)

    def uc(m):
        return diag(m).reshape(S5_G, S5_C, S5_P)

    dbr = ub(dBsg[:, :, 0:S5_W])
    dbi = ub(dBsg[:, :, S5_W:])
    dcr = uc(dCsgT[:, :, 0:S5_W])
    dci = -uc(dCsgT[:, :, S5_W:])
    return dbr, dbi, dcr, dci


def _loss_head(x, nf, target, *, B, NC, S):
    T = B * NC * CH
    nts = S // CH

    def f(xv, w, t):
        y = _rms(xv, w)
        return 0.5 * _colsum(jnp.mean(jnp.square(y - t), axis=-1, keepdims=True))

    def body(x_ref, w_ref, t_ref, dx_ref, ls_ref, dw_ref):
        i = pl.program_id(0)
        on = (i % NC) > 0
        t = t_ref[...]
        l, vjp = jax.vjp(lambda a, b: f(a, b, t), x_ref[...], w_ref[...])
        dx, dw = vjp(jnp.ones((1, 1), F32))
        g = jnp.where(on, 1.0, 0.0)
        dx_ref[...] = dx * g
        lv = jnp.zeros((1, CH), F32) + l * g

        @pl.when(i == 0)
        def _():
            ls_ref[...] = lv
            dw_ref[...] = dw * g

        @pl.when(i > 0)
        def _():
            ls_ref[...] += lv
            dw_ref[...] += dw * g

    def tix(i):
        return ((i // NC) * nts + jnp.maximum(i % NC - 1, 0), 0)

    return pl.pallas_call(
        body, name="loss_head", grid=(B * NC,),
        in_specs=[pl.BlockSpec((CH, D), lambda i: (i, 0)),
                  pl.BlockSpec((1, D), lambda i: (0, 0)),
                  pl.BlockSpec((CH, D), tix)],
        out_specs=[pl.BlockSpec((CH, D), lambda i: (i, 0)),
                   pl.BlockSpec((1, CH), lambda i: (0, 0)),
                   pl.BlockSpec((1, D), lambda i: (0, 0))],
        out_shape=[jax.ShapeDtypeStruct((T, D), F32),
                   jax.ShapeDtypeStruct((1, CH), F32),
                   jax.ShapeDtypeStruct((1, D), F32)],
        compiler_params=_cp(("arbitrary",)),
    )(x, nf, target)


def _ew(fn, ins, n_out, out_dtypes, *, name, tile=None):
    R, C = ins[0].shape
    tile = tile or _pick(R, (512, 256, 128, 64, 32, 16, 8, 1))
    if tile % 8 != 0:
        tile = R

    def body(*refs):
        outs = fn(*[r[...] for r in refs[:len(ins)]])
        if not isinstance(outs, (tuple, list)):
            outs = (outs,)
        for r, v in zip(refs[len(ins):], outs):
            r[...] = v.astype(r.dtype)

    spec = pl.BlockSpec((tile, C), lambda i: (i, 0))
    res = pl.pallas_call(
        body, name=name, grid=(R // tile,), in_specs=[spec] * len(ins), out_specs=[spec] * n_out,
        out_shape=[jax.ShapeDtypeStruct((R, C), dt) for dt in out_dtypes],
        compiler_params=_cp(("parallel",)),
    )(*ins)
    return res


def _adam_fn(w, g, m, v):
    m = ADAM_B1 * m + (1.0 - ADAM_B1) * g
    v = ADAM_B2 * v + (1.0 - ADAM_B2) * jnp.square(g)
    m_hat = m / (1.0 - ADAM_B1 ** ADAM_STEP)
    v_hat = v / (1.0 - ADAM_B2 ** ADAM_STEP)
    delta = -ADAM_LR * (m_hat / (jnp.sqrt(v_hat) + ADAM_EPS) + ADAM_WD * w)
    return delta, m, v


def _adam(w, g, m, v, name):
    shp = w.shape
    C = shp[-1]
    f = lambda a: a.reshape(-1, C)
    d, nm, nv = _ew(_adam_fn, [f(w), f(g), f(m), f(v)], 3, [F32] * 3, name=name)
    return d.reshape(shp), nm.reshape(shp), nv.reshape(shp)


def _me():
    return lax.axis_index("x"), lax.axis_index("y"), lax.axis_index("c")


def _all_gather(v, name):
    def body(x_ref, out_ref, send_sems, recv_sems, local_sem):
        x, y, c = _me()
        me, sibling = (x, y, c), (x, y, 1 - c)
        chips = [(1 - x, y), (x, 1 - y), (1 - x, 1 - y)]

        def slot(px, py, pc):
            return out_ref.at[4 * px + 2 * py + pc]

        def copy(k, block, to, src=None):
            return pltpu.make_async_remote_copy(
                src_ref=slot(*block) if src is None else src, dst_ref=slot(*block),
                send_sem=send_sems.at[k], recv_sem=recv_sems.at[k], device_id=to, device_id_type=MESH)

        mine = pltpu.make_async_copy(x_ref, slot(*me), local_sem)
        mine.start()
        first = [copy(0, me, sibling, src=x_ref)]
        first += [copy(1 + j, me, (*chip, c), src=x_ref) for j, chip in enumerate(chips)]
        for cp in first:
            cp.start()
        passed = [copy(4 + j, (*chip, c), sibling) for j, chip in enumerate(chips)]
        for j, chip in enumerate(chips):
            copy(1 + j, (*chip, c), me).wait_recv()
            passed[j].start()
        copy(0, sibling, me).wait_recv()
        for j, chip in enumerate(chips):
            copy(4 + j, (*chip, 1 - c), me).wait_recv()
        for cp in first + passed:
            cp.wait_send()
        mine.wait()

    return pl.pallas_call(
        body, name=name, out_shape=jax.ShapeDtypeStruct((N_DEV,) + v.shape, v.dtype),
        in_specs=[pl.BlockSpec(memory_space=pl.ANY)], out_specs=pl.BlockSpec(memory_space=pl.ANY),
        scratch_shapes=[pltpu.SemaphoreType.DMA((7,)), pltpu.SemaphoreType.DMA((7,)), pltpu.SemaphoreType.DMA],
    )(v)


def _swap_core(g, name):
    def body(g_ref, out_ref, send_sems, recv_sems):
        x, y, c = _me()
        cps = [pltpu.make_async_remote_copy(
            src_ref=g_ref.at[q, 1 - c], dst_ref=out_ref.at[q], send_sem=send_sems.at[q], recv_sem=recv_sems.at[q],
            device_id=(x, y, 1 - c), device_id_type=MESH) for q in range(4)]
        for cp in cps:
            cp.start()
        for cp in cps:
            cp.wait()

    return pl.pallas_call(
        body, name=name, out_shape=jax.ShapeDtypeStruct((4,) + g.shape[2:], g.dtype),
        in_specs=[pl.BlockSpec(memory_space=pl.ANY)], out_specs=pl.BlockSpec(memory_space=pl.ANY),
        scratch_shapes=[pltpu.SemaphoreType.DMA((4,)), pltpu.SemaphoreType.DMA((4,))],
    )(g)


def _swap_chips(hb, name):
    flips = [(1, 0), (0, 1), (1, 1)]

    def body(h_ref, out_ref, send_sems, recv_sems):
        x, y, c = _me()
        cps = []
        for j, (fx, fy) in enumerate(flips):
            px = x + fx - 2 * x * fx
            py = y + fy - 2 * y * fy
            cps.append(pltpu.make_async_remote_copy(
                src_ref=h_ref.at[2 * px + py], dst_ref=out_ref.at[j], send_sem=send_sems.at[j],
                recv_sem=recv_sems.at[j], device_id=(px, py, c), device_id_type=MESH))
        for cp in cps:
            cp.start()
        for cp in cps:
            cp.wait()

    return pl.pallas_call(
        body, name=name, out_shape=jax.ShapeDtypeStruct((3,) + hb.shape[1:], hb.dtype),
        in_specs=[pl.BlockSpec(memory_space=pl.ANY)], out_specs=pl.BlockSpec(memory_space=pl.ANY),
        scratch_shapes=[pltpu.SemaphoreType.DMA((3,)), pltpu.SemaphoreType.DMA((3,))],
    )(hb)


def _reduce_scatter(g8, tag):
    shard = g8.shape[1:]
    C = shard[-1]
    x, y, c = _me()
    g4 = g8.reshape((4, 2) + shard)
    got = _swap_core(g4, "rs_core_" + tag)
    mine = lax.dynamic_index_in_dim(g4, c, axis=1, keepdims=False)
    h, hb = _ew(lambda a, b: (a + b, a + b), [mine.reshape(-1, C), got.reshape(-1, C)], 2, [F32, BF16],
                name="rs_pair_sum_" + tag)
    own = lax.dynamic_index_in_dim(h.reshape((4,) + shard), 2 * x + y, axis=0, keepdims=False)
    got3 = _swap_chips(hb.reshape((4,) + shard), "rs_chips_" + tag)
    out, = _ew(lambda a, b, c_, d: a + b.astype(F32) + c_.astype(F32) + d.astype(F32),
               [own.reshape(-1, C)] + [got3[j].reshape(-1, C) for j in range(3)], 1, [F32],
               name="rs_chip_sum_" + tag)
    return out.reshape(shard)


def _sum8(a):
    out, = _ew(lambda *v: functools.reduce(lambda p, q: p + q, v), [a[k] for k in range(N_DEV)], 1, [F32],
               name="sum8")
    return out


def _pad_rows(flat, cols, mult):
    n = flat.shape[0]
    per = cols * mult
    tot = ((n + per - 1) // per) * per
    return jnp.pad(flat, (0, tot - n)).reshape(-1, cols)


class _Packer:
    def __init__(self, shapes, mult):
        self.shapes = shapes
        self.sizes = [int(np.prod(s)) for s in shapes]
        self.rows = [8 * ((n + 8 * D - 1) // (8 * D)) for n in self.sizes]
        tot = sum(self.rows)
        self.tail = (-tot) % mult

    def pack(self, arrs, dtype):
        parts = [_pad_rows(a.reshape(-1).astype(dtype), D, 8) for a in arrs]
        if self.tail:
            parts.append(jnp.zeros((self.tail, D), dtype))
        return jnp.concatenate(parts, axis=0)

    def unpack(self, buf):
        out, o = [], 0
        for s, n, r in zip(self.shapes, self.sizes, self.rows):
            out.append(buf[o:o + r].reshape(-1)[:n].reshape(s))
            o += r
        return out


def _w_cat(w_in_l):
    sm = jnp.concatenate([w_in_l[:, O_DT:O_DT + SSD_H], w_in_l[:, O_F:O_F + FOX_H],
                          jnp.zeros((D, CH - SSD_H - FOX_H), w_in_l.dtype)], axis=1)
    return jnp.concatenate([w_in_l[:, O_Z:O_XBC], w_in_l[:, O_XBC:O_DT], w_in_l[:, O_QKV:O_F],
                            w_in_l[:, O_U:O_G], w_in_l[:, O_G:D_IN], sm], axis=1)


def _w_uncat(g):
    return jnp.concatenate([g[:, OFF_Z:OFF_XBC], g[:, OFF_XBC:OFF_QKV], g[:, OFF_SM:OFF_SM + SSD_H],
                            g[:, OFF_QKV:OFF_U], g[:, OFF_SM + SSD_H:OFF_SM + SSD_H + FOX_H],
                            g[:, OFF_U:OFF_G], g[:, OFF_G:OFF_SM]], axis=1)


def _layer_fwd(x, p, geom, dims):
    B, NC, Lp, pad = dims
    T = geom[0]
    rm = functools.partial(_rowmap, geom=geom)
    sv = {}
    xn1, = rm(_f_norm, [x], [p["norm1"]], [(D, BF16)], [], tile=384 if Lp % 384 == 0 else CH, name="norm1")
    Wc = p["w_cat"]
    pz = _mm(xn1, Wc, "nn", BF16, n=D, b_off=OFF_Z, name="in_z")
    pxbc = _mm(xn1, Wc, "nn", F32, n=CONV_DIM, b_off=OFF_XBC, name="in_xbc")
    qkv = _mm(xn1, Wc, "nn", BF16, n=3 * D, b_off=OFF_QKV, name="in_qkv")
    pu = _mm(xn1, Wc, "nn", BF16, n=D, b_off=OFF_U, name="in_u")
    pg = _mm(xn1, Wc, "nn", BF16, n=3 * D, b_off=OFF_G, name="in_g")
    psm = _mm(xn1, Wc, "nn", F32, n=CH, b_off=OFF_SM, name="in_sm")
    t_r = 384 if Lp % 384 == 0 else CH
    sm, = rm(_smallact, [psm], [p["smallbias"]], [(CH, F32)], [], tile=t_r, name="smallact")
    xbc = _conv_fwd(pxbc, p["conv_w"], p["conv_b"], geom=geom, tile=t_r)
    y_ssd, states = _ssd_fwd(xbc, sm, p["a_log"], B=B, NC=NC)
    y_a, = rm(_ssd_post, [y_ssd, (xbc, D, 0), pz], [p["d_rep"], p["ssd_norm"]], [(D, BF16)], [], tile=t_r,
              name="ssd_post")
    cum = _cumsum_seq(sm, B=B, NC=NC, reverse=False, name="fox_cum")
    cumT = _fox_keybias(cum, B=B, Lp=Lp, pad=pad)
    y_b, lse = _fox_fwd(qkv, cumT, B=B, Lp=Lp)
    y_ssm, hst = _s5_fwd(pu, p["Bsg"], p["Csg"], p["lam"], B=B, NC=NC)
    y1, = rm(_s5_pre, [y_ssm, pu], [p["s5_d"]], [(D, BF16)], [], tile=t_r, name="s5_pre")
    tg = _mm(y1, p["w_glu"], "nn", BF16, name="s5_glu_mm")
    y_c, = rm(_s5_glu, [y_ssm, pu, tg], [p["s5_d"]], [(D, BF16)], [], tile=t_r, name="s5_glu")
    br = [_mm(yy, p["w_branch"][n], "nn", BF16, name=f"branch{n}") for n, yy in enumerate((y_a, y_b, y_c))]
    mix, = rm(_merge, [(pg, D, 0), (pg, D, 1), (pg, D, 2)] + br, [], [(D, BF16)], [], tile=t_r, name="merge")
    x_mid = _mm(mix, p["w_out"], "nn", F32, res=x, name="out_proj")
    xn2, = rm(_f_norm, [x_mid], [p["norm2"]], [(D, BF16)], [], tile=t_r, name="norm2")
    hff = _mm(xn2, p["w_ffn_in"], "nn", BF16, name="ffn_in")
    act, = rm(_swiglu, [(hff, DFF, 0), (hff, DFF, 1)], [], [(DFF, BF16)], [], tile=t_r, name="swiglu")
    x_out = _mm(act, p["w_ffn_out"], "nn", F32, res=x_mid, name="ffn_out")
    sv.update(x=x, xn1=xn1, pz=pz, pxbc=pxbc, qkv=qkv, pu=pu, pg=pg, psm=psm, sm=sm, xbc=xbc, y_ssd=y_ssd,
              states=states, y_a=y_a, cum=cum, cumT=cumT, y_b=y_b, lse=lse, y_ssm=y_ssm, hst=hst, y1=y1, tg=tg,
              y_c=y_c, br=br, mix=mix, x_mid=x_mid, xn2=xn2, hff=hff, act=act)
    return x_out, sv


def _layer_bwd(dx_out, p, sv, geom, dims):
    B, NC, Lp, pad = dims
    T = geom[0]
    rm = functools.partial(_rowmap, geom=geom)
    t_r = 384 if Lp % 384 == 0 else CH
    g = {}
    dact = _mm(dx_out, p["w_ffn_out"], "nt", BF16, name="ffn_out_dx")
    g["w_ffn_out"] = _mm(sv["act"], dx_out, "tn", F32, name="ffn_out_dw")
    dhff, = rm(_b_swiglu, [(sv["hff"], DFF, 0), (sv["hff"], DFF, 1), dact], [], [(2 * DFF, BF16)], [], tile=CH,
               name="swiglu_bwd")
    dxn2 = _mm(dhff, p["w_ffn_in"], "nt", F32, name="ffn_in_dx")
    g["w_ffn_in"] = _mm(sv["xn2"], dhff, "tn", F32, name="ffn_in_dw")
    dx_mid, g["norm2"] = rm(_b_norm, [sv["x_mid"], dxn2, dx_out], [p["norm2"]], [(D, F32)], [(1, D)], tile=t_r,
                            name="norm2_bwd")
    dmix = _mm(dx_mid, p["w_out"], "nt", BF16, name="out_proj_dx")
    g["w_out"] = _mm(sv["mix"], dx_mid, "tn", F32, name="out_proj_dw")
    pg = sv["pg"]
    dpg, db0, db1, db2 = rm(_b_merge, [(pg, D, 0), (pg, D, 1), (pg, D, 2)] + sv["br"] + [dmix], [],
                            [(3 * D, BF16), (D, BF16), (D, BF16), (D, BF16)], [], tile=t_r, name="merge_bwd")
    ys = (sv["y_a"], sv["y_b"], sv["y_c"])
    dbs = (db0, db1, db2)
    g["w_branch"] = [_mm(ys[n], dbs[n], "tn", F32, name=f"branch{n}_dw") for n in range(3)]
    dy = [_mm(dbs[n], p["w_branch"][n], "nt", BF16, name=f"branch{n}_dx") for n in range(3)]
    dtg, dy1a = rm(_b_s5_glu, [sv["y_ssm"], sv["pu"], sv["tg"], dy[2]], [p["s5_d"]], [(D, BF16), (D, F32)], [],
                   tile=t_r, name="s5_glu_bwd")
    dy1b = _mm(dtg, p["w_glu"], "nt", BF16, name="s5_glu_mm_dx")
    g["w_glu"] = _mm(sv["y1"], dtg, "tn", F32, name="s5_glu_mm_dw")
    dys, du_skip, g["s5_d"] = rm(_b_s5_pre, [sv["y_ssm"], sv["pu"], dy1a, dy1b], [p["s5_d"]],
                                 [(D, F32), (D, F32)], [(1, D)], tile=t_r, name="s5_pre_bwd")
    du, g["Bsg"], g["Csg"], g["lam"] = _s5_bwd(sv["pu"], sv["hst"], dys, du_skip, p["Bsg"], p["Csg"], p["lam"],
                                               B=B, NC=NC)
    dq, dk, dv, dckT, dcq = _fox_bwd(sv["qkv"], dy[1], sv["y_b"], sv["lse"], sv["cumT"], B=B, Lp=Lp)
    dcum8 = dcq.reshape(B, FOX_H, Lp).transpose(0, 2, 1) + dckT.reshape(B, FOX_H, Lp).transpose(0, 2, 1)
    dcum = jnp.pad(dcum8.reshape(T, FOX_H), ((0, 0), (SSD_H, CH - SSD_H - FOX_H)))
    dlogf = _cumsum_seq(dcum, B=B, NC=NC, reverse=True, name="fox_cum_bwd")
    dy_ssd, dxs_skip, dz, g["d_rep"], g["ssd_norm"] = rm(
        _b_ssd_post, [sv["y_ssd"], (sv["xbc"], D, 0), sv["pz"], dy[0]], [p["d_rep"], p["ssd_norm"]],
        [(D, F32), (D, F32), (D, BF16)], [(1, D), (1, D)], tile=t_r, name="ssd_post_bwd")
    dxbc_act, ddt, g["a_log"] = _ssd_bwd(sv["xbc"], sv["sm"], p["a_log"], sv["states"], dy_ssd, dxs_skip, B=B, NC=NC)
    dpsm, g["smallbias"] = rm(_b_smallact, [sv["psm"], ddt, dlogf], [p["smallbias"]], [(CH, BF16)], [(1, CH)],
                              tile=t_r, name="smallact_bwd")
    dconv, g["conv_w"], g["conv_b"] = _conv_bwd_pre(sv["pxbc"], dxbc_act, p["conv_w"], p["conv_b"], geom=geom, tile=t_r)
    dpxbc = _conv_bwd_x(dconv, p["conv_w"], geom=geom, tile=t_r)
    dproj = jnp.concatenate([dz, dpxbc, dq.astype(BF16), dk, dv, du, dpg, dpsm], axis=1)
    dxn1 = _mm(dproj, p["w_cat"], "nt", F32, name="in_dx")
    g["w_cat"] = _mm(sv["xn1"], dproj, "tn", F32, name="in_dw")
    dx_in, g["norm1"] = rm(_b_norm, [sv["x"], dxn1, dx_mid], [p["norm1"]], [(D, F32)], [(1, D)], tile=t_r,
                           name="norm1_bwd")
    return dx_in, g


_BIG = ["w_in", "s5_w_glu", "w_branch", "w_out", "w_ffn_in", "w_ffn_out"]
_NAMES = ['meta', 'norm1', 'w_in', 'ssd_conv_w', 'ssd_conv_b', 'ssd_dt_bias', 'ssd_a_log', 'ssd_d', 'ssd_norm',
          'fox_bf', 's5_lam_re', 's5_lam_im', 's5_b_re', 's5_b_im', 's5_c_re', 's5_c_im', 's5_log_step', 's5_d',
          's5_w_glu', 'w_branch', 'w_out', 'norm2', 'w_ffn_in', 'w_ffn_out', 'norm_f']
_SHARD_AXIS = {"meta": 1, "ssd_conv_w": 2}
_BIG_AXIS = {"w_in": 2, "s5_w_glu": 1, "w_branch": 2, "w_out": 1, "w_ffn_in": 2, "w_ffn_out": 1}


def kernel(x, meta, norm1, w_in, ssd_conv_w, ssd_conv_b, ssd_dt_bias, ssd_a_log, ssd_d, ssd_norm, fox_bf, s5_lam_re, s5_lam_im, s5_b_re, s5_b_im, s5_c_re, s5_c_im, s5_log_step, s5_d, s5_w_glu, w_branch, w_out, norm2, w_ffn_in, w_ffn_out, norm_f, loss_target, m_meta, m_norm1, m_w_in, m_ssd_conv_w, m_ssd_conv_b, m_ssd_dt_bias, m_ssd_a_log, m_ssd_d, m_ssd_norm, m_fox_bf, m_s5_lam_re, m_s5_lam_im, m_s5_b_re, m_s5_b_im, m_s5_c_re, m_s5_c_im, m_s5_log_step, m_s5_d, m_s5_w_glu, m_w_branch, m_w_out, m_norm2, m_w_ffn_in, m_w_ffn_out, m_norm_f, v_meta, v_norm1, v_w_in, v_ssd_conv_w, v_ssd_conv_b, v_ssd_dt_bias, v_ssd_a_log, v_ssd_d, v_ssd_norm, v_fox_bf, v_s5_lam_re, v_s5_lam_im, v_s5_b_re, v_s5_b_im, v_s5_c_re, v_s5_c_im, v_s5_log_step, v_s5_d, v_s5_w_glu, v_w_branch, v_w_out, v_norm2, v_w_ffn_in, v_w_ffn_out, v_norm_f):
    args = locals()
    W = {n: args[n] for n in _NAMES}
    Mo = {n: args["m_" + n] for n in _NAMES}
    Vo = {n: args["v_" + n] for n in _NAMES}
    B, S, _ = x.shape
    depth = norm1.shape[0]
    L = S + N_META
    Lp = ((L + CH - 1) // CH) * CH
    pad = Lp - L
    assert pad + N_META == CH and S % CH == 0
    NC = Lp // CH
    T = B * Lp
    geom = (T, Lp, pad)
    dims = (B, NC, Lp, pad)
    xi, yi, ci = _me()
    dev = 4 * xi + 2 * yi + ci

    gath = {n: _all_gather(W[n].astype(BF16), "gather_" + n) for n in _BIG}
    full = {n: jnp.concatenate([gath[n][k] for k in range(N_DEV)], axis=_BIG_AXIS[n]) for n in _BIG}
    sm_pack = _Packer([meta.shape, ssd_conv_w.shape], 8)
    sm_g = _all_gather(sm_pack.pack([meta, ssd_conv_w], F32), "gather_small")
    sm_parts = [sm_pack.unpack(sm_g[k]) for k in range(N_DEV)]
    meta_full = jnp.concatenate([sm_parts[k][0] for k in range(N_DEV)], axis=1)
    conv_w_full = jnp.concatenate([sm_parts[k][1] for k in range(N_DEV)], axis=2)

    layers = []
    s5_in = []
    for l in range(depth):
        lre = _s5_tile(s5_lam_re[l])
        lim = _s5_tile(s5_lam_im[l])
        lst = _s5_tile(jnp.repeat(s5_log_step[l], S5_P))
        bre = _s5_tile_b(s5_b_re[l])
        bim = _s5_tile_b(s5_b_im[l])
        s5_in.append((lre, lim, lst, bre, bim))
        a, b, br_, bi_ = _s5_params(lre, lim, lst, bre, bim)
        Bsg, Csg = _s5_blockdiag(_s5_untile_b(br_), _s5_untile_b(bi_), s5_c_re[l], s5_c_im[l])
        NS = S5_G // S5_SG
        lam = jnp.concatenate([a.reshape(NS, 1, S5_W), b.reshape(NS, 1, S5_W)], axis=2)
        zpad = jnp.zeros((CH - SSD_H - FOX_H,), F32)
        layers.append(dict(
            norm1=norm1[l][None], w_cat=_w_cat(full["w_in"][l]),
            smallbias=jnp.concatenate([ssd_dt_bias[l], fox_bf[l], zpad])[None],
            conv_w=conv_w_full[l], conv_b=ssd_conv_b[l][None],
            a_log=jnp.concatenate([ssd_a_log[l], jnp.zeros((CH - SSD_H,), F32)])[None],
            d_rep=jnp.repeat(ssd_d[l], SSD_P)[None], ssd_norm=ssd_norm[l][None],
            Bsg=Bsg, Csg=Csg, lam=lam, s5_d=s5_d[l][None], w_glu=full["s5_w_glu"][l],
            w_branch=[full["w_branch"][l, n] for n in range(3)], w_out=full["w_out"][l],
            norm2=norm2[l][None], w_ffn_in=full["w_ffn_in"][l], w_ffn_out=full["w_ffn_out"][l]))

    xs = jnp.concatenate([jnp.zeros((B, pad, D), F32), jnp.broadcast_to(meta_full[None], (B, N_META, D)), x], axis=1)
    h = xs.reshape(T, D)
    saved = []
    for l in range(depth):
        h, sv = _layer_fwd(h, layers[l], geom, dims)
        saved.append(sv)
    dh, loss_row, g_nf = _loss_head(h, norm_f[None], loss_target.reshape(B * S, D), B=B, NC=NC, S=S)
    loss = lax.psum(loss_row[0, 0], AXES)

    G = {n: [None] * depth for n in _NAMES}
    for l in reversed(range(depth)):
        dh, g = _layer_bwd(dh, layers[l], saved[l], geom, dims)
        saved[l] = None
        G["norm1"][l] = g["norm1"][0]
        G["norm2"][l] = g["norm2"][0]
        G["w_in"][l] = _w_uncat(g["w_cat"])
        G["ssd_conv_w"][l] = g["conv_w"]
        G["ssd_conv_b"][l] = g["conv_b"][0]
        G["ssd_dt_bias"][l] = g["smallbias"][0, 0:SSD_H]
        G["fox_bf"][l] = g["smallbias"][0, SSD_H:SSD_H + FOX_H]
        G["ssd_a_log"][l] = g["a_log"][0, 0:SSD_H]
        G["ssd_d"][l] = g["d_rep"].reshape(SSD_H, SSD_P).sum(axis=1)
        G["ssd_norm"][l] = g["ssd_norm"][0]
        dbr, dbi, dcr, dci = _s5_unblock(g["Bsg"], g["Csg"])
        da = _s5_tile(g["lam"][:, 0, 0:S5_W])
        db = _s5_tile(g["lam"][:, 0, S5_W:])
        dlre, dlim, dlst, dbre, dbim = _s5_params_bwd(*s5_in[l], da, db, _s5_tile_b(dbr), _s5_tile_b(dbi))
        G["s5_lam_re"][l] = dlre.reshape(S5_G, S5_P)
        G["s5_lam_im"][l] = dlim.reshape(S5_G, S5_P)
        G["s5_log_step"][l] = dlst.reshape(S5_G, S5_P).sum(axis=1)
        G["s5_b_re"][l] = _s5_untile_b(dbre)
        G["s5_b_im"][l] = _s5_untile_b(dbim)
        G["s5_c_re"][l] = dcr
        G["s5_c_im"][l] = dci
        G["s5_d"][l] = g["s5_d"][0]
        G["s5_w_glu"][l] = g["w_glu"]
        G["w_branch"][l] = jnp.stack(g["w_branch"])
        G["w_out"][l] = g["w_out"]
        G["w_ffn_in"][l] = g["w_ffn_in"]
        G["w_ffn_out"][l] = g["w_ffn_out"]
    dxs = dh.reshape(B, Lp, D)
    grad_x = dxs[:, pad + N_META:, :]
    part = {n: jnp.stack(G[n]) for n in _NAMES if n not in ("meta", "norm_f")}
    part["meta"] = dxs[:, pad:pad + N_META, :].sum(axis=0)
    part["norm_f"] = g_nf[0]

    grads = {}
    for n in _BIG:
        ax = _BIG_AXIS[n]
        a = part[n]
        a = a.reshape(a.shape[:ax] + (N_DEV, a.shape[ax] // N_DEV) + a.shape[ax + 1:])
        grads[n] = _reduce_scatter(jnp.moveaxis(a, ax, 0), n)

    small = [n for n in _NAMES if n not in _BIG]
    sp = _Packer([part[n].shape for n in small], 128)
    tot = sp.unpack(_sum8(_all_gather(sp.pack([part[n] for n in small], F32), "gather_small_grads")))
    for n, t in zip(small, tot):
        if n in _SHARD_AXIS:
            ax = _SHARD_AXIS[n]
            w = W[n].shape[ax]
            t = lax.dynamic_slice_in_dim(t, dev * w, w, axis=ax)
        grads[n] = t

    delta, new_m, new_v = {}, {}, {}
    for n in _BIG:
        delta[n], new_m[n], new_v[n] = _adam(W[n], grads[n], Mo[n], Vo[n], "adam_" + n)
    ap = _Packer([W[n].shape for n in small], 128)
    d_, m_, v_ = _adam(ap.pack([W[n] for n in small], F32), ap.pack([grads[n] for n in small], F32),
                       ap.pack([Mo[n] for n in small], F32), ap.pack([Vo[n] for n in small], F32), "adam_small")
    for n, a, b, c in zip(small, ap.unpack(d_), ap.unpack(m_), ap.unpack(v_)):
        delta[n], new_m[n], new_v[n] = a, b, c
    return (loss, grad_x, *[grads[n] for n in _NAMES], *[delta[n] for n in _NAMES],
            *[new_m[n] for n in _NAMES], *[new_v[n] for n in _NAMES])
```

```python
import functools
import math

import numpy as np
import jax
import jax.numpy as jnp
from jax import lax
from jax.experimental import pallas as pl
from jax.experimental.pallas import tpu as pltpu

F32 = jnp.float32
BF16 = jnp.bfloat16
AXES = ("x", "y", "c")
MESH = pl.DeviceIdType.MESH
N_DEV = 8

D = 1024
N_META = 16
CH = 128
EPS = 1e-6
NEG = -1e30
SSD_H, SSD_P, SSD_N, SSD_G = 16, 64, 128, 2
CONV_K, CONV_DIM = 4, 1536
FOX_H, FOX_DH = 8, 128
S5_G, S5_P, S5_C = 64, 64, 16
S5_SG = 8
S5_W = S5_SG * S5_P
DFF = 2816
D_IN = 9752
OFF_Z, OFF_XBC, OFF_QKV, OFF_U, OFF_G, OFF_SM, D_CAT = 0, 1024, 2560, 5632, 6656, 9728, 9856
O_Z, O_XBC, O_DT, O_QKV, O_F, O_U, O_G = 0, 1024, 2560, 2576, 5648, 5656, 6680

ADAM_LR, ADAM_B1, ADAM_B2, ADAM_EPS, ADAM_WD, ADAM_STEP = 0.001, 0.9, 0.999, 1e-08, 0.01, 10

VMEM_LIMIT_V7X = 52 * 1024 * 1024
HI = lax.Precision.HIGHEST


def _cp(sem=None):
    return pltpu.CompilerParams(dimension_semantics=sem, vmem_limit_bytes=VMEM_LIMIT_V7X)


def _pick(n, cands):
    for c in cands:
        if n % c == 0:
            return c
    raise ValueError(f"no tile for {n}")


_TILES = (1408, 1024, 896, 768, 512, 384, 256, 128)


def _mm(a, b, mode, out_dtype, *, name, n=None, b_off=0, res=None, tm=None, tn=None, tk=None):
    if mode == "tn":
        K, M = a.shape
    else:
        M, K = a.shape
    if mode == "nt":
        N = b.shape[0]
    else:
        N = n if n is not None else b.shape[1]
    tm = tm or _pick(M, (1408, 1024, 768, 512, 384, 256, 128, 64, 16, 8))
    tn = tn or _pick(math.gcd(N, b_off) if b_off else N, (1408, 1024, 896, 768, 512, 384, 256, 128))
    tk = tk or _pick(K, _TILES)
    nk = K // tk
    joff = b_off // tn

    def body(*refs):
        if res is None:
            a_ref, b_ref, o_ref, acc = refs
            r_ref = None
        else:
            a_ref, b_ref, r_ref, o_ref, acc = refs
        k = pl.program_id(2)
        av = a_ref[...].astype(BF16)
        bv = b_ref[...].astype(BF16)
        if mode == "nn":
            p = jnp.dot(av, bv, preferred_element_type=F32)
        elif mode == "nt":
            p = lax.dot_general(av, bv, (((1,), (1,)), ((), ())), preferred_element_type=F32)
        else:
            p = lax.dot_general(av, bv, (((0,), (0,)), ((), ())), preferred_element_type=F32)

        @pl.when(k == 0)
        def _():
            acc[...] = p

        @pl.when(k > 0)
        def _():
            acc[...] += p

        @pl.when(k == nk - 1)
        def _():
            r = acc[...]
            if r_ref is not None:
                r = r + r_ref[...]
            o_ref[...] = r.astype(o_ref.dtype)

    if mode == "tn":
        a_spec = pl.BlockSpec((tk, tm), lambda i, j, k: (k, i))
    else:
        a_spec = pl.BlockSpec((tm, tk), lambda i, j, k: (i, k))
    if mode == "nt":
        b_spec = pl.BlockSpec((tn, tk), lambda i, j, k: (j, k))
    else:
        b_spec = pl.BlockSpec((tk, tn), lambda i, j, k: (k, j + joff))
    o_spec = pl.BlockSpec((tm, tn), lambda i, j, k: (i, j))
    in_specs = [a_spec, b_spec] + ([o_spec] if res is not None else [])
    args = (a, b) + ((res,) if res is not None else ())
    return pl.pallas_call(
        body, name=name, grid=(M // tm, N // tn, nk),
        in_specs=in_specs, out_specs=o_spec,
        out_shape=jax.ShapeDtypeStruct((M, N), out_dtype),
        scratch_shapes=[pltpu.VMEM((tm, tn), F32)],
        compiler_params=_cp(("parallel", "parallel", "arbitrary")),
    )(*args)


def _rowmap(fn, row_ins, const_ins, row_outs, acc_outs, *, geom, tile, name):
    T, Lp, pad = geom
    assert Lp % tile == 0
    per_seq = Lp // tile
    specs, args = [], []
    for r in row_ins:
        arr, w, cb = r if isinstance(r, tuple) else (r, r.shape[1], 0)
        specs.append(pl.BlockSpec((tile, w), functools.partial(lambda i, cb: (i, cb), cb=cb)))
        args.append(arr)
    for c in const_ins:
        specs.append(pl.BlockSpec(c.shape, functools.partial(lambda i, nd: (0,) * nd, nd=c.ndim)))
        args.append(c)
    n_r, n_c, n_o, n_a = len(row_ins), len(const_ins), len(row_outs), len(acc_outs)
    out_specs = [pl.BlockSpec((tile, w), lambda i: (i, 0)) for w, _ in row_outs]
    out_specs += [pl.BlockSpec(s, lambda i: (0, 0)) for s in acc_outs]
    out_shape = [jax.ShapeDtypeStruct((T, w), dt) for w, dt in row_outs]
    out_shape += [jax.ShapeDtypeStruct(s, F32) for s in acc_outs]

    def body(*refs):
        i = pl.program_id(0)
        pos = (i % per_seq) * tile + lax.broadcasted_iota(jnp.int32, (tile, 1), 0)
        valid = pos >= pad
        vals = [r[...].astype(F32) for r in refs[:n_r]] + [r[...] for r in refs[n_r:n_r + n_c]]
        outs = fn(valid, *vals)
        if not isinstance(outs, (tuple, list)):
            outs = (outs,)
        orefs = refs[n_r + n_c:]
        for r, v in zip(orefs[:n_o], outs[:n_o]):
            r[...] = v.astype(r.dtype)
        for r, v in zip(orefs[n_o:], outs[n_o:]):
            @pl.when(i == 0)
            def _(r=r, v=v):
                r[...] = v

            @pl.when(i > 0)
            def _(r=r, v=v):
                r[...] += v

    res = pl.pallas_call(
        body, name=name, grid=(T // tile,), in_specs=specs, out_specs=out_specs, out_shape=out_shape,
        compiler_params=_cp(("arbitrary",)),
    )(*args)
    return res


def _sigmoid(x):
    return 1.0 / (1.0 + jnp.exp(-x))


def _silu(x):
    return x * _sigmoid(x)


def _softplus(x):
    return jnp.maximum(x, 0.0) + jnp.log(1.0 + jnp.exp(-jnp.abs(x)))


def _gelu(x):
    return 0.5 * x * (1.0 + jnp.tanh(math.sqrt(2.0 / math.pi) * (x + 0.044715 * x * x * x)))


def _rms(x, w):
    return x * lax.rsqrt(jnp.mean(x * x, axis=-1, keepdims=True) + EPS) * w


def _colsum(v):
    return jnp.sum(v, axis=0, keepdims=True)


def _f_norm(valid, x, w):
    return _rms(x, w)


def _b_norm(valid, x, dxn, dres, w):
    _, vjp = jax.vjp(_rms, x, w)
    dx, dw = vjp(dxn)
    return jnp.where(valid, dx + dres, 0.0), dw


def _smallact(valid, raw, bias):
    lane = lax.broadcasted_iota(jnp.int32, raw.shape, 1)
    v = raw + bias
    dt = _softplus(v)
    logf = -_softplus(-v)
    out = jnp.where(lane < SSD_H, dt, jnp.where(lane < SSD_H + FOX_H, logf, 0.0))
    return jnp.where(valid, out, 0.0)


def _b_smallact(valid, raw, d1, d2, bias):
    _, vjp = jax.vjp(lambda r, b: _smallact(valid, r, b), raw, bias)
    return vjp(d1 + d2)


def _ssd_post(valid, y, xs, z, drep, nw):
    y = (y + xs * drep) * _silu(z)
    return _rms(y, nw)


def _b_ssd_post(valid, y, xs, z, dya, drep, nw):
    _, vjp = jax.vjp(lambda a, b, c, d, e: _ssd_post(valid, a, b, c, d, e), y, xs, z, drep, nw)
    dy, dxs, dz, dd, dn = vjp(dya)
    return dy, dxs, dz, dd, dn


def _s5_pre(valid, ys, u, d):
    return _gelu(ys + d * u)


def _s5_glu(valid, ys, u, t, d):
    y1 = _gelu(ys + d * u)
    return y1 * _sigmoid(t)


def _b_s5_glu(valid, ys, u, t, dyc, d):
    y1 = _gelu(ys + d * u)
    _, vjp = jax.vjp(lambda a, b: a * _sigmoid(b), y1, t)
    dy1, dt = vjp(dyc)
    return dt, dy1


def _b_s5_pre(valid, ys, u, dy1a, dy1b, d):
    _, vjp = jax.vjp(lambda a, b, c: _gelu(a + c * b), ys, u, d)
    dys, du, dd = vjp(dy1a + dy1b)
    return dys, du, dd


def _merge(valid, g0, g1, g2, b0, b1, b2):
    m = _sigmoid(g0) * b0 + _sigmoid(g1) * b1 + _sigmoid(g2) * b2
    return jnp.where(valid, m, 0.0)


def _b_merge(valid, g0, g1, g2, b0, b1, b2, dmix):
    _, vjp = jax.vjp(lambda *a: _merge(valid, *a), g0, g1, g2, b0, b1, b2)
    d = vjp(dmix)
    return jnp.concatenate(d[:3], axis=1), d[3], d[4], d[5]


def _swiglu(valid, g, up):
    return _silu(g) * up


def _b_swiglu(valid, g, up, dact):
    _, vjp = jax.vjp(lambda a, b: _silu(a) * b, g, up)
    dg, dup = vjp(dact)
    return jnp.concatenate([dg, dup], axis=1)


def _conv_taps(ext, tile):
    taps = []
    for k in range(CONV_K):
        sh = CONV_K - 1 - k
        v = ext if sh == 0 else pltpu.roll(ext, sh, 0)
        taps.append(v[8:8 + tile])
    return taps


def _conv_fwd(x, w, b, *, geom, tile):
    T, Lp, pad = geom
    per_seq = Lp // tile
    hb = tile // 8

    def body(x_ref, h_ref, w_ref, b_ref, o_ref):
        i = pl.program_id(0)
        pos = (i % per_seq) * tile + lax.broadcasted_iota(jnp.int32, (tile, 1), 0)
        ext = jnp.concatenate([h_ref[...], x_ref[...]], axis=0)
        taps = _conv_taps(ext, tile)
        acc = b_ref[...] + taps[0] * w_ref[0:1, :]
        for k in range(1, CONV_K):
            acc = acc + taps[k] * w_ref[k:k + 1, :]
        o_ref[...] = jnp.where(pos >= pad, _silu(acc), 0.0)

    return pl.pallas_call(
        body, name="conv_fwd", grid=(T // tile,),
        in_specs=[pl.BlockSpec((tile, CONV_DIM), lambda i: (i, 0)),
                  pl.BlockSpec((8, CONV_DIM), lambda i: (jnp.maximum(i * hb - 1, 0), 0)),
                  pl.BlockSpec((CONV_K, CONV_DIM), lambda i: (0, 0)),
                  pl.BlockSpec((1, CONV_DIM), lambda i: (0, 0))],
        out_specs=pl.BlockSpec((tile, CONV_DIM), lambda i: (i, 0)),
        out_shape=jax.ShapeDtypeStruct((T, CONV_DIM), F32),
        compiler_params=_cp(("arbitrary",)),
    )(x, x, w, b)


def _conv_bwd_pre(x, dact, w, b, *, geom, tile):
    T, Lp, pad = geom
    per_seq = Lp // tile
    hb = tile // 8

    def body(x_ref, h_ref, d_ref, w_ref, b_ref, dc_ref, dw_ref, db_ref):
        i = pl.program_id(0)
        pos = (i % per_seq) * tile + lax.broadcasted_iota(jnp.int32, (tile, 1), 0)
        ext = jnp.concatenate([h_ref[...], x_ref[...]], axis=0)
        taps = _conv_taps(ext, tile)
        acc = b_ref[...] + taps[0] * w_ref[0:1, :]
        for k in range(1, CONV_K):
            acc = acc + taps[k] * w_ref[k:k + 1, :]
        sg = _sigmoid(acc)
        dsilu = sg * (1.0 + acc * (1.0 - sg))
        dc = jnp.where(pos >= pad, d_ref[...] * dsilu, 0.0)
        dc_ref[...] = dc
        dw = jnp.concatenate([_colsum(dc * taps[k]) for k in range(CONV_K)], axis=0)
        db = _colsum(dc)

        @pl.when(i == 0)
        def _():
            dw_ref[...] = dw
            db_ref[...] = db

        @pl.when(i > 0)
        def _():
            dw_ref[...] += dw
            db_ref[...] += db

    return pl.pallas_call(
        body, name="conv_bwd_pre", grid=(T // tile,),
        in_specs=[pl.BlockSpec((tile, CONV_DIM), lambda i: (i, 0)),
                  pl.BlockSpec((8, CONV_DIM), lambda i: (jnp.maximum(i * hb - 1, 0), 0)),
                  pl.BlockSpec((tile, CONV_DIM), lambda i: (i, 0)),
                  pl.BlockSpec((CONV_K, CONV_DIM), lambda i: (0, 0)),
                  pl.BlockSpec((1, CONV_DIM), lambda i: (0, 0))],
        out_specs=[pl.BlockSpec((tile, CONV_DIM), lambda i: (i, 0)),
                   pl.BlockSpec((CONV_K, CONV_DIM), lambda i: (0, 0)),
                   pl.BlockSpec((1, CONV_DIM), lambda i: (0, 0))],
        out_shape=[jax.ShapeDtypeStruct((T, CONV_DIM), F32),
                   jax.ShapeDtypeStruct((CONV_K, CONV_DIM), F32),
                   jax.ShapeDtypeStruct((1, CONV_DIM), F32)],
        compiler_params=_cp(("arbitrary",)),
    )(x, x, dact, w, b)


def _conv_bwd_x(dc, w, *, geom, tile):
    T, Lp, pad = geom
    nt = T // tile
    hb = tile // 8

    def body(d_ref, h_ref, w_ref, o_ref):
        i = pl.program_id(0)
        halo = jnp.where(i < nt - 1, h_ref[...], 0.0)
        ext = jnp.concatenate([d_ref[...], halo], axis=0)
        n_ext = tile + 8
        acc = ext[0:tile] * w_ref[CONV_K - 1:CONV_K, :]
        for j in range(1, CONV_K):
            acc = acc + pltpu.roll(ext, n_ext - j, 0)[0:tile] * w_ref[CONV_K - 1 - j:CONV_K - j, :]
        o_ref[...] = acc.astype(o_ref.dtype)

    return pl.pallas_call(
        body, name="conv_bwd_x", grid=(nt,),
        in_specs=[pl.BlockSpec((tile, CONV_DIM), lambda i: (i, 0)),
                  pl.BlockSpec((8, CONV_DIM), lambda i: (jnp.minimum((i + 1) * hb, nt * hb - 1), 0)),
                  pl.BlockSpec((CONV_K, CONV_DIM), lambda i: (0, 0))],
        out_specs=pl.BlockSpec((tile, CONV_DIM), lambda i: (i, 0)),
        out_shape=jax.ShapeDtypeStruct((T, CONV_DIM), BF16),
        compiler_params=_cp(("arbitrary",)),
    )(dc, dc, w)


def _ssd_common(sm_ref, alog_ref):
    lane = lax.broadcasted_iota(jnp.int32, (1, CH), 1)
    A = jnp.where(lane < SSD_H, -jnp.exp(alog_ref[...]), 0.0)
    dt = sm_ref[...]
    adt = dt * A
    r = lax.broadcasted_iota(jnp.int32, (CH, CH), 0)
    c = lax.broadcasted_iota(jnp.int32, (CH, CH), 1)
    tril = (r >= c).astype(F32)
    cs = jnp.dot(tril, adt, precision=HI, preferred_element_type=F32)
    csT = cs.T
    cs_last = jnp.sum(jnp.where(r == CH - 1, cs, 0.0), axis=0, keepdims=True)
    return A, dt, cs, csT, cs_last, tril, r, c


def _ssd_lanes():
    r = lax.broadcasted_iota(jnp.int32, (CH, D), 0)
    c = lax.broadcasted_iota(jnp.int32, (CH, D), 1)
    return (c // SSD_P == r).astype(F32)


def _ssd_per_lane(dt, cs):
    ex = _ssd_lanes()
    dt_rep = jnp.dot(dt, ex, precision=HI, preferred_element_type=F32)
    cs_rep = jnp.dot(cs, ex, precision=HI, preferred_element_type=F32)
    r = lax.broadcasted_iota(jnp.int32, (CH, D), 0)
    last_rep = jnp.sum(jnp.where(r == CH - 1, cs_rep, 0.0), axis=0, keepdims=True)
    return dt_rep, cs_rep, last_rep


def _ssd_per_head(*per_lane):
    ex = _ssd_lanes()
    return [lax.dot_general(v, ex, (((1,), (1,)), ((), ())), precision=HI, preferred_element_type=F32)
            for v in per_lane]


def _nt(a, b):
    return lax.dot_general(a, b, (((1,), (1,)), ((), ())), preferred_element_type=F32)


def _tn(a, b):
    return lax.dot_general(a, b, (((0,), (0,)), ((), ())), preferred_element_type=F32)


def _nn(a, b):
    return jnp.dot(a, b, preferred_element_type=F32)


def _ssd_fwd(xbc, sm, alog, *, B, NC):
    T = B * NC * CH

    def body(x_ref, sm_ref, alog_ref, y_ref, st_ref, S):
        cidx = pl.program_id(1)

        @pl.when(cidx == 0)
        def _():
            S[...] = jnp.zeros_like(S)

        st_ref[0] = S[...]
        A, dt, cs, csT, cs_last, tril, _, _ = _ssd_common(sm_ref, alog_ref)
        dt_rep, cs_rep, last_rep = _ssd_per_lane(dt, cs)
        xdt = x_ref[:, 0:D] * dt_rep
        xdec = (xdt * jnp.exp(last_rep - cs_rep)).astype(BF16)
        e_rep = jnp.exp(cs_rep)
        HG = SSD_H // SSD_G
        for g in range(SSD_G):
            gl = slice(g * HG * SSD_P, (g + 1) * HG * SSD_P)
            Bb = x_ref[:, D + g * SSD_N:D + (g + 1) * SSD_N].astype(BF16)
            Cb = x_ref[:, D + SSD_G * SSD_N + g * SSD_N:D + SSD_G * SSD_N + (g + 1) * SSD_N].astype(BF16)
            G = _nt(Cb, Bb)
            STg = S[g * HG:(g + 1) * HG].reshape(HG * SSD_P, SSD_N)
            y_off = e_rep[:, gl] * _nt(Cb, STg.astype(BF16))
            upd = _tn(xdec[:, gl], Bb)
            for rr in range(HG):
                h = g * HG + rr
                hl = slice(h * SSD_P, (h + 1) * SSD_P)
                col = cs[:, h:h + 1]
                row = csT[h:h + 1, :]
                Ld = jnp.where(tril > 0, jnp.exp(jnp.minimum(col - row, 0.0)), 0.0)
                M = (G * Ld).astype(BF16)
                y_ref[:, hl] = _nn(M, xdt[:, hl].astype(BF16)) + y_off[:, rr * SSD_P:(rr + 1) * SSD_P]
                rows = slice(rr * SSD_P, (rr + 1) * SSD_P)
                S[h] = jnp.exp(cs_last[:, h:h + 1]) * STg[rows] + upd[rows]

    return pl.pallas_call(
        body, name="ssd_fwd", grid=(B, NC),
        in_specs=[pl.BlockSpec((CH, CONV_DIM), lambda b, c: (b * NC + c, 0)),
                  pl.BlockSpec((CH, CH), lambda b, c: (b * NC + c, 0)),
                  pl.BlockSpec((1, CH), lambda b, c: (0, 0))],
        out_specs=[pl.BlockSpec((CH, D), lambda b, c: (b * NC + c, 0)),
                   pl.BlockSpec((1, SSD_H, SSD_P, SSD_N), lambda b, c: (b * NC + c, 0, 0, 0))],
        out_shape=[jax.ShapeDtypeStruct((T, D), F32),
                   jax.ShapeDtypeStruct((B * NC, SSD_H, SSD_P, SSD_N), F32)],
        scratch_shapes=[pltpu.VMEM((SSD_H, SSD_P, SSD_N), F32)],
        compiler_params=_cp(("arbitrary", "arbitrary")),
    )(xbc, sm, alog)


def _ssd_bwd(xbc, sm, alog, states, dy, dxs_skip, *, B, NC):
    T = B * NC * CH

    def rix(b, c):
        return b * NC + (NC - 1 - c)

    def body(x_ref, sm_ref, alog_ref, st_ref, dy_ref, sk_ref, dx_ref, ddt_ref, dal_ref, dS):
        bidx = pl.program_id(0)
        cidx = pl.program_id(1)

        @pl.when(cidx == 0)
        def _():
            dS[...] = jnp.zeros_like(dS)

        A, dt, cs, csT, cs_last, tril, r, c = _ssd_common(sm_ref, alog_ref)
        lane = lax.broadcasted_iota(jnp.int32, (1, CH), 1)
        dt_rep, cs_rep, last_rep = _ssd_per_lane(dt, cs)
        xs = x_ref[:, 0:D]
        xdt = xs * dt_rep
        e_rep = jnp.exp(cs_rep)
        dec_rep = jnp.exp(last_rep - cs_rep)
        dye = dy_ref[...] * e_rep
        xdec = xdt * dec_rep
        HG = SSD_H // SSD_G
        DCcol = jnp.zeros((CH, CH), F32)
        DCrow = jnp.zeros((CH, CH), F32)
        dlast = jnp.zeros((1, CH), F32)
        t_off, t_dec, dx_state = [], [], []
        for g in range(SSD_G):
            ob = D + g * SSD_N
            oc = D + SSD_G * SSD_N + g * SSD_N
            gl = slice(g * HG * SSD_P, (g + 1) * HG * SSD_P)
            Bb = x_ref[:, ob:ob + SSD_N].astype(BF16)
            Cb = x_ref[:, oc:oc + SSD_N].astype(BF16)
            G = _nt(Cb, Bb)
            STg = st_ref[0, g * HG:(g + 1) * HG].reshape(HG * SSD_P, SSD_N)
            dSTg = dS[g * HG:(g + 1) * HG].reshape(HG * SSD_P, SSD_N)
            STb = STg.astype(BF16)
            dSTb = dSTg.astype(BF16)
            dyeb = dye[:, gl].astype(BF16)
            t_off.append(dye[:, gl] * _nt(Cb, STb))
            dCg = _nn(dyeb, STb)
            dS_in = _tn(dyeb, Cb)
            Z = _nt(Bb, dSTb)
            dx_state.append(dec_rep[:, gl] * Z)
            t_dec.append(xdec[:, gl] * Z)
            dBg = _nn(xdec[:, gl].astype(BF16), dSTb)
            dG = jnp.zeros((CH, CH), F32)
            for rr in range(HG):
                h = g * HG + rr
                hl = slice(h * SSD_P, (h + 1) * SSD_P)
                rows = slice(rr * SSD_P, (rr + 1) * SSD_P)
                col = cs[:, h:h + 1]
                row = csT[h:h + 1, :]
                Ld = jnp.where(tril > 0, jnp.exp(jnp.minimum(col - row, 0.0)), 0.0)
                Mf = G * Ld
                dyb = dy_ref[:, hl].astype(BF16)
                dx_ref[:, hl] = _tn(Mf.astype(BF16), dyb)
                dM = _nt(dyb, xdt[:, hl].astype(BF16))
                dG = dG + dM * Ld
                W = dM * Mf
                DCcol = DCcol + jnp.where(c == h, jnp.sum(W, axis=1, keepdims=True), 0.0)
                DCrow = DCrow - jnp.where(r == h, jnp.sum(W, axis=0, keepdims=True), 0.0)
                el = jnp.exp(cs_last[:, h:h + 1])
                dl = el * jnp.sum(jnp.sum(dSTg[rows] * STg[rows], axis=1, keepdims=True), axis=0, keepdims=True)
                dlast = dlast + jnp.where(lane == h, dl, 0.0)
                dS[h] = dS_in[rows] + el * dSTg[rows]
            dGb = dG.astype(BF16)
            dx_ref[:, ob:ob + SSD_N] = dBg + _tn(dGb, Cb)
            dx_ref[:, oc:oc + SSD_N] = dCg + _nn(dGb, Bb)
        dxdt = dx_ref[:, 0:D] + jnp.concatenate(dx_state, axis=1)
        dx_ref[:, 0:D] = dxdt * dt_rep + sk_ref[...]
        s_off, s_dec, DX = _ssd_per_head(jnp.concatenate(t_off, axis=1), jnp.concatenate(t_dec, axis=1), dxdt * xs)
        dlast = dlast + jnp.sum(s_dec, axis=0, keepdims=True)
        DC = DCcol + s_off - s_dec + DCrow.T + jnp.where(r == CH - 1, dlast, 0.0)
        triu = (r <= c).astype(F32)
        dadt = jnp.dot(triu, DC, precision=HI, preferred_element_type=F32)
        ddt_ref[...] = dadt * A + DX
        dal = jnp.sum(dadt * dt, axis=0, keepdims=True) * A

        @pl.when((bidx == 0) & (cidx == 0))
        def _():
            dal_ref[...] = dal

        @pl.when((bidx > 0) | (cidx > 0))
        def _():
            dal_ref[...] += dal

    return pl.pallas_call(
        body, name="ssd_bwd", grid=(B, NC),
        in_specs=[pl.BlockSpec((CH, CONV_DIM), lambda b, c: (rix(b, c), 0)),
                  pl.BlockSpec((CH, CH), lambda b, c: (rix(b, c), 0)),
                  pl.BlockSpec((1, CH), lambda b, c: (0, 0)),
                  pl.BlockSpec((1, SSD_H, SSD_P, SSD_N), lambda b, c: (rix(b, c), 0, 0, 0)),
                  pl.BlockSpec((CH, D), lambda b, c: (rix(b, c), 0)),
                  pl.BlockSpec((CH, D), lambda b, c: (rix(b, c), 0))],
        out_specs=[pl.BlockSpec((CH, CONV_DIM), lambda b, c: (rix(b, c), 0)),
                   pl.BlockSpec((CH, CH), lambda b, c: (rix(b, c), 0)),
                   pl.BlockSpec((1, CH), lambda b, c: (0, 0))],
        out_shape=[jax.ShapeDtypeStruct((T, CONV_DIM), F32),
                   jax.ShapeDtypeStruct((T, CH), F32),
                   jax.ShapeDtypeStruct((1, CH), F32)],
        scratch_shapes=[pltpu.VMEM((SSD_H, SSD_P, SSD_N), F32)],
        compiler_params=_cp(("arbitrary", "arbitrary")),
    )(xbc, sm, alog, states, dy, dxs_skip)


def _cumsum_seq(v, *, B, NC, reverse, name):
    T = B * NC * CH
    R = 3 * CH if (NC % 3 == 0 and NC > 3) else CH
    NR = NC * CH // R

    def ix(b, c):
        return b * NR + ((NR - 1 - c) if reverse else c)

    def body(v_ref, o_ref, carry):
        cidx = pl.program_id(1)

        @pl.when(cidx == 0)
        def _():
            carry[...] = jnp.zeros_like(carry)

        r = lax.broadcasted_iota(jnp.int32, (R, R), 0)
        c = lax.broadcasted_iota(jnp.int32, (R, R), 1)
        tri = ((r <= c) if reverse else (r >= c)).astype(F32)
        cs = jnp.dot(tri, v_ref[...], precision=HI, preferred_element_type=F32) + carry[...]
        o_ref[...] = cs
        edge = 0 if reverse else R - 1
        rows = lax.broadcasted_iota(jnp.int32, (R, CH), 0)
        carry[...] = jnp.sum(jnp.where(rows == edge, cs, 0.0), axis=0, keepdims=True)

    return pl.pallas_call(
        body, name=name, grid=(B, NR),
        in_specs=[pl.BlockSpec((R, CH), lambda b, c: (ix(b, c), 0))],
        out_specs=pl.BlockSpec((R, CH), lambda b, c: (ix(b, c), 0)),
        out_shape=jax.ShapeDtypeStruct((T, CH), F32),
        scratch_shapes=[pltpu.VMEM((1, CH), F32)],
        compiler_params=_cp(("arbitrary", "arbitrary")),
    )(v)


def _fox_tb(Lp):
    return 384 if (Lp % 384 == 0 and Lp > 384) else CH


def _fox_keybias(cum, *, B, Lp, pad):
    ck = cum.reshape(B, Lp, CH)[:, :, SSD_H:SSD_H + FOX_H].transpose(0, 2, 1)
    pos = lax.broadcasted_iota(jnp.int32, ck.shape, 2)
    return jnp.where(pos < pad, -NEG, ck).reshape(B * FOX_H, 1, Lp)


def _fox_tril(TB):
    r = lax.broadcasted_iota(jnp.int32, (TB, TB), 0)
    c = lax.broadcasted_iota(jnp.int32, (TB, TB), 1)
    return r >= c


def _fox_fwd(qkv, cumT, *, B, Lp):
    TB = _fox_tb(Lp)
    NQ = Lp // TB
    T = B * Lp
    scale = FOX_DH ** -0.5

    def body(q_ref, k_ref, v_ref, ct_ref, o_ref, lse_ref):
        i = pl.program_id(2)
        q = q_ref[...]

        def block(j, nb, carry, diag):
            m, l, acc = carry
            off = pl.multiple_of(j * TB, TB)
            k = k_ref[pl.ds(off, nb * TB), :]
            v = v_ref[pl.ds(off, nb * TB), :]
            s = _nt(q, k) * scale - ct_ref[0, :, pl.ds(off, nb * TB)]
            if diag:
                s = jnp.where(_fox_tril(TB), s, NEG)
            m_new = jnp.maximum(m, jnp.max(s, axis=1, keepdims=True))
            p = jnp.exp(s - m_new)
            alpha = jnp.exp(m - m_new)
            l = alpha * l + jnp.sum(p, axis=1, keepdims=True)
            acc = alpha * acc + _nn(p.astype(BF16), v)
            return m_new, l, acc

        init = (jnp.full((TB, 1), NEG, F32), jnp.zeros((TB, 1), F32), jnp.zeros((TB, FOX_DH), F32))
        carry = lax.fori_loop(0, i // 2, lambda t, c: block(2 * t, 2, c, False), init)
        carry = lax.fori_loop(2 * (i // 2), i, lambda j, c: block(j, 1, c, False), carry)
        m, l, acc = block(i, 1, carry, True)
        o_ref[...] = (acc / l).astype(o_ref.dtype)
        lse_ref[0, 0] = m + jnp.log(l)

    return pl.pallas_call(
        body, name="fox_fwd", grid=(B, FOX_H, NQ),
        in_specs=[pl.BlockSpec((TB, FOX_DH), lambda b, h, i: (b * NQ + i, h)),
                  pl.BlockSpec((Lp, FOX_DH), lambda b, h, i: (b, FOX_H + h)),
                  pl.BlockSpec((Lp, FOX_DH), lambda b, h, i: (b, 2 * FOX_H + h)),
                  pl.BlockSpec((1, 1, Lp), lambda b, h, i: (b * FOX_H + h, 0, 0))],
        out_specs=[pl.BlockSpec((TB, FOX_DH), lambda b, h, i: (b * NQ + i, h)),
                   pl.BlockSpec((1, 1, TB, 1), lambda b, h, i: (b, h, i, 0))],
        out_shape=[jax.ShapeDtypeStruct((T, D), BF16),
                   jax.ShapeDtypeStruct((B, FOX_H, Lp, 1), F32)],
        compiler_params=_cp(("arbitrary", "arbitrary", "arbitrary")),
    )(qkv, qkv, qkv, cumT)


def _fox_bwd(qkv, dy, o, lse, cumT, *, B, Lp):
    TB = _fox_tb(Lp)
    NQ = Lp // TB
    T = B * Lp
    scale = FOX_DH ** -0.5

    def body(q_ref, k_ref, v_ref, dy_ref, o_ref, lse_ref, ct_ref, dq_ref, dk_ref, dv_ref, dck_ref, dcq_ref, dl_s):
        j = pl.program_id(2)
        k = k_ref[...]
        v = v_ref[...]
        ck = ct_ref[0]

        @pl.when(j == 0)
        def _():
            dq_ref[...] = jnp.zeros_like(dq_ref)
            dcq_ref[...] = jnp.zeros_like(dcq_ref)
            for i in range(NQ):
                sl = slice(i * TB, (i + 1) * TB)
                dl_s[sl, :] = jnp.sum(dy_ref[sl, :].astype(F32) * o_ref[sl, :].astype(F32), axis=1, keepdims=True)

        def block(i, nb, carry, diag):
            dk, dv, dck = carry
            off = pl.multiple_of(i * TB, TB)
            rows = pl.ds(off, nb * TB)
            q = q_ref[rows, :]
            dob = dy_ref[rows, :].astype(BF16)
            e = _nt(q, k) * scale - ck - lse_ref[0, 0, rows, :]
            if diag:
                e = jnp.where(_fox_tril(TB), e, NEG)
            p = jnp.exp(e)
            dv = dv + _tn(p.astype(BF16), dob)
            ds = p * (_nt(dob, v) - dl_s[rows, :])
            dsb = ds.astype(BF16)
            dk = dk + _tn(dsb, q)
            dq_ref[rows, :] += _nn(dsb, k) * scale
            dcq_ref[0, 0, rows, :] += jnp.sum(ds, axis=1, keepdims=True)
            dck = dck - jnp.sum(ds, axis=0, keepdims=True)
            return dk, dv, dck

        z = jnp.zeros((TB, FOX_DH), F32)
        carry = block(j, 1, (z, z, jnp.zeros((1, TB), F32)), True)
        npair = (NQ - 1 - j) // 2
        carry = lax.fori_loop(0, npair, lambda t, c: block(j + 1 + 2 * t, 2, c, False), carry)
        dk, dv, dck = lax.fori_loop(j + 1 + 2 * npair, NQ, lambda i, c: block(i, 1, c, False), carry)
        dk_ref[...] = (dk * scale).astype(dk_ref.dtype)
        dv_ref[...] = dv.astype(dv_ref.dtype)
        dck_ref[0] = dck

    head = lambda b, h, j: (b, h)
    return pl.pallas_call(
        body, name="fox_bwd", grid=(B, FOX_H, NQ),
        in_specs=[pl.BlockSpec((Lp, FOX_DH), head),
                  pl.BlockSpec((TB, FOX_DH), lambda b, h, j: (b * NQ + j, FOX_H + h)),
                  pl.BlockSpec((TB, FOX_DH), lambda b, h, j: (b * NQ + j, 2 * FOX_H + h)),
                  pl.BlockSpec((Lp, FOX_DH), head),
                  pl.BlockSpec((Lp, FOX_DH), head),
                  pl.BlockSpec((1, 1, Lp, 1), lambda b, h, j: (b, h, 0, 0)),
                  pl.BlockSpec((1, 1, TB), lambda b, h, j: (b * FOX_H + h, 0, j))],
        out_specs=[pl.BlockSpec((Lp, FOX_DH), head),
                   pl.BlockSpec((TB, FOX_DH), lambda b, h, j: (b * NQ + j, h)),
                   pl.BlockSpec((TB, FOX_DH), lambda b, h, j: (b * NQ + j, h)),
                   pl.BlockSpec((1, 1, TB), lambda b, h, j: (b * FOX_H + h, 0, j)),
                   pl.BlockSpec((1, 1, Lp, 1), lambda b, h, j: (b, h, 0, 0))],
        out_shape=[jax.ShapeDtypeStruct((T, D), F32),
                   jax.ShapeDtypeStruct((T, D), BF16),
                   jax.ShapeDtypeStruct((T, D), BF16),
                   jax.ShapeDtypeStruct((B * FOX_H, 1, Lp), F32),
                   jax.ShapeDtypeStruct((B, FOX_H, Lp, 1), F32)],
        scratch_shapes=[pltpu.VMEM((Lp, 1), F32)],
        compiler_params=_cp(("arbitrary", "arbitrary", "arbitrary")),
    )(qkv, qkv, qkv, dy, o, lse, cumT)


S5_TILE = 8


def _s5_pows(lam_ref, pw, tab, reverse):
    lr = lam_ref[0, :, 0:S5_W]
    li = lam_ref[0, :, S5_W:2 * S5_W]
    if reverse:
        li = -li
    ar, ai = lr, li
    sub = lax.broadcasted_iota(jnp.int32, (S5_TILE, 1), 0)
    for k, s in enumerate((1, 2, 4)):
        keep = (sub < S5_TILE - s) if reverse else (sub >= s)
        pw[k * S5_TILE:(k + 1) * S5_TILE, 0:S5_W] = jnp.where(keep, ar, 0.0)
        pw[k * S5_TILE:(k + 1) * S5_TILE, S5_W:2 * S5_W] = jnp.where(keep, ai, 0.0)
        ar, ai = ar * ar - ai * ai, 2.0 * ar * ai
    ar, ai = lr, li
    for r in range(S5_TILE):
        row = (S5_TILE - 1 - r) if reverse else r
        tab[row:row + 1, 0:S5_W] = ar
        tab[row:row + 1, S5_W:2 * S5_W] = ai
        ar, ai = ar * lr - ai * li, ar * li + ai * lr


def _s5_scan(hs, pw, tab, carry, reverse):
    n = hs.shape[0]
    tiles = list(range(n // S5_TILE))
    if reverse:
        tiles.reverse()
    for t in tiles:
        lo = t * S5_TILE
        vr = hs[lo:lo + S5_TILE, 0:S5_W]
        vi = hs[lo:lo + S5_TILE, S5_W:2 * S5_W]
        for k, s in enumerate((1, 2, 4)):
            sh = (S5_TILE - s) if reverse else s
            sr = pltpu.roll(vr, sh, 0)
            si = pltpu.roll(vi, sh, 0)
            ar = pw[k * S5_TILE:(k + 1) * S5_TILE, 0:S5_W]
            ai = pw[k * S5_TILE:(k + 1) * S5_TILE, S5_W:2 * S5_W]
            vr, vi = vr + ar * sr - ai * si, vi + ar * si + ai * sr
        hs[lo:lo + S5_TILE, 0:S5_W] = vr
        hs[lo:lo + S5_TILE, S5_W:2 * S5_W] = vi
    e8 = 0 if reverse else S5_TILE - 1
    l8r = tab[e8:e8 + 1, 0:S5_W]
    l8i = tab[e8:e8 + 1, S5_W:2 * S5_W]
    cr = carry[:, 0:S5_W]
    ci = carry[:, S5_W:2 * S5_W]
    states = []
    for t in tiles:
        states.append((cr, ci))
        edge = t * S5_TILE + e8
        er = hs[edge:edge + 1, 0:S5_W]
        ei = hs[edge:edge + 1, S5_W:2 * S5_W]
        cr, ci = er + l8r * cr - l8i * ci, ei + l8r * ci + l8i * cr
    carry[:, 0:S5_W] = cr
    carry[:, S5_W:2 * S5_W] = ci
    tr = tab[:, 0:S5_W]
    ti = tab[:, S5_W:2 * S5_W]
    for t, (cr, ci) in zip(tiles, states):
        lo = t * S5_TILE
        hs[lo:lo + S5_TILE, 0:S5_W] += tr * cr - ti * ci
        hs[lo:lo + S5_TILE, S5_W:2 * S5_W] += tr * ci + ti * cr


def _s5_rows(NC):
    return 3 * CH if (NC % 3 == 0 and NC > 3) else CH


def _s5_fwd(u, Bsg, Csg, lam, *, B, NC):
    T = B * NC * CH
    R = _s5_rows(NC)
    NR = NC * CH // R

    def body(u_ref, b_ref, c_ref, lam_ref, y_ref, h_ref, pw, tab, hs, carry):
        cidx = pl.program_id(2)

        @pl.when(cidx == 0)
        def _():
            _s5_pows(lam_ref, pw, tab, False)
            carry[...] = jnp.zeros_like(carry)

        hs[...] = _nn(u_ref[...].astype(BF16), b_ref[0])
        _s5_scan(hs, pw, tab, carry, False)
        hb = hs[...].astype(BF16)
        h_ref[...] = hb
        y_ref[...] = _nn(hb, c_ref[0])

    return pl.pallas_call(
        body, name="s5_fwd", grid=(B, S5_G // S5_SG, NR),
        in_specs=[pl.BlockSpec((R, CH), lambda b, s, c: (b * NR + c, s)),
                  pl.BlockSpec((1, CH, 2 * S5_W), lambda b, s, c: (s, 0, 0)),
                  pl.BlockSpec((1, 2 * S5_W, CH), lambda b, s, c: (s, 0, 0)),
                  pl.BlockSpec((1, 1, 2 * S5_W), lambda b, s, c: (s, 0, 0))],
        out_specs=[pl.BlockSpec((R, CH), lambda b, s, c: (b * NR + c, s)),
                   pl.BlockSpec((R, 2 * S5_W), lambda b, s, c: (b * NR + c, s))],
        out_shape=[jax.ShapeDtypeStruct((T, D), F32),
                   jax.ShapeDtypeStruct((T, (S5_G // S5_SG) * 2 * S5_W), BF16)],
        scratch_shapes=[pltpu.VMEM((3 * S5_TILE, 2 * S5_W), F32), pltpu.VMEM((S5_TILE, 2 * S5_W), F32),
                        pltpu.VMEM((R, 2 * S5_W), F32), pltpu.VMEM((1, 2 * S5_W), F32)],
        compiler_params=_cp(("arbitrary", "arbitrary", "arbitrary")),
    )(u, Bsg, Csg, lam)


def _s5_bwd(u, hst, dy, du_skip, Bsg, Csg, lam, *, B, NC):
    T = B * NC * CH
    NS = S5_G // S5_SG
    R = _s5_rows(NC)
    NR = NC * CH // R
    hb16 = R // 16

    def rix(b, c):
        return b * NR + (NR - 1 - c)

    def body(u_ref, h_ref, hp_ref, dy_ref, sk_ref, b_ref, c_ref, lam_ref,
             du_ref, db_ref, dc_ref, dl_ref, pw, tab, gs, carry):
        bidx = pl.program_id(1)
        cidx = pl.program_id(2)
        first = (bidx == 0) & (cidx == 0)

        @pl.when(cidx == 0)
        def _():
            _s5_pows(lam_ref, pw, tab, True)
            carry[...] = jnp.zeros_like(carry)

        dyb = dy_ref[...].astype(BF16)
        gs[...] = _nt(dyb, c_ref[0])
        _s5_scan(gs, pw, tab, carry, True)
        gr = gs[:, 0:S5_W]
        gi = gs[:, S5_W:]
        gb = gs[...].astype(BF16)
        du_ref[...] = (_nt(gb, b_ref[0]) + sk_ref[...]).astype(du_ref.dtype)
        ub = u_ref[...].astype(BF16)
        hcur = h_ref[...]
        dB = _tn(ub, gb)
        dC = _tn(dyb, hcur)
        hf = hcur.astype(F32)
        row = lax.broadcasted_iota(jnp.int32, (R, 1), 0)
        prev_last = jnp.where(cidx < NR - 1, hp_ref[15:16, :].astype(F32), 0.0)
        hprev = jnp.where(row == 0, prev_last, pltpu.roll(hf, 1, 0))
        pr = hprev[:, 0:S5_W]
        pi = hprev[:, S5_W:]
        da = _colsum(gr * pr + gi * pi)
        dbb = _colsum(gi * pr - gr * pi)
        dl = jnp.concatenate([da, dbb], axis=1)

        @pl.when(first)
        def _():
            db_ref[0] = dB
            dc_ref[0] = dC
            dl_ref[0] = dl

        @pl.when(jnp.logical_not(first))
        def _():
            db_ref[0] += dB
            dc_ref[0] += dC
            dl_ref[0] += dl

    return pl.pallas_call(
        body, name="s5_bwd", grid=(NS, B, NR),
        in_specs=[pl.BlockSpec((R, CH), lambda s, b, c: (rix(b, c), s)),
                  pl.BlockSpec((R, 2 * S5_W), lambda s, b, c: (rix(b, c), s)),
                  pl.BlockSpec((16, 2 * S5_W), lambda s, b, c: (jnp.maximum(rix(b, c) * hb16 - 1, 0), s)),
                  pl.BlockSpec((R, CH), lambda s, b, c: (rix(b, c), s)),
                  pl.BlockSpec((R, CH), lambda s, b, c: (rix(b, c), s)),
                  pl.BlockSpec((1, CH, 2 * S5_W), lambda s, b, c: (s, 0, 0)),
                  pl.BlockSpec((1, 2 * S5_W, CH), lambda s, b, c: (s, 0, 0)),
                  pl.BlockSpec((1, 1, 2 * S5_W), lambda s, b, c: (s, 0, 0))],
        out_specs=[pl.BlockSpec((R, CH), lambda s, b, c: (rix(b, c), s)),
                   pl.BlockSpec((1, CH, 2 * S5_W), lambda s, b, c: (s, 0, 0)),
                   pl.BlockSpec((1, CH, 2 * S5_W), lambda s, b, c: (s, 0, 0)),
                   pl.BlockSpec((1, 1, 2 * S5_W), lambda s, b, c: (s, 0, 0))],
        out_shape=[jax.ShapeDtypeStruct((T, D), BF16),
                   jax.ShapeDtypeStruct((NS, CH, 2 * S5_W), F32),
                   jax.ShapeDtypeStruct((NS, CH, 2 * S5_W), F32),
                   jax.ShapeDtypeStruct((NS, 1, 2 * S5_W), F32)],
        scratch_shapes=[pltpu.VMEM((3 * S5_TILE, 2 * S5_W), F32), pltpu.VMEM((S5_TILE, 2 * S5_W), F32),
                        pltpu.VMEM((R, 2 * S5_W), F32), pltpu.VMEM((1, 2 * S5_W), F32)],
        compiler_params=_cp(("arbitrary", "arbitrary", "arbitrary")),
    )(u, hst, hst, dy, du_skip, Bsg, Csg, lam)


def _s5_param_fn(lre, lim, lstep, bre, bim):
    step = jnp.exp(lstep)
    zr = lre * step
    zi = lim * step
    e = jnp.exp(zr)
    a = e * jnp.cos(zi)
    b = e * jnp.sin(zi)
    den = lre * lre + lim * lim
    qr = ((a - 1.0) * lre + b * lim) / den
    qi = (b * lre - (a - 1.0) * lim) / den
    return a, b, qr[None] * bre - qi[None] * bim, qr[None] * bim + qi[None] * bre


_S5_ROWS = S5_G * S5_P // CH


def _s5_tile(v):
    return v.reshape(_S5_ROWS, CH)


def _s5_tile_b(v):
    return v.reshape(S5_G * S5_P, S5_C).T.reshape(S5_C, _S5_ROWS, CH)


def _s5_untile_b(v):
    return v.reshape(S5_C, S5_G * S5_P).T.reshape(S5_G, S5_P, S5_C)


def _s5_params(lre, lim, lstep, bre, bim):
    def body(a_ref, b_ref, c_ref, d_ref, e_ref, o1, o2, o3, o4):
        outs = _s5_param_fn(a_ref[...], b_ref[...], c_ref[...], d_ref[...], e_ref[...])
        for o, v in zip((o1, o2, o3, o4), outs):
            o[...] = v

    shp = [jax.ShapeDtypeStruct(lre.shape, F32)] * 2 + [jax.ShapeDtypeStruct(bre.shape, F32)] * 2
    return pl.pallas_call(body, name="s5_params", out_shape=shp, compiler_params=_cp())(lre, lim, lstep, bre, bim)


def _s5_params_bwd(lre, lim, lstep, bre, bim, da, db, dbr, dbi):
    def body(a_ref, b_ref, c_ref, d_ref, e_ref, g1, g2, g3, g4, o1, o2, o3, o4, o5):
        _, vjp = jax.vjp(_s5_param_fn, a_ref[...], b_ref[...], c_ref[...], d_ref[...], e_ref[...])
        outs = vjp((g1[...], g2[...], g3[...], g4[...]))
        for o, v in zip((o1, o2, o3, o4, o5), outs):
            o[...] = v

    shp = [jax.ShapeDtypeStruct(lre.shape, F32)] * 3 + [jax.ShapeDtypeStruct(bre.shape, F32)] * 2
    return pl.pallas_call(body, name="s5_params_bwd", out_shape=shp, compiler_params=_cp())(
        lre, lim, lstep, bre, bim, da, db, dbr, dbi)


def _s5_blockdiag(br, bi, cre, cim):
    NS = S5_G // S5_SG
    eye = jnp.eye(S5_SG, dtype=F32)

    def bmat(v):
        v = v.reshape(NS, S5_SG, S5_P, S5_C)
        m = jnp.einsum("sgpc,gh->sgchp", v, eye)
        return m.reshape(NS, S5_SG * S5_C, S5_SG * S5_P)

    def cmat(v):
        v = v.reshape(NS, S5_SG, S5_C, S5_P)
        m = jnp.einsum("sgcp,gh->sgphc", v, eye)
        return m.reshape(NS, S5_SG * S5_P, S5_SG * S5_C)

    Bsg = jnp.concatenate([bmat(br), bmat(bi)], axis=2).astype(BF16)
    Csg = jnp.concatenate([cmat(cre), cmat(-cim)], axis=1).astype(BF16)
    return Bsg, Csg


def _s5_unblock(dBsg, dCsgT):
    NS = S5_G // S5_SG

    def diag(m):
        m = m.reshape(NS, S5_SG, S5_C, S5_SG, S5_P)
        return jnp.stack([m[:, g, :, g, :] for g in range(S5_SG)], axis=1)

    def ub(m):
        return diag(m).transpose(0, 1, 3, 2).reshape(S5_G, S5_P, S5_C)

    def uc(m):
        return diag(m).reshape(S5_G, S5_C, S5_P)

    dbr = ub(dBsg[:, :, 0:S5_W])
    dbi = ub(dBsg[:, :, S5_W:])
    dcr = uc(dCsgT[:, :, 0:S5_W])
    dci = -uc(dCsgT[:, :, S5_W:])
    return dbr, dbi, dcr, dci


def _loss_head(x, nf, target, *, B, NC, S):
    T = B * NC * CH
    nts = S // CH

    def f(xv, w, t):
        y = _rms(xv, w)
        return 0.5 * _colsum(jnp.mean(jnp.square(y - t), axis=-1, keepdims=True))

    def body(x_ref, w_ref, t_ref, dx_ref, ls_ref, dw_ref):
        i = pl.program_id(0)
        on = (i % NC) > 0
        t = t_ref[...]
        l, vjp = jax.vjp(lambda a, b: f(a, b, t), x_ref[...], w_ref[...])
        dx, dw = vjp(jnp.ones((1, 1), F32))
        g = jnp.where(on, 1.0, 0.0)
        dx_ref[...] = dx * g
        lv = jnp.zeros((1, CH), F32) + l * g

        @pl.when(i == 0)
        def _():
            ls_ref[...] = lv
            dw_ref[...] = dw * g

        @pl.when(i > 0)
        def _():
            ls_ref[...] += lv
            dw_ref[...] += dw * g

    def tix(i):
        return ((i // NC) * nts + jnp.maximum(i % NC - 1, 0), 0)

    return pl.pallas_call(
        body, name="loss_head", grid=(B * NC,),
        in_specs=[pl.BlockSpec((CH, D), lambda i: (i, 0)),
                  pl.BlockSpec((1, D), lambda i: (0, 0)),
                  pl.BlockSpec((CH, D), tix)],
        out_specs=[pl.BlockSpec((CH, D), lambda i: (i, 0)),
                   pl.BlockSpec((1, CH), lambda i: (0, 0)),
                   pl.BlockSpec((1, D), lambda i: (0, 0))],
        out_shape=[jax.ShapeDtypeStruct((T, D), F32),
                   jax.ShapeDtypeStruct((1, CH), F32),
                   jax.ShapeDtypeStruct((1, D), F32)],
        compiler_params=_cp(("arbitrary",)),
    )(x, nf, target)


def _ew(fn, ins, n_out, out_dtypes, *, name, tile=None):
    R, C = ins[0].shape
    tile = tile or _pick(R, (512, 256, 128, 64, 32, 16, 8, 1))
    if tile % 8 != 0:
        tile = R

    def body(*refs):
        outs = fn(*[r[...] for r in refs[:len(ins)]])
        if not isinstance(outs, (tuple, list)):
            outs = (outs,)
        for r, v in zip(refs[len(ins):], outs):
            r[...] = v.astype(r.dtype)

    spec = pl.BlockSpec((tile, C), lambda i: (i, 0))
    res = pl.pallas_call(
        body, name=name, grid=(R // tile,), in_specs=[spec] * len(ins), out_specs=[spec] * n_out,
        out_shape=[jax.ShapeDtypeStruct((R, C), dt) for dt in out_dtypes],
        compiler_params=_cp(("parallel",)),
    )(*ins)
    return res


def _adam_fn(w, g, m, v):
    m = ADAM_B1 * m + (1.0 - ADAM_B1) * g
    v = ADAM_B2 * v + (1.0 - ADAM_B2) * jnp.square(g)
    m_hat = m / (1.0 - ADAM_B1 ** ADAM_STEP)
    v_hat = v / (1.0 - ADAM_B2 ** ADAM_STEP)
    delta = -ADAM_LR * (m_hat / (jnp.sqrt(v_hat) + ADAM_EPS) + ADAM_WD * w)
    return delta, m, v


def _adam(w, g, m, v, name):
    shp = w.shape
    C = shp[-1]
    f = lambda a: a.reshape(-1, C)
    d, nm, nv = _ew(_adam_fn, [f(w), f(g), f(m), f(v)], 3, [F32] * 3, name=name)
    return d.reshape(shp), nm.reshape(shp), nv.reshape(shp)


def _me():
    return lax.axis_index("x"), lax.axis_index("y"), lax.axis_index("c")


def _all_gather(v, name):
    def body(x_ref, out_ref, send_sems, recv_sems, local_sem):
        x, y, c = _me()
        me, sibling = (x, y, c), (x, y, 1 - c)
        chips = [(1 - x, y), (x, 1 - y), (1 - x, 1 - y)]

        def slot(px, py, pc):
            return out_ref.at[4 * px + 2 * py + pc]

        def copy(k, block, to, src=None):
            return pltpu.make_async_remote_copy(
                src_ref=slot(*block) if src is None else src, dst_ref=slot(*block),
                send_sem=send_sems.at[k], recv_sem=recv_sems.at[k], device_id=to, device_id_type=MESH)

        mine = pltpu.make_async_copy(x_ref, slot(*me), local_sem)
        mine.start()
        first = [copy(0, me, sibling, src=x_ref)]
        first += [copy(1 + j, me, (*chip, c), src=x_ref) for j, chip in enumerate(chips)]
        for cp in first:
            cp.start()
        passed = [copy(4 + j, (*chip, c), sibling) for j, chip in enumerate(chips)]
        for j, chip in enumerate(chips):
            copy(1 + j, (*chip, c), me).wait_recv()
            passed[j].start()
        copy(0, sibling, me).wait_recv()
        for j, chip in enumerate(chips):
            copy(4 + j, (*chip, 1 - c), me).wait_recv()
        for cp in first + passed:
            cp.wait_send()
        mine.wait()

    return pl.pallas_call(
        body, name=name, out_shape=jax.ShapeDtypeStruct((N_DEV,) + v.shape, v.dtype),
        in_specs=[pl.BlockSpec(memory_space=pl.ANY)], out_specs=pl.BlockSpec(memory_space=pl.ANY),
        scratch_shapes=[pltpu.SemaphoreType.DMA((7,)), pltpu.SemaphoreType.DMA((7,)), pltpu.SemaphoreType.DMA],
    )(v)


def _swap_core(g, name):
    def body(g_ref, out_ref, send_sems, recv_sems):
        x, y, c = _me()
        cps = [pltpu.make_async_remote_copy(
            src_ref=g_ref.at[q, 1 - c], dst_ref=out_ref.at[q], send_sem=send_sems.at[q], recv_sem=recv_sems.at[q],
            device_id=(x, y, 1 - c), device_id_type=MESH) for q in range(4)]
        for cp in cps:
            cp.start()
        for cp in cps:
            cp.wait()

    return pl.pallas_call(
        body, name=name, out_shape=jax.ShapeDtypeStruct((4,) + g.shape[2:], g.dtype),
        in_specs=[pl.BlockSpec(memory_space=pl.ANY)], out_specs=pl.BlockSpec(memory_space=pl.ANY),
        scratch_shapes=[pltpu.SemaphoreType.DMA((4,)), pltpu.SemaphoreType.DMA((4,))],
    )(g)


def _swap_chips(hb, name):
    flips = [(1, 0), (0, 1), (1, 1)]

    def body(h_ref, out_ref, send_sems, recv_sems):
        x, y, c = _me()
        cps = []
        for j, (fx, fy) in enumerate(flips):
            px = x + fx - 2 * x * fx
            py = y + fy - 2 * y * fy
            cps.append(pltpu.make_async_remote_copy(
                src_ref=h_ref.at[2 * px + py], dst_ref=out_ref.at[j], send_sem=send_sems.at[j],
                recv_sem=recv_sems.at[j], device_id=(px, py, c), device_id_type=MESH))
        for cp in cps:
            cp.start()
        for cp in cps:
            cp.wait()

    return pl.pallas_call(
        body, name=name, out_shape=jax.ShapeDtypeStruct((3,) + hb.shape[1:], hb.dtype),
        in_specs=[pl.BlockSpec(memory_space=pl.ANY)], out_specs=pl.BlockSpec(memory_space=pl.ANY),
        scratch_shapes=[pltpu.SemaphoreType.DMA((3,)), pltpu.SemaphoreType.DMA((3,))],
    )(hb)


def _reduce_scatter(g8, tag):
    shard = g8.shape[1:]
    C = shard[-1]
    x, y, c = _me()
    g4 = g8.reshape((4, 2) + shard)
    got = _swap_core(g4, "rs_core_" + tag)
    mine = lax.dynamic_index_in_dim(g4, c, axis=1, keepdims=False)
    h, hb = _ew(lambda a, b: (a + b, a + b), [mine.reshape(-1, C), got.reshape(-1, C)], 2, [F32, BF16],
                name="rs_pair_sum_" + tag)
    own = lax.dynamic_index_in_dim(h.reshape((4,) + shard), 2 * x + y, axis=0, keepdims=False)
    got3 = _swap_chips(hb.reshape((4,) + shard), "rs_chips_" + tag)
    out, = _ew(lambda a, b, c_, d: a + b.astype(F32) + c_.astype(F32) + d.astype(F32),
               [own.reshape(-1, C)] + [got3[j].reshape(-1, C) for j in range(3)], 1, [F32],
               name="rs_chip_sum_" + tag)
    return out.reshape(shard)


def _sum8(a):
    out, = _ew(lambda *v: functools.reduce(lambda p, q: p + q, v), [a[k] for k in range(N_DEV)], 1, [F32],
               name="sum8")
    return out


def _pad_rows(flat, cols, mult):
    n = flat.shape[0]
    per = cols * mult
    tot = ((n + per - 1) // per) * per
    return jnp.pad(flat, (0, tot - n)).reshape(-1, cols)


class _Packer:
    def __init__(self, shapes, mult):
        self.shapes = shapes
        self.sizes = [int(np.prod(s)) for s in shapes]
        self.rows = [8 * ((n + 8 * D - 1) // (8 * D)) for n in self.sizes]
        tot = sum(self.rows)
        self.tail = (-tot) % mult

    def pack(self, arrs, dtype):
        parts = [_pad_rows(a.reshape(-1).astype(dtype), D, 8) for a in arrs]
        if self.tail:
            parts.append(jnp.zeros((self.tail, D), dtype))
        return jnp.concatenate(parts, axis=0)

    def unpack(self, buf):
        out, o = [], 0
        for s, n, r in zip(self.shapes, self.sizes, self.rows):
            out.append(buf[o:o + r].reshape(-1)[:n].reshape(s))
            o += r
        return out


def _w_cat(w_in_l):
    sm = jnp.concatenate([w_in_l[:, O_DT:O_DT + SSD_H], w_in_l[:, O_F:O_F + FOX_H],
                          jnp.zeros((D, CH - SSD_H - FOX_H), w_in_l.dtype)], axis=1)
    return jnp.concatenate([w_in_l[:, O_Z:O_XBC], w_in_l[:, O_XBC:O_DT], w_in_l[:, O_QKV:O_F],
                            w_in_l[:, O_U:O_G], w_in_l[:, O_G:D_IN], sm], axis=1)


def _w_uncat(g):
    return jnp.concatenate([g[:, OFF_Z:OFF_XBC], g[:, OFF_XBC:OFF_QKV], g[:, OFF_SM:OFF_SM + SSD_H],
                            g[:, OFF_QKV:OFF_U], g[:, OFF_SM + SSD_H:OFF_SM + SSD_H + FOX_H],
                            g[:, OFF_U:OFF_G], g[:, OFF_G:OFF_SM]], axis=1)


def _layer_fwd(x, p, geom, dims):
    B, NC, Lp, pad = dims
    T = geom[0]
    rm = functools.partial(_rowmap, geom=geom)
    sv = {}
    xn1, = rm(_f_norm, [x], [p["norm1"]], [(D, BF16)], [], tile=384 if Lp % 384 == 0 else CH, name="norm1")
    Wc = p["w_cat"]
    pz = _mm(xn1, Wc, "nn", BF16, n=D, b_off=OFF_Z, name="in_z")
    pxbc = _mm(xn1, Wc, "nn", F32, n=CONV_DIM, b_off=OFF_XBC, name="in_xbc")
    qkv = _mm(xn1, Wc, "nn", BF16, n=3 * D, b_off=OFF_QKV, name="in_qkv")
    pu = _mm(xn1, Wc, "nn", BF16, n=D, b_off=OFF_U, name="in_u")
    pg = _mm(xn1, Wc, "nn", BF16, n=3 * D, b_off=OFF_G, name="in_g")
    psm = _mm(xn1, Wc, "nn", F32, n=CH, b_off=OFF_SM, name="in_sm")
    t_r = 384 if Lp % 384 == 0 else CH
    sm, = rm(_smallact, [psm], [p["smallbias"]], [(CH, F32)], [], tile=t_r, name="smallact")
    xbc = _conv_fwd(pxbc, p["conv_w"], p["conv_b"], geom=geom, tile=t_r)
    y_ssd, states = _ssd_fwd(xbc, sm, p["a_log"], B=B, NC=NC)
    y_a, = rm(_ssd_post, [y_ssd, (xbc, D, 0), pz], [p["d_rep"], p["ssd_norm"]], [(D, BF16)], [], tile=t_r,
              name="ssd_post")
    cum = _cumsum_seq(sm, B=B, NC=NC, reverse=False, name="fox_cum")
    cumT = _fox_keybias(cum, B=B, Lp=Lp, pad=pad)
    y_b, lse = _fox_fwd(qkv, cumT, B=B, Lp=Lp)
    y_ssm, hst = _s5_fwd(pu, p["Bsg"], p["Csg"], p["lam"], B=B, NC=NC)
    y1, = rm(_s5_pre, [y_ssm, pu], [p["s5_d"]], [(D, BF16)], [], tile=t_r, name="s5_pre")
    tg = _mm(y1, p["w_glu"], "nn", BF16, name="s5_glu_mm")
    y_c, = rm(_s5_glu, [y_ssm, pu, tg], [p["s5_d"]], [(D, BF16)], [], tile=t_r, name="s5_glu")
    br = [_mm(yy, p["w_branch"][n], "nn", BF16, name=f"branch{n}") for n, yy in enumerate((y_a, y_b, y_c))]
    mix, = rm(_merge, [(pg, D, 0), (pg, D, 1), (pg, D, 2)] + br, [], [(D, BF16)], [], tile=t_r, name="merge")
    x_mid = _mm(mix, p["w_out"], "nn", F32, res=x, name="out_proj")
    xn2, = rm(_f_norm, [x_mid], [p["norm2"]], [(D, BF16)], [], tile=t_r, name="norm2")
    hff = _mm(xn2, p["w_ffn_in"], "nn", BF16, name="ffn_in")
    act, = rm(_swiglu, [(hff, DFF, 0), (hff, DFF, 1)], [], [(DFF, BF16)], [], tile=t_r, name="swiglu")
    x_out = _mm(act, p["w_ffn_out"], "nn", F32, res=x_mid, name="ffn_out")
    sv.update(x=x, xn1=xn1, pz=pz, pxbc=pxbc, qkv=qkv, pu=pu, pg=pg, psm=psm, sm=sm, xbc=xbc, y_ssd=y_ssd,
              states=states, y_a=y_a, cum=cum, cumT=cumT, y_b=y_b, lse=lse, y_ssm=y_ssm, hst=hst, y1=y1, tg=tg,
              y_c=y_c, br=br, mix=mix, x_mid=x_mid, xn2=xn2, hff=hff, act=act)
    return x_out, sv


def _layer_bwd(dx_out, p, sv, geom, dims):
    B, NC, Lp, pad = dims
    T = geom[0]
    rm = functools.partial(_rowmap, geom=geom)
    t_r = 384 if Lp % 384 == 0 else CH
    g = {}
    dact = _mm(dx_out, p["w_ffn_out"], "nt", BF16, name="ffn_out_dx")
    g["w_ffn_out"] = _mm(sv["act"], dx_out, "tn", F32, name="ffn_out_dw")
    dhff, = rm(_b_swiglu, [(sv["hff"], DFF, 0), (sv["hff"], DFF, 1), dact], [], [(2 * DFF, BF16)], [], tile=t_r,
               name="swiglu_bwd")
    dxn2 = _mm(dhff, p["w_ffn_in"], "nt", F32, name="ffn_in_dx")
    g["w_ffn_in"] = _mm(sv["xn2"], dhff, "tn", F32, name="ffn_in_dw")
    dx_mid, g["norm2"] = rm(_b_norm, [sv["x_mid"], dxn2, dx_out], [p["norm2"]], [(D, F32)], [(1, D)], tile=t_r,
                            name="norm2_bwd")
    dmix = _mm(dx_mid, p["w_out"], "nt", BF16, name="out_proj_dx")
    g["w_out"] = _mm(sv["mix"], dx_mid, "tn", F32, name="out_proj_dw")
    pg = sv["pg"]
    dpg, db0, db1, db2 = rm(_b_merge, [(pg, D, 0), (pg, D, 1), (pg, D, 2)] + sv["br"] + [dmix], [],
                            [(3 * D, BF16), (D, BF16), (D, BF16), (D, BF16)], [], tile=t_r, name="merge_bwd")
    ys = (sv["y_a"], sv["y_b"], sv["y_c"])
    dbs = (db0, db1, db2)
    g["w_branch"] = [_mm(ys[n], dbs[n], "tn", F32, name=f"branch{n}_dw") for n in range(3)]
    dy = [_mm(dbs[n], p["w_branch"][n], "nt", BF16, name=f"branch{n}_dx") for n in range(3)]
    dtg, dy1a = rm(_b_s5_glu, [sv["y_ssm"], sv["pu"], sv["tg"], dy[2]], [p["s5_d"]], [(D, BF16), (D, F32)], [],
                   tile=t_r, name="s5_glu_bwd")
    dy1b = _mm(dtg, p["w_glu"], "nt", BF16, name="s5_glu_mm_dx")
    g["w_glu"] = _mm(sv["y1"], dtg, "tn", F32, name="s5_glu_mm_dw")
    dys, du_skip, g["s5_d"] = rm(_b_s5_pre, [sv["y_ssm"], sv["pu"], dy1a, dy1b], [p["s5_d"]],
                                 [(D, F32), (D, F32)], [(1, D)], tile=t_r, name="s5_pre_bwd")
    du, g["Bsg"], g["Csg"], g["lam"] = _s5_bwd(sv["pu"], sv["hst"], dys, du_skip, p["Bsg"], p["Csg"], p["lam"],
                                               B=B, NC=NC)
    dq, dk, dv, dckT, dcq = _fox_bwd(sv["qkv"], dy[1], sv["y_b"], sv["lse"], sv["cumT"], B=B, Lp=Lp)
    dcum8 = dcq.reshape(B, FOX_H, Lp).transpose(0, 2, 1) + dckT.reshape(B, FOX_H, Lp).transpose(0, 2, 1)
    dcum = jnp.pad(dcum8.reshape(T, FOX_H), ((0, 0), (SSD_H, CH - SSD_H - FOX_H)))
    dlogf = _cumsum_seq(dcum, B=B, NC=NC, reverse=True, name="fox_cum_bwd")
    dy_ssd, dxs_skip, dz, g["d_rep"], g["ssd_norm"] = rm(
        _b_ssd_post, [sv["y_ssd"], (sv["xbc"], D, 0), sv["pz"], dy[0]], [p["d_rep"], p["ssd_norm"]],
        [(D, F32), (D, F32), (D, BF16)], [(1, D), (1, D)], tile=t_r, name="ssd_post_bwd")
    dxbc_act, ddt, g["a_log"] = _ssd_bwd(sv["xbc"], sv["sm"], p["a_log"], sv["states"], dy_ssd, dxs_skip, B=B, NC=NC)
    dpsm, g["smallbias"] = rm(_b_smallact, [sv["psm"], ddt, dlogf], [p["smallbias"]], [(CH, BF16)], [(1, CH)],
                              tile=t_r, name="smallact_bwd")
    dconv, g["conv_w"], g["conv_b"] = _conv_bwd_pre(sv["pxbc"], dxbc_act, p["conv_w"], p["conv_b"], geom=geom, tile=t_r)
    dpxbc = _conv_bwd_x(dconv, p["conv_w"], geom=geom, tile=t_r)
    dproj = jnp.concatenate([dz, dpxbc, dq.astype(BF16), dk, dv, du, dpg, dpsm], axis=1)
    dxn1 = _mm(dproj, p["w_cat"], "nt", F32, name="in_dx")
    g["w_cat"] = _mm(sv["xn1"], dproj, "tn", F32, name="in_dw")
    dx_in, g["norm1"] = rm(_b_norm, [sv["x"], dxn1, dx_mid], [p["norm1"]], [(D, F32)], [(1, D)], tile=t_r,
                           name="norm1_bwd")
    return dx_in, g


_BIG = ["w_in", "s5_w_glu", "w_branch", "w_out", "w_ffn_in", "w_ffn_out"]
_NAMES = ['meta', 'norm1', 'w_in', 'ssd_conv_w', 'ssd_conv_b', 'ssd_dt_bias', 'ssd_a_log', 'ssd_d', 'ssd_norm',
          'fox_bf', 's5_lam_re', 's5_lam_im', 's5_b_re', 's5_b_im', 's5_c_re', 's5_c_im', 's5_log_step', 's5_d',
          's5_w_glu', 'w_branch', 'w_out', 'norm2', 'w_ffn_in', 'w_ffn_out', 'norm_f']
_SHARD_AXIS = {"meta": 1, "ssd_conv_w": 2}
_BIG_AXIS = {"w_in": 2, "s5_w_glu": 1, "w_branch": 2, "w_out": 1, "w_ffn_in": 2, "w_ffn_out": 1}


def kernel(x, meta, norm1, w_in, ssd_conv_w, ssd_conv_b, ssd_dt_bias, ssd_a_log, ssd_d, ssd_norm, fox_bf, s5_lam_re, s5_lam_im, s5_b_re, s5_b_im, s5_c_re, s5_c_im, s5_log_step, s5_d, s5_w_glu, w_branch, w_out, norm2, w_ffn_in, w_ffn_out, norm_f, loss_target, m_meta, m_norm1, m_w_in, m_ssd_conv_w, m_ssd_conv_b, m_ssd_dt_bias, m_ssd_a_log, m_ssd_d, m_ssd_norm, m_fox_bf, m_s5_lam_re, m_s5_lam_im, m_s5_b_re, m_s5_b_im, m_s5_c_re, m_s5_c_im, m_s5_log_step, m_s5_d, m_s5_w_glu, m_w_branch, m_w_out, m_norm2, m_w_ffn_in, m_w_ffn_out, m_norm_f, v_meta, v_norm1, v_w_in, v_ssd_conv_w, v_ssd_conv_b, v_ssd_dt_bias, v_ssd_a_log, v_ssd_d, v_ssd_norm, v_fox_bf, v_s5_lam_re, v_s5_lam_im, v_s5_b_re, v_s5_b_im, v_s5_c_re, v_s5_c_im, v_s5_log_step, v_s5_d, v_s5_w_glu, v_w_branch, v_w_out, v_norm2, v_w_ffn_in, v_w_ffn_out, v_norm_f):
    args = locals()
    W = {n: args[n] for n in _NAMES}
    Mo = {n: args["m_" + n] for n in _NAMES}
    Vo = {n: args["v_" + n] for n in _NAMES}
    B, S, _ = x.shape
    depth = norm1.shape[0]
    L = S + N_META
    Lp = ((L + CH - 1) // CH) * CH
    pad = Lp - L
    assert pad + N_META == CH and S % CH == 0
    NC = Lp // CH
    T = B * Lp
    geom = (T, Lp, pad)
    dims = (B, NC, Lp, pad)
    xi, yi, ci = _me()
    dev = 4 * xi + 2 * yi + ci

    gath = {n: _all_gather(W[n].astype(BF16), "gather_" + n) for n in _BIG}
    full = {n: jnp.concatenate([gath[n][k] for k in range(N_DEV)], axis=_BIG_AXIS[n]) for n in _BIG}
    sm_pack = _Packer([meta.shape, ssd_conv_w.shape], 8)
    sm_g = _all_gather(sm_pack.pack([meta, ssd_conv_w], F32), "gather_small")
    sm_parts = [sm_pack.unpack(sm_g[k]) for k in range(N_DEV)]
    meta_full = jnp.concatenate([sm_parts[k][0] for k in range(N_DEV)], axis=1)
    conv_w_full = jnp.concatenate([sm_parts[k][1] for k in range(N_DEV)], axis=2)

    layers = []
    s5_in = []
    for l in range(depth):
        lre = _s5_tile(s5_lam_re[l])
        lim = _s5_tile(s5_lam_im[l])
        lst = _s5_tile(jnp.repeat(s5_log_step[l], S5_P))
        bre = _s5_tile_b(s5_b_re[l])
        bim = _s5_tile_b(s5_b_im[l])
        s5_in.append((lre, lim, lst, bre, bim))
        a, b, br_, bi_ = _s5_params(lre, lim, lst, bre, bim)
        Bsg, Csg = _s5_blockdiag(_s5_untile_b(br_), _s5_untile_b(bi_), s5_c_re[l], s5_c_im[l])
        NS = S5_G // S5_SG
        lam = jnp.concatenate([a.reshape(NS, 1, S5_W), b.reshape(NS, 1, S5_W)], axis=2)
        zpad = jnp.zeros((CH - SSD_H - FOX_H,), F32)
        layers.append(dict(
            norm1=norm1[l][None], w_cat=_w_cat(full["w_in"][l]),
            smallbias=jnp.concatenate([ssd_dt_bias[l], fox_bf[l], zpad])[None],
            conv_w=conv_w_full[l], conv_b=ssd_conv_b[l][None],
            a_log=jnp.concatenate([ssd_a_log[l], jnp.zeros((CH - SSD_H,), F32)])[None],
            d_rep=jnp.repeat(ssd_d[l], SSD_P)[None], ssd_norm=ssd_norm[l][None],
            Bsg=Bsg, Csg=Csg, lam=lam, s5_d=s5_d[l][None], w_glu=full["s5_w_glu"][l],
            w_branch=[full["w_branch"][l, n] for n in range(3)], w_out=full["w_out"][l],
            norm2=norm2[l][None], w_ffn_in=full["w_ffn_in"][l], w_ffn_out=full["w_ffn_out"][l]))

    xs = jnp.concatenate([jnp.zeros((B, pad, D), F32), jnp.broadcast_to(meta_full[None], (B, N_META, D)), x], axis=1)
    h = xs.reshape(T, D)
    saved = []
    for l in range(depth):
        h, sv = _layer_fwd(h, layers[l], geom, dims)
        saved.append(sv)
    dh, loss_row, g_nf = _loss_head(h, norm_f[None], loss_target.reshape(B * S, D), B=B, NC=NC, S=S)
    loss = lax.psum(loss_row[0, 0], AXES)

    G = {n: [None] * depth for n in _NAMES}
    for l in reversed(range(depth)):
        dh, g = _layer_bwd(dh, layers[l], saved[l], geom, dims)
        saved[l] = None
        G["norm1"][l] = g["norm1"][0]
        G["norm2"][l] = g["norm2"][0]
        G["w_in"][l] = _w_uncat(g["w_cat"])
        G["ssd_conv_w"][l] = g["conv_w"]
        G["ssd_conv_b"][l] = g["conv_b"][0]
        G["ssd_dt_bias"][l] = g["smallbias"][0, 0:SSD_H]
        G["fox_bf"][l] = g["smallbias"][0, SSD_H:SSD_H + FOX_H]
        G["ssd_a_log"][l] = g["a_log"][0, 0:SSD_H]
        G["ssd_d"][l] = g["d_rep"].reshape(SSD_H, SSD_P).sum(axis=1)
        G["ssd_norm"][l] = g["ssd_norm"][0]
        dbr, dbi, dcr, dci = _s5_unblock(g["Bsg"], g["Csg"])
        da = _s5_tile(g["lam"][:, 0, 0:S5_W])
        db = _s5_tile(g["lam"][:, 0, S5_W:])
        dlre, dlim, dlst, dbre, dbim = _s5_params_bwd(*s5_in[l], da, db, _s5_tile_b(dbr), _s5_tile_b(dbi))
        G["s5_lam_re"][l] = dlre.reshape(S5_G, S5_P)
        G["s5_lam_im"][l] = dlim.reshape(S5_G, S5_P)
        G["s5_log_step"][l] = dlst.reshape(S5_G, S5_P).sum(axis=1)
        G["s5_b_re"][l] = _s5_untile_b(dbre)
        G["s5_b_im"][l] = _s5_untile_b(dbim)
        G["s5_c_re"][l] = dcr
        G["s5_c_im"][l] = dci
        G["s5_d"][l] = g["s5_d"][0]
        G["s5_w_glu"][l] = g["w_glu"]
        G["w_branch"][l] = jnp.stack(g["w_branch"])
        G["w_out"][l] = g["w_out"]
        G["w_ffn_in"][l] = g["w_ffn_in"]
        G["w_ffn_out"][l] = g["w_ffn_out"]
    dxs = dh.reshape(B, Lp, D)
    grad_x = dxs[:, pad + N_META:, :]
    part = {n: jnp.stack(G[n]) for n in _NAMES if n not in ("meta", "norm_f")}
    part["meta"] = dxs[:, pad:pad + N_META, :].sum(axis=0)
    part["norm_f"] = g_nf[0]

    grads = {}
    for n in _BIG:
        ax = _BIG_AXIS[n]
        a = part[n]
        a = a.reshape(a.shape[:ax] + (N_DEV, a.shape[ax] // N_DEV) + a.shape[ax + 1:])
        grads[n] = _reduce_scatter(jnp.moveaxis(a, ax, 0), n)

    small = [n for n in _NAMES if n not in _BIG]
    sp = _Packer([part[n].shape for n in small], 128)
    tot = sp.unpack(_sum8(_all_gather(sp.pack([part[n] for n in small], F32), "gather_small_grads")))
    for n, t in zip(small, tot):
        if n in _SHARD_AXIS:
            ax = _SHARD_AXIS[n]
            w = W[n].shape[ax]
            t = lax.dynamic_slice_in_dim(t, dev * w, w, axis=ax)
        grads[n] = t

    delta, new_m, new_v = {}, {}, {}
    for n in _BIG:
        delta[n], new_m[n], new_v[n] = _adam(W[n], grads[n], Mo[n], Vo[n], "adam_" + n)
    ap = _Packer([W[n].shape for n in small], 128)
    d_, m_, v_ = _adam(ap.pack([W[n] for n in small], F32), ap.pack([grads[n] for n in small], F32),
                       ap.pack([Mo[n] for n in small], F32), ap.pack([Vo[n] for n in small], F32), "adam_small")
    for n, a, b, c in zip(small, ap.unpack(d_), ap.unpack(m_), ap.unpack(v_)):
        delta[n], new_m[n], new_v[n] = a, b, c
    return (loss, grad_x, *[grads[n] for n in _NAMES], *[delta[n] for n in _NAMES],
            *[new_m[n] for n in _NAMES], *[new_v[n] for n in _NAMES])
```

```python
import functools
import math

import numpy as np
import jax
import jax.numpy as jnp
from jax import lax
from jax.experimental import pallas as pl
from jax.experimental.pallas import tpu as pltpu

F32 = jnp.float32
BF16 = jnp.bfloat16
AXES = ("x", "y", "c")
MESH = pl.DeviceIdType.MESH
N_DEV = 8

D = 1024
N_META = 16
CH = 128
EPS = 1e-6
NEG = -1e30
SSD_H, SSD_P, SSD_N, SSD_G = 16, 64, 128, 2
CONV_K, CONV_DIM = 4, 1536
FOX_H, FOX_DH = 8, 128
S5_G, S5_P, S5_C = 64, 64, 16
S5_SG = 8
S5_W = S5_SG * S5_P
DFF = 2816
D_IN = 9752
OFF_Z, OFF_XBC, OFF_QKV, OFF_U, OFF_G, OFF_SM, D_CAT = 0, 1024, 2560, 5632, 6656, 9728, 9856
O_Z, O_XBC, O_DT, O_QKV, O_F, O_U, O_G = 0, 1024, 2560, 2576, 5648, 5656, 6680

ADAM_LR, ADAM_B1, ADAM_B2, ADAM_EPS, ADAM_WD, ADAM_STEP = 0.001, 0.9, 0.999, 1e-08, 0.01, 10

VMEM_LIMIT_V7X = 52 * 1024 * 1024
HI = lax.Precision.HIGHEST


def _cp(sem=None):
    return pltpu.CompilerParams(dimension_semantics=sem, vmem_limit_bytes=VMEM_LIMIT_V7X)


def _pick(n, cands):
    for c in cands:
        if n % c == 0:
            return c
    raise ValueError(f"no tile for {n}")


_TILES = (1408, 1024, 896, 768, 512, 384, 256, 128)


def _mm(a, b, mode, out_dtype, *, name, n=None, b_off=0, res=None, tm=None, tn=None, tk=None):
    if mode == "tn":
        K, M = a.shape
    else:
        M, K = a.shape
    if mode == "nt":
        N = b.shape[0]
    else:
        N = n if n is not None else b.shape[1]
    tm = tm or _pick(M, (1408, 1024, 768, 512, 384, 256, 128, 64, 16, 8))
    tn = tn or _pick(math.gcd(N, b_off) if b_off else N, (1408, 1024, 896, 768, 512, 384, 256, 128))
    tk = tk or _pick(K, _TILES)
    nk = K // tk
    joff = b_off // tn

    def body(*refs):
        if res is None:
            a_ref, b_ref, o_ref, acc = refs
            r_ref = None
        else:
            a_ref, b_ref, r_ref, o_ref, acc = refs
        k = pl.program_id(2)
        av = a_ref[...].astype(BF16)
        bv = b_ref[...].astype(BF16)
        if mode == "nn":
            p = jnp.dot(av, bv, preferred_element_type=F32)
        elif mode == "nt":
            p = lax.dot_general(av, bv, (((1,), (1,)), ((), ())), preferred_element_type=F32)
        else:
            p = lax.dot_general(av, bv, (((0,), (0,)), ((), ())), preferred_element_type=F32)

        @pl.when(k == 0)
        def _():
            acc[...] = p

        @pl.when(k > 0)
        def _():
            acc[...] += p

        @pl.when(k == nk - 1)
        def _():
            r = acc[...]
            if r_ref is not None:
                r = r + r_ref[...]
            o_ref[...] = r.astype(o_ref.dtype)

    if mode == "tn":
        a_spec = pl.BlockSpec((tk, tm), lambda i, j, k: (k, i))
    else:
        a_spec = pl.BlockSpec((tm, tk), lambda i, j, k: (i, k))
    if mode == "nt":
        b_spec = pl.BlockSpec((tn, tk), lambda i, j, k: (j, k))
    else:
        b_spec = pl.BlockSpec((tk, tn), lambda i, j, k: (k, j + joff))
    o_spec = pl.BlockSpec((tm, tn), lambda i, j, k: (i, j))
    in_specs = [a_spec, b_spec] + ([o_spec] if res is not None else [])
    args = (a, b) + ((res,) if res is not None else ())
    return pl.pallas_call(
        body, name=name, grid=(M // tm, N // tn, nk),
        in_specs=in_specs, out_specs=o_spec,
        out_shape=jax.ShapeDtypeStruct((M, N), out_dtype),
        scratch_shapes=[pltpu.VMEM((tm, tn), F32)],
        compiler_params=_cp(("parallel", "parallel", "arbitrary")),
    )(*args)


def _rowmap(fn, row_ins, const_ins, row_outs, acc_outs, *, geom, tile, name):
    T, Lp, pad = geom
    assert Lp % tile == 0
    per_seq = Lp // tile
    specs, args = [], []
    for r in row_ins:
        arr, w, cb = r if isinstance(r, tuple) else (r, r.shape[1], 0)
        specs.append(pl.BlockSpec((tile, w), functools.partial(lambda i, cb: (i, cb), cb=cb)))
        args.append(arr)
    for c in const_ins:
        specs.append(pl.BlockSpec(c.shape, functools.partial(lambda i, nd: (0,) * nd, nd=c.ndim)))
        args.append(c)
    n_r, n_c, n_o, n_a = len(row_ins), len(const_ins), len(row_outs), len(acc_outs)
    out_specs = [pl.BlockSpec((tile, w), lambda i: (i, 0)) for w, _ in row_outs]
    out_specs += [pl.BlockSpec(s, lambda i: (0, 0)) for s in acc_outs]
    out_shape = [jax.ShapeDtypeStruct((T, w), dt) for w, dt in row_outs]
    out_shape += [jax.ShapeDtypeStruct(s, F32) for s in acc_outs]

    def body(*refs):
        i = pl.program_id(0)
        pos = (i % per_seq) * tile + lax.broadcasted_iota(jnp.int32, (tile, 1), 0)
        valid = pos >= pad
        vals = [r[...].astype(F32) for r in refs[:n_r]] + [r[...] for r in refs[n_r:n_r + n_c]]
        outs = fn(valid, *vals)
        if not isinstance(outs, (tuple, list)):
            outs = (outs,)
        orefs = refs[n_r + n_c:]
        for r, v in zip(orefs[:n_o], outs[:n_o]):
            r[...] = v.astype(r.dtype)
        for r, v in zip(orefs[n_o:], outs[n_o:]):
            @pl.when(i == 0)
            def _(r=r, v=v):
                r[...] = v

            @pl.when(i > 0)
            def _(r=r, v=v):
                r[...] += v

    res = pl.pallas_call(
        body, name=name, grid=(T // tile,), in_specs=specs, out_specs=out_specs, out_shape=out_shape,
        compiler_params=_cp(("arbitrary",)),
    )(*args)
    return res


def _sigmoid(x):
    return 1.0 / (1.0 + jnp.exp(-x))


def _silu(x):
    return x * _sigmoid(x)


def _softplus(x):
    return jnp.maximum(x, 0.0) + jnp.log(1.0 + jnp.exp(-jnp.abs(x)))


def _gelu(x):
    return 0.5 * x * (1.0 + jnp.tanh(math.sqrt(2.0 / math.pi) * (x + 0.044715 * x * x * x)))


def _rms(x, w):
    return x * lax.rsqrt(jnp.mean(x * x, axis=-1, keepdims=True) + EPS) * w


def _colsum(v):
    return jnp.sum(v, axis=0, keepdims=True)


def _f_norm(valid, x, w):
    return _rms(x, w)


def _b_norm(valid, x, dxn, dres, w):
    _, vjp = jax.vjp(_rms, x, w)
    dx, dw = vjp(dxn)
    return jnp.where(valid, dx + dres, 0.0), dw


def _smallact(valid, raw, bias):
    lane = lax.broadcasted_iota(jnp.int32, raw.shape, 1)
    v = raw + bias
    dt = _softplus(v)
    logf = -_softplus(-v)
    out = jnp.where(lane < SSD_H, dt, jnp.where(lane < SSD_H + FOX_H, logf, 0.0))
    return jnp.where(valid, out, 0.0)


def _b_smallact(valid, raw, d1, d2, bias):
    _, vjp = jax.vjp(lambda r, b: _smallact(valid, r, b), raw, bias)
    return vjp(d1 + d2)


def _ssd_post(valid, y, xs, z, drep, nw):
    y = (y + xs * drep) * _silu(z)
    return _rms(y, nw)


def _b_ssd_post(valid, y, xs, z, dya, drep, nw):
    _, vjp = jax.vjp(lambda a, b, c, d, e: _ssd_post(valid, a, b, c, d, e), y, xs, z, drep, nw)
    dy, dxs, dz, dd, dn = vjp(dya)
    return dy, dxs, dz, dd, dn


def _s5_pre(valid, ys, u, d):
    return _gelu(ys + d * u)


def _s5_glu(valid, ys, u, t, d):
    y1 = _gelu(ys + d * u)
    return y1 * _sigmoid(t)


def _b_s5_glu(valid, ys, u, t, dyc, d):
    y1 = _gelu(ys + d * u)
    _, vjp = jax.vjp(lambda a, b: a * _sigmoid(b), y1, t)
    dy1, dt = vjp(dyc)
    return dt, dy1


def _b_s5_pre(valid, ys, u, dy1a, dy1b, d):
    _, vjp = jax.vjp(lambda a, b, c: _gelu(a + c * b), ys, u, d)
    dys, du, dd = vjp(dy1a + dy1b)
    return dys, du, dd


def _merge(valid, g0, g1, g2, b0, b1, b2):
    m = _sigmoid(g0) * b0 + _sigmoid(g1) * b1 + _sigmoid(g2) * b2
    return jnp.where(valid, m, 0.0)


def _b_merge(valid, g0, g1, g2, b0, b1, b2, dmix):
    _, vjp = jax.vjp(lambda *a: _merge(valid, *a), g0, g1, g2, b0, b1, b2)
    d = vjp(dmix)
    return jnp.concatenate(d[:3], axis=1), d[3], d[4], d[5]


def _swiglu(valid, g, up):
    return _silu(g) * up


def _b_swiglu(valid, g, up, dact):
    _, vjp = jax.vjp(lambda a, b: _silu(a) * b, g, up)
    dg, dup = vjp(dact)
    return jnp.concatenate([dg, dup], axis=1)


def _conv_taps(ext, tile):
    taps = []
    for k in range(CONV_K):
        sh = CONV_K - 1 - k
        v = ext if sh == 0 else pltpu.roll(ext, sh, 0)
        taps.append(v[8:8 + tile])
    return taps


def _conv_fwd(x, w, b, *, geom, tile):
    T, Lp, pad = geom
    per_seq = Lp // tile
    hb = tile // 8

    def body(x_ref, h_ref, w_ref, b_ref, o_ref):
        i = pl.program_id(0)
        pos = (i % per_seq) * tile + lax.broadcasted_iota(jnp.int32, (tile, 1), 0)
        ext = jnp.concatenate([h_ref[...], x_ref[...]], axis=0)
        taps = _conv_taps(ext, tile)
        acc = b_ref[...] + taps[0] * w_ref[0:1, :]
        for k in range(1, CONV_K):
            acc = acc + taps[k] * w_ref[k:k + 1, :]
        o_ref[...] = jnp.where(pos >= pad, _silu(acc), 0.0)

    return pl.pallas_call(
        body, name="conv_fwd", grid=(T // tile,),
        in_specs=[pl.BlockSpec((tile, CONV_DIM), lambda i: (i, 0)),
                  pl.BlockSpec((8, CONV_DIM), lambda i: (jnp.maximum(i * hb - 1, 0), 0)),
                  pl.BlockSpec((CONV_K, CONV_DIM), lambda i: (0, 0)),
                  pl.BlockSpec((1, CONV_DIM), lambda i: (0, 0))],
        out_specs=pl.BlockSpec((tile, CONV_DIM), lambda i: (i, 0)),
        out_shape=jax.ShapeDtypeStruct((T, CONV_DIM), F32),
        compiler_params=_cp(("arbitrary",)),
    )(x, x, w, b)


def _conv_bwd_pre(x, dact, w, b, *, geom, tile):
    T, Lp, pad = geom
    per_seq = Lp // tile
    hb = tile // 8

    def body(x_ref, h_ref, d_ref, w_ref, b_ref, dc_ref, dw_ref, db_ref):
        i = pl.program_id(0)
        pos = (i % per_seq) * tile + lax.broadcasted_iota(jnp.int32, (tile, 1), 0)
        ext = jnp.concatenate([h_ref[...], x_ref[...]], axis=0)
        taps = _conv_taps(ext, tile)
        acc = b_ref[...] + taps[0] * w_ref[0:1, :]
        for k in range(1, CONV_K):
            acc = acc + taps[k] * w_ref[k:k + 1, :]
        sg = _sigmoid(acc)
        dsilu = sg * (1.0 + acc * (1.0 - sg))
        dc = jnp.where(pos >= pad, d_ref[...] * dsilu, 0.0)
        dc_ref[...] = dc
        dw = jnp.concatenate([_colsum(dc * taps[k]) for k in range(CONV_K)], axis=0)
        db = _colsum(dc)

        @pl.when(i == 0)
        def _():
            dw_ref[...] = dw
            db_ref[...] = db

        @pl.when(i > 0)
        def _():
            dw_ref[...] += dw
            db_ref[...] += db

    return pl.pallas_call(
        body, name="conv_bwd_pre", grid=(T // tile,),
        in_specs=[pl.BlockSpec((tile, CONV_DIM), lambda i: (i, 0)),
                  pl.BlockSpec((8, CONV_DIM), lambda i: (jnp.maximum(i * hb - 1, 0), 0)),
                  pl.BlockSpec((tile, CONV_DIM), lambda i: (i, 0)),
                  pl.BlockSpec((CONV_K, CONV_DIM), lambda i: (0, 0)),
                  pl.BlockSpec((1, CONV_DIM), lambda i: (0, 0))],
        out_specs=[pl.BlockSpec((tile, CONV_DIM), lambda i: (i, 0)),
                   pl.BlockSpec((CONV_K, CONV_DIM), lambda i: (0, 0)),
                   pl.BlockSpec((1, CONV_DIM), lambda i: (0, 0))],
        out_shape=[jax.ShapeDtypeStruct((T, CONV_DIM), F32),
                   jax.ShapeDtypeStruct((CONV_K, CONV_DIM), F32),
                   jax.ShapeDtypeStruct((1, CONV_DIM), F32)],
        compiler_params=_cp(("arbitrary",)),
    )(x, x, dact, w, b)


def _conv_bwd_x(dc, w, *, geom, tile):
    T, Lp, pad = geom
    nt = T // tile
    hb = tile // 8

    def body(d_ref, h_ref, w_ref, o_ref):
        i = pl.program_id(0)
        halo = jnp.where(i < nt - 1, h_ref[...], 0.0)
        ext = jnp.concatenate([d_ref[...], halo], axis=0)
        n_ext = tile + 8
        acc = ext[0:tile] * w_ref[CONV_K - 1:CONV_K, :]
        for j in range(1, CONV_K):
            acc = acc + pltpu.roll(ext, n_ext - j, 0)[0:tile] * w_ref[CONV_K - 1 - j:CONV_K - j, :]
        o_ref[...] = acc.astype(o_ref.dtype)

    return pl.pallas_call(
        body, name="conv_bwd_x", grid=(nt,),
        in_specs=[pl.BlockSpec((tile, CONV_DIM), lambda i: (i, 0)),
                  pl.BlockSpec((8, CONV_DIM), lambda i: (jnp.minimum((i + 1) * hb, nt * hb - 1), 0)),
                  pl.BlockSpec((CONV_K, CONV_DIM), lambda i: (0, 0))],
        out_specs=pl.BlockSpec((tile, CONV_DIM), lambda i: (i, 0)),
        out_shape=jax.ShapeDtypeStruct((T, CONV_DIM), BF16),
        compiler_params=_cp(("arbitrary",)),
    )(dc, dc, w)


def _ssd_common(sm_ref, alog_ref):
    lane = lax.broadcasted_iota(jnp.int32, (1, CH), 1)
    A = jnp.where(lane < SSD_H, -jnp.exp(alog_ref[...]), 0.0)
    dt = sm_ref[...]
    adt = dt * A
    r = lax.broadcasted_iota(jnp.int32, (CH, CH), 0)
    c = lax.broadcasted_iota(jnp.int32, (CH, CH), 1)
    tril = (r >= c).astype(F32)
    cs = jnp.dot(tril, adt, precision=HI, preferred_element_type=F32)
    csT = cs.T
    cs_last = jnp.sum(jnp.where(r == CH - 1, cs, 0.0), axis=0, keepdims=True)
    return A, dt, cs, csT, cs_last, tril, r, c


def _ssd_lanes():
    r = lax.broadcasted_iota(jnp.int32, (CH, D), 0)
    c = lax.broadcasted_iota(jnp.int32, (CH, D), 1)
    return (c // SSD_P == r).astype(F32)


def _ssd_per_lane(dt, cs):
    ex = _ssd_lanes()
    dt_rep = jnp.dot(dt, ex, precision=HI, preferred_element_type=F32)
    cs_rep = jnp.dot(cs, ex, precision=HI, preferred_element_type=F32)
    r = lax.broadcasted_iota(jnp.int32, (CH, D), 0)
    last_rep = jnp.sum(jnp.where(r == CH - 1, cs_rep, 0.0), axis=0, keepdims=True)
    return dt_rep, cs_rep, last_rep


def _ssd_per_head(*per_lane):
    ex = _ssd_lanes()
    return [lax.dot_general(v, ex, (((1,), (1,)), ((), ())), precision=HI, preferred_element_type=F32)
            for v in per_lane]


def _nt(a, b):
    return lax.dot_general(a, b, (((1,), (1,)), ((), ())), preferred_element_type=F32)


def _tn(a, b):
    return lax.dot_general(a, b, (((0,), (0,)), ((), ())), preferred_element_type=F32)


def _nn(a, b):
    return jnp.dot(a, b, preferred_element_type=F32)


def _ssd_fwd(xbc, sm, alog, *, B, NC):
    T = B * NC * CH

    def body(x_ref, sm_ref, alog_ref, y_ref, st_ref, S):
        cidx = pl.program_id(1)

        @pl.when(cidx == 0)
        def _():
            S[...] = jnp.zeros_like(S)

        st_ref[0] = S[...]
        A, dt, cs, csT, cs_last, tril, _, _ = _ssd_common(sm_ref, alog_ref)
        dt_rep, cs_rep, last_rep = _ssd_per_lane(dt, cs)
        xdt = x_ref[:, 0:D] * dt_rep
        xdec = (xdt * jnp.exp(last_rep - cs_rep)).astype(BF16)
        e_rep = jnp.exp(cs_rep)
        HG = SSD_H // SSD_G
        for g in range(SSD_G):
            gl = slice(g * HG * SSD_P, (g + 1) * HG * SSD_P)
            Bb = x_ref[:, D + g * SSD_N:D + (g + 1) * SSD_N].astype(BF16)
            Cb = x_ref[:, D + SSD_G * SSD_N + g * SSD_N:D + SSD_G * SSD_N + (g + 1) * SSD_N].astype(BF16)
            G = _nt(Cb, Bb)
            STg = S[g * HG:(g + 1) * HG].reshape(HG * SSD_P, SSD_N)
            y_off = e_rep[:, gl] * _nt(Cb, STg.astype(BF16))
            upd = _tn(xdec[:, gl], Bb)
            for rr in range(HG):
                h = g * HG + rr
                hl = slice(h * SSD_P, (h + 1) * SSD_P)
                col = cs[:, h:h + 1]
                row = csT[h:h + 1, :]
                Ld = jnp.where(tril > 0, jnp.exp(jnp.minimum(col - row, 0.0)), 0.0)
                M = (G * Ld).astype(BF16)
                y_ref[:, hl] = _nn(M, xdt[:, hl].astype(BF16)) + y_off[:, rr * SSD_P:(rr + 1) * SSD_P]
                rows = slice(rr * SSD_P, (rr + 1) * SSD_P)
                S[h] = jnp.exp(cs_last[:, h:h + 1]) * STg[rows] + upd[rows]

    return pl.pallas_call(
        body, name="ssd_fwd", grid=(B, NC),
        in_specs=[pl.BlockSpec((CH, CONV_DIM), lambda b, c: (b * NC + c, 0)),
                  pl.BlockSpec((CH, CH), lambda b, c: (b * NC + c, 0)),
                  pl.BlockSpec((1, CH), lambda b, c: (0, 0))],
        out_specs=[pl.BlockSpec((CH, D), lambda b, c: (b * NC + c, 0)),
                   pl.BlockSpec((1, SSD_H, SSD_P, SSD_N), lambda b, c: (b * NC + c, 0, 0, 0))],
        out_shape=[jax.ShapeDtypeStruct((T, D), F32),
                   jax.ShapeDtypeStruct((B * NC, SSD_H, SSD_P, SSD_N), F32)],
        scratch_shapes=[pltpu.VMEM((SSD_H, SSD_P, SSD_N), F32)],
        compiler_params=_cp(("arbitrary", "arbitrary")),
    )(xbc, sm, alog)


def _ssd_bwd(xbc, sm, alog, states, dy, dxs_skip, *, B, NC):
    T = B * NC * CH

    def rix(b, c):
        return b * NC + (NC - 1 - c)

    def body(x_ref, sm_ref, alog_ref, st_ref, dy_ref, sk_ref, dx_ref, ddt_ref, dal_ref, dS):
        bidx = pl.program_id(0)
        cidx = pl.program_id(1)

        @pl.when(cidx == 0)
        def _():
            dS[...] = jnp.zeros_like(dS)

        A, dt, cs, csT, cs_last, tril, r, c = _ssd_common(sm_ref, alog_ref)
        lane = lax.broadcasted_iota(jnp.int32, (1, CH), 1)
        dt_rep, cs_rep, last_rep = _ssd_per_lane(dt, cs)
        xs = x_ref[:, 0:D]
        xdt = xs * dt_rep
        e_rep = jnp.exp(cs_rep)
        dec_rep = jnp.exp(last_rep - cs_rep)
        dye = dy_ref[...] * e_rep
        xdec = xdt * dec_rep
        HG = SSD_H // SSD_G
        DCcol = jnp.zeros((CH, CH), F32)
        DCrow = jnp.zeros((CH, CH), F32)
        dlast = jnp.zeros((1, CH), F32)
        t_off, t_dec, dx_state = [], [], []
        for g in range(SSD_G):
            ob = D + g * SSD_N
            oc = D + SSD_G * SSD_N + g * SSD_N
            gl = slice(g * HG * SSD_P, (g + 1) * HG * SSD_P)
            Bb = x_ref[:, ob:ob + SSD_N].astype(BF16)
            Cb = x_ref[:, oc:oc + SSD_N].astype(BF16)
            G = _nt(Cb, Bb)
            STg = st_ref[0, g * HG:(g + 1) * HG].reshape(HG * SSD_P, SSD_N)
            dSTg = dS[g * HG:(g + 1) * HG].reshape(HG * SSD_P, SSD_N)
            STb = STg.astype(BF16)
            dSTb = dSTg.astype(BF16)
            dyeb = dye[:, gl].astype(BF16)
            t_off.append(dye[:, gl] * _nt(Cb, STb))
            dCg = _nn(dyeb, STb)
            dS_in = _tn(dyeb, Cb)
            Z = _nt(Bb, dSTb)
            dx_state.append(dec_rep[:, gl] * Z)
            t_dec.append(xdec[:, gl] * Z)
            dBg = _nn(xdec[:, gl].astype(BF16), dSTb)
            dG = jnp.zeros((CH, CH), F32)
            for rr in range(HG):
                h = g * HG + rr
                hl = slice(h * SSD_P, (h + 1) * SSD_P)
                rows = slice(rr * SSD_P, (rr + 1) * SSD_P)
                col = cs[:, h:h + 1]
                row = csT[h:h + 1, :]
                Ld = jnp.where(tril > 0, jnp.exp(jnp.minimum(col - row, 0.0)), 0.0)
                Mf = G * Ld
                dyb = dy_ref[:, hl].astype(BF16)
                dx_ref[:, hl] = _tn(Mf.astype(BF16), dyb)
                dM = _nt(dyb, xdt[:, hl].astype(BF16))
                dG = dG + dM * Ld
                W = dM * Mf
                DCcol = DCcol + jnp.where(c == h, jnp.sum(W, axis=1, keepdims=True), 0.0)
                DCrow = DCrow - jnp.where(r == h, jnp.sum(W, axis=0, keepdims=True), 0.0)
                el = jnp.exp(cs_last[:, h:h + 1])
                dl = el * jnp.sum(jnp.sum(dSTg[rows] * STg[rows], axis=1, keepdims=True), axis=0, keepdims=True)
                dlast = dlast + jnp.where(lane == h, dl, 0.0)
                dS[h] = dS_in[rows] + el * dSTg[rows]
            dGb = dG.astype(BF16)
            dx_ref[:, ob:ob + SSD_N] = dBg + _tn(dGb, Cb)
            dx_ref[:, oc:oc + SSD_N] = dCg + _nn(dGb, Bb)
        dxdt = dx_ref[:, 0:D] + jnp.concatenate(dx_state, axis=1)
        dx_ref[:, 0:D] = dxdt * dt_rep + sk_ref[...]
        s_off, s_dec, DX = _ssd_per_head(jnp.concatenate(t_off, axis=1), jnp.concatenate(t_dec, axis=1), dxdt * xs)
        dlast = dlast + jnp.sum(s_dec, axis=0, keepdims=True)
        DC = DCcol + s_off - s_dec + DCrow.T + jnp.where(r == CH - 1, dlast, 0.0)
        triu = (r <= c).astype(F32)
        dadt = jnp.dot(triu, DC, precision=HI, preferred_element_type=F32)
        ddt_ref[...] = dadt * A + DX
        dal = jnp.sum(dadt * dt, axis=0, keepdims=True) * A

        @pl.when((bidx == 0) & (cidx == 0))
        def _():
            dal_ref[...] = dal

        @pl.when((bidx > 0) | (cidx > 0))
        def _():
            dal_ref[...] += dal

    return pl.pallas_call(
        body, name="ssd_bwd", grid=(B, NC),
        in_specs=[pl.BlockSpec((CH, CONV_DIM), lambda b, c: (rix(b, c), 0)),
                  pl.BlockSpec((CH, CH), lambda b, c: (rix(b, c), 0)),
                  pl.BlockSpec((1, CH), lambda b, c: (0, 0)),
                  pl.BlockSpec((1, SSD_H, SSD_P, SSD_N), lambda b, c: (rix(b, c), 0, 0, 0)),
                  pl.BlockSpec((CH, D), lambda b, c: (rix(b, c), 0)),
                  pl.BlockSpec((CH, D), lambda b, c: (rix(b, c), 0))],
        out_specs=[pl.BlockSpec((CH, CONV_DIM), lambda b, c: (rix(b, c), 0)),
                   pl.BlockSpec((CH, CH), lambda b, c: (rix(b, c), 0)),
                   pl.BlockSpec((1, CH), lambda b, c: (0, 0))],
        out_shape=[jax.ShapeDtypeStruct((T, CONV_DIM), F32),
                   jax.ShapeDtypeStruct((T, CH), F32),
                   jax.ShapeDtypeStruct((1, CH), F32)],
        scratch_shapes=[pltpu.VMEM((SSD_H, SSD_P, SSD_N), F32)],
        compiler_params=_cp(("arbitrary", "arbitrary")),
    )(xbc, sm, alog, states, dy, dxs_skip)


def _cumsum_seq(v, *, B, NC, reverse, name):
    T = B * NC * CH
    R = 3 * CH if (NC % 3 == 0 and NC > 3) else CH
    NR = NC * CH // R

    def ix(b, c):
        return b * NR + ((NR - 1 - c) if reverse else c)

    def body(v_ref, o_ref, carry):
        cidx = pl.program_id(1)

        @pl.when(cidx == 0)
        def _():
            carry[...] = jnp.zeros_like(carry)

        r = lax.broadcasted_iota(jnp.int32, (R, R), 0)
        c = lax.broadcasted_iota(jnp.int32, (R, R), 1)
        tri = ((r <= c) if reverse else (r >= c)).astype(F32)
        cs = jnp.dot(tri, v_ref[...], precision=HI, preferred_element_type=F32) + carry[...]
        o_ref[...] = cs
        edge = 0 if reverse else R - 1
        rows = lax.broadcasted_iota(jnp.int32, (R, CH), 0)
        carry[...] = jnp.sum(jnp.where(rows == edge, cs, 0.0), axis=0, keepdims=True)

    return pl.pallas_call(
        body, name=name, grid=(B, NR),
        in_specs=[pl.BlockSpec((R, CH), lambda b, c: (ix(b, c), 0))],
        out_specs=pl.BlockSpec((R, CH), lambda b, c: (ix(b, c), 0)),
        out_shape=jax.ShapeDtypeStruct((T, CH), F32),
        scratch_shapes=[pltpu.VMEM((1, CH), F32)],
        compiler_params=_cp(("arbitrary", "arbitrary")),
    )(v)


def _fox_tb(Lp):
    return 384 if (Lp % 384 == 0 and Lp > 384) else CH


def _fox_keybias(cum, *, B, Lp, pad):
    ck = cum.reshape(B, Lp, CH)[:, :, SSD_H:SSD_H + FOX_H].transpose(0, 2, 1)
    pos = lax.broadcasted_iota(jnp.int32, ck.shape, 2)
    return jnp.where(pos < pad, -NEG, ck).reshape(B * FOX_H, 1, Lp)


def _fox_tril(TB):
    r = lax.broadcasted_iota(jnp.int32, (TB, TB), 0)
    c = lax.broadcasted_iota(jnp.int32, (TB, TB), 1)
    return r >= c


def _fox_fwd(qkv, cumT, *, B, Lp):
    TB = _fox_tb(Lp)
    NQ = Lp // TB
    T = B * Lp
    scale = FOX_DH ** -0.5

    def body(q_ref, k_ref, v_ref, ct_ref, o_ref, lse_ref):
        i = pl.program_id(2)
        q = q_ref[...]

        def block(j, nb, carry, diag):
            m, l, acc = carry
            off = pl.multiple_of(j * TB, TB)
            k = k_ref[pl.ds(off, nb * TB), :]
            v = v_ref[pl.ds(off, nb * TB), :]
            s = _nt(q, k) * scale - ct_ref[0, :, pl.ds(off, nb * TB)]
            if diag:
                s = jnp.where(_fox_tril(TB), s, NEG)
            m_new = jnp.maximum(m, jnp.max(s, axis=1, keepdims=True))
            p = jnp.exp(s - m_new)
            alpha = jnp.exp(m - m_new)
            l = alpha * l + jnp.sum(p, axis=1, keepdims=True)
            acc = alpha * acc + _nn(p.astype(BF16), v)
            return m_new, l, acc

        init = (jnp.full((TB, 1), NEG, F32), jnp.zeros((TB, 1), F32), jnp.zeros((TB, FOX_DH), F32))
        carry = lax.fori_loop(0, i // 2, lambda t, c: block(2 * t, 2, c, False), init)
        carry = lax.fori_loop(2 * (i // 2), i, lambda j, c: block(j, 1, c, False), carry)
        m, l, acc = block(i, 1, carry, True)
        o_ref[...] = (acc / l).astype(o_ref.dtype)
        lse_ref[0, 0] = m + jnp.log(l)

    return pl.pallas_call(
        body, name="fox_fwd", grid=(B, FOX_H, NQ),
        in_specs=[pl.BlockSpec((TB, FOX_DH), lambda b, h, i: (b * NQ + i, h)),
                  pl.BlockSpec((Lp, FOX_DH), lambda b, h, i: (b, FOX_H + h)),
                  pl.BlockSpec((Lp, FOX_DH), lambda b, h, i: (b, 2 * FOX_H + h)),
                  pl.BlockSpec((1, 1, Lp), lambda b, h, i: (b * FOX_H + h, 0, 0))],
        out_specs=[pl.BlockSpec((TB, FOX_DH), lambda b, h, i: (b * NQ + i, h)),
                   pl.BlockSpec((1, 1, TB, 1), lambda b, h, i: (b, h, i, 0))],
        out_shape=[jax.ShapeDtypeStruct((T, D), BF16),
                   jax.ShapeDtypeStruct((B, FOX_H, Lp, 1), F32)],
        compiler_params=_cp(("arbitrary", "arbitrary", "arbitrary")),
    )(qkv, qkv, qkv, cumT)


def _fox_bwd(qkv, dy, o, lse, cumT, *, B, Lp):
    TB = _fox_tb(Lp)
    NQ = Lp // TB
    T = B * Lp
    scale = FOX_DH ** -0.5

    def body(q_ref, k_ref, v_ref, dy_ref, o_ref, lse_ref, ct_ref, dq_ref, dk_ref, dv_ref, dck_ref, dcq_ref, dl_s):
        j = pl.program_id(2)
        k = k_ref[...]
        v = v_ref[...]
        ck = ct_ref[0]

        @pl.when(j == 0)
        def _():
            dq_ref[...] = jnp.zeros_like(dq_ref)
            dcq_ref[...] = jnp.zeros_like(dcq_ref)
            for i in range(NQ):
                sl = slice(i * TB, (i + 1) * TB)
                dl_s[sl, :] = jnp.sum(dy_ref[sl, :].astype(F32) * o_ref[sl, :].astype(F32), axis=1, keepdims=True)

        def block(i, nb, carry, diag):
            dk, dv, dck = carry
            off = pl.multiple_of(i * TB, TB)
            rows = pl.ds(off, nb * TB)
            q = q_ref[rows, :]
            dob = dy_ref[rows, :].astype(BF16)
            e = _nt(q, k) * scale - ck - lse_ref[0, 0, rows, :]
            if diag:
                e = jnp.where(_fox_tril(TB), e, NEG)
            p = jnp.exp(e)
            dv = dv + _tn(p.astype(BF16), dob)
            ds = p * (_nt(dob, v) - dl_s[rows, :])
            dsb = ds.astype(BF16)
            dk = dk + _tn(dsb, q)
            dq_ref[rows, :] += _nn(dsb, k) * scale
            dcq_ref[0, 0, rows, :] += jnp.sum(ds, axis=1, keepdims=True)
            dck = dck - jnp.sum(ds, axis=0, keepdims=True)
            return dk, dv, dck

        z = jnp.zeros((TB, FOX_DH), F32)
        carry = block(j, 1, (z, z, jnp.zeros((1, TB), F32)), True)
        npair = (NQ - 1 - j) // 2
        carry = lax.fori_loop(0, npair, lambda t, c: block(j + 1 + 2 * t, 2, c, False), carry)
        dk, dv, dck = lax.fori_loop(j + 1 + 2 * npair, NQ, lambda i, c: block(i, 1, c, False), carry)
        dk_ref[...] = (dk * scale).astype(dk_ref.dtype)
        dv_ref[...] = dv.astype(dv_ref.dtype)
        dck_ref[0] = dck

    head = lambda b, h, j: (b, h)
    return pl.pallas_call(
        body, name="fox_bwd", grid=(B, FOX_H, NQ),
        in_specs=[pl.BlockSpec((Lp, FOX_DH), head),
                  pl.BlockSpec((TB, FOX_DH), lambda b, h, j: (b * NQ + j, FOX_H + h)),
                  pl.BlockSpec((TB, FOX_DH), lambda b, h, j: (b * NQ + j, 2 * FOX_H + h)),
                  pl.BlockSpec((Lp, FOX_DH), head),
                  pl.BlockSpec((Lp, FOX_DH), head),
                  pl.BlockSpec((1, 1, Lp, 1), lambda b, h, j: (b, h, 0, 0)),
                  pl.BlockSpec((1, 1, TB), lambda b, h, j: (b * FOX_H + h, 0, j))],
        out_specs=[pl.BlockSpec((Lp, FOX_DH), head),
                   pl.BlockSpec((TB, FOX_DH), lambda b, h, j: (b * NQ + j, h)),
                   pl.BlockSpec((TB, FOX_DH), lambda b, h, j: (b * NQ + j, h)),
                   pl.BlockSpec((1, 1, TB), lambda b, h, j: (b * FOX_H + h, 0, j)),
                   pl.BlockSpec((1, 1, Lp, 1), lambda b, h, j: (b, h, 0, 0))],
        out_shape=[jax.ShapeDtypeStruct((T, D), F32),
                   jax.ShapeDtypeStruct((T, D), BF16),
                   jax.ShapeDtypeStruct((T, D), BF16),
                   jax.ShapeDtypeStruct((B * FOX_H, 1, Lp), F32),
                   jax.ShapeDtypeStruct((B, FOX_H, Lp, 1), F32)],
        scratch_shapes=[pltpu.VMEM((Lp, 1), F32)],
        compiler_params=_cp(("arbitrary", "arbitrary", "arbitrary")),
    )(qkv, qkv, qkv, dy, o, lse, cumT)


S5_TILE = 8


def _s5_pows(lam_ref, pw, tab, reverse):
    lr = lam_ref[0, :, 0:S5_W]
    li = lam_ref[0, :, S5_W:2 * S5_W]
    if reverse:
        li = -li
    ar, ai = lr, li
    sub = lax.broadcasted_iota(jnp.int32, (S5_TILE, 1), 0)
    for k, s in enumerate((1, 2, 4)):
        keep = (sub < S5_TILE - s) if reverse else (sub >= s)
        pw[k * S5_TILE:(k + 1) * S5_TILE, 0:S5_W] = jnp.where(keep, ar, 0.0)
        pw[k * S5_TILE:(k + 1) * S5_TILE, S5_W:2 * S5_W] = jnp.where(keep, ai, 0.0)
        ar, ai = ar * ar - ai * ai, 2.0 * ar * ai
    ar, ai = lr, li
    for r in range(S5_TILE):
        row = (S5_TILE - 1 - r) if reverse else r
        tab[row:row + 1, 0:S5_W] = ar
        tab[row:row + 1, S5_W:2 * S5_W] = ai
        ar, ai = ar * lr - ai * li, ar * li + ai * lr


def _s5_scan(hs, pw, tab, carry, reverse):
    n = hs.shape[0]
    tiles = list(range(n // S5_TILE))
    if reverse:
        tiles.reverse()
    for t in tiles:
        lo = t * S5_TILE
        vr = hs[lo:lo + S5_TILE, 0:S5_W]
        vi = hs[lo:lo + S5_TILE, S5_W:2 * S5_W]
        for k, s in enumerate((1, 2, 4)):
            sh = (S5_TILE - s) if reverse else s
            sr = pltpu.roll(vr, sh, 0)
            si = pltpu.roll(vi, sh, 0)
            ar = pw[k * S5_TILE:(k + 1) * S5_TILE, 0:S5_W]
            ai = pw[k * S5_TILE:(k + 1) * S5_TILE, S5_W:2 * S5_W]
            vr, vi = vr + ar * sr - ai * si, vi + ar * si + ai * sr
        hs[lo:lo + S5_TILE, 0:S5_W] = vr
        hs[lo:lo + S5_TILE, S5_W:2 * S5_W] = vi
    e8 = 0 if reverse else S5_TILE - 1
    l8r = tab[e8:e8 + 1, 0:S5_W]
    l8i = tab[e8:e8 + 1, S5_W:2 * S5_W]
    cr = carry[:, 0:S5_W]
    ci = carry[:, S5_W:2 * S5_W]
    states = []
    for t in tiles:
        states.append((cr, ci))
        edge = t * S5_TILE + e8
        er = hs[edge:edge + 1, 0:S5_W]
        ei = hs[edge:edge + 1, S5_W:2 * S5_W]
        cr, ci = er + l8r * cr - l8i * ci, ei + l8r * ci + l8i * cr
    carry[:, 0:S5_W] = cr
    carry[:, S5_W:2 * S5_W] = ci
    tr = tab[:, 0:S5_W]
    ti = tab[:, S5_W:2 * S5_W]
    for t, (cr, ci) in zip(tiles, states):
        lo = t * S5_TILE
        hs[lo:lo + S5_TILE, 0:S5_W] += tr * cr - ti * ci
        hs[lo:lo + S5_TILE, S5_W:2 * S5_W] += tr * ci + ti * cr


def _s5_rows(NC):
    return 3 * CH if (NC % 3 == 0 and NC > 3) else CH


def _s5_fwd(u, Bsg, Csg, lam, *, B, NC):
    T = B * NC * CH
    R = _s5_rows(NC)
    NR = NC * CH // R

    def body(u_ref, b_ref, c_ref, lam_ref, y_ref, h_ref, pw, tab, hs, carry):
        cidx = pl.program_id(2)

        @pl.when(cidx == 0)
        def _():
            _s5_pows(lam_ref, pw, tab, False)
            carry[...] = jnp.zeros_like(carry)

        hs[...] = _nn(u_ref[...].astype(BF16), b_ref[0])
        _s5_scan(hs, pw, tab, carry, False)
        hb = hs[...].astype(BF16)
        h_ref[...] = hb
        y_ref[...] = _nn(hb, c_ref[0])

    return pl.pallas_call(
        body, name="s5_fwd", grid=(B, S5_G // S5_SG, NR),
        in_specs=[pl.BlockSpec((R, CH), lambda b, s, c: (b * NR + c, s)),
                  pl.BlockSpec((1, CH, 2 * S5_W), lambda b, s, c: (s, 0, 0)),
                  pl.BlockSpec((1, 2 * S5_W, CH), lambda b, s, c: (s, 0, 0)),
                  pl.BlockSpec((1, 1, 2 * S5_W), lambda b, s, c: (s, 0, 0))],
        out_specs=[pl.BlockSpec((R, CH), lambda b, s, c: (b * NR + c, s)),
                   pl.BlockSpec((R, 2 * S5_W), lambda b, s, c: (b * NR + c, s))],
        out_shape=[jax.ShapeDtypeStruct((T, D), F32),
                   jax.ShapeDtypeStruct((T, (S5_G // S5_SG) * 2 * S5_W), BF16)],
        scratch_shapes=[pltpu.VMEM((3 * S5_TILE, 2 * S5_W), F32), pltpu.VMEM((S5_TILE, 2 * S5_W), F32),
                        pltpu.VMEM((R, 2 * S5_W), F32), pltpu.VMEM((1, 2 * S5_W), F32)],
        compiler_params=_cp(("arbitrary", "arbitrary", "arbitrary")),
    )(u, Bsg, Csg, lam)


def _s5_bwd(u, hst, dy, du_skip, Bsg, Csg, lam, *, B, NC):
    T = B * NC * CH
    NS = S5_G // S5_SG
    R = _s5_rows(NC)
    NR = NC * CH // R
    hb16 = R // 16

    def rix(b, c):
        return b * NR + (NR - 1 - c)

    def body(u_ref, h_ref, hp_ref, dy_ref, sk_ref, b_ref, c_ref, lam_ref,
             du_ref, db_ref, dc_ref, dl_ref, pw, tab, gs, carry):
        bidx = pl.program_id(1)
        cidx = pl.program_id(2)
        first = (bidx == 0) & (cidx == 0)

        @pl.when(cidx == 0)
        def _():
            _s5_pows(lam_ref, pw, tab, True)
            carry[...] = jnp.zeros_like(carry)

        dyb = dy_ref[...].astype(BF16)
        gs[...] = _nt(dyb, c_ref[0])
        _s5_scan(gs, pw, tab, carry, True)
        gr = gs[:, 0:S5_W]
        gi = gs[:, S5_W:]
        gb = gs[...].astype(BF16)
        du_ref[...] = (_nt(gb, b_ref[0]) + sk_ref[...]).astype(du_ref.dtype)
        ub = u_ref[...].astype(BF16)
        hcur = h_ref[...]
        dB = _tn(ub, gb)
        dC = _tn(dyb, hcur)
        hf = hcur.astype(F32)
        row = lax.broadcasted_iota(jnp.int32, (R, 1), 0)
        prev_last = jnp.where(cidx < NR - 1, hp_ref[15:16, :].astype(F32), 0.0)
        hprev = jnp.where(row == 0, prev_last, pltpu.roll(hf, 1, 0))
        pr = hprev[:, 0:S5_W]
        pi = hprev[:, S5_W:]
        da = _colsum(gr * pr + gi * pi)
        dbb = _colsum(gi * pr - gr * pi)
        dl = jnp.concatenate([da, dbb], axis=1)

        @pl.when(first)
        def _():
            db_ref[0] = dB
            dc_ref[0] = dC
            dl_ref[0] = dl

        @pl.when(jnp.logical_not(first))
        def _():
            db_ref[0] += dB
            dc_ref[0] += dC
            dl_ref[0] += dl

    return pl.pallas_call(
        body, name="s5_bwd", grid=(NS, B, NR),
        in_specs=[pl.BlockSpec((R, CH), lambda s, b, c: (rix(b, c), s)),
                  pl.BlockSpec((R, 2 * S5_W), lambda s, b, c: (rix(b, c), s)),
                  pl.BlockSpec((16, 2 * S5_W), lambda s, b, c: (jnp.maximum(rix(b, c) * hb16 - 1, 0), s)),
                  pl.BlockSpec((R, CH), lambda s, b, c: (rix(b, c), s)),
                  pl.BlockSpec((R, CH), lambda s, b, c: (rix(b, c), s)),
                  pl.BlockSpec((1, CH, 2 * S5_W), lambda s, b, c: (s, 0, 0)),
                  pl.BlockSpec((1, 2 * S5_W, CH), lambda s, b, c: (s, 0, 0)),
                  pl.BlockSpec((1, 1, 2 * S5_W), lambda s, b, c: (s, 0, 0))],
        out_specs=[pl.BlockSpec((R, CH), lambda s, b, c: (rix(b, c), s)),
                   pl.BlockSpec((1, CH, 2 * S5_W), lambda s, b, c: (s, 0, 0)),
                   pl.BlockSpec((1, CH, 2 * S5_W), lambda s, b, c: (s, 0, 0)),
                   pl.BlockSpec((1, 1, 2 * S5_W), lambda s, b, c: (s, 0, 0))],
        out_shape=[jax.ShapeDtypeStruct((T, D), BF16),
                   jax.ShapeDtypeStruct((NS, CH, 2 * S5_W), F32),
                   jax.ShapeDtypeStruct((NS, CH, 2 * S5_W), F32),
                   jax.ShapeDtypeStruct((NS, 1, 2 * S5_W), F32)],
        scratch_shapes=[pltpu.VMEM((3 * S5_TILE, 2 * S5_W), F32), pltpu.VMEM((S5_TILE, 2 * S5_W), F32),
                        pltpu.VMEM((R, 2 * S5_W), F32), pltpu.VMEM((1, 2 * S5_W), F32)],
        compiler_params=_cp(("arbitrary", "arbitrary", "arbitrary")),
    )(u, hst, hst, dy, du_skip, Bsg, Csg, lam)


def _s5_param_fn(lre, lim, lstep, bre, bim):
    step = jnp.exp(lstep)
    zr = lre * step
    zi = lim * step
    e = jnp.exp(zr)
    a = e * jnp.cos(zi)
    b = e * jnp.sin(zi)
    den = lre * lre + lim * lim
    qr = ((a - 1.0) * lre + b * lim) / den
    qi = (b * lre - (a - 1.0) * lim) / den
    return a, b, qr[None] * bre - qi[None] * bim, qr[None] * bim + qi[None] * bre


_S5_ROWS = S5_G * S5_P // CH


def _s5_tile(v):
    return v.reshape(_S5_ROWS, CH)


def _s5_tile_b(v):
    return v.reshape(S5_G * S5_P, S5_C).T.reshape(S5_C, _S5_ROWS, CH)


def _s5_untile_b(v):
    return v.reshape(S5_C, S5_G * S5_P).T.reshape(S5_G, S5_P, S5_C)


def _s5_params(lre, lim, lstep, bre, bim):
    def body(a_ref, b_ref, c_ref, d_ref, e_ref, o1, o2, o3, o4):
        outs = _s5_param_fn(a_ref[...], b_ref[...], c_ref[...], d_ref[...], e_ref[...])
        for o, v in zip((o1, o2, o3, o4), outs):
            o[...] = v

    shp = [jax.ShapeDtypeStruct(lre.shape, F32)] * 2 + [jax.ShapeDtypeStruct(bre.shape, F32)] * 2
    return pl.pallas_call(body, name="s5_params", out_shape=shp, compiler_params=_cp())(lre, lim, lstep, bre, bim)


def _s5_params_bwd(lre, lim, lstep, bre, bim, da, db, dbr, dbi):
    def body(a_ref, b_ref, c_ref, d_ref, e_ref, g1, g2, g3, g4, o1, o2, o3, o4, o5):
        _, vjp = jax.vjp(_s5_param_fn, a_ref[...], b_ref[...], c_ref[...], d_ref[...], e_ref[...])
        outs = vjp((g1[...], g2[...], g3[...], g4[...]))
        for o, v in zip((o1, o2, o3, o4, o5), outs):
            o[...] = v

    shp = [jax.ShapeDtypeStruct(lre.shape, F32)] * 3 + [jax.ShapeDtypeStruct(bre.shape, F32)] * 2
    return pl.pallas_call(body, name="s5_params_bwd", out_shape=shp, compiler_params=_cp())(
        lre, lim, lstep, bre, bim, da, db, dbr, dbi)


def _s5_blockdiag(br, bi, cre, cim):
    NS = S5_G // S5_SG
    eye = jnp.eye(S5_SG, dtype=F32)

    def bmat(v):
        v = v.reshape(NS, S5_SG, S5_P, S5_C)
        m = jnp.einsum("sgpc,gh->sgchp", v, eye)
        return m.reshape(NS, S5_SG * S5_C, S5_SG * S5_P)

    def cmat(v):
        v = v.reshape(NS, S5_SG, S5_C, S5_P)
        m = jnp.einsum("sgcp,gh->sgphc", v, eye)
        return m.reshape(NS, S5_SG * S5_P, S5_SG * S5_C)

    Bsg = jnp.concatenate([bmat(br), bmat(bi)], axis=2).astype(BF16)
    Csg = jnp.concatenate([cmat(cre), cmat(-cim)], axis=1).astype(BF16)
    return Bsg, Csg


def _s5_unblock(dBsg, dCsgT):
    NS = S5_G // S5_SG

    def diag(m):
        m = m.reshape(NS, S5_SG, S5_C, S5_SG, S5_P)
        return jnp.stack([m[:, g, :, g, :] for g in range(S5_SG)], axis=1)

    def ub(m):
        return diag(m).transpose(0, 1, 3, 2).reshape(S5_G, S5_P, S5_C)

    def uc(m):
        return diag(m).reshape(S5_G, S5_C, S5_P)

    dbr = ub(dBsg[:, :, 0:S5_W])
    dbi = ub(dBsg[:, :, S5_W:])
    dcr = uc(dCsgT[:, :, 0:S5_W])
    dci = -uc(dCsgT[:, :, S5_W:])
    return dbr, dbi, dcr, dci


def _loss_head(x, nf, target, *, B, NC, S):
    T = B * NC * CH
    nts = S // CH

    def f(xv, w, t):
        y = _rms(xv, w)
        return 0.5 * _colsum(jnp.mean(jnp.square(y - t), axis=-1, keepdims=True))

    def body(x_ref, w_ref, t_ref, dx_ref, ls_ref, dw_ref):
        i = pl.program_id(0)
        on = (i % NC) > 0
        t = t_ref[...]
        l, vjp = jax.vjp(lambda a, b: f(a, b, t), x_ref[...], w_ref[...])
        dx, dw = vjp(jnp.ones((1, 1), F32))
        g = jnp.where(on, 1.0, 0.0)
        dx_ref[...] = dx * g
        lv = jnp.zeros((1, CH), F32) + l * g

        @pl.when(i == 0)
        def _():
            ls_ref[...] = lv
            dw_ref[...] = dw * g

        @pl.when(i > 0)
        def _():
            ls_ref[...] += lv
            dw_ref[...] += dw * g

    def tix(i):
        return ((i // NC) * nts + jnp.maximum(i % NC - 1, 0), 0)

    return pl.pallas_call(
        body, name="loss_head", grid=(B * NC,),
        in_specs=[pl.BlockSpec((CH, D), lambda i: (i, 0)),
                  pl.BlockSpec((1, D), lambda i: (0, 0)),
                  pl.BlockSpec((CH, D), tix)],
        out_specs=[pl.BlockSpec((CH, D), lambda i: (i, 0)),
                   pl.BlockSpec((1, CH), lambda i: (0, 0)),
                   pl.BlockSpec((1, D), lambda i: (0, 0))],
        out_shape=[jax.ShapeDtypeStruct((T, D), F32),
                   jax.ShapeDtypeStruct((1, CH), F32),
                   jax.ShapeDtypeStruct((1, D), F32)],
        compiler_params=_cp(("arbitrary",)),
    )(x, nf, target)


def _ew(fn, ins, n_out, out_dtypes, *, name, tile=None):
    R, C = ins[0].shape
    tile = tile or _pick(R, (512, 256, 128, 64, 32, 16, 8, 1))
    if tile % 8 != 0:
        tile = R

    def body(*refs):
        outs = fn(*[r[...] for r in refs[:len(ins)]])
        if not isinstance(outs, (tuple, list)):
            outs = (outs,)
        for r, v in zip(refs[len(ins):], outs):
            r[...] = v.astype(r.dtype)

    spec = pl.BlockSpec((tile, C), lambda i: (i, 0))
    res = pl.pallas_call(
        body, name=name, grid=(R // tile,), in_specs=[spec] * len(ins), out_specs=[spec] * n_out,
        out_shape=[jax.ShapeDtypeStruct((R, C), dt) for dt in out_dtypes],
        compiler_params=_cp(("parallel",)),
    )(*ins)
    return res


def _adam_fn(w, g, m, v):
    m = ADAM_B1 * m + (1.0 - ADAM_B1) * g
    v = ADAM_B2 * v + (1.0 - ADAM_B2) * jnp.square(g)
    m_hat = m / (1.0 - ADAM_B1 ** ADAM_STEP)
    v_hat = v / (1.0 - ADAM_B2 ** ADAM_STEP)
    delta = -ADAM_LR * (m_hat / (jnp.sqrt(v_hat) + ADAM_EPS) + ADAM_WD * w)
    return delta, m, v


def _adam(w, g, m, v, name):
    shp = w.shape
    C = shp[-1]
    f = lambda a: a.reshape(-1, C)
    d, nm, nv = _ew(_adam_fn, [f(w), f(g), f(m), f(v)], 3, [F32] * 3, name=name)
    return d.reshape(shp), nm.reshape(shp), nv.reshape(shp)


def _me():
    return lax.axis_index("x"), lax.axis_index("y"), lax.axis_index("c")


def _all_gather(v, name):
    def body(x_ref, out_ref, send_sems, recv_sems, local_sem):
        x, y, c = _me()
        me, sibling = (x, y, c), (x, y, 1 - c)
        chips = [(1 - x, y), (x, 1 - y), (1 - x, 1 - y)]

        def slot(px, py, pc):
            return out_ref.at[4 * px + 2 * py + pc]

        def copy(k, block, to, src=None):
            return pltpu.make_async_remote_copy(
                src_ref=slot(*block) if src is None else src, dst_ref=slot(*block),
                send_sem=send_sems.at[k], recv_sem=recv_sems.at[k], device_id=to, device_id_type=MESH)

        mine = pltpu.make_async_copy(x_ref, slot(*me), local_sem)
        mine.start()
        first = [copy(0, me, sibling, src=x_ref)]
        first += [copy(1 + j, me, (*chip, c), src=x_ref) for j, chip in enumerate(chips)]
        for cp in first:
            cp.start()
        passed = [copy(4 + j, (*chip, c), sibling) for j, chip in enumerate(chips)]
        for j, chip in enumerate(chips):
            copy(1 + j, (*chip, c), me).wait_recv()
            passed[j].start()
        copy(0, sibling, me).wait_recv()
        for j, chip in enumerate(chips):
            copy(4 + j, (*chip, 1 - c), me).wait_recv()
        for cp in first + passed:
            cp.wait_send()
        mine.wait()

    return pl.pallas_call(
        body, name=name, out_shape=jax.ShapeDtypeStruct((N_DEV,) + v.shape, v.dtype),
        in_specs=[pl.BlockSpec(memory_space=pl.ANY)], out_specs=pl.BlockSpec(memory_space=pl.ANY),
        scratch_shapes=[pltpu.SemaphoreType.DMA((7,)), pltpu.SemaphoreType.DMA((7,)), pltpu.SemaphoreType.DMA],
    )(v)


def _swap_core(g, name):
    def body(g_ref, out_ref, send_sems, recv_sems):
        x, y, c = _me()
        cps = [pltpu.make_async_remote_copy(
            src_ref=g_ref.at[q, 1 - c], dst_ref=out_ref.at[q], send_sem=send_sems.at[q], recv_sem=recv_sems.at[q],
            device_id=(x, y, 1 - c), device_id_type=MESH) for q in range(4)]
        for cp in cps:
            cp.start()
        for cp in cps:
            cp.wait()

    return pl.pallas_call(
        body, name=name, out_shape=jax.ShapeDtypeStruct((4,) + g.shape[2:], g.dtype),
        in_specs=[pl.BlockSpec(memory_space=pl.ANY)], out_specs=pl.BlockSpec(memory_space=pl.ANY),
        scratch_shapes=[pltpu.SemaphoreType.DMA((4,)), pltpu.SemaphoreType.DMA((4,))],
    )(g)


def _swap_chips(hb, name):
    flips = [(1, 0), (0, 1), (1, 1)]

    def body(h_ref, out_ref, send_sems, recv_sems):
        x, y, c = _me()
        cps = []
        for j, (fx, fy) in enumerate(flips):
            px = x + fx - 2 * x * fx
            py = y + fy - 2 * y * fy
            cps.append(pltpu.make_async_remote_copy(
                src_ref=h_ref.at[2 * px + py], dst_ref=out_ref.at[j], send_sem=send_sems.at[j],
                recv_sem=recv_sems.at[j], device_id=(px, py, c), device_id_type=MESH))
        for cp in cps:
            cp.start()
        for cp in cps:
            cp.wait()

    return pl.pallas_call(
        body, name=name, out_shape=jax.ShapeDtypeStruct((3,) + hb.shape[1:], hb.dtype),
        in_specs=[pl.BlockSpec(memory_space=pl.ANY)], out_specs=pl.BlockSpec(memory_space=pl.ANY),
        scratch_shapes=[pltpu.SemaphoreType.DMA((3,)), pltpu.SemaphoreType.DMA((3,))],
    )(hb)


def _reduce_scatter(g8, tag):
    shard = g8.shape[1:]
    C = shard[-1]
    rows = int(np.prod(shard[:-1]))
    tile = _pick(rows, (512, 256, 128, 64, 32, 16))
    x, y, c = _me()
    g4 = g8.reshape((4, 2) + shard)
    got = _swap_core(g4, "rs_core_" + tag)
    sel = jnp.stack([c, 2 * x + y]).astype(jnp.int32)

    def pair_body(sel_ref, a_ref, b_ref, h_ref, hb_ref):
        s = a_ref[...] + b_ref[...]
        h_ref[...] = s
        hb_ref[...] = s.astype(BF16)

    blk = pl.BlockSpec((None, tile, C), lambda q, i, sel_ref: (q, i, 0))
    h, hb = pl.pallas_call(
        pair_body, name="rs_pair_sum_" + tag,
        grid_spec=pltpu.PrefetchScalarGridSpec(
            num_scalar_prefetch=1, grid=(4, rows // tile),
            in_specs=[pl.BlockSpec((None, None, tile, C), lambda q, i, sel_ref: (q, sel_ref[0], i, 0)), blk],
            out_specs=[blk, blk]),
        out_shape=[jax.ShapeDtypeStruct((4, rows, C), F32), jax.ShapeDtypeStruct((4, rows, C), BF16)],
        compiler_params=_cp(("arbitrary", "arbitrary")),
    )(sel, g4.reshape(4, 2, rows, C), got.reshape(4, rows, C))
    got3 = _swap_chips(hb.reshape((4,) + shard), "rs_chips_" + tag)

    def chip_body(sel_ref, a_ref, b_ref, c_ref, d_ref, o_ref):
        o_ref[...] = a_ref[...] + b_ref[...].astype(F32) + c_ref[...].astype(F32) + d_ref[...].astype(F32)

    def from_chip(j):
        return pl.BlockSpec((None, tile, C), lambda i, sel_ref: (j, i, 0))

    g3 = got3.reshape(3, rows, C)
    out = pl.pallas_call(
        chip_body, name="rs_chip_sum_" + tag,
        grid_spec=pltpu.PrefetchScalarGridSpec(
            num_scalar_prefetch=1, grid=(rows // tile,),
            in_specs=[pl.BlockSpec((None, tile, C), lambda i, sel_ref: (sel_ref[1], i, 0)),
                      from_chip(0), from_chip(1), from_chip(2)],
            out_specs=pl.BlockSpec((tile, C), lambda i, sel_ref: (i, 0))),
        out_shape=jax.ShapeDtypeStruct((rows, C), F32),
        compiler_params=_cp(("arbitrary",)),
    )(sel, h, g3, g3, g3)
    return out.reshape(shard)


def _sum8(a):
    out, = _ew(lambda *v: functools.reduce(lambda p, q: p + q, v), [a[k] for k in range(N_DEV)], 1, [F32],
               name="sum8")
    return out


def _pad_rows(flat, cols, mult):
    n = flat.shape[0]
    per = cols * mult
    tot = ((n + per - 1) // per) * per
    return jnp.pad(flat, (0, tot - n)).reshape(-1, cols)


class _Packer:
    def __init__(self, shapes, mult):
        self.shapes = shapes
        self.sizes = [int(np.prod(s)) for s in shapes]
        self.rows = [8 * ((n + 8 * D - 1) // (8 * D)) for n in self.sizes]
        tot = sum(self.rows)
        self.tail = (-tot) % mult

    def pack(self, arrs, dtype):
        parts = [_pad_rows(a.reshape(-1).astype(dtype), D, 8) for a in arrs]
        if self.tail:
            parts.append(jnp.zeros((self.tail, D), dtype))
        return jnp.concatenate(parts, axis=0)

    def unpack(self, buf):
        out, o = [], 0
        for s, n, r in zip(self.shapes, self.sizes, self.rows):
            out.append(buf[o:o + r].reshape(-1)[:n].reshape(s))
            o += r
        return out


def _w_cat(w_in_l):
    sm = jnp.concatenate([w_in_l[:, O_DT:O_DT + SSD_H], w_in_l[:, O_F:O_F + FOX_H],
                          jnp.zeros((D, CH - SSD_H - FOX_H), w_in_l.dtype)], axis=1)
    return jnp.concatenate([w_in_l[:, O_Z:O_XBC], w_in_l[:, O_XBC:O_DT], w_in_l[:, O_QKV:O_F],
                            w_in_l[:, O_U:O_G], w_in_l[:, O_G:D_IN], sm], axis=1)


def _w_uncat(g):
    return jnp.concatenate([g[:, OFF_Z:OFF_XBC], g[:, OFF_XBC:OFF_QKV], g[:, OFF_SM:OFF_SM + SSD_H],
                            g[:, OFF_QKV:OFF_U], g[:, OFF_SM + SSD_H:OFF_SM + SSD_H + FOX_H],
                            g[:, OFF_U:OFF_G], g[:, OFF_G:OFF_SM]], axis=1)


def _layer_fwd(x, p, geom, dims):
    B, NC, Lp, pad = dims
    T = geom[0]
    rm = functools.partial(_rowmap, geom=geom)
    sv = {}
    xn1, = rm(_f_norm, [x], [p["norm1"]], [(D, BF16)], [], tile=384 if Lp % 384 == 0 else CH, name="norm1")
    Wc = p["w_cat"]
    pz = _mm(xn1, Wc, "nn", BF16, n=D, b_off=OFF_Z, name="in_z")
    pxbc = _mm(xn1, Wc, "nn", F32, n=CONV_DIM, b_off=OFF_XBC, name="in_xbc")
    qkv = _mm(xn1, Wc, "nn", BF16, n=3 * D, b_off=OFF_QKV, name="in_qkv")
    pu = _mm(xn1, Wc, "nn", BF16, n=D, b_off=OFF_U, name="in_u")
    pg = _mm(xn1, Wc, "nn", BF16, n=3 * D, b_off=OFF_G, name="in_g")
    psm = _mm(xn1, Wc, "nn", F32, n=CH, b_off=OFF_SM, name="in_sm")
    t_r = 384 if Lp % 384 == 0 else CH
    sm, = rm(_smallact, [psm], [p["smallbias"]], [(CH, F32)], [], tile=t_r, name="smallact")
    xbc = _conv_fwd(pxbc, p["conv_w"], p["conv_b"], geom=geom, tile=t_r)
    y_ssd, states = _ssd_fwd(xbc, sm, p["a_log"], B=B, NC=NC)
    y_a, = rm(_ssd_post, [y_ssd, (xbc, D, 0), pz], [p["d_rep"], p["ssd_norm"]], [(D, BF16)], [], tile=t_r,
              name="ssd_post")
    cum = _cumsum_seq(sm, B=B, NC=NC, reverse=False, name="fox_cum")
    cumT = _fox_keybias(cum, B=B, Lp=Lp, pad=pad)
    y_b, lse = _fox_fwd(qkv, cumT, B=B, Lp=Lp)
    y_ssm, hst = _s5_fwd(pu, p["Bsg"], p["Csg"], p["lam"], B=B, NC=NC)
    y1, = rm(_s5_pre, [y_ssm, pu], [p["s5_d"]], [(D, BF16)], [], tile=t_r, name="s5_pre")
    tg = _mm(y1, p["w_glu"], "nn", BF16, name="s5_glu_mm")
    y_c, = rm(_s5_glu, [y_ssm, pu, tg], [p["s5_d"]], [(D, BF16)], [], tile=t_r, name="s5_glu")
    br = [_mm(yy, p["w_branch"][n], "nn", BF16, name=f"branch{n}") for n, yy in enumerate((y_a, y_b, y_c))]
    mix, = rm(_merge, [(pg, D, 0), (pg, D, 1), (pg, D, 2)] + br, [], [(D, BF16)], [], tile=t_r, name="merge")
    x_mid = _mm(mix, p["w_out"], "nn", F32, res=x, name="out_proj")
    xn2, = rm(_f_norm, [x_mid], [p["norm2"]], [(D, BF16)], [], tile=t_r, name="norm2")
    hff = _mm(xn2, p["w_ffn_in"], "nn", BF16, name="ffn_in")
    act, = rm(_swiglu, [(hff, DFF, 0), (hff, DFF, 1)], [], [(DFF, BF16)], [], tile=t_r, name="swiglu")
    x_out = _mm(act, p["w_ffn_out"], "nn", F32, res=x_mid, name="ffn_out")
    sv.update(x=x, xn1=xn1, pz=pz, pxbc=pxbc, qkv=qkv, pu=pu, pg=pg, psm=psm, sm=sm, xbc=xbc, y_ssd=y_ssd,
              states=states, y_a=y_a, cum=cum, cumT=cumT, y_b=y_b, lse=lse, y_ssm=y_ssm, hst=hst, y1=y1, tg=tg,
              y_c=y_c, br=br, mix=mix, x_mid=x_mid, xn2=xn2, hff=hff, act=act)
    return x_out, sv


def _layer_bwd(dx_out, p, sv, geom, dims):
    B, NC, Lp, pad = dims
    T = geom[0]
    rm = functools.partial(_rowmap, geom=geom)
    t_r = 384 if Lp % 384 == 0 else CH
    g = {}
    dact = _mm(dx_out, p["w_ffn_out"], "nt", BF16, name="ffn_out_dx")
    g["w_ffn_out"] = _mm(sv["act"], dx_out, "tn", F32, name="ffn_out_dw")
    dhff, = rm(_b_swiglu, [(sv["hff"], DFF, 0), (sv["hff"], DFF, 1), dact], [], [(2 * DFF, BF16)], [], tile=t_r,
               name="swiglu_bwd")
    dxn2 = _mm(dhff, p["w_ffn_in"], "nt", F32, name="ffn_in_dx")
    g["w_ffn_in"] = _mm(sv["xn2"], dhff, "tn", F32, name="ffn_in_dw")
    dx_mid, g["norm2"] = rm(_b_norm, [sv["x_mid"], dxn2, dx_out], [p["norm2"]], [(D, F32)], [(1, D)], tile=t_r,
                            name="norm2_bwd")
    dmix = _mm(dx_mid, p["w_out"], "nt", BF16, name="out_proj_dx")
    g["w_out"] = _mm(sv["mix"], dx_mid, "tn", F32, name="out_proj_dw")
    pg = sv["pg"]
    dpg, db0, db1, db2 = rm(_b_merge, [(pg, D, 0), (pg, D, 1), (pg, D, 2)] + sv["br"] + [dmix], [],
                            [(3 * D, BF16), (D, BF16), (D, BF16), (D, BF16)], [], tile=t_r, name="merge_bwd")
    ys = (sv["y_a"], sv["y_b"], sv["y_c"])
    dbs = (db0, db1, db2)
    g["w_branch"] = [_mm(ys[n], dbs[n], "tn", F32, name=f"branch{n}_dw") for n in range(3)]
    dy = [_mm(dbs[n], p["w_branch"][n], "nt", BF16, name=f"branch{n}_dx") for n in range(3)]
    dtg, dy1a = rm(_b_s5_glu, [sv["y_ssm"], sv["pu"], sv["tg"], dy[2]], [p["s5_d"]], [(D, BF16), (D, F32)], [],
                   tile=t_r, name="s5_glu_bwd")
    dy1b = _mm(dtg, p["w_glu"], "nt", BF16, name="s5_glu_mm_dx")
    g["w_glu"] = _mm(sv["y1"], dtg, "tn", F32, name="s5_glu_mm_dw")
    dys, du_skip, g["s5_d"] = rm(_b_s5_pre, [sv["y_ssm"], sv["pu"], dy1a, dy1b], [p["s5_d"]],
                                 [(D, F32), (D, F32)], [(1, D)], tile=t_r, name="s5_pre_bwd")
    du, g["Bsg"], g["Csg"], g["lam"] = _s5_bwd(sv["pu"], sv["hst"], dys, du_skip, p["Bsg"], p["Csg"], p["lam"],
                                               B=B, NC=NC)
    dq, dk, dv, dckT, dcq = _fox_bwd(sv["qkv"], dy[1], sv["y_b"], sv["lse"], sv["cumT"], B=B, Lp=Lp)
    dcum8 = dcq.reshape(B, FOX_H, Lp).transpose(0, 2, 1) + dckT.reshape(B, FOX_H, Lp).transpose(0, 2, 1)
    dcum = jnp.pad(dcum8.reshape(T, FOX_H), ((0, 0), (SSD_H, CH - SSD_H - FOX_H)))
    dlogf = _cumsum_seq(dcum, B=B, NC=NC, reverse=True, name="fox_cum_bwd")
    dy_ssd, dxs_skip, dz, g["d_rep"], g["ssd_norm"] = rm(
        _b_ssd_post, [sv["y_ssd"], (sv["xbc"], D, 0), sv["pz"], dy[0]], [p["d_rep"], p["ssd_norm"]],
        [(D, F32), (D, F32), (D, BF16)], [(1, D), (1, D)], tile=t_r, name="ssd_post_bwd")
    dxbc_act, ddt, g["a_log"] = _ssd_bwd(sv["xbc"], sv["sm"], p["a_log"], sv["states"], dy_ssd, dxs_skip, B=B, NC=NC)
    dpsm, g["smallbias"] = rm(_b_smallact, [sv["psm"], ddt, dlogf], [p["smallbias"]], [(CH, BF16)], [(1, CH)],
                              tile=t_r, name="smallact_bwd")
    dconv, g["conv_w"], g["conv_b"] = _conv_bwd_pre(sv["pxbc"], dxbc_act, p["conv_w"], p["conv_b"], geom=geom, tile=t_r)
    dpxbc = _conv_bwd_x(dconv, p["conv_w"], geom=geom, tile=t_r)
    dproj = jnp.concatenate([dz, dpxbc, dq.astype(BF16), dk, dv, du, dpg, dpsm], axis=1)
    dxn1 = _mm(dproj, p["w_cat"], "nt", F32, name="in_dx")
    g["w_cat"] = _mm(sv["xn1"], dproj, "tn", F32, name="in_dw")
    dx_in, g["norm1"] = rm(_b_norm, [sv["x"], dxn1, dx_mid], [p["norm1"]], [(D, F32)], [(1, D)], tile=t_r,
                           name="norm1_bwd")
    return dx_in, g


_BIG = ["w_in", "s5_w_glu", "w_branch", "w_out", "w_ffn_in", "w_ffn_out"]
_NAMES = ['meta', 'norm1', 'w_in', 'ssd_conv_w', 'ssd_conv_b', 'ssd_dt_bias', 'ssd_a_log', 'ssd_d', 'ssd_norm',
          'fox_bf', 's5_lam_re', 's5_lam_im', 's5_b_re', 's5_b_im', 's5_c_re', 's5_c_im', 's5_log_step', 's5_d',
          's5_w_glu', 'w_branch', 'w_out', 'norm2', 'w_ffn_in', 'w_ffn_out', 'norm_f']
_SHARD_AXIS = {"meta": 1, "ssd_conv_w": 2}
_BIG_AXIS = {"w_in": 2, "s5_w_glu": 1, "w_branch": 2, "w_out": 1, "w_ffn_in": 2, "w_ffn_out": 1}


def kernel(x, meta, norm1, w_in, ssd_conv_w, ssd_conv_b, ssd_dt_bias, ssd_a_log, ssd_d, ssd_norm, fox_bf, s5_lam_re, s5_lam_im, s5_b_re, s5_b_im, s5_c_re, s5_c_im, s5_log_step, s5_d, s5_w_glu, w_branch, w_out, norm2, w_ffn_in, w_ffn_out, norm_f, loss_target, m_meta, m_norm1, m_w_in, m_ssd_conv_w, m_ssd_conv_b, m_ssd_dt_bias, m_ssd_a_log, m_ssd_d, m_ssd_norm, m_fox_bf, m_s5_lam_re, m_s5_lam_im, m_s5_b_re, m_s5_b_im, m_s5_c_re, m_s5_c_im, m_s5_log_step, m_s5_d, m_s5_w_glu, m_w_branch, m_w_out, m_norm2, m_w_ffn_in, m_w_ffn_out, m_norm_f, v_meta, v_norm1, v_w_in, v_ssd_conv_w, v_ssd_conv_b, v_ssd_dt_bias, v_ssd_a_log, v_ssd_d, v_ssd_norm, v_fox_bf, v_s5_lam_re, v_s5_lam_im, v_s5_b_re, v_s5_b_im, v_s5_c_re, v_s5_c_im, v_s5_log_step, v_s5_d, v_s5_w_glu, v_w_branch, v_w_out, v_norm2, v_w_ffn_in, v_w_ffn_out, v_norm_f):
    args = locals()
    W = {n: args[n] for n in _NAMES}
    Mo = {n: args["m_" + n] for n in _NAMES}
    Vo = {n: args["v_" + n] for n in _NAMES}
    B, S, _ = x.shape
    depth = norm1.shape[0]
    L = S + N_META
    Lp = ((L + CH - 1) // CH) * CH
    pad = Lp - L
    assert pad + N_META == CH and S % CH == 0
    NC = Lp // CH
    T = B * Lp
    geom = (T, Lp, pad)
    dims = (B, NC, Lp, pad)
    xi, yi, ci = _me()
    dev = 4 * xi + 2 * yi + ci

    gath = {n: _all_gather(W[n].astype(BF16), "gather_" + n) for n in _BIG}
    full = {n: jnp.concatenate([gath[n][k] for k in range(N_DEV)], axis=_BIG_AXIS[n]) for n in _BIG}
    sm_pack = _Packer([meta.shape, ssd_conv_w.shape], 8)
    sm_g = _all_gather(sm_pack.pack([meta, ssd_conv_w], F32), "gather_small")
    sm_parts = [sm_pack.unpack(sm_g[k]) for k in range(N_DEV)]
    meta_full = jnp.concatenate([sm_parts[k][0] for k in range(N_DEV)], axis=1)
    conv_w_full = jnp.concatenate([sm_parts[k][1] for k in range(N_DEV)], axis=2)

    layers = []
    s5_in = []
    for l in range(depth):
        lre = _s5_tile(s5_lam_re[l])
        lim = _s5_tile(s5_lam_im[l])
        lst = _s5_tile(jnp.repeat(s5_log_step[l], S5_P))
        bre = _s5_tile_b(s5_b_re[l])
        bim = _s5_tile_b(s5_b_im[l])
        s5_in.append((lre, lim, lst, bre, bim))
        a, b, br_, bi_ = _s5_params(lre, lim, lst, bre, bim)
        Bsg, Csg = _s5_blockdiag(_s5_untile_b(br_), _s5_untile_b(bi_), s5_c_re[l], s5_c_im[l])
        NS = S5_G // S5_SG
        lam = jnp.concatenate([a.reshape(NS, 1, S5_W), b.reshape(NS, 1, S5_W)], axis=2)
        zpad = jnp.zeros((CH - SSD_H - FOX_H,), F32)
        layers.append(dict(
            norm1=norm1[l][None], w_cat=_w_cat(full["w_in"][l]),
            smallbias=jnp.concatenate([ssd_dt_bias[l], fox_bf[l], zpad])[None],
            conv_w=conv_w_full[l], conv_b=ssd_conv_b[l][None],
            a_log=jnp.concatenate([ssd_a_log[l], jnp.zeros((CH - SSD_H,), F32)])[None],
            d_rep=jnp.repeat(ssd_d[l], SSD_P)[None], ssd_norm=ssd_norm[l][None],
            Bsg=Bsg, Csg=Csg, lam=lam, s5_d=s5_d[l][None], w_glu=full["s5_w_glu"][l],
            w_branch=[full["w_branch"][l, n] for n in range(3)], w_out=full["w_out"][l],
            norm2=norm2[l][None], w_ffn_in=full["w_ffn_in"][l], w_ffn_out=full["w_ffn_out"][l]))

    xs = jnp.concatenate([jnp.zeros((B, pad, D), F32), jnp.broadcast_to(meta_full[None], (B, N_META, D)), x], axis=1)
    h = xs.reshape(T, D)
    saved = []
    for l in range(depth):
        h, sv = _layer_fwd(h, layers[l], geom, dims)
        saved.append(sv)
    dh, loss_row, g_nf = _loss_head(h, norm_f[None], loss_target.reshape(B * S, D), B=B, NC=NC, S=S)
    loss = lax.psum(loss_row[0, 0], AXES)

    G = {n: [None] * depth for n in _NAMES}
    for l in reversed(range(depth)):
        dh, g = _layer_bwd(dh, layers[l], saved[l], geom, dims)
        saved[l] = None
        G["norm1"][l] = g["norm1"][0]
        G["norm2"][l] = g["norm2"][0]
        G["w_in"][l] = _w_uncat(g["w_cat"])
        G["ssd_conv_w"][l] = g["conv_w"]
        G["ssd_conv_b"][l] = g["conv_b"][0]
        G["ssd_dt_bias"][l] = g["smallbias"][0, 0:SSD_H]
        G["fox_bf"][l] = g["smallbias"][0, SSD_H:SSD_H + FOX_H]
        G["ssd_a_log"][l] = g["a_log"][0, 0:SSD_H]
        G["ssd_d"][l] = g["d_rep"].reshape(SSD_H, SSD_P).sum(axis=1)
        G["ssd_norm"][l] = g["ssd_norm"][0]
        dbr, dbi, dcr, dci = _s5_unblock(g["Bsg"], g["Csg"])
        da = _s5_tile(g["lam"][:, 0, 0:S5_W])
        db = _s5_tile(g["lam"][:, 0, S5_W:])
        dlre, dlim, dlst, dbre, dbim = _s5_params_bwd(*s5_in[l], da, db, _s5_tile_b(dbr), _s5_tile_b(dbi))
        G["s5_lam_re"][l] = dlre.reshape(S5_G, S5_P)
        G["s5_lam_im"][l] = dlim.reshape(S5_G, S5_P)
        G["s5_log_step"][l] = dlst.reshape(S5_G, S5_P).sum(axis=1)
        G["s5_b_re"][l] = _s5_untile_b(dbre)
        G["s5_b_im"][l] = _s5_untile_b(dbim)
        G["s5_c_re"][l] = dcr
        G["s5_c_im"][l] = dci
        G["s5_d"][l] = g["s5_d"][0]
        G["s5_w_glu"][l] = g["w_glu"]
        G["w_branch"][l] = jnp.stack(g["w_branch"])
        G["w_out"][l] = g["w_out"]
        G["w_ffn_in"][l] = g["w_ffn_in"]
        G["w_ffn_out"][l] = g["w_ffn_out"]
    dxs = dh.reshape(B, Lp, D)
    grad_x = dxs[:, pad + N_META:, :]
    part = {n: jnp.stack(G[n]) for n in _NAMES if n not in ("meta", "norm_f")}
    part["meta"] = dxs[:, pad:pad + N_META, :].sum(axis=0)
    part["norm_f"] = g_nf[0]

    grads = {}
    for n in _BIG:
        ax = _BIG_AXIS[n]
        a = part[n]
        a = a.reshape(a.shape[:ax] + (N_DEV, a.shape[ax] // N_DEV) + a.shape[ax + 1:])
        grads[n] = _reduce_scatter(jnp.moveaxis(a, ax, 0), n)

    small = [n for n in _NAMES if n not in _BIG]
    sp = _Packer([part[n].shape for n in small], 128)
    tot = sp.unpack(_sum8(_all_gather(sp.pack([part[n] for n in small], F32), "gather_small_grads")))
    for n, t in zip(small, tot):
        if n in _SHARD_AXIS:
            ax = _SHARD_AXIS[n]
            w = W[n].shape[ax]
            t = lax.dynamic_slice_in_dim(t, dev * w, w, axis=ax)
        grads[n] = t

    delta, new_m, new_v = {}, {}, {}
    for n in _BIG:
        delta[n], new_m[n], new_v[n] = _adam(W[n], grads[n], Mo[n], Vo[n], "adam_" + n)
    ap = _Packer([W[n].shape for n in small], 128)
    d_, m_, v_ = _adam(ap.pack([W[n] for n in small], F32), ap.pack([grads[n] for n in small], F32),
                       ap.pack([Mo[n] for n in small], F32), ap.pack([Vo[n] for n in small], F32), "adam_small")
    for n, a, b, c in zip(small, ap.unpack(d_), ap.unpack(m_), ap.unpack(v_)):
        delta[n], new_m[n], new_v[n] = a, b, c
    return (loss, grad_x, *[grads[n] for n in _NAMES], *[delta[n] for n in _NAMES],
            *[new_m[n] for n in _NAMES], *[new_v[n] for n in _NAMES])
```
